```python
import jax
import jax.numpy as jnp
from jax import lax
import numpy as np

D_MODEL = 2048
BATCH = 8
SEQ = 4096
DEPTH = 1

CHUNK = 64
SUB_CHUNK = 16
CONV_WIDTH = 4
GDN_HEAD_DIM = 128
GDN_HEADS = D_MODEL // (2 * GDN_HEAD_DIM)
GDN_WIDTH = GDN_HEADS * GDN_HEAD_DIM
HGRN_HEAD_DIM = 128
HGRN_VALUE_DIM = 128
HGRN_HEADS = D_MODEL // (2 * HGRN_VALUE_DIM)
HGRN_WIDTH = HGRN_HEADS * HGRN_HEAD_DIM
HGRN_V_WIDTH = HGRN_HEADS * HGRN_VALUE_DIM
D_MIX = GDN_WIDTH + HGRN_V_WIDTH
IN_PROJ_WIDTH = 4 * GDN_WIDTH + 2 * GDN_HEADS + 2 * HGRN_WIDTH + 2 * HGRN_V_WIDTH
D_FF = 4 * D_MODEL
NORM_EPS = 1e-6
L2_EPS = 1e-6

kernel_name = 'hybrid_gdn_hgrn2_block'


def rms_norm(x, w):
    xf = x.astype(jnp.float32)
    y = xf * lax.rsqrt(jnp.mean(xf * xf, axis=-1, keepdims=True) + NORM_EPS)
    return (y * w.astype(jnp.float32)).astype(x.dtype)


def head_rms_norm(o, w):
    return o * lax.rsqrt(jnp.mean(o * o, axis=-1, keepdims=True) + NORM_EPS) * w.astype(jnp.float32)


def l2_normalize(t):
    return t * lax.rsqrt(jnp.sum(t * t, axis=-1, keepdims=True) + L2_EPS)


def to_heads(t, n_heads):
    b, s, _ = t.shape
    return t.reshape(b, s, n_heads, -1).transpose(0, 2, 1, 3).astype(jnp.float32)


def from_heads(t):
    b, h, s, d = t.shape
    return t.transpose(0, 2, 1, 3).reshape(b, s, h * d)


def causal_depthwise_conv(t, w):
    return lax.conv_general_dilated(
        t, w[:, None, :].astype(t.dtype), window_strides=(1,),
        padding=[(w.shape[0] - 1, 0)], dimension_numbers=('NWC', 'WIO', 'NWC'),
        feature_group_count=t.shape[-1])


def gated_delta_rule_chunked(q, k, v, beta, g):
    b_, h_, t_, dk = q.shape
    dv = v.shape[-1]
    n_chunks = t_ // CHUNK
    q, k, v, beta, g = (t.reshape(b_, h_, n_chunks, CHUNK, *t.shape[3:]) for t in (q, k, v, beta, g))
    G = jnp.cumsum(g, axis=-1)
    pos = jnp.arange(CHUNK)
    incl = pos[:, None] >= pos[None, :]
    strict = pos[:, None] > pos[None, :]
    decay = jnp.exp(jnp.where(incl, G[..., :, None] - G[..., None, :], -jnp.inf))
    kk = jnp.einsum('bhncd,bhnsd->bhncs', k, k)
    unit_lower = jnp.where(strict, beta[..., :, None] * kk * decay, 0.0) + jnp.eye(CHUNK, dtype=jnp.float32)
    rhs = beta[..., None] * jnp.concatenate([v, jnp.exp(G)[..., None] * k], axis=-1)
    sol = lax.linalg.triangular_solve(unit_lower, rhs, left_side=True, lower=True, unit_diagonal=True)
    u_v, w = sol[..., :dv], sol[..., dv:]
    attn = jnp.einsum('bhncd,bhnsd->bhncs', q, k) * decay
    q_g = q * jnp.exp(G)[..., None]
    g_last = G[..., -1:]
    k_end = k * jnp.exp(g_last - G)[..., None]
    state_decay = jnp.exp(G[..., -1])

    def step(S, xs):
        u_v_c, w_c, q_c, attn_c, k_c, sd_c = xs
        u = u_v_c - jnp.einsum('bhcd,bhde->bhce', w_c, S)
        o = jnp.einsum('bhcd,bhde->bhce', q_c, S) + jnp.einsum('bhcs,bhse->bhce', attn_c, u)
        S = S * sd_c[..., None, None] + jnp.einsum('bhcd,bhce->bhde', k_c, u)
        return S, o

    xs = tuple(jnp.moveaxis(t, 2, 0) for t in (u_v, w, q_g, attn, k_end, state_decay))
    s0 = jnp.zeros((b_, h_, dk, dv), jnp.float32)
    _, o = lax.scan(step, s0, xs)
    return jnp.moveaxis(o, 0, 2).reshape(b_, h_, t_, dv)


def hgrn2_chunked(q, k, v, log_f):
    b_, h_, t_, dk = q.shape
    dv = v.shape[-1]
    n_chunks = t_ // CHUNK
    n_sub = CHUNK // SUB_CHUNK
    b_cum = jnp.cumsum(log_f.reshape(b_, h_, n_chunks, CHUNK, dk), axis=3)

    def blocks(t):
        return jnp.moveaxis(t.reshape(b_, h_, n_chunks, n_sub, SUB_CHUNK, t.shape[-1]), 2, 0)

    sub = jnp.arange(SUB_CHUNK)
    diag_mask = (sub[:, None] >= sub[None, :])[:, :, None]
    blk = jnp.arange(n_sub)
    off_mask = (blk[:, None] > blk[None, :])[:, :, None]

    def step(S, xs):
        qc, kc, vc, bc = xs
        b_end = bc[..., -1, :]
        b_start = jnp.concatenate([jnp.zeros_like(b_end[..., :1, :]), b_end[..., :-1, :]], axis=-2)
        o_inter = jnp.einsum('bhsid,bhde->bhsie', qc * jnp.exp(bc), S)
        d_diag = jnp.exp(jnp.where(diag_mask, bc[..., :, None, :] - bc[..., None, :, :], -jnp.inf))
        a_diag = jnp.einsum('bhsid,bhsijd,bhsjd->bhsij', qc, d_diag, kc)
        q_rel = qc * jnp.exp(bc - b_start[..., None, :])
        k_rel = kc * jnp.exp(b_end[..., None, :] - bc)
        d_off = jnp.exp(jnp.where(off_mask, b_start[..., :, None, :] - b_end[..., None, :, :], -jnp.inf))
        a_off = jnp.einsum('bhxid,bhxyd,bhyjd->bhxyij', q_rel, d_off, k_rel)
        o = (o_inter + jnp.einsum('bhsij,bhsje->bhsie', a_diag, vc)
             + jnp.einsum('bhxyij,bhyje->bhxie', a_off, vc))
        b_last = b_end[..., -1, :]
        k_state = kc * jnp.exp(b_last[:, :, None, None, :] - bc)
        S = S * jnp.exp(b_last)[..., None] + jnp.einsum('bhsjd,bhsje->bhde', k_state, vc)
        return S, o

    xs = (blocks(q), blocks(k), blocks(v), blocks(b_cum))
    s0 = jnp.zeros((b_, h_, dk, dv), jnp.float32)
    _, o = lax.scan(step, s0, xs)
    return jnp.moveaxis(o, 0, 2).reshape(b_, h_, t_, dv)


def gated_deltanet_group(qkv, z, a, b, conv_w, a_log, dt_bias, norm_w):
    qkv = jax.nn.silu(causal_depthwise_conv(qkv, conv_w))
    q, k, v = jnp.split(qkv, 3, axis=-1)
    q = l2_normalize(to_heads(q, GDN_HEADS)) * (GDN_HEAD_DIM ** -0.5)
    k = l2_normalize(to_heads(k, GDN_HEADS))
    v = to_heads(v, GDN_HEADS)
    beta = jax.nn.sigmoid(b.astype(jnp.float32)).transpose(0, 2, 1)
    g = (-jnp.exp(a_log.astype(jnp.float32))
         * jax.nn.softplus(a.astype(jnp.float32) + dt_bias.astype(jnp.float32))).transpose(0, 2, 1)
    o = gated_delta_rule_chunked(q, k, v, beta, g)
    o = head_rms_norm(o, norm_w) * jax.nn.silu(to_heads(z, GDN_HEADS))
    return from_heads(o)


def hgrn2_group(q, f, i, g, lower_bound, norm_w):
    lb = lower_bound.reshape(HGRN_HEADS, 1, HGRN_HEAD_DIM)
    f_logit = to_heads(f, HGRN_HEADS)
    forget = lb + (1.0 - lb) * jax.nn.sigmoid(f_logit)
    key = (1.0 - lb) * jax.nn.sigmoid(-f_logit)
    o = hgrn2_chunked(jax.nn.silu(to_heads(q, HGRN_HEADS)), key, to_heads(i, HGRN_HEADS), jnp.log(forget))
    o = head_rms_norm(o, norm_w) * jax.nn.silu(to_heads(g, HGRN_HEADS))
    return from_heads(o)


def hybrid_token_mixer(n, w_in, conv_w, a_log, dt_bias, gdn_norm_w, lower_bound, hgrn_norm_w, w_out):
    proj = n @ w_in
    o1 = 3 * GDN_WIDTH
    o2 = o1 + GDN_WIDTH
    o3 = o2 + GDN_HEADS
    o4 = o3 + GDN_HEADS
    o5 = o4 + HGRN_WIDTH
    o6 = o5 + HGRN_WIDTH
    o7 = o6 + HGRN_V_WIDTH
    qkv_a, z_a, a_a, b_a, q_b, f_b, i_b, g_b = jnp.split(proj, [o1, o2, o3, o4, o5, o6, o7], axis=-1)
    y_a = gated_deltanet_group(qkv_a, z_a, a_a, b_a, conv_w, a_log, dt_bias, gdn_norm_w)
    y_b = hgrn2_group(q_b, f_b, i_b, g_b, lower_bound, hgrn_norm_w)
    y = jnp.concatenate([y_a, y_b], axis=-1).astype(n.dtype)
    return y @ w_out


def squared_relu_mlp(n, w1, w2):
    return jnp.square(jax.nn.relu(n @ w1)) @ w2


def _fwd_setup_inputs(seed: int = 0) -> dict:
    key = jax.random.key(seed)
    ks = jax.random.split(key, 16)
    f32 = jnp.float32
    x = jax.random.normal(ks[0], (BATCH, SEQ, D_MODEL), f32)
    w_in = jax.random.normal(ks[1], (DEPTH, D_MODEL, IN_PROJ_WIDTH), f32) * D_MODEL ** -0.5
    conv_w = jax.random.normal(ks[2], (DEPTH, CONV_WIDTH, 3 * GDN_WIDTH), f32) * CONV_WIDTH ** -0.5
    gdn_a_log = jnp.log(jax.random.uniform(ks[3], (DEPTH, GDN_HEADS), f32, 1.0, 16.0))
    dt = jnp.exp(jax.random.uniform(ks[4], (DEPTH, GDN_HEADS), f32, np.log(1e-3), np.log(1e-1)))
    gdn_dt_bias = dt + jnp.log(-jnp.expm1(-dt))
    gdn_norm_w = 1.0 + 0.02 * jax.random.normal(ks[5], (DEPTH, GDN_HEAD_DIM), f32)
    hgrn_lb_logits = 0.1 * jax.random.normal(ks[6], (DEPTH + 1, HGRN_WIDTH), f32)
    hgrn_norm_w = 1.0 + 0.02 * jax.random.normal(ks[7], (DEPTH, HGRN_VALUE_DIM), f32)
    w_out = jax.random.normal(ks[8], (DEPTH, D_MIX, D_MODEL), f32) * D_MIX ** -0.5
    norm_mix_w = 1.0 + 0.02 * jax.random.normal(ks[9], (DEPTH, D_MODEL), f32)
    norm_ffn_w = 1.0 + 0.02 * jax.random.normal(ks[10], (DEPTH, D_MODEL), f32)
    w_ff1 = jax.random.normal(ks[11], (DEPTH, D_MODEL, D_FF), f32) * D_MODEL ** -0.5
    w_ff2 = jax.random.normal(ks[12], (DEPTH, D_FF, D_MODEL), f32) * D_FF ** -0.5
    norm_final_w = 1.0 + 0.02 * jax.random.normal(ks[13], (D_MODEL,), f32)
    return {'x': x, 'w_in': w_in, 'conv_w': conv_w, 'gdn_a_log': gdn_a_log,
            'gdn_dt_bias': gdn_dt_bias, 'gdn_norm_w': gdn_norm_w, 'hgrn_lb_logits': hgrn_lb_logits,
            'hgrn_norm_w': hgrn_norm_w, 'w_out': w_out, 'norm_mix_w': norm_mix_w,
            'norm_ffn_w': norm_ffn_w, 'w_ff1': w_ff1, 'w_ff2': w_ff2, 'norm_final_w': norm_final_w}


def _fwd_reference(x, w_in, conv_w, gdn_a_log, gdn_dt_bias, gdn_norm_w, hgrn_lb_logits, hgrn_norm_w,
              w_out, norm_mix_w, norm_ffn_w, w_ff1, w_ff2, norm_final_w):
    lower_bounds = jnp.cumsum(jax.nn.softmax(hgrn_lb_logits.astype(jnp.float32), axis=0), axis=0)
    h = x
    for layer in range(DEPTH):
        n = rms_norm(h, norm_mix_w[layer])
        h = h + hybrid_token_mixer(n, w_in[layer], conv_w[layer], gdn_a_log[layer], gdn_dt_bias[layer],
                                   gdn_norm_w[layer], lower_bounds[layer], hgrn_norm_w[layer], w_out[layer])
        n = rms_norm(h, norm_ffn_w[layer])
        h = h + squared_relu_mlp(n, w_ff1[layer], w_ff2[layer])
    return rms_norm(h, norm_final_w)


import jax as _jax
import jax.numpy as _jnp

TWIN_FORMAT = 'train_step'
FWD_PARAMS = ['x', 'w_in', 'conv_w', 'gdn_a_log', 'gdn_dt_bias', 'gdn_norm_w', 'hgrn_lb_logits', 'hgrn_norm_w', 'w_out', 'norm_mix_w', 'norm_ffn_w', 'w_ff1', 'w_ff2', 'norm_final_w']
TWIN_WEIGHTS = ['w_in', 'conv_w', 'gdn_a_log', 'gdn_dt_bias', 'gdn_norm_w', 'hgrn_lb_logits', 'hgrn_norm_w', 'w_out', 'norm_mix_w', 'norm_ffn_w', 'w_ff1', 'w_ff2', 'norm_final_w']
TWIN_DIFF_INPUT = 'x'
TWIN_INPUTS = ['x', 'w_in', 'conv_w', 'gdn_a_log', 'gdn_dt_bias', 'gdn_norm_w', 'hgrn_lb_logits', 'hgrn_norm_w', 'w_out', 'norm_mix_w', 'norm_ffn_w', 'w_ff1', 'w_ff2', 'norm_final_w', 'loss_target', 'm_w_in', 'm_conv_w', 'm_gdn_a_log', 'm_gdn_dt_bias', 'm_gdn_norm_w', 'm_hgrn_lb_logits', 'm_hgrn_norm_w', 'm_w_out', 'm_norm_mix_w', 'm_norm_ffn_w', 'm_w_ff1', 'm_w_ff2', 'm_norm_final_w', 'v_w_in', 'v_conv_w', 'v_gdn_a_log', 'v_gdn_dt_bias', 'v_gdn_norm_w', 'v_hgrn_lb_logits', 'v_hgrn_norm_w', 'v_w_out', 'v_norm_mix_w', 'v_norm_ffn_w', 'v_w_ff1', 'v_w_ff2', 'v_norm_final_w']
TWIN_OUTPUTS = ['loss', 'grad_x', 'grad_w_in', 'grad_conv_w', 'grad_gdn_a_log', 'grad_gdn_dt_bias', 'grad_gdn_norm_w', 'grad_hgrn_lb_logits', 'grad_hgrn_norm_w', 'grad_w_out', 'grad_norm_mix_w', 'grad_norm_ffn_w', 'grad_w_ff1', 'grad_w_ff2', 'grad_norm_final_w', 'delta_w_in', 'delta_conv_w', 'delta_gdn_a_log', 'delta_gdn_dt_bias', 'delta_gdn_norm_w', 'delta_hgrn_lb_logits', 'delta_hgrn_norm_w', 'delta_w_out', 'delta_norm_mix_w', 'delta_norm_ffn_w', 'delta_w_ff1', 'delta_w_ff2', 'delta_norm_final_w', 'new_m_w_in', 'new_m_conv_w', 'new_m_gdn_a_log', 'new_m_gdn_dt_bias', 'new_m_gdn_norm_w', 'new_m_hgrn_lb_logits', 'new_m_hgrn_norm_w', 'new_m_w_out', 'new_m_norm_mix_w', 'new_m_norm_ffn_w', 'new_m_w_ff1', 'new_m_w_ff2', 'new_m_norm_final_w', 'new_v_w_in', 'new_v_conv_w', 'new_v_gdn_a_log', 'new_v_gdn_dt_bias', 'new_v_gdn_norm_w', 'new_v_hgrn_lb_logits', 'new_v_hgrn_norm_w', 'new_v_w_out', 'new_v_norm_mix_w', 'new_v_norm_ffn_w', 'new_v_w_ff1', 'new_v_w_ff2', 'new_v_norm_final_w']
TWIN_LEAF_KINDS = {'loss': 'loss', 'grad_x': 'grad_x', 'grad_w_in': 'grad_w', 'grad_conv_w': 'grad_w', 'grad_gdn_a_log': 'grad_w', 'grad_gdn_dt_bias': 'grad_w', 'grad_gdn_norm_w': 'grad_w', 'grad_hgrn_lb_logits': 'grad_w', 'grad_hgrn_norm_w': 'grad_w', 'grad_w_out': 'grad_w', 'grad_norm_mix_w': 'grad_w', 'grad_norm_ffn_w': 'grad_w', 'grad_w_ff1': 'grad_w', 'grad_w_ff2': 'grad_w', 'grad_norm_final_w': 'grad_w', 'delta_w_in': 'delta_w', 'delta_conv_w': 'delta_w', 'delta_gdn_a_log': 'delta_w', 'delta_gdn_dt_bias': 'delta_w', 'delta_gdn_norm_w': 'delta_w', 'delta_hgrn_lb_logits': 'delta_w', 'delta_hgrn_norm_w': 'delta_w', 'delta_w_out': 'delta_w', 'delta_norm_mix_w': 'delta_w', 'delta_norm_ffn_w': 'delta_w', 'delta_w_ff1': 'delta_w', 'delta_w_ff2': 'delta_w', 'delta_norm_final_w': 'delta_w', 'new_m_w_in': 'new_m', 'new_m_conv_w': 'new_m', 'new_m_gdn_a_log': 'new_m', 'new_m_gdn_dt_bias': 'new_m', 'new_m_gdn_norm_w': 'new_m', 'new_m_hgrn_lb_logits': 'new_m', 'new_m_hgrn_norm_w': 'new_m', 'new_m_w_out': 'new_m', 'new_m_norm_mix_w': 'new_m', 'new_m_norm_ffn_w': 'new_m', 'new_m_w_ff1': 'new_m', 'new_m_w_ff2': 'new_m', 'new_m_norm_final_w': 'new_m', 'new_v_w_in': 'new_v', 'new_v_conv_w': 'new_v', 'new_v_gdn_a_log': 'new_v', 'new_v_gdn_dt_bias': 'new_v', 'new_v_gdn_norm_w': 'new_v', 'new_v_hgrn_lb_logits': 'new_v', 'new_v_hgrn_norm_w': 'new_v', 'new_v_w_out': 'new_v', 'new_v_norm_mix_w': 'new_v', 'new_v_norm_ffn_w': 'new_v', 'new_v_w_ff1': 'new_v', 'new_v_w_ff2': 'new_v', 'new_v_norm_final_w': 'new_v'}


def _forward(args):
    return _fwd_reference(*[args[k] for k in FWD_PARAMS])


def _output_shape():
    def fwd():
        inp = _fwd_setup_inputs(0)
        return _fwd_reference(*[inp[k] for k in FWD_PARAMS])
    out = _jax.eval_shape(fwd)
    return out.shape, out.dtype

N_MICROBATCH = 1
ADAM_LR = 0.001
ADAM_B1 = 0.9
ADAM_B2 = 0.999
ADAM_EPS = 1e-08
ADAM_WD = 0.01
ADAM_STEP = 10
PER_EXAMPLE_BATCH_AXIS = {'x': 0, 'loss_target': 0}
SHARED_INPUTS = []
_WEIGHT_DTYPES = {'w_in': _jnp.float32, 'conv_w': _jnp.float32, 'gdn_a_log': _jnp.float32, 'gdn_dt_bias': _jnp.float32, 'gdn_norm_w': _jnp.float32, 'hgrn_lb_logits': _jnp.float32, 'hgrn_norm_w': _jnp.float32, 'w_out': _jnp.float32, 'norm_mix_w': _jnp.float32, 'norm_ffn_w': _jnp.float32, 'w_ff1': _jnp.float32, 'w_ff2': _jnp.float32, 'norm_final_w': _jnp.float32}
MOMENT_SCALE = {'w_in': 3.882982e-02, 'conv_w': 3.967635e-02, 'gdn_a_log': 2.369881e-01, 'gdn_dt_bias': 2.307685e-01, 'gdn_norm_w': 1.691541e-01, 'hgrn_lb_logits': 4.728447e-03, 'hgrn_norm_w': 1.367816e-01, 'w_out': 4.963762e-02, 'norm_mix_w': 7.912434e-02, 'norm_ffn_w': 7.323252e-02, 'w_ff1': 3.725719e-02, 'w_ff2': 7.767213e-02, 'norm_final_w': 1.614593e+01}


def _to_microbatches(a, axis):
    t = _jnp.moveaxis(a, axis, 0)
    t = t.reshape((N_MICROBATCH, t.shape[0] // N_MICROBATCH) + t.shape[1:])
    return _jnp.moveaxis(t, 1, axis + 1)


def setup_inputs(seed: int = 0) -> dict:
    inp = _fwd_setup_inputs(seed)
    key = _jax.random.fold_in(_jax.random.key(seed), 7919)
    shape, _ = _output_shape()
    out = dict(inp)
    out["loss_target"] = _jax.random.normal(_jax.random.fold_in(key, 0), shape, _jnp.float32)
    for i, name in enumerate(TWIN_WEIGHTS):
        w = inp[name].astype(_jnp.float32)
        if MOMENT_SCALE is None:
            s = _jnp.sqrt(_jnp.mean(_jnp.square(w)) + 1e-30)
        else:
            s = MOMENT_SCALE[name]
        km, kv = _jax.random.split(_jax.random.fold_in(key, i + 1))
        out[name] = w
        out["m_" + name] = s * _jax.random.normal(km, w.shape, _jnp.float32)
        out["v_" + name] = (s * s) * _jax.random.uniform(kv, w.shape, _jnp.float32, 0.5, 1.5)
    if N_MICROBATCH > 1:
        for name, axis in PER_EXAMPLE_BATCH_AXIS.items():
            out[name] = _to_microbatches(out[name], axis)
    return {'x': out['x'], 'w_in': out['w_in'], 'conv_w': out['conv_w'], 'gdn_a_log': out['gdn_a_log'], 'gdn_dt_bias': out['gdn_dt_bias'], 'gdn_norm_w': out['gdn_norm_w'], 'hgrn_lb_logits': out['hgrn_lb_logits'], 'hgrn_norm_w': out['hgrn_norm_w'], 'w_out': out['w_out'], 'norm_mix_w': out['norm_mix_w'], 'norm_ffn_w': out['norm_ffn_w'], 'w_ff1': out['w_ff1'], 'w_ff2': out['w_ff2'], 'norm_final_w': out['norm_final_w'], 'loss_target': out['loss_target'], 'm_w_in': out['m_w_in'], 'm_conv_w': out['m_conv_w'], 'm_gdn_a_log': out['m_gdn_a_log'], 'm_gdn_dt_bias': out['m_gdn_dt_bias'], 'm_gdn_norm_w': out['m_gdn_norm_w'], 'm_hgrn_lb_logits': out['m_hgrn_lb_logits'], 'm_hgrn_norm_w': out['m_hgrn_norm_w'], 'm_w_out': out['m_w_out'], 'm_norm_mix_w': out['m_norm_mix_w'], 'm_norm_ffn_w': out['m_norm_ffn_w'], 'm_w_ff1': out['m_w_ff1'], 'm_w_ff2': out['m_w_ff2'], 'm_norm_final_w': out['m_norm_final_w'], 'v_w_in': out['v_w_in'], 'v_conv_w': out['v_conv_w'], 'v_gdn_a_log': out['v_gdn_a_log'], 'v_gdn_dt_bias': out['v_gdn_dt_bias'], 'v_gdn_norm_w': out['v_gdn_norm_w'], 'v_hgrn_lb_logits': out['v_hgrn_lb_logits'], 'v_hgrn_norm_w': out['v_hgrn_norm_w'], 'v_w_out': out['v_w_out'], 'v_norm_mix_w': out['v_norm_mix_w'], 'v_norm_ffn_w': out['v_norm_ffn_w'], 'v_w_ff1': out['v_w_ff1'], 'v_w_ff2': out['v_w_ff2'], 'v_norm_final_w': out['v_norm_final_w']}


def _loss(weights, diff, rest, loss_target):
    with _jax.named_scope("forward"):
        args = {**rest, TWIN_DIFF_INPUT: diff, **{k: w.astype(_WEIGHT_DTYPES[k]) for k, w in weights.items()}}
        y = _forward(args)
    with _jax.named_scope("loss_head"):
        err = _jnp.square(y.astype(_jnp.float32) - loss_target)
        return 0.5 * _jnp.sum(_jnp.mean(err, axis=-1)) if err.ndim else 0.5 * err


def _adamw(w, g, m, v):
    m = ADAM_B1 * m + (1.0 - ADAM_B1) * g
    v = ADAM_B2 * v + (1.0 - ADAM_B2) * _jnp.square(g)
    m_hat = m / (1.0 - ADAM_B1 ** ADAM_STEP)
    v_hat = v / (1.0 - ADAM_B2 ** ADAM_STEP)
    delta = -ADAM_LR * (m_hat / (_jnp.sqrt(v_hat) + ADAM_EPS) + ADAM_WD * w)
    return delta, m, v


def reference(x, w_in, conv_w, gdn_a_log, gdn_dt_bias, gdn_norm_w, hgrn_lb_logits, hgrn_norm_w, w_out, norm_mix_w, norm_ffn_w, w_ff1, w_ff2, norm_final_w, loss_target, m_w_in, m_conv_w, m_gdn_a_log, m_gdn_dt_bias, m_gdn_norm_w, m_hgrn_lb_logits, m_hgrn_norm_w, m_w_out, m_norm_mix_w, m_norm_ffn_w, m_w_ff1, m_w_ff2, m_norm_final_w, v_w_in, v_conv_w, v_gdn_a_log, v_gdn_dt_bias, v_gdn_norm_w, v_hgrn_lb_logits, v_hgrn_norm_w, v_w_out, v_norm_mix_w, v_norm_ffn_w, v_w_ff1, v_w_ff2, v_norm_final_w):
    given = dict(x=x, w_in=w_in, conv_w=conv_w, gdn_a_log=gdn_a_log, gdn_dt_bias=gdn_dt_bias, gdn_norm_w=gdn_norm_w, hgrn_lb_logits=hgrn_lb_logits, hgrn_norm_w=hgrn_norm_w, w_out=w_out, norm_mix_w=norm_mix_w, norm_ffn_w=norm_ffn_w, w_ff1=w_ff1, w_ff2=w_ff2, norm_final_w=norm_final_w, loss_target=loss_target, m_w_in=m_w_in, m_conv_w=m_conv_w, m_gdn_a_log=m_gdn_a_log, m_gdn_dt_bias=m_gdn_dt_bias, m_gdn_norm_w=m_gdn_norm_w, m_hgrn_lb_logits=m_hgrn_lb_logits, m_hgrn_norm_w=m_hgrn_norm_w, m_w_out=m_w_out, m_norm_mix_w=m_norm_mix_w, m_norm_ffn_w=m_norm_ffn_w, m_w_ff1=m_w_ff1, m_w_ff2=m_w_ff2, m_norm_final_w=m_norm_final_w, v_w_in=v_w_in, v_conv_w=v_conv_w, v_gdn_a_log=v_gdn_a_log, v_gdn_dt_bias=v_gdn_dt_bias, v_gdn_norm_w=v_gdn_norm_w, v_hgrn_lb_logits=v_hgrn_lb_logits, v_hgrn_norm_w=v_hgrn_norm_w, v_w_out=v_w_out, v_norm_mix_w=v_norm_mix_w, v_norm_ffn_w=v_norm_ffn_w, v_w_ff1=v_w_ff1, v_w_ff2=v_w_ff2, v_norm_final_w=v_norm_final_w)
    weights = {n: given[n] for n in TWIN_WEIGHTS}
    shared = {n: given[n] for n in SHARED_INPUTS}
    per_example = {n: given[n] for n in ['x']}
    grad_fn = _jax.value_and_grad(_loss, argnums=(0, 1))

    def one_microbatch(ex, loss_target):
        ex = dict(ex)
        diff = ex.pop(TWIN_DIFF_INPUT)
        return grad_fn(weights, diff, {**shared, **ex}, loss_target)

    if N_MICROBATCH == 1:
        loss, (grad_w, grad_x) = one_microbatch(per_example, given["loss_target"])
    else:
        def body(carry, xs):
            loss_sum, grad_sum = carry
            l_k, (gw_k, gx_k) = one_microbatch(xs[0], xs[1])
            with _jax.named_scope("update"):
                return (loss_sum + l_k, _jax.tree.map(_jnp.add, grad_sum, gw_k)), gx_k

        init = (_jnp.zeros((), _jnp.float32), _jax.tree.map(_jnp.zeros_like, weights))
        (loss, grad_w), grad_x = _jax.lax.scan(body, init, (per_example, given["loss_target"]))
    with _jax.named_scope("update"):
        delta_w, new_m, new_v = {}, {}, {}
        for n in TWIN_WEIGHTS:
            delta_w[n], new_m[n], new_v[n] = _adamw(weights[n], grad_w[n], given["m_" + n], given["v_" + n])
    return (loss, grad_x, *[grad_w[n] for n in TWIN_WEIGHTS], *[delta_w[n] for n in TWIN_WEIGHTS],
            *[new_m[n] for n in TWIN_WEIGHTS], *[new_v[n] for n in TWIN_WEIGHTS])
```

```python
import functools

import jax
import jax.numpy as jnp
from jax import lax
from jax.experimental import pallas as pl
from jax.experimental.pallas import tpu as pltpu

F32 = jnp.float32
BF16 = jnp.bfloat16

HEAD_DIM = 128
CHUNK = 128
SUB = 16
EXP_CAP = 80.0
NORM_EPS = 1e-6
L2_EPS = 1e-6
CONV_W = 4
VMEM_LIMIT = 56 * 1024 * 1024

ADAM_LR, ADAM_B1, ADAM_B2, ADAM_EPS, ADAM_WD, ADAM_STEP = 1e-3, 0.9, 0.999, 1e-8, 0.01, 10

NN = ((1,), (0,))
NT = ((1,), (1,))
TN = ((0,), (0,))
MESH = pl.DeviceIdType.MESH


def _dot(a, b, dims):
    return lax.dot_general(a.astype(BF16), b.astype(BF16), (dims, ((), ())),
                           preferred_element_type=F32)


def _split(a):
    hi = a.astype(BF16)
    return hi, (a - hi.astype(F32)).astype(BF16)


def _dot3(a, b, dims):
    ah, al = _split(a)
    bh, bl = _split(b)
    d = lambda x, y: lax.dot_general(x, y, (dims, ((), ())), preferred_element_type=F32)
    return d(ah, bh) + (d(ah, bl) + d(al, bh))


def _sigmoid(x):
    return 1.0 / (1.0 + jnp.exp(-x))


def _silu(x):
    return x * _sigmoid(x)


def _dsilu(x):
    s = _sigmoid(x)
    return s * (1.0 + x * (1.0 - s))


def _softplus(x):
    e = jnp.exp(-jnp.abs(x))
    u = 1.0 + e
    log1p = jnp.where(u == 1.0, e, jnp.log(u) * (e / jnp.where(u == 1.0, 1.0, u - 1.0)))
    return jnp.maximum(x, 0.0) + log1p


def _iota(shape, axis):
    return lax.broadcasted_iota(jnp.int32, shape, axis)


def _cumsum_rows(x):
    n = x.shape[0]
    row = _iota(x.shape, 0)
    s = 1
    while s < n:
        x = x + jnp.where(row >= s, pltpu.roll(x, s, 0), 0.0)
        s *= 2
    return x


def _rev_cumsum_rows(x):
    return jnp.sum(x, axis=0, keepdims=True) - _cumsum_rows(x) + x


def _params(sem):
    return pltpu.CompilerParams(dimension_semantics=sem, vmem_limit_bytes=VMEM_LIMIT)


def _tile(n, want):
    t = min(n, want)
    while n % t:
        t //= 2
    return t


def _mm(a, b, mode, out_dtypes, name, epi=None, extras=(), tm=1024, tn=1024, tk=512):
    if mode == "tn":
        kdim, m = a.shape
    else:
        m, kdim = a.shape
    n = b.shape[0] if mode == "nt" else b.shape[1]
    tm, tn, tk = _tile(m, tm), _tile(n, tn), _tile(kdim, tk)
    nk = kdim // tk
    dims = {"nn": NN, "nt": NT, "tn": TN}[mode]
    a_spec = (pl.BlockSpec((tk, tm), lambda i, j, k: (k, i)) if mode == "tn"
              else pl.BlockSpec((tm, tk), lambda i, j, k: (i, k)))
    b_spec = (pl.BlockSpec((tn, tk), lambda i, j, k: (j, k)) if mode == "nt"
              else pl.BlockSpec((tk, tn), lambda i, j, k: (k, j)))
    mn_spec = pl.BlockSpec((tm, tn), lambda i, j, k: (i, j))
    ne, no = len(extras), len(out_dtypes)
    if epi is None:
        epi = lambda acc: (acc,)

    def body(a_ref, b_ref, *rest):
        extra_refs, out_refs, acc = rest[:ne], rest[ne:ne + no], rest[-1]
        k = pl.program_id(2)

        @pl.when(k == 0)
        def _():
            acc[...] = jnp.zeros_like(acc)

        acc[...] += _dot(a_ref[...], b_ref[...], dims)

        @pl.when(k == nk - 1)
        def _():
            outs = epi(acc[...], *[r[...] for r in extra_refs])
            for o_ref, o in zip(out_refs, outs):
                o_ref[...] = o.astype(o_ref.dtype)

    outs = pl.pallas_call(
        body, name=name,
        out_shape=tuple(jax.ShapeDtypeStruct((m, n), d) for d in out_dtypes),
        grid=(m // tm, n // tn, nk),
        in_specs=[a_spec, b_spec] + [mn_spec] * ne,
        out_specs=tuple(mn_spec for _ in out_dtypes),
        scratch_shapes=[pltpu.VMEM((tm, tn), F32)],
        compiler_params=_params(("parallel", "parallel", "arbitrary")),
    )(a, b, *extras)
    return outs if no > 1 else outs[0]


ROWS = 256


def _rms_fwd(x, w, name):
    t, d = x.shape
    tr = _tile(t, ROWS)

    def body(x_ref, w_ref, n_ref):
        xv = x_ref[...]
        r = lax.rsqrt(jnp.mean(xv * xv, axis=-1, keepdims=True) + NORM_EPS)
        n_ref[...] = (xv * r * w_ref[...]).astype(n_ref.dtype)

    return pl.pallas_call(
        body, name=name, out_shape=jax.ShapeDtypeStruct((t, d), BF16), grid=(t // tr,),
        in_specs=[pl.BlockSpec((tr, d), lambda i: (i, 0)), pl.BlockSpec((1, d), lambda i: (0, 0))],
        out_specs=pl.BlockSpec((tr, d), lambda i: (i, 0)),
        compiler_params=_params(("parallel",)),
    )(x, w.reshape(1, d))


def _rms_bwd(dn, x, w, dres, name):
    t, d = x.shape
    tr = _tile(t, ROWS)

    def body(dn_ref, x_ref, w_ref, dres_ref, dx_ref, dxb_ref, dw_ref):
        i = pl.program_id(0)
        xv, dnv = x_ref[...], dn_ref[...]
        r = lax.rsqrt(jnp.mean(xv * xv, axis=-1, keepdims=True) + NORM_EPS)
        xh = xv * r
        dxh = dnv * w_ref[...]
        dx = dres_ref[...] + r * (dxh - xh * jnp.mean(dxh * xh, axis=-1, keepdims=True))
        dx_ref[...] = dx
        dxb_ref[...] = dx.astype(BF16)

        @pl.when(i == 0)
        def _():
            dw_ref[...] = jnp.zeros_like(dw_ref)

        dw_ref[...] += jnp.sum(dnv * xh, axis=0, keepdims=True)

    row = pl.BlockSpec((tr, d), lambda i: (i, 0))
    vec = pl.BlockSpec((1, d), lambda i: (0, 0))
    return pl.pallas_call(
        body, name=name,
        out_shape=(jax.ShapeDtypeStruct((t, d), F32), jax.ShapeDtypeStruct((t, d), BF16),
                   jax.ShapeDtypeStruct((1, d), F32)),
        grid=(t // tr,), in_specs=[row, row, vec, row], out_specs=(row, row, vec),
        compiler_params=_params(("arbitrary",)),
    )(dn, x, w.reshape(1, d), dres)


def _loss_head(h, w, target):
    t, d = h.shape
    tr = _tile(t, ROWS)

    def body(h_ref, w_ref, t_ref, loss_ref, dh_ref, dhb_ref, dw_ref):
        i = pl.program_id(0)
        hv, wv = h_ref[...], w_ref[...]
        r = lax.rsqrt(jnp.mean(hv * hv, axis=-1, keepdims=True) + NORM_EPS)
        hh = hv * r
        err = hh * wv - t_ref[...]
        dout = err * (1.0 / d)
        dhh = dout * wv
        dh = r * (dhh - hh * jnp.mean(dhh * hh, axis=-1, keepdims=True))
        dh_ref[...] = dh
        dhb_ref[...] = dh.astype(BF16)

        @pl.when(i == 0)
        def _():
            dw_ref[...] = jnp.zeros_like(dw_ref)
            loss_ref[...] = jnp.zeros_like(loss_ref)

        dw_ref[...] += jnp.sum(dout * hh, axis=0, keepdims=True)
        loss_ref[...] += jnp.full((1, 128), 0.5 / d, F32) * jnp.sum(err * err)

    row = pl.BlockSpec((tr, d), lambda i: (i, 0))
    vec = pl.BlockSpec((1, d), lambda i: (0, 0))
    lspec = pl.BlockSpec((1, 128), lambda i: (0, 0))
    return pl.pallas_call(
        body, name="loss_head",
        out_shape=(jax.ShapeDtypeStruct((1, 128), F32), jax.ShapeDtypeStruct((t, d), F32),
                   jax.ShapeDtypeStruct((t, d), BF16), jax.ShapeDtypeStruct((1, d), F32)),
        grid=(t // tr,), in_specs=[row, vec, row], out_specs=(lspec, row, row, vec),
        compiler_params=_params(("arbitrary",)),
    )(h, w.reshape(1, d), target)


def _inv_unit_lower(a):
    c = a.shape[0]
    eye = (_iota((c, c), 0) == _iota((c, c), 1)).astype(F32)
    x = eye - a
    p = _dot3(a, a, NN)
    n = 2
    while n < c:
        x = x + _dot3(x, p, NN)
        n *= 2
        if n < c:
            p = _dot3(p, p, NN)
    return x


def _gdn_chunk(q, k, v, beta, g):
    c = q.shape[0]
    row, col = _iota((c, c), 0), _iota((c, c), 1)
    gc = _cumsum_rows(g)
    diff = gc - gc.T
    dec = jnp.where(row >= col, jnp.exp(jnp.minimum(diff, 0.0)), 0.0)
    dec_s = jnp.where(row > col, dec, 0.0)
    gam = jnp.exp(gc)
    g_last = jnp.sum(g, axis=0, keepdims=True)
    kk = _dot(k, k, NT)
    a = beta * kk * dec_s
    p = _dot(q, k, NT) * dec
    e_end = jnp.exp(g_last - gc)
    return dict(dec=dec, dec_s=dec_s, gam=gam, gam_last=jnp.exp(g_last), e_end=e_end,
                k_end=k * e_end, kk=kk, a=a, p=p)


def _gdn_fwd(q, k, v, beta_bc, g_bc):
    t = q.shape[0]
    h = q.shape[1] // HEAD_DIM
    nc = t // CHUNK

    def body(q_ref, k_ref, v_ref, b_ref, g_ref, o_ref, s_ref, t_ref, state):
        @pl.when(pl.program_id(1) == 0)
        def _():
            state[...] = jnp.zeros_like(state)

        qv, kv, vv, beta = q_ref[...], k_ref[...], v_ref[...], b_ref[...]
        ch = _gdn_chunk(qv, kv, vv, beta, g_ref[...])
        tm = _inv_unit_lower(ch["a"])
        sol = _dot(tm, jnp.concatenate([beta * vv, beta * ch["gam"] * kv], axis=1), NN)
        u_v, w = sol[:, :HEAD_DIM], sol[:, HEAD_DIM:]
        s0 = state[...]
        u = u_v - _dot(w, s0, NN)
        o_ref[...] = _dot(qv * ch["gam"], s0, NN) + _dot(ch["p"], u, NN)
        s_ref[...] = s0
        t_ref[...] = tm
        state[...] = ch["gam_last"] * s0 + _dot(ch["k_end"], u, TN)

    tok = pl.BlockSpec((CHUNK, HEAD_DIM), lambda hh, c: (c, hh))
    bc = pl.BlockSpec((None, CHUNK, HEAD_DIM), lambda hh, c: (hh, c, 0))
    mat = pl.BlockSpec((None, None, HEAD_DIM, HEAD_DIM), lambda hh, c: (hh, c, 0, 0))
    return pl.pallas_call(
        body, name="gdn_fwd",
        out_shape=(jax.ShapeDtypeStruct(q.shape, F32),
                   jax.ShapeDtypeStruct((h, nc, HEAD_DIM, HEAD_DIM), F32),
                   jax.ShapeDtypeStruct((h, nc, CHUNK, CHUNK), F32)),
        grid=(h, nc), in_specs=[tok, tok, tok, bc, bc], out_specs=(tok, mat, mat),
        scratch_shapes=[pltpu.VMEM((HEAD_DIM, HEAD_DIM), F32)],
        compiler_params=_params(("parallel", "arbitrary")),
    )(q, k, v, beta_bc, g_bc)


def _gdn_bwd(q, k, v, beta_bc, g_bc, states, invs, do, do_blk=0):
    t = q.shape[0]
    h = q.shape[1] // HEAD_DIM
    nc = t // CHUNK

    def body(q_ref, k_ref, v_ref, b_ref, g_ref, s_ref, t_ref, do_ref,
             dq_ref, dk_ref, dv_ref, db_ref, dg_ref, dstate):
        @pl.when(pl.program_id(1) == 0)
        def _():
            dstate[...] = jnp.zeros_like(dstate)

        qv, kv, vv, beta = q_ref[...], k_ref[...], v_ref[...], b_ref[...]
        dov, s0, tm, ds1 = do_ref[...], s_ref[...], t_ref[...], dstate[...]
        ch = _gdn_chunk(qv, kv, vv, beta, g_ref[...])
        gam, dec, dec_s, kk = ch["gam"], ch["dec"], ch["dec_s"], ch["kk"]
        r_v, r_w = beta * vv, beta * gam * kv
        sol = _dot(tm, jnp.concatenate([r_v, r_w], axis=1), NN)
        u_v, w = sol[:, :HEAD_DIM], sol[:, HEAD_DIM:]
        u = u_v - _dot(w, s0, NN)
        qg = qv * gam

        du = _dot(ch["p"], dov, TN) + _dot(ch["k_end"], ds1, NN)
        dp = _dot(dov, u, NT)
        dpd = dp * dec
        dqg = _dot(dov, s0, NT)
        dk_end = _dot(u, ds1, NT)
        dq = dqg * gam + _dot(dpd, kv, NN)
        dk = _dot(dpd, qv, TN) + dk_end * ch["e_end"]
        dstate[...] = _dot(qg, dov, TN) + ch["gam_last"] * ds1 - _dot(w, du, TN)
        dw = -_dot(du, s0, NT)
        dr = _dot(tm, jnp.concatenate([du, dw], axis=1), TN)
        dr_v, dr_w = dr[:, :HEAD_DIM], dr[:, HEAD_DIM:]
        da = -_dot(dr, sol, NT)
        dkk = da * beta * dec_s
        dk = dk + _dot(dkk, kv, NN) + _dot(dkk, kv, TN) + beta * gam * dr_w
        dbeta = (jnp.sum(da * kk * dec_s, axis=1, keepdims=True)
                 + jnp.sum(dr_v * vv + dr_w * gam * kv, axis=1, keepdims=True))

        pair = dp * ch["p"] + da * ch["a"]
        end = jnp.sum(dk_end * ch["k_end"], axis=1, keepdims=True)
        dgc = (jnp.sum(pair - pair.T, axis=1, keepdims=True)
               + jnp.sum(dqg * qg + dr_w * r_w, axis=1, keepdims=True) - end)
        at_end = jnp.sum(end) + ch["gam_last"] * jnp.sum(s0 * ds1)
        dgc = jnp.broadcast_to(dgc, (CHUNK, HEAD_DIM))
        dgc = dgc + jnp.where(_iota((CHUNK, HEAD_DIM), 0) == CHUNK - 1, at_end, 0.0)
        dq_ref[...] = dq
        dk_ref[...] = dk
        dv_ref[...] = beta * dr_v
        db_ref[...] = jnp.broadcast_to(dbeta, (CHUNK, HEAD_DIM))
        dg_ref[...] = _rev_cumsum_rows(dgc)

    rev = lambda c: nc - 1 - c
    tok = pl.BlockSpec((CHUNK, HEAD_DIM), lambda hh, c: (rev(c), hh))
    bc = pl.BlockSpec((None, CHUNK, HEAD_DIM), lambda hh, c: (hh, rev(c), 0))
    mat = pl.BlockSpec((None, None, HEAD_DIM, HEAD_DIM), lambda hh, c: (hh, rev(c), 0, 0))
    tok_shape = jax.ShapeDtypeStruct(q.shape, F32)
    bc_shape = jax.ShapeDtypeStruct((h, t, HEAD_DIM), F32)
    return pl.pallas_call(
        body, name="gdn_bwd",
        out_shape=(tok_shape, tok_shape, tok_shape, bc_shape, bc_shape),
        grid=(h, nc),
        in_specs=[tok, tok, tok, bc, bc, mat, mat,
                  pl.BlockSpec((CHUNK, HEAD_DIM), lambda hh, c: (rev(c), do_blk + hh))],
        out_specs=(tok, tok, tok, bc, bc),
        scratch_shapes=[pltpu.VMEM((HEAD_DIM, HEAD_DIM), F32)],
        compiler_params=_params(("parallel", "arbitrary")),
    )(q, k, v, beta_bc, g_bc, states, invs, do)


def _hgrn_chunk(q, k, lf):
    c = q.shape[0]
    row = _iota((c, HEAD_DIM), 0)
    b = _cumsum_rows(lf)
    q_subs, k_facs, a_rows = [], [], []
    for x in range(c // SUB):
        b_start = jnp.sum(jnp.where(row < x * SUB, lf, 0.0), axis=0, keepdims=True)
        q_x = (q * jnp.exp(jnp.minimum(b - b_start, 0.0)))[x * SUB:(x + 1) * SUB]
        k_fac = jnp.where(row < (x + 1) * SUB, jnp.exp(jnp.minimum(b_start - b, EXP_CAP)), 0.0)
        q_subs.append(q_x)
        k_facs.append(k_fac)
        a_rows.append(_dot(q_x, k * k_fac, NT))
    a = jnp.concatenate(a_rows, axis=0)
    a = jnp.where(_iota((c, c), 0) >= _iota((c, c), 1), a, 0.0)
    b_last = jnp.sum(lf, axis=0, keepdims=True)
    return dict(b=b, a=a, q_subs=q_subs, k_facs=k_facs, e_b=jnp.exp(b),
                e_end=jnp.exp(b_last - b), e_last=jnp.exp(b_last))


def _hgrn_fwd(q, k, v, lf, v_blk=0):
    t = q.shape[0]
    h = q.shape[1] // HEAD_DIM
    nc = t // CHUNK

    def body(q_ref, k_ref, v_ref, lf_ref, o_ref, s_ref, state):
        @pl.when(pl.program_id(1) == 0)
        def _():
            state[...] = jnp.zeros_like(state)

        qv, kv, vv = q_ref[...], k_ref[...], v_ref[...]
        ch = _hgrn_chunk(qv, kv, lf_ref[...])
        s0 = state[...]
        o_ref[...] = _dot(qv * ch["e_b"], s0, NT) + _dot(ch["a"], vv, NN)
        s_ref[...] = s0
        state[...] = s0 * ch["e_last"] + _dot(vv, kv * ch["e_end"], TN)

    tok = pl.BlockSpec((CHUNK, HEAD_DIM), lambda hh, c: (c, hh))
    mat = pl.BlockSpec((None, None, HEAD_DIM, HEAD_DIM), lambda hh, c: (hh, c, 0, 0))
    return pl.pallas_call(
        body, name="hgrn_fwd",
        out_shape=(jax.ShapeDtypeStruct(q.shape, F32),
                   jax.ShapeDtypeStruct((h, nc, HEAD_DIM, HEAD_DIM), F32)),
        grid=(h, nc),
        in_specs=[tok, tok, pl.BlockSpec((CHUNK, HEAD_DIM), lambda hh, c: (c, v_blk + hh)), tok],
        out_specs=(tok, mat),
        scratch_shapes=[pltpu.VMEM((HEAD_DIM, HEAD_DIM), F32)],
        compiler_params=_params(("parallel", "arbitrary")),
    )(q, k, v, lf)


def _hgrn_bwd(q, k, v, lf, states, do, v_blk=0, do_blk=0):
    t = q.shape[0]
    nc = t // CHUNK
    h = q.shape[1] // HEAD_DIM

    def body(q_ref, k_ref, v_ref, lf_ref, s_ref, do_ref, dq_ref, dk_ref, dv_ref, dlf_ref, dstate):
        @pl.when(pl.program_id(1) == 0)
        def _():
            dstate[...] = jnp.zeros_like(dstate)

        qv, kv, vv, dov, s0 = q_ref[...], k_ref[...], v_ref[...], do_ref[...], s_ref[...]
        ds1 = dstate[...]
        ch = _hgrn_chunk(qv, kv, lf_ref[...])
        c = CHUNK
        row = _iota((c, HEAD_DIM), 0)
        qh = qv * ch["e_b"]
        k_end = kv * ch["e_end"]
        da = jnp.where(_iota((c, c), 0) >= _iota((c, c), 1), _dot(dov, vv, NT), 0.0)
        dqh = _dot(dov, s0, NN)
        dk_end = _dot(vv, ds1, NN)
        end = dk_end * k_end
        dk = dk_end * ch["e_end"]
        db = dqh * qh - end + jnp.where(
            row == c - 1, jnp.sum(end + s0 * ch["e_last"] * ds1, axis=0, keepdims=True), 0.0)
        dq_rows, qdq_rows = [], []
        for x in range(c // SUB):
            da_x = da[x * SUB:(x + 1) * SUB]
            k_x = kv * ch["k_facs"][x]
            dq_x = _dot(da_x, k_x, NN)
            dk_x = _dot(da_x, ch["q_subs"][x], TN)
            dq_rows.append(dq_x)
            qdq_rows.append(dq_x * ch["q_subs"][x])
            dk = dk + dk_x * ch["k_facs"][x]
            kdk = dk_x * k_x
            db = db - kdk
            if x > 0:
                at_start = jnp.sum(kdk, axis=0, keepdims=True) - jnp.sum(qdq_rows[x], axis=0, keepdims=True)
                db = db + jnp.where(row == x * SUB - 1, at_start, 0.0)
        b_start = jnp.zeros((c, HEAD_DIM), F32)
        for x in range(1, c // SUB):
            b_x = jnp.sum(jnp.where(row < x * SUB, lf_ref[...], 0.0), axis=0, keepdims=True)
            b_start = jnp.where(row >= x * SUB, b_x, b_start)
        dq = dqh * ch["e_b"] + jnp.concatenate(dq_rows, axis=0) * jnp.exp(jnp.minimum(ch["b"] - b_start, 0.0))
        db = db + jnp.concatenate(qdq_rows, axis=0)
        dstate[...] = _dot(dov, qh, TN) + ds1 * ch["e_last"]
        dq_ref[...] = dq
        dk_ref[...] = dk
        dv_ref[...] = _dot(ch["a"], dov, TN) + _dot(k_end, ds1, NT)
        dlf_ref[...] = _rev_cumsum_rows(db)

    rev = lambda c: nc - 1 - c
    tok = pl.BlockSpec((CHUNK, HEAD_DIM), lambda hh, c: (rev(c), hh))
    mat = pl.BlockSpec((None, None, HEAD_DIM, HEAD_DIM), lambda hh, c: (hh, rev(c), 0, 0))
    tok_shape = jax.ShapeDtypeStruct(q.shape, F32)
    return pl.pallas_call(
        body, name="hgrn_bwd",
        out_shape=(tok_shape, tok_shape, tok_shape, tok_shape),
        grid=(h, nc),
        in_specs=[tok, tok, pl.BlockSpec((CHUNK, HEAD_DIM), lambda hh, c: (rev(c), v_blk + hh)), tok, mat,
                  pl.BlockSpec((CHUNK, HEAD_DIM), lambda hh, c: (rev(c), do_blk + hh))],
        out_specs=(tok, tok, tok, tok),
        scratch_shapes=[pltpu.VMEM((HEAD_DIM, HEAD_DIM), F32)],
        compiler_params=_params(("parallel", "arbitrary")),
    )(q, k, v, lf, states, do)


CONV_ROWS = 256
HALO = 8


def _shift_down(cur, prev, s):
    rt = cur.shape[0]
    head = jnp.concatenate([pltpu.roll(prev, s, 0), jnp.zeros((rt - HALO, cur.shape[1]), F32)], axis=0)
    return jnp.where(_iota(cur.shape, 0) < s, head, pltpu.roll(cur, s, 0))


def _shift_up(cur, nxt, s):
    rt = cur.shape[0]
    tail = jnp.concatenate([jnp.zeros((rt - HALO, cur.shape[1]), F32), pltpu.roll(nxt, HALO - s, 0)], axis=0)
    return jnp.where(_iota(cur.shape, 0) >= rt - s, tail, pltpu.roll(cur, rt - s, 0))


def _tile_with_prev(ref, i, rt):
    r0 = pl.multiple_of(i * rt, rt)
    cur = ref[pl.ds(r0, rt), :]
    prev = ref[pl.ds(pl.multiple_of(jnp.maximum(r0 - HALO, 0), HALO), HALO), :]
    return cur, jnp.where(i > 0, prev, 0.0)


def _tile_with_next(ref, i, rt, n_tiles):
    r0 = pl.multiple_of(i * rt, rt)
    cur = ref[pl.ds(r0, rt), :]
    nxt = ref[pl.ds(pl.multiple_of(jnp.minimum(r0 + rt, (n_tiles - 1) * rt), HALO), HALO), :]
    return cur, jnp.where(i < n_tiles - 1, nxt, 0.0)


def _conv_tile(x_ref, w_ref, i, rt):
    cur, prev = _tile_with_prev(x_ref, i, rt)
    shifted = [_shift_down(cur, prev, CONV_W - 1 - j) for j in range(CONV_W - 1)] + [cur]
    c = shifted[0] * w_ref[pl.ds(0, 1), :]
    for j in range(1, CONV_W):
        c = c + shifted[j] * w_ref[pl.ds(j, 1), :]
    return c, shifted


def _l2n(s):
    return s * lax.rsqrt(jnp.sum(s * s, axis=-1, keepdims=True) + L2_EPS)


def _gdn_prep_fwd(proj, conv_w, h):
    t = proj.shape[0]
    rt = _tile(t, CONV_ROWS)
    nt = t // rt
    scale = HEAD_DIM ** -0.5

    def body(xq, xk, xv, wq, wk, wv, q_ref, k_ref, v_ref):
        def tile(i, carry):
            rows = pl.ds(pl.multiple_of(i * rt, rt), rt)
            q_ref[rows, :] = _l2n(_silu(_conv_tile(xq, wq, i, rt)[0])) * scale
            k_ref[rows, :] = _l2n(_silu(_conv_tile(xk, wk, i, rt)[0]))
            v_ref[rows, :] = _silu(_conv_tile(xv, wv, i, rt)[0])
            return carry

        lax.fori_loop(0, nt, tile, 0)

    col = lambda p: pl.BlockSpec((t, HEAD_DIM), lambda hh: (0, p * h + hh))
    wcol = lambda p: pl.BlockSpec((CONV_W, HEAD_DIM), lambda hh: (0, p * h + hh))
    out = pl.BlockSpec((t, HEAD_DIM), lambda hh: (0, hh))
    shape = jax.ShapeDtypeStruct((t, h * HEAD_DIM), F32)
    return pl.pallas_call(
        body, name="gdn_prep_fwd", out_shape=(shape, shape, shape), grid=(h,),
        in_specs=[col(0), col(1), col(2), wcol(0), wcol(1), wcol(2)], out_specs=(out, out, out),
        compiler_params=_params(("parallel",)),
    )(proj, proj, proj, conv_w, conv_w, conv_w)


def _gdn_prep_bwd(proj, conv_w, dq, dk, dv, h):
    t = proj.shape[0]
    rt = _tile(t, CONV_ROWS)
    nt = t // rt
    scale = HEAD_DIM ** -0.5

    def part(x_ref, w_ref, dy_ref, dx_ref, dw_ref, dc_ref, norm_scale):
        def first(i, dws):
            rows = pl.ds(pl.multiple_of(i * rt, rt), rt)
            c, shifted = _conv_tile(x_ref, w_ref, i, rt)
            ds = dy_ref[rows, :]
            if norm_scale is not None:
                s = _silu(c)
                r = lax.rsqrt(jnp.sum(s * s, axis=-1, keepdims=True) + L2_EPS)
                y = s * r
                dyn = ds * norm_scale
                ds = r * (dyn - y * jnp.sum(dyn * y, axis=-1, keepdims=True))
            dc = ds * _dsilu(c)
            dc_ref[rows, :] = dc
            return tuple(dws[j] + jnp.sum(dc * shifted[j], axis=0, keepdims=True) for j in range(CONV_W))

        dws = lax.fori_loop(0, nt, first, tuple(jnp.zeros((1, HEAD_DIM), F32) for _ in range(CONV_W)))
        for j in range(CONV_W):
            dw_ref[pl.ds(j, 1), :] = dws[j]

        def second(i, carry):
            rows = pl.ds(pl.multiple_of(i * rt, rt), rt)
            cur, nxt = _tile_with_next(dc_ref, i, rt, nt)
            dx = cur * w_ref[pl.ds(CONV_W - 1, 1), :]
            for j in range(CONV_W - 1):
                dx = dx + _shift_up(cur, nxt, CONV_W - 1 - j) * w_ref[pl.ds(j, 1), :]
            dx_ref[rows, :] = dx.astype(dx_ref.dtype)
            return carry

        lax.fori_loop(0, nt, second, 0)

    def body(xq, xk, xv, wq, wk, wv, dq_ref, dk_ref, dv_ref, dxq, dxk, dxv, dwq, dwk, dwv, dc_ref):
        part(xq, wq, dq_ref, dxq, dwq, dc_ref, scale)
        part(xk, wk, dk_ref, dxk, dwk, dc_ref, 1.0)
        part(xv, wv, dv_ref, dxv, dwv, dc_ref, None)

    col = lambda p: pl.BlockSpec((t, HEAD_DIM), lambda hh: (0, p * h + hh))
    wcol = lambda p: pl.BlockSpec((CONV_W, HEAD_DIM), lambda hh: (0, p * h + hh))
    own = pl.BlockSpec((t, HEAD_DIM), lambda hh: (0, hh))
    wown = pl.BlockSpec((CONV_W, HEAD_DIM), lambda hh: (0, hh))
    dx_shape = jax.ShapeDtypeStruct((t, h * HEAD_DIM), BF16)
    dw_shape = jax.ShapeDtypeStruct((CONV_W, h * HEAD_DIM), F32)
    return pl.pallas_call(
        body, name="gdn_prep_bwd",
        out_shape=(dx_shape, dx_shape, dx_shape, dw_shape, dw_shape, dw_shape), grid=(h,),
        in_specs=[col(0), col(1), col(2), wcol(0), wcol(1), wcol(2), own, own, own],
        out_specs=(own, own, own, wown, wown, wown),
        scratch_shapes=[pltpu.VMEM((t, HEAD_DIM), F32)],
        compiler_params=_params(("parallel",)),
    )(proj, proj, proj, conv_w, conv_w, conv_w, dq, dk, dv)


def _gdn_gates_fwd(ab, a_log_row, dt_bias_row):
    t = ab.shape[0]
    tr = _tile(t, 512)

    def body(ab_ref, al_ref, dt_ref, g_ref, b_ref):
        g_ref[...] = -jnp.exp(al_ref[...]) * _softplus(ab_ref[:, :HEAD_DIM] + dt_ref[...])
        b_ref[...] = _sigmoid(ab_ref[:, HEAD_DIM:])

    row = pl.BlockSpec((tr, HEAD_DIM), lambda i: (i, 0))
    vec = pl.BlockSpec((1, HEAD_DIM), lambda i: (0, 0))
    shape = jax.ShapeDtypeStruct((t, HEAD_DIM), F32)
    return pl.pallas_call(
        body, name="gdn_gates_fwd", out_shape=(shape, shape), grid=(t // tr,),
        in_specs=[pl.BlockSpec((tr, 2 * HEAD_DIM), lambda i: (i, 0)), vec, vec], out_specs=(row, row),
        compiler_params=_params(("parallel",)),
    )(ab, a_log_row, dt_bias_row)


def _gdn_gates_bwd(ab, a_log_row, dt_bias_row, dg, dbeta):
    t = ab.shape[0]
    tr = _tile(t, 512)

    def body(ab_ref, al_ref, dt_ref, dg_ref, db_ref, dab_ref, dal_ref, ddt_ref):
        @pl.when(pl.program_id(0) == 0)
        def _():
            dal_ref[...] = jnp.zeros_like(dal_ref)
            ddt_ref[...] = jnp.zeros_like(ddt_ref)

        xa = ab_ref[:, :HEAD_DIM] + dt_ref[...]
        neg_a = -jnp.exp(al_ref[...])
        dgv = dg_ref[...]
        da = dgv * neg_a * _sigmoid(xa)
        beta = _sigmoid(ab_ref[:, HEAD_DIM:])
        dab_ref[:, :HEAD_DIM] = da.astype(BF16)
        dab_ref[:, HEAD_DIM:] = (db_ref[...] * beta * (1.0 - beta)).astype(BF16)
        dal_ref[...] += jnp.sum(dgv * neg_a * _softplus(xa), axis=0, keepdims=True)
        ddt_ref[...] += jnp.sum(da, axis=0, keepdims=True)

    row = pl.BlockSpec((tr, HEAD_DIM), lambda i: (i, 0))
    row2 = pl.BlockSpec((tr, 2 * HEAD_DIM), lambda i: (i, 0))
    vec = pl.BlockSpec((1, HEAD_DIM), lambda i: (0, 0))
    vshape = jax.ShapeDtypeStruct((1, HEAD_DIM), F32)
    return pl.pallas_call(
        body, name="gdn_gates_bwd",
        out_shape=(jax.ShapeDtypeStruct((t, 2 * HEAD_DIM), BF16), vshape, vshape), grid=(t // tr,),
        in_specs=[row2, vec, vec, row, row], out_specs=(row2, vec, vec),
        compiler_params=_params(("arbitrary",)),
    )(ab, a_log_row, dt_bias_row, dg, dbeta)


def _lower_bound(lb_ref):
    return _sigmoid(lb_ref[pl.ds(0, 1), :] - lb_ref[pl.ds(1, 1), :])


def _hgrn_prep_fwd(proj, lb_logits, h, q_blk, f_blk):
    t = proj.shape[0]
    tr = _tile(t, 512)

    def body(xq, xf, lb_ref, q_ref, k_ref, lf_ref):
        lb = _lower_bound(lb_ref)
        s = _sigmoid(xf[...])
        q_ref[...] = _silu(xq[...])
        k_ref[...] = (1.0 - lb) * (1.0 - s)
        lf_ref[...] = jnp.log(lb + (1.0 - lb) * s)

    col = lambda b0: pl.BlockSpec((tr, HEAD_DIM), lambda hh, i: (i, b0 + hh))
    own = pl.BlockSpec((tr, HEAD_DIM), lambda hh, i: (i, hh))
    shape = jax.ShapeDtypeStruct((t, h * HEAD_DIM), F32)
    return pl.pallas_call(
        body, name="hgrn_prep_fwd", out_shape=(shape, shape, shape), grid=(h, t // tr),
        in_specs=[col(q_blk), col(f_blk), pl.BlockSpec((2, HEAD_DIM), lambda hh, i: (0, hh))],
        out_specs=(own, own, own), compiler_params=_params(("parallel", "parallel")),
    )(proj, proj, lb_logits)


def _hgrn_prep_bwd(proj, lb_logits, dq, dk, dlf, h, q_blk, f_blk):
    t = proj.shape[0]
    tr = _tile(t, 512)

    def body(xq, xf, lb_ref, dq_ref, dk_ref, dlf_ref, dxq, dxf, dlb_ref):
        @pl.when(pl.program_id(1) == 0)
        def _():
            dlb_ref[...] = jnp.zeros_like(dlb_ref)

        lb = _lower_bound(lb_ref)
        s = _sigmoid(xf[...])
        e = dlf_ref[...] / (lb + (1.0 - lb) * s) - dk_ref[...]
        dxq[...] = (dq_ref[...] * _dsilu(xq[...])).astype(BF16)
        dxf[...] = (s * (1.0 - s) * (1.0 - lb) * e).astype(BF16)
        d0 = jnp.sum((1.0 - s) * e, axis=0, keepdims=True) * (lb * (1.0 - lb))
        dlb_ref[pl.ds(0, 1), :] += d0
        dlb_ref[pl.ds(1, 1), :] += -d0

    col = lambda b0: pl.BlockSpec((tr, HEAD_DIM), lambda hh, i: (i, b0 + hh))
    own = pl.BlockSpec((tr, HEAD_DIM), lambda hh, i: (i, hh))
    lbs = pl.BlockSpec((2, HEAD_DIM), lambda hh, i: (0, hh))
    shape = jax.ShapeDtypeStruct((t, h * HEAD_DIM), BF16)
    return pl.pallas_call(
        body, name="hgrn_prep_bwd",
        out_shape=(shape, shape, jax.ShapeDtypeStruct((2, h * HEAD_DIM), F32)), grid=(h, t // tr),
        in_specs=[col(q_blk), col(f_blk), lbs, own, own, own], out_specs=(own, own, lbs),
        compiler_params=_params(("parallel", "arbitrary")),
    )(proj, proj, lb_logits, dq, dk, dlf)


def _gate_specs(h, z_blk, g_blk, tr):
    o_a = pl.BlockSpec((tr, HEAD_DIM), lambda hh, i: (i, jnp.minimum(hh, h - 1)))
    o_b = pl.BlockSpec((tr, HEAD_DIM), lambda hh, i: (i, jnp.maximum(hh - h, 0)))
    gate = pl.BlockSpec((tr, HEAD_DIM), lambda hh, i: (i, jnp.where(hh < h, z_blk + hh, g_blk + hh - h)))
    w = pl.BlockSpec((None, 1, HEAD_DIM), lambda hh, i: (hh // h, 0, 0))
    cat = pl.BlockSpec((tr, HEAD_DIM), lambda hh, i: (i, hh))
    return o_a, o_b, gate, w, cat


def _gate_fwd(o_a, o_b, proj, norm_w, h, z_blk, g_blk):
    t = o_a.shape[0]
    tr = _tile(t, 512)

    def body(oa_ref, ob_ref, z_ref, w_ref, y_ref):
        o = jnp.where(pl.program_id(0) < h, oa_ref[...], ob_ref[...])
        r = lax.rsqrt(jnp.mean(o * o, axis=-1, keepdims=True) + NORM_EPS)
        y_ref[...] = (o * r * w_ref[...] * _silu(z_ref[...])).astype(y_ref.dtype)

    sa, sb, sg, sw, cat = _gate_specs(h, z_blk, g_blk, tr)
    return pl.pallas_call(
        body, name="gate_fwd", out_shape=jax.ShapeDtypeStruct((t, 2 * h * HEAD_DIM), BF16),
        grid=(2 * h, t // tr), in_specs=[sa, sb, sg, sw], out_specs=cat,
        compiler_params=_params(("parallel", "parallel")),
    )(o_a, o_b, proj, norm_w)


def _gate_bwd(o_a, o_b, proj, norm_w, dy, h, z_blk, g_blk):
    t = o_a.shape[0]
    tr = _tile(t, 512)

    def body(oa_ref, ob_ref, z_ref, w_ref, dy_ref, do_ref, dz_ref, dw_ref):
        hh = pl.program_id(0)

        @pl.when(jnp.logical_and(hh % h == 0, pl.program_id(1) == 0))
        def _():
            dw_ref[...] = jnp.zeros_like(dw_ref)

        o = jnp.where(hh < h, oa_ref[...], ob_ref[...])
        z, w, dyv = z_ref[...], w_ref[...], dy_ref[...]
        r = lax.rsqrt(jnp.mean(o * o, axis=-1, keepdims=True) + NORM_EPS)
        oh = o * r
        dz_ref[...] = (dyv * oh * w * _dsilu(z)).astype(dz_ref.dtype)
        dn = dyv * _silu(z)
        doh = dn * w
        do_ref[...] = r * (doh - oh * jnp.mean(doh * oh, axis=-1, keepdims=True))
        dw_ref[...] += jnp.sum(dn * oh, axis=0, keepdims=True)

    sa, sb, sg, sw, cat = _gate_specs(h, z_blk, g_blk, tr)
    width = 2 * h * HEAD_DIM
    return pl.pallas_call(
        body, name="gate_bwd",
        out_shape=(jax.ShapeDtypeStruct((t, width), F32), jax.ShapeDtypeStruct((t, width), BF16),
                   jax.ShapeDtypeStruct((2, 1, HEAD_DIM), F32)),
        grid=(2 * h, t // tr), in_specs=[sa, sb, sg, sw, cat], out_specs=(cat, cat, sw),
        compiler_params=_params(("arbitrary", "arbitrary")),
    )(o_a, o_b, proj, norm_w, dy)


def _lane_row(vec):
    return jnp.pad(vec.reshape(1, -1), ((0, 0), (0, HEAD_DIM - vec.shape[-1])))


def _add_epi(acc, res):
    return (acc + res,)


def _split_w_in(w_in, h):
    gw = h * HEAD_DIM
    main = jnp.concatenate([w_in[:, :4 * gw], w_in[:, 4 * gw + 2 * h:]], axis=1)
    pad = jnp.zeros((w_in.shape[0], HEAD_DIM - h), w_in.dtype)
    ab = jnp.concatenate([w_in[:, 4 * gw:4 * gw + h], pad, w_in[:, 4 * gw + h:4 * gw + 2 * h], pad], axis=1)
    return main, ab


def _merge_w_in(main, ab, h):
    gw = h * HEAD_DIM
    return jnp.concatenate([main[:, :4 * gw], ab[:, :h], ab[:, HEAD_DIM:HEAD_DIM + h], main[:, 4 * gw:]], axis=1)


def _local_step(x, target, w_main, w_ab, conv_w, a_log, dt_bias, gdn_norm_w, lb_logits, hgrn_norm_w,
                w_out, norm_mix_w, norm_ffn_w, w_ff1, w_ff2, norm_final_w):
    t, d = x.shape
    h = d // (2 * HEAD_DIM)
    gw = h * HEAD_DIM
    k_blk, v_blk, z_blk, qb_blk, fb_blk, ib_blk, gb_blk = (i * h for i in range(1, 8))
    del k_blk, v_blk

    n1 = _rms_fwd(x, norm_mix_w, "rms_mix")
    proj = _mm(n1, w_main, "nn", (F32,), "in_proj")
    ab = _mm(n1, w_ab, "nn", (F32,), "in_proj_ab")

    q, k, v = _gdn_prep_fwd(proj, conv_w, h)
    a_log_row, dt_row = _lane_row(a_log), _lane_row(dt_bias)
    g_tm, beta_tm = _gdn_gates_fwd(ab, a_log_row, dt_row)
    to_heads = lambda a: jnp.broadcast_to(a[:, :h].T[:, :, None], (h, t, HEAD_DIM))
    g_bc, beta_bc = to_heads(g_tm), to_heads(beta_tm)
    o_a, st_a, inv_a = _gdn_fwd(q, k, v, beta_bc, g_bc)

    qh, kh, lf = _hgrn_prep_fwd(proj, lb_logits, h, qb_blk, fb_blk)
    o_b, st_b = _hgrn_fwd(qh, kh, proj, lf, v_blk=ib_blk)

    gate_w = jnp.stack([gdn_norm_w.reshape(1, HEAD_DIM), hgrn_norm_w.reshape(1, HEAD_DIM)])
    y = _gate_fwd(o_a, o_b, proj, gate_w, h, z_blk, gb_blk)
    h1 = _mm(y, w_out, "nn", (F32,), "out_proj", epi=_add_epi, extras=(x,))
    n2 = _rms_fwd(h1, norm_ffn_w, "rms_ffn")
    act, r = _mm(n2, w_ff1, "nn", (F32, BF16), "ff1",
                 epi=lambda acc: (acc, jnp.square(jnp.maximum(acc, 0.0))))
    h2 = _mm(r, w_ff2, "nn", (F32,), "ff2", epi=_add_epi, extras=(h1,))
    loss, dh2, dh2_b, d_norm_final = _loss_head(h2, norm_final_w, target)

    da = _mm(dh2_b, w_ff2, "nt", (BF16,), "ff2_dx",
             epi=lambda acc, a: (acc * (2.0 * jnp.maximum(a, 0.0)),), extras=(act,))
    d_ff2 = _mm(r, dh2_b, "tn", (F32,), "ff2_dw")
    dn2 = _mm(da, w_ff1, "nt", (F32,), "ff1_dx")
    d_ff1 = _mm(n2, da, "tn", (F32,), "ff1_dw")
    dh1, dh1_b, d_norm_ffn = _rms_bwd(dn2, h1, norm_ffn_w, dh2, "rms_ffn_bwd")
    dy = _mm(dh1_b, w_out, "nt", (F32,), "out_proj_dx")
    d_out = _mm(y, dh1_b, "tn", (F32,), "out_proj_dw")

    do, dgate, d_gate_w = _gate_bwd(o_a, o_b, proj, gate_w, dy, h, z_blk, gb_blk)
    dq, dk, dv, dbeta_bc, dg_bc = _gdn_bwd(q, k, v, beta_bc, g_bc, st_a, inv_a, do, do_blk=0)
    dxq, dxk, dxv, dcq, dck, dcv = _gdn_prep_bwd(proj, conv_w, dq, dk, dv, h)
    from_heads = lambda a: jnp.pad(a[:, :, 0].T, ((0, 0), (0, HEAD_DIM - h)))
    dab, d_a_log, d_dt_bias = _gdn_gates_bwd(ab, a_log_row, dt_row, from_heads(dg_bc), from_heads(dbeta_bc))
    dqh, dkh, dvh, dlf = _hgrn_bwd(qh, kh, proj, lf, st_b, do, v_blk=ib_blk, do_blk=h)
    dxqb, dxfb, d_lb = _hgrn_prep_bwd(proj, lb_logits, dqh, dkh, dlf, h, qb_blk, fb_blk)

    dproj = jnp.concatenate([dxq, dxk, dxv, dgate[:, :gw], dxqb, dxfb, dvh.astype(BF16), dgate[:, gw:]], axis=1)
    dn1_ab = _mm(dab, w_ab, "nt", (F32,), "in_proj_ab_dx")
    dn1 = _mm(dproj, w_main, "nt", (F32,), "in_proj_dx", epi=_add_epi, extras=(dn1_ab,))
    d_main = _mm(n1, dproj, "tn", (F32,), "in_proj_dw")
    d_ab = _mm(n1, dab, "tn", (F32,), "in_proj_ab_dw")
    dx, _, d_norm_mix = _rms_bwd(dn1, x, norm_mix_w, dh1, "rms_mix_bwd")

    grads = dict(
        w_main=d_main, w_ab=d_ab, conv_w=jnp.concatenate([dcq, dck, dcv], axis=1),
        gdn_a_log=d_a_log[:, :h], gdn_dt_bias=d_dt_bias[:, :h], gdn_norm_w=d_gate_w[0],
        hgrn_lb_logits=d_lb, hgrn_norm_w=d_gate_w[1], w_out=d_out, norm_mix_w=d_norm_mix,
        norm_ffn_w=d_norm_ffn, w_ff1=d_ff1, w_ff2=d_ff2, norm_final_w=d_norm_final)
    return loss, dx, grads


N_CHIPS = 4
ANY = pl.BlockSpec(memory_space=pl.ANY)


def _place():
    x, y, c = lax.axis_index("x"), lax.axis_index("y"), lax.axis_index("c")
    chips = [(1 - x, y), (x, 1 - y), (1 - x, 1 - y)]
    return x, y, c, chips


def _remote(src, dst, send_sems, recv_sems, k, to):
    return pltpu.make_async_remote_copy(src_ref=src, dst_ref=dst, send_sem=send_sems.at[k],
                                        recv_sem=recv_sems.at[k], device_id=to, device_id_type=MESH)


def _gather_weights(big, small):
    nb, ns = len(big), len(small)
    n_sem = 6 * nb + 3 * ns

    def body(*refs):
        ins, outs = refs[:nb + ns], refs[nb + ns:2 * (nb + ns)]
        send_sems, recv_sems, local_sems = refs[2 * (nb + ns):]
        x, y, c, chips = _place()
        me, sibling = 2 * x + y, (x, y, 1 - c)
        local = [pltpu.make_async_copy(ins[a], outs[a].at[me], local_sems.at[a]) for a in range(nb + ns)]
        for cp in local:
            cp.start()

        def half(a, chip, hc):
            rh = big[a].shape[0] // 2
            return outs[a].at[2 * chip[0] + chip[1], pl.ds(hc * rh, rh), :]

        first, passed = [], []
        for a in range(nb):
            rh = big[a].shape[0] // 2
            for j, chip in enumerate(chips):
                first.append(_remote(ins[a].at[pl.ds(c * rh, rh), :], half(a, (x, y), c),
                                     send_sems, recv_sems, 6 * a + j, (*chip, c)))
        for s in range(ns):
            for j, chip in enumerate(chips):
                first.append(_remote(ins[nb + s], outs[nb + s].at[me], send_sems, recv_sems,
                                     6 * nb + 3 * s + j, (*chip, c)))
        for cp in first:
            cp.start()
        for a in range(nb):
            for j, chip in enumerate(chips):
                _remote(half(a, chip, c), half(a, chip, c), send_sems, recv_sems, 6 * a + j, (*chip, c)).wait_recv()
                fwd = _remote(half(a, chip, c), half(a, chip, c), send_sems, recv_sems, 6 * a + 3 + j, sibling)
                fwd.start()
                passed.append(fwd)
        for s in range(ns):
            for j, chip in enumerate(chips):
                dst = outs[nb + s].at[2 * chip[0] + chip[1]]
                _remote(dst, dst, send_sems, recv_sems, 6 * nb + 3 * s + j, (*chip, c)).wait_recv()
        for a in range(nb):
            for j, chip in enumerate(chips):
                _remote(half(a, chip, 1 - c), half(a, chip, 1 - c), send_sems, recv_sems,
                        6 * a + 3 + j, sibling).wait_recv()
        for cp in first + passed:
            cp.wait_send()
        for cp in local:
            cp.wait()

    arrays = list(big) + list(small)
    return pl.pallas_call(
        body, name="gather_weights",
        out_shape=tuple(jax.ShapeDtypeStruct((N_CHIPS,) + a.shape, a.dtype) for a in arrays),
        in_specs=[ANY] * len(arrays), out_specs=tuple(ANY for _ in arrays),
        scratch_shapes=[pltpu.SemaphoreType.DMA((n_sem,)), pltpu.SemaphoreType.DMA((n_sem,)),
                        pltpu.SemaphoreType.DMA((len(arrays),))],
    )(*arrays)


def _swap_halves(parts):
    n = len(parts)

    def body(*refs):
        ins, outs = refs[:n], refs[n:2 * n]
        send_sems, recv_sems = refs[2 * n:]
        x, y, c, _ = _place()
        copies = [_remote(ins[a].at[s, 1 - c], outs[a].at[s], send_sems, recv_sems, N_CHIPS * a + s, (x, y, 1 - c))
                  for a in range(n) for s in range(N_CHIPS)]
        for cp in copies:
            cp.start()
        for cp in copies:
            cp.wait()

    return pl.pallas_call(
        body, name="grad_swap_halves",
        out_shape=tuple(jax.ShapeDtypeStruct((N_CHIPS,) + p.shape[2:], p.dtype) for p in parts),
        in_specs=[ANY] * n, out_specs=tuple(ANY for _ in parts),
        scratch_shapes=[pltpu.SemaphoreType.DMA((N_CHIPS * n,)), pltpu.SemaphoreType.DMA((N_CHIPS * n,))],
    )(*parts)


def _scatter_to_owners(parts):
    n = len(parts)

    def body(*refs):
        ins, outs = refs[:n], refs[n:2 * n]
        send_sems, recv_sems = refs[2 * n:]
        x, y, c, chips = _place()
        copies = [_remote(ins[a].at[2 * chip[0] + chip[1]], outs[a].at[j], send_sems, recv_sems,
                          3 * a + j, (*chip, c))
                  for a in range(n) for j, chip in enumerate(chips)]
        for cp in copies:
            cp.start()
        for cp in copies:
            cp.wait()

    return pl.pallas_call(
        body, name="grad_scatter_to_owners",
        out_shape=tuple(jax.ShapeDtypeStruct((3,) + p.shape[1:], p.dtype) for p in parts),
        in_specs=[ANY] * n, out_specs=tuple(ANY for _ in parts),
        scratch_shapes=[pltpu.SemaphoreType.DMA((3 * n,)), pltpu.SemaphoreType.DMA((3 * n,))],
    )(*parts)


def _join_halves(halves):
    n = len(halves)

    def body(*refs):
        ins, outs = refs[:n], refs[n:2 * n]
        send_sems, recv_sems, local_sems = refs[2 * n:]
        x, y, c, _ = _place()
        local = [pltpu.make_async_copy(ins[a], outs[a].at[c], local_sems.at[a]) for a in range(n)]
        copies = [_remote(ins[a], outs[a].at[c], send_sems, recv_sems, a, (x, y, 1 - c)) for a in range(n)]
        for cp in local + copies:
            cp.start()
        for a in range(n):
            _remote(ins[a], outs[a].at[1 - c], send_sems, recv_sems, a, (x, y, 1 - c)).wait_recv()
        for cp in copies:
            cp.wait_send()
        for cp in local:
            cp.wait()

    return pl.pallas_call(
        body, name="grad_join_halves",
        out_shape=tuple(jax.ShapeDtypeStruct((2,) + p.shape, p.dtype) for p in halves),
        in_specs=[ANY] * n, out_specs=tuple(ANY for _ in halves),
        scratch_shapes=[pltpu.SemaphoreType.DMA((n,)), pltpu.SemaphoreType.DMA((n,)),
                        pltpu.SemaphoreType.DMA((n,))],
    )(*halves)


N_DEV = 8


def _all_reduce_small(vec):
    def body(v_ref, gathered, total, send_sems, recv_sems):
        x, y, c, _ = _place()
        me = 4 * x + 2 * y + c
        gathered[me] = v_ref[...]
        copies = []
        for k in range(1, N_DEV):
            px = 1 - x if k & 4 else x
            py = 1 - y if k & 2 else y
            pc = 1 - c if k & 1 else c
            copies.append(_remote(v_ref, gathered.at[me], send_sems, recv_sems, k - 1, (px, py, pc)))
        for cp in copies:
            cp.start()
        for k, cp in enumerate(copies):
            cp.wait_send()
        for k in range(1, N_DEV):
            px = 1 - x if k & 4 else x
            py = 1 - y if k & 2 else y
            pc = 1 - c if k & 1 else c
            src = gathered.at[4 * px + 2 * py + pc]
            _remote(src, src, send_sems, recv_sems, k - 1, (px, py, pc)).wait_recv()
        acc = gathered[0]
        for dev in range(1, N_DEV):
            acc = acc + gathered[dev]
        total[...] = acc

    vm = pl.BlockSpec(memory_space=pltpu.VMEM)
    return pl.pallas_call(
        body, name="all_reduce_small",
        out_shape=(jax.ShapeDtypeStruct((N_DEV,) + vec.shape, F32), jax.ShapeDtypeStruct(vec.shape, F32)),
        in_specs=[vm], out_specs=(vm, vm),
        scratch_shapes=[pltpu.SemaphoreType.DMA((N_DEV - 1,)), pltpu.SemaphoreType.DMA((N_DEV - 1,))],
    )(vec)[1]


def _chip_sum(part, recv, c):
    _, _, rh, cols = part.shape
    tr = _tile(rh, 256)

    def body(c_ref, p_ref, r_ref, s_ref, sb_ref):
        s = p_ref[...] + r_ref[...]
        s_ref[...] = s
        sb_ref[...] = s.astype(BF16)

    blk = pl.BlockSpec((None, tr, cols), lambda s, i, c_ref: (s, i, 0))
    return pl.pallas_call(
        body, name="grad_chip_sum",
        out_shape=(jax.ShapeDtypeStruct(recv.shape, F32), jax.ShapeDtypeStruct(recv.shape, BF16)),
        grid_spec=pltpu.PrefetchScalarGridSpec(
            num_scalar_prefetch=1, grid=(N_CHIPS, rh // tr),
            in_specs=[pl.BlockSpec((None, None, tr, cols), lambda s, i, c_ref: (s, c_ref[0], i, 0)), blk],
            out_specs=(blk, blk)),
        compiler_params=_params(("parallel", "parallel")),
    )(c, part, recv)


def _owner_sum(own, recv, me):
    _, rh, cols = own.shape
    tr = _tile(rh, 256)

    def body(me_ref, o_ref, r0, r1, r2, g_ref):
        g_ref[...] = ((o_ref[...] + r0[...].astype(F32)) + r1[...].astype(F32)) + r2[...].astype(F32)

    slot = lambda j: pl.BlockSpec((None, tr, cols), lambda i, me_ref: (j, i, 0))
    return pl.pallas_call(
        body, name="grad_owner_sum", out_shape=jax.ShapeDtypeStruct((rh, cols), F32),
        grid_spec=pltpu.PrefetchScalarGridSpec(
            num_scalar_prefetch=1, grid=(rh // tr,),
            in_specs=[pl.BlockSpec((None, tr, cols), lambda i, me_ref: (me_ref[0], i, 0)), slot(0), slot(1), slot(2)],
            out_specs=pl.BlockSpec((tr, cols), lambda i, me_ref: (i, 0))),
        compiler_params=_params(("parallel",)),
    )(me, own, recv, recv, recv)


def _adamw(w, g, m, v, name):
    rows, cols = w.shape
    tr = _tile(rows, 256) if rows % 8 == 0 else rows
    c1 = 1.0 / (1.0 - ADAM_B1 ** ADAM_STEP)
    c2 = 1.0 / (1.0 - ADAM_B2 ** ADAM_STEP)

    def body(w_ref, g_ref, m_ref, v_ref, d_ref, nm_ref, nv_ref):
        gv = g_ref[...]
        nm = ADAM_B1 * m_ref[...] + (1.0 - ADAM_B1) * gv
        nv = ADAM_B2 * v_ref[...] + (1.0 - ADAM_B2) * (gv * gv)
        d_ref[...] = -ADAM_LR * ((nm * c1) / (jnp.sqrt(nv * c2) + ADAM_EPS) + ADAM_WD * w_ref[...])
        nm_ref[...] = nm
        nv_ref[...] = nv

    blk = pl.BlockSpec((tr, cols), lambda i: (i, 0))
    shape = jax.ShapeDtypeStruct((rows, cols), F32)
    return pl.pallas_call(
        body, name=name, out_shape=(shape, shape, shape), grid=(rows // tr,),
        in_specs=[blk, blk, blk, blk], out_specs=(blk, blk, blk),
        compiler_params=_params(("parallel",)),
    )(w, g, m, v)


SMALL = ("gdn_a_log", "gdn_dt_bias", "gdn_norm_w", "hgrn_lb_logits", "hgrn_norm_w",
         "norm_mix_w", "norm_ffn_w", "norm_final_w")
BIG = ("w_in", "w_out", "w_ff1", "w_ff2")
ORDER = ("w_in", "conv_w", "gdn_a_log", "gdn_dt_bias", "gdn_norm_w", "hgrn_lb_logits", "hgrn_norm_w",
         "w_out", "norm_mix_w", "norm_ffn_w", "w_ff1", "w_ff2", "norm_final_w")


def _pack(pieces):
    flat = jnp.concatenate([p.reshape(-1).astype(F32) for p in pieces])
    rows = -(-flat.shape[0] // (8 * HEAD_DIM)) * 8
    return jnp.pad(flat, (0, rows * HEAD_DIM - flat.shape[0])).reshape(rows, HEAD_DIM)


def _unpack(packed, shapes):
    flat, out, at = packed.reshape(-1), [], 0
    for s in shapes:
        n = 1
        for dim in s:
            n *= dim
        out.append(flat[at:at + n].reshape(s))
        at += n
    return out


def kernel(x, w_in, conv_w, gdn_a_log, gdn_dt_bias, gdn_norm_w, hgrn_lb_logits, hgrn_norm_w, w_out, norm_mix_w, norm_ffn_w, w_ff1, w_ff2, norm_final_w, loss_target, m_w_in, m_conv_w, m_gdn_a_log, m_gdn_dt_bias, m_gdn_norm_w, m_hgrn_lb_logits, m_hgrn_norm_w, m_w_out, m_norm_mix_w, m_norm_ffn_w, m_w_ff1, m_w_ff2, m_norm_final_w, v_w_in, v_conv_w, v_gdn_a_log, v_gdn_dt_bias, v_gdn_norm_w, v_hgrn_lb_logits, v_hgrn_norm_w, v_w_out, v_norm_mix_w, v_norm_ffn_w, v_w_ff1, v_w_ff2, v_norm_final_w):
    w = dict(w_in=w_in, conv_w=conv_w, gdn_a_log=gdn_a_log, gdn_dt_bias=gdn_dt_bias, gdn_norm_w=gdn_norm_w,
             hgrn_lb_logits=hgrn_lb_logits, hgrn_norm_w=hgrn_norm_w, w_out=w_out, norm_mix_w=norm_mix_w,
             norm_ffn_w=norm_ffn_w, w_ff1=w_ff1, w_ff2=w_ff2, norm_final_w=norm_final_w)
    m = dict(w_in=m_w_in, conv_w=m_conv_w, gdn_a_log=m_gdn_a_log, gdn_dt_bias=m_gdn_dt_bias,
             gdn_norm_w=m_gdn_norm_w, hgrn_lb_logits=m_hgrn_lb_logits, hgrn_norm_w=m_hgrn_norm_w,
             w_out=m_w_out, norm_mix_w=m_norm_mix_w, norm_ffn_w=m_norm_ffn_w, w_ff1=m_w_ff1, w_ff2=m_w_ff2,
             norm_final_w=m_norm_final_w)
    v = dict(w_in=v_w_in, conv_w=v_conv_w, gdn_a_log=v_gdn_a_log, gdn_dt_bias=v_gdn_dt_bias,
             gdn_norm_w=v_gdn_norm_w, hgrn_lb_logits=v_hgrn_lb_logits, hgrn_norm_w=v_hgrn_norm_w,
             w_out=v_w_out, norm_mix_w=v_norm_mix_w, norm_ffn_w=v_norm_ffn_w, w_ff1=v_w_ff1, w_ff2=v_w_ff2,
             norm_final_w=v_norm_final_w)
    d = x.shape[-1]
    h = d // (2 * HEAD_DIM)
    my_c = lax.axis_index("c").astype(jnp.int32).reshape(1)
    my_chip = (2 * lax.axis_index("x") + lax.axis_index("y")).astype(jnp.int32)

    shards = [w[n][0].astype(BF16) for n in BIG]
    conv_shard = jnp.pad(conv_w[0], ((0, 8 - CONV_W), (0, 0)))
    f_in, f_out, f_ff1, f_ff2, f_conv = _gather_weights(shards, [conv_shard])
    cols = lambda st: st.transpose(1, 0, 2).reshape(st.shape[1], -1)
    w_main, w_ab = _split_w_in(cols(f_in), h)
    conv_full = cols(f_conv[:, :CONV_W])

    loss, dx, g = _local_step(
        x[0], loss_target[0], w_main, w_ab, conv_full, gdn_a_log[0], gdn_dt_bias[0], gdn_norm_w[0],
        hgrn_lb_logits, hgrn_norm_w[0], f_out.reshape(-1, d), norm_mix_w[0], norm_ffn_w[0],
        cols(f_ff1), f_ff2.reshape(-1, d), norm_final_w)

    def by_shard(name, full):
        if name in ("w_in", "w_ff1"):
            st = full.reshape(full.shape[0], N_CHIPS, -1).transpose(1, 0, 2)
        else:
            st = full.reshape(N_CHIPS, -1, full.shape[1])
        return st.reshape(N_CHIPS, 2, st.shape[1] // 2, st.shape[2])

    g["w_in"] = _merge_w_in(g["w_main"], g["w_ab"], h)
    parts = [by_shard(n, g[n]) for n in BIG]
    from_sibling = _swap_halves(parts)
    sums = [_chip_sum(p, r, my_c) for p, r in zip(parts, from_sibling)]
    from_chips = _scatter_to_owners([s[1] for s in sums])
    halves = [_owner_sum(s[0], r, my_chip.reshape(1)) for s, r in zip(sums, from_chips)]
    joined = _join_halves(halves)
    grads = {n: j.reshape(w[n].shape) for n, j in zip(BIG, joined)}

    small_shapes = [w[n].shape for n in SMALL] + [conv_full.shape, (1,)]
    total = _all_reduce_small(_pack([g[n] for n in SMALL] + [g["conv_w"], loss[0, :1]]))
    *small_grads, conv_grad, loss_sum = _unpack(total, small_shapes)
    for n, sg in zip(SMALL, small_grads):
        grads[n] = sg
    shard_cols = conv_w.shape[-1]
    grads["conv_w"] = lax.dynamic_slice_in_dim(conv_grad, my_chip * shard_cols, shard_cols, axis=1)[None]

    delta, new_m, new_v = {}, {}, {}
    for n in BIG:
        flat = lambda a: a.reshape(-1, a.shape[-1])
        out = _adamw(flat(w[n]), flat(grads[n]), flat(m[n]), flat(v[n]), "adamw_" + n)
        delta[n], new_m[n], new_v[n] = (o.reshape(w[n].shape) for o in out)
    packed_names = SMALL + ("conv_w",)
    packed = [_pack([t[n] for n in packed_names]) for t in (w, grads, m, v)]
    outs = _adamw(*packed, "adamw_small")
    shapes = [w[n].shape for n in packed_names]
    for res, o in zip((delta, new_m, new_v), outs):
        for n, a in zip(packed_names, _unpack(o, shapes)):
            res[n] = a

    return (loss_sum.reshape(()), dx[None], *[grads[n] for n in ORDER], *[delta[n] for n in ORDER],
            *[new_m[n] for n in ORDER], *[new_v[n] for n in ORDER])
```

```python
import functools

import jax
import jax.numpy as jnp
from jax import lax
from jax.experimental import pallas as pl
from jax.experimental.pallas import tpu as pltpu

F32 = jnp.float32
BF16 = jnp.bfloat16

HEAD_DIM = 128
CHUNK = 128
SUB = 16
EXP_CAP = 80.0
NORM_EPS = 1e-6
L2_EPS = 1e-6
CONV_W = 4
VMEM_LIMIT = 56 * 1024 * 1024

ADAM_LR, ADAM_B1, ADAM_B2, ADAM_EPS, ADAM_WD, ADAM_STEP = 1e-3, 0.9, 0.999, 1e-8, 0.01, 10

NN = ((1,), (0,))
NT = ((1,), (1,))
TN = ((0,), (0,))
MESH = pl.DeviceIdType.MESH


def _dot(a, b, dims):
    return lax.dot_general(a.astype(BF16), b.astype(BF16), (dims, ((), ())),
                           preferred_element_type=F32)


def _split(a):
    hi = a.astype(BF16)
    return hi, (a - hi.astype(F32)).astype(BF16)


def _dot3(a, b, dims):
    ah, al = _split(a)
    bh, bl = _split(b)
    d = lambda x, y: lax.dot_general(x, y, (dims, ((), ())), preferred_element_type=F32)
    return d(ah, bh) + (d(ah, bl) + d(al, bh))


def _sigmoid(x):
    return 1.0 / (1.0 + jnp.exp(-x))


def _silu(x):
    return x * _sigmoid(x)


def _dsilu(x):
    s = _sigmoid(x)
    return s * (1.0 + x * (1.0 - s))


def _softplus(x):
    e = jnp.exp(-jnp.abs(x))
    u = 1.0 + e
    log1p = jnp.where(u == 1.0, e, jnp.log(u) * (e / jnp.where(u == 1.0, 1.0, u - 1.0)))
    return jnp.maximum(x, 0.0) + log1p


def _iota(shape, axis):
    return lax.broadcasted_iota(jnp.int32, shape, axis)


def _cumsum_rows(x):
    n = x.shape[0]
    row = _iota(x.shape, 0)
    s = 1
    while s < n:
        x = x + jnp.where(row >= s, pltpu.roll(x, s, 0), 0.0)
        s *= 2
    return x


def _rev_cumsum_rows(x):
    return jnp.sum(x, axis=0, keepdims=True) - _cumsum_rows(x) + x


def _params(sem):
    return pltpu.CompilerParams(dimension_semantics=sem, vmem_limit_bytes=VMEM_LIMIT)


HEADS_PER_STEP = 4


def _hps(h):
    return min(HEADS_PER_STEP, h)


def _head_view(ref, hb):
    if len(ref.shape) == 2:
        return ref.at[:, pl.ds(hb * HEAD_DIM, HEAD_DIM)]
    return ref.at[hb]


class _Staged:
    def __init__(self, ref, load):
        self.ref = ref
        self.loaded = ref[...] if load else None
        self.written = None

    def __getitem__(self, idx):
        return self.loaded

    def __setitem__(self, idx, value):
        self.written = value


def _each_head(one_head, n_in):
    def body(*refs):
        @pl.when(pl.program_id(1) == 0)
        def _():
            refs[-1][...] = jnp.zeros_like(refs[-1])

        last = len(refs) - 1
        staged = [[_Staged(_head_view(r, hb), i < n_in or i == last) for i, r in enumerate(refs)]
                  for hb in range(refs[-1].shape[0])]
        running = [one_head(*per_head) for per_head in staged]
        while running:
            for gen in list(running):
                try:
                    next(gen)
                except StopIteration:
                    running.remove(gen)
        for per_head in staged:
            for s in per_head:
                if s.written is not None:
                    s.ref[...] = s.written
    return body


def _tile(n, want):
    t = min(n, want)
    while n % t:
        t //= 2
    return t


def _mm(a, b, mode, out_dtypes, name, epi=None, extras=(), tm=1024, tn=1024, tk=2048,
        b_stacked=False, out_stacked=False):
    if mode == "tn":
        kdim, m = a.shape
    else:
        m, kdim = a.shape
    if b_stacked:
        n = N_CHIPS * b.shape[2] if mode == "nn" else b.shape[1]
        kdim_b = b.shape[1] if mode == "nn" else N_CHIPS * b.shape[2]
        assert kdim_b == kdim
    else:
        n = b.shape[0] if mode == "nt" else b.shape[1]
    per_shard = (n if (mode == "nn" or out_stacked) else kdim) // N_CHIPS
    tm, tn, tk = _tile(m, tm), _tile(n, tn), _tile(kdim, tk)
    if (b_stacked and mode == "nn") or out_stacked:
        tn = _tile(per_shard, tn)
    if b_stacked and mode == "nt":
        tk = _tile(per_shard, tk)
    nk = kdim // tk
    dims = {"nn": NN, "nt": NT, "tn": TN}[mode]
    a_spec = (pl.BlockSpec((tk, tm), lambda i, j, k: (k, i)) if mode == "tn"
              else pl.BlockSpec((tm, tk), lambda i, j, k: (i, k)))
    if b_stacked and mode == "nn":
        per = per_shard // tn
        b_spec = pl.BlockSpec((None, tk, tn), lambda i, j, k: (j // per, k, j % per))
    elif b_stacked:
        per = per_shard // tk
        b_spec = pl.BlockSpec((None, tn, tk), lambda i, j, k: (k // per, j, k % per))
    else:
        b_spec = (pl.BlockSpec((tn, tk), lambda i, j, k: (j, k)) if mode == "nt"
                  else pl.BlockSpec((tk, tn), lambda i, j, k: (k, j)))
    mn_spec = pl.BlockSpec((tm, tn), lambda i, j, k: (i, j))
    if out_stacked:
        per_o = per_shard // tn
        out_spec = pl.BlockSpec((None, tm, tn), lambda i, j, k: (j // per_o, i, j % per_o))
        out_shape = (N_CHIPS, m, per_shard)
    else:
        out_spec, out_shape = mn_spec, (m, n)
    ne, no = len(extras), len(out_dtypes)
    if epi is None:
        epi = lambda acc: (acc,)

    def body(a_ref, b_ref, *rest):
        extra_refs, out_refs = rest[:ne], rest[ne:ne + no]
        part = _dot(a_ref[...], b_ref[...], dims)

        def finish(total):
            outs = epi(total, *[r[...] for r in extra_refs])
            for o_ref, o in zip(out_refs, outs):
                o_ref[...] = o.astype(o_ref.dtype)

        if nk == 1:
            finish(part)
            return
        acc = rest[-1]
        k = pl.program_id(2)

        @pl.when(k == 0)
        def _():
            acc[...] = part

        @pl.when(jnp.logical_and(k > 0, k < nk - 1))
        def _():
            acc[...] += part

        @pl.when(k == nk - 1)
        def _():
            finish(acc[...] + part)

    outs = pl.pallas_call(
        body, name=name,
        out_shape=tuple(jax.ShapeDtypeStruct(out_shape, d) for d in out_dtypes),
        grid=(m // tm, n // tn, nk),
        in_specs=[a_spec, b_spec] + [mn_spec] * ne,
        out_specs=tuple(out_spec for _ in out_dtypes),
        scratch_shapes=[pltpu.VMEM((tm, tn), F32)] if nk > 1 else [],
        compiler_params=_params(("parallel", "parallel", "arbitrary")),
    )(a, b, *extras)
    return outs if no > 1 else outs[0]


ROWS = 256


def _rms_fwd(x, w, name):
    t, d = x.shape
    tr = _tile(t, ROWS)

    def body(x_ref, w_ref, n_ref):
        xv = x_ref[...]
        r = lax.rsqrt(jnp.mean(xv * xv, axis=-1, keepdims=True) + NORM_EPS)
        n_ref[...] = (xv * r * w_ref[...]).astype(n_ref.dtype)

    return pl.pallas_call(
        body, name=name, out_shape=jax.ShapeDtypeStruct((t, d), BF16), grid=(t // tr,),
        in_specs=[pl.BlockSpec((tr, d), lambda i: (i, 0)), pl.BlockSpec((1, d), lambda i: (0, 0))],
        out_specs=pl.BlockSpec((tr, d), lambda i: (i, 0)),
        compiler_params=_params(("parallel",)),
    )(x, w.reshape(1, d))


def _rms_bwd(dn, x, w, dres, name):
    t, d = x.shape
    tr = _tile(t, ROWS)

    def body(dn_ref, x_ref, w_ref, dres_ref, dx_ref, dxb_ref, dw_ref):
        i = pl.program_id(0)
        xv, dnv = x_ref[...], dn_ref[...]
        r = lax.rsqrt(jnp.mean(xv * xv, axis=-1, keepdims=True) + NORM_EPS)
        xh = xv * r
        dxh = dnv * w_ref[...]
        dx = dres_ref[...] + r * (dxh - xh * jnp.mean(dxh * xh, axis=-1, keepdims=True))
        dx_ref[...] = dx
        dxb_ref[...] = dx.astype(BF16)

        @pl.when(i == 0)
        def _():
            dw_ref[...] = jnp.zeros_like(dw_ref)

        dw_ref[...] += jnp.sum(dnv * xh, axis=0, keepdims=True)

    row = pl.BlockSpec((tr, d), lambda i: (i, 0))
    vec = pl.BlockSpec((1, d), lambda i: (0, 0))
    return pl.pallas_call(
        body, name=name,
        out_shape=(jax.ShapeDtypeStruct((t, d), F32), jax.ShapeDtypeStruct((t, d), BF16),
                   jax.ShapeDtypeStruct((1, d), F32)),
        grid=(t // tr,), in_specs=[row, row, vec, row], out_specs=(row, row, vec),
        compiler_params=_params(("arbitrary",)),
    )(dn, x, w.reshape(1, d), dres)


def _loss_head(h, w, target):
    t, d = h.shape
    tr = _tile(t, ROWS)

    def body(h_ref, w_ref, t_ref, loss_ref, dh_ref, dhb_ref, dw_ref):
        i = pl.program_id(0)
        hv, wv = h_ref[...], w_ref[...]
        r = lax.rsqrt(jnp.mean(hv * hv, axis=-1, keepdims=True) + NORM_EPS)
        hh = hv * r
        err = hh * wv - t_ref[...]
        dout = err * (1.0 / d)
        dhh = dout * wv
        dh = r * (dhh - hh * jnp.mean(dhh * hh, axis=-1, keepdims=True))
        dh_ref[...] = dh
        dhb_ref[...] = dh.astype(BF16)

        @pl.when(i == 0)
        def _():
            dw_ref[...] = jnp.zeros_like(dw_ref)
            loss_ref[...] = jnp.zeros_like(loss_ref)

        dw_ref[...] += jnp.sum(dout * hh, axis=0, keepdims=True)
        loss_ref[...] += jnp.full((1, 128), 0.5 / d, F32) * jnp.sum(err * err)

    row = pl.BlockSpec((tr, d), lambda i: (i, 0))
    vec = pl.BlockSpec((1, d), lambda i: (0, 0))
    lspec = pl.BlockSpec((1, 128), lambda i: (0, 0))
    return pl.pallas_call(
        body, name="loss_head",
        out_shape=(jax.ShapeDtypeStruct((1, 128), F32), jax.ShapeDtypeStruct((t, d), F32),
                   jax.ShapeDtypeStruct((t, d), BF16), jax.ShapeDtypeStruct((1, d), F32)),
        grid=(t // tr,), in_specs=[row, vec, row], out_specs=(lspec, row, row, vec),
        compiler_params=_params(("arbitrary",)),
    )(h, w.reshape(1, d), target)


def _inv_unit_lower(a):
    c = a.shape[0]
    eye = (_iota((c, c), 0) == _iota((c, c), 1)).astype(F32)
    x = eye - a
    p = _dot3(a, a, NN)
    yield
    n = 2
    while n < c:
        x = x + _dot3(x, p, NN)
        n *= 2
        if n < c:
            p = _dot3(p, p, NN)
        yield
    return x


def _gdn_chunk(q, k, v, beta, g):
    c = q.shape[0]
    row, col = _iota((c, c), 0), _iota((c, c), 1)
    gc = _cumsum_rows(g)
    diff = gc - gc.T
    dec = jnp.where(row >= col, jnp.exp(jnp.minimum(diff, 0.0)), 0.0)
    dec_s = jnp.where(row > col, dec, 0.0)
    gam = jnp.exp(gc)
    g_last = jnp.sum(g, axis=0, keepdims=True)
    kk = _dot(k, k, NT)
    a = beta * kk * dec_s
    p = _dot(q, k, NT) * dec
    e_end = jnp.exp(g_last - gc)
    return dict(dec=dec, dec_s=dec_s, gam=gam, gam_last=jnp.exp(g_last), e_end=e_end,
                k_end=k * e_end, kk=kk, a=a, p=p)


def _gdn_fwd(q, k, v, beta_bc, g_bc):
    t = q.shape[0]
    h = q.shape[1] // HEAD_DIM
    nc = t // CHUNK

    def body(q_ref, k_ref, v_ref, b_ref, g_ref, o_ref, s_ref, t_ref, state):
        qv, kv, vv, beta = q_ref[...], k_ref[...], v_ref[...], b_ref[...]
        ch = _gdn_chunk(qv, kv, vv, beta, g_ref[...])
        yield
        tm = yield from _inv_unit_lower(ch["a"])
        sol = _dot(tm, jnp.concatenate([beta * vv, beta * ch["gam"] * kv], axis=1), NN)
        yield
        u_v, w = sol[:, :HEAD_DIM], sol[:, HEAD_DIM:]
        s0 = state[...]
        u = u_v - _dot(w, s0, NN)
        yield
        o_ref[...] = _dot(qv * ch["gam"], s0, NN) + _dot(ch["p"], u, NN)
        s_ref[...] = s0
        t_ref[...] = tm
        state[...] = ch["gam_last"] * s0 + _dot(ch["k_end"], u, TN)

    tok = pl.BlockSpec((CHUNK, _hps(h) * HEAD_DIM), lambda hh, c: (c, hh))
    bc = pl.BlockSpec((_hps(h), CHUNK, HEAD_DIM), lambda hh, c: (hh, c, 0))
    mat = pl.BlockSpec((_hps(h), None, HEAD_DIM, HEAD_DIM), lambda hh, c: (hh, c, 0, 0))
    return pl.pallas_call(
        _each_head(body, 5), name="gdn_fwd",
        out_shape=(jax.ShapeDtypeStruct(q.shape, F32),
                   jax.ShapeDtypeStruct((h, nc, HEAD_DIM, HEAD_DIM), F32),
                   jax.ShapeDtypeStruct((h, nc, CHUNK, CHUNK), F32)),
        grid=(h // _hps(h), nc), in_specs=[tok, tok, tok, bc, bc], out_specs=(tok, mat, mat),
        scratch_shapes=[pltpu.VMEM((_hps(h), HEAD_DIM, HEAD_DIM), F32)],
        compiler_params=_params(("parallel", "arbitrary")),
    )(q, k, v, beta_bc, g_bc)


def _gdn_bwd(q, k, v, beta_bc, g_bc, states, invs, do, do_blk=0):
    t = q.shape[0]
    h = q.shape[1] // HEAD_DIM
    nc = t // CHUNK

    def body(q_ref, k_ref, v_ref, b_ref, g_ref, s_ref, t_ref, do_ref,
             dq_ref, dk_ref, dv_ref, db_ref, dg_ref, dstate):
        qv, kv, vv, beta = q_ref[...], k_ref[...], v_ref[...], b_ref[...]
        dov, s0, tm, ds1 = do_ref[...], s_ref[...], t_ref[...], dstate[...]
        ch = _gdn_chunk(qv, kv, vv, beta, g_ref[...])
        yield
        gam, dec, dec_s, kk = ch["gam"], ch["dec"], ch["dec_s"], ch["kk"]
        r_v, r_w = beta * vv, beta * gam * kv
        sol = _dot(tm, jnp.concatenate([r_v, r_w], axis=1), NN)
        yield
        u_v, w = sol[:, :HEAD_DIM], sol[:, HEAD_DIM:]
        u = u_v - _dot(w, s0, NN)
        qg = qv * gam
        yield

        du = _dot(ch["p"], dov, TN) + _dot(ch["k_end"], ds1, NN)
        dp = _dot(dov, u, NT)
        dpd = dp * dec
        dqg = _dot(dov, s0, NT)
        dk_end = _dot(u, ds1, NT)
        yield
        dq = dqg * gam + _dot(dpd, kv, NN)
        dk = _dot(dpd, qv, TN) + dk_end * ch["e_end"]
        dstate[...] = _dot(qg, dov, TN) + ch["gam_last"] * ds1 - _dot(w, du, TN)
        dw = -_dot(du, s0, NT)
        yield
        dr = _dot(tm, jnp.concatenate([du, dw], axis=1), TN)
        yield
        dr_v, dr_w = dr[:, :HEAD_DIM], dr[:, HEAD_DIM:]
        da = -_dot(dr, sol, NT)
        yield
        dkk = da * beta * dec_s
        dk = dk + _dot(dkk, kv, NN) + _dot(dkk, kv, TN) + beta * gam * dr_w
        dbeta = (jnp.sum(da * kk * dec_s, axis=1, keepdims=True)
                 + jnp.sum(dr_v * vv + dr_w * gam * kv, axis=1, keepdims=True))

        pair = dp * ch["p"] + da * ch["a"]
        end = jnp.sum(dk_end * ch["k_end"], axis=1, keepdims=True)
        dgc = (jnp.sum(pair - pair.T, axis=1, keepdims=True)
               + jnp.sum(dqg * qg + dr_w * r_w, axis=1, keepdims=True) - end)
        at_end = jnp.sum(end) + ch["gam_last"] * jnp.sum(s0 * ds1)
        dgc = jnp.broadcast_to(dgc, (CHUNK, HEAD_DIM))
        dgc = dgc + jnp.where(_iota((CHUNK, HEAD_DIM), 0) == CHUNK - 1, at_end, 0.0)
        dq_ref[...] = dq
        dk_ref[...] = dk
        dv_ref[...] = beta * dr_v
        db_ref[...] = jnp.broadcast_to(dbeta, (CHUNK, HEAD_DIM))
        dg_ref[...] = _rev_cumsum_rows(dgc)

    rev = lambda c: nc - 1 - c
    tok = pl.BlockSpec((CHUNK, _hps(h) * HEAD_DIM), lambda hh, c: (rev(c), hh))
    bc = pl.BlockSpec((_hps(h), CHUNK, HEAD_DIM), lambda hh, c: (hh, rev(c), 0))
    mat = pl.BlockSpec((_hps(h), None, HEAD_DIM, HEAD_DIM), lambda hh, c: (hh, rev(c), 0, 0))
    tok_shape = jax.ShapeDtypeStruct(q.shape, F32)
    bc_shape = jax.ShapeDtypeStruct((h, t, HEAD_DIM), F32)
    return pl.pallas_call(
        _each_head(body, 8), name="gdn_bwd",
        out_shape=(tok_shape, tok_shape, tok_shape, bc_shape, bc_shape),
        grid=(h // _hps(h), nc),
        in_specs=[tok, tok, tok, bc, bc, mat, mat,
                  pl.BlockSpec((CHUNK, _hps(h) * HEAD_DIM), lambda hh, c: (rev(c), do_blk // _hps(h) + hh))],
        out_specs=(tok, tok, tok, bc, bc),
        scratch_shapes=[pltpu.VMEM((_hps(h), HEAD_DIM, HEAD_DIM), F32)],
        compiler_params=_params(("parallel", "arbitrary")),
    )(q, k, v, beta_bc, g_bc, states, invs, do)


def _hgrn_chunk(q, k, lf):
    c = q.shape[0]
    row = _iota((c, HEAD_DIM), 0)
    b = _cumsum_rows(lf)
    q_subs, k_facs, a_rows = [], [], []
    for x in range(c // SUB):
        b_start = jnp.sum(jnp.where(row < x * SUB, lf, 0.0), axis=0, keepdims=True)
        q_x = (q * jnp.exp(jnp.minimum(b - b_start, 0.0)))[x * SUB:(x + 1) * SUB]
        k_fac = jnp.where(row < (x + 1) * SUB, jnp.exp(jnp.minimum(b_start - b, EXP_CAP)), 0.0)
        q_subs.append(q_x)
        k_facs.append(k_fac)
        a_rows.append(_dot(q_x, k * k_fac, NT))
    a = jnp.concatenate(a_rows, axis=0)
    a = jnp.where(_iota((c, c), 0) >= _iota((c, c), 1), a, 0.0)
    b_last = jnp.sum(lf, axis=0, keepdims=True)
    return dict(b=b, a=a, q_subs=q_subs, k_facs=k_facs, e_b=jnp.exp(b),
                e_end=jnp.exp(b_last - b), e_last=jnp.exp(b_last))


def _hgrn_fwd(q, k, v, lf, v_blk=0):
    t = q.shape[0]
    h = q.shape[1] // HEAD_DIM
    nc = t // CHUNK

    def body(q_ref, k_ref, v_ref, lf_ref, o_ref, s_ref, state):
        qv, kv, vv = q_ref[...], k_ref[...], v_ref[...]
        ch = _hgrn_chunk(qv, kv, lf_ref[...])
        yield
        s0 = state[...]
        o_ref[...] = _dot(qv * ch["e_b"], s0, NT) + _dot(ch["a"], vv, NN)
        s_ref[...] = s0
        state[...] = s0 * ch["e_last"] + _dot(vv, kv * ch["e_end"], TN)

    tok = pl.BlockSpec((CHUNK, _hps(h) * HEAD_DIM), lambda hh, c: (c, hh))
    mat = pl.BlockSpec((_hps(h), None, HEAD_DIM, HEAD_DIM), lambda hh, c: (hh, c, 0, 0))
    return pl.pallas_call(
        _each_head(body, 4), name="hgrn_fwd",
        out_shape=(jax.ShapeDtypeStruct(q.shape, F32),
                   jax.ShapeDtypeStruct((h, nc, HEAD_DIM, HEAD_DIM), F32)),
        grid=(h // _hps(h), nc),
        in_specs=[tok, tok, pl.BlockSpec((CHUNK, _hps(h) * HEAD_DIM), lambda hh, c: (c, v_blk // _hps(h) + hh)), tok],
        out_specs=(tok, mat),
        scratch_shapes=[pltpu.VMEM((_hps(h), HEAD_DIM, HEAD_DIM), F32)],
        compiler_params=_params(("parallel", "arbitrary")),
    )(q, k, v, lf)


def _hgrn_bwd(q, k, v, lf, states, do, v_blk=0, do_blk=0):
    t = q.shape[0]
    nc = t // CHUNK
    h = q.shape[1] // HEAD_DIM

    def body(q_ref, k_ref, v_ref, lf_ref, s_ref, do_ref, dq_ref, dk_ref, dv_ref, dlf_ref, dstate):
        qv, kv, vv, dov, s0 = q_ref[...], k_ref[...], v_ref[...], do_ref[...], s_ref[...]
        ds1 = dstate[...]
        ch = _hgrn_chunk(qv, kv, lf_ref[...])
        yield
        c = CHUNK
        row = _iota((c, HEAD_DIM), 0)
        qh = qv * ch["e_b"]
        k_end = kv * ch["e_end"]
        da = jnp.where(_iota((c, c), 0) >= _iota((c, c), 1), _dot(dov, vv, NT), 0.0)
        dqh = _dot(dov, s0, NN)
        dk_end = _dot(vv, ds1, NN)
        yield
        end = dk_end * k_end
        dk = dk_end * ch["e_end"]
        db = dqh * qh - end + jnp.where(
            row == c - 1, jnp.sum(end + s0 * ch["e_last"] * ds1, axis=0, keepdims=True), 0.0)
        dq_rows, qdq_rows = [], []
        for x in range(c // SUB):
            da_x = da[x * SUB:(x + 1) * SUB]
            k_x = kv * ch["k_facs"][x]
            dq_x = _dot(da_x, k_x, NN)
            dk_x = _dot(da_x, ch["q_subs"][x], TN)
            dq_rows.append(dq_x)
            qdq_rows.append(dq_x * ch["q_subs"][x])
            dk = dk + dk_x * ch["k_facs"][x]
            kdk = dk_x * k_x
            db = db - kdk
            if x > 0:
                at_start = jnp.sum(kdk, axis=0, keepdims=True) - jnp.sum(qdq_rows[x], axis=0, keepdims=True)
                db = db + jnp.where(row == x * SUB - 1, at_start, 0.0)
        yield
        b_start = jnp.zeros((c, HEAD_DIM), F32)
        for x in range(1, c // SUB):
            b_x = jnp.sum(jnp.where(row < x * SUB, lf_ref[...], 0.0), axis=0, keepdims=True)
            b_start = jnp.where(row >= x * SUB, b_x, b_start)
        dq = dqh * ch["e_b"] + jnp.concatenate(dq_rows, axis=0) * jnp.exp(jnp.minimum(ch["b"] - b_start, 0.0))
        db = db + jnp.concatenate(qdq_rows, axis=0)
        dstate[...] = _dot(dov, qh, TN) + ds1 * ch["e_last"]
        dq_ref[...] = dq
        dk_ref[...] = dk
        dv_ref[...] = _dot(ch["a"], dov, TN) + _dot(k_end, ds1, NT)
        dlf_ref[...] = _rev_cumsum_rows(db)

    rev = lambda c: nc - 1 - c
    tok = pl.BlockSpec((CHUNK, _hps(h) * HEAD_DIM), lambda hh, c: (rev(c), hh))
    mat = pl.BlockSpec((_hps(h), None, HEAD_DIM, HEAD_DIM), lambda hh, c: (hh, rev(c), 0, 0))
    tok_shape = jax.ShapeDtypeStruct(q.shape, F32)
    return pl.pallas_call(
        _each_head(body, 6), name="hgrn_bwd",
        out_shape=(tok_shape, tok_shape, tok_shape, tok_shape),
        grid=(h // _hps(h), nc),
        in_specs=[tok, tok, pl.BlockSpec((CHUNK, _hps(h) * HEAD_DIM), lambda hh, c: (rev(c), v_blk // _hps(h) + hh)), tok, mat,
                  pl.BlockSpec((CHUNK, _hps(h) * HEAD_DIM), lambda hh, c: (rev(c), do_blk // _hps(h) + hh))],
        out_specs=(tok, tok, tok, tok),
        scratch_shapes=[pltpu.VMEM((_hps(h), HEAD_DIM, HEAD_DIM), F32)],
        compiler_params=_params(("parallel", "arbitrary")),
    )(q, k, v, lf, states, do)


CONV_ROWS = 256
HALO = 8


def _shift_down(cur, prev, s):
    rt = cur.shape[0]
    head = jnp.concatenate([pltpu.roll(prev, s, 0), jnp.zeros((rt - HALO, cur.shape[1]), F32)], axis=0)
    return jnp.where(_iota(cur.shape, 0) < s, head, pltpu.roll(cur, s, 0))


def _shift_up(cur, nxt, s):
    rt = cur.shape[0]
    tail = jnp.concatenate([jnp.zeros((rt - HALO, cur.shape[1]), F32), pltpu.roll(nxt, HALO - s, 0)], axis=0)
    return jnp.where(_iota(cur.shape, 0) >= rt - s, tail, pltpu.roll(cur, rt - s, 0))


def _tile_with_prev(ref, i, rt):
    r0 = pl.multiple_of(i * rt, rt)
    cur = ref[pl.ds(r0, rt), :]
    prev = ref[pl.ds(pl.multiple_of(jnp.maximum(r0 - HALO, 0), HALO), HALO), :]
    return cur, jnp.where(i > 0, prev, 0.0)


def _tile_with_next(ref, i, rt, n_tiles):
    r0 = pl.multiple_of(i * rt, rt)
    cur = ref[pl.ds(r0, rt), :]
    nxt = ref[pl.ds(pl.multiple_of(jnp.minimum(r0 + rt, (n_tiles - 1) * rt), HALO), HALO), :]
    return cur, jnp.where(i < n_tiles - 1, nxt, 0.0)


def _conv_tile(x_ref, w_ref, i, rt):
    cur, prev = _tile_with_prev(x_ref, i, rt)
    shifted = [_shift_down(cur, prev, CONV_W - 1 - j) for j in range(CONV_W - 1)] + [cur]
    c = shifted[0] * w_ref[pl.ds(0, 1), :]
    for j in range(1, CONV_W):
        c = c + shifted[j] * w_ref[pl.ds(j, 1), :]
    return c, shifted


def _l2n(s):
    return s * lax.rsqrt(jnp.sum(s * s, axis=-1, keepdims=True) + L2_EPS)


def _gdn_prep_fwd(proj, conv_w, h):
    t = proj.shape[0]
    rt = _tile(t, CONV_ROWS)
    nt = t // rt
    scale = HEAD_DIM ** -0.5

    def body(xq, xk, xv, wq, wk, wv, q_ref, k_ref, v_ref):
        def tile(i, carry):
            rows = pl.ds(pl.multiple_of(i * rt, rt), rt)
            q_ref[rows, :] = _l2n(_silu(_conv_tile(xq, wq, i, rt)[0])) * scale
            k_ref[rows, :] = _l2n(_silu(_conv_tile(xk, wk, i, rt)[0]))
            v_ref[rows, :] = _silu(_conv_tile(xv, wv, i, rt)[0])
            return carry

        lax.fori_loop(0, nt, tile, 0)

    col = lambda p: pl.BlockSpec((t, HEAD_DIM), lambda hh: (0, p * h + hh))
    wcol = lambda p: pl.BlockSpec((CONV_W, HEAD_DIM), lambda hh: (0, p * h + hh))
    out = pl.BlockSpec((t, HEAD_DIM), lambda hh: (0, hh))
    shape = jax.ShapeDtypeStruct((t, h * HEAD_DIM), F32)
    return pl.pallas_call(
        body, name="gdn_prep_fwd", out_shape=(shape, shape, shape), grid=(h,),
        in_specs=[col(0), col(1), col(2), wcol(0), wcol(1), wcol(2)], out_specs=(out, out, out),
        compiler_params=_params(("parallel",)),
    )(proj, proj, proj, conv_w, conv_w, conv_w)


def _gdn_prep_bwd(proj, conv_w, dq, dk, dv, h):
    t = proj.shape[0]
    rt = _tile(t, CONV_ROWS)
    nt = t // rt
    scale = HEAD_DIM ** -0.5

    def part(x_ref, w_ref, dy_ref, dx_ref, dw_ref, dc_ref, norm_scale):
        def first(i, dws):
            rows = pl.ds(pl.multiple_of(i * rt, rt), rt)
            c, shifted = _conv_tile(x_ref, w_ref, i, rt)
            ds = dy_ref[rows, :]
            if norm_scale is not None:
                s = _silu(c)
                r = lax.rsqrt(jnp.sum(s * s, axis=-1, keepdims=True) + L2_EPS)
                y = s * r
                dyn = ds * norm_scale
                ds = r * (dyn - y * jnp.sum(dyn * y, axis=-1, keepdims=True))
            dc = ds * _dsilu(c)
            dc_ref[rows, :] = dc
            return tuple(dws[j] + jnp.sum(dc * shifted[j], axis=0, keepdims=True) for j in range(CONV_W))

        dws = lax.fori_loop(0, nt, first, tuple(jnp.zeros((1, HEAD_DIM), F32) for _ in range(CONV_W)))
        for j in range(CONV_W):
            dw_ref[pl.ds(j, 1), :] = dws[j]

        def second(i, carry):
            rows = pl.ds(pl.multiple_of(i * rt, rt), rt)
            cur, nxt = _tile_with_next(dc_ref, i, rt, nt)
            dx = cur * w_ref[pl.ds(CONV_W - 1, 1), :]
            for j in range(CONV_W - 1):
                dx = dx + _shift_up(cur, nxt, CONV_W - 1 - j) * w_ref[pl.ds(j, 1), :]
            dx_ref[rows, :] = dx.astype(dx_ref.dtype)
            return carry

        lax.fori_loop(0, nt, second, 0)

    def body(xq, xk, xv, wq, wk, wv, dq_ref, dk_ref, dv_ref, dxq, dxk, dxv, dwq, dwk, dwv, dc_ref):
        part(xq, wq, dq_ref, dxq, dwq, dc_ref, scale)
        part(xk, wk, dk_ref, dxk, dwk, dc_ref, 1.0)
        part(xv, wv, dv_ref, dxv, dwv, dc_ref, None)

    col = lambda p: pl.BlockSpec((t, HEAD_DIM), lambda hh: (0, p * h + hh))
    wcol = lambda p: pl.BlockSpec((CONV_W, HEAD_DIM), lambda hh: (0, p * h + hh))
    own = pl.BlockSpec((t, HEAD_DIM), lambda hh: (0, hh))
    wown = pl.BlockSpec((CONV_W, HEAD_DIM), lambda hh: (0, hh))
    dx_shape = jax.ShapeDtypeStruct((t, h * HEAD_DIM), BF16)
    dw_shape = jax.ShapeDtypeStruct((CONV_W, h * HEAD_DIM), F32)
    return pl.pallas_call(
        body, name="gdn_prep_bwd",
        out_shape=(dx_shape, dx_shape, dx_shape, dw_shape, dw_shape, dw_shape), grid=(h,),
        in_specs=[col(0), col(1), col(2), wcol(0), wcol(1), wcol(2), own, own, own],
        out_specs=(own, own, own, wown, wown, wown),
        scratch_shapes=[pltpu.VMEM((t, HEAD_DIM), F32)],
        compiler_params=_params(("parallel",)),
    )(proj, proj, proj, conv_w, conv_w, conv_w, dq, dk, dv)


def _gdn_gates_fwd(ab, a_log_row, dt_bias_row):
    t = ab.shape[0]
    tr = _tile(t, 512)

    def body(ab_ref, al_ref, dt_ref, g_ref, b_ref):
        g_ref[...] = -jnp.exp(al_ref[...]) * _softplus(ab_ref[:, :HEAD_DIM] + dt_ref[...])
        b_ref[...] = _sigmoid(ab_ref[:, HEAD_DIM:])

    row = pl.BlockSpec((tr, HEAD_DIM), lambda i: (i, 0))
    vec = pl.BlockSpec((1, HEAD_DIM), lambda i: (0, 0))
    shape = jax.ShapeDtypeStruct((t, HEAD_DIM), F32)
    return pl.pallas_call(
        body, name="gdn_gates_fwd", out_shape=(shape, shape), grid=(t // tr,),
        in_specs=[pl.BlockSpec((tr, 2 * HEAD_DIM), lambda i: (i, 0)), vec, vec], out_specs=(row, row),
        compiler_params=_params(("parallel",)),
    )(ab, a_log_row, dt_bias_row)


def _gdn_gates_bwd(ab, a_log_row, dt_bias_row, dg, dbeta):
    t = ab.shape[0]
    tr = _tile(t, 512)

    def body(ab_ref, al_ref, dt_ref, dg_ref, db_ref, dab_ref, dal_ref, ddt_ref):
        @pl.when(pl.program_id(0) == 0)
        def _():
            dal_ref[...] = jnp.zeros_like(dal_ref)
            ddt_ref[...] = jnp.zeros_like(ddt_ref)

        xa = ab_ref[:, :HEAD_DIM] + dt_ref[...]
        neg_a = -jnp.exp(al_ref[...])
        dgv = dg_ref[...]
        da = dgv * neg_a * _sigmoid(xa)
        beta = _sigmoid(ab_ref[:, HEAD_DIM:])
        dab_ref[:, :HEAD_DIM] = da.astype(BF16)
        dab_ref[:, HEAD_DIM:] = (db_ref[...] * beta * (1.0 - beta)).astype(BF16)
        dal_ref[...] += jnp.sum(dgv * neg_a * _softplus(xa), axis=0, keepdims=True)
        ddt_ref[...] += jnp.sum(da, axis=0, keepdims=True)

    row = pl.BlockSpec((tr, HEAD_DIM), lambda i: (i, 0))
    row2 = pl.BlockSpec((tr, 2 * HEAD_DIM), lambda i: (i, 0))
    vec = pl.BlockSpec((1, HEAD_DIM), lambda i: (0, 0))
    vshape = jax.ShapeDtypeStruct((1, HEAD_DIM), F32)
    return pl.pallas_call(
        body, name="gdn_gates_bwd",
        out_shape=(jax.ShapeDtypeStruct((t, 2 * HEAD_DIM), BF16), vshape, vshape), grid=(t // tr,),
        in_specs=[row2, vec, vec, row, row], out_specs=(row2, vec, vec),
        compiler_params=_params(("arbitrary",)),
    )(ab, a_log_row, dt_bias_row, dg, dbeta)


def _lower_bound(lb_ref):
    return _sigmoid(lb_ref[pl.ds(0, 1), :] - lb_ref[pl.ds(1, 1), :])


def _hgrn_prep_fwd(proj, lb_logits, h, q_blk, f_blk):
    t = proj.shape[0]
    tr = _tile(t, 512)

    def body(xq, xf, lb_ref, q_ref, k_ref, lf_ref):
        lb = _lower_bound(lb_ref)
        s = _sigmoid(xf[...])
        q_ref[...] = _silu(xq[...])
        k_ref[...] = (1.0 - lb) * (1.0 - s)
        lf_ref[...] = jnp.log(lb + (1.0 - lb) * s)

    col = lambda b0: pl.BlockSpec((tr, HEAD_DIM), lambda hh, i: (i, b0 + hh))
    own = pl.BlockSpec((tr, HEAD_DIM), lambda hh, i: (i, hh))
    shape = jax.ShapeDtypeStruct((t, h * HEAD_DIM), F32)
    return pl.pallas_call(
        body, name="hgrn_prep_fwd", out_shape=(shape, shape, shape), grid=(h, t // tr),
        in_specs=[col(q_blk), col(f_blk), pl.BlockSpec((2, HEAD_DIM), lambda hh, i: (0, hh))],
        out_specs=(own, own, own), compiler_params=_params(("parallel", "parallel")),
    )(proj, proj, lb_logits)


def _hgrn_prep_bwd(proj, lb_logits, dq, dk, dlf, h, q_blk, f_blk):
    t = proj.shape[0]
    tr = _tile(t, 512)

    def body(xq, xf, lb_ref, dq_ref, dk_ref, dlf_ref, dxq, dxf, dlb_ref):
        @pl.when(pl.program_id(1) == 0)
        def _():
            dlb_ref[...] = jnp.zeros_like(dlb_ref)

        lb = _lower_bound(lb_ref)
        s = _sigmoid(xf[...])
        e = dlf_ref[...] / (lb + (1.0 - lb) * s) - dk_ref[...]
        dxq[...] = (dq_ref[...] * _dsilu(xq[...])).astype(BF16)
        dxf[...] = (s * (1.0 - s) * (1.0 - lb) * e).astype(BF16)
        d0 = jnp.sum((1.0 - s) * e, axis=0, keepdims=True) * (lb * (1.0 - lb))
        dlb_ref[pl.ds(0, 1), :] += d0
        dlb_ref[pl.ds(1, 1), :] += -d0

    col = lambda b0: pl.BlockSpec((tr, HEAD_DIM), lambda hh, i: (i, b0 + hh))
    own = pl.BlockSpec((tr, HEAD_DIM), lambda hh, i: (i, hh))
    lbs = pl.BlockSpec((2, HEAD_DIM), lambda hh, i: (0, hh))
    shape = jax.ShapeDtypeStruct((t, h * HEAD_DIM), BF16)
    return pl.pallas_call(
        body, name="hgrn_prep_bwd",
        out_shape=(shape, shape, jax.ShapeDtypeStruct((2, h * HEAD_DIM), F32)), grid=(h, t // tr),
        in_specs=[col(q_blk), col(f_blk), lbs, own, own, own], out_specs=(own, own, lbs),
        compiler_params=_params(("parallel", "arbitrary")),
    )(proj, proj, lb_logits, dq, dk, dlf)


def _gate_specs(h, z_blk, g_blk, tr):
    o_a = pl.BlockSpec((tr, HEAD_DIM), lambda hh, i: (i, jnp.minimum(hh, h - 1)))
    o_b = pl.BlockSpec((tr, HEAD_DIM), lambda hh, i: (i, jnp.maximum(hh - h, 0)))
    gate = pl.BlockSpec((tr, HEAD_DIM), lambda hh, i: (i, jnp.where(hh < h, z_blk + hh, g_blk + hh - h)))
    w = pl.BlockSpec((None, 1, HEAD_DIM), lambda hh, i: (hh // h, 0, 0))
    cat = pl.BlockSpec((tr, HEAD_DIM), lambda hh, i: (i, hh))
    return o_a, o_b, gate, w, cat


def _gate_fwd(o_a, o_b, proj, norm_w, h, z_blk, g_blk):
    t = o_a.shape[0]
    tr = _tile(t, 512)

    def body(oa_ref, ob_ref, z_ref, w_ref, y_ref):
        o = jnp.where(pl.program_id(0) < h, oa_ref[...], ob_ref[...])
        r = lax.rsqrt(jnp.mean(o * o, axis=-1, keepdims=True) + NORM_EPS)
        y_ref[...] = (o * r * w_ref[...] * _silu(z_ref[...])).astype(y_ref.dtype)

    sa, sb, sg, sw, cat = _gate_specs(h, z_blk, g_blk, tr)
    return pl.pallas_call(
        body, name="gate_fwd", out_shape=jax.ShapeDtypeStruct((t, 2 * h * HEAD_DIM), BF16),
        grid=(2 * h, t // tr), in_specs=[sa, sb, sg, sw], out_specs=cat,
        compiler_params=_params(("parallel", "parallel")),
    )(o_a, o_b, proj, norm_w)


def _gate_bwd(o_a, o_b, proj, norm_w, dy, h, z_blk, g_blk):
    t = o_a.shape[0]
    tr = _tile(t, 512)

    def body(oa_ref, ob_ref, z_ref, w_ref, dy_ref, do_ref, dz_ref, dw_ref):
        hh = pl.program_id(0)

        @pl.when(jnp.logical_and(hh % h == 0, pl.program_id(1) == 0))
        def _():
            dw_ref[...] = jnp.zeros_like(dw_ref)

        o = jnp.where(hh < h, oa_ref[...], ob_ref[...])
        z, w, dyv = z_ref[...], w_ref[...], dy_ref[...]
        r = lax.rsqrt(jnp.mean(o * o, axis=-1, keepdims=True) + NORM_EPS)
        oh = o * r
        dz_ref[...] = (dyv * oh * w * _dsilu(z)).astype(dz_ref.dtype)
        dn = dyv * _silu(z)
        doh = dn * w
        do_ref[...] = r * (doh - oh * jnp.mean(doh * oh, axis=-1, keepdims=True))
        dw_ref[...] += jnp.sum(dn * oh, axis=0, keepdims=True)

    sa, sb, sg, sw, cat = _gate_specs(h, z_blk, g_blk, tr)
    width = 2 * h * HEAD_DIM
    return pl.pallas_call(
        body, name="gate_bwd",
        out_shape=(jax.ShapeDtypeStruct((t, width), F32), jax.ShapeDtypeStruct((t, width), BF16),
                   jax.ShapeDtypeStruct((2, 1, HEAD_DIM), F32)),
        grid=(2 * h, t // tr), in_specs=[sa, sb, sg, sw, cat], out_specs=(cat, cat, sw),
        compiler_params=_params(("arbitrary", "arbitrary")),
    )(o_a, o_b, proj, norm_w, dy)


def _lane_row(vec):
    return jnp.pad(vec.reshape(1, -1), ((0, 0), (0, HEAD_DIM - vec.shape[-1])))


def _add_epi(acc, res):
    return (acc + res,)


def _split_w_in(w_in, h):
    gw = h * HEAD_DIM
    main = jnp.concatenate([w_in[:, :4 * gw], w_in[:, 4 * gw + 2 * h:]], axis=1)
    pad = jnp.zeros((w_in.shape[0], HEAD_DIM - h), w_in.dtype)
    ab = jnp.concatenate([w_in[:, 4 * gw:4 * gw + h], pad, w_in[:, 4 * gw + h:4 * gw + 2 * h], pad], axis=1)
    return main, ab


def _merge_w_in(main, ab, h):
    gw = h * HEAD_DIM
    return jnp.concatenate([main[:, :4 * gw], ab[:, :h], ab[:, HEAD_DIM:HEAD_DIM + h], main[:, 4 * gw:]], axis=1)


def _local_step(x, target, w_main, w_ab, conv_w, a_log, dt_bias, gdn_norm_w, lb_logits, hgrn_norm_w,
                w_out, norm_mix_w, norm_ffn_w, w_ff1, w_ff2, norm_final_w):
    t, d = x.shape
    h = d // (2 * HEAD_DIM)
    gw = h * HEAD_DIM
    k_blk, v_blk, z_blk, qb_blk, fb_blk, ib_blk, gb_blk = (i * h for i in range(1, 8))
    del k_blk, v_blk

    n1 = _rms_fwd(x, norm_mix_w, "rms_mix")
    proj = _mm(n1, w_main, "nn", (F32,), "in_proj")
    ab = _mm(n1, w_ab, "nn", (F32,), "in_proj_ab")

    q, k, v = _gdn_prep_fwd(proj, conv_w, h)
    a_log_row, dt_row = _lane_row(a_log), _lane_row(dt_bias)
    g_tm, beta_tm = _gdn_gates_fwd(ab, a_log_row, dt_row)
    to_heads = lambda a: jnp.broadcast_to(a[:, :h].T[:, :, None], (h, t, HEAD_DIM))
    g_bc, beta_bc = to_heads(g_tm), to_heads(beta_tm)
    o_a, st_a, inv_a = _gdn_fwd(q, k, v, beta_bc, g_bc)

    qh, kh, lf = _hgrn_prep_fwd(proj, lb_logits, h, qb_blk, fb_blk)
    o_b, st_b = _hgrn_fwd(qh, kh, proj, lf, v_blk=ib_blk)

    gate_w = jnp.stack([gdn_norm_w.reshape(1, HEAD_DIM), hgrn_norm_w.reshape(1, HEAD_DIM)])
    y = _gate_fwd(o_a, o_b, proj, gate_w, h, z_blk, gb_blk)
    h1 = _mm(y, w_out, "nn", (F32,), "out_proj", epi=_add_epi, extras=(x,))
    n2 = _rms_fwd(h1, norm_ffn_w, "rms_ffn")
    act, r = _mm(n2, w_ff1, "nn", (F32, BF16), "ff1", b_stacked=True,
                 epi=lambda acc: (acc, jnp.square(jnp.maximum(acc, 0.0))))
    h2 = _mm(r, w_ff2, "nn", (F32,), "ff2", epi=_add_epi, extras=(h1,))
    loss, dh2, dh2_b, d_norm_final = _loss_head(h2, norm_final_w, target)

    da = _mm(dh2_b, w_ff2, "nt", (BF16,), "ff2_dx",
             epi=lambda acc, a: (acc * (2.0 * jnp.maximum(a, 0.0)),), extras=(act,))
    d_ff2 = _mm(r, dh2_b, "tn", (F32,), "ff2_dw")
    dn2 = _mm(da, w_ff1, "nt", (F32,), "ff1_dx", b_stacked=True)
    d_ff1 = _mm(n2, da, "tn", (F32,), "ff1_dw", out_stacked=True)
    dh1, dh1_b, d_norm_ffn = _rms_bwd(dn2, h1, norm_ffn_w, dh2, "rms_ffn_bwd")
    dy = _mm(dh1_b, w_out, "nt", (F32,), "out_proj_dx")
    d_out = _mm(y, dh1_b, "tn", (F32,), "out_proj_dw")

    do, dgate, d_gate_w = _gate_bwd(o_a, o_b, proj, gate_w, dy, h, z_blk, gb_blk)
    dq, dk, dv, dbeta_bc, dg_bc = _gdn_bwd(q, k, v, beta_bc, g_bc, st_a, inv_a, do, do_blk=0)
    dxq, dxk, dxv, dcq, dck, dcv = _gdn_prep_bwd(proj, conv_w, dq, dk, dv, h)
    from_heads = lambda a: jnp.pad(a[:, :, 0].T, ((0, 0), (0, HEAD_DIM - h)))
    dab, d_a_log, d_dt_bias = _gdn_gates_bwd(ab, a_log_row, dt_row, from_heads(dg_bc), from_heads(dbeta_bc))
    dqh, dkh, dvh, dlf = _hgrn_bwd(qh, kh, proj, lf, st_b, do, v_blk=ib_blk, do_blk=h)
    dxqb, dxfb, d_lb = _hgrn_prep_bwd(proj, lb_logits, dqh, dkh, dlf, h, qb_blk, fb_blk)

    dproj = jnp.concatenate([dxq, dxk, dxv, dgate[:, :gw], dxqb, dxfb, dvh.astype(BF16), dgate[:, gw:]], axis=1)
    dn1_ab = _mm(dab, w_ab, "nt", (F32,), "in_proj_ab_dx")
    dn1 = _mm(dproj, w_main, "nt", (F32,), "in_proj_dx", epi=_add_epi, extras=(dn1_ab,))
    d_main = _mm(n1, dproj, "tn", (F32,), "in_proj_dw")
    d_ab = _mm(n1, dab, "tn", (F32,), "in_proj_ab_dw")
    dx, _, d_norm_mix = _rms_bwd(dn1, x, norm_mix_w, dh1, "rms_mix_bwd")

    grads = dict(
        w_main=d_main, w_ab=d_ab, conv_w=jnp.concatenate([dcq, dck, dcv], axis=1),
        gdn_a_log=d_a_log[:, :h], gdn_dt_bias=d_dt_bias[:, :h], gdn_norm_w=d_gate_w[0],
        hgrn_lb_logits=d_lb, hgrn_norm_w=d_gate_w[1], w_out=d_out, norm_mix_w=d_norm_mix,
        norm_ffn_w=d_norm_ffn, w_ff1=d_ff1, w_ff2=d_ff2, norm_final_w=d_norm_final)
    return loss, dx, grads


N_CHIPS = 4
ANY = pl.BlockSpec(memory_space=pl.ANY)


def _place():
    x, y, c = lax.axis_index("x"), lax.axis_index("y"), lax.axis_index("c")
    chips = [(1 - x, y), (x, 1 - y), (1 - x, 1 - y)]
    return x, y, c, chips


def _remote(src, dst, send_sems, recv_sems, k, to):
    return pltpu.make_async_remote_copy(src_ref=src, dst_ref=dst, send_sem=send_sems.at[k],
                                        recv_sem=recv_sems.at[k], device_id=to, device_id_type=MESH)


def _gather_weights(big, small):
    nb, ns = len(big), len(small)
    n_sem = 6 * nb + 3 * ns

    def body(*refs):
        ins, outs = refs[:nb + ns], refs[nb + ns:2 * (nb + ns)]
        send_sems, recv_sems = refs[2 * (nb + ns):]
        x, y, c, chips = _place()
        me, sibling = 2 * x + y, (x, y, 1 - c)

        def half(a, chip, hc):
            rh = big[a].shape[0] // 2
            return outs[a].at[2 * chip[0] + chip[1], pl.ds(hc * rh, rh), :]

        first, passed = [], []
        for a in range(nb):
            rh = big[a].shape[0] // 2
            for j, chip in enumerate(chips):
                first.append(_remote(ins[a].at[pl.ds(c * rh, rh), :], half(a, (x, y), c),
                                     send_sems, recv_sems, 6 * a + j, (*chip, c)))
        for s in range(ns):
            for j, chip in enumerate(chips):
                first.append(_remote(ins[nb + s], outs[nb + s].at[me], send_sems, recv_sems,
                                     6 * nb + 3 * s + j, (*chip, c)))
        for cp in first:
            cp.start()
        for a in range(nb):
            for j, chip in enumerate(chips):
                _remote(half(a, chip, c), half(a, chip, c), send_sems, recv_sems, 6 * a + j, (*chip, c)).wait_recv()
                fwd = _remote(half(a, chip, c), half(a, chip, c), send_sems, recv_sems, 6 * a + 3 + j, sibling)
                fwd.start()
                passed.append(fwd)
        for s in range(ns):
            for j, chip in enumerate(chips):
                dst = outs[nb + s].at[2 * chip[0] + chip[1]]
                _remote(dst, dst, send_sems, recv_sems, 6 * nb + 3 * s + j, (*chip, c)).wait_recv()
        for a in range(nb):
            for j, chip in enumerate(chips):
                _remote(half(a, chip, 1 - c), half(a, chip, 1 - c), send_sems, recv_sems,
                        6 * a + 3 + j, sibling).wait_recv()
        for cp in first + passed:
            cp.wait_send()

    arrays = list(big) + list(small)
    return pl.pallas_call(
        body, name="gather_weights",
        out_shape=tuple(jax.ShapeDtypeStruct((N_CHIPS,) + a.shape, a.dtype) for a in arrays),
        in_specs=[ANY] * len(arrays), out_specs=tuple(ANY for _ in arrays),
        scratch_shapes=[pltpu.SemaphoreType.DMA((n_sem,)), pltpu.SemaphoreType.DMA((n_sem,))],
    )(*arrays)


def _swap_halves(parts):
    n = len(parts)

    def body(*refs):
        ins, outs = refs[:n], refs[n:2 * n]
        send_sems, recv_sems = refs[2 * n:]
        x, y, c, _ = _place()
        copies = [_remote(ins[a].at[s, 1 - c], outs[a].at[s], send_sems, recv_sems, N_CHIPS * a + s, (x, y, 1 - c))
                  for a in range(n) for s in range(N_CHIPS)]
        for cp in copies:
            cp.start()
        for cp in copies:
            cp.wait()

    return pl.pallas_call(
        body, name="grad_swap_halves",
        out_shape=tuple(jax.ShapeDtypeStruct((N_CHIPS,) + p.shape[2:], p.dtype) for p in parts),
        in_specs=[ANY] * n, out_specs=tuple(ANY for _ in parts),
        scratch_shapes=[pltpu.SemaphoreType.DMA((N_CHIPS * n,)), pltpu.SemaphoreType.DMA((N_CHIPS * n,))],
    )(*parts)


def _scatter_to_owners(parts):
    n = len(parts)

    def body(*refs):
        ins, outs = refs[:n], refs[n:2 * n]
        send_sems, recv_sems = refs[2 * n:]
        x, y, c, chips = _place()
        copies = [_remote(ins[a].at[2 * chip[0] + chip[1]], outs[a].at[j], send_sems, recv_sems,
                          3 * a + j, (*chip, c))
                  for a in range(n) for j, chip in enumerate(chips)]
        for cp in copies:
            cp.start()
        for cp in copies:
            cp.wait()

    return pl.pallas_call(
        body, name="grad_scatter_to_owners",
        out_shape=tuple(jax.ShapeDtypeStruct((3,) + p.shape[1:], p.dtype) for p in parts),
        in_specs=[ANY] * n, out_specs=tuple(ANY for _ in parts),
        scratch_shapes=[pltpu.SemaphoreType.DMA((3 * n,)), pltpu.SemaphoreType.DMA((3 * n,))],
    )(*parts)


def _send_to_sibling(halves):
    n = len(halves)

    def body(*refs):
        ins, outs = refs[:n], refs[n:2 * n]
        send_sems, recv_sems = refs[2 * n:]
        x, y, c, _ = _place()
        copies = [_remote(ins[a], outs[a], send_sems, recv_sems, a, (x, y, 1 - c)) for a in range(n)]
        for cp in copies:
            cp.start()
        for cp in copies:
            cp.wait()

    return pl.pallas_call(
        body, name="grad_send_to_sibling",
        out_shape=tuple(jax.ShapeDtypeStruct(p.shape, p.dtype) for p in halves),
        in_specs=[ANY] * n, out_specs=tuple(ANY for _ in halves),
        scratch_shapes=[pltpu.SemaphoreType.DMA((n,)), pltpu.SemaphoreType.DMA((n,))],
    )(*halves)


N_DEV = 8


def _all_reduce_small(vec):
    def body(v_ref, gathered, total, send_sems, recv_sems):
        x, y, c, _ = _place()
        me = 4 * x + 2 * y + c
        gathered[me] = v_ref[...]
        copies = []
        for k in range(1, N_DEV):
            px = 1 - x if k & 4 else x
            py = 1 - y if k & 2 else y
            pc = 1 - c if k & 1 else c
            copies.append(_remote(v_ref, gathered.at[me], send_sems, recv_sems, k - 1, (px, py, pc)))
        for cp in copies:
            cp.start()
        for k, cp in enumerate(copies):
            cp.wait_send()
        for k in range(1, N_DEV):
            px = 1 - x if k & 4 else x
            py = 1 - y if k & 2 else y
            pc = 1 - c if k & 1 else c
            src = gathered.at[4 * px + 2 * py + pc]
            _remote(src, src, send_sems, recv_sems, k - 1, (px, py, pc)).wait_recv()
        acc = gathered[0]
        for dev in range(1, N_DEV):
            acc = acc + gathered[dev]
        total[...] = acc

    vm = pl.BlockSpec(memory_space=pltpu.VMEM)
    return pl.pallas_call(
        body, name="all_reduce_small",
        out_shape=(jax.ShapeDtypeStruct((N_DEV,) + vec.shape, F32), jax.ShapeDtypeStruct(vec.shape, F32)),
        in_specs=[vm], out_specs=(vm, vm),
        scratch_shapes=[pltpu.SemaphoreType.DMA((N_DEV - 1,)), pltpu.SemaphoreType.DMA((N_DEV - 1,))],
    )(vec)[1]


def _chip_sum(part, recv, c):
    _, _, rh, cols = part.shape
    tr = _tile(rh, 256)

    def body(c_ref, p_ref, r_ref, s_ref, sb_ref):
        s = p_ref[...] + r_ref[...]
        s_ref[...] = s
        sb_ref[...] = s.astype(BF16)

    blk = pl.BlockSpec((None, tr, cols), lambda s, i, c_ref: (s, i, 0))
    return pl.pallas_call(
        body, name="grad_chip_sum",
        out_shape=(jax.ShapeDtypeStruct(recv.shape, F32), jax.ShapeDtypeStruct(recv.shape, BF16)),
        grid_spec=pltpu.PrefetchScalarGridSpec(
            num_scalar_prefetch=1, grid=(N_CHIPS, rh // tr),
            in_specs=[pl.BlockSpec((None, None, tr, cols), lambda s, i, c_ref: (s, c_ref[0], i, 0)), blk],
            out_specs=(blk, blk)),
        compiler_params=_params(("parallel", "parallel")),
    )(c, part, recv)


def _owner_sum(own, recv, me):
    _, rh, cols = own.shape
    tr = _tile(rh, 256)

    def body(me_ref, o_ref, r0, r1, r2, g_ref):
        g_ref[...] = ((o_ref[...] + r0[...].astype(F32)) + r1[...].astype(F32)) + r2[...].astype(F32)

    slot = lambda j: pl.BlockSpec((None, tr, cols), lambda i, me_ref: (j, i, 0))
    return pl.pallas_call(
        body, name="grad_owner_sum", out_shape=jax.ShapeDtypeStruct((rh, cols), F32),
        grid_spec=pltpu.PrefetchScalarGridSpec(
            num_scalar_prefetch=1, grid=(rh // tr,),
            in_specs=[pl.BlockSpec((None, tr, cols), lambda i, me_ref: (me_ref[0], i, 0)), slot(0), slot(1), slot(2)],
            out_specs=pl.BlockSpec((tr, cols), lambda i, me_ref: (i, 0))),
        compiler_params=_params(("parallel",)),
    )(me, own, recv, recv, recv)


def _adamw(w, g, m, v, name):
    rows, cols = w.shape
    tr = _tile(rows, 256) if rows % 8 == 0 else rows
    c1 = 1.0 / (1.0 - ADAM_B1 ** ADAM_STEP)
    c2 = 1.0 / (1.0 - ADAM_B2 ** ADAM_STEP)

    def body(w_ref, g_ref, m_ref, v_ref, d_ref, nm_ref, nv_ref):
        gv = g_ref[...]
        nm = ADAM_B1 * m_ref[...] + (1.0 - ADAM_B1) * gv
        nv = ADAM_B2 * v_ref[...] + (1.0 - ADAM_B2) * (gv * gv)
        d_ref[...] = -ADAM_LR * ((nm * c1) / (jnp.sqrt(nv * c2) + ADAM_EPS) + ADAM_WD * w_ref[...])
        nm_ref[...] = nm
        nv_ref[...] = nv

    blk = pl.BlockSpec((tr, cols), lambda i: (i, 0))
    shape = jax.ShapeDtypeStruct((rows, cols), F32)
    return pl.pallas_call(
        body, name=name, out_shape=(shape, shape, shape), grid=(rows // tr,),
        in_specs=[blk, blk, blk, blk], out_specs=(blk, blk, blk),
        compiler_params=_params(("parallel",)),
    )(w, g, m, v)


def _adamw_halves(w, g_own, g_sib, m, v, c, name):
    rows, cols = w.shape
    rh = rows // 2
    tr = _tile(rh, 256)
    per = rh // tr
    c1 = 1.0 / (1.0 - ADAM_B1 ** ADAM_STEP)
    c2 = 1.0 / (1.0 - ADAM_B2 ** ADAM_STEP)

    def body(c_ref, w_ref, go_ref, gs_ref, m_ref, v_ref, g_ref, d_ref, nm_ref, nv_ref):
        own = pl.program_id(0) // per == c_ref[0]
        gv = jnp.where(own, go_ref[...], gs_ref[...])
        nm = ADAM_B1 * m_ref[...] + (1.0 - ADAM_B1) * gv
        nv = ADAM_B2 * v_ref[...] + (1.0 - ADAM_B2) * (gv * gv)
        g_ref[...] = gv
        d_ref[...] = -ADAM_LR * ((nm * c1) / (jnp.sqrt(nv * c2) + ADAM_EPS) + ADAM_WD * w_ref[...])
        nm_ref[...] = nm
        nv_ref[...] = nv

    blk = pl.BlockSpec((tr, cols), lambda i, c_ref: (i, 0))
    half = pl.BlockSpec((tr, cols), lambda i, c_ref: (i % per, 0))
    shape = jax.ShapeDtypeStruct((rows, cols), F32)
    return pl.pallas_call(
        body, name=name, out_shape=(shape, shape, shape, shape),
        grid_spec=pltpu.PrefetchScalarGridSpec(
            num_scalar_prefetch=1, grid=(rows // tr,),
            in_specs=[blk, half, half, blk, blk], out_specs=(blk, blk, blk, blk)),
        compiler_params=_params(("parallel",)),
    )(c, w, g_own, g_sib, m, v)


SMALL = ("gdn_a_log", "gdn_dt_bias", "gdn_norm_w", "hgrn_lb_logits", "hgrn_norm_w",
         "norm_mix_w", "norm_ffn_w", "norm_final_w")
BIG = ("w_in", "w_out", "w_ff1", "w_ff2")
ORDER = ("w_in", "conv_w", "gdn_a_log", "gdn_dt_bias", "gdn_norm_w", "hgrn_lb_logits", "hgrn_norm_w",
         "w_out", "norm_mix_w", "norm_ffn_w", "w_ff1", "w_ff2", "norm_final_w")


def _pack(pieces):
    flat = jnp.concatenate([p.reshape(-1).astype(F32) for p in pieces])
    rows = -(-flat.shape[0] // (8 * HEAD_DIM)) * 8
    return jnp.pad(flat, (0, rows * HEAD_DIM - flat.shape[0])).reshape(rows, HEAD_DIM)


def _unpack(packed, shapes):
    flat, out, at = packed.reshape(-1), [], 0
    for s in shapes:
        n = 1
        for dim in s:
            n *= dim
        out.append(flat[at:at + n].reshape(s))
        at += n
    return out


def kernel(x, w_in, conv_w, gdn_a_log, gdn_dt_bias, gdn_norm_w, hgrn_lb_logits, hgrn_norm_w, w_out, norm_mix_w, norm_ffn_w, w_ff1, w_ff2, norm_final_w, loss_target, m_w_in, m_conv_w, m_gdn_a_log, m_gdn_dt_bias, m_gdn_norm_w, m_hgrn_lb_logits, m_hgrn_norm_w, m_w_out, m_norm_mix_w, m_norm_ffn_w, m_w_ff1, m_w_ff2, m_norm_final_w, v_w_in, v_conv_w, v_gdn_a_log, v_gdn_dt_bias, v_gdn_norm_w, v_hgrn_lb_logits, v_hgrn_norm_w, v_w_out, v_norm_mix_w, v_norm_ffn_w, v_w_ff1, v_w_ff2, v_norm_final_w):
    w = dict(w_in=w_in, conv_w=conv_w, gdn_a_log=gdn_a_log, gdn_dt_bias=gdn_dt_bias, gdn_norm_w=gdn_norm_w,
             hgrn_lb_logits=hgrn_lb_logits, hgrn_norm_w=hgrn_norm_w, w_out=w_out, norm_mix_w=norm_mix_w,
             norm_ffn_w=norm_ffn_w, w_ff1=w_ff1, w_ff2=w_ff2, norm_final_w=norm_final_w)
    m = dict(w_in=m_w_in, conv_w=m_conv_w, gdn_a_log=m_gdn_a_log, gdn_dt_bias=m_gdn_dt_bias,
             gdn_norm_w=m_gdn_norm_w, hgrn_lb_logits=m_hgrn_lb_logits, hgrn_norm_w=m_hgrn_norm_w,
             w_out=m_w_out, norm_mix_w=m_norm_mix_w, norm_ffn_w=m_norm_ffn_w, w_ff1=m_w_ff1, w_ff2=m_w_ff2,
             norm_final_w=m_norm_final_w)
    v = dict(w_in=v_w_in, conv_w=v_conv_w, gdn_a_log=v_gdn_a_log, gdn_dt_bias=v_gdn_dt_bias,
             gdn_norm_w=v_gdn_norm_w, hgrn_lb_logits=v_hgrn_lb_logits, hgrn_norm_w=v_hgrn_norm_w,
             w_out=v_w_out, norm_mix_w=v_norm_mix_w, norm_ffn_w=v_norm_ffn_w, w_ff1=v_w_ff1, w_ff2=v_w_ff2,
             norm_final_w=v_norm_final_w)
    d = x.shape[-1]
    h = d // (2 * HEAD_DIM)
    my_c = lax.axis_index("c").astype(jnp.int32).reshape(1)
    my_chip = (2 * lax.axis_index("x") + lax.axis_index("y")).astype(jnp.int32)

    shards = [w[n][0].astype(BF16) for n in BIG]
    conv_shard = jnp.pad(conv_w[0], ((0, 8 - CONV_W), (0, 0)))
    gathered = _gather_weights(shards, [conv_shard])
    f_in, f_out, f_ff1, f_ff2, f_conv = (
        lax.dynamic_update_index_in_dim(st, own, my_chip, 0)
        for st, own in zip(gathered, shards + [conv_shard]))
    cols = lambda st: st.transpose(1, 0, 2).reshape(st.shape[1], -1)
    w_main, w_ab = _split_w_in(cols(f_in), h)
    conv_full = cols(f_conv[:, :CONV_W])

    loss, dx, g = _local_step(
        x[0], loss_target[0], w_main, w_ab, conv_full, gdn_a_log[0], gdn_dt_bias[0], gdn_norm_w[0],
        hgrn_lb_logits, hgrn_norm_w[0], f_out.reshape(-1, d), norm_mix_w[0], norm_ffn_w[0],
        f_ff1, f_ff2.reshape(-1, d), norm_final_w)

    def by_shard(name, full):
        if name == "w_in":
            st = full.reshape(full.shape[0], N_CHIPS, -1).transpose(1, 0, 2)
        elif name == "w_ff1":
            st = full
        else:
            st = full.reshape(N_CHIPS, -1, full.shape[1])
        return st.reshape(N_CHIPS, 2, st.shape[1] // 2, st.shape[2])

    g["w_in"] = _merge_w_in(g["w_main"], g["w_ab"], h)
    parts = [by_shard(n, g[n]) for n in BIG]
    from_sibling = _swap_halves(parts)
    sums = [_chip_sum(p, r, my_c) for p, r in zip(parts, from_sibling)]
    from_chips = _scatter_to_owners([s[1] for s in sums])
    halves = [_owner_sum(s[0], r, my_chip.reshape(1)) for s, r in zip(sums, from_chips)]
    from_sibling_half = _send_to_sibling(halves)
    grads, delta, new_m, new_v = {}, {}, {}, {}
    for n, own, sib in zip(BIG, halves, from_sibling_half):
        flat = lambda a: a.reshape(-1, a.shape[-1])
        out = _adamw_halves(flat(w[n]), own, sib, flat(m[n]), flat(v[n]), my_c, "adamw_" + n)
        grads[n], delta[n], new_m[n], new_v[n] = (o.reshape(w[n].shape) for o in out)

    small_shapes = [w[n].shape for n in SMALL] + [conv_full.shape, (1,)]
    total = _all_reduce_small(_pack([g[n] for n in SMALL] + [g["conv_w"], loss[0, :1]]))
    *small_grads, conv_grad, loss_sum = _unpack(total, small_shapes)
    for n, sg in zip(SMALL, small_grads):
        grads[n] = sg
    shard_cols = conv_w.shape[-1]
    grads["conv_w"] = lax.dynamic_slice_in_dim(conv_grad, my_chip * shard_cols, shard_cols, axis=1)[None]

    packed_names = SMALL + ("conv_w",)
    packed = [_pack([t[n] for n in packed_names]) for t in (w, grads, m, v)]
    outs = _adamw(*packed, "adamw_small")
    shapes = [w[n].shape for n in packed_names]
    for res, o in zip((delta, new_m, new_v), outs):
        for n, a in zip(packed_names, _unpack(o, shapes)):
            res[n] = a

    return (loss_sum.reshape(()), dx[None], *[grads[n] for n in ORDER], *[delta[n] for n in ORDER],
            *[new_m[n] for n in ORDER], *[new_v[n] for n in ORDER])
```

```python
import functools

import jax
import jax.numpy as jnp
from jax import lax
from jax.experimental import pallas as pl
from jax.experimental.pallas import tpu as pltpu
from jax.experimental.pallas import tpu_sc as plsc

F32 = jnp.float32
BF16 = jnp.bfloat16

HEAD_DIM = 128
CHUNK = 128
SUB = 16
EXP_CAP = 80.0
NORM_EPS = 1e-6
L2_EPS = 1e-6
CONV_W = 4
VMEM_LIMIT = 56 * 1024 * 1024

ADAM_LR, ADAM_B1, ADAM_B2, ADAM_EPS, ADAM_WD, ADAM_STEP = 1e-3, 0.9, 0.999, 1e-8, 0.01, 10

NN = ((1,), (0,))
NT = ((1,), (1,))
TN = ((0,), (0,))
MESH = pl.DeviceIdType.MESH


def _dot(a, b, dims):
    return lax.dot_general(a.astype(BF16), b.astype(BF16), (dims, ((), ())),
                           preferred_element_type=F32)


def _split(a):
    hi = a.astype(BF16)
    return hi, (a - hi.astype(F32)).astype(BF16)


def _dot3(a, b, dims):
    ah, al = _split(a)
    bh, bl = _split(b)
    d = lambda x, y: lax.dot_general(x, y, (dims, ((), ())), preferred_element_type=F32)
    return d(ah, bh) + (d(ah, bl) + d(al, bh))


def _sigmoid(x):
    return 1.0 / (1.0 + jnp.exp(-x))


def _silu(x):
    return x * _sigmoid(x)


def _dsilu(x):
    s = _sigmoid(x)
    return s * (1.0 + x * (1.0 - s))


def _softplus(x):
    e = jnp.exp(-jnp.abs(x))
    u = 1.0 + e
    log1p = jnp.where(u == 1.0, e, jnp.log(u) * (e / jnp.where(u == 1.0, 1.0, u - 1.0)))
    return jnp.maximum(x, 0.0) + log1p


def _iota(shape, axis):
    return lax.broadcasted_iota(jnp.int32, shape, axis)


def _cumsum_rows(x):
    n = x.shape[0]
    row = _iota(x.shape, 0)
    s = 1
    while s < n:
        x = x + jnp.where(row >= s, pltpu.roll(x, s, 0), 0.0)
        s *= 2
    return x


def _rev_cumsum_rows(x):
    return jnp.sum(x, axis=0, keepdims=True) - _cumsum_rows(x) + x


def _params(sem):
    return pltpu.CompilerParams(dimension_semantics=sem, vmem_limit_bytes=VMEM_LIMIT)


ROW_TILE = 8
HEADS_PER_STEP = 4


def _hps(h):
    return min(HEADS_PER_STEP, h)


def _head_view(ref, hb):
    if len(ref.shape) == 2:
        return ref.at[:, pl.ds(hb * HEAD_DIM, HEAD_DIM)]
    return ref.at[hb]


class _Staged:
    def __init__(self, ref, load):
        self.ref = ref
        self.loaded = ref[...] if load else None
        self.written = None

    def __getitem__(self, idx):
        return self.loaded

    def __setitem__(self, idx, value):
        self.written = value


def _each_head(one_head, n_in):
    def body(*refs):
        @pl.when(pl.program_id(1) == 0)
        def _():
            refs[-1][...] = jnp.zeros_like(refs[-1])

        last = len(refs) - 1
        staged = [[_Staged(_head_view(r, hb), i < n_in or i == last) for i, r in enumerate(refs)]
                  for hb in range(refs[-1].shape[0])]
        running = [one_head(*per_head) for per_head in staged]
        while running:
            for gen in list(running):
                try:
                    next(gen)
                except StopIteration:
                    running.remove(gen)
        for per_head in staged:
            for s in per_head:
                if s.written is not None:
                    s.ref[...] = s.written
    return body


def _tile(n, want):
    t = min(n, want)
    while n % t:
        t //= 2
    return t


def _mm(a, b, mode, out_dtypes, name, epi=None, extras=(), tm=1024, tn=1024, tk=2048,
        b_stacked=False, out_stacked=False):
    if mode == "tn":
        kdim, m = a.shape
    else:
        m, kdim = a.shape
    if b_stacked:
        n = N_CHIPS * b.shape[2] if mode == "nn" else b.shape[1]
        kdim_b = b.shape[1] if mode == "nn" else N_CHIPS * b.shape[2]
        assert kdim_b == kdim
    else:
        n = b.shape[0] if mode == "nt" else b.shape[1]
    per_shard = (n if (mode == "nn" or out_stacked) else kdim) // N_CHIPS
    tm, tn, tk = _tile(m, tm), _tile(n, tn), _tile(kdim, tk)
    if (b_stacked and mode == "nn") or out_stacked:
        tn = _tile(per_shard, tn)
    if b_stacked and mode == "nt":
        tk = _tile(per_shard, tk)
    nk = kdim // tk
    dims = {"nn": NN, "nt": NT, "tn": TN}[mode]
    a_spec = (pl.BlockSpec((tk, tm), lambda i, j, k: (k, i)) if mode == "tn"
              else pl.BlockSpec((tm, tk), lambda i, j, k: (i, k)))
    if b_stacked and mode == "nn":
        per = per_shard // tn
        b_spec = pl.BlockSpec((None, tk, tn), lambda i, j, k: (j // per, k, j % per))
    elif b_stacked:
        per = per_shard // tk
        b_spec = pl.BlockSpec((None, tn, tk), lambda i, j, k: (k // per, j, k % per))
    else:
        b_spec = (pl.BlockSpec((tn, tk), lambda i, j, k: (j, k)) if mode == "nt"
                  else pl.BlockSpec((tk, tn), lambda i, j, k: (k, j)))
    mn_spec = pl.BlockSpec((tm, tn), lambda i, j, k: (i, j))
    if out_stacked:
        per_o = per_shard // tn
        out_spec = pl.BlockSpec((None, tm, tn), lambda i, j, k: (j // per_o, i, j % per_o))
        out_shape = (N_CHIPS, m, per_shard)
    else:
        out_spec, out_shape = mn_spec, (m, n)
    ne, no = len(extras), len(out_dtypes)
    if epi is None:
        epi = lambda acc: (acc,)

    def body(a_ref, b_ref, *rest):
        extra_refs, out_refs = rest[:ne], rest[ne:ne + no]
        part = _dot(a_ref[...], b_ref[...], dims)

        def finish(total):
            outs = epi(total, *[r[...] for r in extra_refs])
            for o_ref, o in zip(out_refs, outs):
                o_ref[...] = o.astype(o_ref.dtype)

        if nk == 1:
            finish(part)
            return
        acc = rest[-1]
        k = pl.program_id(2)

        @pl.when(k == 0)
        def _():
            acc[...] = part

        @pl.when(jnp.logical_and(k > 0, k < nk - 1))
        def _():
            acc[...] += part

        @pl.when(k == nk - 1)
        def _():
            finish(acc[...] + part)

    outs = pl.pallas_call(
        body, name=name,
        out_shape=tuple(jax.ShapeDtypeStruct(out_shape, d) for d in out_dtypes),
        grid=(m // tm, n // tn, nk),
        in_specs=[a_spec, b_spec] + [mn_spec] * ne,
        out_specs=tuple(out_spec for _ in out_dtypes),
        scratch_shapes=[pltpu.VMEM((tm, tn), F32)] if nk > 1 else [],
        compiler_params=_params(("parallel", "parallel", "arbitrary")),
    )(a, b, *extras)
    return outs if no > 1 else outs[0]


ROWS = 256


def _rms_fwd(x, w, name):
    t, d = x.shape
    tr = _tile(t, ROWS)

    def body(x_ref, w_ref, n_ref):
        xv = x_ref[...]
        r = lax.rsqrt(jnp.mean(xv * xv, axis=-1, keepdims=True) + NORM_EPS)
        n_ref[...] = (xv * r * w_ref[...]).astype(n_ref.dtype)

    return pl.pallas_call(
        body, name=name, out_shape=jax.ShapeDtypeStruct((t, d), BF16), grid=(t // tr,),
        in_specs=[pl.BlockSpec((tr, d), lambda i: (i, 0)), pl.BlockSpec((1, d), lambda i: (0, 0))],
        out_specs=pl.BlockSpec((tr, d), lambda i: (i, 0)),
        compiler_params=_params(("parallel",)),
    )(x, w.reshape(1, d))


def _rms_bwd(dn, x, w, dres, name):
    t, d = x.shape
    tr = _tile(t, ROWS)

    def body(dn_ref, x_ref, w_ref, dres_ref, dx_ref, dxb_ref, dw_ref):
        i = pl.program_id(0)
        xv, dnv = x_ref[...], dn_ref[...]
        r = lax.rsqrt(jnp.mean(xv * xv, axis=-1, keepdims=True) + NORM_EPS)
        xh = xv * r
        dxh = dnv * w_ref[...]
        dx = dres_ref[...] + r * (dxh - xh * jnp.mean(dxh * xh, axis=-1, keepdims=True))
        dx_ref[...] = dx
        dxb_ref[...] = dx.astype(BF16)

        @pl.when(i == 0)
        def _():
            dw_ref[...] = jnp.zeros_like(dw_ref)

        dw_ref[...] += jnp.sum(dnv * xh, axis=0, keepdims=True)

    row = pl.BlockSpec((tr, d), lambda i: (i, 0))
    vec = pl.BlockSpec((1, d), lambda i: (0, 0))
    return pl.pallas_call(
        body, name=name,
        out_shape=(jax.ShapeDtypeStruct((t, d), F32), jax.ShapeDtypeStruct((t, d), BF16),
                   jax.ShapeDtypeStruct((1, d), F32)),
        grid=(t // tr,), in_specs=[row, row, vec, row], out_specs=(row, row, vec),
        compiler_params=_params(("arbitrary",)),
    )(dn, x, w.reshape(1, d), dres)


def _loss_head(h, w, target):
    t, d = h.shape
    tr = _tile(t, ROWS)

    def body(h_ref, w_ref, t_ref, loss_ref, dh_ref, dhb_ref, dw_ref):
        i = pl.program_id(0)
        hv, wv = h_ref[...], w_ref[...]
        r = lax.rsqrt(jnp.mean(hv * hv, axis=-1, keepdims=True) + NORM_EPS)
        hh = hv * r
        err = hh * wv - t_ref[...]
        dout = err * (1.0 / d)
        dhh = dout * wv
        dh = r * (dhh - hh * jnp.mean(dhh * hh, axis=-1, keepdims=True))
        dh_ref[...] = dh
        dhb_ref[...] = dh.astype(BF16)

        @pl.when(i == 0)
        def _():
            dw_ref[...] = jnp.zeros_like(dw_ref)
            loss_ref[...] = jnp.zeros_like(loss_ref)

        dw_ref[...] += jnp.sum(dout * hh, axis=0, keepdims=True)
        loss_ref[...] += jnp.full((1, 128), 0.5 / d, F32) * jnp.sum(err * err)

    row = pl.BlockSpec((tr, d), lambda i: (i, 0))
    vec = pl.BlockSpec((1, d), lambda i: (0, 0))
    lspec = pl.BlockSpec((1, 128), lambda i: (0, 0))
    return pl.pallas_call(
        body, name="loss_head",
        out_shape=(jax.ShapeDtypeStruct((1, 128), F32), jax.ShapeDtypeStruct((t, d), F32),
                   jax.ShapeDtypeStruct((t, d), BF16), jax.ShapeDtypeStruct((1, d), F32)),
        grid=(t // tr,), in_specs=[row, vec, row], out_specs=(lspec, row, row, vec),
        compiler_params=_params(("arbitrary",)),
    )(h, w.reshape(1, d), target)


def _inv_unit_lower(a):
    c = a.shape[0]
    eye = (_iota((c, c), 0) == _iota((c, c), 1)).astype(F32)
    x = eye - a
    p = _dot3(a, a, NN)
    yield
    n = 2
    while n < c:
        x = x + _dot3(x, p, NN)
        n *= 2
        if n < c:
            p = _dot3(p, p, NN)
        yield
    return x


def _gdn_chunk(q, k, v, beta, g):
    c = q.shape[0]
    row, col = _iota((c, c), 0), _iota((c, c), 1)
    gc = _cumsum_rows(g)
    diff = gc - gc.T
    dec = jnp.where(row >= col, jnp.exp(jnp.minimum(diff, 0.0)), 0.0)
    dec_s = jnp.where(row > col, dec, 0.0)
    gam = jnp.exp(gc)
    g_last = jnp.sum(g, axis=0, keepdims=True)
    kk = _dot(k, k, NT)
    a = beta * kk * dec_s
    p = _dot(q, k, NT) * dec
    e_end = jnp.exp(g_last - gc)
    return dict(dec=dec, dec_s=dec_s, gam=gam, gam_last=jnp.exp(g_last), e_end=e_end,
                k_end=k * e_end, kk=kk, a=a, p=p)


def _gdn_fwd(q, k, v, beta_bc, g_bc):
    t = q.shape[0]
    h = q.shape[1] // HEAD_DIM
    nc = t // CHUNK

    def body(q_ref, k_ref, v_ref, b_ref, g_ref, o_ref, s_ref, t_ref, state):
        qv, kv, vv, beta = q_ref[...], k_ref[...], v_ref[...], b_ref[...]
        ch = _gdn_chunk(qv, kv, vv, beta, g_ref[...])
        yield
        tm = yield from _inv_unit_lower(ch["a"])
        sol = _dot(tm, jnp.concatenate([beta * vv, beta * ch["gam"] * kv], axis=1), NN)
        yield
        u_v, w = sol[:, :HEAD_DIM], sol[:, HEAD_DIM:]
        s0 = state[...]
        u = u_v - _dot(w, s0, NN)
        yield
        o_ref[...] = _dot(qv * ch["gam"], s0, NN) + _dot(ch["p"], u, NN)
        s_ref[...] = s0
        t_ref[...] = tm
        state[...] = ch["gam_last"] * s0 + _dot(ch["k_end"], u, TN)

    tok = pl.BlockSpec((CHUNK, _hps(h) * HEAD_DIM), lambda hh, c: (c, hh))
    bc = pl.BlockSpec((_hps(h), CHUNK, HEAD_DIM), lambda hh, c: (hh, c, 0))
    mat = pl.BlockSpec((_hps(h), None, HEAD_DIM, HEAD_DIM), lambda hh, c: (hh, c, 0, 0))
    return pl.pallas_call(
        _each_head(body, 5), name="gdn_fwd",
        out_shape=(jax.ShapeDtypeStruct(q.shape, F32),
                   jax.ShapeDtypeStruct((h, nc, HEAD_DIM, HEAD_DIM), F32),
                   jax.ShapeDtypeStruct((h, nc, CHUNK, CHUNK), F32)),
        grid=(h // _hps(h), nc), in_specs=[tok, tok, tok, bc, bc], out_specs=(tok, mat, mat),
        scratch_shapes=[pltpu.VMEM((_hps(h), HEAD_DIM, HEAD_DIM), F32)],
        compiler_params=_params(("parallel", "arbitrary")),
    )(q, k, v, beta_bc, g_bc)


def _gdn_bwd(q, k, v, beta_bc, g_bc, states, invs, do, do_blk=0):
    t = q.shape[0]
    h = q.shape[1] // HEAD_DIM
    nc = t // CHUNK

    def body(q_ref, k_ref, v_ref, b_ref, g_ref, s_ref, t_ref, do_ref,
             dq_ref, dk_ref, dv_ref, db_ref, dg_ref, dstate):
        qv, kv, vv, beta = q_ref[...], k_ref[...], v_ref[...], b_ref[...]
        dov, s0, tm, ds1 = do_ref[...], s_ref[...], t_ref[...], dstate[...]
        ch = _gdn_chunk(qv, kv, vv, beta, g_ref[...])
        yield
        gam, dec, dec_s, kk = ch["gam"], ch["dec"], ch["dec_s"], ch["kk"]
        r_v, r_w = beta * vv, beta * gam * kv
        sol = _dot(tm, jnp.concatenate([r_v, r_w], axis=1), NN)
        yield
        u_v, w = sol[:, :HEAD_DIM], sol[:, HEAD_DIM:]
        u = u_v - _dot(w, s0, NN)
        qg = qv * gam
        yield

        du = _dot(ch["p"], dov, TN) + _dot(ch["k_end"], ds1, NN)
        dp = _dot(dov, u, NT)
        dpd = dp * dec
        dqg = _dot(dov, s0, NT)
        dk_end = _dot(u, ds1, NT)
        yield
        dq = dqg * gam + _dot(dpd, kv, NN)
        dk = _dot(dpd, qv, TN) + dk_end * ch["e_end"]
        dstate[...] = _dot(qg, dov, TN) + ch["gam_last"] * ds1 - _dot(w, du, TN)
        dw = -_dot(du, s0, NT)
        yield
        dr = _dot(tm, jnp.concatenate([du, dw], axis=1), TN)
        yield
        dr_v, dr_w = dr[:, :HEAD_DIM], dr[:, HEAD_DIM:]
        da = -_dot(dr, sol, NT)
        yield
        dkk = da * beta * dec_s
        dk = dk + _dot(dkk, kv, NN) + _dot(dkk, kv, TN) + beta * gam * dr_w
        dbeta = (jnp.sum(da * kk * dec_s, axis=1, keepdims=True)
                 + jnp.sum(dr_v * vv + dr_w * gam * kv, axis=1, keepdims=True))

        pair = dp * ch["p"] + da * ch["a"]
        end = jnp.sum(dk_end * ch["k_end"], axis=1, keepdims=True)
        dgc = (jnp.sum(pair - pair.T, axis=1, keepdims=True)
               + jnp.sum(dqg * qg + dr_w * r_w, axis=1, keepdims=True) - end)
        at_end = jnp.sum(end) + ch["gam_last"] * jnp.sum(s0 * ds1)
        dgc = jnp.broadcast_to(dgc, (CHUNK, HEAD_DIM))
        dgc = dgc + jnp.where(_iota((CHUNK, HEAD_DIM), 0) == CHUNK - 1, at_end, 0.0)
        dq_ref[...] = dq
        dk_ref[...] = dk
        dv_ref[...] = beta * dr_v
        db_ref[...] = jnp.broadcast_to(dbeta, (CHUNK, HEAD_DIM)).T[:ROW_TILE]
        dg_ref[...] = _rev_cumsum_rows(dgc).T[:ROW_TILE]

    rev = lambda c: nc - 1 - c
    tok = pl.BlockSpec((CHUNK, _hps(h) * HEAD_DIM), lambda hh, c: (rev(c), hh))
    bc = pl.BlockSpec((_hps(h), CHUNK, HEAD_DIM), lambda hh, c: (hh, rev(c), 0))
    mat = pl.BlockSpec((_hps(h), None, HEAD_DIM, HEAD_DIM), lambda hh, c: (hh, rev(c), 0, 0))
    tok_shape = jax.ShapeDtypeStruct(q.shape, F32)
    row_shape = jax.ShapeDtypeStruct((h, nc, ROW_TILE, CHUNK), F32)
    rows = pl.BlockSpec((_hps(h), None, ROW_TILE, CHUNK), lambda hh, c: (hh, rev(c), 0, 0))
    return pl.pallas_call(
        _each_head(body, 8), name="gdn_bwd",
        out_shape=(tok_shape, tok_shape, tok_shape, row_shape, row_shape),
        grid=(h // _hps(h), nc),
        in_specs=[tok, tok, tok, bc, bc, mat, mat,
                  pl.BlockSpec((CHUNK, _hps(h) * HEAD_DIM), lambda hh, c: (rev(c), do_blk // _hps(h) + hh))],
        out_specs=(tok, tok, tok, rows, rows),
        scratch_shapes=[pltpu.VMEM((_hps(h), HEAD_DIM, HEAD_DIM), F32)],
        compiler_params=_params(("parallel", "arbitrary")),
    )(q, k, v, beta_bc, g_bc, states, invs, do)


def _hgrn_chunk(q, k, lf):
    c = q.shape[0]
    row = _iota((c, HEAD_DIM), 0)
    b = _cumsum_rows(lf)
    q_subs, k_facs, a_rows = [], [], []
    for x in range(c // SUB):
        b_start = jnp.sum(jnp.where(row < x * SUB, lf, 0.0), axis=0, keepdims=True)
        q_x = (q * jnp.exp(jnp.minimum(b - b_start, 0.0)))[x * SUB:(x + 1) * SUB]
        k_fac = jnp.where(row < (x + 1) * SUB, jnp.exp(jnp.minimum(b_start - b, EXP_CAP)), 0.0)
        q_subs.append(q_x)
        k_facs.append(k_fac)
        a_rows.append(_dot(q_x, k * k_fac, NT))
    a = jnp.concatenate(a_rows, axis=0)
    a = jnp.where(_iota((c, c), 0) >= _iota((c, c), 1), a, 0.0)
    b_last = jnp.sum(lf, axis=0, keepdims=True)
    return dict(b=b, a=a, q_subs=q_subs, k_facs=k_facs, e_b=jnp.exp(b),
                e_end=jnp.exp(b_last - b), e_last=jnp.exp(b_last))


def _hgrn_fwd(q, k, v, lf, v_blk=0):
    t = q.shape[0]
    h = q.shape[1] // HEAD_DIM
    nc = t // CHUNK

    def body(q_ref, k_ref, v_ref, lf_ref, o_ref, s_ref, state):
        qv, kv, vv = q_ref[...], k_ref[...], v_ref[...]
        ch = _hgrn_chunk(qv, kv, lf_ref[...])
        yield
        s0 = state[...]
        o_ref[...] = _dot(qv * ch["e_b"], s0, NT) + _dot(ch["a"], vv, NN)
        s_ref[...] = s0
        state[...] = s0 * ch["e_last"] + _dot(vv, kv * ch["e_end"], TN)

    tok = pl.BlockSpec((CHUNK, _hps(h) * HEAD_DIM), lambda hh, c: (c, hh))
    mat = pl.BlockSpec((_hps(h), None, HEAD_DIM, HEAD_DIM), lambda hh, c: (hh, c, 0, 0))
    return pl.pallas_call(
        _each_head(body, 4), name="hgrn_fwd",
        out_shape=(jax.ShapeDtypeStruct(q.shape, F32),
                   jax.ShapeDtypeStruct((h, nc, HEAD_DIM, HEAD_DIM), F32)),
        grid=(h // _hps(h), nc),
        in_specs=[tok, tok, pl.BlockSpec((CHUNK, _hps(h) * HEAD_DIM), lambda hh, c: (c, v_blk // _hps(h) + hh)), tok],
        out_specs=(tok, mat),
        scratch_shapes=[pltpu.VMEM((_hps(h), HEAD_DIM, HEAD_DIM), F32)],
        compiler_params=_params(("parallel", "arbitrary")),
    )(q, k, v, lf)


def _hgrn_bwd(q, k, v, lf, states, do, v_blk=0, do_blk=0):
    t = q.shape[0]
    nc = t // CHUNK
    h = q.shape[1] // HEAD_DIM

    def body(q_ref, k_ref, v_ref, lf_ref, s_ref, do_ref, dq_ref, dk_ref, dv_ref, dlf_ref, dstate):
        qv, kv, vv, dov, s0 = q_ref[...], k_ref[...], v_ref[...], do_ref[...], s_ref[...]
        ds1 = dstate[...]
        ch = _hgrn_chunk(qv, kv, lf_ref[...])
        yield
        c = CHUNK
        row = _iota((c, HEAD_DIM), 0)
        qh = qv * ch["e_b"]
        k_end = kv * ch["e_end"]
        da = jnp.where(_iota((c, c), 0) >= _iota((c, c), 1), _dot(dov, vv, NT), 0.0)
        dqh = _dot(dov, s0, NN)
        dk_end = _dot(vv, ds1, NN)
        yield
        end = dk_end * k_end
        dk = dk_end * ch["e_end"]
        db = dqh * qh - end + jnp.where(
            row == c - 1, jnp.sum(end + s0 * ch["e_last"] * ds1, axis=0, keepdims=True), 0.0)
        dq_rows, qdq_rows = [], []
        for x in range(c // SUB):
            da_x = da[x * SUB:(x + 1) * SUB]
            k_x = kv * ch["k_facs"][x]
            dq_x = _dot(da_x, k_x, NN)
            dk_x = _dot(da_x, ch["q_subs"][x], TN)
            dq_rows.append(dq_x)
            qdq_rows.append(dq_x * ch["q_subs"][x])
            dk = dk + dk_x * ch["k_facs"][x]
            kdk = dk_x * k_x
            db = db - kdk
            if x > 0:
                at_start = jnp.sum(kdk, axis=0, keepdims=True) - jnp.sum(qdq_rows[x], axis=0, keepdims=True)
                db = db + jnp.where(row == x * SUB - 1, at_start, 0.0)
        yield
        b_start = jnp.zeros((c, HEAD_DIM), F32)
        for x in range(1, c // SUB):
            b_x = jnp.sum(jnp.where(row < x * SUB, lf_ref[...], 0.0), axis=0, keepdims=True)
            b_start = jnp.where(row >= x * SUB, b_x, b_start)
        dq = dqh * ch["e_b"] + jnp.concatenate(dq_rows, axis=0) * jnp.exp(jnp.minimum(ch["b"] - b_start, 0.0))
        db = db + jnp.concatenate(qdq_rows, axis=0)
        dstate[...] = _dot(dov, qh, TN) + ds1 * ch["e_last"]
        dq_ref[...] = dq
        dk_ref[...] = dk
        dv_ref[...] = _dot(ch["a"], dov, TN) + _dot(k_end, ds1, NT)
        dlf_ref[...] = _rev_cumsum_rows(db)

    rev = lambda c: nc - 1 - c
    tok = pl.BlockSpec((CHUNK, _hps(h) * HEAD_DIM), lambda hh, c: (rev(c), hh))
    mat = pl.BlockSpec((_hps(h), None, HEAD_DIM, HEAD_DIM), lambda hh, c: (hh, rev(c), 0, 0))
    tok_shape = jax.ShapeDtypeStruct(q.shape, F32)
    return pl.pallas_call(
        _each_head(body, 6), name="hgrn_bwd",
        out_shape=(tok_shape, tok_shape, tok_shape, tok_shape),
        grid=(h // _hps(h), nc),
        in_specs=[tok, tok, pl.BlockSpec((CHUNK, _hps(h) * HEAD_DIM), lambda hh, c: (rev(c), v_blk // _hps(h) + hh)), tok, mat,
                  pl.BlockSpec((CHUNK, _hps(h) * HEAD_DIM), lambda hh, c: (rev(c), do_blk // _hps(h) + hh))],
        out_specs=(tok, tok, tok, tok),
        scratch_shapes=[pltpu.VMEM((_hps(h), HEAD_DIM, HEAD_DIM), F32)],
        compiler_params=_params(("parallel", "arbitrary")),
    )(q, k, v, lf, states, do)


CONV_ROWS = 256
HALO = 8


def _shift_down(cur, prev, s):
    rt = cur.shape[0]
    head = jnp.concatenate([pltpu.roll(prev, s, 0), jnp.zeros((rt - HALO, cur.shape[1]), F32)], axis=0)
    return jnp.where(_iota(cur.shape, 0) < s, head, pltpu.roll(cur, s, 0))


def _shift_up(cur, nxt, s):
    rt = cur.shape[0]
    tail = jnp.concatenate([jnp.zeros((rt - HALO, cur.shape[1]), F32), pltpu.roll(nxt, HALO - s, 0)], axis=0)
    return jnp.where(_iota(cur.shape, 0) >= rt - s, tail, pltpu.roll(cur, rt - s, 0))


def _tile_with_prev(ref, i, rt):
    r0 = pl.multiple_of(i * rt, rt)
    cur = ref[pl.ds(r0, rt), :]
    prev = ref[pl.ds(pl.multiple_of(jnp.maximum(r0 - HALO, 0), HALO), HALO), :]
    return cur, jnp.where(i > 0, prev, 0.0)


def _tile_with_next(ref, i, rt, n_tiles):
    r0 = pl.multiple_of(i * rt, rt)
    cur = ref[pl.ds(r0, rt), :]
    nxt = ref[pl.ds(pl.multiple_of(jnp.minimum(r0 + rt, (n_tiles - 1) * rt), HALO), HALO), :]
    return cur, jnp.where(i < n_tiles - 1, nxt, 0.0)


def _conv_tile(x_ref, w_ref, i, rt):
    cur, prev = _tile_with_prev(x_ref, i, rt)
    shifted = [_shift_down(cur, prev, CONV_W - 1 - j) for j in range(CONV_W - 1)] + [cur]
    c = shifted[0] * w_ref[pl.ds(0, 1), :]
    for j in range(1, CONV_W):
        c = c + shifted[j] * w_ref[pl.ds(j, 1), :]
    return c, shifted


def _l2n(s):
    return s * lax.rsqrt(jnp.sum(s * s, axis=-1, keepdims=True) + L2_EPS)


def _gdn_prep_fwd(proj, conv_w, h):
    t = proj.shape[0]
    rt = _tile(t, CONV_ROWS)
    nt = t // rt
    scale = HEAD_DIM ** -0.5

    def body(xq, xk, xv, wq, wk, wv, q_ref, k_ref, v_ref):
        def tile(i, carry):
            rows = pl.ds(pl.multiple_of(i * rt, rt), rt)
            q_ref[rows, :] = _l2n(_silu(_conv_tile(xq, wq, i, rt)[0])) * scale
            k_ref[rows, :] = _l2n(_silu(_conv_tile(xk, wk, i, rt)[0]))
            v_ref[rows, :] = _silu(_conv_tile(xv, wv, i, rt)[0])
            return carry

        lax.fori_loop(0, nt, tile, 0)

    col = lambda p: pl.BlockSpec((t, HEAD_DIM), lambda hh: (0, p * h + hh))
    wcol = lambda p: pl.BlockSpec((CONV_W, HEAD_DIM), lambda hh: (0, p * h + hh))
    out = pl.BlockSpec((t, HEAD_DIM), lambda hh: (0, hh))
    shape = jax.ShapeDtypeStruct((t, h * HEAD_DIM), F32)
    return pl.pallas_call(
        body, name="gdn_prep_fwd", out_shape=(shape, shape, shape), grid=(h,),
        in_specs=[col(0), col(1), col(2), wcol(0), wcol(1), wcol(2)], out_specs=(out, out, out),
        compiler_params=_params(("parallel",)),
    )(proj, proj, proj, conv_w, conv_w, conv_w)


def _gdn_prep_bwd(proj, conv_w, dq, dk, dv, h):
    t = proj.shape[0]
    rt = _tile(t, CONV_ROWS)
    nt = t // rt
    scale = HEAD_DIM ** -0.5

    def part(x_ref, w_ref, dy_ref, dx_ref, dw_ref, dc_ref, norm_scale):
        def first(i, dws):
            rows = pl.ds(pl.multiple_of(i * rt, rt), rt)
            c, shifted = _conv_tile(x_ref, w_ref, i, rt)
            ds = dy_ref[rows, :]
            if norm_scale is not None:
                s = _silu(c)
                r = lax.rsqrt(jnp.sum(s * s, axis=-1, keepdims=True) + L2_EPS)
                y = s * r
                dyn = ds * norm_scale
                ds = r * (dyn - y * jnp.sum(dyn * y, axis=-1, keepdims=True))
            dc = ds * _dsilu(c)
            dc_ref[rows, :] = dc
            return tuple(dws[j] + jnp.sum(dc * shifted[j], axis=0, keepdims=True) for j in range(CONV_W))

        dws = lax.fori_loop(0, nt, first, tuple(jnp.zeros((1, HEAD_DIM), F32) for _ in range(CONV_W)))
        for j in range(CONV_W):
            dw_ref[pl.ds(j, 1), :] = dws[j]

        def second(i, carry):
            rows = pl.ds(pl.multiple_of(i * rt, rt), rt)
            cur, nxt = _tile_with_next(dc_ref, i, rt, nt)
            dx = cur * w_ref[pl.ds(CONV_W - 1, 1), :]
            for j in range(CONV_W - 1):
                dx = dx + _shift_up(cur, nxt, CONV_W - 1 - j) * w_ref[pl.ds(j, 1), :]
            dx_ref[rows, :] = dx.astype(dx_ref.dtype)
            return carry

        lax.fori_loop(0, nt, second, 0)

    def body(xq, xk, xv, wq, wk, wv, dq_ref, dk_ref, dv_ref, dxq, dxk, dxv, dwq, dwk, dwv, dc_ref):
        part(xq, wq, dq_ref, dxq, dwq, dc_ref, scale)
        part(xk, wk, dk_ref, dxk, dwk, dc_ref, 1.0)
        part(xv, wv, dv_ref, dxv, dwv, dc_ref, None)

    col = lambda p: pl.BlockSpec((t, HEAD_DIM), lambda hh: (0, p * h + hh))
    wcol = lambda p: pl.BlockSpec((CONV_W, HEAD_DIM), lambda hh: (0, p * h + hh))
    own = pl.BlockSpec((t, HEAD_DIM), lambda hh: (0, hh))
    wown = pl.BlockSpec((CONV_W, HEAD_DIM), lambda hh: (0, hh))
    dx_shape = jax.ShapeDtypeStruct((t, h * HEAD_DIM), BF16)
    dw_shape = jax.ShapeDtypeStruct((CONV_W, h * HEAD_DIM), F32)
    return pl.pallas_call(
        body, name="gdn_prep_bwd",
        out_shape=(dx_shape, dx_shape, dx_shape, dw_shape, dw_shape, dw_shape), grid=(h,),
        in_specs=[col(0), col(1), col(2), wcol(0), wcol(1), wcol(2), own, own, own],
        out_specs=(own, own, own, wown, wown, wown),
        scratch_shapes=[pltpu.VMEM((t, HEAD_DIM), F32)],
        compiler_params=_params(("parallel",)),
    )(proj, proj, proj, conv_w, conv_w, conv_w, dq, dk, dv)


def _gdn_gates_fwd(ab, a_log_row, dt_bias_row):
    t = ab.shape[0]
    tr = _tile(t, 512)

    def body(ab_ref, al_ref, dt_ref, g_ref, b_ref):
        g_ref[...] = -jnp.exp(al_ref[...]) * _softplus(ab_ref[:, :HEAD_DIM] + dt_ref[...])
        b_ref[...] = _sigmoid(ab_ref[:, HEAD_DIM:])

    row = pl.BlockSpec((tr, HEAD_DIM), lambda i: (i, 0))
    vec = pl.BlockSpec((1, HEAD_DIM), lambda i: (0, 0))
    shape = jax.ShapeDtypeStruct((t, HEAD_DIM), F32)
    return pl.pallas_call(
        body, name="gdn_gates_fwd", out_shape=(shape, shape), grid=(t // tr,),
        in_specs=[pl.BlockSpec((tr, 2 * HEAD_DIM), lambda i: (i, 0)), vec, vec], out_specs=(row, row),
        compiler_params=_params(("parallel",)),
    )(ab, a_log_row, dt_bias_row)


def _gdn_gates_bwd(ab, a_log_row, dt_bias_row, dg, dbeta):
    t = ab.shape[0]
    tr = _tile(t, 512)

    def body(ab_ref, al_ref, dt_ref, dg_ref, db_ref, dab_ref, dal_ref, ddt_ref):
        @pl.when(pl.program_id(0) == 0)
        def _():
            dal_ref[...] = jnp.zeros_like(dal_ref)
            ddt_ref[...] = jnp.zeros_like(ddt_ref)

        xa = ab_ref[:, :HEAD_DIM] + dt_ref[...]
        neg_a = -jnp.exp(al_ref[...])
        dgv = dg_ref[...]
        da = dgv * neg_a * _sigmoid(xa)
        beta = _sigmoid(ab_ref[:, HEAD_DIM:])
        dab_ref[:, :HEAD_DIM] = da.astype(BF16)
        dab_ref[:, HEAD_DIM:] = (db_ref[...] * beta * (1.0 - beta)).astype(BF16)
        dal_ref[...] += jnp.sum(dgv * neg_a * _softplus(xa), axis=0, keepdims=True)
        ddt_ref[...] += jnp.sum(da, axis=0, keepdims=True)

    row = pl.BlockSpec((tr, HEAD_DIM), lambda i: (i, 0))
    row2 = pl.BlockSpec((tr, 2 * HEAD_DIM), lambda i: (i, 0))
    vec = pl.BlockSpec((1, HEAD_DIM), lambda i: (0, 0))
    vshape = jax.ShapeDtypeStruct((1, HEAD_DIM), F32)
    return pl.pallas_call(
        body, name="gdn_gates_bwd",
        out_shape=(jax.ShapeDtypeStruct((t, 2 * HEAD_DIM), BF16), vshape, vshape), grid=(t // tr,),
        in_specs=[row2, vec, vec, row, row], out_specs=(row2, vec, vec),
        compiler_params=_params(("arbitrary",)),
    )(ab, a_log_row, dt_bias_row, dg, dbeta)


def _lower_bound(lb_ref):
    return _sigmoid(lb_ref[pl.ds(0, 1), :] - lb_ref[pl.ds(1, 1), :])


def _hgrn_prep_fwd(proj, lb_logits, h, q_blk, f_blk):
    t = proj.shape[0]
    tr = _tile(t, 512)

    def body(xq, xf, lb_ref, q_ref, k_ref, lf_ref):
        lb = _lower_bound(lb_ref)
        s = _sigmoid(xf[...])
        q_ref[...] = _silu(xq[...])
        k_ref[...] = (1.0 - lb) * (1.0 - s)
        lf_ref[...] = jnp.log(lb + (1.0 - lb) * s)

    col = lambda b0: pl.BlockSpec((tr, HEAD_DIM), lambda hh, i: (i, b0 + hh))
    own = pl.BlockSpec((tr, HEAD_DIM), lambda hh, i: (i, hh))
    shape = jax.ShapeDtypeStruct((t, h * HEAD_DIM), F32)
    return pl.pallas_call(
        body, name="hgrn_prep_fwd", out_shape=(shape, shape, shape), grid=(h, t // tr),
        in_specs=[col(q_blk), col(f_blk), pl.BlockSpec((2, HEAD_DIM), lambda hh, i: (0, hh))],
        out_specs=(own, own, own), compiler_params=_params(("parallel", "parallel")),
    )(proj, proj, lb_logits)


def _hgrn_prep_bwd(proj, lb_logits, dq, dk, dlf, h, q_blk, f_blk):
    t = proj.shape[0]
    tr = _tile(t, 512)

    def body(xq, xf, lb_ref, dq_ref, dk_ref, dlf_ref, dxq, dxf, dlb_ref):
        @pl.when(pl.program_id(1) == 0)
        def _():
            dlb_ref[...] = jnp.zeros_like(dlb_ref)

        lb = _lower_bound(lb_ref)
        s = _sigmoid(xf[...])
        e = dlf_ref[...] / (lb + (1.0 - lb) * s) - dk_ref[...]
        dxq[...] = (dq_ref[...] * _dsilu(xq[...])).astype(BF16)
        dxf[...] = (s * (1.0 - s) * (1.0 - lb) * e).astype(BF16)
        d0 = jnp.sum((1.0 - s) * e, axis=0, keepdims=True) * (lb * (1.0 - lb))
        dlb_ref[pl.ds(0, 1), :] += d0
        dlb_ref[pl.ds(1, 1), :] += -d0

    col = lambda b0: pl.BlockSpec((tr, HEAD_DIM), lambda hh, i: (i, b0 + hh))
    own = pl.BlockSpec((tr, HEAD_DIM), lambda hh, i: (i, hh))
    lbs = pl.BlockSpec((2, HEAD_DIM), lambda hh, i: (0, hh))
    shape = jax.ShapeDtypeStruct((t, h * HEAD_DIM), BF16)
    return pl.pallas_call(
        body, name="hgrn_prep_bwd",
        out_shape=(shape, shape, jax.ShapeDtypeStruct((2, h * HEAD_DIM), F32)), grid=(h, t // tr),
        in_specs=[col(q_blk), col(f_blk), lbs, own, own, own], out_specs=(own, own, lbs),
        compiler_params=_params(("parallel", "arbitrary")),
    )(proj, proj, lb_logits, dq, dk, dlf)


def _gate_specs(h, z_blk, g_blk, tr):
    o_a = pl.BlockSpec((tr, HEAD_DIM), lambda hh, i: (i, jnp.minimum(hh, h - 1)))
    o_b = pl.BlockSpec((tr, HEAD_DIM), lambda hh, i: (i, jnp.maximum(hh - h, 0)))
    gate = pl.BlockSpec((tr, HEAD_DIM), lambda hh, i: (i, jnp.where(hh < h, z_blk + hh, g_blk + hh - h)))
    w = pl.BlockSpec((None, 1, HEAD_DIM), lambda hh, i: (hh // h, 0, 0))
    cat = pl.BlockSpec((tr, HEAD_DIM), lambda hh, i: (i, hh))
    return o_a, o_b, gate, w, cat


def _gate_fwd(o_a, o_b, proj, norm_w, h, z_blk, g_blk):
    t = o_a.shape[0]
    tr = _tile(t, 512)

    def body(oa_ref, ob_ref, z_ref, w_ref, y_ref):
        o = jnp.where(pl.program_id(0) < h, oa_ref[...], ob_ref[...])
        r = lax.rsqrt(jnp.mean(o * o, axis=-1, keepdims=True) + NORM_EPS)
        y_ref[...] = (o * r * w_ref[...] * _silu(z_ref[...])).astype(y_ref.dtype)

    sa, sb, sg, sw, cat = _gate_specs(h, z_blk, g_blk, tr)
    return pl.pallas_call(
        body, name="gate_fwd", out_shape=jax.ShapeDtypeStruct((t, 2 * h * HEAD_DIM), BF16),
        grid=(2 * h, t // tr), in_specs=[sa, sb, sg, sw], out_specs=cat,
        compiler_params=_params(("parallel", "parallel")),
    )(o_a, o_b, proj, norm_w)


def _gate_bwd(o_a, o_b, proj, norm_w, dy, h, z_blk, g_blk):
    t = o_a.shape[0]
    tr = _tile(t, 512)

    def body(oa_ref, ob_ref, z_ref, w_ref, dy_ref, do_ref, dz_ref, dw_ref):
        hh = pl.program_id(0)

        @pl.when(jnp.logical_and(hh % h == 0, pl.program_id(1) == 0))
        def _():
            dw_ref[...] = jnp.zeros_like(dw_ref)

        o = jnp.where(hh < h, oa_ref[...], ob_ref[...])
        z, w, dyv = z_ref[...], w_ref[...], dy_ref[...]
        r = lax.rsqrt(jnp.mean(o * o, axis=-1, keepdims=True) + NORM_EPS)
        oh = o * r
        dz_ref[...] = (dyv * oh * w * _dsilu(z)).astype(dz_ref.dtype)
        dn = dyv * _silu(z)
        doh = dn * w
        do_ref[...] = r * (doh - oh * jnp.mean(doh * oh, axis=-1, keepdims=True))
        dw_ref[...] += jnp.sum(dn * oh, axis=0, keepdims=True)

    sa, sb, sg, sw, cat = _gate_specs(h, z_blk, g_blk, tr)
    width = 2 * h * HEAD_DIM
    return pl.pallas_call(
        body, name="gate_bwd",
        out_shape=(jax.ShapeDtypeStruct((t, width), F32), jax.ShapeDtypeStruct((t, width), BF16),
                   jax.ShapeDtypeStruct((2, 1, HEAD_DIM), F32)),
        grid=(2 * h, t // tr), in_specs=[sa, sb, sg, sw, cat], out_specs=(cat, cat, sw),
        compiler_params=_params(("arbitrary", "arbitrary")),
    )(o_a, o_b, proj, norm_w, dy)


def _lane_row(vec):
    return jnp.pad(vec.reshape(1, -1), ((0, 0), (0, HEAD_DIM - vec.shape[-1])))


def _add_epi(acc, res):
    return (acc + res,)


def _split_w_in(w_in, h):
    gw = h * HEAD_DIM
    main = jnp.concatenate([w_in[:, :4 * gw], w_in[:, 4 * gw + 2 * h:]], axis=1)
    pad = jnp.zeros((w_in.shape[0], HEAD_DIM - h), w_in.dtype)
    ab = jnp.concatenate([w_in[:, 4 * gw:4 * gw + h], pad, w_in[:, 4 * gw + h:4 * gw + 2 * h], pad], axis=1)
    return main, ab


def _merge_w_in(main, ab, h):
    gw = h * HEAD_DIM
    return jnp.concatenate([main[:, :4 * gw], ab[:, :h], ab[:, HEAD_DIM:HEAD_DIM + h], main[:, 4 * gw:]], axis=1)


def _local_step(x, target, w_main, w_ab, conv_w, a_log, dt_bias, gdn_norm_w, lb_logits, hgrn_norm_w,
                w_out, norm_mix_w, norm_ffn_w, w_ff1, w_ff2, norm_final_w, reducer=None):
    t, d = x.shape
    h = d // (2 * HEAD_DIM)
    gw = h * HEAD_DIM
    k_blk, v_blk, z_blk, qb_blk, fb_blk, ib_blk, gb_blk = (i * h for i in range(1, 8))
    del k_blk, v_blk

    n1 = _rms_fwd(x, norm_mix_w, "rms_mix")
    proj = _mm(n1, w_main, "nn", (F32,), "in_proj")
    ab = _mm(n1, w_ab, "nn", (F32,), "in_proj_ab")

    q, k, v = _gdn_prep_fwd(proj, conv_w, h)
    a_log_row, dt_row = _lane_row(a_log), _lane_row(dt_bias)
    g_tm, beta_tm = _gdn_gates_fwd(ab, a_log_row, dt_row)
    to_heads = lambda a: jnp.broadcast_to(a[:, :h].T[:, :, None], (h, t, HEAD_DIM))
    g_bc, beta_bc = to_heads(g_tm), to_heads(beta_tm)
    o_a, st_a, inv_a = _gdn_fwd(q, k, v, beta_bc, g_bc)

    qh, kh, lf = _hgrn_prep_fwd(proj, lb_logits, h, qb_blk, fb_blk)
    o_b, st_b = _hgrn_fwd(qh, kh, proj, lf, v_blk=ib_blk)

    gate_w = jnp.stack([gdn_norm_w.reshape(1, HEAD_DIM), hgrn_norm_w.reshape(1, HEAD_DIM)])
    y = _gate_fwd(o_a, o_b, proj, gate_w, h, z_blk, gb_blk)
    h1 = _mm(y, w_out, "nn", (F32,), "out_proj", epi=_add_epi, extras=(x,))
    n2 = _rms_fwd(h1, norm_ffn_w, "rms_ffn")
    act, r = _mm(n2, w_ff1, "nn", (F32, BF16), "ff1", b_stacked=True,
                 epi=lambda acc: (acc, jnp.square(jnp.maximum(acc, 0.0))))
    h2 = _mm(r, w_ff2, "nn", (F32,), "ff2", epi=_add_epi, extras=(h1,))
    loss, dh2, dh2_b, d_norm_final = _loss_head(h2, norm_final_w, target)

    da = _mm(dh2_b, w_ff2, "nt", (BF16,), "ff2_dx",
             epi=lambda acc, a: (acc * (2.0 * jnp.maximum(a, 0.0)),), extras=(act,))
    pending = []

    def step(anchor, name=None, full=None):
        if reducer is not None:
            pending.extend(reducer.step(name, full, anchor))

    def after_step(value):
        if not pending:
            return value
        value = lax.optimization_barrier((value, *pending))[0]
        pending.clear()
        return value

    d_ff2 = _mm(r, dh2_b, "tn", (F32,), "ff2_dw")
    step(None, "w_ff2", d_ff2)
    dn2 = _mm(da, w_ff1, "nt", (F32,), "ff1_dx", b_stacked=True)
    d_ff1 = _mm(n2, da, "tn", (F32,), "ff1_dw", out_stacked=True)
    step(d_ff1, "w_ff1", d_ff1)
    dh1, dh1_b, d_norm_ffn = _rms_bwd(after_step(dn2), h1, norm_ffn_w, dh2, "rms_ffn_bwd")
    dy = _mm(dh1_b, w_out, "nt", (F32,), "out_proj_dx")
    d_out = _mm(y, dh1_b, "tn", (F32,), "out_proj_dw")
    step(d_out, "w_out", d_out)

    do, dgate, d_gate_w = _gate_bwd(o_a, o_b, proj, gate_w, after_step(dy), h, z_blk, gb_blk)
    step(do)
    dq, dk, dv, dbeta_bc, dg_bc = _gdn_bwd(q, k, v, beta_bc, g_bc, st_a, inv_a, after_step(do), do_blk=0)
    step(dq)
    dxq, dxk, dxv, dcq, dck, dcv = _gdn_prep_bwd(proj, conv_w, after_step(dq), dk, dv, h)
    step(dxq)
    from_heads = lambda a: jnp.pad(a[:, :, 0, :].reshape(h, t).T, ((0, 0), (0, HEAD_DIM - h)))
    dab, d_a_log, d_dt_bias = _gdn_gates_bwd(ab, a_log_row, dt_row, from_heads(dg_bc), from_heads(dbeta_bc))
    dqh, dkh, dvh, dlf = _hgrn_bwd(after_step(qh), kh, proj, lf, st_b, do, v_blk=ib_blk, do_blk=h)
    step(dqh)
    dxqb, dxfb, d_lb = _hgrn_prep_bwd(proj, lb_logits, dqh, dkh, dlf, h, qb_blk, fb_blk)

    dproj = jnp.concatenate([after_step(dxq), dxk, dxv, dgate[:, :gw], dxqb, dxfb, dvh.astype(BF16), dgate[:, gw:]],
                            axis=1)
    d_main = _mm(n1, dproj, "tn", (F32,), "in_proj_dw")
    d_ab = _mm(n1, dab, "tn", (F32,), "in_proj_ab_dw")
    step(d_main, "w_in", _merge_w_in(d_main, d_ab, h))
    dn1_ab = _mm(after_step(dab), w_ab, "nt", (F32,), "in_proj_ab_dx")
    step(dn1_ab)
    dn1 = _mm(after_step(dproj), w_main, "nt", (F32,), "in_proj_dx", epi=_add_epi, extras=(dn1_ab,))
    step(dn1)
    dx, _, d_norm_mix = _rms_bwd(after_step(dn1), x, norm_mix_w, dh1, "rms_mix_bwd")
    step(dx)

    grads = dict(
        w_main=d_main, w_ab=d_ab, conv_w=jnp.concatenate([dcq, dck, dcv], axis=1),
        gdn_a_log=d_a_log[:, :h], gdn_dt_bias=d_dt_bias[:, :h], gdn_norm_w=d_gate_w[0],
        hgrn_lb_logits=d_lb, hgrn_norm_w=d_gate_w[1], w_out=d_out, norm_mix_w=d_norm_mix,
        norm_ffn_w=d_norm_ffn, w_ff1=d_ff1, w_ff2=d_ff2, norm_final_w=d_norm_final)
    return loss, dx, grads


N_CHIPS = 4
ANY = pl.BlockSpec(memory_space=pl.ANY)


def _place():
    x, y, c = lax.axis_index("x"), lax.axis_index("y"), lax.axis_index("c")
    chips = [(1 - x, y), (x, 1 - y), (1 - x, 1 - y)]
    return x, y, c, chips


def _remote(src, dst, send_sems, recv_sems, k, to):
    return pltpu.make_async_remote_copy(src_ref=src, dst_ref=dst, send_sem=send_sems.at[k],
                                        recv_sem=recv_sems.at[k], device_id=to, device_id_type=MESH)


def _to_sibling(x, y, c, chips):
    return [(x, y, 1 - c)]


def _to_same_core_of_chips(x, y, c, chips):
    return [(*chip, c) for chip in chips]


def _to_all_gather_peers(x, y, c, chips):
    return _to_sibling(x, y, c, chips) + _to_same_core_of_chips(x, y, c, chips)


SIBLING_EXCHANGE = (1, _to_sibling)
CHIP_EXCHANGE = (2, _to_same_core_of_chips)
GATHER_EXCHANGE = (3, _to_all_gather_peers)


def _launch(body, name, out_shapes, arrays, sem_counts, sequencer=None, after=()):
    n, n_after = len(arrays), len(after)
    sems = [pltpu.SemaphoreType.DMA((k,)) for k in sem_counts]
    strip = lambda refs: refs[:n] + refs[n + n_after:]
    if sequencer is None:
        return pl.pallas_call(
            lambda *refs: body(*strip(refs)), name=name, out_shape=tuple(out_shapes),
            in_specs=[ANY] * (n + n_after), out_specs=tuple(ANY for _ in out_shapes), scratch_shapes=sems,
        )(*arrays, *after)
    collective_id, peers = sequencer

    def sequencer_body(*refs):
        x, y, c, chips = _place()
        barrier = pltpu.get_barrier_semaphore()
        targets = peers(x, y, c, chips)
        for target in targets:
            pl.semaphore_signal(barrier, inc=1, device_id=target, device_id_type=MESH)
        pl.semaphore_wait(barrier, len(targets))
        body(*strip(refs))

    return pl.kernel(
        sequencer_body, name=name, out_type=tuple(out_shapes),
        mesh=plsc.ScalarSubcoreMesh(axis_name="sequencer", num_cores=1), scratch_types=tuple(sems),
        compiler_params=pltpu.CompilerParams(collective_id=collective_id),
    )(*arrays, *after)


def _gather_weights(big, small, name, sequencer=None, after=()):
    nb, ns = len(big), len(small)
    n_sem = 6 * nb + 3 * ns

    def body(*refs):
        ins, outs = refs[:nb + ns], refs[nb + ns:2 * (nb + ns)]
        send_sems, recv_sems = refs[2 * (nb + ns):]
        x, y, c, chips = _place()
        me, sibling = 2 * x + y, (x, y, 1 - c)

        def half(a, chip, hc):
            rh = big[a].shape[0] // 2
            return outs[a].at[2 * chip[0] + chip[1], pl.ds(hc * rh, rh), :]

        first, passed = [], []
        for a in range(nb):
            rh = big[a].shape[0] // 2
            for j, chip in enumerate(chips):
                first.append(_remote(ins[a].at[pl.ds(c * rh, rh), :], half(a, (x, y), c),
                                     send_sems, recv_sems, 6 * a + j, (*chip, c)))
        for s in range(ns):
            for j, chip in enumerate(chips):
                first.append(_remote(ins[nb + s], outs[nb + s].at[me], send_sems, recv_sems,
                                     6 * nb + 3 * s + j, (*chip, c)))
        for cp in first:
            cp.start()
        for a in range(nb):
            for j, chip in enumerate(chips):
                _remote(half(a, chip, c), half(a, chip, c), send_sems, recv_sems, 6 * a + j, (*chip, c)).wait_recv()
                fwd = _remote(half(a, chip, c), half(a, chip, c), send_sems, recv_sems, 6 * a + 3 + j, sibling)
                fwd.start()
                passed.append(fwd)
        for s in range(ns):
            for j, chip in enumerate(chips):
                dst = outs[nb + s].at[2 * chip[0] + chip[1]]
                _remote(dst, dst, send_sems, recv_sems, 6 * nb + 3 * s + j, (*chip, c)).wait_recv()
        for a in range(nb):
            for j, chip in enumerate(chips):
                _remote(half(a, chip, 1 - c), half(a, chip, 1 - c), send_sems, recv_sems,
                        6 * a + 3 + j, sibling).wait_recv()
        for cp in first + passed:
            cp.wait_send()

    arrays = list(big) + list(small)
    out_shapes = [jax.ShapeDtypeStruct((N_CHIPS,) + a.shape, a.dtype) for a in arrays]
    return _launch(body, name, out_shapes, arrays, (n_sem, n_sem), sequencer, after)


def _swap_halves(parts, name, sequencer=None):
    n = len(parts)

    def body(*refs):
        ins, outs = refs[:n], refs[n:2 * n]
        send_sems, recv_sems = refs[2 * n:]
        x, y, c, _ = _place()
        copies = [_remote(ins[a].at[s, 1 - c], outs[a].at[s], send_sems, recv_sems, N_CHIPS * a + s, (x, y, 1 - c))
                  for a in range(n) for s in range(N_CHIPS)]
        for cp in copies:
            cp.start()
        for cp in copies:
            cp.wait()

    out_shapes = [jax.ShapeDtypeStruct((N_CHIPS,) + p.shape[2:], p.dtype) for p in parts]
    return _launch(body, name, out_shapes, parts, (N_CHIPS * n, N_CHIPS * n), sequencer)


def _scatter_to_owners(parts, name, sequencer=None):
    n = len(parts)

    def body(*refs):
        ins, outs = refs[:n], refs[n:2 * n]
        send_sems, recv_sems = refs[2 * n:]
        x, y, c, chips = _place()
        copies = [_remote(ins[a].at[2 * chip[0] + chip[1]], outs[a].at[j], send_sems, recv_sems,
                          3 * a + j, (*chip, c))
                  for a in range(n) for j, chip in enumerate(chips)]
        for cp in copies:
            cp.start()
        for cp in copies:
            cp.wait()

    out_shapes = [jax.ShapeDtypeStruct((3,) + p.shape[1:], p.dtype) for p in parts]
    return _launch(body, name, out_shapes, parts, (3 * n, 3 * n), sequencer)


def _send_to_sibling(halves, name, sequencer=None):
    n = len(halves)

    def body(*refs):
        ins, outs = refs[:n], refs[n:2 * n]
        send_sems, recv_sems = refs[2 * n:]
        x, y, c, _ = _place()
        copies = [_remote(ins[a], outs[a], send_sems, recv_sems, a, (x, y, 1 - c)) for a in range(n)]
        for cp in copies:
            cp.start()
        for cp in copies:
            cp.wait()

    out_shapes = [jax.ShapeDtypeStruct(p.shape, p.dtype) for p in halves]
    return _launch(body, name, out_shapes, halves, (n, n), sequencer)


N_DEV = 8


def _all_reduce_small(vec):
    def body(v_ref, gathered, total, send_sems, recv_sems):
        x, y, c, _ = _place()
        me = 4 * x + 2 * y + c
        gathered[me] = v_ref[...]
        copies = []
        for k in range(1, N_DEV):
            px = 1 - x if k & 4 else x
            py = 1 - y if k & 2 else y
            pc = 1 - c if k & 1 else c
            copies.append(_remote(v_ref, gathered.at[me], send_sems, recv_sems, k - 1, (px, py, pc)))
        for cp in copies:
            cp.start()
        for k, cp in enumerate(copies):
            cp.wait_send()
        for k in range(1, N_DEV):
            px = 1 - x if k & 4 else x
            py = 1 - y if k & 2 else y
            pc = 1 - c if k & 1 else c
            src = gathered.at[4 * px + 2 * py + pc]
            _remote(src, src, send_sems, recv_sems, k - 1, (px, py, pc)).wait_recv()
        acc = gathered[0]
        for dev in range(1, N_DEV):
            acc = acc + gathered[dev]
        total[...] = acc

    vm = pl.BlockSpec(memory_space=pltpu.VMEM)
    return pl.pallas_call(
        body, name="all_reduce_small",
        out_shape=(jax.ShapeDtypeStruct((N_DEV,) + vec.shape, F32), jax.ShapeDtypeStruct(vec.shape, F32)),
        in_specs=[vm], out_specs=(vm, vm),
        scratch_shapes=[pltpu.SemaphoreType.DMA((N_DEV - 1,)), pltpu.SemaphoreType.DMA((N_DEV - 1,))],
    )(vec)[1]


def _chip_sum(part, recv, c):
    _, _, rh, cols = part.shape
    tr = _tile(rh, 256)

    def body(c_ref, p_ref, r_ref, s_ref, sb_ref):
        s = p_ref[...] + r_ref[...]
        s_ref[...] = s
        sb_ref[...] = s.astype(BF16)

    blk = pl.BlockSpec((None, tr, cols), lambda s, i, c_ref: (s, i, 0))
    return pl.pallas_call(
        body, name="grad_chip_sum",
        out_shape=(jax.ShapeDtypeStruct(recv.shape, F32), jax.ShapeDtypeStruct(recv.shape, BF16)),
        grid_spec=pltpu.PrefetchScalarGridSpec(
            num_scalar_prefetch=1, grid=(N_CHIPS, rh // tr),
            in_specs=[pl.BlockSpec((None, None, tr, cols), lambda s, i, c_ref: (s, c_ref[0], i, 0)), blk],
            out_specs=(blk, blk)),
        compiler_params=_params(("parallel", "parallel")),
    )(c, part, recv)


def _owner_sum(own, recv, me):
    _, rh, cols = own.shape
    tr = _tile(rh, 256)

    def body(me_ref, o_ref, r0, r1, r2, g_ref):
        g_ref[...] = ((o_ref[...] + r0[...].astype(F32)) + r1[...].astype(F32)) + r2[...].astype(F32)

    slot = lambda j: pl.BlockSpec((None, tr, cols), lambda i, me_ref: (j, i, 0))
    return pl.pallas_call(
        body, name="grad_owner_sum", out_shape=jax.ShapeDtypeStruct((rh, cols), F32),
        grid_spec=pltpu.PrefetchScalarGridSpec(
            num_scalar_prefetch=1, grid=(rh // tr,),
            in_specs=[pl.BlockSpec((None, tr, cols), lambda i, me_ref: (me_ref[0], i, 0)), slot(0), slot(1), slot(2)],
            out_specs=pl.BlockSpec((tr, cols), lambda i, me_ref: (i, 0))),
        compiler_params=_params(("parallel",)),
    )(me, own, recv, recv, recv)


def _adamw(w, g, m, v, name):
    rows, cols = w.shape
    tr = _tile(rows, 256) if rows % 8 == 0 else rows
    c1 = 1.0 / (1.0 - ADAM_B1 ** ADAM_STEP)
    c2 = 1.0 / (1.0 - ADAM_B2 ** ADAM_STEP)

    def body(w_ref, g_ref, m_ref, v_ref, d_ref, nm_ref, nv_ref):
        gv = g_ref[...]
        nm = ADAM_B1 * m_ref[...] + (1.0 - ADAM_B1) * gv
        nv = ADAM_B2 * v_ref[...] + (1.0 - ADAM_B2) * (gv * gv)
        d_ref[...] = -ADAM_LR * ((nm * c1) / (jnp.sqrt(nv * c2) + ADAM_EPS) + ADAM_WD * w_ref[...])
        nm_ref[...] = nm
        nv_ref[...] = nv

    blk = pl.BlockSpec((tr, cols), lambda i: (i, 0))
    shape = jax.ShapeDtypeStruct((rows, cols), F32)
    return pl.pallas_call(
        body, name=name, out_shape=(shape, shape, shape), grid=(rows // tr,),
        in_specs=[blk, blk, blk, blk], out_specs=(blk, blk, blk),
        compiler_params=_params(("parallel",)),
    )(w, g, m, v)


def _adamw_halves(w, g_own, g_sib, m, v, c, name):
    _, rows, cols = w.shape
    rh = rows // 2
    tr = _tile(rh, 256)
    per = rh // tr
    c1 = 1.0 / (1.0 - ADAM_B1 ** ADAM_STEP)
    c2 = 1.0 / (1.0 - ADAM_B2 ** ADAM_STEP)

    def body(c_ref, w_ref, go_ref, gs_ref, m_ref, v_ref, g_ref, d_ref, nm_ref, nv_ref):
        own = pl.program_id(0) // per == c_ref[0]
        gv = jnp.where(own, go_ref[...], gs_ref[...])
        nm = ADAM_B1 * m_ref[...] + (1.0 - ADAM_B1) * gv
        nv = ADAM_B2 * v_ref[...] + (1.0 - ADAM_B2) * (gv * gv)
        g_ref[...] = gv
        d_ref[...] = -ADAM_LR * ((nm * c1) / (jnp.sqrt(nv * c2) + ADAM_EPS) + ADAM_WD * w_ref[...])
        nm_ref[...] = nm
        nv_ref[...] = nv

    blk = pl.BlockSpec((None, tr, cols), lambda i, c_ref: (0, i, 0))
    half = pl.BlockSpec((tr, cols), lambda i, c_ref: (i % per, 0))
    shape = jax.ShapeDtypeStruct((1, rows, cols), F32)
    return pl.pallas_call(
        body, name=name, out_shape=(shape, shape, shape, shape),
        grid_spec=pltpu.PrefetchScalarGridSpec(
            num_scalar_prefetch=1, grid=(rows // tr,),
            in_specs=[blk, half, half, blk, blk], out_specs=(blk, blk, blk, blk)),
        compiler_params=_params(("parallel",)),
    )(c, w, g_own, g_sib, m, v)


def _by_shard(name, full):
    if name == "w_in":
        st = full.reshape(full.shape[0], N_CHIPS, -1).transpose(1, 0, 2)
    elif name == "w_ff1":
        st = full
    else:
        st = full.reshape(N_CHIPS, -1, full.shape[1])
    return st.reshape(N_CHIPS, 2, st.shape[1] // 2, st.shape[2])


class _GradReducer:
    def __init__(self, w, m, v, my_c, my_chip):
        self.w, self.m, self.v, self.my_c, self.my_chip = w, m, v, my_c, my_chip
        self.in_flight = []
        self.computed = []
        self.anchor = None
        self.done = {}

    def step(self, name=None, full=None, anchor=None):
        stages, self.in_flight, self.computed, self.anchor = self.in_flight, [], [], anchor
        for stage in [s for s in stages if not getattr(s, "long", False)]:
            self._advance(stage)
        if name is not None:
            self.in_flight.append(self._swap(name, _by_shard(name, full)))
        for stage in [s for s in stages if getattr(s, "long", False)]:
            self._advance(stage)
        return self.computed

    def _held(self, value):
        if self.anchor is None:
            return value
        return lax.optimization_barrier((value, self.anchor))[0]

    def _advance(self, stage):
        nxt = stage()
        if nxt is not None:
            self.in_flight.append(nxt)

    def finish(self):
        while self.in_flight:
            self.step()
        return self.done

    def _swap(self, name, part):
        got, = _swap_halves([part], "grad_swap_" + name, SIBLING_EXCHANGE)

        def scatter():
            total, total_bf16 = _chip_sum(part, self._held(got), self.my_c)
            self.computed.append(total_bf16)
            recv, = _scatter_to_owners([total_bf16], "grad_scatter_" + name, CHIP_EXCHANGE)

            def send():
                half = _owner_sum(total, self._held(recv), self.my_chip.reshape(1))
                self.computed.append(half)
                sib, = _send_to_sibling([half], "grad_send_" + name, SIBLING_EXCHANGE)

                def update():
                    self.done[name] = _adamw_halves(self.w[name], half, self._held(sib), self.m[name], self.v[name],
                                                    self.my_c, "adamw_" + name)
                    self.computed.append(self.done[name][0])
                return update
            return lambda: send
        scatter.long = True
        return scatter


SMALL = ("gdn_a_log", "gdn_dt_bias", "gdn_norm_w", "hgrn_lb_logits", "hgrn_norm_w",
         "norm_mix_w", "norm_ffn_w", "norm_final_w")
BIG = ("w_in", "w_out", "w_ff1", "w_ff2")
ORDER = ("w_in", "conv_w", "gdn_a_log", "gdn_dt_bias", "gdn_norm_w", "hgrn_lb_logits", "hgrn_norm_w",
         "w_out", "norm_mix_w", "norm_ffn_w", "w_ff1", "w_ff2", "norm_final_w")


def _pack(pieces):
    flat = jnp.concatenate([p.reshape(-1).astype(F32) for p in pieces])
    rows = -(-flat.shape[0] // (8 * HEAD_DIM)) * 8
    return jnp.pad(flat, (0, rows * HEAD_DIM - flat.shape[0])).reshape(rows, HEAD_DIM)


def _unpack(packed, shapes):
    flat, out, at = packed.reshape(-1), [], 0
    for s in shapes:
        n = 1
        for dim in s:
            n *= dim
        out.append(flat[at:at + n].reshape(s))
        at += n
    return out


def kernel(x, w_in, conv_w, gdn_a_log, gdn_dt_bias, gdn_norm_w, hgrn_lb_logits, hgrn_norm_w, w_out, norm_mix_w, norm_ffn_w, w_ff1, w_ff2, norm_final_w, loss_target, m_w_in, m_conv_w, m_gdn_a_log, m_gdn_dt_bias, m_gdn_norm_w, m_hgrn_lb_logits, m_hgrn_norm_w, m_w_out, m_norm_mix_w, m_norm_ffn_w, m_w_ff1, m_w_ff2, m_norm_final_w, v_w_in, v_conv_w, v_gdn_a_log, v_gdn_dt_bias, v_gdn_norm_w, v_hgrn_lb_logits, v_hgrn_norm_w, v_w_out, v_norm_mix_w, v_norm_ffn_w, v_w_ff1, v_w_ff2, v_norm_final_w):
    w = dict(w_in=w_in, conv_w=conv_w, gdn_a_log=gdn_a_log, gdn_dt_bias=gdn_dt_bias, gdn_norm_w=gdn_norm_w,
             hgrn_lb_logits=hgrn_lb_logits, hgrn_norm_w=hgrn_norm_w, w_out=w_out, norm_mix_w=norm_mix_w,
             norm_ffn_w=norm_ffn_w, w_ff1=w_ff1, w_ff2=w_ff2, norm_final_w=norm_final_w)
    m = dict(w_in=m_w_in, conv_w=m_conv_w, gdn_a_log=m_gdn_a_log, gdn_dt_bias=m_gdn_dt_bias,
             gdn_norm_w=m_gdn_norm_w, hgrn_lb_logits=m_hgrn_lb_logits, hgrn_norm_w=m_hgrn_norm_w,
             w_out=m_w_out, norm_mix_w=m_norm_mix_w, norm_ffn_w=m_norm_ffn_w, w_ff1=m_w_ff1, w_ff2=m_w_ff2,
             norm_final_w=m_norm_final_w)
    v = dict(w_in=v_w_in, conv_w=v_conv_w, gdn_a_log=v_gdn_a_log, gdn_dt_bias=v_gdn_dt_bias,
             gdn_norm_w=v_gdn_norm_w, hgrn_lb_logits=v_hgrn_lb_logits, hgrn_norm_w=v_hgrn_norm_w,
             w_out=v_w_out, norm_mix_w=v_norm_mix_w, norm_ffn_w=v_norm_ffn_w, w_ff1=v_w_ff1, w_ff2=v_w_ff2,
             norm_final_w=v_norm_final_w)
    d = x.shape[-1]
    h = d // (2 * HEAD_DIM)
    my_c = lax.axis_index("c").astype(jnp.int32).reshape(1)
    my_chip = (2 * lax.axis_index("x") + lax.axis_index("y")).astype(jnp.int32)

    shards = [w[n][0].astype(BF16) for n in BIG]
    conv_shard = jnp.pad(conv_w[0], ((0, 8 - CONV_W), (0, 0)))
    first = _gather_weights(shards[:1], [conv_shard], "gather_in_proj")
    own_slot = lambda st, own: lax.dynamic_update_index_in_dim(st, own, my_chip, 0)
    f_in, f_conv = own_slot(first[0], shards[0]), own_slot(first[1], conv_shard)
    cols = lambda st: st.transpose(1, 0, 2).reshape(st.shape[1], -1)
    w_main, w_ab = _split_w_in(cols(f_in), h)
    conv_full = cols(f_conv[:, :CONV_W])
    rest = _gather_weights(shards[1:], [], "gather_rest", GATHER_EXCHANGE, after=[w_ab])
    f_out, f_ff1, f_ff2 = (own_slot(st, own) for st, own in zip(rest, shards[1:]))

    reducer = _GradReducer(w, m, v, my_c, my_chip)
    loss, dx, g = _local_step(
        x[0], loss_target[0], w_main, w_ab, conv_full, gdn_a_log[0], gdn_dt_bias[0], gdn_norm_w[0],
        hgrn_lb_logits, hgrn_norm_w[0], f_out.reshape(-1, d), norm_mix_w[0], norm_ffn_w[0],
        f_ff1, f_ff2.reshape(-1, d), norm_final_w, reducer)

    grads, delta, new_m, new_v = {}, {}, {}, {}
    for n, out in reducer.finish().items():
        grads[n], delta[n], new_m[n], new_v[n] = out

    small_shapes = [w[n].shape for n in SMALL] + [conv_full.shape, (1,)]
    total = _all_reduce_small(_pack([g[n] for n in SMALL] + [g["conv_w"], loss[0, :1]]))
    *small_grads, conv_grad, loss_sum = _unpack(total, small_shapes)
    for n, sg in zip(SMALL, small_grads):
        grads[n] = sg
    shard_cols = conv_w.shape[-1]
    grads["conv_w"] = lax.dynamic_slice_in_dim(conv_grad, my_chip * shard_cols, shard_cols, axis=1)[None]

    packed_names = SMALL + ("conv_w",)
    packed = [_pack([t[n] for n in packed_names]) for t in (w, grads, m, v)]
    outs = _adamw(*packed, "adamw_small")
    shapes = [w[n].shape for n in packed_names]
    for res, o in zip((delta, new_m, new_v), outs):
        for n, a in zip(packed_names, _unpack(o, shapes)):
            res[n] = a

    return (loss_sum.reshape(()), dx[None], *[grads[n] for n in ORDER], *[delta[n] for n in ORDER],
            *[new_m[n] for n in ORDER], *[new_v[n] for n in ORDER])
```

```python
import functools

import jax
import jax.numpy as jnp
from jax import lax
from jax.experimental import pallas as pl
from jax.experimental.pallas import tpu as pltpu
from jax.experimental.pallas import tpu_sc as plsc

F32 = jnp.float32
BF16 = jnp.bfloat16

HEAD_DIM = 128
CHUNK = 128
SUB = 16
EXP_CAP = 80.0
NORM_EPS = 1e-6
L2_EPS = 1e-6
CONV_W = 4
VMEM_LIMIT = 56 * 1024 * 1024

ADAM_LR, ADAM_B1, ADAM_B2, ADAM_EPS, ADAM_WD, ADAM_STEP = 1e-3, 0.9, 0.999, 1e-8, 0.01, 10

NN = ((1,), (0,))
NT = ((1,), (1,))
TN = ((0,), (0,))
MESH = pl.DeviceIdType.MESH


def _dot(a, b, dims):
    return lax.dot_general(a.astype(BF16), b.astype(BF16), (dims, ((), ())),
                           preferred_element_type=F32)


def _split(a):
    hi = a.astype(BF16)
    return hi, (a - hi.astype(F32)).astype(BF16)


def _dot3(a, b, dims):
    ah, al = _split(a)
    bh, bl = _split(b)
    d = lambda x, y: lax.dot_general(x, y, (dims, ((), ())), preferred_element_type=F32)
    return d(ah, bh) + (d(ah, bl) + d(al, bh))


def _sigmoid(x):
    return 1.0 / (1.0 + jnp.exp(-x))


def _silu(x):
    return x * _sigmoid(x)


def _dsilu(x):
    s = _sigmoid(x)
    return s * (1.0 + x * (1.0 - s))


def _softplus(x):
    e = jnp.exp(-jnp.abs(x))
    u = 1.0 + e
    log1p = jnp.where(u == 1.0, e, jnp.log(u) * (e / jnp.where(u == 1.0, 1.0, u - 1.0)))
    return jnp.maximum(x, 0.0) + log1p


def _iota(shape, axis):
    return lax.broadcasted_iota(jnp.int32, shape, axis)


def _cumsum_rows(x):
    n = x.shape[0]
    row = _iota(x.shape, 0)
    s = 1
    while s < n:
        x = x + jnp.where(row >= s, pltpu.roll(x, s, 0), 0.0)
        s *= 2
    return x


def _rev_cumsum_rows(x):
    return jnp.sum(x, axis=0, keepdims=True) - _cumsum_rows(x) + x


def _params(sem):
    return pltpu.CompilerParams(dimension_semantics=sem, vmem_limit_bytes=VMEM_LIMIT)


ROW_TILE = 8
HEADS_PER_STEP = 4


def _hps(h):
    return min(HEADS_PER_STEP, h)


def _head_view(ref, hb):
    if len(ref.shape) == 2:
        return ref.at[:, pl.ds(hb * HEAD_DIM, HEAD_DIM)]
    return ref.at[hb]


class _Staged:
    def __init__(self, ref, load):
        self.ref = ref
        self.loaded = ref[...] if load else None
        self.written = None

    def __getitem__(self, idx):
        return self.loaded

    def __setitem__(self, idx, value):
        self.written = value


def _each_head(one_head, n_in):
    def body(*refs):
        @pl.when(pl.program_id(1) == 0)
        def _():
            refs[-1][...] = jnp.zeros_like(refs[-1])

        last = len(refs) - 1
        staged = [[_Staged(_head_view(r, hb), i < n_in or i == last) for i, r in enumerate(refs)]
                  for hb in range(refs[-1].shape[0])]
        running = [one_head(*per_head) for per_head in staged]
        while running:
            for gen in list(running):
                try:
                    next(gen)
                except StopIteration:
                    running.remove(gen)
        for per_head in staged:
            for s in per_head:
                if s.written is not None:
                    s.ref[...] = s.written
    return body


def _tile(n, want):
    t = min(n, want)
    while n % t:
        t //= 2
    return t


def _mm(a, b, mode, out_dtypes, name, epi=None, extras=(), tm=1024, tn=1024, tk=2048,
        b_stacked=False, out_stacked=False):
    if mode == "tn":
        kdim, m = a.shape
    else:
        m, kdim = a.shape
    if b_stacked:
        n = N_CHIPS * b.shape[2] if mode == "nn" else b.shape[1]
        kdim_b = b.shape[1] if mode == "nn" else N_CHIPS * b.shape[2]
        assert kdim_b == kdim
    else:
        n = b.shape[0] if mode == "nt" else b.shape[1]
    per_shard = (n if (mode == "nn" or out_stacked) else kdim) // N_CHIPS
    tm, tn, tk = _tile(m, tm), _tile(n, tn), _tile(kdim, tk)
    if (b_stacked and mode == "nn") or out_stacked:
        tn = _tile(per_shard, tn)
    if b_stacked and mode == "nt":
        tk = _tile(per_shard, tk)
    nk = kdim // tk
    dims = {"nn": NN, "nt": NT, "tn": TN}[mode]
    a_spec = (pl.BlockSpec((tk, tm), lambda i, j, k: (k, i)) if mode == "tn"
              else pl.BlockSpec((tm, tk), lambda i, j, k: (i, k)))
    if b_stacked and mode == "nn":
        per = per_shard // tn
        b_spec = pl.BlockSpec((None, tk, tn), lambda i, j, k: (j // per, k, j % per))
    elif b_stacked:
        per = per_shard // tk
        b_spec = pl.BlockSpec((None, tn, tk), lambda i, j, k: (k // per, j, k % per))
    else:
        b_spec = (pl.BlockSpec((tn, tk), lambda i, j, k: (j, k)) if mode == "nt"
                  else pl.BlockSpec((tk, tn), lambda i, j, k: (k, j)))
    mn_spec = pl.BlockSpec((tm, tn), lambda i, j, k: (i, j))
    if out_stacked:
        per_o = per_shard // tn
        out_spec = pl.BlockSpec((None, tm, tn), lambda i, j, k: (j // per_o, i, j % per_o))
        out_shape = (N_CHIPS, m, per_shard)
    else:
        out_spec, out_shape = mn_spec, (m, n)
    ne, no = len(extras), len(out_dtypes)
    if epi is None:
        epi = lambda acc: (acc,)

    def body(a_ref, b_ref, *rest):
        extra_refs, out_refs = rest[:ne], rest[ne:ne + no]
        part = _dot(a_ref[...], b_ref[...], dims)

        def finish(total):
            outs = epi(total, *[r[...] for r in extra_refs])
            for o_ref, o in zip(out_refs, outs):
                o_ref[...] = o.astype(o_ref.dtype)

        if nk == 1:
            finish(part)
            return
        acc = rest[-1]
        k = pl.program_id(2)

        @pl.when(k == 0)
        def _():
            acc[...] = part

        @pl.when(jnp.logical_and(k > 0, k < nk - 1))
        def _():
            acc[...] += part

        @pl.when(k == nk - 1)
        def _():
            finish(acc[...] + part)

    outs = pl.pallas_call(
        body, name=name,
        out_shape=tuple(jax.ShapeDtypeStruct(out_shape, d) for d in out_dtypes),
        grid=(m // tm, n // tn, nk),
        in_specs=[a_spec, b_spec] + [mn_spec] * ne,
        out_specs=tuple(out_spec for _ in out_dtypes),
        scratch_shapes=[pltpu.VMEM((tm, tn), F32)] if nk > 1 else [],
        compiler_params=_params(("parallel", "parallel", "arbitrary")),
    )(a, b, *extras)
    return outs if no > 1 else outs[0]


ROWS = 256


def _rms_fwd(x, w, name):
    t, d = x.shape
    tr = _tile(t, ROWS)

    def body(x_ref, w_ref, n_ref):
        xv = x_ref[...]
        r = lax.rsqrt(jnp.mean(xv * xv, axis=-1, keepdims=True) + NORM_EPS)
        n_ref[...] = (xv * r * w_ref[...]).astype(n_ref.dtype)

    return pl.pallas_call(
        body, name=name, out_shape=jax.ShapeDtypeStruct((t, d), BF16), grid=(t // tr,),
        in_specs=[pl.BlockSpec((tr, d), lambda i: (i, 0)), pl.BlockSpec((1, d), lambda i: (0, 0))],
        out_specs=pl.BlockSpec((tr, d), lambda i: (i, 0)),
        compiler_params=_params(("parallel",)),
    )(x, w.reshape(1, d))


def _rms_bwd(dn, x, w, dres, name):
    t, d = x.shape
    tr = _tile(t, ROWS)

    def body(dn_ref, x_ref, w_ref, dres_ref, dx_ref, dxb_ref, dw_ref):
        i = pl.program_id(0)
        xv, dnv = x_ref[...], dn_ref[...]
        r = lax.rsqrt(jnp.mean(xv * xv, axis=-1, keepdims=True) + NORM_EPS)
        xh = xv * r
        dxh = dnv * w_ref[...]
        dx = dres_ref[...] + r * (dxh - xh * jnp.mean(dxh * xh, axis=-1, keepdims=True))
        dx_ref[...] = dx
        dxb_ref[...] = dx.astype(BF16)

        @pl.when(i == 0)
        def _():
            dw_ref[...] = jnp.zeros_like(dw_ref)

        dw_ref[...] += jnp.sum(dnv * xh, axis=0, keepdims=True)

    row = pl.BlockSpec((tr, d), lambda i: (i, 0))
    vec = pl.BlockSpec((1, d), lambda i: (0, 0))
    return pl.pallas_call(
        body, name=name,
        out_shape=(jax.ShapeDtypeStruct((t, d), F32), jax.ShapeDtypeStruct((t, d), BF16),
                   jax.ShapeDtypeStruct((1, d), F32)),
        grid=(t // tr,), in_specs=[row, row, vec, row], out_specs=(row, row, vec),
        compiler_params=_params(("arbitrary",)),
    )(dn, x, w.reshape(1, d), dres)


def _loss_head(h, w, target):
    t, d = h.shape
    tr = _tile(t, ROWS)

    def body(h_ref, w_ref, t_ref, loss_ref, dh_ref, dhb_ref, dw_ref):
        i = pl.program_id(0)
        hv, wv = h_ref[...], w_ref[...]
        r = lax.rsqrt(jnp.mean(hv * hv, axis=-1, keepdims=True) + NORM_EPS)
        hh = hv * r
        err = hh * wv - t_ref[...]
        dout = err * (1.0 / d)
        dhh = dout * wv
        dh = r * (dhh - hh * jnp.mean(dhh * hh, axis=-1, keepdims=True))
        dh_ref[...] = dh
        dhb_ref[...] = dh.astype(BF16)

        @pl.when(i == 0)
        def _():
            dw_ref[...] = jnp.zeros_like(dw_ref)
            loss_ref[...] = jnp.zeros_like(loss_ref)

        dw_ref[...] += jnp.sum(dout * hh, axis=0, keepdims=True)
        loss_ref[...] += jnp.full((1, 128), 0.5 / d, F32) * jnp.sum(err * err)

    row = pl.BlockSpec((tr, d), lambda i: (i, 0))
    vec = pl.BlockSpec((1, d), lambda i: (0, 0))
    lspec = pl.BlockSpec((1, 128), lambda i: (0, 0))
    return pl.pallas_call(
        body, name="loss_head",
        out_shape=(jax.ShapeDtypeStruct((1, 128), F32), jax.ShapeDtypeStruct((t, d), F32),
                   jax.ShapeDtypeStruct((t, d), BF16), jax.ShapeDtypeStruct((1, d), F32)),
        grid=(t // tr,), in_specs=[row, vec, row], out_specs=(lspec, row, row, vec),
        compiler_params=_params(("arbitrary",)),
    )(h, w.reshape(1, d), target)


def _inv_unit_lower(a):
    c = a.shape[0]
    eye = (_iota((c, c), 0) == _iota((c, c), 1)).astype(F32)
    x = eye - a
    p = _dot3(a, a, NN)
    yield
    n = 2
    while n < c:
        x = x + _dot3(x, p, NN)
        n *= 2
        if n < c:
            p = _dot3(p, p, NN)
        yield
    return x


def _gdn_chunk(q, k, v, beta, g):
    c = q.shape[0]
    row, col = _iota((c, c), 0), _iota((c, c), 1)
    gc = _cumsum_rows(g)
    diff = gc - gc.T
    dec = jnp.where(row >= col, jnp.exp(jnp.minimum(diff, 0.0)), 0.0)
    dec_s = jnp.where(row > col, dec, 0.0)
    gam = jnp.exp(gc)
    g_last = jnp.sum(g, axis=0, keepdims=True)
    kk = _dot(k, k, NT)
    a = beta * kk * dec_s
    p = _dot(q, k, NT) * dec
    e_end = jnp.exp(g_last - gc)
    return dict(dec=dec, dec_s=dec_s, gam=gam, gam_last=jnp.exp(g_last), e_end=e_end,
                k_end=k * e_end, kk=kk, a=a, p=p)


def _gdn_fwd(q, k, v, beta_bc, g_bc):
    t = q.shape[0]
    h = q.shape[1] // HEAD_DIM
    nc = t // CHUNK

    def body(q_ref, k_ref, v_ref, b_ref, g_ref, o_ref, s_ref, t_ref, state):
        qv, kv, vv, beta = q_ref[...], k_ref[...], v_ref[...], b_ref[...]
        ch = _gdn_chunk(qv, kv, vv, beta, g_ref[...])
        yield
        tm = yield from _inv_unit_lower(ch["a"])
        sol = _dot(tm, jnp.concatenate([beta * vv, beta * ch["gam"] * kv], axis=1), NN)
        yield
        u_v, w = sol[:, :HEAD_DIM], sol[:, HEAD_DIM:]
        s0 = state[...]
        u = u_v - _dot(w, s0, NN)
        yield
        o_ref[...] = _dot(qv * ch["gam"], s0, NN) + _dot(ch["p"], u, NN)
        s_ref[...] = s0
        t_ref[...] = tm
        state[...] = ch["gam_last"] * s0 + _dot(ch["k_end"], u, TN)

    tok = pl.BlockSpec((CHUNK, _hps(h) * HEAD_DIM), lambda hh, c: (c, hh))
    bc = pl.BlockSpec((_hps(h), CHUNK, HEAD_DIM), lambda hh, c: (hh, c, 0))
    mat = pl.BlockSpec((_hps(h), None, HEAD_DIM, HEAD_DIM), lambda hh, c: (hh, c, 0, 0))
    return pl.pallas_call(
        _each_head(body, 5), name="gdn_fwd",
        out_shape=(jax.ShapeDtypeStruct(q.shape, F32),
                   jax.ShapeDtypeStruct((h, nc, HEAD_DIM, HEAD_DIM), F32),
                   jax.ShapeDtypeStruct((h, nc, CHUNK, CHUNK), F32)),
        grid=(h // _hps(h), nc), in_specs=[tok, tok, tok, bc, bc], out_specs=(tok, mat, mat),
        scratch_shapes=[pltpu.VMEM((_hps(h), HEAD_DIM, HEAD_DIM), F32)],
        compiler_params=_params(("parallel", "arbitrary")),
    )(q, k, v, beta_bc, g_bc)


def _gdn_bwd(q, k, v, beta_bc, g_bc, states, invs, do, do_blk=0):
    t = q.shape[0]
    h = q.shape[1] // HEAD_DIM
    nc = t // CHUNK

    def body(q_ref, k_ref, v_ref, b_ref, g_ref, s_ref, t_ref, do_ref,
             dq_ref, dk_ref, dv_ref, db_ref, dg_ref, dstate):
        qv, kv, vv, beta = q_ref[...], k_ref[...], v_ref[...], b_ref[...]
        dov, s0, tm, ds1 = do_ref[...], s_ref[...], t_ref[...], dstate[...]
        ch = _gdn_chunk(qv, kv, vv, beta, g_ref[...])
        yield
        gam, dec, dec_s, kk = ch["gam"], ch["dec"], ch["dec_s"], ch["kk"]
        r_v, r_w = beta * vv, beta * gam * kv
        sol = _dot(tm, jnp.concatenate([r_v, r_w], axis=1), NN)
        yield
        u_v, w = sol[:, :HEAD_DIM], sol[:, HEAD_DIM:]
        u = u_v - _dot(w, s0, NN)
        qg = qv * gam
        yield

        du = _dot(ch["p"], dov, TN) + _dot(ch["k_end"], ds1, NN)
        dp = _dot(dov, u, NT)
        dpd = dp * dec
        dqg = _dot(dov, s0, NT)
        dk_end = _dot(u, ds1, NT)
        yield
        dq = dqg * gam + _dot(dpd, kv, NN)
        dk = _dot(dpd, qv, TN) + dk_end * ch["e_end"]
        dstate[...] = _dot(qg, dov, TN) + ch["gam_last"] * ds1 - _dot(w, du, TN)
        dw = -_dot(du, s0, NT)
        yield
        dr = _dot(tm, jnp.concatenate([du, dw], axis=1), TN)
        yield
        dr_v, dr_w = dr[:, :HEAD_DIM], dr[:, HEAD_DIM:]
        da = -_dot(dr, sol, NT)
        yield
        dkk = da * beta * dec_s
        dk = dk + _dot(dkk, kv, NN) + _dot(dkk, kv, TN) + beta * gam * dr_w
        dbeta = (jnp.sum(da * kk * dec_s, axis=1, keepdims=True)
                 + jnp.sum(dr_v * vv + dr_w * gam * kv, axis=1, keepdims=True))

        pair = dp * ch["p"] + da * ch["a"]
        end = jnp.sum(dk_end * ch["k_end"], axis=1, keepdims=True)
        dgc = (jnp.sum(pair - pair.T, axis=1, keepdims=True)
               + jnp.sum(dqg * qg + dr_w * r_w, axis=1, keepdims=True) - end)
        at_end = jnp.sum(end) + ch["gam_last"] * jnp.sum(s0 * ds1)
        dgc = jnp.broadcast_to(dgc, (CHUNK, HEAD_DIM))
        dgc = dgc + jnp.where(_iota((CHUNK, HEAD_DIM), 0) == CHUNK - 1, at_end, 0.0)
        dq_ref[...] = dq
        dk_ref[...] = dk
        dv_ref[...] = beta * dr_v
        db_ref[...] = jnp.broadcast_to(dbeta, (CHUNK, HEAD_DIM)).T[:ROW_TILE]
        dg_ref[...] = _rev_cumsum_rows(dgc).T[:ROW_TILE]

    rev = lambda c: nc - 1 - c
    tok = pl.BlockSpec((CHUNK, _hps(h) * HEAD_DIM), lambda hh, c: (rev(c), hh))
    bc = pl.BlockSpec((_hps(h), CHUNK, HEAD_DIM), lambda hh, c: (hh, rev(c), 0))
    mat = pl.BlockSpec((_hps(h), None, HEAD_DIM, HEAD_DIM), lambda hh, c: (hh, rev(c), 0, 0))
    tok_shape = jax.ShapeDtypeStruct(q.shape, F32)
    row_shape = jax.ShapeDtypeStruct((h, nc, ROW_TILE, CHUNK), F32)
    rows = pl.BlockSpec((_hps(h), None, ROW_TILE, CHUNK), lambda hh, c: (hh, rev(c), 0, 0))
    return pl.pallas_call(
        _each_head(body, 8), name="gdn_bwd",
        out_shape=(tok_shape, tok_shape, tok_shape, row_shape, row_shape),
        grid=(h // _hps(h), nc),
        in_specs=[tok, tok, tok, bc, bc, mat, mat,
                  pl.BlockSpec((CHUNK, _hps(h) * HEAD_DIM), lambda hh, c: (rev(c), do_blk // _hps(h) + hh))],
        out_specs=(tok, tok, tok, rows, rows),
        scratch_shapes=[pltpu.VMEM((_hps(h), HEAD_DIM, HEAD_DIM), F32)],
        compiler_params=_params(("parallel", "arbitrary")),
    )(q, k, v, beta_bc, g_bc, states, invs, do)


def _hgrn_chunk(q, k, lf):
    c = q.shape[0]
    row = _iota((c, HEAD_DIM), 0)
    b = _cumsum_rows(lf)
    q_subs, k_facs, a_rows = [], [], []
    for x in range(c // SUB):
        b_start = jnp.sum(jnp.where(row < x * SUB, lf, 0.0), axis=0, keepdims=True)
        q_x = (q * jnp.exp(jnp.minimum(b - b_start, 0.0)))[x * SUB:(x + 1) * SUB]
        k_fac = jnp.where(row < (x + 1) * SUB, jnp.exp(jnp.minimum(b_start - b, EXP_CAP)), 0.0)
        q_subs.append(q_x)
        k_facs.append(k_fac)
        a_rows.append(_dot(q_x, k * k_fac, NT))
    a = jnp.concatenate(a_rows, axis=0)
    a = jnp.where(_iota((c, c), 0) >= _iota((c, c), 1), a, 0.0)
    b_last = jnp.sum(lf, axis=0, keepdims=True)
    return dict(b=b, a=a, q_subs=q_subs, k_facs=k_facs, e_b=jnp.exp(b),
                e_end=jnp.exp(b_last - b), e_last=jnp.exp(b_last))


def _hgrn_fwd(q, k, v, lf, v_blk=0):
    t = q.shape[0]
    h = q.shape[1] // HEAD_DIM
    nc = t // CHUNK

    def body(q_ref, k_ref, v_ref, lf_ref, o_ref, s_ref, state):
        qv, kv, vv = q_ref[...], k_ref[...], v_ref[...]
        ch = _hgrn_chunk(qv, kv, lf_ref[...])
        yield
        s0 = state[...]
        o_ref[...] = _dot(qv * ch["e_b"], s0, NT) + _dot(ch["a"], vv, NN)
        s_ref[...] = s0
        state[...] = s0 * ch["e_last"] + _dot(vv, kv * ch["e_end"], TN)

    tok = pl.BlockSpec((CHUNK, _hps(h) * HEAD_DIM), lambda hh, c: (c, hh))
    mat = pl.BlockSpec((_hps(h), None, HEAD_DIM, HEAD_DIM), lambda hh, c: (hh, c, 0, 0))
    return pl.pallas_call(
        _each_head(body, 4), name="hgrn_fwd",
        out_shape=(jax.ShapeDtypeStruct(q.shape, F32),
                   jax.ShapeDtypeStruct((h, nc, HEAD_DIM, HEAD_DIM), F32)),
        grid=(h // _hps(h), nc),
        in_specs=[tok, tok, pl.BlockSpec((CHUNK, _hps(h) * HEAD_DIM), lambda hh, c: (c, v_blk // _hps(h) + hh)), tok],
        out_specs=(tok, mat),
        scratch_shapes=[pltpu.VMEM((_hps(h), HEAD_DIM, HEAD_DIM), F32)],
        compiler_params=_params(("parallel", "arbitrary")),
    )(q, k, v, lf)


def _hgrn_bwd(q, k, v, lf, states, do, v_blk=0, do_blk=0):
    t = q.shape[0]
    nc = t // CHUNK
    h = q.shape[1] // HEAD_DIM

    def body(q_ref, k_ref, v_ref, lf_ref, s_ref, do_ref, dq_ref, dk_ref, dv_ref, dlf_ref, dstate):
        qv, kv, vv, dov, s0 = q_ref[...], k_ref[...], v_ref[...], do_ref[...], s_ref[...]
        ds1 = dstate[...]
        ch = _hgrn_chunk(qv, kv, lf_ref[...])
        yield
        c = CHUNK
        row = _iota((c, HEAD_DIM), 0)
        qh = qv * ch["e_b"]
        k_end = kv * ch["e_end"]
        da = jnp.where(_iota((c, c), 0) >= _iota((c, c), 1), _dot(dov, vv, NT), 0.0)
        dqh = _dot(dov, s0, NN)
        dk_end = _dot(vv, ds1, NN)
        yield
        end = dk_end * k_end
        dk = dk_end * ch["e_end"]
        db = dqh * qh - end + jnp.where(
            row == c - 1, jnp.sum(end + s0 * ch["e_last"] * ds1, axis=0, keepdims=True), 0.0)
        dq_rows, qdq_rows = [], []
        for x in range(c // SUB):
            da_x = da[x * SUB:(x + 1) * SUB]
            k_x = kv * ch["k_facs"][x]
            dq_x = _dot(da_x, k_x, NN)
            dk_x = _dot(da_x, ch["q_subs"][x], TN)
            dq_rows.append(dq_x)
            qdq_rows.append(dq_x * ch["q_subs"][x])
            dk = dk + dk_x * ch["k_facs"][x]
            kdk = dk_x * k_x
            db = db - kdk
            if x > 0:
                at_start = jnp.sum(kdk, axis=0, keepdims=True) - jnp.sum(qdq_rows[x], axis=0, keepdims=True)
                db = db + jnp.where(row == x * SUB - 1, at_start, 0.0)
        yield
        b_start = jnp.zeros((c, HEAD_DIM), F32)
        for x in range(1, c // SUB):
            b_x = jnp.sum(jnp.where(row < x * SUB, lf_ref[...], 0.0), axis=0, keepdims=True)
            b_start = jnp.where(row >= x * SUB, b_x, b_start)
        dq = dqh * ch["e_b"] + jnp.concatenate(dq_rows, axis=0) * jnp.exp(jnp.minimum(ch["b"] - b_start, 0.0))
        db = db + jnp.concatenate(qdq_rows, axis=0)
        dstate[...] = _dot(dov, qh, TN) + ds1 * ch["e_last"]
        dq_ref[...] = dq
        dk_ref[...] = dk
        dv_ref[...] = _dot(ch["a"], dov, TN) + _dot(k_end, ds1, NT)
        dlf_ref[...] = _rev_cumsum_rows(db)

    rev = lambda c: nc - 1 - c
    tok = pl.BlockSpec((CHUNK, _hps(h) * HEAD_DIM), lambda hh, c: (rev(c), hh))
    mat = pl.BlockSpec((_hps(h), None, HEAD_DIM, HEAD_DIM), lambda hh, c: (hh, rev(c), 0, 0))
    tok_shape = jax.ShapeDtypeStruct(q.shape, F32)
    return pl.pallas_call(
        _each_head(body, 6), name="hgrn_bwd",
        out_shape=(tok_shape, tok_shape, tok_shape, tok_shape),
        grid=(h // _hps(h), nc),
        in_specs=[tok, tok, pl.BlockSpec((CHUNK, _hps(h) * HEAD_DIM), lambda hh, c: (rev(c), v_blk // _hps(h) + hh)), tok, mat,
                  pl.BlockSpec((CHUNK, _hps(h) * HEAD_DIM), lambda hh, c: (rev(c), do_blk // _hps(h) + hh))],
        out_specs=(tok, tok, tok, tok),
        scratch_shapes=[pltpu.VMEM((_hps(h), HEAD_DIM, HEAD_DIM), F32)],
        compiler_params=_params(("parallel", "arbitrary")),
    )(q, k, v, lf, states, do)


CONV_ROWS = 256
HALO = 8


def _shift_down(cur, prev, s):
    rt = cur.shape[0]
    head = jnp.concatenate([pltpu.roll(prev, s, 0), jnp.zeros((rt - HALO, cur.shape[1]), F32)], axis=0)
    return jnp.where(_iota(cur.shape, 0) < s, head, pltpu.roll(cur, s, 0))


def _shift_up(cur, nxt, s):
    rt = cur.shape[0]
    tail = jnp.concatenate([jnp.zeros((rt - HALO, cur.shape[1]), F32), pltpu.roll(nxt, HALO - s, 0)], axis=0)
    return jnp.where(_iota(cur.shape, 0) >= rt - s, tail, pltpu.roll(cur, rt - s, 0))


def _tile_with_prev(ref, i, rt):
    r0 = pl.multiple_of(i * rt, rt)
    cur = ref[pl.ds(r0, rt), :]
    prev = ref[pl.ds(pl.multiple_of(jnp.maximum(r0 - HALO, 0), HALO), HALO), :]
    return cur, jnp.where(i > 0, prev, 0.0)


def _tile_with_next(ref, i, rt, n_tiles):
    r0 = pl.multiple_of(i * rt, rt)
    cur = ref[pl.ds(r0, rt), :]
    nxt = ref[pl.ds(pl.multiple_of(jnp.minimum(r0 + rt, (n_tiles - 1) * rt), HALO), HALO), :]
    return cur, jnp.where(i < n_tiles - 1, nxt, 0.0)


def _conv_tile(x_ref, w_ref, i, rt):
    cur, prev = _tile_with_prev(x_ref, i, rt)
    shifted = [_shift_down(cur, prev, CONV_W - 1 - j) for j in range(CONV_W - 1)] + [cur]
    c = shifted[0] * w_ref[pl.ds(0, 1), :]
    for j in range(1, CONV_W):
        c = c + shifted[j] * w_ref[pl.ds(j, 1), :]
    return c, shifted


def _l2n(s):
    return s * lax.rsqrt(jnp.sum(s * s, axis=-1, keepdims=True) + L2_EPS)


def _gdn_prep_fwd(proj, conv_w, h):
    t = proj.shape[0]
    rt = _tile(t, CONV_ROWS)
    nt = t // rt
    scale = HEAD_DIM ** -0.5

    def body(xq, xk, xv, wq, wk, wv, q_ref, k_ref, v_ref):
        def tile(i, carry):
            rows = pl.ds(pl.multiple_of(i * rt, rt), rt)
            q_ref[rows, :] = _l2n(_silu(_conv_tile(xq, wq, i, rt)[0])) * scale
            k_ref[rows, :] = _l2n(_silu(_conv_tile(xk, wk, i, rt)[0]))
            v_ref[rows, :] = _silu(_conv_tile(xv, wv, i, rt)[0])
            return carry

        lax.fori_loop(0, nt, tile, 0)

    col = lambda p: pl.BlockSpec((t, HEAD_DIM), lambda hh: (0, p * h + hh))
    wcol = lambda p: pl.BlockSpec((CONV_W, HEAD_DIM), lambda hh: (0, p * h + hh))
    out = pl.BlockSpec((t, HEAD_DIM), lambda hh: (0, hh))
    shape = jax.ShapeDtypeStruct((t, h * HEAD_DIM), F32)
    return pl.pallas_call(
        body, name="gdn_prep_fwd", out_shape=(shape, shape, shape), grid=(h,),
        in_specs=[col(0), col(1), col(2), wcol(0), wcol(1), wcol(2)], out_specs=(out, out, out),
        compiler_params=_params(("parallel",)),
    )(proj, proj, proj, conv_w, conv_w, conv_w)


def _gdn_prep_bwd(proj, conv_w, dq, dk, dv, h):
    t = proj.shape[0]
    rt = _tile(t, CONV_ROWS)
    nt = t // rt
    scale = HEAD_DIM ** -0.5

    def part(x_ref, w_ref, dy_ref, dx_ref, dw_ref, dc_ref, norm_scale):
        def first(i, dws):
            rows = pl.ds(pl.multiple_of(i * rt, rt), rt)
            c, shifted = _conv_tile(x_ref, w_ref, i, rt)
            ds = dy_ref[rows, :]
            if norm_scale is not None:
                s = _silu(c)
                r = lax.rsqrt(jnp.sum(s * s, axis=-1, keepdims=True) + L2_EPS)
                y = s * r
                dyn = ds * norm_scale
                ds = r * (dyn - y * jnp.sum(dyn * y, axis=-1, keepdims=True))
            dc = ds * _dsilu(c)
            dc_ref[rows, :] = dc
            return tuple(dws[j] + jnp.sum(dc * shifted[j], axis=0, keepdims=True) for j in range(CONV_W))

        dws = lax.fori_loop(0, nt, first, tuple(jnp.zeros((1, HEAD_DIM), F32) for _ in range(CONV_W)))
        for j in range(CONV_W):
            dw_ref[pl.ds(j, 1), :] = dws[j]

        def second(i, carry):
            rows = pl.ds(pl.multiple_of(i * rt, rt), rt)
            cur, nxt = _tile_with_next(dc_ref, i, rt, nt)
            dx = cur * w_ref[pl.ds(CONV_W - 1, 1), :]
            for j in range(CONV_W - 1):
                dx = dx + _shift_up(cur, nxt, CONV_W - 1 - j) * w_ref[pl.ds(j, 1), :]
            dx_ref[rows, :] = dx.astype(dx_ref.dtype)
            return carry

        lax.fori_loop(0, nt, second, 0)

    def body(xq, xk, xv, wq, wk, wv, dq_ref, dk_ref, dv_ref, dxq, dxk, dxv, dwq, dwk, dwv, dc_ref):
        part(xq, wq, dq_ref, dxq, dwq, dc_ref, scale)
        part(xk, wk, dk_ref, dxk, dwk, dc_ref, 1.0)
        part(xv, wv, dv_ref, dxv, dwv, dc_ref, None)

    col = lambda p: pl.BlockSpec((t, HEAD_DIM), lambda hh: (0, p * h + hh))
    wcol = lambda p: pl.BlockSpec((CONV_W, HEAD_DIM), lambda hh: (0, p * h + hh))
    own = pl.BlockSpec((t, HEAD_DIM), lambda hh: (0, hh))
    wown = pl.BlockSpec((CONV_W, HEAD_DIM), lambda hh: (0, hh))
    dx_shape = jax.ShapeDtypeStruct((t, h * HEAD_DIM), BF16)
    dw_shape = jax.ShapeDtypeStruct((CONV_W, h * HEAD_DIM), F32)
    return pl.pallas_call(
        body, name="gdn_prep_bwd",
        out_shape=(dx_shape, dx_shape, dx_shape, dw_shape, dw_shape, dw_shape), grid=(h,),
        in_specs=[col(0), col(1), col(2), wcol(0), wcol(1), wcol(2), own, own, own],
        out_specs=(own, own, own, wown, wown, wown),
        scratch_shapes=[pltpu.VMEM((t, HEAD_DIM), F32)],
        compiler_params=_params(("parallel",)),
    )(proj, proj, proj, conv_w, conv_w, conv_w, dq, dk, dv)


def _gdn_gates_fwd(ab, a_log_row, dt_bias_row):
    t = ab.shape[0]
    tr = _tile(t, 512)

    def body(ab_ref, al_ref, dt_ref, g_ref, b_ref):
        g_ref[...] = -jnp.exp(al_ref[...]) * _softplus(ab_ref[:, :HEAD_DIM] + dt_ref[...])
        b_ref[...] = _sigmoid(ab_ref[:, HEAD_DIM:])

    row = pl.BlockSpec((tr, HEAD_DIM), lambda i: (i, 0))
    vec = pl.BlockSpec((1, HEAD_DIM), lambda i: (0, 0))
    shape = jax.ShapeDtypeStruct((t, HEAD_DIM), F32)
    return pl.pallas_call(
        body, name="gdn_gates_fwd", out_shape=(shape, shape), grid=(t // tr,),
        in_specs=[pl.BlockSpec((tr, 2 * HEAD_DIM), lambda i: (i, 0)), vec, vec], out_specs=(row, row),
        compiler_params=_params(("parallel",)),
    )(ab, a_log_row, dt_bias_row)


def _gdn_gates_bwd(ab, a_log_row, dt_bias_row, dg, dbeta):
    t = ab.shape[0]
    tr = _tile(t, 512)

    def body(ab_ref, al_ref, dt_ref, dg_ref, db_ref, dab_ref, dal_ref, ddt_ref):
        @pl.when(pl.program_id(0) == 0)
        def _():
            dal_ref[...] = jnp.zeros_like(dal_ref)
            ddt_ref[...] = jnp.zeros_like(ddt_ref)

        xa = ab_ref[:, :HEAD_DIM] + dt_ref[...]
        neg_a = -jnp.exp(al_ref[...])
        dgv = dg_ref[...]
        da = dgv * neg_a * _sigmoid(xa)
        beta = _sigmoid(ab_ref[:, HEAD_DIM:])
        dab_ref[:, :HEAD_DIM] = da.astype(BF16)
        dab_ref[:, HEAD_DIM:] = (db_ref[...] * beta * (1.0 - beta)).astype(BF16)
        dal_ref[...] += jnp.sum(dgv * neg_a * _softplus(xa), axis=0, keepdims=True)
        ddt_ref[...] += jnp.sum(da, axis=0, keepdims=True)

    row = pl.BlockSpec((tr, HEAD_DIM), lambda i: (i, 0))
    row2 = pl.BlockSpec((tr, 2 * HEAD_DIM), lambda i: (i, 0))
    vec = pl.BlockSpec((1, HEAD_DIM), lambda i: (0, 0))
    vshape = jax.ShapeDtypeStruct((1, HEAD_DIM), F32)
    return pl.pallas_call(
        body, name="gdn_gates_bwd",
        out_shape=(jax.ShapeDtypeStruct((t, 2 * HEAD_DIM), BF16), vshape, vshape), grid=(t // tr,),
        in_specs=[row2, vec, vec, row, row], out_specs=(row2, vec, vec),
        compiler_params=_params(("arbitrary",)),
    )(ab, a_log_row, dt_bias_row, dg, dbeta)


def _lower_bound(lb_ref):
    return _sigmoid(lb_ref[pl.ds(0, 1), :] - lb_ref[pl.ds(1, 1), :])


def _hgrn_prep_fwd(proj, lb_logits, h, q_blk, f_blk):
    t = proj.shape[0]
    tr = _tile(t, 512)

    def body(xq, xf, lb_ref, q_ref, k_ref, lf_ref):
        lb = _lower_bound(lb_ref)
        s = _sigmoid(xf[...])
        q_ref[...] = _silu(xq[...])
        k_ref[...] = (1.0 - lb) * (1.0 - s)
        lf_ref[...] = jnp.log(lb + (1.0 - lb) * s)

    col = lambda b0: pl.BlockSpec((tr, HEAD_DIM), lambda hh, i: (i, b0 + hh))
    own = pl.BlockSpec((tr, HEAD_DIM), lambda hh, i: (i, hh))
    shape = jax.ShapeDtypeStruct((t, h * HEAD_DIM), F32)
    return pl.pallas_call(
        body, name="hgrn_prep_fwd", out_shape=(shape, shape, shape), grid=(h, t // tr),
        in_specs=[col(q_blk), col(f_blk), pl.BlockSpec((2, HEAD_DIM), lambda hh, i: (0, hh))],
        out_specs=(own, own, own), compiler_params=_params(("parallel", "parallel")),
    )(proj, proj, lb_logits)


def _hgrn_prep_bwd(proj, lb_logits, dq, dk, dlf, h, q_blk, f_blk):
    t = proj.shape[0]
    tr = _tile(t, 512)

    def body(xq, xf, lb_ref, dq_ref, dk_ref, dlf_ref, dxq, dxf, dlb_ref):
        @pl.when(pl.program_id(1) == 0)
        def _():
            dlb_ref[...] = jnp.zeros_like(dlb_ref)

        lb = _lower_bound(lb_ref)
        s = _sigmoid(xf[...])
        e = dlf_ref[...] / (lb + (1.0 - lb) * s) - dk_ref[...]
        dxq[...] = (dq_ref[...] * _dsilu(xq[...])).astype(BF16)
        dxf[...] = (s * (1.0 - s) * (1.0 - lb) * e).astype(BF16)
        d0 = jnp.sum((1.0 - s) * e, axis=0, keepdims=True) * (lb * (1.0 - lb))
        dlb_ref[pl.ds(0, 1), :] += d0
        dlb_ref[pl.ds(1, 1), :] += -d0

    col = lambda b0: pl.BlockSpec((tr, HEAD_DIM), lambda hh, i: (i, b0 + hh))
    own = pl.BlockSpec((tr, HEAD_DIM), lambda hh, i: (i, hh))
    lbs = pl.BlockSpec((2, HEAD_DIM), lambda hh, i: (0, hh))
    shape = jax.ShapeDtypeStruct((t, h * HEAD_DIM), BF16)
    return pl.pallas_call(
        body, name="hgrn_prep_bwd",
        out_shape=(shape, shape, jax.ShapeDtypeStruct((2, h * HEAD_DIM), F32)), grid=(h, t // tr),
        in_specs=[col(q_blk), col(f_blk), lbs, own, own, own], out_specs=(own, own, lbs),
        compiler_params=_params(("parallel", "arbitrary")),
    )(proj, proj, lb_logits, dq, dk, dlf)


def _gate_specs(h, z_blk, g_blk, tr):
    o_a = pl.BlockSpec((tr, HEAD_DIM), lambda hh, i: (i, jnp.minimum(hh, h - 1)))
    o_b = pl.BlockSpec((tr, HEAD_DIM), lambda hh, i: (i, jnp.maximum(hh - h, 0)))
    gate = pl.BlockSpec((tr, HEAD_DIM), lambda hh, i: (i, jnp.where(hh < h, z_blk + hh, g_blk + hh - h)))
    w = pl.BlockSpec((None, 1, HEAD_DIM), lambda hh, i: (hh // h, 0, 0))
    cat = pl.BlockSpec((tr, HEAD_DIM), lambda hh, i: (i, hh))
    return o_a, o_b, gate, w, cat


def _gate_fwd(o_a, o_b, proj, norm_w, h, z_blk, g_blk):
    t = o_a.shape[0]
    tr = _tile(t, 512)

    def body(oa_ref, ob_ref, z_ref, w_ref, y_ref):
        o = jnp.where(pl.program_id(0) < h, oa_ref[...], ob_ref[...])
        r = lax.rsqrt(jnp.mean(o * o, axis=-1, keepdims=True) + NORM_EPS)
        y_ref[...] = (o * r * w_ref[...] * _silu(z_ref[...])).astype(y_ref.dtype)

    sa, sb, sg, sw, cat = _gate_specs(h, z_blk, g_blk, tr)
    return pl.pallas_call(
        body, name="gate_fwd", out_shape=jax.ShapeDtypeStruct((t, 2 * h * HEAD_DIM), BF16),
        grid=(2 * h, t // tr), in_specs=[sa, sb, sg, sw], out_specs=cat,
        compiler_params=_params(("parallel", "parallel")),
    )(o_a, o_b, proj, norm_w)


def _gate_bwd(o_a, o_b, proj, norm_w, dy, h, z_blk, g_blk):
    t = o_a.shape[0]
    tr = _tile(t, 512)

    def body(oa_ref, ob_ref, z_ref, w_ref, dy_ref, do_ref, dz_ref, dw_ref):
        hh = pl.program_id(0)

        @pl.when(jnp.logical_and(hh % h == 0, pl.program_id(1) == 0))
        def _():
            dw_ref[...] = jnp.zeros_like(dw_ref)

        o = jnp.where(hh < h, oa_ref[...], ob_ref[...])
        z, w, dyv = z_ref[...], w_ref[...], dy_ref[...]
        r = lax.rsqrt(jnp.mean(o * o, axis=-1, keepdims=True) + NORM_EPS)
        oh = o * r
        dz_ref[...] = (dyv * oh * w * _dsilu(z)).astype(dz_ref.dtype)
        dn = dyv * _silu(z)
        doh = dn * w
        do_ref[...] = r * (doh - oh * jnp.mean(doh * oh, axis=-1, keepdims=True))
        dw_ref[...] += jnp.sum(dn * oh, axis=0, keepdims=True)

    sa, sb, sg, sw, cat = _gate_specs(h, z_blk, g_blk, tr)
    width = 2 * h * HEAD_DIM
    return pl.pallas_call(
        body, name="gate_bwd",
        out_shape=(jax.ShapeDtypeStruct((t, width), F32), jax.ShapeDtypeStruct((t, width), BF16),
                   jax.ShapeDtypeStruct((2, 1, HEAD_DIM), F32)),
        grid=(2 * h, t // tr), in_specs=[sa, sb, sg, sw, cat], out_specs=(cat, cat, sw),
        compiler_params=_params(("arbitrary", "arbitrary")),
    )(o_a, o_b, proj, norm_w, dy)


def _lane_row(vec):
    return jnp.pad(vec.reshape(1, -1), ((0, 0), (0, HEAD_DIM - vec.shape[-1])))


def _add_epi(acc, res):
    return (acc + res,)


def _split_w_in(w_in, h):
    gw = h * HEAD_DIM
    main = jnp.concatenate([w_in[:, :4 * gw], w_in[:, 4 * gw + 2 * h:]], axis=1)
    pad = jnp.zeros((w_in.shape[0], HEAD_DIM - h), w_in.dtype)
    ab = jnp.concatenate([w_in[:, 4 * gw:4 * gw + h], pad, w_in[:, 4 * gw + h:4 * gw + 2 * h], pad], axis=1)
    return main, ab


def _merge_w_in(main, ab, h):
    gw = h * HEAD_DIM
    return jnp.concatenate([main[:, :4 * gw], ab[:, :h], ab[:, HEAD_DIM:HEAD_DIM + h], main[:, 4 * gw:]], axis=1)


def _local_step(x, target, w_main, w_ab, conv_w, a_log, dt_bias, gdn_norm_w, lb_logits, hgrn_norm_w,
                w_out, norm_mix_w, norm_ffn_w, w_ff1, w_ff2, norm_final_w, reducer=None):
    t, d = x.shape
    h = d // (2 * HEAD_DIM)
    gw = h * HEAD_DIM
    k_blk, v_blk, z_blk, qb_blk, fb_blk, ib_blk, gb_blk = (i * h for i in range(1, 8))
    del k_blk, v_blk

    n1 = _rms_fwd(x, norm_mix_w, "rms_mix")
    proj = _mm(n1, w_main, "nn", (F32,), "in_proj")
    ab = _mm(n1, w_ab, "nn", (F32,), "in_proj_ab")

    q, k, v = _gdn_prep_fwd(proj, conv_w, h)
    a_log_row, dt_row = _lane_row(a_log), _lane_row(dt_bias)
    g_tm, beta_tm = _gdn_gates_fwd(ab, a_log_row, dt_row)
    to_heads = lambda a: jnp.broadcast_to(a[:, :h].T[:, :, None], (h, t, HEAD_DIM))
    g_bc, beta_bc = to_heads(g_tm), to_heads(beta_tm)
    o_a, st_a, inv_a = _gdn_fwd(q, k, v, beta_bc, g_bc)

    qh, kh, lf = _hgrn_prep_fwd(proj, lb_logits, h, qb_blk, fb_blk)
    o_b, st_b = _hgrn_fwd(qh, kh, proj, lf, v_blk=ib_blk)

    gate_w = jnp.stack([gdn_norm_w.reshape(1, HEAD_DIM), hgrn_norm_w.reshape(1, HEAD_DIM)])
    y = _gate_fwd(o_a, o_b, proj, gate_w, h, z_blk, gb_blk)
    h1 = _mm(y, w_out, "nn", (F32,), "out_proj", epi=_add_epi, extras=(x,))
    n2 = _rms_fwd(h1, norm_ffn_w, "rms_ffn")
    act, r = _mm(n2, w_ff1, "nn", (F32, BF16), "ff1", b_stacked=True,
                 epi=lambda acc: (acc, jnp.square(jnp.maximum(acc, 0.0))))
    h2 = _mm(r, w_ff2, "nn", (F32,), "ff2", epi=_add_epi, extras=(h1,))
    loss, dh2, dh2_b, d_norm_final = _loss_head(h2, norm_final_w, target)

    da = _mm(dh2_b, w_ff2, "nt", (BF16,), "ff2_dx",
             epi=lambda acc, a: (acc * (2.0 * jnp.maximum(a, 0.0)),), extras=(act,))
    pending = []

    def step(anchor, name=None, full=None):
        if reducer is not None:
            pending.extend(reducer.step(name, full, anchor))

    def after_step(value):
        if not pending:
            return value
        value = lax.optimization_barrier((value, *pending))[0]
        pending.clear()
        return value

    d_ff2 = _mm(r, dh2_b, "tn", (F32,), "ff2_dw")
    step(None, "w_ff2", d_ff2)
    dn2 = _mm(da, w_ff1, "nt", (F32,), "ff1_dx", b_stacked=True)
    d_ff1 = _mm(n2, da, "tn", (F32,), "ff1_dw", out_stacked=True)
    step(d_ff1, "w_ff1", d_ff1)
    dh1, dh1_b, d_norm_ffn = _rms_bwd(after_step(dn2), h1, norm_ffn_w, dh2, "rms_ffn_bwd")
    dy = _mm(dh1_b, w_out, "nt", (F32,), "out_proj_dx")
    d_out = _mm(y, dh1_b, "tn", (F32,), "out_proj_dw")
    step(d_out, "w_out", d_out)

    do, dgate, d_gate_w = _gate_bwd(o_a, o_b, proj, gate_w, after_step(dy), h, z_blk, gb_blk)
    step(do)
    dq, dk, dv, dbeta_bc, dg_bc = _gdn_bwd(q, k, v, beta_bc, g_bc, st_a, inv_a, after_step(do), do_blk=0)
    step(dq)
    dxq, dxk, dxv, dcq, dck, dcv = _gdn_prep_bwd(proj, conv_w, after_step(dq), dk, dv, h)
    step(dxq)
    from_heads = lambda a: jnp.pad(a[:, :, 0, :].reshape(h, t).T, ((0, 0), (0, HEAD_DIM - h)))
    dab, d_a_log, d_dt_bias = _gdn_gates_bwd(ab, a_log_row, dt_row, from_heads(dg_bc), from_heads(dbeta_bc))
    dqh, dkh, dvh, dlf = _hgrn_bwd(after_step(qh), kh, proj, lf, st_b, do, v_blk=ib_blk, do_blk=h)
    step(dqh)
    dxqb, dxfb, d_lb = _hgrn_prep_bwd(proj, lb_logits, dqh, dkh, dlf, h, qb_blk, fb_blk)

    dproj = jnp.concatenate([after_step(dxq), dxk, dxv, dgate[:, :gw], dxqb, dxfb, dvh.astype(BF16), dgate[:, gw:]],
                            axis=1)
    d_main = _mm(n1, dproj, "tn", (F32,), "in_proj_dw")
    d_ab = _mm(n1, dab, "tn", (F32,), "in_proj_ab_dw")
    step(d_main, "w_in", _merge_w_in(d_main, d_ab, h))
    dn1_ab = _mm(after_step(dab), w_ab, "nt", (F32,), "in_proj_ab_dx")
    step(dn1_ab)
    dn1 = _mm(after_step(dproj), w_main, "nt", (F32,), "in_proj_dx", epi=_add_epi, extras=(dn1_ab,))
    step(dn1)
    dx, _, d_norm_mix = _rms_bwd(after_step(dn1), x, norm_mix_w, dh1, "rms_mix_bwd")
    step(dx)

    grads = dict(
        w_main=d_main, w_ab=d_ab, conv_w=jnp.concatenate([dcq, dck, dcv], axis=1),
        gdn_a_log=d_a_log[:, :h], gdn_dt_bias=d_dt_bias[:, :h], gdn_norm_w=d_gate_w[0],
        hgrn_lb_logits=d_lb, hgrn_norm_w=d_gate_w[1], w_out=d_out, norm_mix_w=d_norm_mix,
        norm_ffn_w=d_norm_ffn, w_ff1=d_ff1, w_ff2=d_ff2, norm_final_w=d_norm_final)
    return loss, dx, grads


N_CHIPS = 4
ANY = pl.BlockSpec(memory_space=pl.ANY)


def _place():
    x, y, c = lax.axis_index("x"), lax.axis_index("y"), lax.axis_index("c")
    chips = [(1 - x, y), (x, 1 - y), (1 - x, 1 - y)]
    return x, y, c, chips


def _remote(src, dst, send_sems, recv_sems, k, to):
    return pltpu.make_async_remote_copy(src_ref=src, dst_ref=dst, send_sem=send_sems.at[k],
                                        recv_sem=recv_sems.at[k], device_id=to, device_id_type=MESH)


def _to_sibling(x, y, c, chips):
    return [(x, y, 1 - c)]


def _to_same_core_of_chips(x, y, c, chips):
    return [(*chip, c) for chip in chips]


def _to_all_gather_peers(x, y, c, chips):
    return _to_sibling(x, y, c, chips) + _to_same_core_of_chips(x, y, c, chips)


SIBLING_EXCHANGE = (1, _to_sibling)
CHIP_EXCHANGE = (2, _to_same_core_of_chips)
GATHER_EXCHANGE = (3, _to_all_gather_peers)


def _launch(body, name, out_shapes, arrays, sem_counts, sequencer=None, after=()):
    n, n_after = len(arrays), len(after)
    sems = [pltpu.SemaphoreType.DMA((k,)) for k in sem_counts]
    strip = lambda refs: refs[:n] + refs[n + n_after:]
    if sequencer is None:
        return pl.pallas_call(
            lambda *refs: body(*strip(refs)), name=name, out_shape=tuple(out_shapes),
            in_specs=[ANY] * (n + n_after), out_specs=tuple(ANY for _ in out_shapes), scratch_shapes=sems,
        )(*arrays, *after)
    collective_id, peers = sequencer

    def sequencer_body(*refs):
        x, y, c, chips = _place()
        barrier = pltpu.get_barrier_semaphore()
        targets = peers(x, y, c, chips)
        for target in targets:
            pl.semaphore_signal(barrier, inc=1, device_id=target, device_id_type=MESH)
        pl.semaphore_wait(barrier, len(targets))
        body(*strip(refs))

    return pl.kernel(
        sequencer_body, name=name, out_type=tuple(out_shapes),
        mesh=plsc.ScalarSubcoreMesh(axis_name="sequencer", num_cores=1), scratch_types=tuple(sems),
        compiler_params=pltpu.CompilerParams(collective_id=collective_id),
    )(*arrays, *after)


def _gather_weights(big, small, name, sequencer=None, after=()):
    nb, ns = len(big), len(small)
    n_sem = 6 * nb + 3 * ns

    def body(*refs):
        ins, outs = refs[:nb + ns], refs[nb + ns:2 * (nb + ns)]
        send_sems, recv_sems = refs[2 * (nb + ns):]
        x, y, c, chips = _place()
        me, sibling = 2 * x + y, (x, y, 1 - c)

        def half(a, chip, hc):
            rh = big[a].shape[0] // 2
            return outs[a].at[2 * chip[0] + chip[1], pl.ds(hc * rh, rh), :]

        first, passed = [], []
        for a in range(nb):
            rh = big[a].shape[0] // 2
            for j, chip in enumerate(chips):
                first.append(_remote(ins[a].at[pl.ds(c * rh, rh), :], half(a, (x, y), c),
                                     send_sems, recv_sems, 6 * a + j, (*chip, c)))
        for s in range(ns):
            for j, chip in enumerate(chips):
                first.append(_remote(ins[nb + s], outs[nb + s].at[me], send_sems, recv_sems,
                                     6 * nb + 3 * s + j, (*chip, c)))
        for cp in first:
            cp.start()
        for a in range(nb):
            for j, chip in enumerate(chips):
                _remote(half(a, chip, c), half(a, chip, c), send_sems, recv_sems, 6 * a + j, (*chip, c)).wait_recv()
                fwd = _remote(half(a, chip, c), half(a, chip, c), send_sems, recv_sems, 6 * a + 3 + j, sibling)
                fwd.start()
                passed.append(fwd)
        for s in range(ns):
            for j, chip in enumerate(chips):
                dst = outs[nb + s].at[2 * chip[0] + chip[1]]
                _remote(dst, dst, send_sems, recv_sems, 6 * nb + 3 * s + j, (*chip, c)).wait_recv()
        for a in range(nb):
            for j, chip in enumerate(chips):
                _remote(half(a, chip, 1 - c), half(a, chip, 1 - c), send_sems, recv_sems,
                        6 * a + 3 + j, sibling).wait_recv()
        for cp in first + passed:
            cp.wait_send()

    arrays = list(big) + list(small)
    out_shapes = [jax.ShapeDtypeStruct((N_CHIPS,) + a.shape, a.dtype) for a in arrays]
    return _launch(body, name, out_shapes, arrays, (n_sem, n_sem), sequencer, after)


def _swap_halves(parts, name, sequencer=None):
    n = len(parts)

    def body(*refs):
        ins, outs = refs[:n], refs[n:2 * n]
        send_sems, recv_sems = refs[2 * n:]
        x, y, c, _ = _place()
        copies = [_remote(ins[a].at[s, 1 - c], outs[a].at[s], send_sems, recv_sems, N_CHIPS * a + s, (x, y, 1 - c))
                  for a in range(n) for s in range(N_CHIPS)]
        for cp in copies:
            cp.start()
        for cp in copies:
            cp.wait()

    out_shapes = [jax.ShapeDtypeStruct((N_CHIPS,) + p.shape[2:], p.dtype) for p in parts]
    return _launch(body, name, out_shapes, parts, (N_CHIPS * n, N_CHIPS * n), sequencer)


def _scatter_to_owners(parts, name, sequencer=None):
    n = len(parts)

    def body(*refs):
        ins, outs = refs[:n], refs[n:2 * n]
        send_sems, recv_sems = refs[2 * n:]
        x, y, c, chips = _place()
        copies = [_remote(ins[a].at[2 * chip[0] + chip[1]], outs[a].at[j], send_sems, recv_sems,
                          3 * a + j, (*chip, c))
                  for a in range(n) for j, chip in enumerate(chips)]
        for cp in copies:
            cp.start()
        for cp in copies:
            cp.wait()

    out_shapes = [jax.ShapeDtypeStruct((3,) + p.shape[1:], p.dtype) for p in parts]
    return _launch(body, name, out_shapes, parts, (3 * n, 3 * n), sequencer)


def _send_to_sibling(halves, name, sequencer=None):
    n = len(halves)

    def body(*refs):
        ins, outs = refs[:n], refs[n:2 * n]
        send_sems, recv_sems = refs[2 * n:]
        x, y, c, _ = _place()
        copies = [_remote(ins[a], outs[a], send_sems, recv_sems, a, (x, y, 1 - c)) for a in range(n)]
        for cp in copies:
            cp.start()
        for cp in copies:
            cp.wait()

    out_shapes = [jax.ShapeDtypeStruct(p.shape, p.dtype) for p in halves]
    return _launch(body, name, out_shapes, halves, (n, n), sequencer)


N_DEV = 8


def _all_reduce_small(vec):
    def body(v_ref, gathered, total, send_sems, recv_sems):
        x, y, c, _ = _place()
        me = 4 * x + 2 * y + c
        gathered[me] = v_ref[...]
        copies = []
        for k in range(1, N_DEV):
            px = 1 - x if k & 4 else x
            py = 1 - y if k & 2 else y
            pc = 1 - c if k & 1 else c
            copies.append(_remote(v_ref, gathered.at[me], send_sems, recv_sems, k - 1, (px, py, pc)))
        for cp in copies:
            cp.start()
        for k, cp in enumerate(copies):
            cp.wait_send()
        for k in range(1, N_DEV):
            px = 1 - x if k & 4 else x
            py = 1 - y if k & 2 else y
            pc = 1 - c if k & 1 else c
            src = gathered.at[4 * px + 2 * py + pc]
            _remote(src, src, send_sems, recv_sems, k - 1, (px, py, pc)).wait_recv()
        acc = gathered[0]
        for dev in range(1, N_DEV):
            acc = acc + gathered[dev]
        total[...] = acc

    vm = pl.BlockSpec(memory_space=pltpu.VMEM)
    return pl.pallas_call(
        body, name="all_reduce_small",
        out_shape=(jax.ShapeDtypeStruct((N_DEV,) + vec.shape, F32), jax.ShapeDtypeStruct(vec.shape, F32)),
        in_specs=[vm], out_specs=(vm, vm),
        scratch_shapes=[pltpu.SemaphoreType.DMA((N_DEV - 1,)), pltpu.SemaphoreType.DMA((N_DEV - 1,))],
    )(vec)[1]


def _chip_sum(part, recv, c):
    _, _, rh, cols = part.shape
    tr = _tile(rh, 256)

    def body(c_ref, p_ref, r_ref, s_ref, sb_ref):
        s = p_ref[...] + r_ref[...]
        s_ref[...] = s
        sb_ref[...] = s.astype(BF16)

    blk = pl.BlockSpec((None, tr, cols), lambda s, i, c_ref: (s, i, 0))
    return pl.pallas_call(
        body, name="grad_chip_sum",
        out_shape=(jax.ShapeDtypeStruct(recv.shape, F32), jax.ShapeDtypeStruct(recv.shape, BF16)),
        grid_spec=pltpu.PrefetchScalarGridSpec(
            num_scalar_prefetch=1, grid=(N_CHIPS, rh // tr),
            in_specs=[pl.BlockSpec((None, None, tr, cols), lambda s, i, c_ref: (s, c_ref[0], i, 0)), blk],
            out_specs=(blk, blk)),
        compiler_params=_params(("parallel", "parallel")),
    )(c, part, recv)


def _owner_sum(own, recv, me):
    _, rh, cols = own.shape
    tr = _tile(rh, 256)

    def body(me_ref, o_ref, r0, r1, r2, g_ref):
        g_ref[...] = ((o_ref[...] + r0[...].astype(F32)) + r1[...].astype(F32)) + r2[...].astype(F32)

    slot = lambda j: pl.BlockSpec((None, tr, cols), lambda i, me_ref: (j, i, 0))
    return pl.pallas_call(
        body, name="grad_owner_sum", out_shape=jax.ShapeDtypeStruct((rh, cols), F32),
        grid_spec=pltpu.PrefetchScalarGridSpec(
            num_scalar_prefetch=1, grid=(rh // tr,),
            in_specs=[pl.BlockSpec((None, tr, cols), lambda i, me_ref: (me_ref[0], i, 0)), slot(0), slot(1), slot(2)],
            out_specs=pl.BlockSpec((tr, cols), lambda i, me_ref: (i, 0))),
        compiler_params=_params(("parallel",)),
    )(me, own, recv, recv, recv)


def _adamw_math(w, g, m, v):
    c1 = 1.0 / (1.0 - ADAM_B1 ** ADAM_STEP)
    c2 = 1.0 / (1.0 - ADAM_B2 ** ADAM_STEP)
    nm = ADAM_B1 * m + (1.0 - ADAM_B1) * g
    nv = ADAM_B2 * v + (1.0 - ADAM_B2) * (g * g)
    return -ADAM_LR * ((nm * c1) / (jnp.sqrt(nv * c2) + ADAM_EPS) + ADAM_WD * w), nm, nv


def _adamw_unit_rows(w, g, m, v, name):
    rows, _, cols = w.shape
    tr = max(d for d in range(1, 33) if rows % d == 0)

    def body(w_ref, g_ref, m_ref, v_ref, d_ref, nm_ref, nv_ref):
        d_ref[...], nm_ref[...], nv_ref[...] = _adamw_math(w_ref[...], g_ref[...], m_ref[...], v_ref[...])

    blk = pl.BlockSpec((tr, 1, cols), lambda i: (i, 0, 0))
    shape = jax.ShapeDtypeStruct(w.shape, F32)
    return pl.pallas_call(
        body, name=name, out_shape=(shape, shape, shape), grid=(rows // tr,),
        in_specs=[blk, blk, blk, blk], out_specs=(blk, blk, blk),
        compiler_params=_params(("parallel",)),
    )(w, g, m, v)


def _divisor_tile(n, want):
    return max(d for d in range(ROW_TILE, want + 1, ROW_TILE) if n % d == 0)


def _adamw(w, g, m, v, name):
    if w.ndim == 3:
        return _adamw_unit_rows(w, g, m, v, name)
    rows, cols = w.shape
    tr = _divisor_tile(rows, 2048) if rows % 8 == 0 else rows
    c1 = 1.0 / (1.0 - ADAM_B1 ** ADAM_STEP)
    c2 = 1.0 / (1.0 - ADAM_B2 ** ADAM_STEP)

    def body(w_ref, g_ref, m_ref, v_ref, d_ref, nm_ref, nv_ref):
        gv = g_ref[...]
        nm = ADAM_B1 * m_ref[...] + (1.0 - ADAM_B1) * gv
        nv = ADAM_B2 * v_ref[...] + (1.0 - ADAM_B2) * (gv * gv)
        d_ref[...] = -ADAM_LR * ((nm * c1) / (jnp.sqrt(nv * c2) + ADAM_EPS) + ADAM_WD * w_ref[...])
        nm_ref[...] = nm
        nv_ref[...] = nv

    blk = pl.BlockSpec((tr, cols), lambda i: (i, 0))
    shape = jax.ShapeDtypeStruct((rows, cols), F32)
    return pl.pallas_call(
        body, name=name, out_shape=(shape, shape, shape), grid=(rows // tr,),
        in_specs=[blk, blk, blk, blk], out_specs=(blk, blk, blk),
        compiler_params=_params(("parallel",)),
    )(w, g, m, v)


def _adamw_halves(w, g_own, g_sib, m, v, c, name):
    _, rows, cols = w.shape
    rh = rows // 2
    tr = _tile(rh, 256)
    per = rh // tr
    c1 = 1.0 / (1.0 - ADAM_B1 ** ADAM_STEP)
    c2 = 1.0 / (1.0 - ADAM_B2 ** ADAM_STEP)

    def body(c_ref, w_ref, go_ref, gs_ref, m_ref, v_ref, g_ref, d_ref, nm_ref, nv_ref):
        own = pl.program_id(0) // per == c_ref[0]
        gv = jnp.where(own, go_ref[...], gs_ref[...])
        nm = ADAM_B1 * m_ref[...] + (1.0 - ADAM_B1) * gv
        nv = ADAM_B2 * v_ref[...] + (1.0 - ADAM_B2) * (gv * gv)
        g_ref[...] = gv
        d_ref[...] = -ADAM_LR * ((nm * c1) / (jnp.sqrt(nv * c2) + ADAM_EPS) + ADAM_WD * w_ref[...])
        nm_ref[...] = nm
        nv_ref[...] = nv

    blk = pl.BlockSpec((None, tr, cols), lambda i, c_ref: (0, i, 0))
    half = pl.BlockSpec((tr, cols), lambda i, c_ref: (i % per, 0))
    shape = jax.ShapeDtypeStruct((1, rows, cols), F32)
    return pl.pallas_call(
        body, name=name, out_shape=(shape, shape, shape, shape),
        grid_spec=pltpu.PrefetchScalarGridSpec(
            num_scalar_prefetch=1, grid=(rows // tr,),
            in_specs=[blk, half, half, blk, blk], out_specs=(blk, blk, blk, blk)),
        compiler_params=_params(("parallel",)),
    )(c, w, g_own, g_sib, m, v)


def _by_shard(name, full):
    if name == "w_in":
        st = full.reshape(full.shape[0], N_CHIPS, -1).transpose(1, 0, 2)
    elif name == "w_ff1":
        st = full
    else:
        st = full.reshape(N_CHIPS, -1, full.shape[1])
    return st.reshape(N_CHIPS, 2, st.shape[1] // 2, st.shape[2])


class _GradReducer:
    def __init__(self, w, m, v, my_c, my_chip):
        self.w, self.m, self.v, self.my_c, self.my_chip = w, m, v, my_c, my_chip
        self.in_flight = []
        self.computed = []
        self.anchor = None
        self.done = {}

    def step(self, name=None, full=None, anchor=None):
        stages, self.in_flight, self.computed, self.anchor = self.in_flight, [], [], anchor
        for stage in [s for s in stages if not getattr(s, "long", False)]:
            self._advance(stage)
        if name is not None:
            self.in_flight.append(self._swap(name, _by_shard(name, full)))
        for stage in [s for s in stages if getattr(s, "long", False)]:
            self._advance(stage)
        return self.computed

    def _held(self, value):
        if self.anchor is None:
            return value
        return lax.optimization_barrier((value, self.anchor))[0]

    def _advance(self, stage):
        nxt = stage()
        if nxt is not None:
            self.in_flight.append(nxt)

    def finish(self):
        while self.in_flight:
            self.step()
        return self.done

    def _swap(self, name, part):
        got, = _swap_halves([part], "grad_swap_" + name, SIBLING_EXCHANGE)

        def scatter():
            total, total_bf16 = _chip_sum(part, self._held(got), self.my_c)
            self.computed.append(total_bf16)
            recv, = _scatter_to_owners([total_bf16], "grad_scatter_" + name, CHIP_EXCHANGE)

            def send():
                half = _owner_sum(total, self._held(recv), self.my_chip.reshape(1))
                self.computed.append(half)
                sib, = _send_to_sibling([half], "grad_send_" + name, SIBLING_EXCHANGE)

                def update():
                    update_fn = _adamw_halves if self.w[name].shape[-1] % HEAD_DIM == 0 else _adamw_minor_rows
                    self.done[name] = update_fn(self.w[name], half, self._held(sib), self.m[name], self.v[name],
                                                self.my_c, "adamw_" + name)
                    self.computed.append(self.done[name][0])
                return update
            return lambda: send
        scatter.long = True
        return scatter


def _adamw_minor_rows(w, g_own, g_sib, m, v, c, name):
    _, rows, cols = w.shape
    turned = lambda a: jnp.transpose(a, (2, 0, 1))
    back = lambda a: jnp.transpose(a, (1, 2, 0))
    lower, upper = jnp.where(c[0] == 0, g_own, g_sib), jnp.where(c[0] == 0, g_sib, g_own)
    g = jnp.concatenate([lower, upper], axis=0).T.reshape(cols, 1, rows)
    delta, new_m, new_v = _adamw(turned(w), g, turned(m), turned(v), name)
    return back(g), back(delta), back(new_m), back(new_v)


SMALL = ("gdn_a_log", "gdn_dt_bias", "gdn_norm_w", "hgrn_lb_logits", "hgrn_norm_w",
         "norm_mix_w", "norm_ffn_w", "norm_final_w")
BIG = ("w_in", "w_out", "w_ff1", "w_ff2")
ORDER = ("w_in", "conv_w", "gdn_a_log", "gdn_dt_bias", "gdn_norm_w", "hgrn_lb_logits", "hgrn_norm_w",
         "w_out", "norm_mix_w", "norm_ffn_w", "w_ff1", "w_ff2", "norm_final_w")


def _pack(pieces):
    flat = jnp.concatenate([p.reshape(-1).astype(F32) for p in pieces])
    rows = -(-flat.shape[0] // (8 * HEAD_DIM)) * 8
    return jnp.pad(flat, (0, rows * HEAD_DIM - flat.shape[0])).reshape(rows, HEAD_DIM)


def _unpack(packed, shapes):
    flat, out, at = packed.reshape(-1), [], 0
    for s in shapes:
        n = 1
        for dim in s:
            n *= dim
        out.append(flat[at:at + n].reshape(s))
        at += n
    return out


def kernel(x, w_in, conv_w, gdn_a_log, gdn_dt_bias, gdn_norm_w, hgrn_lb_logits, hgrn_norm_w, w_out, norm_mix_w, norm_ffn_w, w_ff1, w_ff2, norm_final_w, loss_target, m_w_in, m_conv_w, m_gdn_a_log, m_gdn_dt_bias, m_gdn_norm_w, m_hgrn_lb_logits, m_hgrn_norm_w, m_w_out, m_norm_mix_w, m_norm_ffn_w, m_w_ff1, m_w_ff2, m_norm_final_w, v_w_in, v_conv_w, v_gdn_a_log, v_gdn_dt_bias, v_gdn_norm_w, v_hgrn_lb_logits, v_hgrn_norm_w, v_w_out, v_norm_mix_w, v_norm_ffn_w, v_w_ff1, v_w_ff2, v_norm_final_w):
    w = dict(w_in=w_in, conv_w=conv_w, gdn_a_log=gdn_a_log, gdn_dt_bias=gdn_dt_bias, gdn_norm_w=gdn_norm_w,
             hgrn_lb_logits=hgrn_lb_logits, hgrn_norm_w=hgrn_norm_w, w_out=w_out, norm_mix_w=norm_mix_w,
             norm_ffn_w=norm_ffn_w, w_ff1=w_ff1, w_ff2=w_ff2, norm_final_w=norm_final_w)
    m = dict(w_in=m_w_in, conv_w=m_conv_w, gdn_a_log=m_gdn_a_log, gdn_dt_bias=m_gdn_dt_bias,
             gdn_norm_w=m_gdn_norm_w, hgrn_lb_logits=m_hgrn_lb_logits, hgrn_norm_w=m_hgrn_norm_w,
             w_out=m_w_out, norm_mix_w=m_norm_mix_w, norm_ffn_w=m_norm_ffn_w, w_ff1=m_w_ff1, w_ff2=m_w_ff2,
             norm_final_w=m_norm_final_w)
    v = dict(w_in=v_w_in, conv_w=v_conv_w, gdn_a_log=v_gdn_a_log, gdn_dt_bias=v_gdn_dt_bias,
             gdn_norm_w=v_gdn_norm_w, hgrn_lb_logits=v_hgrn_lb_logits, hgrn_norm_w=v_hgrn_norm_w,
             w_out=v_w_out, norm_mix_w=v_norm_mix_w, norm_ffn_w=v_norm_ffn_w, w_ff1=v_w_ff1, w_ff2=v_w_ff2,
             norm_final_w=v_norm_final_w)
    d = x.shape[-1]
    h = d // (2 * HEAD_DIM)
    my_c = lax.axis_index("c").astype(jnp.int32).reshape(1)
    my_chip = (2 * lax.axis_index("x") + lax.axis_index("y")).astype(jnp.int32)

    shards = [w[n][0].astype(BF16) for n in BIG]
    conv_shard = jnp.pad(conv_w[0], ((0, 8 - CONV_W), (0, 0)))
    first = _gather_weights(shards[:1], [conv_shard], "gather_in_proj")
    own_slot = lambda st, own: lax.dynamic_update_index_in_dim(st, own, my_chip, 0)
    f_in, f_conv = own_slot(first[0], shards[0]), own_slot(first[1], conv_shard)
    cols = lambda st: st.transpose(1, 0, 2).reshape(st.shape[1], -1)
    w_main, w_ab = _split_w_in(cols(f_in), h)
    conv_full = cols(f_conv[:, :CONV_W])
    rest = _gather_weights(shards[1:], [], "gather_rest", GATHER_EXCHANGE, after=[w_ab])
    f_out, f_ff1, f_ff2 = (own_slot(st, own) for st, own in zip(rest, shards[1:]))

    reducer = _GradReducer(w, m, v, my_c, my_chip)
    loss, dx, g = _local_step(
        x[0], loss_target[0], w_main, w_ab, conv_full, gdn_a_log[0], gdn_dt_bias[0], gdn_norm_w[0],
        hgrn_lb_logits, hgrn_norm_w[0], f_out.reshape(-1, d), norm_mix_w[0], norm_ffn_w[0],
        f_ff1, f_ff2.reshape(-1, d), norm_final_w, reducer)

    grads, delta, new_m, new_v = {}, {}, {}, {}
    for n, out in reducer.finish().items():
        grads[n], delta[n], new_m[n], new_v[n] = out

    small_shapes = [w[n].shape for n in SMALL] + [conv_full.shape, (1,)]
    total = _all_reduce_small(_pack([g[n] for n in SMALL] + [g["conv_w"], loss[0, :1]]))
    *small_grads, conv_grad, loss_sum = _unpack(total, small_shapes)
    for n, sg in zip(SMALL, small_grads):
        grads[n] = sg
    shard_cols = conv_w.shape[-1]
    grads["conv_w"] = lax.dynamic_slice_in_dim(conv_grad, my_chip * shard_cols, shard_cols, axis=1)[None]

    packed_names = SMALL + ("conv_w",)
    packed = [_pack([t[n] for n in packed_names]) for t in (w, grads, m, v)]
    outs = _adamw(*packed, "adamw_small")
    shapes = [w[n].shape for n in packed_names]
    for res, o in zip((delta, new_m, new_v), outs):
        for n, a in zip(packed_names, _unpack(o, shapes)):
            res[n] = a

    return (loss_sum.reshape(()), dx[None], *[grads[n] for n in ORDER], *[delta[n] for n in ORDER],
            *[new_m[n] for n in ORDER], *[new_v[n] for n in ORDER])
```

```python
import functools

import jax
import jax.numpy as jnp
from jax import lax
from jax.experimental import pallas as pl
from jax.experimental.pallas import tpu as pltpu
from jax.experimental.pallas import tpu_sc as plsc

F32 = jnp.float32
BF16 = jnp.bfloat16

HEAD_DIM = 128
CHUNK = 128
SUB = 16
EXP_CAP = 80.0
NORM_EPS = 1e-6
L2_EPS = 1e-6
CONV_W = 4
VMEM_LIMIT = 56 * 1024 * 1024

ADAM_LR, ADAM_B1, ADAM_B2, ADAM_EPS, ADAM_WD, ADAM_STEP = 1e-3, 0.9, 0.999, 1e-8, 0.01, 10

NN = ((1,), (0,))
NT = ((1,), (1,))
TN = ((0,), (0,))
MESH = pl.DeviceIdType.MESH


def _dot(a, b, dims):
    return lax.dot_general(a.astype(BF16), b.astype(BF16), (dims, ((), ())),
                           preferred_element_type=F32)


def _split(a):
    hi = a.astype(BF16)
    return hi, (a - hi.astype(F32)).astype(BF16)


def _dot3(a, b, dims):
    ah, al = _split(a)
    bh, bl = _split(b)
    d = lambda x, y: lax.dot_general(x, y, (dims, ((), ())), preferred_element_type=F32)
    return d(ah, bh) + (d(ah, bl) + d(al, bh))


def _sigmoid(x):
    return 1.0 / (1.0 + jnp.exp(-x))


def _silu(x):
    return x * _sigmoid(x)


def _dsilu(x):
    s = _sigmoid(x)
    return s * (1.0 + x * (1.0 - s))


def _softplus(x):
    e = jnp.exp(-jnp.abs(x))
    u = 1.0 + e
    log1p = jnp.where(u == 1.0, e, jnp.log(u) * (e / jnp.where(u == 1.0, 1.0, u - 1.0)))
    return jnp.maximum(x, 0.0) + log1p


def _iota(shape, axis):
    return lax.broadcasted_iota(jnp.int32, shape, axis)


def _cumsum_rows(x):
    n = x.shape[0]
    row = _iota(x.shape, 0)
    s = 1
    while s < n:
        x = x + jnp.where(row >= s, pltpu.roll(x, s, 0), 0.0)
        s *= 2
    return x


def _rev_cumsum_rows(x):
    return jnp.sum(x, axis=0, keepdims=True) - _cumsum_rows(x) + x


def _params(sem):
    return pltpu.CompilerParams(dimension_semantics=sem, vmem_limit_bytes=VMEM_LIMIT)


ROW_TILE = 8
HEADS_PER_STEP = 8


def _hps(h):
    return min(HEADS_PER_STEP, h)


def _head_view(ref, hb):
    if len(ref.shape) == 2:
        return ref.at[:, pl.ds(hb * HEAD_DIM, HEAD_DIM)]
    return ref.at[hb]


class _Staged:
    def __init__(self, ref, load):
        self.ref = ref
        self.loaded = ref[...] if load else None
        self.written = None

    def __getitem__(self, idx):
        return self.loaded

    def __setitem__(self, idx, value):
        self.written = value


def _each_head(one_head, n_in):
    def body(*refs):
        @pl.when(pl.program_id(1) == 0)
        def _():
            refs[-1][...] = jnp.zeros_like(refs[-1])

        last = len(refs) - 1
        staged = [[_Staged(_head_view(r, hb), i < n_in or i == last) for i, r in enumerate(refs)]
                  for hb in range(refs[-1].shape[0])]
        running = [one_head(*per_head) for per_head in staged]
        while running:
            for gen in list(running):
                try:
                    next(gen)
                except StopIteration:
                    running.remove(gen)
        for per_head in staged:
            for s in per_head:
                if s.written is not None:
                    s.ref[...] = s.written
    return body


def _tile(n, want):
    t = min(n, want)
    while n % t:
        t //= 2
    return t


def _mm(a, b, mode, out_dtypes, name, epi=None, extras=(), tm=1024, tn=1024, tk=2048,
        b_stacked=False, out_stacked=False):
    if mode == "tn":
        kdim, m = a.shape
    else:
        m, kdim = a.shape
    if b_stacked:
        n = N_CHIPS * b.shape[2] if mode == "nn" else b.shape[1]
        kdim_b = b.shape[1] if mode == "nn" else N_CHIPS * b.shape[2]
        assert kdim_b == kdim
    else:
        n = b.shape[0] if mode == "nt" else b.shape[1]
    per_shard = (n if (mode == "nn" or out_stacked) else kdim) // N_CHIPS
    tm, tn, tk = _tile(m, tm), _tile(n, tn), _tile(kdim, tk)
    if (b_stacked and mode == "nn") or out_stacked:
        tn = _tile(per_shard, tn)
    if b_stacked and mode == "nt":
        tk = _tile(per_shard, tk)
    nk = kdim // tk
    dims = {"nn": NN, "nt": NT, "tn": TN}[mode]
    a_spec = (pl.BlockSpec((tk, tm), lambda i, j, k: (k, i)) if mode == "tn"
              else pl.BlockSpec((tm, tk), lambda i, j, k: (i, k)))
    if b_stacked and mode == "nn":
        per = per_shard // tn
        b_spec = pl.BlockSpec((None, tk, tn), lambda i, j, k: (j // per, k, j % per))
    elif b_stacked:
        per = per_shard // tk
        b_spec = pl.BlockSpec((None, tn, tk), lambda i, j, k: (k // per, j, k % per))
    else:
        b_spec = (pl.BlockSpec((tn, tk), lambda i, j, k: (j, k)) if mode == "nt"
                  else pl.BlockSpec((tk, tn), lambda i, j, k: (k, j)))
    mn_spec = pl.BlockSpec((tm, tn), lambda i, j, k: (i, j))
    if out_stacked:
        per_o = per_shard // tn
        out_spec = pl.BlockSpec((None, tm, tn), lambda i, j, k: (j // per_o, i, j % per_o))
        out_shape = (N_CHIPS, m, per_shard)
    else:
        out_spec, out_shape = mn_spec, (m, n)
    ne, no = len(extras), len(out_dtypes)
    if epi is None:
        epi = lambda acc: (acc,)

    def body(a_ref, b_ref, *rest):
        extra_refs, out_refs = rest[:ne], rest[ne:ne + no]
        part = _dot(a_ref[...], b_ref[...], dims)

        def finish(total):
            outs = epi(total, *[r[...] for r in extra_refs])
            for o_ref, o in zip(out_refs, outs):
                o_ref[...] = o.astype(o_ref.dtype)

        if nk == 1:
            finish(part)
            return
        acc = rest[-1]
        k = pl.program_id(2)

        @pl.when(k == 0)
        def _():
            acc[...] = part

        @pl.when(jnp.logical_and(k > 0, k < nk - 1))
        def _():
            acc[...] += part

        @pl.when(k == nk - 1)
        def _():
            finish(acc[...] + part)

    outs = pl.pallas_call(
        body, name=name,
        out_shape=tuple(jax.ShapeDtypeStruct(out_shape, d) for d in out_dtypes),
        grid=(m // tm, n // tn, nk),
        in_specs=[a_spec, b_spec] + [mn_spec] * ne,
        out_specs=tuple(out_spec for _ in out_dtypes),
        scratch_shapes=[pltpu.VMEM((tm, tn), F32)] if nk > 1 else [],
        compiler_params=_params(("parallel", "parallel", "arbitrary")),
    )(a, b, *extras)
    return outs if no > 1 else outs[0]


ROWS = 256


def _rms_fwd(x, w, name):
    t, d = x.shape
    tr = _tile(t, ROWS)

    def body(x_ref, w_ref, n_ref):
        xv = x_ref[...]
        r = lax.rsqrt(jnp.mean(xv * xv, axis=-1, keepdims=True) + NORM_EPS)
        n_ref[...] = (xv * r * w_ref[...]).astype(n_ref.dtype)

    return pl.pallas_call(
        body, name=name, out_shape=jax.ShapeDtypeStruct((t, d), BF16), grid=(t // tr,),
        in_specs=[pl.BlockSpec((tr, d), lambda i: (i, 0)), pl.BlockSpec((1, d), lambda i: (0, 0))],
        out_specs=pl.BlockSpec((tr, d), lambda i: (i, 0)),
        compiler_params=_params(("parallel",)),
    )(x, w.reshape(1, d))


def _rms_bwd(dn, x, w, dres, name):
    t, d = x.shape
    tr = _tile(t, ROWS)

    def body(dn_ref, x_ref, w_ref, dres_ref, dx_ref, dxb_ref, dw_ref):
        i = pl.program_id(0)
        xv, dnv = x_ref[...], dn_ref[...]
        r = lax.rsqrt(jnp.mean(xv * xv, axis=-1, keepdims=True) + NORM_EPS)
        xh = xv * r
        dxh = dnv * w_ref[...]
        dx = dres_ref[...] + r * (dxh - xh * jnp.mean(dxh * xh, axis=-1, keepdims=True))
        dx_ref[...] = dx
        dxb_ref[...] = dx.astype(BF16)

        @pl.when(i == 0)
        def _():
            dw_ref[...] = jnp.zeros_like(dw_ref)

        dw_ref[...] += jnp.sum(dnv * xh, axis=0, keepdims=True)

    row = pl.BlockSpec((tr, d), lambda i: (i, 0))
    vec = pl.BlockSpec((1, d), lambda i: (0, 0))
    return pl.pallas_call(
        body, name=name,
        out_shape=(jax.ShapeDtypeStruct((t, d), F32), jax.ShapeDtypeStruct((t, d), BF16),
                   jax.ShapeDtypeStruct((1, d), F32)),
        grid=(t // tr,), in_specs=[row, row, vec, row], out_specs=(row, row, vec),
        compiler_params=_params(("arbitrary",)),
    )(dn, x, w.reshape(1, d), dres)


def _loss_head(h, w, target):
    t, d = h.shape
    tr = _tile(t, ROWS)

    def body(h_ref, w_ref, t_ref, loss_ref, dh_ref, dhb_ref, dw_ref):
        i = pl.program_id(0)
        hv, wv = h_ref[...], w_ref[...]
        r = lax.rsqrt(jnp.mean(hv * hv, axis=-1, keepdims=True) + NORM_EPS)
        hh = hv * r
        err = hh * wv - t_ref[...]
        dout = err * (1.0 / d)
        dhh = dout * wv
        dh = r * (dhh - hh * jnp.mean(dhh * hh, axis=-1, keepdims=True))
        dh_ref[...] = dh
        dhb_ref[...] = dh.astype(BF16)

        @pl.when(i == 0)
        def _():
            dw_ref[...] = jnp.zeros_like(dw_ref)
            loss_ref[...] = jnp.zeros_like(loss_ref)

        dw_ref[...] += jnp.sum(dout * hh, axis=0, keepdims=True)
        loss_ref[...] += jnp.full((1, 128), 0.5 / d, F32) * jnp.sum(err * err)

    row = pl.BlockSpec((tr, d), lambda i: (i, 0))
    vec = pl.BlockSpec((1, d), lambda i: (0, 0))
    lspec = pl.BlockSpec((1, 128), lambda i: (0, 0))
    return pl.pallas_call(
        body, name="loss_head",
        out_shape=(jax.ShapeDtypeStruct((1, 128), F32), jax.ShapeDtypeStruct((t, d), F32),
                   jax.ShapeDtypeStruct((t, d), BF16), jax.ShapeDtypeStruct((1, d), F32)),
        grid=(t // tr,), in_specs=[row, vec, row], out_specs=(lspec, row, row, vec),
        compiler_params=_params(("arbitrary",)),
    )(h, w.reshape(1, d), target)


def _inv_unit_lower(a):
    c = a.shape[0]
    eye = (_iota((c, c), 0) == _iota((c, c), 1)).astype(F32)
    x = eye - a
    p = _dot3(a, a, NN)
    yield
    n = 2
    while n < c:
        x = x + _dot3(x, p, NN)
        n *= 2
        if n < c:
            p = _dot3(p, p, NN)
        yield
    return x


def _gdn_chunk(q, k, v, beta, g):
    c = q.shape[0]
    row, col = _iota((c, c), 0), _iota((c, c), 1)
    gc = _cumsum_rows(g)
    diff = gc - gc.T
    dec = jnp.where(row >= col, jnp.exp(jnp.minimum(diff, 0.0)), 0.0)
    dec_s = jnp.where(row > col, dec, 0.0)
    gam = jnp.exp(gc)
    g_last = jnp.sum(g, axis=0, keepdims=True)
    kk = _dot(k, k, NT)
    a = beta * kk * dec_s
    p = _dot(q, k, NT) * dec
    e_end = jnp.exp(g_last - gc)
    return dict(dec=dec, dec_s=dec_s, gam=gam, gam_last=jnp.exp(g_last), e_end=e_end,
                k_end=k * e_end, kk=kk, a=a, p=p)


def _gdn_fwd(q, k, v, beta_bc, g_bc):
    t = q.shape[0]
    h = q.shape[1] // HEAD_DIM
    nc = t // CHUNK

    def body(q_ref, k_ref, v_ref, b_ref, g_ref, o_ref, s_ref, t_ref, state):
        qv, kv, vv, beta = q_ref[...], k_ref[...], v_ref[...], b_ref[...]
        ch = _gdn_chunk(qv, kv, vv, beta, g_ref[...])
        yield
        tm = yield from _inv_unit_lower(ch["a"])
        sol = _dot(tm, jnp.concatenate([beta * vv, beta * ch["gam"] * kv], axis=1), NN)
        yield
        u_v, w = sol[:, :HEAD_DIM], sol[:, HEAD_DIM:]
        s0 = state[...]
        u = u_v - _dot(w, s0, NN)
        yield
        o_ref[...] = _dot(qv * ch["gam"], s0, NN) + _dot(ch["p"], u, NN)
        s_ref[...] = s0
        t_ref[...] = tm
        state[...] = ch["gam_last"] * s0 + _dot(ch["k_end"], u, TN)

    tok = pl.BlockSpec((CHUNK, _hps(h) * HEAD_DIM), lambda hh, c: (c, hh))
    bc = pl.BlockSpec((_hps(h), CHUNK, HEAD_DIM), lambda hh, c: (hh, c, 0))
    mat = pl.BlockSpec((_hps(h), None, HEAD_DIM, HEAD_DIM), lambda hh, c: (hh, c, 0, 0))
    return pl.pallas_call(
        _each_head(body, 5), name="gdn_fwd",
        out_shape=(jax.ShapeDtypeStruct(q.shape, F32),
                   jax.ShapeDtypeStruct((h, nc, HEAD_DIM, HEAD_DIM), F32),
                   jax.ShapeDtypeStruct((h, nc, CHUNK, CHUNK), F32)),
        grid=(h // _hps(h), nc), in_specs=[tok, tok, tok, bc, bc], out_specs=(tok, mat, mat),
        scratch_shapes=[pltpu.VMEM((_hps(h), HEAD_DIM, HEAD_DIM), F32)],
        compiler_params=_params(("parallel", "arbitrary")),
    )(q, k, v, beta_bc, g_bc)


def _gdn_bwd(q, k, v, beta_bc, g_bc, states, invs, do, do_blk=0):
    t = q.shape[0]
    h = q.shape[1] // HEAD_DIM
    nc = t // CHUNK

    def body(q_ref, k_ref, v_ref, b_ref, g_ref, s_ref, t_ref, do_ref,
             dq_ref, dk_ref, dv_ref, db_ref, dg_ref, dstate):
        qv, kv, vv, beta = q_ref[...], k_ref[...], v_ref[...], b_ref[...]
        dov, s0, tm, ds1 = do_ref[...], s_ref[...], t_ref[...], dstate[...]
        ch = _gdn_chunk(qv, kv, vv, beta, g_ref[...])
        yield
        gam, dec, dec_s, kk = ch["gam"], ch["dec"], ch["dec_s"], ch["kk"]
        r_v, r_w = beta * vv, beta * gam * kv
        sol = _dot(tm, jnp.concatenate([r_v, r_w], axis=1), NN)
        yield
        u_v, w = sol[:, :HEAD_DIM], sol[:, HEAD_DIM:]
        u = u_v - _dot(w, s0, NN)
        qg = qv * gam
        yield

        du = _dot(ch["p"], dov, TN) + _dot(ch["k_end"], ds1, NN)
        dp = _dot(dov, u, NT)
        dpd = dp * dec
        dqg = _dot(dov, s0, NT)
        dk_end = _dot(u, ds1, NT)
        yield
        dq = dqg * gam + _dot(dpd, kv, NN)
        dk = _dot(dpd, qv, TN) + dk_end * ch["e_end"]
        dstate[...] = _dot(qg, dov, TN) + ch["gam_last"] * ds1 - _dot(w, du, TN)
        dw = -_dot(du, s0, NT)
        yield
        dr = _dot(tm, jnp.concatenate([du, dw], axis=1), TN)
        yield
        dr_v, dr_w = dr[:, :HEAD_DIM], dr[:, HEAD_DIM:]
        da = -_dot(dr, sol, NT)
        yield
        dkk = da * beta * dec_s
        dk = dk + _dot(dkk, kv, NN) + _dot(dkk, kv, TN) + beta * gam * dr_w
        dbeta = (jnp.sum(da * kk * dec_s, axis=1, keepdims=True)
                 + jnp.sum(dr_v * vv + dr_w * gam * kv, axis=1, keepdims=True))

        pair = dp * ch["p"] + da * ch["a"]
        end = jnp.sum(dk_end * ch["k_end"], axis=1, keepdims=True)
        dgc = (jnp.sum(pair - pair.T, axis=1, keepdims=True)
               + jnp.sum(dqg * qg + dr_w * r_w, axis=1, keepdims=True) - end)
        at_end = jnp.sum(end) + ch["gam_last"] * jnp.sum(s0 * ds1)
        dgc = jnp.broadcast_to(dgc, (CHUNK, HEAD_DIM))
        dgc = dgc + jnp.where(_iota((CHUNK, HEAD_DIM), 0) == CHUNK - 1, at_end, 0.0)
        dq_ref[...] = dq
        dk_ref[...] = dk
        dv_ref[...] = beta * dr_v
        db_ref[...] = jnp.broadcast_to(dbeta, (CHUNK, HEAD_DIM)).T[:ROW_TILE]
        dg_ref[...] = _rev_cumsum_rows(dgc).T[:ROW_TILE]

    rev = lambda c: nc - 1 - c
    tok = pl.BlockSpec((CHUNK, _hps(h) * HEAD_DIM), lambda hh, c: (rev(c), hh))
    bc = pl.BlockSpec((_hps(h), CHUNK, HEAD_DIM), lambda hh, c: (hh, rev(c), 0))
    mat = pl.BlockSpec((_hps(h), None, HEAD_DIM, HEAD_DIM), lambda hh, c: (hh, rev(c), 0, 0))
    tok_shape = jax.ShapeDtypeStruct(q.shape, F32)
    row_shape = jax.ShapeDtypeStruct((h, nc, ROW_TILE, CHUNK), F32)
    rows = pl.BlockSpec((_hps(h), None, ROW_TILE, CHUNK), lambda hh, c: (hh, rev(c), 0, 0))
    return pl.pallas_call(
        _each_head(body, 8), name="gdn_bwd",
        out_shape=(tok_shape, tok_shape, tok_shape, row_shape, row_shape),
        grid=(h // _hps(h), nc),
        in_specs=[tok, tok, tok, bc, bc, mat, mat,
                  pl.BlockSpec((CHUNK, _hps(h) * HEAD_DIM), lambda hh, c: (rev(c), do_blk // _hps(h) + hh))],
        out_specs=(tok, tok, tok, rows, rows),
        scratch_shapes=[pltpu.VMEM((_hps(h), HEAD_DIM, HEAD_DIM), F32)],
        compiler_params=_params(("parallel", "arbitrary")),
    )(q, k, v, beta_bc, g_bc, states, invs, do)


def _hgrn_chunk(q, k, lf):
    c = q.shape[0]
    row = _iota((c, HEAD_DIM), 0)
    b = _cumsum_rows(lf)
    q_subs, k_facs, a_rows = [], [], []
    for x in range(c // SUB):
        b_start = jnp.sum(jnp.where(row < x * SUB, lf, 0.0), axis=0, keepdims=True)
        q_x = (q * jnp.exp(jnp.minimum(b - b_start, 0.0)))[x * SUB:(x + 1) * SUB]
        k_fac = jnp.where(row < (x + 1) * SUB, jnp.exp(jnp.minimum(b_start - b, EXP_CAP)), 0.0)
        q_subs.append(q_x)
        k_facs.append(k_fac)
        a_rows.append(_dot(q_x, k * k_fac, NT))
    a = jnp.concatenate(a_rows, axis=0)
    a = jnp.where(_iota((c, c), 0) >= _iota((c, c), 1), a, 0.0)
    b_last = jnp.sum(lf, axis=0, keepdims=True)
    return dict(b=b, a=a, q_subs=q_subs, k_facs=k_facs, e_b=jnp.exp(b),
                e_end=jnp.exp(b_last - b), e_last=jnp.exp(b_last))


def _hgrn_fwd(q, k, v, lf, v_blk=0):
    t = q.shape[0]
    h = q.shape[1] // HEAD_DIM
    nc = t // CHUNK

    def body(q_ref, k_ref, v_ref, lf_ref, o_ref, s_ref, state):
        qv, kv, vv = q_ref[...], k_ref[...], v_ref[...]
        ch = _hgrn_chunk(qv, kv, lf_ref[...])
        yield
        s0 = state[...]
        o_ref[...] = _dot(qv * ch["e_b"], s0, NT) + _dot(ch["a"], vv, NN)
        s_ref[...] = s0
        state[...] = s0 * ch["e_last"] + _dot(vv, kv * ch["e_end"], TN)

    tok = pl.BlockSpec((CHUNK, _hps(h) * HEAD_DIM), lambda hh, c: (c, hh))
    mat = pl.BlockSpec((_hps(h), None, HEAD_DIM, HEAD_DIM), lambda hh, c: (hh, c, 0, 0))
    return pl.pallas_call(
        _each_head(body, 4), name="hgrn_fwd",
        out_shape=(jax.ShapeDtypeStruct(q.shape, F32),
                   jax.ShapeDtypeStruct((h, nc, HEAD_DIM, HEAD_DIM), F32)),
        grid=(h // _hps(h), nc),
        in_specs=[tok, tok, pl.BlockSpec((CHUNK, _hps(h) * HEAD_DIM), lambda hh, c: (c, v_blk // _hps(h) + hh)), tok],
        out_specs=(tok, mat),
        scratch_shapes=[pltpu.VMEM((_hps(h), HEAD_DIM, HEAD_DIM), F32)],
        compiler_params=_params(("parallel", "arbitrary")),
    )(q, k, v, lf)


def _hgrn_bwd(q, k, v, lf, states, do, v_blk=0, do_blk=0):
    t = q.shape[0]
    nc = t // CHUNK
    h = q.shape[1] // HEAD_DIM

    def body(q_ref, k_ref, v_ref, lf_ref, s_ref, do_ref, dq_ref, dk_ref, dv_ref, dlf_ref, dstate):
        qv, kv, vv, dov, s0 = q_ref[...], k_ref[...], v_ref[...], do_ref[...], s_ref[...]
        ds1 = dstate[...]
        ch = _hgrn_chunk(qv, kv, lf_ref[...])
        yield
        c = CHUNK
        row = _iota((c, HEAD_DIM), 0)
        qh = qv * ch["e_b"]
        k_end = kv * ch["e_end"]
        da = jnp.where(_iota((c, c), 0) >= _iota((c, c), 1), _dot(dov, vv, NT), 0.0)
        dqh = _dot(dov, s0, NN)
        dk_end = _dot(vv, ds1, NN)
        yield
        end = dk_end * k_end
        dk = dk_end * ch["e_end"]
        db = dqh * qh - end + jnp.where(
            row == c - 1, jnp.sum(end + s0 * ch["e_last"] * ds1, axis=0, keepdims=True), 0.0)
        dq_rows, qdq_rows = [], []
        for x in range(c // SUB):
            da_x = da[x * SUB:(x + 1) * SUB]
            k_x = kv * ch["k_facs"][x]
            dq_x = _dot(da_x, k_x, NN)
            dk_x = _dot(da_x, ch["q_subs"][x], TN)
            dq_rows.append(dq_x)
            qdq_rows.append(dq_x * ch["q_subs"][x])
            dk = dk + dk_x * ch["k_facs"][x]
            kdk = dk_x * k_x
            db = db - kdk
            if x > 0:
                at_start = jnp.sum(kdk, axis=0, keepdims=True) - jnp.sum(qdq_rows[x], axis=0, keepdims=True)
                db = db + jnp.where(row == x * SUB - 1, at_start, 0.0)
        yield
        b_start = jnp.zeros((c, HEAD_DIM), F32)
        for x in range(1, c // SUB):
            b_x = jnp.sum(jnp.where(row < x * SUB, lf_ref[...], 0.0), axis=0, keepdims=True)
            b_start = jnp.where(row >= x * SUB, b_x, b_start)
        dq = dqh * ch["e_b"] + jnp.concatenate(dq_rows, axis=0) * jnp.exp(jnp.minimum(ch["b"] - b_start, 0.0))
        db = db + jnp.concatenate(qdq_rows, axis=0)
        dstate[...] = _dot(dov, qh, TN) + ds1 * ch["e_last"]
        dq_ref[...] = dq
        dk_ref[...] = dk
        dv_ref[...] = _dot(ch["a"], dov, TN) + _dot(k_end, ds1, NT)
        dlf_ref[...] = _rev_cumsum_rows(db)

    rev = lambda c: nc - 1 - c
    tok = pl.BlockSpec((CHUNK, _hps(h) * HEAD_DIM), lambda hh, c: (rev(c), hh))
    mat = pl.BlockSpec((_hps(h), None, HEAD_DIM, HEAD_DIM), lambda hh, c: (hh, rev(c), 0, 0))
    tok_shape = jax.ShapeDtypeStruct(q.shape, F32)
    return pl.pallas_call(
        _each_head(body, 6), name="hgrn_bwd",
        out_shape=(tok_shape, tok_shape, tok_shape, tok_shape),
        grid=(h // _hps(h), nc),
        in_specs=[tok, tok, pl.BlockSpec((CHUNK, _hps(h) * HEAD_DIM), lambda hh, c: (rev(c), v_blk // _hps(h) + hh)), tok, mat,
                  pl.BlockSpec((CHUNK, _hps(h) * HEAD_DIM), lambda hh, c: (rev(c), do_blk // _hps(h) + hh))],
        out_specs=(tok, tok, tok, tok),
        scratch_shapes=[pltpu.VMEM((_hps(h), HEAD_DIM, HEAD_DIM), F32)],
        compiler_params=_params(("parallel", "arbitrary")),
    )(q, k, v, lf, states, do)


CONV_ROWS = 256
HALO = 8


def _shift_down(cur, prev, s):
    rt = cur.shape[0]
    head = jnp.concatenate([pltpu.roll(prev, s, 0), jnp.zeros((rt - HALO, cur.shape[1]), F32)], axis=0)
    return jnp.where(_iota(cur.shape, 0) < s, head, pltpu.roll(cur, s, 0))


def _shift_up(cur, nxt, s):
    rt = cur.shape[0]
    tail = jnp.concatenate([jnp.zeros((rt - HALO, cur.shape[1]), F32), pltpu.roll(nxt, HALO - s, 0)], axis=0)
    return jnp.where(_iota(cur.shape, 0) >= rt - s, tail, pltpu.roll(cur, rt - s, 0))


def _tile_with_prev(ref, i, rt):
    r0 = pl.multiple_of(i * rt, rt)
    cur = ref[pl.ds(r0, rt), :]
    prev = ref[pl.ds(pl.multiple_of(jnp.maximum(r0 - HALO, 0), HALO), HALO), :]
    return cur, jnp.where(i > 0, prev, 0.0)


def _tile_with_next(ref, i, rt, n_tiles):
    r0 = pl.multiple_of(i * rt, rt)
    cur = ref[pl.ds(r0, rt), :]
    nxt = ref[pl.ds(pl.multiple_of(jnp.minimum(r0 + rt, (n_tiles - 1) * rt), HALO), HALO), :]
    return cur, jnp.where(i < n_tiles - 1, nxt, 0.0)


def _conv_tile(x_ref, w_ref, i, rt):
    cur, prev = _tile_with_prev(x_ref, i, rt)
    shifted = [_shift_down(cur, prev, CONV_W - 1 - j) for j in range(CONV_W - 1)] + [cur]
    c = shifted[0] * w_ref[pl.ds(0, 1), :]
    for j in range(1, CONV_W):
        c = c + shifted[j] * w_ref[pl.ds(j, 1), :]
    return c, shifted


def _l2n(s):
    return s * lax.rsqrt(jnp.sum(s * s, axis=-1, keepdims=True) + L2_EPS)


def _gdn_prep_fwd(proj, conv_w, h):
    t = proj.shape[0]
    rt = _tile(t, CONV_ROWS)
    nt = t // rt
    scale = HEAD_DIM ** -0.5

    def body(xq, xk, xv, wq, wk, wv, q_ref, k_ref, v_ref):
        def tile(i, carry):
            rows = pl.ds(pl.multiple_of(i * rt, rt), rt)
            q_ref[rows, :] = _l2n(_silu(_conv_tile(xq, wq, i, rt)[0])) * scale
            k_ref[rows, :] = _l2n(_silu(_conv_tile(xk, wk, i, rt)[0]))
            v_ref[rows, :] = _silu(_conv_tile(xv, wv, i, rt)[0])
            return carry

        lax.fori_loop(0, nt, tile, 0)

    col = lambda p: pl.BlockSpec((t, HEAD_DIM), lambda hh: (0, p * h + hh))
    wcol = lambda p: pl.BlockSpec((CONV_W, HEAD_DIM), lambda hh: (0, p * h + hh))
    out = pl.BlockSpec((t, HEAD_DIM), lambda hh: (0, hh))
    shape = jax.ShapeDtypeStruct((t, h * HEAD_DIM), F32)
    return pl.pallas_call(
        body, name="gdn_prep_fwd", out_shape=(shape, shape, shape), grid=(h,),
        in_specs=[col(0), col(1), col(2), wcol(0), wcol(1), wcol(2)], out_specs=(out, out, out),
        compiler_params=_params(("parallel",)),
    )(proj, proj, proj, conv_w, conv_w, conv_w)


def _gdn_prep_bwd(proj, conv_w, dq, dk, dv, h):
    t = proj.shape[0]
    rt = _tile(t, CONV_ROWS)
    nt = t // rt
    scale = HEAD_DIM ** -0.5

    def part(x_ref, w_ref, dy_ref, dx_ref, dw_ref, dc_ref, norm_scale):
        def first(i, dws):
            rows = pl.ds(pl.multiple_of(i * rt, rt), rt)
            c, shifted = _conv_tile(x_ref, w_ref, i, rt)
            ds = dy_ref[rows, :]
            if norm_scale is not None:
                s = _silu(c)
                r = lax.rsqrt(jnp.sum(s * s, axis=-1, keepdims=True) + L2_EPS)
                y = s * r
                dyn = ds * norm_scale
                ds = r * (dyn - y * jnp.sum(dyn * y, axis=-1, keepdims=True))
            dc = ds * _dsilu(c)
            dc_ref[rows, :] = dc
            return tuple(dws[j] + jnp.sum(dc * shifted[j], axis=0, keepdims=True) for j in range(CONV_W))

        dws = lax.fori_loop(0, nt, first, tuple(jnp.zeros((1, HEAD_DIM), F32) for _ in range(CONV_W)))
        for j in range(CONV_W):
            dw_ref[pl.ds(j, 1), :] = dws[j]

        def second(i, carry):
            rows = pl.ds(pl.multiple_of(i * rt, rt), rt)
            cur, nxt = _tile_with_next(dc_ref, i, rt, nt)
            dx = cur * w_ref[pl.ds(CONV_W - 1, 1), :]
            for j in range(CONV_W - 1):
                dx = dx + _shift_up(cur, nxt, CONV_W - 1 - j) * w_ref[pl.ds(j, 1), :]
            dx_ref[rows, :] = dx.astype(dx_ref.dtype)
            return carry

        lax.fori_loop(0, nt, second, 0)

    def body(xq, xk, xv, wq, wk, wv, dq_ref, dk_ref, dv_ref, dxq, dxk, dxv, dwq, dwk, dwv, dc_ref):
        part(xq, wq, dq_ref, dxq, dwq, dc_ref, scale)
        part(xk, wk, dk_ref, dxk, dwk, dc_ref, 1.0)
        part(xv, wv, dv_ref, dxv, dwv, dc_ref, None)

    col = lambda p: pl.BlockSpec((t, HEAD_DIM), lambda hh: (0, p * h + hh))
    wcol = lambda p: pl.BlockSpec((CONV_W, HEAD_DIM), lambda hh: (0, p * h + hh))
    own = pl.BlockSpec((t, HEAD_DIM), lambda hh: (0, hh))
    wown = pl.BlockSpec((CONV_W, HEAD_DIM), lambda hh: (0, hh))
    dx_shape = jax.ShapeDtypeStruct((t, h * HEAD_DIM), BF16)
    dw_shape = jax.ShapeDtypeStruct((CONV_W, h * HEAD_DIM), F32)
    return pl.pallas_call(
        body, name="gdn_prep_bwd",
        out_shape=(dx_shape, dx_shape, dx_shape, dw_shape, dw_shape, dw_shape), grid=(h,),
        in_specs=[col(0), col(1), col(2), wcol(0), wcol(1), wcol(2), own, own, own],
        out_specs=(own, own, own, wown, wown, wown),
        scratch_shapes=[pltpu.VMEM((t, HEAD_DIM), F32)],
        compiler_params=_params(("parallel",)),
    )(proj, proj, proj, conv_w, conv_w, conv_w, dq, dk, dv)


def _gdn_gates_fwd(ab, a_log_row, dt_bias_row):
    t = ab.shape[0]
    tr = _tile(t, 512)

    def body(ab_ref, al_ref, dt_ref, g_ref, b_ref):
        g_ref[...] = -jnp.exp(al_ref[...]) * _softplus(ab_ref[:, :HEAD_DIM] + dt_ref[...])
        b_ref[...] = _sigmoid(ab_ref[:, HEAD_DIM:])

    row = pl.BlockSpec((tr, HEAD_DIM), lambda i: (i, 0))
    vec = pl.BlockSpec((1, HEAD_DIM), lambda i: (0, 0))
    shape = jax.ShapeDtypeStruct((t, HEAD_DIM), F32)
    return pl.pallas_call(
        body, name="gdn_gates_fwd", out_shape=(shape, shape), grid=(t // tr,),
        in_specs=[pl.BlockSpec((tr, 2 * HEAD_DIM), lambda i: (i, 0)), vec, vec], out_specs=(row, row),
        compiler_params=_params(("parallel",)),
    )(ab, a_log_row, dt_bias_row)


def _gdn_gates_bwd(ab, a_log_row, dt_bias_row, dg, dbeta):
    t = ab.shape[0]
    tr = _tile(t, 512)

    def body(ab_ref, al_ref, dt_ref, dg_ref, db_ref, dab_ref, dal_ref, ddt_ref):
        @pl.when(pl.program_id(0) == 0)
        def _():
            dal_ref[...] = jnp.zeros_like(dal_ref)
            ddt_ref[...] = jnp.zeros_like(ddt_ref)

        xa = ab_ref[:, :HEAD_DIM] + dt_ref[...]
        neg_a = -jnp.exp(al_ref[...])
        dgv = dg_ref[...]
        da = dgv * neg_a * _sigmoid(xa)
        beta = _sigmoid(ab_ref[:, HEAD_DIM:])
        dab_ref[:, :HEAD_DIM] = da.astype(BF16)
        dab_ref[:, HEAD_DIM:] = (db_ref[...] * beta * (1.0 - beta)).astype(BF16)
        dal_ref[...] += jnp.sum(dgv * neg_a * _softplus(xa), axis=0, keepdims=True)
        ddt_ref[...] += jnp.sum(da, axis=0, keepdims=True)

    row = pl.BlockSpec((tr, HEAD_DIM), lambda i: (i, 0))
    row2 = pl.BlockSpec((tr, 2 * HEAD_DIM), lambda i: (i, 0))
    vec = pl.BlockSpec((1, HEAD_DIM), lambda i: (0, 0))
    vshape = jax.ShapeDtypeStruct((1, HEAD_DIM), F32)
    return pl.pallas_call(
        body, name="gdn_gates_bwd",
        out_shape=(jax.ShapeDtypeStruct((t, 2 * HEAD_DIM), BF16), vshape, vshape), grid=(t // tr,),
        in_specs=[row2, vec, vec, row, row], out_specs=(row2, vec, vec),
        compiler_params=_params(("arbitrary",)),
    )(ab, a_log_row, dt_bias_row, dg, dbeta)


def _lower_bound(lb_ref):
    return _sigmoid(lb_ref[pl.ds(0, 1), :] - lb_ref[pl.ds(1, 1), :])


def _hgrn_prep_fwd(proj, lb_logits, h, q_blk, f_blk):
    t = proj.shape[0]
    tr = _tile(t, 512)

    def body(xq, xf, lb_ref, q_ref, k_ref, lf_ref):
        lb = _lower_bound(lb_ref)
        s = _sigmoid(xf[...])
        q_ref[...] = _silu(xq[...])
        k_ref[...] = (1.0 - lb) * (1.0 - s)
        lf_ref[...] = jnp.log(lb + (1.0 - lb) * s)

    width = h * HEAD_DIM
    col = lambda b0: pl.BlockSpec((tr, width), lambda i: (i, b0 // h))
    own = pl.BlockSpec((tr, width), lambda i: (i, 0))
    shape = jax.ShapeDtypeStruct((t, width), F32)
    return pl.pallas_call(
        body, name="hgrn_prep_fwd", out_shape=(shape, shape, shape), grid=(t // tr,),
        in_specs=[col(q_blk), col(f_blk), pl.BlockSpec((2, width), lambda i: (0, 0))],
        out_specs=(own, own, own), compiler_params=_params(("parallel",)),
    )(proj, proj, lb_logits)


def _hgrn_prep_bwd(proj, lb_logits, dq, dk, dlf, h, q_blk, f_blk):
    t = proj.shape[0]
    tr = _tile(t, 512)

    def body(xq, xf, lb_ref, dq_ref, dk_ref, dlf_ref, dxq, dxf, dlb_ref):
        @pl.when(pl.program_id(0) == 0)
        def _():
            dlb_ref[...] = jnp.zeros_like(dlb_ref)

        lb = _lower_bound(lb_ref)
        s = _sigmoid(xf[...])
        e = dlf_ref[...] / (lb + (1.0 - lb) * s) - dk_ref[...]
        dxq[...] = (dq_ref[...] * _dsilu(xq[...])).astype(BF16)
        dxf[...] = (s * (1.0 - s) * (1.0 - lb) * e).astype(BF16)
        d0 = jnp.sum((1.0 - s) * e, axis=0, keepdims=True) * (lb * (1.0 - lb))
        dlb_ref[pl.ds(0, 1), :] += d0
        dlb_ref[pl.ds(1, 1), :] += -d0

    width = h * HEAD_DIM
    col = lambda b0: pl.BlockSpec((tr, width), lambda i: (i, b0 // h))
    own = pl.BlockSpec((tr, width), lambda i: (i, 0))
    lbs = pl.BlockSpec((2, width), lambda i: (0, 0))
    shape = jax.ShapeDtypeStruct((t, width), BF16)
    return pl.pallas_call(
        body, name="hgrn_prep_bwd",
        out_shape=(shape, shape, jax.ShapeDtypeStruct((2, width), F32)), grid=(t // tr,),
        in_specs=[col(q_blk), col(f_blk), lbs, own, own, own], out_specs=(own, own, lbs),
        compiler_params=_params(("arbitrary",)),
    )(proj, proj, lb_logits, dq, dk, dlf)


GATE_HEADS = 4


def _gate_specs(h, z_blk, g_blk, tr):
    g = min(GATE_HEADS, h)
    n = h // g
    width = g * HEAD_DIM
    o_a = pl.BlockSpec((tr, width), lambda gg, i: (i, jnp.minimum(gg, n - 1)))
    o_b = pl.BlockSpec((tr, width), lambda gg, i: (i, jnp.maximum(gg - n, 0)))
    gate = pl.BlockSpec((tr, width), lambda gg, i: (i, jnp.where(gg < n, z_blk // g + gg, g_blk // g + gg - n)))
    w = pl.BlockSpec((None, 1, HEAD_DIM), lambda gg, i: (gg // n, 0, 0))
    cat = pl.BlockSpec((tr, width), lambda gg, i: (i, gg))
    return (o_a, o_b, gate, w, cat), g, n


def _silu_and_grad(x):
    s = _sigmoid(x)
    return x * s, s * (1.0 + x * (1.0 - s))


def _gate_fwd(o_a, o_b, proj, norm_w, h, z_blk, g_blk):
    t = o_a.shape[0]
    tr = _tile(t, 512)

    (sa, sb, sg, sw, cat), g, n = _gate_specs(h, z_blk, g_blk, tr)

    def body(oa_ref, ob_ref, z_ref, w_ref, y_ref):
        for k in range(g):
            lanes = pl.ds(k * HEAD_DIM, HEAD_DIM)
            o = jnp.where(pl.program_id(0) < n, oa_ref[:, lanes], ob_ref[:, lanes])
            r = lax.rsqrt(jnp.mean(o * o, axis=-1, keepdims=True) + NORM_EPS)
            y_ref[:, lanes] = (o * r * w_ref[...] * _silu(z_ref[:, lanes])).astype(y_ref.dtype)

    return pl.pallas_call(
        body, name="gate_fwd", out_shape=jax.ShapeDtypeStruct((t, 2 * h * HEAD_DIM), BF16),
        grid=(2 * n, t // tr), in_specs=[sa, sb, sg, sw], out_specs=cat,
        compiler_params=_params(("parallel", "parallel")),
    )(o_a, o_b, proj, norm_w)


def _gate_bwd(o_a, o_b, proj, norm_w, dy, h, z_blk, g_blk):
    t = o_a.shape[0]
    tr = _tile(t, 512)

    (sa, sb, sg, sw, cat), g, n = _gate_specs(h, z_blk, g_blk, tr)

    def body(oa_ref, ob_ref, z_ref, w_ref, dy_ref, do_ref, dz_ref, dw_ref):
        gg = pl.program_id(0)

        @pl.when(jnp.logical_and(gg % n == 0, pl.program_id(1) == 0))
        def _():
            dw_ref[...] = jnp.zeros_like(dw_ref)

        w = w_ref[...]
        dw = jnp.zeros_like(w)
        for k in range(g):
            lanes = pl.ds(k * HEAD_DIM, HEAD_DIM)
            o = jnp.where(gg < n, oa_ref[:, lanes], ob_ref[:, lanes])
            dyv = dy_ref[:, lanes]
            r = lax.rsqrt(jnp.mean(o * o, axis=-1, keepdims=True) + NORM_EPS)
            oh = o * r
            act, dact = _silu_and_grad(z_ref[:, lanes])
            dz_ref[:, lanes] = (dyv * oh * w * dact).astype(dz_ref.dtype)
            dn = dyv * act
            doh = dn * w
            do_ref[:, lanes] = r * (doh - oh * jnp.mean(doh * oh, axis=-1, keepdims=True))
            dw = dw + jnp.sum(dn * oh, axis=0, keepdims=True)
        dw_ref[...] += dw

    width = 2 * h * HEAD_DIM
    return pl.pallas_call(
        body, name="gate_bwd",
        out_shape=(jax.ShapeDtypeStruct((t, width), F32), jax.ShapeDtypeStruct((t, width), BF16),
                   jax.ShapeDtypeStruct((2, 1, HEAD_DIM), F32)),
        grid=(2 * n, t // tr), in_specs=[sa, sb, sg, sw, cat], out_specs=(cat, cat, sw),
        compiler_params=_params(("arbitrary", "arbitrary")),
    )(o_a, o_b, proj, norm_w, dy)


def _lane_row(vec):
    return jnp.pad(vec.reshape(1, -1), ((0, 0), (0, HEAD_DIM - vec.shape[-1])))


def _add_epi(acc, res):
    return (acc + res,)


def _split_w_in(w_in, h):
    gw = h * HEAD_DIM
    main = jnp.concatenate([w_in[:, :4 * gw], w_in[:, 4 * gw + 2 * h:]], axis=1)
    pad = jnp.zeros((w_in.shape[0], HEAD_DIM - h), w_in.dtype)
    ab = jnp.concatenate([w_in[:, 4 * gw:4 * gw + h], pad, w_in[:, 4 * gw + h:4 * gw + 2 * h], pad], axis=1)
    return main, ab


def _merge_w_in(main, ab, h):
    gw = h * HEAD_DIM
    return jnp.concatenate([main[:, :4 * gw], ab[:, :h], ab[:, HEAD_DIM:HEAD_DIM + h], main[:, 4 * gw:]], axis=1)


def _local_step(x, target, w_main, w_ab, conv_w, a_log, dt_bias, gdn_norm_w, lb_logits, hgrn_norm_w,
                w_out, norm_mix_w, norm_ffn_w, w_ff1, w_ff2, norm_final_w, reducer=None, n1=None):
    t, d = x.shape
    h = d // (2 * HEAD_DIM)
    gw = h * HEAD_DIM
    k_blk, v_blk, z_blk, qb_blk, fb_blk, ib_blk, gb_blk = (i * h for i in range(1, 8))
    del k_blk, v_blk

    if n1 is None:
        n1 = _rms_fwd(x, norm_mix_w, "rms_mix")
    proj = _mm(n1, w_main, "nn", (F32,), "in_proj")
    ab = _mm(n1, w_ab, "nn", (F32,), "in_proj_ab")

    q, k, v = _gdn_prep_fwd(proj, conv_w, h)
    a_log_row, dt_row = _lane_row(a_log), _lane_row(dt_bias)
    g_tm, beta_tm = _gdn_gates_fwd(ab, a_log_row, dt_row)
    to_heads = lambda a: jnp.broadcast_to(a[:, :h].T[:, :, None], (h, t, HEAD_DIM))
    g_bc, beta_bc = to_heads(g_tm), to_heads(beta_tm)
    o_a, st_a, inv_a = _gdn_fwd(q, k, v, beta_bc, g_bc)

    qh, kh, lf = _hgrn_prep_fwd(proj, lb_logits, h, qb_blk, fb_blk)
    o_b, st_b = _hgrn_fwd(qh, kh, proj, lf, v_blk=ib_blk)

    gate_w = jnp.stack([gdn_norm_w.reshape(1, HEAD_DIM), hgrn_norm_w.reshape(1, HEAD_DIM)])
    y = _gate_fwd(o_a, o_b, proj, gate_w, h, z_blk, gb_blk)
    h1 = _mm(y, w_out, "nn", (F32,), "out_proj", epi=_add_epi, extras=(x,))
    n2 = _rms_fwd(h1, norm_ffn_w, "rms_ffn")
    act, r = _mm(n2, w_ff1, "nn", (F32, BF16), "ff1", b_stacked=True,
                 epi=lambda acc: (acc, jnp.square(jnp.maximum(acc, 0.0))))
    h2 = _mm(r, w_ff2, "nn", (F32,), "ff2", epi=_add_epi, extras=(h1,))
    loss, dh2, dh2_b, d_norm_final = _loss_head(h2, norm_final_w, target)

    da = _mm(dh2_b, w_ff2, "nt", (BF16,), "ff2_dx",
             epi=lambda acc, a: (acc * (2.0 * jnp.maximum(a, 0.0)),), extras=(act,))
    pending = []

    def step(anchor, name=None, full=None):
        if reducer is not None:
            pending.extend(reducer.step(name, full, anchor))

    def after_step(value):
        if not pending:
            return value
        value = lax.optimization_barrier((value, *pending))[0]
        pending.clear()
        return value

    d_ff2 = _mm(r, dh2_b, "tn", (F32,), "ff2_dw")
    step(None, "w_ff2", d_ff2)
    dn2 = _mm(da, w_ff1, "nt", (F32,), "ff1_dx", b_stacked=True)
    d_ff1 = _mm(n2, da, "tn", (F32,), "ff1_dw", out_stacked=True)
    step(d_ff1, "w_ff1", d_ff1)
    dh1, dh1_b, d_norm_ffn = _rms_bwd(after_step(dn2), h1, norm_ffn_w, dh2, "rms_ffn_bwd")
    dy = _mm(dh1_b, w_out, "nt", (F32,), "out_proj_dx")
    d_out = _mm(y, dh1_b, "tn", (F32,), "out_proj_dw")
    step(d_out, "w_out", d_out)

    do, dgate, d_gate_w = _gate_bwd(o_a, o_b, proj, gate_w, after_step(dy), h, z_blk, gb_blk)
    step(do)
    dq, dk, dv, dbeta_bc, dg_bc = _gdn_bwd(q, k, v, beta_bc, g_bc, st_a, inv_a, after_step(do), do_blk=0)
    step(dq)
    dxq, dxk, dxv, dcq, dck, dcv = _gdn_prep_bwd(proj, conv_w, after_step(dq), dk, dv, h)
    step(dxq)
    from_heads = lambda a: jnp.pad(a[:, :, 0, :].reshape(h, t).T, ((0, 0), (0, HEAD_DIM - h)))
    dab, d_a_log, d_dt_bias = _gdn_gates_bwd(ab, a_log_row, dt_row, from_heads(dg_bc), from_heads(dbeta_bc))
    dqh, dkh, dvh, dlf = _hgrn_bwd(after_step(qh), kh, proj, lf, st_b, do, v_blk=ib_blk, do_blk=h)
    step(dqh)
    dxqb, dxfb, d_lb = _hgrn_prep_bwd(proj, lb_logits, dqh, dkh, dlf, h, qb_blk, fb_blk)

    dproj = jnp.concatenate([after_step(dxq), dxk, dxv, dgate[:, :gw], dxqb, dxfb, dvh.astype(BF16), dgate[:, gw:]],
                            axis=1)
    d_main = _mm(n1, dproj, "tn", (F32,), "in_proj_dw")
    d_ab = _mm(n1, dab, "tn", (F32,), "in_proj_ab_dw")
    step(d_main, "w_in", _merge_w_in(d_main, d_ab, h))
    dn1_ab = _mm(after_step(dab), w_ab, "nt", (F32,), "in_proj_ab_dx")
    step(dn1_ab)
    dn1 = _mm(after_step(dproj), w_main, "nt", (F32,), "in_proj_dx", epi=_add_epi, extras=(dn1_ab,))
    step(dn1)
    dx, _, d_norm_mix = _rms_bwd(after_step(dn1), x, norm_mix_w, dh1, "rms_mix_bwd")
    step(dx)

    grads = dict(
        w_main=d_main, w_ab=d_ab, conv_w=jnp.concatenate([dcq, dck, dcv], axis=1),
        gdn_a_log=d_a_log[:, :h], gdn_dt_bias=d_dt_bias[:, :h], gdn_norm_w=d_gate_w[0],
        hgrn_lb_logits=d_lb, hgrn_norm_w=d_gate_w[1], w_out=d_out, norm_mix_w=d_norm_mix,
        norm_ffn_w=d_norm_ffn, w_ff1=d_ff1, w_ff2=d_ff2, norm_final_w=d_norm_final)
    return loss, dx, grads


N_CHIPS = 4
ANY = pl.BlockSpec(memory_space=pl.ANY)


def _place():
    x, y, c = lax.axis_index("x"), lax.axis_index("y"), lax.axis_index("c")
    chips = [(1 - x, y), (x, 1 - y), (1 - x, 1 - y)]
    return x, y, c, chips


def _remote(src, dst, send_sems, recv_sems, k, to):
    return pltpu.make_async_remote_copy(src_ref=src, dst_ref=dst, send_sem=send_sems.at[k],
                                        recv_sem=recv_sems.at[k], device_id=to, device_id_type=MESH)


def _to_sibling(x, y, c, chips):
    return [(x, y, 1 - c)]


def _to_same_core_of_chips(x, y, c, chips):
    return [(*chip, c) for chip in chips]


def _to_all_gather_peers(x, y, c, chips):
    return _to_sibling(x, y, c, chips) + _to_same_core_of_chips(x, y, c, chips)


SIBLING_EXCHANGE = (1, _to_sibling)
CHIP_EXCHANGE = (2, _to_same_core_of_chips)
GATHER_EXCHANGE = (3, _to_all_gather_peers)


def _launch(body, name, out_shapes, arrays, sem_counts, sequencer=None, after=()):
    n, n_after = len(arrays), len(after)
    sems = [pltpu.SemaphoreType.DMA((k,)) for k in sem_counts]
    strip = lambda refs: refs[:n] + refs[n + n_after:]
    if sequencer is None:
        return pl.pallas_call(
            lambda *refs: body(*strip(refs)), name=name, out_shape=tuple(out_shapes),
            in_specs=[ANY] * (n + n_after), out_specs=tuple(ANY for _ in out_shapes), scratch_shapes=sems,
        )(*arrays, *after)
    collective_id, peers = sequencer

    def sequencer_body(*refs):
        x, y, c, chips = _place()
        barrier = pltpu.get_barrier_semaphore()
        targets = peers(x, y, c, chips)
        for target in targets:
            pl.semaphore_signal(barrier, inc=1, device_id=target, device_id_type=MESH)
        pl.semaphore_wait(barrier, len(targets))
        body(*strip(refs))

    return pl.kernel(
        sequencer_body, name=name, out_type=tuple(out_shapes),
        mesh=plsc.ScalarSubcoreMesh(axis_name="sequencer", num_cores=1), scratch_types=tuple(sems),
        compiler_params=pltpu.CompilerParams(collective_id=collective_id),
    )(*arrays, *after)


def _gather_weights(big, small, name, sequencer=None, after=()):
    nb, ns = len(big), len(small)
    n_sem = 6 * nb + 3 * ns

    def body(*refs):
        ins, outs = refs[:nb + ns], refs[nb + ns:2 * (nb + ns)]
        send_sems, recv_sems = refs[2 * (nb + ns):]
        x, y, c, chips = _place()
        me, sibling = 2 * x + y, (x, y, 1 - c)

        def half(a, chip, hc):
            rh = big[a].shape[0] // 2
            return outs[a].at[2 * chip[0] + chip[1], pl.ds(hc * rh, rh), :]

        first, passed = [], []
        for a in range(nb):
            rh = big[a].shape[0] // 2
            for j, chip in enumerate(chips):
                first.append(_remote(ins[a].at[pl.ds(c * rh, rh), :], half(a, (x, y), c),
                                     send_sems, recv_sems, 6 * a + j, (*chip, c)))
        for s in range(ns):
            for j, chip in enumerate(chips):
                first.append(_remote(ins[nb + s], outs[nb + s].at[me], send_sems, recv_sems,
                                     6 * nb + 3 * s + j, (*chip, c)))
        for cp in first:
            cp.start()
        for a in range(nb):
            for j, chip in enumerate(chips):
                _remote(half(a, chip, c), half(a, chip, c), send_sems, recv_sems, 6 * a + j, (*chip, c)).wait_recv()
                fwd = _remote(half(a, chip, c), half(a, chip, c), send_sems, recv_sems, 6 * a + 3 + j, sibling)
                fwd.start()
                passed.append(fwd)
        for s in range(ns):
            for j, chip in enumerate(chips):
                dst = outs[nb + s].at[2 * chip[0] + chip[1]]
                _remote(dst, dst, send_sems, recv_sems, 6 * nb + 3 * s + j, (*chip, c)).wait_recv()
        for a in range(nb):
            for j, chip in enumerate(chips):
                _remote(half(a, chip, 1 - c), half(a, chip, 1 - c), send_sems, recv_sems,
                        6 * a + 3 + j, sibling).wait_recv()
        for cp in first + passed:
            cp.wait_send()

    arrays = list(big) + list(small)
    out_shapes = [jax.ShapeDtypeStruct((N_CHIPS,) + a.shape, a.dtype) for a in arrays]
    return _launch(body, name, out_shapes, arrays, (n_sem, n_sem), sequencer, after)


def _swap_halves(parts, name, sequencer=None):
    n = len(parts)

    def body(*refs):
        ins, outs = refs[:n], refs[n:2 * n]
        send_sems, recv_sems = refs[2 * n:]
        x, y, c, _ = _place()
        copies = [_remote(ins[a].at[s, 1 - c], outs[a].at[s], send_sems, recv_sems, N_CHIPS * a + s, (x, y, 1 - c))
                  for a in range(n) for s in range(N_CHIPS)]
        for cp in copies:
            cp.start()
        for cp in copies:
            cp.wait()

    out_shapes = [jax.ShapeDtypeStruct((N_CHIPS,) + p.shape[2:], p.dtype) for p in parts]
    return _launch(body, name, out_shapes, parts, (N_CHIPS * n, N_CHIPS * n), sequencer)


def _scatter_to_owners(parts, name, sequencer=None):
    n = len(parts)

    def body(*refs):
        ins, outs = refs[:n], refs[n:2 * n]
        send_sems, recv_sems = refs[2 * n:]
        x, y, c, chips = _place()
        copies = [_remote(ins[a].at[2 * chip[0] + chip[1]], outs[a].at[j], send_sems, recv_sems,
                          3 * a + j, (*chip, c))
                  for a in range(n) for j, chip in enumerate(chips)]
        for cp in copies:
            cp.start()
        for cp in copies:
            cp.wait()

    out_shapes = [jax.ShapeDtypeStruct((3,) + p.shape[1:], p.dtype) for p in parts]
    return _launch(body, name, out_shapes, parts, (3 * n, 3 * n), sequencer)


def _send_to_sibling(halves, name, sequencer=None):
    n = len(halves)

    def body(*refs):
        ins, outs = refs[:n], refs[n:2 * n]
        send_sems, recv_sems = refs[2 * n:]
        x, y, c, _ = _place()
        copies = [_remote(ins[a], outs[a], send_sems, recv_sems, a, (x, y, 1 - c)) for a in range(n)]
        for cp in copies:
            cp.start()
        for cp in copies:
            cp.wait()

    out_shapes = [jax.ShapeDtypeStruct(p.shape, p.dtype) for p in halves]
    return _launch(body, name, out_shapes, halves, (n, n), sequencer)


N_DEV = 8


def _all_reduce_small(vec):
    def body(v_ref, gathered, total, send_sems, recv_sems):
        x, y, c, _ = _place()
        me = 4 * x + 2 * y + c
        gathered[me] = v_ref[...]
        copies = []
        for k in range(1, N_DEV):
            px = 1 - x if k & 4 else x
            py = 1 - y if k & 2 else y
            pc = 1 - c if k & 1 else c
            copies.append(_remote(v_ref, gathered.at[me], send_sems, recv_sems, k - 1, (px, py, pc)))
        for cp in copies:
            cp.start()
        for k, cp in enumerate(copies):
            cp.wait_send()
        for k in range(1, N_DEV):
            px = 1 - x if k & 4 else x
            py = 1 - y if k & 2 else y
            pc = 1 - c if k & 1 else c
            src = gathered.at[4 * px + 2 * py + pc]
            _remote(src, src, send_sems, recv_sems, k - 1, (px, py, pc)).wait_recv()
        acc = gathered[0]
        for dev in range(1, N_DEV):
            acc = acc + gathered[dev]
        total[...] = acc

    vm = pl.BlockSpec(memory_space=pltpu.VMEM)
    return pl.pallas_call(
        body, name="all_reduce_small",
        out_shape=(jax.ShapeDtypeStruct((N_DEV,) + vec.shape, F32), jax.ShapeDtypeStruct(vec.shape, F32)),
        in_specs=[vm], out_specs=(vm, vm),
        scratch_shapes=[pltpu.SemaphoreType.DMA((N_DEV - 1,)), pltpu.SemaphoreType.DMA((N_DEV - 1,))],
    )(vec)[1]


def _chip_sum(part, recv, c):
    _, _, rh, cols = part.shape
    tr = _tile(rh, 256)

    def body(c_ref, p_ref, r_ref, s_ref, sb_ref):
        s = p_ref[...] + r_ref[...]
        s_ref[...] = s
        sb_ref[...] = s.astype(BF16)

    blk = pl.BlockSpec((None, tr, cols), lambda s, i, c_ref: (s, i, 0))
    return pl.pallas_call(
        body, name="grad_chip_sum",
        out_shape=(jax.ShapeDtypeStruct(recv.shape, F32), jax.ShapeDtypeStruct(recv.shape, BF16)),
        grid_spec=pltpu.PrefetchScalarGridSpec(
            num_scalar_prefetch=1, grid=(N_CHIPS, rh // tr),
            in_specs=[pl.BlockSpec((None, None, tr, cols), lambda s, i, c_ref: (s, c_ref[0], i, 0)), blk],
            out_specs=(blk, blk)),
        compiler_params=_params(("parallel", "parallel")),
    )(c, part, recv)


def _owner_sum(own, recv, me):
    _, rh, cols = own.shape
    tr = _tile(rh, 256)

    def body(me_ref, o_ref, r0, r1, r2, g_ref):
        g_ref[...] = ((o_ref[...] + r0[...].astype(F32)) + r1[...].astype(F32)) + r2[...].astype(F32)

    slot = lambda j: pl.BlockSpec((None, tr, cols), lambda i, me_ref: (j, i, 0))
    return pl.pallas_call(
        body, name="grad_owner_sum", out_shape=jax.ShapeDtypeStruct((rh, cols), F32),
        grid_spec=pltpu.PrefetchScalarGridSpec(
            num_scalar_prefetch=1, grid=(rh // tr,),
            in_specs=[pl.BlockSpec((None, tr, cols), lambda i, me_ref: (me_ref[0], i, 0)), slot(0), slot(1), slot(2)],
            out_specs=pl.BlockSpec((tr, cols), lambda i, me_ref: (i, 0))),
        compiler_params=_params(("parallel",)),
    )(me, own, recv, recv, recv)


def _adamw_math(w, g, m, v):
    c1 = 1.0 / (1.0 - ADAM_B1 ** ADAM_STEP)
    c2 = 1.0 / (1.0 - ADAM_B2 ** ADAM_STEP)
    nm = ADAM_B1 * m + (1.0 - ADAM_B1) * g
    nv = ADAM_B2 * v + (1.0 - ADAM_B2) * (g * g)
    return -ADAM_LR * ((nm * c1) / (jnp.sqrt(nv * c2) + ADAM_EPS) + ADAM_WD * w), nm, nv


def _adamw_unit_rows(w, g, m, v, name):
    rows, _, cols = w.shape
    tr = max(d for d in range(1, 33) if rows % d == 0)

    def body(w_ref, g_ref, m_ref, v_ref, d_ref, nm_ref, nv_ref):
        d_ref[...], nm_ref[...], nv_ref[...] = _adamw_math(w_ref[...], g_ref[...], m_ref[...], v_ref[...])

    blk = pl.BlockSpec((tr, 1, cols), lambda i: (i, 0, 0))
    shape = jax.ShapeDtypeStruct(w.shape, F32)
    return pl.pallas_call(
        body, name=name, out_shape=(shape, shape, shape), grid=(rows // tr,),
        in_specs=[blk, blk, blk, blk], out_specs=(blk, blk, blk),
        compiler_params=_params(("parallel",)),
    )(w, g, m, v)


def _divisor_tile(n, want):
    return max(d for d in range(ROW_TILE, want + 1, ROW_TILE) if n % d == 0)


def _adamw(w, g, m, v, name):
    if w.ndim == 3:
        return _adamw_unit_rows(w, g, m, v, name)
    rows, cols = w.shape
    tr = _divisor_tile(rows, 2048) if rows % 8 == 0 else rows
    c1 = 1.0 / (1.0 - ADAM_B1 ** ADAM_STEP)
    c2 = 1.0 / (1.0 - ADAM_B2 ** ADAM_STEP)

    def body(w_ref, g_ref, m_ref, v_ref, d_ref, nm_ref, nv_ref):
        gv = g_ref[...]
        nm = ADAM_B1 * m_ref[...] + (1.0 - ADAM_B1) * gv
        nv = ADAM_B2 * v_ref[...] + (1.0 - ADAM_B2) * (gv * gv)
        d_ref[...] = -ADAM_LR * ((nm * c1) / (jnp.sqrt(nv * c2) + ADAM_EPS) + ADAM_WD * w_ref[...])
        nm_ref[...] = nm
        nv_ref[...] = nv

    blk = pl.BlockSpec((tr, cols), lambda i: (i, 0))
    shape = jax.ShapeDtypeStruct((rows, cols), F32)
    return pl.pallas_call(
        body, name=name, out_shape=(shape, shape, shape), grid=(rows // tr,),
        in_specs=[blk, blk, blk, blk], out_specs=(blk, blk, blk),
        compiler_params=_params(("parallel",)),
    )(w, g, m, v)


def _adamw_halves(w, g_own, g_sib, m, v, c, name):
    _, rows, cols = w.shape
    rh = rows // 2
    tr = _tile(rh, 256)
    per = rh // tr
    c1 = 1.0 / (1.0 - ADAM_B1 ** ADAM_STEP)
    c2 = 1.0 / (1.0 - ADAM_B2 ** ADAM_STEP)

    def body(c_ref, w_ref, go_ref, gs_ref, m_ref, v_ref, g_ref, d_ref, nm_ref, nv_ref):
        own = pl.program_id(0) // per == c_ref[0]
        gv = jnp.where(own, go_ref[...], gs_ref[...])
        nm = ADAM_B1 * m_ref[...] + (1.0 - ADAM_B1) * gv
        nv = ADAM_B2 * v_ref[...] + (1.0 - ADAM_B2) * (gv * gv)
        g_ref[...] = gv
        d_ref[...] = -ADAM_LR * ((nm * c1) / (jnp.sqrt(nv * c2) + ADAM_EPS) + ADAM_WD * w_ref[...])
        nm_ref[...] = nm
        nv_ref[...] = nv

    blk = pl.BlockSpec((None, tr, cols), lambda i, c_ref: (0, i, 0))
    half = pl.BlockSpec((tr, cols), lambda i, c_ref: (i % per, 0))
    shape = jax.ShapeDtypeStruct((1, rows, cols), F32)
    return pl.pallas_call(
        body, name=name, out_shape=(shape, shape, shape, shape),
        grid_spec=pltpu.PrefetchScalarGridSpec(
            num_scalar_prefetch=1, grid=(rows // tr,),
            in_specs=[blk, half, half, blk, blk], out_specs=(blk, blk, blk, blk)),
        compiler_params=_params(("parallel",)),
    )(c, w, g_own, g_sib, m, v)


def _by_shard(name, full):
    if name == "w_in":
        st = full.reshape(full.shape[0], N_CHIPS, -1).transpose(1, 0, 2)
    elif name == "w_ff1":
        st = full
    else:
        st = full.reshape(N_CHIPS, -1, full.shape[1])
    return st.reshape(N_CHIPS, 2, st.shape[1] // 2, st.shape[2])


class _GradReducer:
    def __init__(self, w, m, v, my_c, my_chip):
        self.w, self.m, self.v, self.my_c, self.my_chip = w, m, v, my_c, my_chip
        self.in_flight = []
        self.computed = []
        self.anchor = None
        self.done = {}

    def step(self, name=None, full=None, anchor=None):
        stages, self.in_flight, self.computed, self.anchor = self.in_flight, [], [], anchor
        for stage in [s for s in stages if not getattr(s, "long", False)]:
            self._advance(stage)
        if name is not None:
            self.in_flight.append(self._swap(name, _by_shard(name, full)))
        for stage in [s for s in stages if getattr(s, "long", False)]:
            self._advance(stage)
        return self.computed

    def _held(self, value):
        if self.anchor is None:
            return value
        return lax.optimization_barrier((value, self.anchor))[0]

    def _advance(self, stage):
        nxt = stage()
        if nxt is not None:
            self.in_flight.append(nxt)

    def finish(self):
        while self.in_flight:
            self.step()
        return self.done

    def _swap(self, name, part):
        got, = _swap_halves([part], "grad_swap_" + name, SIBLING_EXCHANGE)

        def scatter():
            total, total_bf16 = _chip_sum(part, self._held(got), self.my_c)
            self.computed.append(total_bf16)
            recv, = _scatter_to_owners([total_bf16], "grad_scatter_" + name, CHIP_EXCHANGE)

            def send():
                half = _owner_sum(total, self._held(recv), self.my_chip.reshape(1))
                self.computed.append(half)
                sib, = _send_to_sibling([half], "grad_send_" + name, SIBLING_EXCHANGE)

                def update():
                    update_fn = _adamw_halves if self.w[name].shape[-1] % HEAD_DIM == 0 else _adamw_minor_rows
                    self.done[name] = update_fn(self.w[name], half, self._held(sib), self.m[name], self.v[name],
                                                self.my_c, "adamw_" + name)
                    self.computed.append(self.done[name][0])
                return update
            return lambda: send
        scatter.long = True
        return scatter


def _adamw_minor_rows(w, g_own, g_sib, m, v, c, name):
    _, rows, cols = w.shape
    turned = lambda a: jnp.transpose(a, (2, 0, 1))
    back = lambda a: jnp.transpose(a, (1, 2, 0))
    lower, upper = jnp.where(c[0] == 0, g_own, g_sib), jnp.where(c[0] == 0, g_sib, g_own)
    g = jnp.concatenate([lower, upper], axis=0).T.reshape(cols, 1, rows)
    delta, new_m, new_v = _adamw(turned(w), g, turned(m), turned(v), name)
    return back(g), back(delta), back(new_m), back(new_v)


SMALL = ("gdn_a_log", "gdn_dt_bias", "gdn_norm_w", "hgrn_lb_logits", "hgrn_norm_w",
         "norm_mix_w", "norm_ffn_w", "norm_final_w")
BIG = ("w_in", "w_out", "w_ff1", "w_ff2")
ORDER = ("w_in", "conv_w", "gdn_a_log", "gdn_dt_bias", "gdn_norm_w", "hgrn_lb_logits", "hgrn_norm_w",
         "w_out", "norm_mix_w", "norm_ffn_w", "w_ff1", "w_ff2", "norm_final_w")


def _pack(pieces):
    flat = jnp.concatenate([p.reshape(-1).astype(F32) for p in pieces])
    rows = -(-flat.shape[0] // (8 * HEAD_DIM)) * 8
    return jnp.pad(flat, (0, rows * HEAD_DIM - flat.shape[0])).reshape(rows, HEAD_DIM)


def _unpack(packed, shapes):
    flat, out, at = packed.reshape(-1), [], 0
    for s in shapes:
        n = 1
        for dim in s:
            n *= dim
        out.append(flat[at:at + n].reshape(s))
        at += n
    return out


def kernel(x, w_in, conv_w, gdn_a_log, gdn_dt_bias, gdn_norm_w, hgrn_lb_logits, hgrn_norm_w, w_out, norm_mix_w, norm_ffn_w, w_ff1, w_ff2, norm_final_w, loss_target, m_w_in, m_conv_w, m_gdn_a_log, m_gdn_dt_bias, m_gdn_norm_w, m_hgrn_lb_logits, m_hgrn_norm_w, m_w_out, m_norm_mix_w, m_norm_ffn_w, m_w_ff1, m_w_ff2, m_norm_final_w, v_w_in, v_conv_w, v_gdn_a_log, v_gdn_dt_bias, v_gdn_norm_w, v_hgrn_lb_logits, v_hgrn_norm_w, v_w_out, v_norm_mix_w, v_norm_ffn_w, v_w_ff1, v_w_ff2, v_norm_final_w):
    w = dict(w_in=w_in, conv_w=conv_w, gdn_a_log=gdn_a_log, gdn_dt_bias=gdn_dt_bias, gdn_norm_w=gdn_norm_w,
             hgrn_lb_logits=hgrn_lb_logits, hgrn_norm_w=hgrn_norm_w, w_out=w_out, norm_mix_w=norm_mix_w,
             norm_ffn_w=norm_ffn_w, w_ff1=w_ff1, w_ff2=w_ff2, norm_final_w=norm_final_w)
    m = dict(w_in=m_w_in, conv_w=m_conv_w, gdn_a_log=m_gdn_a_log, gdn_dt_bias=m_gdn_dt_bias,
             gdn_norm_w=m_gdn_norm_w, hgrn_lb_logits=m_hgrn_lb_logits, hgrn_norm_w=m_hgrn_norm_w,
             w_out=m_w_out, norm_mix_w=m_norm_mix_w, norm_ffn_w=m_norm_ffn_w, w_ff1=m_w_ff1, w_ff2=m_w_ff2,
             norm_final_w=m_norm_final_w)
    v = dict(w_in=v_w_in, conv_w=v_conv_w, gdn_a_log=v_gdn_a_log, gdn_dt_bias=v_gdn_dt_bias,
             gdn_norm_w=v_gdn_norm_w, hgrn_lb_logits=v_hgrn_lb_logits, hgrn_norm_w=v_hgrn_norm_w,
             w_out=v_w_out, norm_mix_w=v_norm_mix_w, norm_ffn_w=v_norm_ffn_w, w_ff1=v_w_ff1, w_ff2=v_w_ff2,
             norm_final_w=v_norm_final_w)
    d = x.shape[-1]
    h = d // (2 * HEAD_DIM)
    my_c = lax.axis_index("c").astype(jnp.int32).reshape(1)
    my_chip = (2 * lax.axis_index("x") + lax.axis_index("y")).astype(jnp.int32)

    shards = [w[n][0].astype(BF16) for n in BIG]
    conv_shard = jnp.pad(conv_w[0], ((0, 8 - CONV_W), (0, 0)))
    first = _gather_weights(shards[:1], [conv_shard], "gather_in_proj", GATHER_EXCHANGE)
    n1 = _rms_fwd(x[0], norm_mix_w[0], "rms_mix")
    gathered_in, n1, *shards[1:] = lax.optimization_barrier((first[0], n1, *shards[1:]))
    own_slot = lambda st, own: lax.dynamic_update_index_in_dim(st, own, my_chip, 0)
    f_in, f_conv = own_slot(gathered_in, shards[0]), own_slot(first[1], conv_shard)
    cols = lambda st: st.transpose(1, 0, 2).reshape(st.shape[1], -1)
    w_main, w_ab = _split_w_in(cols(f_in), h)
    conv_full = cols(f_conv[:, :CONV_W])
    rest = _gather_weights(shards[1:], [], "gather_rest", GATHER_EXCHANGE, after=[w_ab])
    f_out, f_ff1, f_ff2 = (own_slot(st, own) for st, own in zip(rest, shards[1:]))

    reducer = _GradReducer(w, m, v, my_c, my_chip)
    loss, dx, g = _local_step(
        x[0], loss_target[0], w_main, w_ab, conv_full, gdn_a_log[0], gdn_dt_bias[0], gdn_norm_w[0],
        hgrn_lb_logits, hgrn_norm_w[0], f_out.reshape(-1, d), norm_mix_w[0], norm_ffn_w[0],
        f_ff1, f_ff2.reshape(-1, d), norm_final_w, reducer, n1)

    grads, delta, new_m, new_v = {}, {}, {}, {}
    for n, out in reducer.finish().items():
        grads[n], delta[n], new_m[n], new_v[n] = out

    small_shapes = [w[n].shape for n in SMALL] + [conv_full.shape, (1,)]
    total = _all_reduce_small(_pack([g[n] for n in SMALL] + [g["conv_w"], loss[0, :1]]))
    *small_grads, conv_grad, loss_sum = _unpack(total, small_shapes)
    for n, sg in zip(SMALL, small_grads):
        grads[n] = sg
    shard_cols = conv_w.shape[-1]
    grads["conv_w"] = lax.dynamic_slice_in_dim(conv_grad, my_chip * shard_cols, shard_cols, axis=1)[None]

    packed_names = SMALL + ("conv_w",)
    packed = [_pack([t[n] for n in packed_names]) for t in (w, grads, m, v)]
    outs = _adamw(*packed, "adamw_small")
    shapes = [w[n].shape for n in packed_names]
    for res, o in zip((delta, new_m, new_v), outs):
        for n, a in zip(packed_names, _unpack(o, shapes)):
            res[n] = a

    return (loss_sum.reshape(()), dx[None], *[grads[n] for n in ORDER], *[delta[n] for n in ORDER],
            *[new_m[n] for n in ORDER], *[new_v[n] for n in ORDER])
```

```python
import functools

import jax
import jax.numpy as jnp
from jax import lax
from jax.experimental import pallas as pl
from jax.experimental.pallas import tpu as pltpu
from jax.experimental.pallas import tpu_sc as plsc

F32 = jnp.float32
BF16 = jnp.bfloat16

HEAD_DIM = 128
CHUNK = 128
SUB = 16
EXP_CAP = 80.0
NORM_EPS = 1e-6
L2_EPS = 1e-6
CONV_W = 4
VMEM_LIMIT = 56 * 1024 * 1024

ADAM_LR, ADAM_B1, ADAM_B2, ADAM_EPS, ADAM_WD, ADAM_STEP = 1e-3, 0.9, 0.999, 1e-8, 0.01, 10

NN = ((1,), (0,))
NT = ((1,), (1,))
TN = ((0,), (0,))
MESH = pl.DeviceIdType.MESH


def _dot(a, b, dims):
    return lax.dot_general(a.astype(BF16), b.astype(BF16), (dims, ((), ())),
                           preferred_element_type=F32)


def _split(a):
    hi = a.astype(BF16)
    return hi, (a - hi.astype(F32)).astype(BF16)


def _dot3(a, b, dims):
    ah, al = _split(a)
    bh, bl = _split(b)
    d = lambda x, y: lax.dot_general(x, y, (dims, ((), ())), preferred_element_type=F32)
    return d(ah, bh) + (d(ah, bl) + d(al, bh))


def _sigmoid(x):
    return 1.0 / (1.0 + jnp.exp(-x))


def _silu(x):
    return x * _sigmoid(x)


def _dsilu(x):
    s = _sigmoid(x)
    return s * (1.0 + x * (1.0 - s))


def _softplus(x):
    e = jnp.exp(-jnp.abs(x))
    u = 1.0 + e
    log1p = jnp.where(u == 1.0, e, jnp.log(u) * (e / jnp.where(u == 1.0, 1.0, u - 1.0)))
    return jnp.maximum(x, 0.0) + log1p


def _iota(shape, axis):
    return lax.broadcasted_iota(jnp.int32, shape, axis)


def _cumsum_rows(x):
    n = x.shape[0]
    row = _iota(x.shape, 0)
    s = 1
    while s < n:
        x = x + jnp.where(row >= s, pltpu.roll(x, s, 0), 0.0)
        s *= 2
    return x


def _rev_cumsum_rows(x):
    return jnp.sum(x, axis=0, keepdims=True) - _cumsum_rows(x) + x


def _params(sem):
    return pltpu.CompilerParams(dimension_semantics=sem, vmem_limit_bytes=VMEM_LIMIT)


ROW_TILE = 8
HEADS_PER_STEP = 8


def _hps(h):
    return min(HEADS_PER_STEP, h)


def _head_view(ref, hb):
    if len(ref.shape) == 2:
        return ref.at[:, pl.ds(hb * HEAD_DIM, HEAD_DIM)]
    return ref.at[hb]


class _Staged:
    def __init__(self, ref, load):
        self.ref = ref
        self.loaded = ref[...] if load else None
        self.written = None

    def __getitem__(self, idx):
        return self.loaded

    def __setitem__(self, idx, value):
        self.written = value


def _each_head(one_head, n_in):
    def body(*refs):
        @pl.when(pl.program_id(1) == 0)
        def _():
            refs[-1][...] = jnp.zeros_like(refs[-1])

        last = len(refs) - 1
        staged = [[_Staged(_head_view(r, hb), i < n_in or i == last) for i, r in enumerate(refs)]
                  for hb in range(refs[-1].shape[0])]
        running = [one_head(*per_head) for per_head in staged]
        while running:
            for gen in list(running):
                try:
                    next(gen)
                except StopIteration:
                    running.remove(gen)
        for per_head in staged:
            for s in per_head:
                if s.written is not None:
                    s.ref[...] = s.written
    return body


def _tile(n, want):
    t = min(n, want)
    while n % t:
        t //= 2
    return t


def _mm(a, b, mode, out_dtypes, name, epi=None, extras=(), tm=1024, tn=1024, tk=2048,
        b_stacked=False, out_stacked=False):
    if mode == "tn":
        kdim, m = a.shape
    else:
        m, kdim = a.shape
    if b_stacked:
        n = N_CHIPS * b.shape[2] if mode == "nn" else b.shape[1]
        kdim_b = b.shape[1] if mode == "nn" else N_CHIPS * b.shape[2]
        assert kdim_b == kdim
    else:
        n = b.shape[0] if mode == "nt" else b.shape[1]
    per_shard = (n if (mode == "nn" or out_stacked) else kdim) // N_CHIPS
    tm, tn, tk = _tile(m, tm), _tile(n, tn), _tile(kdim, tk)
    if (b_stacked and mode == "nn") or out_stacked:
        tn = _tile(per_shard, tn)
    if b_stacked and mode == "nt":
        tk = _tile(per_shard, tk)
    nk = kdim // tk
    dims = {"nn": NN, "nt": NT, "tn": TN}[mode]
    a_spec = (pl.BlockSpec((tk, tm), lambda i, j, k: (k, i)) if mode == "tn"
              else pl.BlockSpec((tm, tk), lambda i, j, k: (i, k)))
    if b_stacked and mode == "nn":
        per = per_shard // tn
        b_spec = pl.BlockSpec((None, tk, tn), lambda i, j, k: (j // per, k, j % per))
    elif b_stacked:
        per = per_shard // tk
        b_spec = pl.BlockSpec((None, tn, tk), lambda i, j, k: (k // per, j, k % per))
    else:
        b_spec = (pl.BlockSpec((tn, tk), lambda i, j, k: (j, k)) if mode == "nt"
                  else pl.BlockSpec((tk, tn), lambda i, j, k: (k, j)))
    mn_spec = pl.BlockSpec((tm, tn), lambda i, j, k: (i, j))
    if out_stacked:
        per_o = per_shard // tn
        out_spec = pl.BlockSpec((None, tm, tn), lambda i, j, k: (j // per_o, i, j % per_o))
        out_shape = (N_CHIPS, m, per_shard)
    else:
        out_spec, out_shape = mn_spec, (m, n)
    ne, no = len(extras), len(out_dtypes)
    if epi is None:
        epi = lambda acc: (acc,)

    def body(a_ref, b_ref, *rest):
        extra_refs, out_refs = rest[:ne], rest[ne:ne + no]
        part = _dot(a_ref[...], b_ref[...], dims)

        def finish(total):
            outs = epi(total, *[r[...] for r in extra_refs])
            for o_ref, o in zip(out_refs, outs):
                o_ref[...] = o.astype(o_ref.dtype)

        if nk == 1:
            finish(part)
            return
        acc = rest[-1]
        k = pl.program_id(2)

        @pl.when(k == 0)
        def _():
            acc[...] = part

        @pl.when(jnp.logical_and(k > 0, k < nk - 1))
        def _():
            acc[...] += part

        @pl.when(k == nk - 1)
        def _():
            finish(acc[...] + part)

    outs = pl.pallas_call(
        body, name=name,
        out_shape=tuple(jax.ShapeDtypeStruct(out_shape, d) for d in out_dtypes),
        grid=(m // tm, n // tn, nk),
        in_specs=[a_spec, b_spec] + [mn_spec] * ne,
        out_specs=tuple(out_spec for _ in out_dtypes),
        scratch_shapes=[pltpu.VMEM((tm, tn), F32)] if nk > 1 else [],
        compiler_params=_params(("parallel", "parallel", "arbitrary")),
    )(a, b, *extras)
    return outs if no > 1 else outs[0]


ROWS = 256


def _rms_fwd(x, w, name):
    t, d = x.shape
    tr = _tile(t, ROWS)

    def body(x_ref, w_ref, n_ref):
        xv = x_ref[...]
        r = lax.rsqrt(jnp.mean(xv * xv, axis=-1, keepdims=True) + NORM_EPS)
        n_ref[...] = (xv * r * w_ref[...]).astype(n_ref.dtype)

    return pl.pallas_call(
        body, name=name, out_shape=jax.ShapeDtypeStruct((t, d), BF16), grid=(t // tr,),
        in_specs=[pl.BlockSpec((tr, d), lambda i: (i, 0)), pl.BlockSpec((1, d), lambda i: (0, 0))],
        out_specs=pl.BlockSpec((tr, d), lambda i: (i, 0)),
        compiler_params=_params(("parallel",)),
    )(x, w.reshape(1, d))


def _rms_bwd(dn, x, w, dres, name):
    t, d = x.shape
    tr = _tile(t, ROWS)

    def body(dn_ref, x_ref, w_ref, dres_ref, dx_ref, dxb_ref, dw_ref):
        i = pl.program_id(0)
        xv, dnv = x_ref[...], dn_ref[...]
        r = lax.rsqrt(jnp.mean(xv * xv, axis=-1, keepdims=True) + NORM_EPS)
        xh = xv * r
        dxh = dnv * w_ref[...]
        dx = dres_ref[...] + r * (dxh - xh * jnp.mean(dxh * xh, axis=-1, keepdims=True))
        dx_ref[...] = dx
        dxb_ref[...] = dx.astype(BF16)

        @pl.when(i == 0)
        def _():
            dw_ref[...] = jnp.zeros_like(dw_ref)

        dw_ref[...] += jnp.sum(dnv * xh, axis=0, keepdims=True)

    row = pl.BlockSpec((tr, d), lambda i: (i, 0))
    vec = pl.BlockSpec((1, d), lambda i: (0, 0))
    return pl.pallas_call(
        body, name=name,
        out_shape=(jax.ShapeDtypeStruct((t, d), F32), jax.ShapeDtypeStruct((t, d), BF16),
                   jax.ShapeDtypeStruct((1, d), F32)),
        grid=(t // tr,), in_specs=[row, row, vec, row], out_specs=(row, row, vec),
        compiler_params=_params(("arbitrary",)),
    )(dn, x, w.reshape(1, d), dres)


def _loss_head(h, w, target):
    t, d = h.shape
    tr = _tile(t, ROWS)

    def body(h_ref, w_ref, t_ref, loss_ref, dh_ref, dhb_ref, dw_ref):
        i = pl.program_id(0)
        hv, wv = h_ref[...], w_ref[...]
        r = lax.rsqrt(jnp.mean(hv * hv, axis=-1, keepdims=True) + NORM_EPS)
        hh = hv * r
        err = hh * wv - t_ref[...]
        dout = err * (1.0 / d)
        dhh = dout * wv
        dh = r * (dhh - hh * jnp.mean(dhh * hh, axis=-1, keepdims=True))
        dh_ref[...] = dh
        dhb_ref[...] = dh.astype(BF16)

        @pl.when(i == 0)
        def _():
            dw_ref[...] = jnp.zeros_like(dw_ref)
            loss_ref[...] = jnp.zeros_like(loss_ref)

        dw_ref[...] += jnp.sum(dout * hh, axis=0, keepdims=True)
        loss_ref[...] += jnp.full((1, 128), 0.5 / d, F32) * jnp.sum(err * err)

    row = pl.BlockSpec((tr, d), lambda i: (i, 0))
    vec = pl.BlockSpec((1, d), lambda i: (0, 0))
    lspec = pl.BlockSpec((1, 128), lambda i: (0, 0))
    return pl.pallas_call(
        body, name="loss_head",
        out_shape=(jax.ShapeDtypeStruct((1, 128), F32), jax.ShapeDtypeStruct((t, d), F32),
                   jax.ShapeDtypeStruct((t, d), BF16), jax.ShapeDtypeStruct((1, d), F32)),
        grid=(t // tr,), in_specs=[row, vec, row], out_specs=(lspec, row, row, vec),
        compiler_params=_params(("arbitrary",)),
    )(h, w.reshape(1, d), target)


def _inv_unit_lower(a):
    c = a.shape[0]
    eye = (_iota((c, c), 0) == _iota((c, c), 1)).astype(F32)
    x = eye - a
    p = _dot3(a, a, NN)
    yield
    n = 2
    while n < c:
        x = x + _dot3(x, p, NN)
        n *= 2
        if n < c:
            p = _dot3(p, p, NN)
        yield
    return x


def _gdn_chunk(q, k, v, beta, g):
    c = q.shape[0]
    row, col = _iota((c, c), 0), _iota((c, c), 1)
    gc = _cumsum_rows(g)
    diff = gc - gc.T
    dec = jnp.where(row >= col, jnp.exp(jnp.minimum(diff, 0.0)), 0.0)
    dec_s = jnp.where(row > col, dec, 0.0)
    gam = jnp.exp(gc)
    g_last = jnp.sum(g, axis=0, keepdims=True)
    kk = _dot(k, k, NT)
    a = beta * kk * dec_s
    p = _dot(q, k, NT) * dec
    e_end = jnp.exp(g_last - gc)
    return dict(dec=dec, dec_s=dec_s, gam=gam, gam_last=jnp.exp(g_last), e_end=e_end,
                k_end=k * e_end, kk=kk, a=a, p=p)


def _gdn_fwd(q, k, v, beta_bc, g_bc):
    t = q.shape[0]
    h = q.shape[1] // HEAD_DIM
    nc = t // CHUNK

    def body(q_ref, k_ref, v_ref, b_ref, g_ref, o_ref, s_ref, t_ref, state):
        qv, kv, vv, beta = q_ref[...], k_ref[...], v_ref[...], b_ref[...]
        ch = _gdn_chunk(qv, kv, vv, beta, g_ref[...])
        yield
        tm = yield from _inv_unit_lower(ch["a"])
        sol = _dot(tm, jnp.concatenate([beta * vv, beta * ch["gam"] * kv], axis=1), NN)
        yield
        u_v, w = sol[:, :HEAD_DIM], sol[:, HEAD_DIM:]
        s0 = state[...]
        u = u_v - _dot(w, s0, NN)
        yield
        o_ref[...] = _dot(qv * ch["gam"], s0, NN) + _dot(ch["p"], u, NN)
        s_ref[...] = s0
        t_ref[...] = tm
        state[...] = ch["gam_last"] * s0 + _dot(ch["k_end"], u, TN)

    tok = pl.BlockSpec((CHUNK, _hps(h) * HEAD_DIM), lambda hh, c: (c, hh))
    bc = pl.BlockSpec((_hps(h), CHUNK, HEAD_DIM), lambda hh, c: (hh, c, 0))
    mat = pl.BlockSpec((_hps(h), None, HEAD_DIM, HEAD_DIM), lambda hh, c: (hh, c, 0, 0))
    return pl.pallas_call(
        _each_head(body, 5), name="gdn_fwd",
        out_shape=(jax.ShapeDtypeStruct(q.shape, F32),
                   jax.ShapeDtypeStruct((h, nc, HEAD_DIM, HEAD_DIM), F32),
                   jax.ShapeDtypeStruct((h, nc, CHUNK, CHUNK), F32)),
        grid=(h // _hps(h), nc), in_specs=[tok, tok, tok, bc, bc], out_specs=(tok, mat, mat),
        scratch_shapes=[pltpu.VMEM((_hps(h), HEAD_DIM, HEAD_DIM), F32)],
        compiler_params=_params(("parallel", "arbitrary")),
    )(q, k, v, beta_bc, g_bc)


def _gdn_bwd(q, k, v, beta_bc, g_bc, states, invs, do, do_blk=0):
    t = q.shape[0]
    h = q.shape[1] // HEAD_DIM
    nc = t // CHUNK

    def body(q_ref, k_ref, v_ref, b_ref, g_ref, s_ref, t_ref, do_ref,
             dq_ref, dk_ref, dv_ref, db_ref, dg_ref, dstate):
        qv, kv, vv, beta = q_ref[...], k_ref[...], v_ref[...], b_ref[...]
        dov, s0, tm, ds1 = do_ref[...], s_ref[...], t_ref[...], dstate[...]
        ch = _gdn_chunk(qv, kv, vv, beta, g_ref[...])
        yield
        gam, dec, dec_s, kk = ch["gam"], ch["dec"], ch["dec_s"], ch["kk"]
        r_v, r_w = beta * vv, beta * gam * kv
        sol = _dot(tm, jnp.concatenate([r_v, r_w], axis=1), NN)
        yield
        u_v, w = sol[:, :HEAD_DIM], sol[:, HEAD_DIM:]
        u = u_v - _dot(w, s0, NN)
        qg = qv * gam
        yield

        du = _dot(ch["p"], dov, TN) + _dot(ch["k_end"], ds1, NN)
        dp = _dot(dov, u, NT)
        dpd = dp * dec
        dqg = _dot(dov, s0, NT)
        dk_end = _dot(u, ds1, NT)
        yield
        dq = dqg * gam + _dot(dpd, kv, NN)
        dk = _dot(dpd, qv, TN) + dk_end * ch["e_end"]
        dstate[...] = _dot(qg, dov, TN) + ch["gam_last"] * ds1 - _dot(w, du, TN)
        dw = -_dot(du, s0, NT)
        yield
        dr = _dot(tm, jnp.concatenate([du, dw], axis=1), TN)
        yield
        dr_v, dr_w = dr[:, :HEAD_DIM], dr[:, HEAD_DIM:]
        da = -_dot(dr, sol, NT)
        yield
        dkk = da * beta * dec_s
        dk = dk + _dot(dkk, kv, NN) + _dot(dkk, kv, TN) + beta * gam * dr_w
        dbeta = (jnp.sum(da * kk * dec_s, axis=1, keepdims=True)
                 + jnp.sum(dr_v * vv + dr_w * gam * kv, axis=1, keepdims=True))

        pair = dp * ch["p"] + da * ch["a"]
        end = jnp.sum(dk_end * ch["k_end"], axis=1, keepdims=True)
        dgc = (jnp.sum(pair - pair.T, axis=1, keepdims=True)
               + jnp.sum(dqg * qg + dr_w * r_w, axis=1, keepdims=True) - end)
        at_end = jnp.sum(end) + ch["gam_last"] * jnp.sum(s0 * ds1)
        dgc = jnp.broadcast_to(dgc, (CHUNK, HEAD_DIM))
        dgc = dgc + jnp.where(_iota((CHUNK, HEAD_DIM), 0) == CHUNK - 1, at_end, 0.0)
        dq_ref[...] = dq
        dk_ref[...] = dk
        dv_ref[...] = beta * dr_v
        db_ref[...] = jnp.broadcast_to(dbeta, (CHUNK, HEAD_DIM)).T[:ROW_TILE]
        dg_ref[...] = _rev_cumsum_rows(dgc).T[:ROW_TILE]

    rev = lambda c: nc - 1 - c
    tok = pl.BlockSpec((CHUNK, _hps(h) * HEAD_DIM), lambda hh, c: (rev(c), hh))
    bc = pl.BlockSpec((_hps(h), CHUNK, HEAD_DIM), lambda hh, c: (hh, rev(c), 0))
    mat = pl.BlockSpec((_hps(h), None, HEAD_DIM, HEAD_DIM), lambda hh, c: (hh, rev(c), 0, 0))
    tok_shape = jax.ShapeDtypeStruct(q.shape, F32)
    row_shape = jax.ShapeDtypeStruct((h, nc, ROW_TILE, CHUNK), F32)
    rows = pl.BlockSpec((_hps(h), None, ROW_TILE, CHUNK), lambda hh, c: (hh, rev(c), 0, 0))
    return pl.pallas_call(
        _each_head(body, 8), name="gdn_bwd",
        out_shape=(tok_shape, tok_shape, tok_shape, row_shape, row_shape),
        grid=(h // _hps(h), nc),
        in_specs=[tok, tok, tok, bc, bc, mat, mat,
                  pl.BlockSpec((CHUNK, _hps(h) * HEAD_DIM), lambda hh, c: (rev(c), do_blk // _hps(h) + hh))],
        out_specs=(tok, tok, tok, rows, rows),
        scratch_shapes=[pltpu.VMEM((_hps(h), HEAD_DIM, HEAD_DIM), F32)],
        compiler_params=_params(("parallel", "arbitrary")),
    )(q, k, v, beta_bc, g_bc, states, invs, do)


def _hgrn_chunk(q, k, lf):
    c = q.shape[0]
    row = _iota((c, HEAD_DIM), 0)
    b = _cumsum_rows(lf)
    q_subs, k_facs, a_rows = [], [], []
    for x in range(c // SUB):
        b_start = jnp.sum(jnp.where(row < x * SUB, lf, 0.0), axis=0, keepdims=True)
        q_x = (q * jnp.exp(jnp.minimum(b - b_start, 0.0)))[x * SUB:(x + 1) * SUB]
        k_fac = jnp.where(row < (x + 1) * SUB, jnp.exp(jnp.minimum(b_start - b, EXP_CAP)), 0.0)
        q_subs.append(q_x)
        k_facs.append(k_fac)
        a_rows.append(_dot(q_x, k * k_fac, NT))
    a = jnp.concatenate(a_rows, axis=0)
    a = jnp.where(_iota((c, c), 0) >= _iota((c, c), 1), a, 0.0)
    b_last = jnp.sum(lf, axis=0, keepdims=True)
    return dict(b=b, a=a, q_subs=q_subs, k_facs=k_facs, e_b=jnp.exp(b),
                e_end=jnp.exp(b_last - b), e_last=jnp.exp(b_last))


def _hgrn_fwd(q, k, v, lf, v_blk=0):
    t = q.shape[0]
    h = q.shape[1] // HEAD_DIM
    nc = t // CHUNK

    def body(q_ref, k_ref, v_ref, lf_ref, o_ref, s_ref, state):
        qv, kv, vv = q_ref[...], k_ref[...], v_ref[...]
        ch = _hgrn_chunk(qv, kv, lf_ref[...])
        yield
        s0 = state[...]
        o_ref[...] = _dot(qv * ch["e_b"], s0, NT) + _dot(ch["a"], vv, NN)
        s_ref[...] = s0
        state[...] = s0 * ch["e_last"] + _dot(vv, kv * ch["e_end"], TN)

    tok = pl.BlockSpec((CHUNK, _hps(h) * HEAD_DIM), lambda hh, c: (c, hh))
    mat = pl.BlockSpec((_hps(h), None, HEAD_DIM, HEAD_DIM), lambda hh, c: (hh, c, 0, 0))
    return pl.pallas_call(
        _each_head(body, 4), name="hgrn_fwd",
        out_shape=(jax.ShapeDtypeStruct(q.shape, F32),
                   jax.ShapeDtypeStruct((h, nc, HEAD_DIM, HEAD_DIM), F32)),
        grid=(h // _hps(h), nc),
        in_specs=[tok, tok, pl.BlockSpec((CHUNK, _hps(h) * HEAD_DIM), lambda hh, c: (c, v_blk // _hps(h) + hh)), tok],
        out_specs=(tok, mat),
        scratch_shapes=[pltpu.VMEM((_hps(h), HEAD_DIM, HEAD_DIM), F32)],
        compiler_params=_params(("parallel", "arbitrary")),
    )(q, k, v, lf)


def _hgrn_bwd(q, k, v, lf, states, do, v_blk=0, do_blk=0):
    t = q.shape[0]
    nc = t // CHUNK
    h = q.shape[1] // HEAD_DIM

    def body(q_ref, k_ref, v_ref, lf_ref, s_ref, do_ref, dq_ref, dk_ref, dv_ref, dlf_ref, dstate):
        qv, kv, vv, dov, s0 = q_ref[...], k_ref[...], v_ref[...], do_ref[...], s_ref[...]
        ds1 = dstate[...]
        ch = _hgrn_chunk(qv, kv, lf_ref[...])
        yield
        c = CHUNK
        row = _iota((c, HEAD_DIM), 0)
        qh = qv * ch["e_b"]
        k_end = kv * ch["e_end"]
        da = jnp.where(_iota((c, c), 0) >= _iota((c, c), 1), _dot(dov, vv, NT), 0.0)
        dqh = _dot(dov, s0, NN)
        dk_end = _dot(vv, ds1, NN)
        yield
        end = dk_end * k_end
        dk = dk_end * ch["e_end"]
        db = dqh * qh - end + jnp.where(
            row == c - 1, jnp.sum(end + s0 * ch["e_last"] * ds1, axis=0, keepdims=True), 0.0)
        dq_rows, qdq_rows = [], []
        for x in range(c // SUB):
            da_x = da[x * SUB:(x + 1) * SUB]
            k_x = kv * ch["k_facs"][x]
            dq_x = _dot(da_x, k_x, NN)
            dk_x = _dot(da_x, ch["q_subs"][x], TN)
            dq_rows.append(dq_x)
            qdq_rows.append(dq_x * ch["q_subs"][x])
            dk = dk + dk_x * ch["k_facs"][x]
            kdk = dk_x * k_x
            db = db - kdk
            if x > 0:
                at_start = jnp.sum(kdk, axis=0, keepdims=True) - jnp.sum(qdq_rows[x], axis=0, keepdims=True)
                db = db + jnp.where(row == x * SUB - 1, at_start, 0.0)
        yield
        b_start = jnp.zeros((c, HEAD_DIM), F32)
        for x in range(1, c // SUB):
            b_x = jnp.sum(jnp.where(row < x * SUB, lf_ref[...], 0.0), axis=0, keepdims=True)
            b_start = jnp.where(row >= x * SUB, b_x, b_start)
        dq = dqh * ch["e_b"] + jnp.concatenate(dq_rows, axis=0) * jnp.exp(jnp.minimum(ch["b"] - b_start, 0.0))
        db = db + jnp.concatenate(qdq_rows, axis=0)
        dstate[...] = _dot(dov, qh, TN) + ds1 * ch["e_last"]
        dq_ref[...] = dq
        dk_ref[...] = dk
        dv_ref[...] = _dot(ch["a"], dov, TN) + _dot(k_end, ds1, NT)
        dlf_ref[...] = _rev_cumsum_rows(db)

    rev = lambda c: nc - 1 - c
    tok = pl.BlockSpec((CHUNK, _hps(h) * HEAD_DIM), lambda hh, c: (rev(c), hh))
    mat = pl.BlockSpec((_hps(h), None, HEAD_DIM, HEAD_DIM), lambda hh, c: (hh, rev(c), 0, 0))
    tok_shape = jax.ShapeDtypeStruct(q.shape, F32)
    return pl.pallas_call(
        _each_head(body, 6), name="hgrn_bwd",
        out_shape=(tok_shape, tok_shape, tok_shape, tok_shape),
        grid=(h // _hps(h), nc),
        in_specs=[tok, tok, pl.BlockSpec((CHUNK, _hps(h) * HEAD_DIM), lambda hh, c: (rev(c), v_blk // _hps(h) + hh)), tok, mat,
                  pl.BlockSpec((CHUNK, _hps(h) * HEAD_DIM), lambda hh, c: (rev(c), do_blk // _hps(h) + hh))],
        out_specs=(tok, tok, tok, tok),
        scratch_shapes=[pltpu.VMEM((_hps(h), HEAD_DIM, HEAD_DIM), F32)],
        compiler_params=_params(("parallel", "arbitrary")),
    )(q, k, v, lf, states, do)


CONV_ROWS = 256
HALO = 8


def _shift_down(cur, prev, s):
    rt = cur.shape[0]
    head = jnp.concatenate([pltpu.roll(prev, s, 0), jnp.zeros((rt - HALO, cur.shape[1]), F32)], axis=0)
    return jnp.where(_iota(cur.shape, 0) < s, head, pltpu.roll(cur, s, 0))


def _shift_up(cur, nxt, s):
    rt = cur.shape[0]
    tail = jnp.concatenate([jnp.zeros((rt - HALO, cur.shape[1]), F32), pltpu.roll(nxt, HALO - s, 0)], axis=0)
    return jnp.where(_iota(cur.shape, 0) >= rt - s, tail, pltpu.roll(cur, rt - s, 0))


def _tile_with_prev(ref, i, rt):
    r0 = pl.multiple_of(i * rt, rt)
    cur = ref[pl.ds(r0, rt), :]
    prev = ref[pl.ds(pl.multiple_of(jnp.maximum(r0 - HALO, 0), HALO), HALO), :]
    return cur, jnp.where(i > 0, prev, 0.0)


def _tile_with_next(ref, i, rt, n_tiles):
    r0 = pl.multiple_of(i * rt, rt)
    cur = ref[pl.ds(r0, rt), :]
    nxt = ref[pl.ds(pl.multiple_of(jnp.minimum(r0 + rt, (n_tiles - 1) * rt), HALO), HALO), :]
    return cur, jnp.where(i < n_tiles - 1, nxt, 0.0)


def _conv_tile(x_ref, w_ref, i, rt):
    cur, prev = _tile_with_prev(x_ref, i, rt)
    shifted = [_shift_down(cur, prev, CONV_W - 1 - j) for j in range(CONV_W - 1)] + [cur]
    c = shifted[0] * w_ref[pl.ds(0, 1), :]
    for j in range(1, CONV_W):
        c = c + shifted[j] * w_ref[pl.ds(j, 1), :]
    return c, shifted


def _l2n(s):
    return s * lax.rsqrt(jnp.sum(s * s, axis=-1, keepdims=True) + L2_EPS)


def _gdn_prep_fwd(proj, conv_w, h):
    t = proj.shape[0]
    rt = _tile(t, CONV_ROWS)
    nt = t // rt
    scale = HEAD_DIM ** -0.5

    def body(xq, xk, xv, wq, wk, wv, q_ref, k_ref, v_ref):
        def tile(i, carry):
            rows = pl.ds(pl.multiple_of(i * rt, rt), rt)
            q_ref[rows, :] = _l2n(_silu(_conv_tile(xq, wq, i, rt)[0])) * scale
            k_ref[rows, :] = _l2n(_silu(_conv_tile(xk, wk, i, rt)[0]))
            v_ref[rows, :] = _silu(_conv_tile(xv, wv, i, rt)[0])
            return carry

        lax.fori_loop(0, nt, tile, 0)

    col = lambda p: pl.BlockSpec((t, HEAD_DIM), lambda hh: (0, p * h + hh))
    wcol = lambda p: pl.BlockSpec((CONV_W, HEAD_DIM), lambda hh: (0, p * h + hh))
    out = pl.BlockSpec((t, HEAD_DIM), lambda hh: (0, hh))
    shape = jax.ShapeDtypeStruct((t, h * HEAD_DIM), F32)
    return pl.pallas_call(
        body, name="gdn_prep_fwd", out_shape=(shape, shape, shape), grid=(h,),
        in_specs=[col(0), col(1), col(2), wcol(0), wcol(1), wcol(2)], out_specs=(out, out, out),
        compiler_params=_params(("parallel",)),
    )(proj, proj, proj, conv_w, conv_w, conv_w)


def _gdn_prep_bwd(proj, conv_w, dq, dk, dv, h):
    t = proj.shape[0]
    rt = _tile(t, CONV_ROWS)
    nt = t // rt
    scale = HEAD_DIM ** -0.5

    def part(x_ref, w_ref, dy_ref, dx_ref, dw_ref, dc_ref, norm_scale):
        def first(i, dws):
            rows = pl.ds(pl.multiple_of(i * rt, rt), rt)
            c, shifted = _conv_tile(x_ref, w_ref, i, rt)
            ds = dy_ref[rows, :]
            if norm_scale is not None:
                s = _silu(c)
                r = lax.rsqrt(jnp.sum(s * s, axis=-1, keepdims=True) + L2_EPS)
                y = s * r
                dyn = ds * norm_scale
                ds = r * (dyn - y * jnp.sum(dyn * y, axis=-1, keepdims=True))
            dc = ds * _dsilu(c)
            dc_ref[rows, :] = dc
            return tuple(dws[j] + jnp.sum(dc * shifted[j], axis=0, keepdims=True) for j in range(CONV_W))

        dws = lax.fori_loop(0, nt, first, tuple(jnp.zeros((1, HEAD_DIM), F32) for _ in range(CONV_W)))
        for j in range(CONV_W):
            dw_ref[pl.ds(j, 1), :] = dws[j]

        def second(i, carry):
            rows = pl.ds(pl.multiple_of(i * rt, rt), rt)
            cur, nxt = _tile_with_next(dc_ref, i, rt, nt)
            dx = cur * w_ref[pl.ds(CONV_W - 1, 1), :]
            for j in range(CONV_W - 1):
                dx = dx + _shift_up(cur, nxt, CONV_W - 1 - j) * w_ref[pl.ds(j, 1), :]
            dx_ref[rows, :] = dx.astype(dx_ref.dtype)
            return carry

        lax.fori_loop(0, nt, second, 0)

    def body(xq, xk, xv, wq, wk, wv, dq_ref, dk_ref, dv_ref, dxq, dxk, dxv, dwq, dwk, dwv, dc_ref):
        part(xq, wq, dq_ref, dxq, dwq, dc_ref, scale)
        part(xk, wk, dk_ref, dxk, dwk, dc_ref, 1.0)
        part(xv, wv, dv_ref, dxv, dwv, dc_ref, None)

    col = lambda p: pl.BlockSpec((t, HEAD_DIM), lambda hh: (0, p * h + hh))
    wcol = lambda p: pl.BlockSpec((CONV_W, HEAD_DIM), lambda hh: (0, p * h + hh))
    own = pl.BlockSpec((t, HEAD_DIM), lambda hh: (0, hh))
    wown = pl.BlockSpec((CONV_W, HEAD_DIM), lambda hh: (0, hh))
    dx_shape = jax.ShapeDtypeStruct((t, h * HEAD_DIM), BF16)
    dw_shape = jax.ShapeDtypeStruct((CONV_W, h * HEAD_DIM), F32)
    return pl.pallas_call(
        body, name="gdn_prep_bwd",
        out_shape=(dx_shape, dx_shape, dx_shape, dw_shape, dw_shape, dw_shape), grid=(h,),
        in_specs=[col(0), col(1), col(2), wcol(0), wcol(1), wcol(2), own, own, own],
        out_specs=(own, own, own, wown, wown, wown),
        scratch_shapes=[pltpu.VMEM((t, HEAD_DIM), F32)],
        compiler_params=_params(("parallel",)),
    )(proj, proj, proj, conv_w, conv_w, conv_w, dq, dk, dv)


def _gdn_gates_fwd(ab, a_log_row, dt_bias_row):
    t = ab.shape[0]
    tr = _tile(t, 512)

    def body(ab_ref, al_ref, dt_ref, g_ref, b_ref):
        g_ref[...] = -jnp.exp(al_ref[...]) * _softplus(ab_ref[:, :HEAD_DIM] + dt_ref[...])
        b_ref[...] = _sigmoid(ab_ref[:, HEAD_DIM:])

    row = pl.BlockSpec((tr, HEAD_DIM), lambda i: (i, 0))
    vec = pl.BlockSpec((1, HEAD_DIM), lambda i: (0, 0))
    shape = jax.ShapeDtypeStruct((t, HEAD_DIM), F32)
    return pl.pallas_call(
        body, name="gdn_gates_fwd", out_shape=(shape, shape), grid=(t // tr,),
        in_specs=[pl.BlockSpec((tr, 2 * HEAD_DIM), lambda i: (i, 0)), vec, vec], out_specs=(row, row),
        compiler_params=_params(("parallel",)),
    )(ab, a_log_row, dt_bias_row)


def _gdn_gates_bwd(ab, a_log_row, dt_bias_row, dg, dbeta):
    t = ab.shape[0]
    tr = _tile(t, 512)

    def body(ab_ref, al_ref, dt_ref, dg_ref, db_ref, dab_ref, dal_ref, ddt_ref):
        @pl.when(pl.program_id(0) == 0)
        def _():
            dal_ref[...] = jnp.zeros_like(dal_ref)
            ddt_ref[...] = jnp.zeros_like(ddt_ref)

        xa = ab_ref[:, :HEAD_DIM] + dt_ref[...]
        neg_a = -jnp.exp(al_ref[...])
        dgv = dg_ref[...]
        da = dgv * neg_a * _sigmoid(xa)
        beta = _sigmoid(ab_ref[:, HEAD_DIM:])
        dab_ref[:, :HEAD_DIM] = da.astype(BF16)
        dab_ref[:, HEAD_DIM:] = (db_ref[...] * beta * (1.0 - beta)).astype(BF16)
        dal_ref[...] += jnp.sum(dgv * neg_a * _softplus(xa), axis=0, keepdims=True)
        ddt_ref[...] += jnp.sum(da, axis=0, keepdims=True)

    row = pl.BlockSpec((tr, HEAD_DIM), lambda i: (i, 0))
    row2 = pl.BlockSpec((tr, 2 * HEAD_DIM), lambda i: (i, 0))
    vec = pl.BlockSpec((1, HEAD_DIM), lambda i: (0, 0))
    vshape = jax.ShapeDtypeStruct((1, HEAD_DIM), F32)
    return pl.pallas_call(
        body, name="gdn_gates_bwd",
        out_shape=(jax.ShapeDtypeStruct((t, 2 * HEAD_DIM), BF16), vshape, vshape), grid=(t // tr,),
        in_specs=[row2, vec, vec, row, row], out_specs=(row2, vec, vec),
        compiler_params=_params(("arbitrary",)),
    )(ab, a_log_row, dt_bias_row, dg, dbeta)


def _lower_bound(lb_ref):
    return _sigmoid(lb_ref[pl.ds(0, 1), :] - lb_ref[pl.ds(1, 1), :])


def _hgrn_prep_fwd(proj, lb_logits, h, q_blk, f_blk):
    t = proj.shape[0]
    tr = _tile(t, 512)

    def body(xq, xf, lb_ref, q_ref, k_ref, lf_ref):
        lb = _lower_bound(lb_ref)
        s = _sigmoid(xf[...])
        q_ref[...] = _silu(xq[...])
        k_ref[...] = (1.0 - lb) * (1.0 - s)
        lf_ref[...] = jnp.log(lb + (1.0 - lb) * s)

    width = h * HEAD_DIM
    col = lambda b0: pl.BlockSpec((tr, width), lambda i: (i, b0 // h))
    own = pl.BlockSpec((tr, width), lambda i: (i, 0))
    shape = jax.ShapeDtypeStruct((t, width), F32)
    return pl.pallas_call(
        body, name="hgrn_prep_fwd", out_shape=(shape, shape, shape), grid=(t // tr,),
        in_specs=[col(q_blk), col(f_blk), pl.BlockSpec((2, width), lambda i: (0, 0))],
        out_specs=(own, own, own), compiler_params=_params(("parallel",)),
    )(proj, proj, lb_logits)


def _hgrn_prep_bwd(proj, lb_logits, dq, dk, dlf, h, q_blk, f_blk):
    t = proj.shape[0]
    tr = _tile(t, 512)

    def body(xq, xf, lb_ref, dq_ref, dk_ref, dlf_ref, dxq, dxf, dlb_ref):
        @pl.when(pl.program_id(0) == 0)
        def _():
            dlb_ref[...] = jnp.zeros_like(dlb_ref)

        lb = _lower_bound(lb_ref)
        s = _sigmoid(xf[...])
        e = dlf_ref[...] / (lb + (1.0 - lb) * s) - dk_ref[...]
        dxq[...] = (dq_ref[...] * _dsilu(xq[...])).astype(BF16)
        dxf[...] = (s * (1.0 - s) * (1.0 - lb) * e).astype(BF16)
        d0 = jnp.sum((1.0 - s) * e, axis=0, keepdims=True) * (lb * (1.0 - lb))
        dlb_ref[pl.ds(0, 1), :] += d0
        dlb_ref[pl.ds(1, 1), :] += -d0

    width = h * HEAD_DIM
    col = lambda b0: pl.BlockSpec((tr, width), lambda i: (i, b0 // h))
    own = pl.BlockSpec((tr, width), lambda i: (i, 0))
    lbs = pl.BlockSpec((2, width), lambda i: (0, 0))
    shape = jax.ShapeDtypeStruct((t, width), BF16)
    return pl.pallas_call(
        body, name="hgrn_prep_bwd",
        out_shape=(shape, shape, jax.ShapeDtypeStruct((2, width), F32)), grid=(t // tr,),
        in_specs=[col(q_blk), col(f_blk), lbs, own, own, own], out_specs=(own, own, lbs),
        compiler_params=_params(("arbitrary",)),
    )(proj, proj, lb_logits, dq, dk, dlf)


GATE_HEADS = 4


def _gate_specs(h, z_blk, g_blk, tr):
    g = min(GATE_HEADS, h)
    n = h // g
    width = g * HEAD_DIM
    o_a = pl.BlockSpec((tr, width), lambda gg, i: (i, jnp.minimum(gg, n - 1)))
    o_b = pl.BlockSpec((tr, width), lambda gg, i: (i, jnp.maximum(gg - n, 0)))
    gate = pl.BlockSpec((tr, width), lambda gg, i: (i, jnp.where(gg < n, z_blk // g + gg, g_blk // g + gg - n)))
    w = pl.BlockSpec((None, 1, HEAD_DIM), lambda gg, i: (gg // n, 0, 0))
    cat = pl.BlockSpec((tr, width), lambda gg, i: (i, gg))
    return (o_a, o_b, gate, w, cat), g, n


def _silu_and_grad(x):
    s = _sigmoid(x)
    return x * s, s * (1.0 + x * (1.0 - s))


def _gate_fwd(o_a, o_b, proj, norm_w, h, z_blk, g_blk):
    t = o_a.shape[0]
    tr = _tile(t, 512)

    (sa, sb, sg, sw, cat), g, n = _gate_specs(h, z_blk, g_blk, tr)

    def body(oa_ref, ob_ref, z_ref, w_ref, y_ref):
        for k in range(g):
            lanes = pl.ds(k * HEAD_DIM, HEAD_DIM)
            o = jnp.where(pl.program_id(0) < n, oa_ref[:, lanes], ob_ref[:, lanes])
            r = lax.rsqrt(jnp.mean(o * o, axis=-1, keepdims=True) + NORM_EPS)
            y_ref[:, lanes] = (o * r * w_ref[...] * _silu(z_ref[:, lanes])).astype(y_ref.dtype)

    return pl.pallas_call(
        body, name="gate_fwd", out_shape=jax.ShapeDtypeStruct((t, 2 * h * HEAD_DIM), BF16),
        grid=(2 * n, t // tr), in_specs=[sa, sb, sg, sw], out_specs=cat,
        compiler_params=_params(("parallel", "parallel")),
    )(o_a, o_b, proj, norm_w)


def _gate_bwd(o_a, o_b, proj, norm_w, dy, h, z_blk, g_blk):
    t = o_a.shape[0]
    tr = _tile(t, 512)

    (sa, sb, sg, sw, cat), g, n = _gate_specs(h, z_blk, g_blk, tr)

    def body(oa_ref, ob_ref, z_ref, w_ref, dy_ref, do_ref, dz_ref, dw_ref):
        gg = pl.program_id(0)

        @pl.when(jnp.logical_and(gg % n == 0, pl.program_id(1) == 0))
        def _():
            dw_ref[...] = jnp.zeros_like(dw_ref)

        w = w_ref[...]
        dw = jnp.zeros_like(w)
        for k in range(g):
            lanes = pl.ds(k * HEAD_DIM, HEAD_DIM)
            o = jnp.where(gg < n, oa_ref[:, lanes], ob_ref[:, lanes])
            dyv = dy_ref[:, lanes]
            r = lax.rsqrt(jnp.mean(o * o, axis=-1, keepdims=True) + NORM_EPS)
            oh = o * r
            act, dact = _silu_and_grad(z_ref[:, lanes])
            dz_ref[:, lanes] = (dyv * oh * w * dact).astype(dz_ref.dtype)
            dn = dyv * act
            doh = dn * w
            do_ref[:, lanes] = r * (doh - oh * jnp.mean(doh * oh, axis=-1, keepdims=True))
            dw = dw + jnp.sum(dn * oh, axis=0, keepdims=True)
        dw_ref[...] += dw

    width = 2 * h * HEAD_DIM
    return pl.pallas_call(
        body, name="gate_bwd",
        out_shape=(jax.ShapeDtypeStruct((t, width), F32), jax.ShapeDtypeStruct((t, width), BF16),
                   jax.ShapeDtypeStruct((2, 1, HEAD_DIM), F32)),
        grid=(2 * n, t // tr), in_specs=[sa, sb, sg, sw, cat], out_specs=(cat, cat, sw),
        compiler_params=_params(("arbitrary", "arbitrary")),
    )(o_a, o_b, proj, norm_w, dy)


def _lane_row(vec):
    return jnp.pad(vec.reshape(1, -1), ((0, 0), (0, HEAD_DIM - vec.shape[-1])))


def _add_epi(acc, res):
    return (acc + res,)


def _split_w_in(w_in, h):
    gw = h * HEAD_DIM
    main = jnp.concatenate([w_in[:, :4 * gw], w_in[:, 4 * gw + 2 * h:]], axis=1)
    pad = jnp.zeros((w_in.shape[0], HEAD_DIM - h), w_in.dtype)
    ab = jnp.concatenate([w_in[:, 4 * gw:4 * gw + h], pad, w_in[:, 4 * gw + h:4 * gw + 2 * h], pad], axis=1)
    return main, ab


def _merge_w_in(main, ab, h):
    gw = h * HEAD_DIM
    return jnp.concatenate([main[:, :4 * gw], ab[:, :h], ab[:, HEAD_DIM:HEAD_DIM + h], main[:, 4 * gw:]], axis=1)


def _local_step(x, target, w_main, w_ab, conv_w, a_log, dt_bias, gdn_norm_w, lb_logits, hgrn_norm_w,
                w_out, norm_mix_w, norm_ffn_w, w_ff1, w_ff2, norm_final_w, reducer=None, n1=None):
    t, d = x.shape
    h = d // (2 * HEAD_DIM)
    gw = h * HEAD_DIM
    k_blk, v_blk, z_blk, qb_blk, fb_blk, ib_blk, gb_blk = (i * h for i in range(1, 8))
    del k_blk, v_blk

    if n1 is None:
        n1 = _rms_fwd(x, norm_mix_w, "rms_mix")
    proj = _mm(n1, w_main, "nn", (F32,), "in_proj")
    ab = _mm(n1, w_ab, "nn", (F32,), "in_proj_ab")

    q, k, v = _gdn_prep_fwd(proj, conv_w, h)
    a_log_row, dt_row = _lane_row(a_log), _lane_row(dt_bias)
    g_tm, beta_tm = _gdn_gates_fwd(ab, a_log_row, dt_row)
    to_heads = lambda a: jnp.broadcast_to(a[:, :h].T[:, :, None], (h, t, HEAD_DIM))
    g_bc, beta_bc = to_heads(g_tm), to_heads(beta_tm)
    o_a, st_a, inv_a = _gdn_fwd(q, k, v, beta_bc, g_bc)

    qh, kh, lf = _hgrn_prep_fwd(proj, lb_logits, h, qb_blk, fb_blk)
    o_b, st_b = _hgrn_fwd(qh, kh, proj, lf, v_blk=ib_blk)

    gate_w = jnp.stack([gdn_norm_w.reshape(1, HEAD_DIM), hgrn_norm_w.reshape(1, HEAD_DIM)])
    y = _gate_fwd(o_a, o_b, proj, gate_w, h, z_blk, gb_blk)
    h1 = _mm(y, w_out, "nn", (F32,), "out_proj", epi=_add_epi, extras=(x,))
    n2 = _rms_fwd(h1, norm_ffn_w, "rms_ffn")
    act, r = _mm(n2, w_ff1, "nn", (F32, BF16), "ff1", b_stacked=True,
                 epi=lambda acc: (acc, jnp.square(jnp.maximum(acc, 0.0))))
    h2 = _mm(r, w_ff2, "nn", (F32,), "ff2", epi=_add_epi, extras=(h1,))
    loss, dh2, dh2_b, d_norm_final = _loss_head(h2, norm_final_w, target)

    da = _mm(dh2_b, w_ff2, "nt", (BF16,), "ff2_dx",
             epi=lambda acc, a: (acc * (2.0 * jnp.maximum(a, 0.0)),), extras=(act,))
    pending = []

    def step(anchor, name=None, full=None):
        if reducer is not None:
            pending.extend(reducer.step(name, full, anchor))

    def after_step(value):
        if not pending:
            return value
        value = lax.optimization_barrier((value, *pending))[0]
        pending.clear()
        return value

    d_ff2 = _mm(r, dh2_b, "tn", (F32,), "ff2_dw")
    step(None, "w_ff2", d_ff2)
    dn2 = _mm(da, w_ff1, "nt", (F32,), "ff1_dx", b_stacked=True)
    d_ff1 = _mm(n2, da, "tn", (F32,), "ff1_dw", out_stacked=True)
    step(d_ff1, "w_ff1", d_ff1)
    dh1, dh1_b, d_norm_ffn = _rms_bwd(after_step(dn2), h1, norm_ffn_w, dh2, "rms_ffn_bwd")
    dy = _mm(dh1_b, w_out, "nt", (F32,), "out_proj_dx")
    d_out = _mm(y, dh1_b, "tn", (F32,), "out_proj_dw")
    step(d_out, "w_out", d_out)

    do, dgate, d_gate_w = _gate_bwd(o_a, o_b, proj, gate_w, after_step(dy), h, z_blk, gb_blk)
    step(do)
    dq, dk, dv, dbeta_bc, dg_bc = _gdn_bwd(q, k, v, beta_bc, g_bc, st_a, inv_a, after_step(do), do_blk=0)
    step(dq)
    dxq, dxk, dxv, dcq, dck, dcv = _gdn_prep_bwd(proj, conv_w, after_step(dq), dk, dv, h)
    step(dxq)
    from_heads = lambda a: jnp.pad(a[:, :, 0, :].reshape(h, t).T, ((0, 0), (0, HEAD_DIM - h)))
    dab, d_a_log, d_dt_bias = _gdn_gates_bwd(ab, a_log_row, dt_row, from_heads(dg_bc), from_heads(dbeta_bc))
    dqh, dkh, dvh, dlf = _hgrn_bwd(after_step(qh), kh, proj, lf, st_b, do, v_blk=ib_blk, do_blk=h)
    step(dqh)
    dxqb, dxfb, d_lb = _hgrn_prep_bwd(proj, lb_logits, dqh, dkh, dlf, h, qb_blk, fb_blk)

    dproj = jnp.concatenate([after_step(dxq), dxk, dxv, dgate[:, :gw], dxqb, dxfb, dvh.astype(BF16), dgate[:, gw:]],
                            axis=1)
    d_main = _mm(n1, dproj, "tn", (F32,), "in_proj_dw", out_stacked=True)
    d_ab = _mm(n1, dab, "tn", (F32,), "in_proj_ab_dw")
    step(d_main, "w_in", d_main)
    dn1_ab = _mm(after_step(dab), w_ab, "nt", (F32,), "in_proj_ab_dx")
    step(dn1_ab)
    dn1 = _mm(after_step(dproj), w_main, "nt", (F32,), "in_proj_dx", epi=_add_epi, extras=(dn1_ab,))
    step(dn1)
    dx, _, d_norm_mix = _rms_bwd(after_step(dn1), x, norm_mix_w, dh1, "rms_mix_bwd")
    step(dx)

    grads = dict(
        w_main=d_main, w_ab=d_ab, conv_w=jnp.concatenate([dcq, dck, dcv], axis=1),
        gdn_a_log=d_a_log[:, :h], gdn_dt_bias=d_dt_bias[:, :h], gdn_norm_w=d_gate_w[0],
        hgrn_lb_logits=d_lb, hgrn_norm_w=d_gate_w[1], w_out=d_out, norm_mix_w=d_norm_mix,
        norm_ffn_w=d_norm_ffn, w_ff1=d_ff1, w_ff2=d_ff2, norm_final_w=d_norm_final)
    return loss, dx, grads


N_CHIPS = 4
ANY = pl.BlockSpec(memory_space=pl.ANY)


def _place():
    x, y, c = lax.axis_index("x"), lax.axis_index("y"), lax.axis_index("c")
    chips = [(1 - x, y), (x, 1 - y), (1 - x, 1 - y)]
    return x, y, c, chips


def _remote(src, dst, send_sems, recv_sems, k, to):
    return pltpu.make_async_remote_copy(src_ref=src, dst_ref=dst, send_sem=send_sems.at[k],
                                        recv_sem=recv_sems.at[k], device_id=to, device_id_type=MESH)


def _to_sibling(x, y, c, chips):
    return [(x, y, 1 - c)]


def _to_same_core_of_chips(x, y, c, chips):
    return [(*chip, c) for chip in chips]


def _to_all_gather_peers(x, y, c, chips):
    return _to_sibling(x, y, c, chips) + _to_same_core_of_chips(x, y, c, chips)


SIBLING_EXCHANGE = (1, _to_sibling)
CHIP_EXCHANGE = (2, _to_same_core_of_chips)
GATHER_EXCHANGE = (3, _to_all_gather_peers)


def _launch(body, name, out_shapes, arrays, sem_counts, sequencer=None, after=()):
    n, n_after = len(arrays), len(after)
    sems = [pltpu.SemaphoreType.DMA((k,)) for k in sem_counts]
    strip = lambda refs: refs[:n] + refs[n + n_after:]
    if sequencer is None:
        return pl.pallas_call(
            lambda *refs: body(*strip(refs)), name=name, out_shape=tuple(out_shapes),
            in_specs=[ANY] * (n + n_after), out_specs=tuple(ANY for _ in out_shapes), scratch_shapes=sems,
        )(*arrays, *after)
    collective_id, peers = sequencer

    def sequencer_body(*refs):
        x, y, c, chips = _place()
        barrier = pltpu.get_barrier_semaphore()
        targets = peers(x, y, c, chips)
        for target in targets:
            pl.semaphore_signal(barrier, inc=1, device_id=target, device_id_type=MESH)
        pl.semaphore_wait(barrier, len(targets))
        body(*strip(refs))

    return pl.kernel(
        sequencer_body, name=name, out_type=tuple(out_shapes),
        mesh=plsc.ScalarSubcoreMesh(axis_name="sequencer", num_cores=1), scratch_types=tuple(sems),
        compiler_params=pltpu.CompilerParams(collective_id=collective_id),
    )(*arrays, *after)


def _gather_weights(big, small, name, sequencer=None, after=()):
    nb, ns = len(big), len(small)
    n_sem = 6 * nb + 3 * ns

    def body(*refs):
        ins, outs = refs[:nb + ns], refs[nb + ns:2 * (nb + ns)]
        send_sems, recv_sems = refs[2 * (nb + ns):]
        x, y, c, chips = _place()
        me, sibling = 2 * x + y, (x, y, 1 - c)

        def half(a, chip, hc):
            rh = big[a].shape[0] // 2
            return outs[a].at[2 * chip[0] + chip[1], pl.ds(hc * rh, rh), :]

        first, passed = [], []
        for a in range(nb):
            rh = big[a].shape[0] // 2
            for j, chip in enumerate(chips):
                first.append(_remote(ins[a].at[pl.ds(c * rh, rh), :], half(a, (x, y), c),
                                     send_sems, recv_sems, 6 * a + j, (*chip, c)))
        for s in range(ns):
            for j, chip in enumerate(chips):
                first.append(_remote(ins[nb + s], outs[nb + s].at[me], send_sems, recv_sems,
                                     6 * nb + 3 * s + j, (*chip, c)))
        for cp in first:
            cp.start()
        for a in range(nb):
            for j, chip in enumerate(chips):
                _remote(half(a, chip, c), half(a, chip, c), send_sems, recv_sems, 6 * a + j, (*chip, c)).wait_recv()
                fwd = _remote(half(a, chip, c), half(a, chip, c), send_sems, recv_sems, 6 * a + 3 + j, sibling)
                fwd.start()
                passed.append(fwd)
        for s in range(ns):
            for j, chip in enumerate(chips):
                dst = outs[nb + s].at[2 * chip[0] + chip[1]]
                _remote(dst, dst, send_sems, recv_sems, 6 * nb + 3 * s + j, (*chip, c)).wait_recv()
        for a in range(nb):
            for j, chip in enumerate(chips):
                _remote(half(a, chip, 1 - c), half(a, chip, 1 - c), send_sems, recv_sems,
                        6 * a + 3 + j, sibling).wait_recv()
        for cp in first + passed:
            cp.wait_send()

    arrays = list(big) + list(small)
    out_shapes = [jax.ShapeDtypeStruct((N_CHIPS,) + a.shape, a.dtype) for a in arrays]
    return _launch(body, name, out_shapes, arrays, (n_sem, n_sem), sequencer, after)


def _swap_halves(parts, name, sequencer=None):
    n = len(parts)

    def body(*refs):
        ins, outs = refs[:n], refs[n:2 * n]
        send_sems, recv_sems = refs[2 * n:]
        x, y, c, _ = _place()
        copies = [_remote(ins[a].at[s, 1 - c], outs[a].at[s], send_sems, recv_sems, N_CHIPS * a + s, (x, y, 1 - c))
                  for a in range(n) for s in range(N_CHIPS)]
        for cp in copies:
            cp.start()
        for cp in copies:
            cp.wait()

    out_shapes = [jax.ShapeDtypeStruct((N_CHIPS,) + p.shape[2:], p.dtype) for p in parts]
    return _launch(body, name, out_shapes, parts, (N_CHIPS * n, N_CHIPS * n), sequencer)


def _scatter_to_owners(parts, name, sequencer=None):
    n = len(parts)

    def body(*refs):
        ins, outs = refs[:n], refs[n:2 * n]
        send_sems, recv_sems = refs[2 * n:]
        x, y, c, chips = _place()
        copies = [_remote(ins[a].at[2 * chip[0] + chip[1]], outs[a].at[j], send_sems, recv_sems,
                          3 * a + j, (*chip, c))
                  for a in range(n) for j, chip in enumerate(chips)]
        for cp in copies:
            cp.start()
        for cp in copies:
            cp.wait()

    out_shapes = [jax.ShapeDtypeStruct((3,) + p.shape[1:], p.dtype) for p in parts]
    return _launch(body, name, out_shapes, parts, (3 * n, 3 * n), sequencer)


def _send_to_sibling(halves, name, sequencer=None):
    n = len(halves)

    def body(*refs):
        ins, outs = refs[:n], refs[n:2 * n]
        send_sems, recv_sems = refs[2 * n:]
        x, y, c, _ = _place()
        copies = [_remote(ins[a], outs[a], send_sems, recv_sems, a, (x, y, 1 - c)) for a in range(n)]
        for cp in copies:
            cp.start()
        for cp in copies:
            cp.wait()

    out_shapes = [jax.ShapeDtypeStruct(p.shape, p.dtype) for p in halves]
    return _launch(body, name, out_shapes, halves, (n, n), sequencer)


N_DEV = 8


def _all_reduce_small(vec):
    def body(v_ref, gathered, total, send_sems, recv_sems):
        x, y, c, _ = _place()
        me = 4 * x + 2 * y + c
        gathered[me] = v_ref[...]
        copies = []
        for k in range(1, N_DEV):
            px = 1 - x if k & 4 else x
            py = 1 - y if k & 2 else y
            pc = 1 - c if k & 1 else c
            copies.append(_remote(v_ref, gathered.at[me], send_sems, recv_sems, k - 1, (px, py, pc)))
        for cp in copies:
            cp.start()
        for k, cp in enumerate(copies):
            cp.wait_send()
        for k in range(1, N_DEV):
            px = 1 - x if k & 4 else x
            py = 1 - y if k & 2 else y
            pc = 1 - c if k & 1 else c
            src = gathered.at[4 * px + 2 * py + pc]
            _remote(src, src, send_sems, recv_sems, k - 1, (px, py, pc)).wait_recv()
        acc = gathered[0]
        for dev in range(1, N_DEV):
            acc = acc + gathered[dev]
        total[...] = acc

    vm = pl.BlockSpec(memory_space=pltpu.VMEM)
    return pl.pallas_call(
        body, name="all_reduce_small",
        out_shape=(jax.ShapeDtypeStruct((N_DEV,) + vec.shape, F32), jax.ShapeDtypeStruct(vec.shape, F32)),
        in_specs=[vm], out_specs=(vm, vm),
        scratch_shapes=[pltpu.SemaphoreType.DMA((N_DEV - 1,)), pltpu.SemaphoreType.DMA((N_DEV - 1,))],
    )(vec)[1]


def _chip_sum(part, recv, c):
    _, _, rh, cols = part.shape
    tr = _tile(rh, 256)

    def body(c_ref, p_ref, r_ref, s_ref, sb_ref):
        s = p_ref[...] + r_ref[...]
        s_ref[...] = s
        sb_ref[...] = s.astype(BF16)

    blk = pl.BlockSpec((None, tr, cols), lambda s, i, c_ref: (s, i, 0))
    return pl.pallas_call(
        body, name="grad_chip_sum",
        out_shape=(jax.ShapeDtypeStruct(recv.shape, F32), jax.ShapeDtypeStruct(recv.shape, BF16)),
        grid_spec=pltpu.PrefetchScalarGridSpec(
            num_scalar_prefetch=1, grid=(N_CHIPS, rh // tr),
            in_specs=[pl.BlockSpec((None, None, tr, cols), lambda s, i, c_ref: (s, c_ref[0], i, 0)), blk],
            out_specs=(blk, blk)),
        compiler_params=_params(("parallel", "parallel")),
    )(c, part, recv)


def _owner_sum(own, recv, me):
    _, rh, cols = own.shape
    tr = _tile(rh, 256)

    def body(me_ref, o_ref, r0, r1, r2, g_ref):
        g_ref[...] = ((o_ref[...] + r0[...].astype(F32)) + r1[...].astype(F32)) + r2[...].astype(F32)

    slot = lambda j: pl.BlockSpec((None, tr, cols), lambda i, me_ref: (j, i, 0))
    return pl.pallas_call(
        body, name="grad_owner_sum", out_shape=jax.ShapeDtypeStruct((rh, cols), F32),
        grid_spec=pltpu.PrefetchScalarGridSpec(
            num_scalar_prefetch=1, grid=(rh // tr,),
            in_specs=[pl.BlockSpec((None, tr, cols), lambda i, me_ref: (me_ref[0], i, 0)), slot(0), slot(1), slot(2)],
            out_specs=pl.BlockSpec((tr, cols), lambda i, me_ref: (i, 0))),
        compiler_params=_params(("parallel",)),
    )(me, own, recv, recv, recv)


def _adamw_math(w, g, m, v):
    c1 = 1.0 / (1.0 - ADAM_B1 ** ADAM_STEP)
    c2 = 1.0 / (1.0 - ADAM_B2 ** ADAM_STEP)
    nm = ADAM_B1 * m + (1.0 - ADAM_B1) * g
    nv = ADAM_B2 * v + (1.0 - ADAM_B2) * (g * g)
    return -ADAM_LR * ((nm * c1) / (jnp.sqrt(nv * c2) + ADAM_EPS) + ADAM_WD * w), nm, nv


def _adamw_unit_rows(w, g, m, v, name):
    rows, _, cols = w.shape
    tr = max(d for d in range(1, 33) if rows % d == 0)

    def body(w_ref, g_ref, m_ref, v_ref, d_ref, nm_ref, nv_ref):
        d_ref[...], nm_ref[...], nv_ref[...] = _adamw_math(w_ref[...], g_ref[...], m_ref[...], v_ref[...])

    blk = pl.BlockSpec((tr, 1, cols), lambda i: (i, 0, 0))
    shape = jax.ShapeDtypeStruct(w.shape, F32)
    return pl.pallas_call(
        body, name=name, out_shape=(shape, shape, shape), grid=(rows // tr,),
        in_specs=[blk, blk, blk, blk], out_specs=(blk, blk, blk),
        compiler_params=_params(("parallel",)),
    )(w, g, m, v)


def _divisor_tile(n, want):
    return max(d for d in range(ROW_TILE, want + 1, ROW_TILE) if n % d == 0)


def _adamw(w, g, m, v, name):
    if w.ndim == 3:
        return _adamw_unit_rows(w, g, m, v, name)
    rows, cols = w.shape
    tr = _divisor_tile(rows, 2048) if rows % 8 == 0 else rows
    c1 = 1.0 / (1.0 - ADAM_B1 ** ADAM_STEP)
    c2 = 1.0 / (1.0 - ADAM_B2 ** ADAM_STEP)

    def body(w_ref, g_ref, m_ref, v_ref, d_ref, nm_ref, nv_ref):
        gv = g_ref[...]
        nm = ADAM_B1 * m_ref[...] + (1.0 - ADAM_B1) * gv
        nv = ADAM_B2 * v_ref[...] + (1.0 - ADAM_B2) * (gv * gv)
        d_ref[...] = -ADAM_LR * ((nm * c1) / (jnp.sqrt(nv * c2) + ADAM_EPS) + ADAM_WD * w_ref[...])
        nm_ref[...] = nm
        nv_ref[...] = nv

    blk = pl.BlockSpec((tr, cols), lambda i: (i, 0))
    shape = jax.ShapeDtypeStruct((rows, cols), F32)
    return pl.pallas_call(
        body, name=name, out_shape=(shape, shape, shape), grid=(rows // tr,),
        in_specs=[blk, blk, blk, blk], out_specs=(blk, blk, blk),
        compiler_params=_params(("parallel",)),
    )(w, g, m, v)


def _adamw_halves(w, g_own, g_sib, m, v, c, name):
    _, rows, cols = w.shape
    rh = rows // 2
    tr = _tile(rh, 256)
    per = rh // tr
    c1 = 1.0 / (1.0 - ADAM_B1 ** ADAM_STEP)
    c2 = 1.0 / (1.0 - ADAM_B2 ** ADAM_STEP)

    def body(c_ref, w_ref, go_ref, gs_ref, m_ref, v_ref, g_ref, d_ref, nm_ref, nv_ref):
        own = pl.program_id(0) // per == c_ref[0]
        gv = jnp.where(own, go_ref[...], gs_ref[...])
        nm = ADAM_B1 * m_ref[...] + (1.0 - ADAM_B1) * gv
        nv = ADAM_B2 * v_ref[...] + (1.0 - ADAM_B2) * (gv * gv)
        g_ref[...] = gv
        d_ref[...] = -ADAM_LR * ((nm * c1) / (jnp.sqrt(nv * c2) + ADAM_EPS) + ADAM_WD * w_ref[...])
        nm_ref[...] = nm
        nv_ref[...] = nv

    blk = pl.BlockSpec((None, tr, cols), lambda i, c_ref: (0, i, 0))
    half = pl.BlockSpec((tr, cols), lambda i, c_ref: (i % per, 0))
    shape = jax.ShapeDtypeStruct((1, rows, cols), F32)
    return pl.pallas_call(
        body, name=name, out_shape=(shape, shape, shape, shape),
        grid_spec=pltpu.PrefetchScalarGridSpec(
            num_scalar_prefetch=1, grid=(rows // tr,),
            in_specs=[blk, half, half, blk, blk], out_specs=(blk, blk, blk, blk)),
        compiler_params=_params(("parallel",)),
    )(c, w, g_own, g_sib, m, v)


def _by_shard(name, full):
    if name in ("w_in", "w_ff1"):
        st = full
    else:
        st = full.reshape(N_CHIPS, -1, full.shape[1])
    return st.reshape(N_CHIPS, 2, st.shape[1] // 2, st.shape[2])


class _GradReducer:
    def __init__(self, w, m, v, my_c, my_chip):
        self.w, self.m, self.v, self.my_c, self.my_chip = w, m, v, my_c, my_chip
        self.in_flight = []
        self.computed = []
        self.anchor = None
        self.done = {}

    def step(self, name=None, full=None, anchor=None):
        stages, self.in_flight, self.computed, self.anchor = self.in_flight, [], [], anchor
        for stage in [s for s in stages if not getattr(s, "long", False)]:
            self._advance(stage)
        if name is not None:
            self.in_flight.append(self._swap(name, _by_shard(name, full)))
        for stage in [s for s in stages if getattr(s, "long", False)]:
            self._advance(stage)
        return self.computed

    def _held(self, value):
        if self.anchor is None:
            return value
        return lax.optimization_barrier((value, self.anchor))[0]

    def _advance(self, stage):
        nxt = stage()
        if nxt is not None:
            self.in_flight.append(nxt)

    def finish(self):
        while self.in_flight:
            self.step()
        return self.done

    def _swap(self, name, part):
        got, = _swap_halves([part], "grad_swap_" + name, SIBLING_EXCHANGE)

        def scatter():
            total, total_bf16 = _chip_sum(part, self._held(got), self.my_c)
            self.computed.append(total_bf16)
            recv, = _scatter_to_owners([total_bf16], "grad_scatter_" + name, CHIP_EXCHANGE)

            def send():
                half = _owner_sum(total, self._held(recv), self.my_chip.reshape(1))
                self.computed.append(half)
                sib, = _send_to_sibling([half], "grad_send_" + name, SIBLING_EXCHANGE)

                def update():
                    self.done[name] = _adamw_halves(self.w[name], half, self._held(sib), self.m[name], self.v[name],
                                                    self.my_c, "adamw_" + name)
                    self.computed.append(self.done[name][0])

                if name == "w_in":
                    self.in_proj_halves = (half, sib)
                    return None
                return update
            return lambda: send
        scatter.long = True
        return scatter


def _adamw_minor_rows(w, g, m, v, name):
    _, rows, cols = w.shape
    turned = lambda a: jnp.transpose(a, (2, 0, 1))
    back = lambda a: jnp.transpose(a, (1, 2, 0))
    g = g.T.reshape(cols, 1, rows)
    delta, new_m, new_v = _adamw(turned(w), g, turned(m), turned(v), name)
    return back(g), back(delta), back(new_m), back(new_v)


def _in_proj_edge_columns(d_main, d_ab, h):
    e = h // 2
    return jnp.concatenate([d_main[1][:, :e], d_main[2][:, -e:], d_ab[:, :h], d_ab[:, HEAD_DIM:HEAD_DIM + h]], axis=1)


def _in_proj_shard_grad(q_own, q_sib, edges, c, chip, h):
    e = h // 2
    lower, upper = jnp.where(c[0] == 0, q_own, q_sib), jnp.where(c[0] == 0, q_sib, q_own)
    quarter = jnp.concatenate([lower, upper], axis=0)
    x1, x2, a, b = edges[:, :e], edges[:, e:2 * e], edges[:, 2 * e:2 * e + h], edges[:, 2 * e + h:]
    zeros = jnp.zeros_like(x1)
    left = jnp.where(chip == 2, b, jnp.concatenate([zeros, x2], axis=1))
    right = jnp.where(chip == 1, a, jnp.concatenate([x1, zeros], axis=1))
    start = jnp.where(chip == 0, h, jnp.where(chip == 1, h + e, jnp.where(chip == 2, 0, h - e)))
    padded = jnp.concatenate([left, quarter, right], axis=1)
    return lax.dynamic_slice_in_dim(padded, start, quarter.shape[1] + e, axis=1)


SMALL = ("gdn_a_log", "gdn_dt_bias", "gdn_norm_w", "hgrn_lb_logits", "hgrn_norm_w",
         "norm_mix_w", "norm_ffn_w", "norm_final_w")
BIG = ("w_in", "w_out", "w_ff1", "w_ff2")
ORDER = ("w_in", "conv_w", "gdn_a_log", "gdn_dt_bias", "gdn_norm_w", "hgrn_lb_logits", "hgrn_norm_w",
         "w_out", "norm_mix_w", "norm_ffn_w", "w_ff1", "w_ff2", "norm_final_w")


def _pack(pieces):
    flat = jnp.concatenate([p.reshape(-1).astype(F32) for p in pieces])
    rows = -(-flat.shape[0] // (8 * HEAD_DIM)) * 8
    return jnp.pad(flat, (0, rows * HEAD_DIM - flat.shape[0])).reshape(rows, HEAD_DIM)


def _unpack(packed, shapes):
    flat, out, at = packed.reshape(-1), [], 0
    for s in shapes:
        n = 1
        for dim in s:
            n *= dim
        out.append(flat[at:at + n].reshape(s))
        at += n
    return out


def kernel(x, w_in, conv_w, gdn_a_log, gdn_dt_bias, gdn_norm_w, hgrn_lb_logits, hgrn_norm_w, w_out, norm_mix_w, norm_ffn_w, w_ff1, w_ff2, norm_final_w, loss_target, m_w_in, m_conv_w, m_gdn_a_log, m_gdn_dt_bias, m_gdn_norm_w, m_hgrn_lb_logits, m_hgrn_norm_w, m_w_out, m_norm_mix_w, m_norm_ffn_w, m_w_ff1, m_w_ff2, m_norm_final_w, v_w_in, v_conv_w, v_gdn_a_log, v_gdn_dt_bias, v_gdn_norm_w, v_hgrn_lb_logits, v_hgrn_norm_w, v_w_out, v_norm_mix_w, v_norm_ffn_w, v_w_ff1, v_w_ff2, v_norm_final_w):
    w = dict(w_in=w_in, conv_w=conv_w, gdn_a_log=gdn_a_log, gdn_dt_bias=gdn_dt_bias, gdn_norm_w=gdn_norm_w,
             hgrn_lb_logits=hgrn_lb_logits, hgrn_norm_w=hgrn_norm_w, w_out=w_out, norm_mix_w=norm_mix_w,
             norm_ffn_w=norm_ffn_w, w_ff1=w_ff1, w_ff2=w_ff2, norm_final_w=norm_final_w)
    m = dict(w_in=m_w_in, conv_w=m_conv_w, gdn_a_log=m_gdn_a_log, gdn_dt_bias=m_gdn_dt_bias,
             gdn_norm_w=m_gdn_norm_w, hgrn_lb_logits=m_hgrn_lb_logits, hgrn_norm_w=m_hgrn_norm_w,
             w_out=m_w_out, norm_mix_w=m_norm_mix_w, norm_ffn_w=m_norm_ffn_w, w_ff1=m_w_ff1, w_ff2=m_w_ff2,
             norm_final_w=m_norm_final_w)
    v = dict(w_in=v_w_in, conv_w=v_conv_w, gdn_a_log=v_gdn_a_log, gdn_dt_bias=v_gdn_dt_bias,
             gdn_norm_w=v_gdn_norm_w, hgrn_lb_logits=v_hgrn_lb_logits, hgrn_norm_w=v_hgrn_norm_w,
             w_out=v_w_out, norm_mix_w=v_norm_mix_w, norm_ffn_w=v_norm_ffn_w, w_ff1=v_w_ff1, w_ff2=v_w_ff2,
             norm_final_w=v_norm_final_w)
    d = x.shape[-1]
    h = d // (2 * HEAD_DIM)
    my_c = lax.axis_index("c").astype(jnp.int32).reshape(1)
    my_chip = (2 * lax.axis_index("x") + lax.axis_index("y")).astype(jnp.int32)

    shards = [w[n][0].astype(BF16) for n in BIG]
    conv_shard = jnp.pad(conv_w[0], ((0, 8 - CONV_W), (0, 0)))
    first = _gather_weights(shards[:1], [conv_shard], "gather_in_proj", GATHER_EXCHANGE)
    n1 = _rms_fwd(x[0], norm_mix_w[0], "rms_mix")
    gathered_in, n1, *shards[1:] = lax.optimization_barrier((first[0], n1, *shards[1:]))
    own_slot = lambda st, own: lax.dynamic_update_index_in_dim(st, own, my_chip, 0)
    f_in, f_conv = own_slot(gathered_in, shards[0]), own_slot(first[1], conv_shard)
    cols = lambda st: st.transpose(1, 0, 2).reshape(st.shape[1], -1)
    w_main, w_ab = _split_w_in(cols(f_in), h)
    conv_full = cols(f_conv[:, :CONV_W])
    rest = _gather_weights(shards[1:], [], "gather_rest", GATHER_EXCHANGE, after=[w_ab])
    f_out, f_ff1, f_ff2 = (own_slot(st, own) for st, own in zip(rest, shards[1:]))

    reducer = _GradReducer(w, m, v, my_c, my_chip)
    loss, dx, g = _local_step(
        x[0], loss_target[0], w_main, w_ab, conv_full, gdn_a_log[0], gdn_dt_bias[0], gdn_norm_w[0],
        hgrn_lb_logits, hgrn_norm_w[0], f_out.reshape(-1, d), norm_mix_w[0], norm_ffn_w[0],
        f_ff1, f_ff2.reshape(-1, d), norm_final_w, reducer, n1)

    grads, delta, new_m, new_v = {}, {}, {}, {}
    for n, out in reducer.finish().items():
        grads[n], delta[n], new_m[n], new_v[n] = out

    edges = _in_proj_edge_columns(g["w_main"], g["w_ab"], h)
    small_shapes = [w[n].shape for n in SMALL] + [conv_full.shape, (1,), edges.shape]
    total = _all_reduce_small(_pack([g[n] for n in SMALL] + [g["conv_w"], loss[0, :1], edges]))
    *small_grads, conv_grad, loss_sum, edges = _unpack(total, small_shapes)
    g_in = _in_proj_shard_grad(*reducer.in_proj_halves, edges, my_c, my_chip, h)
    grads["w_in"], delta["w_in"], new_m["w_in"], new_v["w_in"] = _adamw_minor_rows(
        w["w_in"], g_in, m["w_in"], v["w_in"], "adamw_w_in")
    for n, sg in zip(SMALL, small_grads):
        grads[n] = sg
    shard_cols = conv_w.shape[-1]
    grads["conv_w"] = lax.dynamic_slice_in_dim(conv_grad, my_chip * shard_cols, shard_cols, axis=1)[None]

    packed_names = SMALL + ("conv_w",)
    packed = [_pack([t[n] for n in packed_names]) for t in (w, grads, m, v)]
    outs = _adamw(*packed, "adamw_small")
    shapes = [w[n].shape for n in packed_names]
    for res, o in zip((delta, new_m, new_v), outs):
        for n, a in zip(packed_names, _unpack(o, shapes)):
            res[n] = a

    return (loss_sum.reshape(()), dx[None], *[grads[n] for n in ORDER], *[delta[n] for n in ORDER],
            *[new_m[n] for n in ORDER], *[new_v[n] for n in ORDER])
```

```python
import functools

import jax
import jax.numpy as jnp
from jax import lax
from jax.experimental import pallas as pl
from jax.experimental.pallas import tpu as pltpu
from jax.experimental.pallas import tpu_sc as plsc

F32 = jnp.float32
BF16 = jnp.bfloat16

HEAD_DIM = 128
CHUNK = 128
SUB = 16
EXP_CAP = 80.0
NORM_EPS = 1e-6
L2_EPS = 1e-6
CONV_W = 4
VMEM_LIMIT = 56 * 1024 * 1024

ADAM_LR, ADAM_B1, ADAM_B2, ADAM_EPS, ADAM_WD, ADAM_STEP = 1e-3, 0.9, 0.999, 1e-8, 0.01, 10

NN = ((1,), (0,))
NT = ((1,), (1,))
TN = ((0,), (0,))
MESH = pl.DeviceIdType.MESH


def _dot(a, b, dims):
    return lax.dot_general(a.astype(BF16), b.astype(BF16), (dims, ((), ())),
                           preferred_element_type=F32)


def _split(a):
    hi = a.astype(BF16)
    return hi, (a - hi.astype(F32)).astype(BF16)


def _dot3(a, b, dims):
    ah, al = _split(a)
    bh, bl = _split(b)
    d = lambda x, y: lax.dot_general(x, y, (dims, ((), ())), preferred_element_type=F32)
    return d(ah, bh) + (d(ah, bl) + d(al, bh))


def _sigmoid(x):
    return 1.0 / (1.0 + jnp.exp(-x))


def _silu(x):
    return x * _sigmoid(x)


def _dsilu(x):
    s = _sigmoid(x)
    return s * (1.0 + x * (1.0 - s))


def _softplus(x):
    e = jnp.exp(-jnp.abs(x))
    u = 1.0 + e
    log1p = jnp.where(u == 1.0, e, jnp.log(u) * (e / jnp.where(u == 1.0, 1.0, u - 1.0)))
    return jnp.maximum(x, 0.0) + log1p


def _iota(shape, axis):
    return lax.broadcasted_iota(jnp.int32, shape, axis)


def _cumsum_rows(x):
    n = x.shape[0]
    row = _iota(x.shape, 0)
    s = 1
    while s < n:
        x = x + jnp.where(row >= s, pltpu.roll(x, s, 0), 0.0)
        s *= 2
    return x


def _rev_cumsum_rows(x):
    return jnp.sum(x, axis=0, keepdims=True) - _cumsum_rows(x) + x


def _params(sem):
    return pltpu.CompilerParams(dimension_semantics=sem, vmem_limit_bytes=VMEM_LIMIT)


ROW_TILE = 8
HEADS_PER_STEP = 8


def _hps(h):
    return min(HEADS_PER_STEP, h)


def _head_view(ref, hb):
    if len(ref.shape) == 2:
        return ref.at[:, pl.ds(hb * HEAD_DIM, HEAD_DIM)]
    return ref.at[hb]


class _Staged:
    def __init__(self, ref, load):
        self.ref = ref
        self.loaded = ref[...] if load else None
        self.written = None

    def __getitem__(self, idx):
        return self.loaded

    def __setitem__(self, idx, value):
        self.written = value


def _each_head(one_head, n_in):
    def body(*refs):
        @pl.when(pl.program_id(1) == 0)
        def _():
            refs[-1][...] = jnp.zeros_like(refs[-1])

        last = len(refs) - 1
        staged = [[_Staged(_head_view(r, hb), i < n_in or i == last) for i, r in enumerate(refs)]
                  for hb in range(refs[-1].shape[0])]
        running = [one_head(*per_head) for per_head in staged]
        while running:
            for gen in list(running):
                try:
                    next(gen)
                except StopIteration:
                    running.remove(gen)
        for per_head in staged:
            for s in per_head:
                if s.written is not None:
                    s.ref[...] = s.written
    return body


def _tile(n, want):
    t = min(n, want)
    while n % t:
        t //= 2
    return t


def _mm(a, b, mode, out_dtypes, name, epi=None, extras=(), tm=1024, tn=1024, tk=2048,
        b_stacked=False, out_stacked=False):
    if mode == "tn":
        kdim, m = a.shape
    else:
        m, kdim = a.shape
    if b_stacked:
        n = N_CHIPS * b.shape[2] if mode == "nn" else b.shape[1]
        kdim_b = b.shape[1] if mode == "nn" else N_CHIPS * b.shape[2]
        assert kdim_b == kdim
    else:
        n = b.shape[0] if mode == "nt" else b.shape[1]
    per_shard = (n if (mode == "nn" or out_stacked) else kdim) // N_CHIPS
    tm, tn, tk = _tile(m, tm), _tile(n, tn), _tile(kdim, tk)
    if (b_stacked and mode == "nn") or out_stacked:
        tn = _tile(per_shard, tn)
    if b_stacked and mode == "nt":
        tk = _tile(per_shard, tk)
    nk = kdim // tk
    dims = {"nn": NN, "nt": NT, "tn": TN}[mode]
    a_spec = (pl.BlockSpec((tk, tm), lambda i, j, k: (k, i)) if mode == "tn"
              else pl.BlockSpec((tm, tk), lambda i, j, k: (i, k)))
    if b_stacked and mode == "nn":
        per = per_shard // tn
        b_spec = pl.BlockSpec((None, tk, tn), lambda i, j, k: (j // per, k, j % per))
    elif b_stacked:
        per = per_shard // tk
        b_spec = pl.BlockSpec((None, tn, tk), lambda i, j, k: (k // per, j, k % per))
    else:
        b_spec = (pl.BlockSpec((tn, tk), lambda i, j, k: (j, k)) if mode == "nt"
                  else pl.BlockSpec((tk, tn), lambda i, j, k: (k, j)))
    mn_spec = pl.BlockSpec((tm, tn), lambda i, j, k: (i, j))
    if out_stacked:
        per_o = per_shard // tn
        out_spec = pl.BlockSpec((None, tm, tn), lambda i, j, k: (j // per_o, i, j % per_o))
        out_shape = (N_CHIPS, m, per_shard)
    else:
        out_spec, out_shape = mn_spec, (m, n)
    ne, no = len(extras), len(out_dtypes)
    if epi is None:
        epi = lambda acc: (acc,)

    def body(a_ref, b_ref, *rest):
        extra_refs, out_refs = rest[:ne], rest[ne:ne + no]
        part = _dot(a_ref[...], b_ref[...], dims)

        def finish(total):
            outs = epi(total, *[r[...] for r in extra_refs])
            for o_ref, o in zip(out_refs, outs):
                o_ref[...] = o.astype(o_ref.dtype)

        if nk == 1:
            finish(part)
            return
        acc = rest[-1]
        k = pl.program_id(2)

        @pl.when(k == 0)
        def _():
            acc[...] = part

        @pl.when(jnp.logical_and(k > 0, k < nk - 1))
        def _():
            acc[...] += part

        @pl.when(k == nk - 1)
        def _():
            finish(acc[...] + part)

    outs = pl.pallas_call(
        body, name=name,
        out_shape=tuple(jax.ShapeDtypeStruct(out_shape, d) for d in out_dtypes),
        grid=(m // tm, n // tn, nk),
        in_specs=[a_spec, b_spec] + [mn_spec] * ne,
        out_specs=tuple(out_spec for _ in out_dtypes),
        scratch_shapes=[pltpu.VMEM((tm, tn), F32)] if nk > 1 else [],
        compiler_params=_params(("parallel", "parallel", "arbitrary")),
    )(a, b, *extras)
    return outs if no > 1 else outs[0]


ROWS = 256


def _rms_fwd(x, w, name):
    t, d = x.shape
    tr = _tile(t, ROWS)

    def body(x_ref, w_ref, n_ref):
        xv = x_ref[...]
        r = lax.rsqrt(jnp.mean(xv * xv, axis=-1, keepdims=True) + NORM_EPS)
        n_ref[...] = (xv * r * w_ref[...]).astype(n_ref.dtype)

    return pl.pallas_call(
        body, name=name, out_shape=jax.ShapeDtypeStruct((t, d), BF16), grid=(t // tr,),
        in_specs=[pl.BlockSpec((tr, d), lambda i: (i, 0)), pl.BlockSpec((1, d), lambda i: (0, 0))],
        out_specs=pl.BlockSpec((tr, d), lambda i: (i, 0)),
        compiler_params=_params(("parallel",)),
    )(x, w.reshape(1, d))


def _rms_bwd(dn, x, w, dres, name):
    t, d = x.shape
    tr = _tile(t, ROWS)

    def body(dn_ref, x_ref, w_ref, dres_ref, dx_ref, dxb_ref, dw_ref):
        i = pl.program_id(0)
        xv, dnv = x_ref[...], dn_ref[...]
        r = lax.rsqrt(jnp.mean(xv * xv, axis=-1, keepdims=True) + NORM_EPS)
        xh = xv * r
        dxh = dnv * w_ref[...]
        dx = dres_ref[...] + r * (dxh - xh * jnp.mean(dxh * xh, axis=-1, keepdims=True))
        dx_ref[...] = dx
        dxb_ref[...] = dx.astype(BF16)

        @pl.when(i == 0)
        def _():
            dw_ref[...] = jnp.zeros_like(dw_ref)

        dw_ref[...] += jnp.sum(dnv * xh, axis=0, keepdims=True)

    row = pl.BlockSpec((tr, d), lambda i: (i, 0))
    vec = pl.BlockSpec((1, d), lambda i: (0, 0))
    return pl.pallas_call(
        body, name=name,
        out_shape=(jax.ShapeDtypeStruct((t, d), F32), jax.ShapeDtypeStruct((t, d), BF16),
                   jax.ShapeDtypeStruct((1, d), F32)),
        grid=(t // tr,), in_specs=[row, row, vec, row], out_specs=(row, row, vec),
        compiler_params=_params(("arbitrary",)),
    )(dn, x, w.reshape(1, d), dres)


def _loss_head(h, w, target):
    t, d = h.shape
    tr = _tile(t, ROWS)

    def body(h_ref, w_ref, t_ref, loss_ref, dh_ref, dhb_ref, dw_ref):
        i = pl.program_id(0)
        hv, wv = h_ref[...], w_ref[...]
        r = lax.rsqrt(jnp.mean(hv * hv, axis=-1, keepdims=True) + NORM_EPS)
        hh = hv * r
        err = hh * wv - t_ref[...]
        dout = err * (1.0 / d)
        dhh = dout * wv
        dh = r * (dhh - hh * jnp.mean(dhh * hh, axis=-1, keepdims=True))
        dh_ref[...] = dh
        dhb_ref[...] = dh.astype(BF16)

        @pl.when(i == 0)
        def _():
            dw_ref[...] = jnp.zeros_like(dw_ref)
            loss_ref[...] = jnp.zeros_like(loss_ref)

        dw_ref[...] += jnp.sum(dout * hh, axis=0, keepdims=True)
        loss_ref[...] += jnp.full((1, 128), 0.5 / d, F32) * jnp.sum(err * err)

    row = pl.BlockSpec((tr, d), lambda i: (i, 0))
    vec = pl.BlockSpec((1, d), lambda i: (0, 0))
    lspec = pl.BlockSpec((1, 128), lambda i: (0, 0))
    return pl.pallas_call(
        body, name="loss_head",
        out_shape=(jax.ShapeDtypeStruct((1, 128), F32), jax.ShapeDtypeStruct((t, d), F32),
                   jax.ShapeDtypeStruct((t, d), BF16), jax.ShapeDtypeStruct((1, d), F32)),
        grid=(t // tr,), in_specs=[row, vec, row], out_specs=(lspec, row, row, vec),
        compiler_params=_params(("arbitrary",)),
    )(h, w.reshape(1, d), target)


def _inv_unit_lower(a):
    c = a.shape[0]
    eye = (_iota((c, c), 0) == _iota((c, c), 1)).astype(F32)
    x = eye - a
    p = _dot3(a, a, NN)
    yield
    n = 2
    while n < c:
        x = x + _dot3(x, p, NN)
        n *= 2
        if n < c:
            p = _dot3(p, p, NN)
        yield
    return x


def _gdn_chunk(q, k, v, beta, g):
    c = q.shape[0]
    row, col = _iota((c, c), 0), _iota((c, c), 1)
    gc = _cumsum_rows(g)
    diff = gc - gc.T
    dec = jnp.where(row >= col, jnp.exp(jnp.minimum(diff, 0.0)), 0.0)
    dec_s = jnp.where(row > col, dec, 0.0)
    gam = jnp.exp(gc)
    g_last = jnp.sum(g, axis=0, keepdims=True)
    kk = _dot(k, k, NT)
    a = beta * kk * dec_s
    p = _dot(q, k, NT) * dec
    e_end = jnp.exp(g_last - gc)
    return dict(dec=dec, dec_s=dec_s, gam=gam, gam_last=jnp.exp(g_last), e_end=e_end,
                k_end=k * e_end, kk=kk, a=a, p=p)


def _gdn_fwd(q, k, v, beta_bc, g_bc):
    t = q.shape[0]
    h = q.shape[1] // HEAD_DIM
    nc = t // CHUNK

    def body(q_ref, k_ref, v_ref, b_ref, g_ref, o_ref, s_ref, t_ref, state):
        qv, kv, vv, beta = q_ref[...], k_ref[...], v_ref[...], b_ref[...]
        ch = _gdn_chunk(qv, kv, vv, beta, g_ref[...])
        yield
        tm = yield from _inv_unit_lower(ch["a"])
        sol = _dot(tm, jnp.concatenate([beta * vv, beta * ch["gam"] * kv], axis=1), NN)
        yield
        u_v, w = sol[:, :HEAD_DIM], sol[:, HEAD_DIM:]
        s0 = state[...]
        u = u_v - _dot(w, s0, NN)
        yield
        o_ref[...] = _dot(qv * ch["gam"], s0, NN) + _dot(ch["p"], u, NN)
        s_ref[...] = s0
        t_ref[...] = tm
        state[...] = ch["gam_last"] * s0 + _dot(ch["k_end"], u, TN)

    tok = pl.BlockSpec((CHUNK, _hps(h) * HEAD_DIM), lambda hh, c: (c, hh))
    bc = pl.BlockSpec((_hps(h), CHUNK, HEAD_DIM), lambda hh, c: (hh, c, 0))
    mat = pl.BlockSpec((_hps(h), None, HEAD_DIM, HEAD_DIM), lambda hh, c: (hh, c, 0, 0))
    return pl.pallas_call(
        _each_head(body, 5), name="gdn_fwd",
        out_shape=(jax.ShapeDtypeStruct(q.shape, F32),
                   jax.ShapeDtypeStruct((h, nc, HEAD_DIM, HEAD_DIM), F32),
                   jax.ShapeDtypeStruct((h, nc, CHUNK, CHUNK), F32)),
        grid=(h // _hps(h), nc), in_specs=[tok, tok, tok, bc, bc], out_specs=(tok, mat, mat),
        scratch_shapes=[pltpu.VMEM((_hps(h), HEAD_DIM, HEAD_DIM), F32)],
        compiler_params=_params(("parallel", "arbitrary")),
    )(q, k, v, beta_bc, g_bc)


def _gdn_bwd(q, k, v, beta_bc, g_bc, states, invs, do, do_blk=0):
    t = q.shape[0]
    h = q.shape[1] // HEAD_DIM
    nc = t // CHUNK

    def body(q_ref, k_ref, v_ref, b_ref, g_ref, s_ref, t_ref, do_ref,
             dq_ref, dk_ref, dv_ref, db_ref, dg_ref, dstate):
        qv, kv, vv, beta = q_ref[...], k_ref[...], v_ref[...], b_ref[...]
        dov, s0, tm, ds1 = do_ref[...], s_ref[...], t_ref[...], dstate[...]
        ch = _gdn_chunk(qv, kv, vv, beta, g_ref[...])
        yield
        gam, dec, dec_s, kk = ch["gam"], ch["dec"], ch["dec_s"], ch["kk"]
        r_v, r_w = beta * vv, beta * gam * kv
        sol = _dot(tm, jnp.concatenate([r_v, r_w], axis=1), NN)
        yield
        u_v, w = sol[:, :HEAD_DIM], sol[:, HEAD_DIM:]
        u = u_v - _dot(w, s0, NN)
        qg = qv * gam
        yield

        du = _dot(ch["p"], dov, TN) + _dot(ch["k_end"], ds1, NN)
        dp = _dot(dov, u, NT)
        dpd = dp * dec
        dqg = _dot(dov, s0, NT)
        dk_end = _dot(u, ds1, NT)
        yield
        dq = dqg * gam + _dot(dpd, kv, NN)
        dk = _dot(dpd, qv, TN) + dk_end * ch["e_end"]
        dstate[...] = _dot(qg, dov, TN) + ch["gam_last"] * ds1 - _dot(w, du, TN)
        dw = -_dot(du, s0, NT)
        yield
        dr = _dot(tm, jnp.concatenate([du, dw], axis=1), TN)
        yield
        dr_v, dr_w = dr[:, :HEAD_DIM], dr[:, HEAD_DIM:]
        da = -_dot(dr, sol, NT)
        yield
        dkk = da * beta * dec_s
        dk = dk + _dot(dkk, kv, NN) + _dot(dkk, kv, TN) + beta * gam * dr_w
        dbeta = (jnp.sum(da * kk * dec_s, axis=1, keepdims=True)
                 + jnp.sum(dr_v * vv + dr_w * gam * kv, axis=1, keepdims=True))

        pair = dp * ch["p"] + da * ch["a"]
        end = jnp.sum(dk_end * ch["k_end"], axis=1, keepdims=True)
        dgc = (jnp.sum(pair - pair.T, axis=1, keepdims=True)
               + jnp.sum(dqg * qg + dr_w * r_w, axis=1, keepdims=True) - end)
        at_end = jnp.sum(end) + ch["gam_last"] * jnp.sum(s0 * ds1)
        dgc = jnp.broadcast_to(dgc, (CHUNK, HEAD_DIM))
        dgc = dgc + jnp.where(_iota((CHUNK, HEAD_DIM), 0) == CHUNK - 1, at_end, 0.0)
        dq_ref[...] = dq
        dk_ref[...] = dk
        dv_ref[...] = beta * dr_v
        db_ref[...] = jnp.broadcast_to(dbeta, (CHUNK, HEAD_DIM)).T[:ROW_TILE]
        dg_ref[...] = _rev_cumsum_rows(dgc).T[:ROW_TILE]

    rev = lambda c: nc - 1 - c
    tok = pl.BlockSpec((CHUNK, _hps(h) * HEAD_DIM), lambda hh, c: (rev(c), hh))
    bc = pl.BlockSpec((_hps(h), CHUNK, HEAD_DIM), lambda hh, c: (hh, rev(c), 0))
    mat = pl.BlockSpec((_hps(h), None, HEAD_DIM, HEAD_DIM), lambda hh, c: (hh, rev(c), 0, 0))
    tok_shape = jax.ShapeDtypeStruct(q.shape, F32)
    row_shape = jax.ShapeDtypeStruct((h, nc, ROW_TILE, CHUNK), F32)
    rows = pl.BlockSpec((_hps(h), None, ROW_TILE, CHUNK), lambda hh, c: (hh, rev(c), 0, 0))
    return pl.pallas_call(
        _each_head(body, 8), name="gdn_bwd",
        out_shape=(tok_shape, tok_shape, tok_shape, row_shape, row_shape),
        grid=(h // _hps(h), nc),
        in_specs=[tok, tok, tok, bc, bc, mat, mat,
                  pl.BlockSpec((CHUNK, _hps(h) * HEAD_DIM), lambda hh, c: (rev(c), do_blk // _hps(h) + hh))],
        out_specs=(tok, tok, tok, rows, rows),
        scratch_shapes=[pltpu.VMEM((_hps(h), HEAD_DIM, HEAD_DIM), F32)],
        compiler_params=_params(("parallel", "arbitrary")),
    )(q, k, v, beta_bc, g_bc, states, invs, do)


def _hgrn_chunk(q, k, lf):
    c = q.shape[0]
    row = _iota((c, HEAD_DIM), 0)
    b = _cumsum_rows(lf)
    q_subs, k_facs, a_rows = [], [], []
    for x in range(c // SUB):
        b_start = jnp.sum(jnp.where(row < x * SUB, lf, 0.0), axis=0, keepdims=True)
        q_x = (q * jnp.exp(jnp.minimum(b - b_start, 0.0)))[x * SUB:(x + 1) * SUB]
        k_fac = jnp.where(row < (x + 1) * SUB, jnp.exp(jnp.minimum(b_start - b, EXP_CAP)), 0.0)
        q_subs.append(q_x)
        k_facs.append(k_fac)
        a_rows.append(_dot(q_x, k * k_fac, NT))
    a = jnp.concatenate(a_rows, axis=0)
    a = jnp.where(_iota((c, c), 0) >= _iota((c, c), 1), a, 0.0)
    b_last = jnp.sum(lf, axis=0, keepdims=True)
    return dict(b=b, a=a, q_subs=q_subs, k_facs=k_facs, e_b=jnp.exp(b),
                e_end=jnp.exp(b_last - b), e_last=jnp.exp(b_last))


def _hgrn_fwd(q, k, v, lf, v_blk=0):
    t = q.shape[0]
    h = q.shape[1] // HEAD_DIM
    nc = t // CHUNK

    def body(q_ref, k_ref, v_ref, lf_ref, o_ref, s_ref, state):
        qv, kv, vv = q_ref[...], k_ref[...], v_ref[...]
        ch = _hgrn_chunk(qv, kv, lf_ref[...])
        yield
        s0 = state[...]
        o_ref[...] = _dot(qv * ch["e_b"], s0, NT) + _dot(ch["a"], vv, NN)
        s_ref[...] = s0
        state[...] = s0 * ch["e_last"] + _dot(vv, kv * ch["e_end"], TN)

    tok = pl.BlockSpec((CHUNK, _hps(h) * HEAD_DIM), lambda hh, c: (c, hh))
    mat = pl.BlockSpec((_hps(h), None, HEAD_DIM, HEAD_DIM), lambda hh, c: (hh, c, 0, 0))
    return pl.pallas_call(
        _each_head(body, 4), name="hgrn_fwd",
        out_shape=(jax.ShapeDtypeStruct(q.shape, F32),
                   jax.ShapeDtypeStruct((h, nc, HEAD_DIM, HEAD_DIM), F32)),
        grid=(h // _hps(h), nc),
        in_specs=[tok, tok, pl.BlockSpec((CHUNK, _hps(h) * HEAD_DIM), lambda hh, c: (c, v_blk // _hps(h) + hh)), tok],
        out_specs=(tok, mat),
        scratch_shapes=[pltpu.VMEM((_hps(h), HEAD_DIM, HEAD_DIM), F32)],
        compiler_params=_params(("parallel", "arbitrary")),
    )(q, k, v, lf)


def _hgrn_bwd(q, k, v, lf, states, do, v_blk=0, do_blk=0):
    t = q.shape[0]
    nc = t // CHUNK
    h = q.shape[1] // HEAD_DIM

    def body(q_ref, k_ref, v_ref, lf_ref, s_ref, do_ref, dq_ref, dk_ref, dv_ref, dlf_ref, dstate):
        qv, kv, vv, dov, s0 = q_ref[...], k_ref[...], v_ref[...], do_ref[...], s_ref[...]
        ds1 = dstate[...]
        ch = _hgrn_chunk(qv, kv, lf_ref[...])
        yield
        c = CHUNK
        row = _iota((c, HEAD_DIM), 0)
        qh = qv * ch["e_b"]
        k_end = kv * ch["e_end"]
        da = jnp.where(_iota((c, c), 0) >= _iota((c, c), 1), _dot(dov, vv, NT), 0.0)
        dqh = _dot(dov, s0, NN)
        dk_end = _dot(vv, ds1, NN)
        yield
        end = dk_end * k_end
        dk = dk_end * ch["e_end"]
        db = dqh * qh - end + jnp.where(
            row == c - 1, jnp.sum(end + s0 * ch["e_last"] * ds1, axis=0, keepdims=True), 0.0)
        dq_rows, qdq_rows = [], []
        for x in range(c // SUB):
            da_x = da[x * SUB:(x + 1) * SUB]
            k_x = kv * ch["k_facs"][x]
            dq_x = _dot(da_x, k_x, NN)
            dk_x = _dot(da_x, ch["q_subs"][x], TN)
            dq_rows.append(dq_x)
            qdq_rows.append(dq_x * ch["q_subs"][x])
            dk = dk + dk_x * ch["k_facs"][x]
            kdk = dk_x * k_x
            db = db - kdk
            if x > 0:
                at_start = jnp.sum(kdk, axis=0, keepdims=True) - jnp.sum(qdq_rows[x], axis=0, keepdims=True)
                db = db + jnp.where(row == x * SUB - 1, at_start, 0.0)
        yield
        b_start = jnp.zeros((c, HEAD_DIM), F32)
        for x in range(1, c // SUB):
            b_x = jnp.sum(jnp.where(row < x * SUB, lf_ref[...], 0.0), axis=0, keepdims=True)
            b_start = jnp.where(row >= x * SUB, b_x, b_start)
        dq = dqh * ch["e_b"] + jnp.concatenate(dq_rows, axis=0) * jnp.exp(jnp.minimum(ch["b"] - b_start, 0.0))
        db = db + jnp.concatenate(qdq_rows, axis=0)
        dstate[...] = _dot(dov, qh, TN) + ds1 * ch["e_last"]
        dq_ref[...] = dq
        dk_ref[...] = dk
        dv_ref[...] = _dot(ch["a"], dov, TN) + _dot(k_end, ds1, NT)
        dlf_ref[...] = _rev_cumsum_rows(db)

    rev = lambda c: nc - 1 - c
    tok = pl.BlockSpec((CHUNK, _hps(h) * HEAD_DIM), lambda hh, c: (rev(c), hh))
    mat = pl.BlockSpec((_hps(h), None, HEAD_DIM, HEAD_DIM), lambda hh, c: (hh, rev(c), 0, 0))
    tok_shape = jax.ShapeDtypeStruct(q.shape, F32)
    return pl.pallas_call(
        _each_head(body, 6), name="hgrn_bwd",
        out_shape=(tok_shape, tok_shape, tok_shape, tok_shape),
        grid=(h // _hps(h), nc),
        in_specs=[tok, tok, pl.BlockSpec((CHUNK, _hps(h) * HEAD_DIM), lambda hh, c: (rev(c), v_blk // _hps(h) + hh)), tok, mat,
                  pl.BlockSpec((CHUNK, _hps(h) * HEAD_DIM), lambda hh, c: (rev(c), do_blk // _hps(h) + hh))],
        out_specs=(tok, tok, tok, tok),
        scratch_shapes=[pltpu.VMEM((_hps(h), HEAD_DIM, HEAD_DIM), F32)],
        compiler_params=_params(("parallel", "arbitrary")),
    )(q, k, v, lf, states, do)


CONV_ROWS = 256
HALO = 8


def _shift_down(cur, prev, s):
    rt = cur.shape[0]
    head = jnp.concatenate([pltpu.roll(prev, s, 0), jnp.zeros((rt - HALO, cur.shape[1]), F32)], axis=0)
    return jnp.where(_iota(cur.shape, 0) < s, head, pltpu.roll(cur, s, 0))


def _shift_up(cur, nxt, s):
    rt = cur.shape[0]
    tail = jnp.concatenate([jnp.zeros((rt - HALO, cur.shape[1]), F32), pltpu.roll(nxt, HALO - s, 0)], axis=0)
    return jnp.where(_iota(cur.shape, 0) >= rt - s, tail, pltpu.roll(cur, rt - s, 0))


def _tile_with_prev(ref, i, rt):
    r0 = pl.multiple_of(i * rt, rt)
    cur = ref[pl.ds(r0, rt), :]
    prev = ref[pl.ds(pl.multiple_of(jnp.maximum(r0 - HALO, 0), HALO), HALO), :]
    return cur, jnp.where(i > 0, prev, 0.0)


def _tile_with_next(ref, i, rt, n_tiles):
    r0 = pl.multiple_of(i * rt, rt)
    cur = ref[pl.ds(r0, rt), :]
    nxt = ref[pl.ds(pl.multiple_of(jnp.minimum(r0 + rt, (n_tiles - 1) * rt), HALO), HALO), :]
    return cur, jnp.where(i < n_tiles - 1, nxt, 0.0)


def _conv_tile(x_ref, w_ref, i, rt):
    cur, prev = _tile_with_prev(x_ref, i, rt)
    shifted = [_shift_down(cur, prev, CONV_W - 1 - j) for j in range(CONV_W - 1)] + [cur]
    c = shifted[0] * w_ref[pl.ds(0, 1), :]
    for j in range(1, CONV_W):
        c = c + shifted[j] * w_ref[pl.ds(j, 1), :]
    return c, shifted


def _l2n(s):
    return s * lax.rsqrt(jnp.sum(s * s, axis=-1, keepdims=True) + L2_EPS)


def _gdn_prep_fwd(proj, conv_w, h):
    t = proj.shape[0]
    rt = _tile(t, CONV_ROWS)
    nt = t // rt
    scale = HEAD_DIM ** -0.5

    def body(xq, xk, xv, wq, wk, wv, q_ref, k_ref, v_ref):
        def tile(i, carry):
            rows = pl.ds(pl.multiple_of(i * rt, rt), rt)
            q_ref[rows, :] = _l2n(_silu(_conv_tile(xq, wq, i, rt)[0])) * scale
            k_ref[rows, :] = _l2n(_silu(_conv_tile(xk, wk, i, rt)[0]))
            v_ref[rows, :] = _silu(_conv_tile(xv, wv, i, rt)[0])
            return carry

        lax.fori_loop(0, nt, tile, 0)

    col = lambda p: pl.BlockSpec((t, HEAD_DIM), lambda hh: (0, p * h + hh))
    wcol = lambda p: pl.BlockSpec((CONV_W, HEAD_DIM), lambda hh: (0, p * h + hh))
    out = pl.BlockSpec((t, HEAD_DIM), lambda hh: (0, hh))
    shape = jax.ShapeDtypeStruct((t, h * HEAD_DIM), F32)
    return pl.pallas_call(
        body, name="gdn_prep_fwd", out_shape=(shape, shape, shape), grid=(h,),
        in_specs=[col(0), col(1), col(2), wcol(0), wcol(1), wcol(2)], out_specs=(out, out, out),
        compiler_params=_params(("parallel",)),
    )(proj, proj, proj, conv_w, conv_w, conv_w)


def _gdn_prep_bwd(proj, conv_w, dq, dk, dv, h):
    t = proj.shape[0]
    rt = _tile(t, CONV_ROWS)
    nt = t // rt
    scale = HEAD_DIM ** -0.5

    def part(x_ref, w_ref, dy_ref, dx_ref, dw_ref, dc_ref, norm_scale):
        def first(i, dws):
            rows = pl.ds(pl.multiple_of(i * rt, rt), rt)
            c, shifted = _conv_tile(x_ref, w_ref, i, rt)
            ds = dy_ref[rows, :]
            if norm_scale is not None:
                s = _silu(c)
                r = lax.rsqrt(jnp.sum(s * s, axis=-1, keepdims=True) + L2_EPS)
                y = s * r
                dyn = ds * norm_scale
                ds = r * (dyn - y * jnp.sum(dyn * y, axis=-1, keepdims=True))
            dc = ds * _dsilu(c)
            dc_ref[rows, :] = dc
            return tuple(dws[j] + jnp.sum(dc * shifted[j], axis=0, keepdims=True) for j in range(CONV_W))

        dws = lax.fori_loop(0, nt, first, tuple(jnp.zeros((1, HEAD_DIM), F32) for _ in range(CONV_W)))
        for j in range(CONV_W):
            dw_ref[pl.ds(j, 1), :] = dws[j]

        def second(i, carry):
            rows = pl.ds(pl.multiple_of(i * rt, rt), rt)
            cur, nxt = _tile_with_next(dc_ref, i, rt, nt)
            dx = cur * w_ref[pl.ds(CONV_W - 1, 1), :]
            for j in range(CONV_W - 1):
                dx = dx + _shift_up(cur, nxt, CONV_W - 1 - j) * w_ref[pl.ds(j, 1), :]
            dx_ref[rows, :] = dx.astype(dx_ref.dtype)
            return carry

        lax.fori_loop(0, nt, second, 0)

    def body(xq, xk, xv, wq, wk, wv, dq_ref, dk_ref, dv_ref, dxq, dxk, dxv, dwq, dwk, dwv, dc_ref):
        part(xq, wq, dq_ref, dxq, dwq, dc_ref, scale)
        part(xk, wk, dk_ref, dxk, dwk, dc_ref, 1.0)
        part(xv, wv, dv_ref, dxv, dwv, dc_ref, None)

    col = lambda p: pl.BlockSpec((t, HEAD_DIM), lambda hh: (0, p * h + hh))
    wcol = lambda p: pl.BlockSpec((CONV_W, HEAD_DIM), lambda hh: (0, p * h + hh))
    own = pl.BlockSpec((t, HEAD_DIM), lambda hh: (0, hh))
    wown = pl.BlockSpec((CONV_W, HEAD_DIM), lambda hh: (0, hh))
    dx_shape = jax.ShapeDtypeStruct((t, h * HEAD_DIM), BF16)
    dw_shape = jax.ShapeDtypeStruct((CONV_W, h * HEAD_DIM), F32)
    return pl.pallas_call(
        body, name="gdn_prep_bwd",
        out_shape=(dx_shape, dx_shape, dx_shape, dw_shape, dw_shape, dw_shape), grid=(h,),
        in_specs=[col(0), col(1), col(2), wcol(0), wcol(1), wcol(2), own, own, own],
        out_specs=(own, own, own, wown, wown, wown),
        scratch_shapes=[pltpu.VMEM((t, HEAD_DIM), F32)],
        compiler_params=_params(("parallel",)),
    )(proj, proj, proj, conv_w, conv_w, conv_w, dq, dk, dv)


def _gdn_gates_fwd(ab, a_log_row, dt_bias_row):
    t = ab.shape[0]
    tr = _tile(t, 512)

    def body(ab_ref, al_ref, dt_ref, g_ref, b_ref):
        g_ref[...] = -jnp.exp(al_ref[...]) * _softplus(ab_ref[:, :HEAD_DIM] + dt_ref[...])
        b_ref[...] = _sigmoid(ab_ref[:, HEAD_DIM:])

    row = pl.BlockSpec((tr, HEAD_DIM), lambda i: (i, 0))
    vec = pl.BlockSpec((1, HEAD_DIM), lambda i: (0, 0))
    shape = jax.ShapeDtypeStruct((t, HEAD_DIM), F32)
    return pl.pallas_call(
        body, name="gdn_gates_fwd", out_shape=(shape, shape), grid=(t // tr,),
        in_specs=[pl.BlockSpec((tr, 2 * HEAD_DIM), lambda i: (i, 0)), vec, vec], out_specs=(row, row),
        compiler_params=_params(("parallel",)),
    )(ab, a_log_row, dt_bias_row)


def _gdn_gates_bwd(ab, a_log_row, dt_bias_row, dg, dbeta):
    t = ab.shape[0]
    tr = _tile(t, 512)

    def body(ab_ref, al_ref, dt_ref, dg_ref, db_ref, dab_ref, dal_ref, ddt_ref):
        @pl.when(pl.program_id(0) == 0)
        def _():
            dal_ref[...] = jnp.zeros_like(dal_ref)
            ddt_ref[...] = jnp.zeros_like(ddt_ref)

        xa = ab_ref[:, :HEAD_DIM] + dt_ref[...]
        neg_a = -jnp.exp(al_ref[...])
        dgv = dg_ref[...]
        da = dgv * neg_a * _sigmoid(xa)
        beta = _sigmoid(ab_ref[:, HEAD_DIM:])
        dab_ref[:, :HEAD_DIM] = da.astype(BF16)
        dab_ref[:, HEAD_DIM:] = (db_ref[...] * beta * (1.0 - beta)).astype(BF16)
        dal_ref[...] += jnp.sum(dgv * neg_a * _softplus(xa), axis=0, keepdims=True)
        ddt_ref[...] += jnp.sum(da, axis=0, keepdims=True)

    row = pl.BlockSpec((tr, HEAD_DIM), lambda i: (i, 0))
    row2 = pl.BlockSpec((tr, 2 * HEAD_DIM), lambda i: (i, 0))
    vec = pl.BlockSpec((1, HEAD_DIM), lambda i: (0, 0))
    vshape = jax.ShapeDtypeStruct((1, HEAD_DIM), F32)
    return pl.pallas_call(
        body, name="gdn_gates_bwd",
        out_shape=(jax.ShapeDtypeStruct((t, 2 * HEAD_DIM), BF16), vshape, vshape), grid=(t // tr,),
        in_specs=[row2, vec, vec, row, row], out_specs=(row2, vec, vec),
        compiler_params=_params(("arbitrary",)),
    )(ab, a_log_row, dt_bias_row, dg, dbeta)


def _lower_bound(lb_ref):
    return _sigmoid(lb_ref[pl.ds(0, 1), :] - lb_ref[pl.ds(1, 1), :])


def _hgrn_prep_fwd(proj, lb_logits, h, q_blk, f_blk):
    t = proj.shape[0]
    tr = _tile(t, 512)

    def body(xq, xf, lb_ref, q_ref, k_ref, lf_ref):
        lb = _lower_bound(lb_ref)
        s = _sigmoid(xf[...])
        q_ref[...] = _silu(xq[...])
        k_ref[...] = (1.0 - lb) * (1.0 - s)
        lf_ref[...] = jnp.log(lb + (1.0 - lb) * s)

    width = h * HEAD_DIM
    col = lambda b0: pl.BlockSpec((tr, width), lambda i: (i, b0 // h))
    own = pl.BlockSpec((tr, width), lambda i: (i, 0))
    shape = jax.ShapeDtypeStruct((t, width), F32)
    return pl.pallas_call(
        body, name="hgrn_prep_fwd", out_shape=(shape, shape, shape), grid=(t // tr,),
        in_specs=[col(q_blk), col(f_blk), pl.BlockSpec((2, width), lambda i: (0, 0))],
        out_specs=(own, own, own), compiler_params=_params(("parallel",)),
    )(proj, proj, lb_logits)


def _hgrn_prep_bwd(proj, lb_logits, dq, dk, dlf, h, q_blk, f_blk):
    t = proj.shape[0]
    tr = _tile(t, 512)

    def body(xq, xf, lb_ref, dq_ref, dk_ref, dlf_ref, dxq, dxf, dlb_ref):
        @pl.when(pl.program_id(0) == 0)
        def _():
            dlb_ref[...] = jnp.zeros_like(dlb_ref)

        lb = _lower_bound(lb_ref)
        s = _sigmoid(xf[...])
        e = dlf_ref[...] / (lb + (1.0 - lb) * s) - dk_ref[...]
        dxq[...] = (dq_ref[...] * _dsilu(xq[...])).astype(BF16)
        dxf[...] = (s * (1.0 - s) * (1.0 - lb) * e).astype(BF16)
        d0 = jnp.sum((1.0 - s) * e, axis=0, keepdims=True) * (lb * (1.0 - lb))
        dlb_ref[pl.ds(0, 1), :] += d0
        dlb_ref[pl.ds(1, 1), :] += -d0

    width = h * HEAD_DIM
    col = lambda b0: pl.BlockSpec((tr, width), lambda i: (i, b0 // h))
    own = pl.BlockSpec((tr, width), lambda i: (i, 0))
    lbs = pl.BlockSpec((2, width), lambda i: (0, 0))
    shape = jax.ShapeDtypeStruct((t, width), BF16)
    return pl.pallas_call(
        body, name="hgrn_prep_bwd",
        out_shape=(shape, shape, jax.ShapeDtypeStruct((2, width), F32)), grid=(t // tr,),
        in_specs=[col(q_blk), col(f_blk), lbs, own, own, own], out_specs=(own, own, lbs),
        compiler_params=_params(("arbitrary",)),
    )(proj, proj, lb_logits, dq, dk, dlf)


GATE_HEADS = 4


def _gate_specs(h, z_blk, g_blk, tr):
    g = min(GATE_HEADS, h)
    n = h // g
    width = g * HEAD_DIM
    o_a = pl.BlockSpec((tr, width), lambda gg, i: (i, jnp.minimum(gg, n - 1)))
    o_b = pl.BlockSpec((tr, width), lambda gg, i: (i, jnp.maximum(gg - n, 0)))
    gate = pl.BlockSpec((tr, width), lambda gg, i: (i, jnp.where(gg < n, z_blk // g + gg, g_blk // g + gg - n)))
    w = pl.BlockSpec((None, 1, HEAD_DIM), lambda gg, i: (gg // n, 0, 0))
    cat = pl.BlockSpec((tr, width), lambda gg, i: (i, gg))
    return (o_a, o_b, gate, w, cat), g, n


def _silu_and_grad(x):
    s = _sigmoid(x)
    return x * s, s * (1.0 + x * (1.0 - s))


def _gate_fwd(o_a, o_b, proj, norm_w, h, z_blk, g_blk):
    t = o_a.shape[0]
    tr = _tile(t, 512)

    (sa, sb, sg, sw, cat), g, n = _gate_specs(h, z_blk, g_blk, tr)

    def body(oa_ref, ob_ref, z_ref, w_ref, y_ref):
        for k in range(g):
            lanes = pl.ds(k * HEAD_DIM, HEAD_DIM)
            o = jnp.where(pl.program_id(0) < n, oa_ref[:, lanes], ob_ref[:, lanes])
            r = lax.rsqrt(jnp.mean(o * o, axis=-1, keepdims=True) + NORM_EPS)
            y_ref[:, lanes] = (o * r * w_ref[...] * _silu(z_ref[:, lanes])).astype(y_ref.dtype)

    return pl.pallas_call(
        body, name="gate_fwd", out_shape=jax.ShapeDtypeStruct((t, 2 * h * HEAD_DIM), BF16),
        grid=(2 * n, t // tr), in_specs=[sa, sb, sg, sw], out_specs=cat,
        compiler_params=_params(("parallel", "parallel")),
    )(o_a, o_b, proj, norm_w)


def _gate_bwd(o_a, o_b, proj, norm_w, dy, h, z_blk, g_blk):
    t = o_a.shape[0]
    tr = _tile(t, 512)

    (sa, sb, sg, sw, cat), g, n = _gate_specs(h, z_blk, g_blk, tr)

    def body(oa_ref, ob_ref, z_ref, w_ref, dy_ref, do_ref, dz_ref, dw_ref):
        gg = pl.program_id(0)

        @pl.when(jnp.logical_and(gg % n == 0, pl.program_id(1) == 0))
        def _():
            dw_ref[...] = jnp.zeros_like(dw_ref)

        w = w_ref[...]
        dw = jnp.zeros_like(w)
        for k in range(g):
            lanes = pl.ds(k * HEAD_DIM, HEAD_DIM)
            o = jnp.where(gg < n, oa_ref[:, lanes], ob_ref[:, lanes])
            dyv = dy_ref[:, lanes]
            r = lax.rsqrt(jnp.mean(o * o, axis=-1, keepdims=True) + NORM_EPS)
            oh = o * r
            act, dact = _silu_and_grad(z_ref[:, lanes])
            dz_ref[:, lanes] = (dyv * oh * w * dact).astype(dz_ref.dtype)
            dn = dyv * act
            doh = dn * w
            do_ref[:, lanes] = r * (doh - oh * jnp.mean(doh * oh, axis=-1, keepdims=True))
            dw = dw + jnp.sum(dn * oh, axis=0, keepdims=True)
        dw_ref[...] += dw

    width = 2 * h * HEAD_DIM
    return pl.pallas_call(
        body, name="gate_bwd",
        out_shape=(jax.ShapeDtypeStruct((t, width), F32), jax.ShapeDtypeStruct((t, width), BF16),
                   jax.ShapeDtypeStruct((2, 1, HEAD_DIM), F32)),
        grid=(2 * n, t // tr), in_specs=[sa, sb, sg, sw, cat], out_specs=(cat, cat, sw),
        compiler_params=_params(("arbitrary", "arbitrary")),
    )(o_a, o_b, proj, norm_w, dy)


def _lane_row(vec):
    return jnp.pad(vec.reshape(1, -1), ((0, 0), (0, HEAD_DIM - vec.shape[-1])))


def _add_epi(acc, res):
    return (acc + res,)


def _split_w_in(w_in, h):
    gw = h * HEAD_DIM
    main = jnp.concatenate([w_in[:, :4 * gw], w_in[:, 4 * gw + 2 * h:]], axis=1)
    pad = jnp.zeros((w_in.shape[0], HEAD_DIM - h), w_in.dtype)
    ab = jnp.concatenate([w_in[:, 4 * gw:4 * gw + h], pad, w_in[:, 4 * gw + h:4 * gw + 2 * h], pad], axis=1)
    return main, ab


def _merge_w_in(main, ab, h):
    gw = h * HEAD_DIM
    return jnp.concatenate([main[:, :4 * gw], ab[:, :h], ab[:, HEAD_DIM:HEAD_DIM + h], main[:, 4 * gw:]], axis=1)


def _local_step(x, target, w_main, w_ab, conv_w, a_log, dt_bias, gdn_norm_w, lb_logits, hgrn_norm_w,
                w_out, norm_mix_w, norm_ffn_w, w_ff1, w_ff2, norm_final_w, reducer=None, n1=None):
    t, d = x.shape
    h = d // (2 * HEAD_DIM)
    gw = h * HEAD_DIM
    k_blk, v_blk, z_blk, qb_blk, fb_blk, ib_blk, gb_blk = (i * h for i in range(1, 8))
    del k_blk, v_blk

    if n1 is None:
        n1 = _rms_fwd(x, norm_mix_w, "rms_mix")
    stacked = w_main.ndim == 3
    proj = _mm(n1, w_main, "nn", (F32,), "in_proj", b_stacked=stacked)
    ab = _mm(n1, w_ab, "nn", (F32,), "in_proj_ab")

    q, k, v = _gdn_prep_fwd(proj, conv_w, h)
    a_log_row, dt_row = _lane_row(a_log), _lane_row(dt_bias)
    g_tm, beta_tm = _gdn_gates_fwd(ab, a_log_row, dt_row)
    to_heads = lambda a: jnp.broadcast_to(a[:, :h].T[:, :, None], (h, t, HEAD_DIM))
    g_bc, beta_bc = to_heads(g_tm), to_heads(beta_tm)
    o_a, st_a, inv_a = _gdn_fwd(q, k, v, beta_bc, g_bc)

    qh, kh, lf = _hgrn_prep_fwd(proj, lb_logits, h, qb_blk, fb_blk)
    o_b, st_b = _hgrn_fwd(qh, kh, proj, lf, v_blk=ib_blk)

    gate_w = jnp.stack([gdn_norm_w.reshape(1, HEAD_DIM), hgrn_norm_w.reshape(1, HEAD_DIM)])
    y = _gate_fwd(o_a, o_b, proj, gate_w, h, z_blk, gb_blk)
    h1 = _mm(y, w_out, "nn", (F32,), "out_proj", epi=_add_epi, extras=(x,))
    n2 = _rms_fwd(h1, norm_ffn_w, "rms_ffn")
    act, r = _mm(n2, w_ff1, "nn", (F32, BF16), "ff1", b_stacked=True,
                 epi=lambda acc: (acc, jnp.square(jnp.maximum(acc, 0.0))))
    h2 = _mm(r, w_ff2, "nn", (F32,), "ff2", epi=_add_epi, extras=(h1,))
    loss, dh2, dh2_b, d_norm_final = _loss_head(h2, norm_final_w, target)

    da = _mm(dh2_b, w_ff2, "nt", (BF16,), "ff2_dx",
             epi=lambda acc, a: (acc * (2.0 * jnp.maximum(a, 0.0)),), extras=(act,))
    pending = []

    def step(anchor, name=None, full=None):
        if reducer is not None:
            pending.extend(reducer.step(name, full, anchor))

    def after_step(value):
        if not pending:
            return value
        value = lax.optimization_barrier((value, *pending))[0]
        pending.clear()
        return value

    d_ff2 = _mm(r, dh2_b, "tn", (F32,), "ff2_dw")
    step(None, "w_ff2", d_ff2)
    dn2 = _mm(da, w_ff1, "nt", (F32,), "ff1_dx", b_stacked=True)
    d_ff1 = _mm(n2, da, "tn", (F32,), "ff1_dw", out_stacked=True)
    step(d_ff1, "w_ff1", d_ff1)
    dh1, dh1_b, d_norm_ffn = _rms_bwd(after_step(dn2), h1, norm_ffn_w, dh2, "rms_ffn_bwd")
    dy = _mm(dh1_b, w_out, "nt", (F32,), "out_proj_dx")
    d_out = _mm(y, dh1_b, "tn", (F32,), "out_proj_dw")
    step(d_out, "w_out", d_out)

    do, dgate, d_gate_w = _gate_bwd(o_a, o_b, proj, gate_w, after_step(dy), h, z_blk, gb_blk)
    step(do)
    dq, dk, dv, dbeta_bc, dg_bc = _gdn_bwd(q, k, v, beta_bc, g_bc, st_a, inv_a, after_step(do), do_blk=0)
    step(dq)
    dxq, dxk, dxv, dcq, dck, dcv = _gdn_prep_bwd(proj, conv_w, after_step(dq), dk, dv, h)
    step(dxq)
    from_heads = lambda a: jnp.pad(a[:, :, 0, :].reshape(h, t).T, ((0, 0), (0, HEAD_DIM - h)))
    dab, d_a_log, d_dt_bias = _gdn_gates_bwd(ab, a_log_row, dt_row, from_heads(dg_bc), from_heads(dbeta_bc))
    dqh, dkh, dvh, dlf = _hgrn_bwd(after_step(qh), kh, proj, lf, st_b, do, v_blk=ib_blk, do_blk=h)
    step(dqh)
    dxqb, dxfb, d_lb = _hgrn_prep_bwd(proj, lb_logits, dqh, dkh, dlf, h, qb_blk, fb_blk)

    dproj = jnp.concatenate([after_step(dxq), dxk, dxv, dgate[:, :gw], dxqb, dxfb, dvh.astype(BF16), dgate[:, gw:]],
                            axis=1)
    d_main = _mm(n1, dproj, "tn", (F32,), "in_proj_dw", out_stacked=True)
    d_ab = _mm(n1, dab, "tn", (F32,), "in_proj_ab_dw")
    step(d_main, "w_in", d_main)
    dn1_ab = _mm(after_step(dab), w_ab, "nt", (F32,), "in_proj_ab_dx")
    step(dn1_ab)
    dn1 = _mm(after_step(dproj), w_main, "nt", (F32,), "in_proj_dx", epi=_add_epi, extras=(dn1_ab,),
              b_stacked=stacked)
    step(dn1)
    dx, _, d_norm_mix = _rms_bwd(after_step(dn1), x, norm_mix_w, dh1, "rms_mix_bwd")
    step(dx)

    grads = dict(
        w_main=d_main, w_ab=d_ab, conv_w=jnp.concatenate([dcq, dck, dcv], axis=1),
        gdn_a_log=d_a_log[:, :h], gdn_dt_bias=d_dt_bias[:, :h], gdn_norm_w=d_gate_w[0],
        hgrn_lb_logits=d_lb, hgrn_norm_w=d_gate_w[1], w_out=d_out, norm_mix_w=d_norm_mix,
        norm_ffn_w=d_norm_ffn, w_ff1=d_ff1, w_ff2=d_ff2, norm_final_w=d_norm_final)
    return loss, dx, grads


N_CHIPS = 4
ANY = pl.BlockSpec(memory_space=pl.ANY)


def _place():
    x, y, c = lax.axis_index("x"), lax.axis_index("y"), lax.axis_index("c")
    chips = [(1 - x, y), (x, 1 - y), (1 - x, 1 - y)]
    return x, y, c, chips


def _remote(src, dst, send_sems, recv_sems, k, to):
    return pltpu.make_async_remote_copy(src_ref=src, dst_ref=dst, send_sem=send_sems.at[k],
                                        recv_sem=recv_sems.at[k], device_id=to, device_id_type=MESH)


def _to_sibling(x, y, c, chips):
    return [(x, y, 1 - c)]


def _to_same_core_of_chips(x, y, c, chips):
    return [(*chip, c) for chip in chips]


def _to_all_gather_peers(x, y, c, chips):
    return _to_sibling(x, y, c, chips) + _to_same_core_of_chips(x, y, c, chips)


SIBLING_EXCHANGE = (1, _to_sibling)
CHIP_EXCHANGE = (2, _to_same_core_of_chips)
GATHER_EXCHANGE = (3, _to_all_gather_peers)


def _launch(body, name, out_shapes, arrays, sem_counts, sequencer=None, after=()):
    n, n_after = len(arrays), len(after)
    sems = [pltpu.SemaphoreType.DMA((k,)) for k in sem_counts]
    strip = lambda refs: refs[:n] + refs[n + n_after:]
    if sequencer is None:
        return pl.pallas_call(
            lambda *refs: body(*strip(refs)), name=name, out_shape=tuple(out_shapes),
            in_specs=[ANY] * (n + n_after), out_specs=tuple(ANY for _ in out_shapes), scratch_shapes=sems,
        )(*arrays, *after)
    collective_id, peers = sequencer

    def sequencer_body(*refs):
        x, y, c, chips = _place()
        barrier = pltpu.get_barrier_semaphore()
        targets = peers(x, y, c, chips)
        for target in targets:
            pl.semaphore_signal(barrier, inc=1, device_id=target, device_id_type=MESH)
        pl.semaphore_wait(barrier, len(targets))
        body(*strip(refs))

    return pl.kernel(
        sequencer_body, name=name, out_type=tuple(out_shapes),
        mesh=plsc.ScalarSubcoreMesh(axis_name="sequencer", num_cores=1), scratch_types=tuple(sems),
        compiler_params=pltpu.CompilerParams(collective_id=collective_id),
    )(*arrays, *after)


def _gather_weights(big, small, name, sequencer=None, after=()):
    nb, ns = len(big), len(small)
    n_sem = 6 * nb + 3 * ns

    def body(*refs):
        ins, outs = refs[:nb + ns], refs[nb + ns:2 * (nb + ns)]
        send_sems, recv_sems = refs[2 * (nb + ns):]
        x, y, c, chips = _place()
        me, sibling = 2 * x + y, (x, y, 1 - c)

        def half(a, chip, hc):
            rh = big[a].shape[0] // 2
            return outs[a].at[2 * chip[0] + chip[1], pl.ds(hc * rh, rh), :]

        first, passed = [], []
        for a in range(nb):
            rh = big[a].shape[0] // 2
            for j, chip in enumerate(chips):
                first.append(_remote(ins[a].at[pl.ds(c * rh, rh), :], half(a, (x, y), c),
                                     send_sems, recv_sems, 6 * a + j, (*chip, c)))
        for s in range(ns):
            for j, chip in enumerate(chips):
                first.append(_remote(ins[nb + s], outs[nb + s].at[me], send_sems, recv_sems,
                                     6 * nb + 3 * s + j, (*chip, c)))
        for cp in first:
            cp.start()
        for a in range(nb):
            for j, chip in enumerate(chips):
                _remote(half(a, chip, c), half(a, chip, c), send_sems, recv_sems, 6 * a + j, (*chip, c)).wait_recv()
                fwd = _remote(half(a, chip, c), half(a, chip, c), send_sems, recv_sems, 6 * a + 3 + j, sibling)
                fwd.start()
                passed.append(fwd)
        for s in range(ns):
            for j, chip in enumerate(chips):
                dst = outs[nb + s].at[2 * chip[0] + chip[1]]
                _remote(dst, dst, send_sems, recv_sems, 6 * nb + 3 * s + j, (*chip, c)).wait_recv()
        for a in range(nb):
            for j, chip in enumerate(chips):
                _remote(half(a, chip, 1 - c), half(a, chip, 1 - c), send_sems, recv_sems,
                        6 * a + 3 + j, sibling).wait_recv()
        for cp in first + passed:
            cp.wait_send()

    arrays = list(big) + list(small)
    out_shapes = [jax.ShapeDtypeStruct((N_CHIPS,) + a.shape, a.dtype) for a in arrays]
    return _launch(body, name, out_shapes, arrays, (n_sem, n_sem), sequencer, after)


def _swap_halves(parts, name, sequencer=None):
    n = len(parts)

    def body(*refs):
        ins, outs = refs[:n], refs[n:2 * n]
        send_sems, recv_sems = refs[2 * n:]
        x, y, c, _ = _place()
        copies = [_remote(ins[a].at[s, 1 - c], outs[a].at[s], send_sems, recv_sems, N_CHIPS * a + s, (x, y, 1 - c))
                  for a in range(n) for s in range(N_CHIPS)]
        for cp in copies:
            cp.start()
        for cp in copies:
            cp.wait()

    out_shapes = [jax.ShapeDtypeStruct((N_CHIPS,) + p.shape[2:], p.dtype) for p in parts]
    return _launch(body, name, out_shapes, parts, (N_CHIPS * n, N_CHIPS * n), sequencer)


def _scatter_to_owners(parts, name, sequencer=None):
    n = len(parts)

    def body(*refs):
        ins, outs = refs[:n], refs[n:2 * n]
        send_sems, recv_sems = refs[2 * n:]
        x, y, c, chips = _place()
        copies = [_remote(ins[a].at[2 * chip[0] + chip[1]], outs[a].at[j], send_sems, recv_sems,
                          3 * a + j, (*chip, c))
                  for a in range(n) for j, chip in enumerate(chips)]
        for cp in copies:
            cp.start()
        for cp in copies:
            cp.wait()

    out_shapes = [jax.ShapeDtypeStruct((3,) + p.shape[1:], p.dtype) for p in parts]
    return _launch(body, name, out_shapes, parts, (3 * n, 3 * n), sequencer)


def _send_to_sibling(halves, name, sequencer=None):
    n = len(halves)

    def body(*refs):
        ins, outs = refs[:n], refs[n:2 * n]
        send_sems, recv_sems = refs[2 * n:]
        x, y, c, _ = _place()
        copies = [_remote(ins[a], outs[a], send_sems, recv_sems, a, (x, y, 1 - c)) for a in range(n)]
        for cp in copies:
            cp.start()
        for cp in copies:
            cp.wait()

    out_shapes = [jax.ShapeDtypeStruct(p.shape, p.dtype) for p in halves]
    return _launch(body, name, out_shapes, halves, (n, n), sequencer)


N_DEV = 8


def _all_reduce_small(vec):
    def body(v_ref, gathered, total, send_sems, recv_sems):
        x, y, c, _ = _place()
        me = 4 * x + 2 * y + c
        gathered[me] = v_ref[...]
        copies = []
        for k in range(1, N_DEV):
            px = 1 - x if k & 4 else x
            py = 1 - y if k & 2 else y
            pc = 1 - c if k & 1 else c
            copies.append(_remote(v_ref, gathered.at[me], send_sems, recv_sems, k - 1, (px, py, pc)))
        for cp in copies:
            cp.start()
        for k, cp in enumerate(copies):
            cp.wait_send()
        for k in range(1, N_DEV):
            px = 1 - x if k & 4 else x
            py = 1 - y if k & 2 else y
            pc = 1 - c if k & 1 else c
            src = gathered.at[4 * px + 2 * py + pc]
            _remote(src, src, send_sems, recv_sems, k - 1, (px, py, pc)).wait_recv()
        acc = gathered[0]
        for dev in range(1, N_DEV):
            acc = acc + gathered[dev]
        total[...] = acc

    vm = pl.BlockSpec(memory_space=pltpu.VMEM)
    return pl.pallas_call(
        body, name="all_reduce_small",
        out_shape=(jax.ShapeDtypeStruct((N_DEV,) + vec.shape, F32), jax.ShapeDtypeStruct(vec.shape, F32)),
        in_specs=[vm], out_specs=(vm, vm),
        scratch_shapes=[pltpu.SemaphoreType.DMA((N_DEV - 1,)), pltpu.SemaphoreType.DMA((N_DEV - 1,))],
    )(vec)[1]


def _chip_sum(part, recv, c):
    _, _, rh, cols = part.shape
    tr = _tile(rh, 256)

    def body(c_ref, p_ref, r_ref, s_ref, sb_ref):
        s = p_ref[...] + r_ref[...]
        s_ref[...] = s
        sb_ref[...] = s.astype(BF16)

    blk = pl.BlockSpec((None, tr, cols), lambda s, i, c_ref: (s, i, 0))
    return pl.pallas_call(
        body, name="grad_chip_sum",
        out_shape=(jax.ShapeDtypeStruct(recv.shape, F32), jax.ShapeDtypeStruct(recv.shape, BF16)),
        grid_spec=pltpu.PrefetchScalarGridSpec(
            num_scalar_prefetch=1, grid=(N_CHIPS, rh // tr),
            in_specs=[pl.BlockSpec((None, None, tr, cols), lambda s, i, c_ref: (s, c_ref[0], i, 0)), blk],
            out_specs=(blk, blk)),
        compiler_params=_params(("parallel", "parallel")),
    )(c, part, recv)


def _owner_sum(own, recv, me):
    _, rh, cols = own.shape
    tr = _tile(rh, 256)

    def body(me_ref, o_ref, r0, r1, r2, g_ref):
        g_ref[...] = ((o_ref[...] + r0[...].astype(F32)) + r1[...].astype(F32)) + r2[...].astype(F32)

    slot = lambda j: pl.BlockSpec((None, tr, cols), lambda i, me_ref: (j, i, 0))
    return pl.pallas_call(
        body, name="grad_owner_sum", out_shape=jax.ShapeDtypeStruct((rh, cols), F32),
        grid_spec=pltpu.PrefetchScalarGridSpec(
            num_scalar_prefetch=1, grid=(rh // tr,),
            in_specs=[pl.BlockSpec((None, tr, cols), lambda i, me_ref: (me_ref[0], i, 0)), slot(0), slot(1), slot(2)],
            out_specs=pl.BlockSpec((tr, cols), lambda i, me_ref: (i, 0))),
        compiler_params=_params(("parallel",)),
    )(me, own, recv, recv, recv)


def _adamw_math(w, g, m, v):
    c1 = 1.0 / (1.0 - ADAM_B1 ** ADAM_STEP)
    c2 = 1.0 / (1.0 - ADAM_B2 ** ADAM_STEP)
    nm = ADAM_B1 * m + (1.0 - ADAM_B1) * g
    nv = ADAM_B2 * v + (1.0 - ADAM_B2) * (g * g)
    return -ADAM_LR * ((nm * c1) / (jnp.sqrt(nv * c2) + ADAM_EPS) + ADAM_WD * w), nm, nv


def _adamw_unit_rows(w, g, m, v, name):
    rows, _, cols = w.shape
    tr = max(d for d in range(1, 33) if rows % d == 0)

    def body(w_ref, g_ref, m_ref, v_ref, d_ref, nm_ref, nv_ref):
        d_ref[...], nm_ref[...], nv_ref[...] = _adamw_math(w_ref[...], g_ref[...], m_ref[...], v_ref[...])

    blk = pl.BlockSpec((tr, 1, cols), lambda i: (i, 0, 0))
    shape = jax.ShapeDtypeStruct(w.shape, F32)
    return pl.pallas_call(
        body, name=name, out_shape=(shape, shape, shape), grid=(rows // tr,),
        in_specs=[blk, blk, blk, blk], out_specs=(blk, blk, blk),
        compiler_params=_params(("parallel",)),
    )(w, g, m, v)


def _divisor_tile(n, want):
    return max(d for d in range(ROW_TILE, want + 1, ROW_TILE) if n % d == 0)


def _adamw(w, g, m, v, name):
    if w.ndim == 3:
        return _adamw_unit_rows(w, g, m, v, name)
    rows, cols = w.shape
    tr = _divisor_tile(rows, 2048) if rows % 8 == 0 else rows
    c1 = 1.0 / (1.0 - ADAM_B1 ** ADAM_STEP)
    c2 = 1.0 / (1.0 - ADAM_B2 ** ADAM_STEP)

    def body(w_ref, g_ref, m_ref, v_ref, d_ref, nm_ref, nv_ref):
        gv = g_ref[...]
        nm = ADAM_B1 * m_ref[...] + (1.0 - ADAM_B1) * gv
        nv = ADAM_B2 * v_ref[...] + (1.0 - ADAM_B2) * (gv * gv)
        d_ref[...] = -ADAM_LR * ((nm * c1) / (jnp.sqrt(nv * c2) + ADAM_EPS) + ADAM_WD * w_ref[...])
        nm_ref[...] = nm
        nv_ref[...] = nv

    blk = pl.BlockSpec((tr, cols), lambda i: (i, 0))
    shape = jax.ShapeDtypeStruct((rows, cols), F32)
    return pl.pallas_call(
        body, name=name, out_shape=(shape, shape, shape), grid=(rows // tr,),
        in_specs=[blk, blk, blk, blk], out_specs=(blk, blk, blk),
        compiler_params=_params(("parallel",)),
    )(w, g, m, v)


def _adamw_halves(w, g_own, g_sib, m, v, c, name):
    _, rows, cols = w.shape
    rh = rows // 2
    tr = _tile(rh, 256)
    per = rh // tr
    c1 = 1.0 / (1.0 - ADAM_B1 ** ADAM_STEP)
    c2 = 1.0 / (1.0 - ADAM_B2 ** ADAM_STEP)

    def body(c_ref, w_ref, go_ref, gs_ref, m_ref, v_ref, g_ref, d_ref, nm_ref, nv_ref):
        own = pl.program_id(0) // per == c_ref[0]
        gv = jnp.where(own, go_ref[...], gs_ref[...])
        nm = ADAM_B1 * m_ref[...] + (1.0 - ADAM_B1) * gv
        nv = ADAM_B2 * v_ref[...] + (1.0 - ADAM_B2) * (gv * gv)
        g_ref[...] = gv
        d_ref[...] = -ADAM_LR * ((nm * c1) / (jnp.sqrt(nv * c2) + ADAM_EPS) + ADAM_WD * w_ref[...])
        nm_ref[...] = nm
        nv_ref[...] = nv

    blk = pl.BlockSpec((None, tr, cols), lambda i, c_ref: (0, i, 0))
    half = pl.BlockSpec((tr, cols), lambda i, c_ref: (i % per, 0))
    shape = jax.ShapeDtypeStruct((1, rows, cols), F32)
    return pl.pallas_call(
        body, name=name, out_shape=(shape, shape, shape, shape),
        grid_spec=pltpu.PrefetchScalarGridSpec(
            num_scalar_prefetch=1, grid=(rows // tr,),
            in_specs=[blk, half, half, blk, blk], out_specs=(blk, blk, blk, blk)),
        compiler_params=_params(("parallel",)),
    )(c, w, g_own, g_sib, m, v)


def _by_shard(name, full):
    if name in ("w_in", "w_ff1"):
        st = full
    else:
        st = full.reshape(N_CHIPS, -1, full.shape[1])
    return st.reshape(N_CHIPS, 2, st.shape[1] // 2, st.shape[2])


class _GradReducer:
    def __init__(self, w, m, v, my_c, my_chip):
        self.w, self.m, self.v, self.my_c, self.my_chip = w, m, v, my_c, my_chip
        self.in_flight = []
        self.computed = []
        self.anchor = None
        self.done = {}

    def step(self, name=None, full=None, anchor=None):
        stages, self.in_flight, self.computed, self.anchor = self.in_flight, [], [], anchor
        for stage in [s for s in stages if not getattr(s, "long", False)]:
            self._advance(stage)
        if name is not None:
            self.in_flight.append(self._swap(name, _by_shard(name, full)))
        for stage in [s for s in stages if getattr(s, "long", False)]:
            self._advance(stage)
        return self.computed

    def _held(self, value):
        if self.anchor is None:
            return value
        return lax.optimization_barrier((value, self.anchor))[0]

    def _advance(self, stage):
        nxt = stage()
        if nxt is not None:
            self.in_flight.append(nxt)

    def finish(self):
        while self.in_flight:
            self.step()
        return self.done

    def _swap(self, name, part):
        got, = _swap_halves([part], "grad_swap_" + name, SIBLING_EXCHANGE)

        def scatter():
            total, total_bf16 = _chip_sum(part, self._held(got), self.my_c)
            self.computed.append(total_bf16)
            recv, = _scatter_to_owners([total_bf16], "grad_scatter_" + name, CHIP_EXCHANGE)

            def send():
                half = _owner_sum(total, self._held(recv), self.my_chip.reshape(1))
                self.computed.append(half)
                sib, = _send_to_sibling([half], "grad_send_" + name, SIBLING_EXCHANGE)

                def update():
                    self.done[name] = _adamw_halves(self.w[name], half, self._held(sib), self.m[name], self.v[name],
                                                    self.my_c, "adamw_" + name)
                    self.computed.append(self.done[name][0])

                if name == "w_in":
                    self.in_proj_halves = (half, sib)
                    return None
                return update
            return lambda: send
        scatter.long = True
        return scatter


def _adamw_minor_rows(w, g, m, v, name):
    _, rows, cols = w.shape
    turned = lambda a: jnp.transpose(a, (2, 0, 1))
    back = lambda a: jnp.transpose(a, (1, 2, 0))
    g = g.T.reshape(cols, 1, rows)
    delta, new_m, new_v = _adamw(turned(w), g, turned(m), turned(v), name)
    return back(g), back(delta), back(new_m), back(new_v)


def _in_proj_quarter_part(shard, chip, h):
    e = h // 2
    qw = shard.shape[1] - e
    zeros = jnp.zeros((shard.shape[0], h), shard.dtype)
    padded = jnp.concatenate([zeros, shard, zeros], axis=1)
    at_quarter = jnp.where(chip == 0, h, jnp.where(chip == 1, h - e, jnp.where(chip == 2, 2 * h, h + e)))
    at_edge = jnp.where(chip == 0, h + qw, jnp.where(chip == 1, h + qw - e, jnp.where(chip == 2, h, h - e)))
    return (lax.dynamic_slice_in_dim(padded, at_quarter, qw, axis=1).astype(BF16),
            lax.dynamic_slice_in_dim(padded, at_edge, h, axis=1))


def _in_proj_quarters(parts, edges, h):
    e = h // 2
    x1, a, b, x2 = edges[0][:, :e], edges[1], edges[2], edges[3][:, e:]
    parts = parts.at[1, :, :e].set(x1.astype(BF16))
    parts = parts.at[2, :, parts.shape[2] - e:].set(x2.astype(BF16))
    pad = jnp.zeros((a.shape[0], HEAD_DIM - h), a.dtype)
    return parts, jnp.concatenate([a, pad, b, pad], axis=1).astype(BF16)


def _in_proj_edge_columns(d_main, d_ab, h):
    e = h // 2
    return jnp.concatenate([d_main[1][:, :e], d_main[2][:, -e:], d_ab[:, :h], d_ab[:, HEAD_DIM:HEAD_DIM + h]], axis=1)


def _in_proj_shard_grad(q_own, q_sib, edges, c, chip, h):
    e = h // 2
    lower, upper = jnp.where(c[0] == 0, q_own, q_sib), jnp.where(c[0] == 0, q_sib, q_own)
    quarter = jnp.concatenate([lower, upper], axis=0)
    x1, x2, a, b = edges[:, :e], edges[:, e:2 * e], edges[:, 2 * e:2 * e + h], edges[:, 2 * e + h:]
    zeros = jnp.zeros_like(x1)
    left = jnp.where(chip == 2, b, jnp.concatenate([zeros, x2], axis=1))
    right = jnp.where(chip == 1, a, jnp.concatenate([x1, zeros], axis=1))
    start = jnp.where(chip == 0, h, jnp.where(chip == 1, h + e, jnp.where(chip == 2, 0, h - e)))
    padded = jnp.concatenate([left, quarter, right], axis=1)
    return lax.dynamic_slice_in_dim(padded, start, quarter.shape[1] + e, axis=1)


SMALL = ("gdn_a_log", "gdn_dt_bias", "gdn_norm_w", "hgrn_lb_logits", "hgrn_norm_w",
         "norm_mix_w", "norm_ffn_w", "norm_final_w")
BIG = ("w_in", "w_out", "w_ff1", "w_ff2")
ORDER = ("w_in", "conv_w", "gdn_a_log", "gdn_dt_bias", "gdn_norm_w", "hgrn_lb_logits", "hgrn_norm_w",
         "w_out", "norm_mix_w", "norm_ffn_w", "w_ff1", "w_ff2", "norm_final_w")


def _pack(pieces):
    flat = jnp.concatenate([p.reshape(-1).astype(F32) for p in pieces])
    rows = -(-flat.shape[0] // (8 * HEAD_DIM)) * 8
    return jnp.pad(flat, (0, rows * HEAD_DIM - flat.shape[0])).reshape(rows, HEAD_DIM)


def _unpack(packed, shapes):
    flat, out, at = packed.reshape(-1), [], 0
    for s in shapes:
        n = 1
        for dim in s:
            n *= dim
        out.append(flat[at:at + n].reshape(s))
        at += n
    return out


def kernel(x, w_in, conv_w, gdn_a_log, gdn_dt_bias, gdn_norm_w, hgrn_lb_logits, hgrn_norm_w, w_out, norm_mix_w, norm_ffn_w, w_ff1, w_ff2, norm_final_w, loss_target, m_w_in, m_conv_w, m_gdn_a_log, m_gdn_dt_bias, m_gdn_norm_w, m_hgrn_lb_logits, m_hgrn_norm_w, m_w_out, m_norm_mix_w, m_norm_ffn_w, m_w_ff1, m_w_ff2, m_norm_final_w, v_w_in, v_conv_w, v_gdn_a_log, v_gdn_dt_bias, v_gdn_norm_w, v_hgrn_lb_logits, v_hgrn_norm_w, v_w_out, v_norm_mix_w, v_norm_ffn_w, v_w_ff1, v_w_ff2, v_norm_final_w):
    w = dict(w_in=w_in, conv_w=conv_w, gdn_a_log=gdn_a_log, gdn_dt_bias=gdn_dt_bias, gdn_norm_w=gdn_norm_w,
             hgrn_lb_logits=hgrn_lb_logits, hgrn_norm_w=hgrn_norm_w, w_out=w_out, norm_mix_w=norm_mix_w,
             norm_ffn_w=norm_ffn_w, w_ff1=w_ff1, w_ff2=w_ff2, norm_final_w=norm_final_w)
    m = dict(w_in=m_w_in, conv_w=m_conv_w, gdn_a_log=m_gdn_a_log, gdn_dt_bias=m_gdn_dt_bias,
             gdn_norm_w=m_gdn_norm_w, hgrn_lb_logits=m_hgrn_lb_logits, hgrn_norm_w=m_hgrn_norm_w,
             w_out=m_w_out, norm_mix_w=m_norm_mix_w, norm_ffn_w=m_norm_ffn_w, w_ff1=m_w_ff1, w_ff2=m_w_ff2,
             norm_final_w=m_norm_final_w)
    v = dict(w_in=v_w_in, conv_w=v_conv_w, gdn_a_log=v_gdn_a_log, gdn_dt_bias=v_gdn_dt_bias,
             gdn_norm_w=v_gdn_norm_w, hgrn_lb_logits=v_hgrn_lb_logits, hgrn_norm_w=v_hgrn_norm_w,
             w_out=v_w_out, norm_mix_w=v_norm_mix_w, norm_ffn_w=v_norm_ffn_w, w_ff1=v_w_ff1, w_ff2=v_w_ff2,
             norm_final_w=v_norm_final_w)
    d = x.shape[-1]
    h = d // (2 * HEAD_DIM)
    my_c = lax.axis_index("c").astype(jnp.int32).reshape(1)
    my_chip = (2 * lax.axis_index("x") + lax.axis_index("y")).astype(jnp.int32)

    shards = [w[n][0].astype(BF16) for n in BIG]
    conv_shard = jnp.pad(conv_w[0], ((0, 8 - CONV_W), (0, 0)))
    quarter_part, edge_part = _in_proj_quarter_part(w_in[0], my_chip, h)
    first = _gather_weights([quarter_part], [conv_shard, edge_part], "gather_in_proj", GATHER_EXCHANGE)
    n1 = _rms_fwd(x[0], norm_mix_w[0], "rms_mix")
    gathered_in, n1, *shards[1:] = lax.optimization_barrier((first[0], n1, *shards[1:]))
    own_slot = lambda st, own: lax.dynamic_update_index_in_dim(st, own, my_chip, 0)
    f_conv, f_edges = own_slot(first[1], conv_shard), own_slot(first[2], edge_part)
    w_main, w_ab = _in_proj_quarters(own_slot(gathered_in, quarter_part), f_edges, h)
    cols = lambda st: st.transpose(1, 0, 2).reshape(st.shape[1], -1)
    conv_full = cols(f_conv[:, :CONV_W])
    rest = _gather_weights(shards[1:], [], "gather_rest", GATHER_EXCHANGE, after=[w_ab])
    f_out, f_ff1, f_ff2 = (own_slot(st, own) for st, own in zip(rest, shards[1:]))

    reducer = _GradReducer(w, m, v, my_c, my_chip)
    loss, dx, g = _local_step(
        x[0], loss_target[0], w_main, w_ab, conv_full, gdn_a_log[0], gdn_dt_bias[0], gdn_norm_w[0],
        hgrn_lb_logits, hgrn_norm_w[0], f_out.reshape(-1, d), norm_mix_w[0], norm_ffn_w[0],
        f_ff1, f_ff2.reshape(-1, d), norm_final_w, reducer, n1)

    grads, delta, new_m, new_v = {}, {}, {}, {}
    for n, out in reducer.finish().items():
        grads[n], delta[n], new_m[n], new_v[n] = out

    edges = _in_proj_edge_columns(g["w_main"], g["w_ab"], h)
    small_shapes = [w[n].shape for n in SMALL] + [conv_full.shape, (1,), edges.shape]
    total = _all_reduce_small(_pack([g[n] for n in SMALL] + [g["conv_w"], loss[0, :1], edges]))
    *small_grads, conv_grad, loss_sum, edges = _unpack(total, small_shapes)
    g_in = _in_proj_shard_grad(*reducer.in_proj_halves, edges, my_c, my_chip, h)
    grads["w_in"], delta["w_in"], new_m["w_in"], new_v["w_in"] = _adamw_minor_rows(
        w["w_in"], g_in, m["w_in"], v["w_in"], "adamw_w_in")
    for n, sg in zip(SMALL, small_grads):
        grads[n] = sg
    shard_cols = conv_w.shape[-1]
    grads["conv_w"] = lax.dynamic_slice_in_dim(conv_grad, my_chip * shard_cols, shard_cols, axis=1)[None]

    packed_names = SMALL + ("conv_w",)
    packed = [_pack([t[n] for n in packed_names]) for t in (w, grads, m, v)]
    outs = _adamw(*packed, "adamw_small")
    shapes = [w[n].shape for n in packed_names]
    for res, o in zip((delta, new_m, new_v), outs):
        for n, a in zip(packed_names, _unpack(o, shapes)):
            res[n] = a

    return (loss_sum.reshape(()), dx[None], *[grads[n] for n in ORDER], *[delta[n] for n in ORDER],
            *[new_m[n] for n in ORDER], *[new_v[n] for n in ORDER])
```

```python
import functools

import jax
import jax.numpy as jnp
from jax import lax
from jax.experimental import pallas as pl
from jax.experimental.pallas import tpu as pltpu
from jax.experimental.pallas import tpu_sc as plsc

F32 = jnp.float32
BF16 = jnp.bfloat16

HEAD_DIM = 128
CHUNK = 128
SUB = 16
EXP_CAP = 80.0
NORM_EPS = 1e-6
L2_EPS = 1e-6
CONV_W = 4
VMEM_LIMIT = 56 * 1024 * 1024

ADAM_LR, ADAM_B1, ADAM_B2, ADAM_EPS, ADAM_WD, ADAM_STEP = 1e-3, 0.9, 0.999, 1e-8, 0.01, 10

NN = ((1,), (0,))
NT = ((1,), (1,))
TN = ((0,), (0,))
MESH = pl.DeviceIdType.MESH


def _dot(a, b, dims):
    return lax.dot_general(a.astype(BF16), b.astype(BF16), (dims, ((), ())),
                           preferred_element_type=F32)


def _split(a):
    hi = a.astype(BF16)
    return hi, (a - hi.astype(F32)).astype(BF16)


def _dot3(a, b, dims):
    ah, al = _split(a)
    bh, bl = _split(b)
    d = lambda x, y: lax.dot_general(x, y, (dims, ((), ())), preferred_element_type=F32)
    return d(ah, bh) + (d(ah, bl) + d(al, bh))


def _sigmoid(x):
    return 1.0 / (1.0 + jnp.exp(-x))


def _silu(x):
    return x * _sigmoid(x)


def _dsilu(x):
    s = _sigmoid(x)
    return s * (1.0 + x * (1.0 - s))


def _softplus(x):
    e = jnp.exp(-jnp.abs(x))
    u = 1.0 + e
    log1p = jnp.where(u == 1.0, e, jnp.log(u) * (e / jnp.where(u == 1.0, 1.0, u - 1.0)))
    return jnp.maximum(x, 0.0) + log1p


def _iota(shape, axis):
    return lax.broadcasted_iota(jnp.int32, shape, axis)


def _cumsum_rows(x):
    n = x.shape[0]
    row = _iota(x.shape, 0)
    s = 1
    while s < n:
        x = x + jnp.where(row >= s, pltpu.roll(x, s, 0), 0.0)
        s *= 2
    return x


def _rev_cumsum_rows(x):
    return jnp.sum(x, axis=0, keepdims=True) - _cumsum_rows(x) + x


def _params(sem):
    return pltpu.CompilerParams(dimension_semantics=sem, vmem_limit_bytes=VMEM_LIMIT)


ROW_TILE = 8
HEADS_PER_STEP = 8


def _hps(h):
    return min(HEADS_PER_STEP, h)


def _head_view(ref, hb):
    if len(ref.shape) == 2:
        return ref.at[:, pl.ds(hb * HEAD_DIM, HEAD_DIM)]
    return ref.at[hb]


class _Staged:
    def __init__(self, ref, load):
        self.ref = ref
        self.loaded = ref[...] if load else None
        self.written = None

    def __getitem__(self, idx):
        return self.loaded

    def __setitem__(self, idx, value):
        self.written = value


def _each_head(one_head, n_in):
    def body(*refs):
        @pl.when(pl.program_id(1) == 0)
        def _():
            refs[-1][...] = jnp.zeros_like(refs[-1])

        last = len(refs) - 1
        staged = [[_Staged(_head_view(r, hb), i < n_in or i == last) for i, r in enumerate(refs)]
                  for hb in range(refs[-1].shape[0])]
        running = [one_head(*per_head) for per_head in staged]
        while running:
            for gen in list(running):
                try:
                    next(gen)
                except StopIteration:
                    running.remove(gen)
        for per_head in staged:
            for s in per_head:
                if s.written is not None:
                    s.ref[...] = s.written
    return body


def _tile(n, want):
    t = min(n, want)
    while n % t:
        t //= 2
    return t


def _mm(a, b, mode, out_dtypes, name, epi=None, extras=(), tm=1024, tn=1024, tk=2048,
        b_stacked=False, out_stacked=False):
    if mode == "tn":
        kdim, m = a.shape
    else:
        m, kdim = a.shape
    if b_stacked:
        n = N_CHIPS * b.shape[2] if mode == "nn" else b.shape[1]
        kdim_b = b.shape[1] if mode == "nn" else N_CHIPS * b.shape[2]
        assert kdim_b == kdim
    else:
        n = b.shape[0] if mode == "nt" else b.shape[1]
    per_shard = (n if (mode == "nn" or out_stacked) else kdim) // N_CHIPS
    tm, tn, tk = _tile(m, tm), _tile(n, tn), _tile(kdim, tk)
    if (b_stacked and mode == "nn") or out_stacked:
        tn = _tile(per_shard, tn)
    if b_stacked and mode == "nt":
        tk = _tile(per_shard, tk)
    nk = kdim // tk
    dims = {"nn": NN, "nt": NT, "tn": TN}[mode]
    a_spec = (pl.BlockSpec((tk, tm), lambda i, j, k: (k, i)) if mode == "tn"
              else pl.BlockSpec((tm, tk), lambda i, j, k: (i, k)))
    if b_stacked and mode == "nn":
        per = per_shard // tn
        b_spec = pl.BlockSpec((None, tk, tn), lambda i, j, k: (j // per, k, j % per))
    elif b_stacked:
        per = per_shard // tk
        b_spec = pl.BlockSpec((None, tn, tk), lambda i, j, k: (k // per, j, k % per))
    else:
        b_spec = (pl.BlockSpec((tn, tk), lambda i, j, k: (j, k)) if mode == "nt"
                  else pl.BlockSpec((tk, tn), lambda i, j, k: (k, j)))
    mn_spec = pl.BlockSpec((tm, tn), lambda i, j, k: (i, j))
    if out_stacked:
        per_o = per_shard // tn
        out_spec = pl.BlockSpec((None, tm, tn), lambda i, j, k: (j // per_o, i, j % per_o))
        out_shape = (N_CHIPS, m, per_shard)
    else:
        out_spec, out_shape = mn_spec, (m, n)
    ne, no = len(extras), len(out_dtypes)
    if epi is None:
        epi = lambda acc: (acc,)

    def body(a_ref, b_ref, *rest):
        extra_refs, out_refs = rest[:ne], rest[ne:ne + no]
        part = _dot(a_ref[...], b_ref[...], dims)

        def finish(total):
            outs = epi(total, *[r[...] for r in extra_refs])
            for o_ref, o in zip(out_refs, outs):
                o_ref[...] = o.astype(o_ref.dtype)

        if nk == 1:
            finish(part)
            return
        acc = rest[-1]
        k = pl.program_id(2)

        @pl.when(k == 0)
        def _():
            acc[...] = part

        @pl.when(jnp.logical_and(k > 0, k < nk - 1))
        def _():
            acc[...] += part

        @pl.when(k == nk - 1)
        def _():
            finish(acc[...] + part)

    outs = pl.pallas_call(
        body, name=name,
        out_shape=tuple(jax.ShapeDtypeStruct(out_shape, d) for d in out_dtypes),
        grid=(m // tm, n // tn, nk),
        in_specs=[a_spec, b_spec] + [mn_spec] * ne,
        out_specs=tuple(out_spec for _ in out_dtypes),
        scratch_shapes=[pltpu.VMEM((tm, tn), F32)] if nk > 1 else [],
        compiler_params=_params(("parallel", "parallel", "arbitrary")),
    )(a, b, *extras)
    return outs if no > 1 else outs[0]


ROWS = 256


def _rms_fwd(x, w, name):
    t, d = x.shape
    tr = _tile(t, ROWS)

    def body(x_ref, w_ref, n_ref):
        xv = x_ref[...]
        r = lax.rsqrt(jnp.mean(xv * xv, axis=-1, keepdims=True) + NORM_EPS)
        n_ref[...] = (xv * r * w_ref[...]).astype(n_ref.dtype)

    return pl.pallas_call(
        body, name=name, out_shape=jax.ShapeDtypeStruct((t, d), BF16), grid=(t // tr,),
        in_specs=[pl.BlockSpec((tr, d), lambda i: (i, 0)), pl.BlockSpec((1, d), lambda i: (0, 0))],
        out_specs=pl.BlockSpec((tr, d), lambda i: (i, 0)),
        compiler_params=_params(("parallel",)),
    )(x, w.reshape(1, d))


def _rms_bwd(dn, x, w, dres, name):
    t, d = x.shape
    tr = _tile(t, ROWS)

    def body(dn_ref, x_ref, w_ref, dres_ref, dx_ref, dxb_ref, dw_ref):
        i = pl.program_id(0)
        xv, dnv = x_ref[...], dn_ref[...]
        r = lax.rsqrt(jnp.mean(xv * xv, axis=-1, keepdims=True) + NORM_EPS)
        xh = xv * r
        dxh = dnv * w_ref[...]
        dx = dres_ref[...] + r * (dxh - xh * jnp.mean(dxh * xh, axis=-1, keepdims=True))
        dx_ref[...] = dx
        dxb_ref[...] = dx.astype(BF16)

        @pl.when(i == 0)
        def _():
            dw_ref[...] = jnp.zeros_like(dw_ref)

        dw_ref[...] += jnp.sum(dnv * xh, axis=0, keepdims=True)

    row = pl.BlockSpec((tr, d), lambda i: (i, 0))
    vec = pl.BlockSpec((1, d), lambda i: (0, 0))
    return pl.pallas_call(
        body, name=name,
        out_shape=(jax.ShapeDtypeStruct((t, d), F32), jax.ShapeDtypeStruct((t, d), BF16),
                   jax.ShapeDtypeStruct((1, d), F32)),
        grid=(t // tr,), in_specs=[row, row, vec, row], out_specs=(row, row, vec),
        compiler_params=_params(("arbitrary",)),
    )(dn, x, w.reshape(1, d), dres)


def _loss_head(h, w, target):
    t, d = h.shape
    tr = _tile(t, ROWS)

    def body(h_ref, w_ref, t_ref, loss_ref, dh_ref, dhb_ref, dw_ref):
        i = pl.program_id(0)
        hv, wv = h_ref[...], w_ref[...]
        r = lax.rsqrt(jnp.mean(hv * hv, axis=-1, keepdims=True) + NORM_EPS)
        hh = hv * r
        err = hh * wv - t_ref[...]
        dout = err * (1.0 / d)
        dhh = dout * wv
        dh = r * (dhh - hh * jnp.mean(dhh * hh, axis=-1, keepdims=True))
        dh_ref[...] = dh
        dhb_ref[...] = dh.astype(BF16)

        @pl.when(i == 0)
        def _():
            dw_ref[...] = jnp.zeros_like(dw_ref)
            loss_ref[...] = jnp.zeros_like(loss_ref)

        dw_ref[...] += jnp.sum(dout * hh, axis=0, keepdims=True)
        loss_ref[...] += jnp.full((1, 128), 0.5 / d, F32) * jnp.sum(err * err)

    row = pl.BlockSpec((tr, d), lambda i: (i, 0))
    vec = pl.BlockSpec((1, d), lambda i: (0, 0))
    lspec = pl.BlockSpec((1, 128), lambda i: (0, 0))
    return pl.pallas_call(
        body, name="loss_head",
        out_shape=(jax.ShapeDtypeStruct((1, 128), F32), jax.ShapeDtypeStruct((t, d), F32),
                   jax.ShapeDtypeStruct((t, d), BF16), jax.ShapeDtypeStruct((1, d), F32)),
        grid=(t // tr,), in_specs=[row, vec, row], out_specs=(lspec, row, row, vec),
        compiler_params=_params(("arbitrary",)),
    )(h, w.reshape(1, d), target)


def _inv_unit_lower(a):
    c = a.shape[0]
    eye = (_iota((c, c), 0) == _iota((c, c), 1)).astype(F32)
    x = eye - a
    p = _dot3(a, a, NN)
    yield
    n = 2
    while n < c:
        x = x + _dot3(x, p, NN)
        n *= 2
        if n < c:
            p = _dot3(p, p, NN)
        yield
    return x


def _gdn_chunk(q, k, v, beta, g):
    c = q.shape[0]
    row, col = _iota((c, c), 0), _iota((c, c), 1)
    gc = _cumsum_rows(g)
    diff = gc - gc.T
    dec = jnp.where(row >= col, jnp.exp(jnp.minimum(diff, 0.0)), 0.0)
    dec_s = jnp.where(row > col, dec, 0.0)
    gam = jnp.exp(gc)
    g_last = jnp.sum(g, axis=0, keepdims=True)
    kk = _dot(k, k, NT)
    a = beta * kk * dec_s
    p = _dot(q, k, NT) * dec
    e_end = jnp.exp(g_last - gc)
    return dict(dec=dec, dec_s=dec_s, gam=gam, gam_last=jnp.exp(g_last), e_end=e_end,
                k_end=k * e_end, kk=kk, a=a, p=p)


def _gdn_fwd(q, k, v, beta_bc, g_bc):
    t = q.shape[0]
    h = q.shape[1] // HEAD_DIM
    nc = t // CHUNK

    def body(q_ref, k_ref, v_ref, b_ref, g_ref, o_ref, s_ref, t_ref, state):
        qv, kv, vv, beta = q_ref[...], k_ref[...], v_ref[...], b_ref[...]
        ch = _gdn_chunk(qv, kv, vv, beta, g_ref[...])
        yield
        tm = yield from _inv_unit_lower(ch["a"])
        sol = _dot(tm, jnp.concatenate([beta * vv, beta * ch["gam"] * kv], axis=1), NN)
        yield
        u_v, w = sol[:, :HEAD_DIM], sol[:, HEAD_DIM:]
        s0 = state[...]
        u = u_v - _dot(w, s0, NN)
        yield
        o_ref[...] = _dot(qv * ch["gam"], s0, NN) + _dot(ch["p"], u, NN)
        s_ref[...] = s0
        t_ref[...] = tm
        state[...] = ch["gam_last"] * s0 + _dot(ch["k_end"], u, TN)

    tok = pl.BlockSpec((CHUNK, _hps(h) * HEAD_DIM), lambda hh, c: (c, hh))
    bc = pl.BlockSpec((_hps(h), CHUNK, HEAD_DIM), lambda hh, c: (hh, c, 0))
    mat = pl.BlockSpec((_hps(h), None, HEAD_DIM, HEAD_DIM), lambda hh, c: (hh, c, 0, 0))
    return pl.pallas_call(
        _each_head(body, 5), name="gdn_fwd",
        out_shape=(jax.ShapeDtypeStruct(q.shape, F32),
                   jax.ShapeDtypeStruct((h, nc, HEAD_DIM, HEAD_DIM), F32),
                   jax.ShapeDtypeStruct((h, nc, CHUNK, CHUNK), F32)),
        grid=(h // _hps(h), nc), in_specs=[tok, tok, tok, bc, bc], out_specs=(tok, mat, mat),
        scratch_shapes=[pltpu.VMEM((_hps(h), HEAD_DIM, HEAD_DIM), F32)],
        compiler_params=_params(("parallel", "arbitrary")),
    )(q, k, v, beta_bc, g_bc)


def _gdn_bwd(q, k, v, beta_bc, g_bc, states, invs, do, do_blk=0):
    t = q.shape[0]
    h = q.shape[1] // HEAD_DIM
    nc = t // CHUNK

    def body(q_ref, k_ref, v_ref, b_ref, g_ref, s_ref, t_ref, do_ref,
             dq_ref, dk_ref, dv_ref, db_ref, dg_ref, dstate):
        qv, kv, vv, beta = q_ref[...], k_ref[...], v_ref[...], b_ref[...]
        dov, s0, tm, ds1 = do_ref[...], s_ref[...], t_ref[...], dstate[...]
        ch = _gdn_chunk(qv, kv, vv, beta, g_ref[...])
        yield
        gam, dec, dec_s, kk = ch["gam"], ch["dec"], ch["dec_s"], ch["kk"]
        r_v, r_w = beta * vv, beta * gam * kv
        sol = _dot(tm, jnp.concatenate([r_v, r_w], axis=1), NN)
        yield
        u_v, w = sol[:, :HEAD_DIM], sol[:, HEAD_DIM:]
        u = u_v - _dot(w, s0, NN)
        qg = qv * gam
        yield

        du = _dot(ch["p"], dov, TN) + _dot(ch["k_end"], ds1, NN)
        dp = _dot(dov, u, NT)
        dpd = dp * dec
        dqg = _dot(dov, s0, NT)
        dk_end = _dot(u, ds1, NT)
        yield
        dq = dqg * gam + _dot(dpd, kv, NN)
        dk = _dot(dpd, qv, TN) + dk_end * ch["e_end"]
        dstate[...] = _dot(qg, dov, TN) + ch["gam_last"] * ds1 - _dot(w, du, TN)
        dw = -_dot(du, s0, NT)
        yield
        dr = _dot(tm, jnp.concatenate([du, dw], axis=1), TN)
        yield
        dr_v, dr_w = dr[:, :HEAD_DIM], dr[:, HEAD_DIM:]
        da = -_dot(dr, sol, NT)
        yield
        dkk = da * beta * dec_s
        dk = dk + _dot(dkk, kv, NN) + _dot(dkk, kv, TN) + beta * gam * dr_w
        dbeta = (jnp.sum(da * kk * dec_s, axis=1, keepdims=True)
                 + jnp.sum(dr_v * vv + dr_w * gam * kv, axis=1, keepdims=True))

        pair = dp * ch["p"] + da * ch["a"]
        end = jnp.sum(dk_end * ch["k_end"], axis=1, keepdims=True)
        dgc = (jnp.sum(pair - pair.T, axis=1, keepdims=True)
               + jnp.sum(dqg * qg + dr_w * r_w, axis=1, keepdims=True) - end)
        at_end = jnp.sum(end) + ch["gam_last"] * jnp.sum(s0 * ds1)
        dgc = jnp.broadcast_to(dgc, (CHUNK, HEAD_DIM))
        dgc = dgc + jnp.where(_iota((CHUNK, HEAD_DIM), 0) == CHUNK - 1, at_end, 0.0)
        dq_ref[...] = dq
        dk_ref[...] = dk
        dv_ref[...] = beta * dr_v
        db_ref[...] = jnp.broadcast_to(dbeta, (CHUNK, HEAD_DIM)).T[:ROW_TILE]
        dg_ref[...] = _rev_cumsum_rows(dgc).T[:ROW_TILE]

    rev = lambda c: nc - 1 - c
    tok = pl.BlockSpec((CHUNK, _hps(h) * HEAD_DIM), lambda hh, c: (rev(c), hh))
    bc = pl.BlockSpec((_hps(h), CHUNK, HEAD_DIM), lambda hh, c: (hh, rev(c), 0))
    mat = pl.BlockSpec((_hps(h), None, HEAD_DIM, HEAD_DIM), lambda hh, c: (hh, rev(c), 0, 0))
    tok_shape = jax.ShapeDtypeStruct(q.shape, F32)
    row_shape = jax.ShapeDtypeStruct((h, nc, ROW_TILE, CHUNK), F32)
    rows = pl.BlockSpec((_hps(h), None, ROW_TILE, CHUNK), lambda hh, c: (hh, rev(c), 0, 0))
    return pl.pallas_call(
        _each_head(body, 8), name="gdn_bwd",
        out_shape=(tok_shape, tok_shape, tok_shape, row_shape, row_shape),
        grid=(h // _hps(h), nc),
        in_specs=[tok, tok, tok, bc, bc, mat, mat,
                  pl.BlockSpec((CHUNK, _hps(h) * HEAD_DIM), lambda hh, c: (rev(c), do_blk // _hps(h) + hh))],
        out_specs=(tok, tok, tok, rows, rows),
        scratch_shapes=[pltpu.VMEM((_hps(h), HEAD_DIM, HEAD_DIM), F32)],
        compiler_params=_params(("parallel", "arbitrary")),
    )(q, k, v, beta_bc, g_bc, states, invs, do)


def _hgrn_chunk(q, k, lf):
    c = q.shape[0]
    row = _iota((c, HEAD_DIM), 0)
    b = _cumsum_rows(lf)
    q_subs, k_facs, a_rows = [], [], []
    for x in range(c // SUB):
        b_start = jnp.sum(jnp.where(row < x * SUB, lf, 0.0), axis=0, keepdims=True)
        q_x = (q * jnp.exp(jnp.minimum(b - b_start, 0.0)))[x * SUB:(x + 1) * SUB]
        k_fac = jnp.where(row < (x + 1) * SUB, jnp.exp(jnp.minimum(b_start - b, EXP_CAP)), 0.0)
        q_subs.append(q_x)
        k_facs.append(k_fac)
        a_rows.append(_dot(q_x, k * k_fac, NT))
    a = jnp.concatenate(a_rows, axis=0)
    a = jnp.where(_iota((c, c), 0) >= _iota((c, c), 1), a, 0.0)
    b_last = jnp.sum(lf, axis=0, keepdims=True)
    return dict(b=b, a=a, q_subs=q_subs, k_facs=k_facs, e_b=jnp.exp(b),
                e_end=jnp.exp(b_last - b), e_last=jnp.exp(b_last))


def _hgrn_fwd(q, k, v, lf, v_blk=0):
    t = q.shape[0]
    h = q.shape[1] // HEAD_DIM
    nc = t // CHUNK

    def body(q_ref, k_ref, v_ref, lf_ref, o_ref, s_ref, state):
        qv, kv, vv = q_ref[...], k_ref[...], v_ref[...]
        ch = _hgrn_chunk(qv, kv, lf_ref[...])
        yield
        s0 = state[...]
        o_ref[...] = _dot(qv * ch["e_b"], s0, NT) + _dot(ch["a"], vv, NN)
        s_ref[...] = s0
        state[...] = s0 * ch["e_last"] + _dot(vv, kv * ch["e_end"], TN)

    tok = pl.BlockSpec((CHUNK, _hps(h) * HEAD_DIM), lambda hh, c: (c, hh))
    mat = pl.BlockSpec((_hps(h), None, HEAD_DIM, HEAD_DIM), lambda hh, c: (hh, c, 0, 0))
    return pl.pallas_call(
        _each_head(body, 4), name="hgrn_fwd",
        out_shape=(jax.ShapeDtypeStruct(q.shape, F32),
                   jax.ShapeDtypeStruct((h, nc, HEAD_DIM, HEAD_DIM), F32)),
        grid=(h // _hps(h), nc),
        in_specs=[tok, tok, pl.BlockSpec((CHUNK, _hps(h) * HEAD_DIM), lambda hh, c: (c, v_blk // _hps(h) + hh)), tok],
        out_specs=(tok, mat),
        scratch_shapes=[pltpu.VMEM((_hps(h), HEAD_DIM, HEAD_DIM), F32)],
        compiler_params=_params(("parallel", "arbitrary")),
    )(q, k, v, lf)


def _hgrn_bwd(q, k, v, lf, states, do, v_blk=0, do_blk=0):
    t = q.shape[0]
    nc = t // CHUNK
    h = q.shape[1] // HEAD_DIM

    def body(q_ref, k_ref, v_ref, lf_ref, s_ref, do_ref, dq_ref, dk_ref, dv_ref, dlf_ref, dstate):
        qv, kv, vv, dov, s0 = q_ref[...], k_ref[...], v_ref[...], do_ref[...], s_ref[...]
        ds1 = dstate[...]
        ch = _hgrn_chunk(qv, kv, lf_ref[...])
        yield
        c = CHUNK
        row = _iota((c, HEAD_DIM), 0)
        qh = qv * ch["e_b"]
        k_end = kv * ch["e_end"]
        da = jnp.where(_iota((c, c), 0) >= _iota((c, c), 1), _dot(dov, vv, NT), 0.0)
        dqh = _dot(dov, s0, NN)
        dk_end = _dot(vv, ds1, NN)
        yield
        end = dk_end * k_end
        dk = dk_end * ch["e_end"]
        db = dqh * qh - end + jnp.where(
            row == c - 1, jnp.sum(end + s0 * ch["e_last"] * ds1, axis=0, keepdims=True), 0.0)
        dq_rows, qdq_rows = [], []
        for x in range(c // SUB):
            da_x = da[x * SUB:(x + 1) * SUB]
            k_x = kv * ch["k_facs"][x]
            dq_x = _dot(da_x, k_x, NN)
            dk_x = _dot(da_x, ch["q_subs"][x], TN)
            dq_rows.append(dq_x)
            qdq_rows.append(dq_x * ch["q_subs"][x])
            dk = dk + dk_x * ch["k_facs"][x]
            kdk = dk_x * k_x
            db = db - kdk
            if x > 0:
                at_start = jnp.sum(kdk, axis=0, keepdims=True) - jnp.sum(qdq_rows[x], axis=0, keepdims=True)
                db = db + jnp.where(row == x * SUB - 1, at_start, 0.0)
        yield
        b_start = jnp.zeros((c, HEAD_DIM), F32)
        for x in range(1, c // SUB):
            b_x = jnp.sum(jnp.where(row < x * SUB, lf_ref[...], 0.0), axis=0, keepdims=True)
            b_start = jnp.where(row >= x * SUB, b_x, b_start)
        dq = dqh * ch["e_b"] + jnp.concatenate(dq_rows, axis=0) * jnp.exp(jnp.minimum(ch["b"] - b_start, 0.0))
        db = db + jnp.concatenate(qdq_rows, axis=0)
        dstate[...] = _dot(dov, qh, TN) + ds1 * ch["e_last"]
        dq_ref[...] = dq
        dk_ref[...] = dk
        dv_ref[...] = _dot(ch["a"], dov, TN) + _dot(k_end, ds1, NT)
        dlf_ref[...] = _rev_cumsum_rows(db)

    rev = lambda c: nc - 1 - c
    tok = pl.BlockSpec((CHUNK, _hps(h) * HEAD_DIM), lambda hh, c: (rev(c), hh))
    mat = pl.BlockSpec((_hps(h), None, HEAD_DIM, HEAD_DIM), lambda hh, c: (hh, rev(c), 0, 0))
    tok_shape = jax.ShapeDtypeStruct(q.shape, F32)
    return pl.pallas_call(
        _each_head(body, 6), name="hgrn_bwd",
        out_shape=(tok_shape, tok_shape, tok_shape, tok_shape),
        grid=(h // _hps(h), nc),
        in_specs=[tok, tok, pl.BlockSpec((CHUNK, _hps(h) * HEAD_DIM), lambda hh, c: (rev(c), v_blk // _hps(h) + hh)), tok, mat,
                  pl.BlockSpec((CHUNK, _hps(h) * HEAD_DIM), lambda hh, c: (rev(c), do_blk // _hps(h) + hh))],
        out_specs=(tok, tok, tok, tok),
        scratch_shapes=[pltpu.VMEM((_hps(h), HEAD_DIM, HEAD_DIM), F32)],
        compiler_params=_params(("parallel", "arbitrary")),
    )(q, k, v, lf, states, do)


CONV_ROWS = 256
HALO = 8


def _shift_down(cur, prev, s):
    rt = cur.shape[0]
    head = jnp.concatenate([pltpu.roll(prev, s, 0), jnp.zeros((rt - HALO, cur.shape[1]), F32)], axis=0)
    return jnp.where(_iota(cur.shape, 0) < s, head, pltpu.roll(cur, s, 0))


def _shift_up(cur, nxt, s):
    rt = cur.shape[0]
    tail = jnp.concatenate([jnp.zeros((rt - HALO, cur.shape[1]), F32), pltpu.roll(nxt, HALO - s, 0)], axis=0)
    return jnp.where(_iota(cur.shape, 0) >= rt - s, tail, pltpu.roll(cur, rt - s, 0))


def _tile_with_prev(ref, i, rt):
    r0 = pl.multiple_of(i * rt, rt)
    cur = ref[pl.ds(r0, rt), :]
    prev = ref[pl.ds(pl.multiple_of(jnp.maximum(r0 - HALO, 0), HALO), HALO), :]
    return cur, jnp.where(i > 0, prev, 0.0)


def _tile_with_next(ref, i, rt, n_tiles):
    r0 = pl.multiple_of(i * rt, rt)
    cur = ref[pl.ds(r0, rt), :]
    nxt = ref[pl.ds(pl.multiple_of(jnp.minimum(r0 + rt, (n_tiles - 1) * rt), HALO), HALO), :]
    return cur, jnp.where(i < n_tiles - 1, nxt, 0.0)


def _conv_tile(x_ref, w_ref, i, rt):
    cur, prev = _tile_with_prev(x_ref, i, rt)
    shifted = [_shift_down(cur, prev, CONV_W - 1 - j) for j in range(CONV_W - 1)] + [cur]
    c = shifted[0] * w_ref[pl.ds(0, 1), :]
    for j in range(1, CONV_W):
        c = c + shifted[j] * w_ref[pl.ds(j, 1), :]
    return c, shifted


def _l2n(s):
    return s * lax.rsqrt(jnp.sum(s * s, axis=-1, keepdims=True) + L2_EPS)


def _gdn_prep_fwd(proj, conv_w, h):
    t = proj.shape[0]
    rt = _tile(t, CONV_ROWS)
    nt = t // rt
    scale = HEAD_DIM ** -0.5

    def body(xq, xk, xv, wq, wk, wv, q_ref, k_ref, v_ref):
        def tile(i, carry):
            rows = pl.ds(pl.multiple_of(i * rt, rt), rt)
            q_ref[rows, :] = _l2n(_silu(_conv_tile(xq, wq, i, rt)[0])) * scale
            k_ref[rows, :] = _l2n(_silu(_conv_tile(xk, wk, i, rt)[0]))
            v_ref[rows, :] = _silu(_conv_tile(xv, wv, i, rt)[0])
            return carry

        lax.fori_loop(0, nt, tile, 0)

    col = lambda p: pl.BlockSpec((t, HEAD_DIM), lambda hh: (0, p * h + hh))
    wcol = lambda p: pl.BlockSpec((CONV_W, HEAD_DIM), lambda hh: (0, p * h + hh))
    out = pl.BlockSpec((t, HEAD_DIM), lambda hh: (0, hh))
    shape = jax.ShapeDtypeStruct((t, h * HEAD_DIM), F32)
    return pl.pallas_call(
        body, name="gdn_prep_fwd", out_shape=(shape, shape, shape), grid=(h,),
        in_specs=[col(0), col(1), col(2), wcol(0), wcol(1), wcol(2)], out_specs=(out, out, out),
        compiler_params=_params(("parallel",)),
    )(proj, proj, proj, conv_w, conv_w, conv_w)


def _gdn_prep_bwd(proj, conv_w, dq, dk, dv, h):
    t = proj.shape[0]
    rt = _tile(t, CONV_ROWS)
    nt = t // rt
    scale = HEAD_DIM ** -0.5

    def part(x_ref, w_ref, dy_ref, dx_ref, dw_ref, dc_ref, norm_scale):
        def first(i, dws):
            rows = pl.ds(pl.multiple_of(i * rt, rt), rt)
            c, shifted = _conv_tile(x_ref, w_ref, i, rt)
            ds = dy_ref[rows, :]
            s, ds_dc = _silu_and_grad(c)
            if norm_scale is not None:
                r = lax.rsqrt(jnp.sum(s * s, axis=-1, keepdims=True) + L2_EPS)
                y = s * r
                dyn = ds * norm_scale
                ds = r * (dyn - y * jnp.sum(dyn * y, axis=-1, keepdims=True))
            dc = ds * ds_dc
            dc_ref[rows, :] = dc
            return tuple(dws[j] + jnp.sum(dc * shifted[j], axis=0, keepdims=True) for j in range(CONV_W))

        dws = lax.fori_loop(0, nt, first, tuple(jnp.zeros((1, HEAD_DIM), F32) for _ in range(CONV_W)))
        for j in range(CONV_W):
            dw_ref[pl.ds(j, 1), :] = dws[j]

        def second(i, carry):
            rows = pl.ds(pl.multiple_of(i * rt, rt), rt)
            cur, nxt = _tile_with_next(dc_ref, i, rt, nt)
            dx = cur * w_ref[pl.ds(CONV_W - 1, 1), :]
            for j in range(CONV_W - 1):
                dx = dx + _shift_up(cur, nxt, CONV_W - 1 - j) * w_ref[pl.ds(j, 1), :]
            dx_ref[rows, :] = dx.astype(dx_ref.dtype)
            return carry

        lax.fori_loop(0, nt, second, 0)

    def body(xq, xk, xv, wq, wk, wv, dq_ref, dk_ref, dv_ref, dxq, dxk, dxv, dwq, dwk, dwv, dc_ref):
        part(xq, wq, dq_ref, dxq, dwq, dc_ref, scale)
        part(xk, wk, dk_ref, dxk, dwk, dc_ref, 1.0)
        part(xv, wv, dv_ref, dxv, dwv, dc_ref, None)

    col = lambda p: pl.BlockSpec((t, HEAD_DIM), lambda hh: (0, p * h + hh))
    wcol = lambda p: pl.BlockSpec((CONV_W, HEAD_DIM), lambda hh: (0, p * h + hh))
    own = pl.BlockSpec((t, HEAD_DIM), lambda hh: (0, hh))
    wown = pl.BlockSpec((CONV_W, HEAD_DIM), lambda hh: (0, hh))
    dx_shape = jax.ShapeDtypeStruct((t, h * HEAD_DIM), BF16)
    dw_shape = jax.ShapeDtypeStruct((CONV_W, h * HEAD_DIM), F32)
    return pl.pallas_call(
        body, name="gdn_prep_bwd",
        out_shape=(dx_shape, dx_shape, dx_shape, dw_shape, dw_shape, dw_shape), grid=(h,),
        in_specs=[col(0), col(1), col(2), wcol(0), wcol(1), wcol(2), own, own, own],
        out_specs=(own, own, own, wown, wown, wown),
        scratch_shapes=[pltpu.VMEM((t, HEAD_DIM), F32)],
        compiler_params=_params(("parallel",)),
    )(proj, proj, proj, conv_w, conv_w, conv_w, dq, dk, dv)


def _gdn_gates_fwd(ab, a_log_row, dt_bias_row):
    t = ab.shape[0]
    tr = _tile(t, 512)

    def body(ab_ref, al_ref, dt_ref, g_ref, b_ref):
        g_ref[...] = -jnp.exp(al_ref[...]) * _softplus(ab_ref[:, :HEAD_DIM] + dt_ref[...])
        b_ref[...] = _sigmoid(ab_ref[:, HEAD_DIM:])

    row = pl.BlockSpec((tr, HEAD_DIM), lambda i: (i, 0))
    vec = pl.BlockSpec((1, HEAD_DIM), lambda i: (0, 0))
    shape = jax.ShapeDtypeStruct((t, HEAD_DIM), F32)
    return pl.pallas_call(
        body, name="gdn_gates_fwd", out_shape=(shape, shape), grid=(t // tr,),
        in_specs=[pl.BlockSpec((tr, 2 * HEAD_DIM), lambda i: (i, 0)), vec, vec], out_specs=(row, row),
        compiler_params=_params(("parallel",)),
    )(ab, a_log_row, dt_bias_row)


def _gdn_gates_bwd(ab, a_log_row, dt_bias_row, dg, dbeta):
    t = ab.shape[0]
    tr = _tile(t, 512)

    def body(ab_ref, al_ref, dt_ref, dg_ref, db_ref, dab_ref, dal_ref, ddt_ref):
        @pl.when(pl.program_id(0) == 0)
        def _():
            dal_ref[...] = jnp.zeros_like(dal_ref)
            ddt_ref[...] = jnp.zeros_like(ddt_ref)

        xa = ab_ref[:, :HEAD_DIM] + dt_ref[...]
        neg_a = -jnp.exp(al_ref[...])
        dgv = dg_ref[...]
        da = dgv * neg_a * _sigmoid(xa)
        beta = _sigmoid(ab_ref[:, HEAD_DIM:])
        dab_ref[:, :HEAD_DIM] = da.astype(BF16)
        dab_ref[:, HEAD_DIM:] = (db_ref[...] * beta * (1.0 - beta)).astype(BF16)
        dal_ref[...] += jnp.sum(dgv * neg_a * _softplus(xa), axis=0, keepdims=True)
        ddt_ref[...] += jnp.sum(da, axis=0, keepdims=True)

    row = pl.BlockSpec((tr, HEAD_DIM), lambda i: (i, 0))
    row2 = pl.BlockSpec((tr, 2 * HEAD_DIM), lambda i: (i, 0))
    vec = pl.BlockSpec((1, HEAD_DIM), lambda i: (0, 0))
    vshape = jax.ShapeDtypeStruct((1, HEAD_DIM), F32)
    return pl.pallas_call(
        body, name="gdn_gates_bwd",
        out_shape=(jax.ShapeDtypeStruct((t, 2 * HEAD_DIM), BF16), vshape, vshape), grid=(t // tr,),
        in_specs=[row2, vec, vec, row, row], out_specs=(row2, vec, vec),
        compiler_params=_params(("arbitrary",)),
    )(ab, a_log_row, dt_bias_row, dg, dbeta)


def _lower_bound(lb_ref):
    return _sigmoid(lb_ref[pl.ds(0, 1), :] - lb_ref[pl.ds(1, 1), :])


def _hgrn_prep_fwd(proj, lb_logits, h, q_blk, f_blk):
    t = proj.shape[0]
    tr = _tile(t, 512)

    def body(xq, xf, lb_ref, q_ref, k_ref, lf_ref):
        lb = _lower_bound(lb_ref)
        s = _sigmoid(xf[...])
        q_ref[...] = _silu(xq[...])
        k_ref[...] = (1.0 - lb) * (1.0 - s)
        lf_ref[...] = jnp.log(lb + (1.0 - lb) * s)

    width = h * HEAD_DIM
    col = lambda b0: pl.BlockSpec((tr, width), lambda i: (i, b0 // h))
    own = pl.BlockSpec((tr, width), lambda i: (i, 0))
    shape = jax.ShapeDtypeStruct((t, width), F32)
    return pl.pallas_call(
        body, name="hgrn_prep_fwd", out_shape=(shape, shape, shape), grid=(t // tr,),
        in_specs=[col(q_blk), col(f_blk), pl.BlockSpec((2, width), lambda i: (0, 0))],
        out_specs=(own, own, own), compiler_params=_params(("parallel",)),
    )(proj, proj, lb_logits)


def _hgrn_prep_bwd(proj, lb_logits, dq, dk, dlf, h, q_blk, f_blk):
    t = proj.shape[0]
    tr = _tile(t, 512)

    def body(xq, xf, lb_ref, dq_ref, dk_ref, dlf_ref, dxq, dxf, dlb_ref):
        @pl.when(pl.program_id(0) == 0)
        def _():
            dlb_ref[...] = jnp.zeros_like(dlb_ref)

        lb = _lower_bound(lb_ref)
        s = _sigmoid(xf[...])
        e = dlf_ref[...] / (lb + (1.0 - lb) * s) - dk_ref[...]
        dxq[...] = (dq_ref[...] * _dsilu(xq[...])).astype(BF16)
        dxf[...] = (s * (1.0 - s) * (1.0 - lb) * e).astype(BF16)
        d0 = jnp.sum((1.0 - s) * e, axis=0, keepdims=True) * (lb * (1.0 - lb))
        dlb_ref[pl.ds(0, 1), :] += d0
        dlb_ref[pl.ds(1, 1), :] += -d0

    width = h * HEAD_DIM
    col = lambda b0: pl.BlockSpec((tr, width), lambda i: (i, b0 // h))
    own = pl.BlockSpec((tr, width), lambda i: (i, 0))
    lbs = pl.BlockSpec((2, width), lambda i: (0, 0))
    shape = jax.ShapeDtypeStruct((t, width), BF16)
    return pl.pallas_call(
        body, name="hgrn_prep_bwd",
        out_shape=(shape, shape, jax.ShapeDtypeStruct((2, width), F32)), grid=(t // tr,),
        in_specs=[col(q_blk), col(f_blk), lbs, own, own, own], out_specs=(own, own, lbs),
        compiler_params=_params(("arbitrary",)),
    )(proj, proj, lb_logits, dq, dk, dlf)


GATE_HEADS = 4


def _gate_specs(h, z_blk, g_blk, tr):
    g = min(GATE_HEADS, h)
    n = h // g
    width = g * HEAD_DIM
    o_a = pl.BlockSpec((tr, width), lambda gg, i: (i, jnp.minimum(gg, n - 1)))
    o_b = pl.BlockSpec((tr, width), lambda gg, i: (i, jnp.maximum(gg - n, 0)))
    gate = pl.BlockSpec((tr, width), lambda gg, i: (i, jnp.where(gg < n, z_blk // g + gg, g_blk // g + gg - n)))
    w = pl.BlockSpec((None, 1, HEAD_DIM), lambda gg, i: (gg // n, 0, 0))
    cat = pl.BlockSpec((tr, width), lambda gg, i: (i, gg))
    return (o_a, o_b, gate, w, cat), g, n


def _silu_and_grad(x):
    s = _sigmoid(x)
    return x * s, s * (1.0 + x * (1.0 - s))


def _gate_fwd(o_a, o_b, proj, norm_w, h, z_blk, g_blk):
    t = o_a.shape[0]
    tr = _tile(t, 512)

    (sa, sb, sg, sw, cat), g, n = _gate_specs(h, z_blk, g_blk, tr)

    def body(oa_ref, ob_ref, z_ref, w_ref, y_ref):
        for k in range(g):
            lanes = pl.ds(k * HEAD_DIM, HEAD_DIM)
            o = jnp.where(pl.program_id(0) < n, oa_ref[:, lanes], ob_ref[:, lanes])
            r = lax.rsqrt(jnp.mean(o * o, axis=-1, keepdims=True) + NORM_EPS)
            y_ref[:, lanes] = (o * r * w_ref[...] * _silu(z_ref[:, lanes])).astype(y_ref.dtype)

    return pl.pallas_call(
        body, name="gate_fwd", out_shape=jax.ShapeDtypeStruct((t, 2 * h * HEAD_DIM), BF16),
        grid=(2 * n, t // tr), in_specs=[sa, sb, sg, sw], out_specs=cat,
        compiler_params=_params(("parallel", "parallel")),
    )(o_a, o_b, proj, norm_w)


def _gate_bwd(o_a, o_b, proj, norm_w, dy, h, z_blk, g_blk):
    t = o_a.shape[0]
    tr = _tile(t, 512)

    (sa, sb, sg, sw, cat), g, n = _gate_specs(h, z_blk, g_blk, tr)

    def body(oa_ref, ob_ref, z_ref, w_ref, dy_ref, do_ref, dz_ref, dw_ref):
        gg = pl.program_id(0)

        @pl.when(jnp.logical_and(gg % n == 0, pl.program_id(1) == 0))
        def _():
            dw_ref[...] = jnp.zeros_like(dw_ref)

        w = w_ref[...]
        dw = jnp.zeros_like(w)
        for k in range(g):
            lanes = pl.ds(k * HEAD_DIM, HEAD_DIM)
            o = jnp.where(gg < n, oa_ref[:, lanes], ob_ref[:, lanes])
            dyv = dy_ref[:, lanes]
            r = lax.rsqrt(jnp.mean(o * o, axis=-1, keepdims=True) + NORM_EPS)
            oh = o * r
            act, dact = _silu_and_grad(z_ref[:, lanes])
            dz_ref[:, lanes] = (dyv * oh * w * dact).astype(dz_ref.dtype)
            dn = dyv * act
            doh = dn * w
            do_ref[:, lanes] = r * (doh - oh * jnp.mean(doh * oh, axis=-1, keepdims=True))
            dw = dw + jnp.sum(dn * oh, axis=0, keepdims=True)
        dw_ref[...] += dw

    width = 2 * h * HEAD_DIM
    return pl.pallas_call(
        body, name="gate_bwd",
        out_shape=(jax.ShapeDtypeStruct((t, width), F32), jax.ShapeDtypeStruct((t, width), BF16),
                   jax.ShapeDtypeStruct((2, 1, HEAD_DIM), F32)),
        grid=(2 * n, t // tr), in_specs=[sa, sb, sg, sw, cat], out_specs=(cat, cat, sw),
        compiler_params=_params(("arbitrary", "arbitrary")),
    )(o_a, o_b, proj, norm_w, dy)


def _lane_row(vec):
    return jnp.pad(vec.reshape(1, -1), ((0, 0), (0, HEAD_DIM - vec.shape[-1])))


def _add_epi(acc, res):
    return (acc + res,)


def _split_w_in(w_in, h):
    gw = h * HEAD_DIM
    main = jnp.concatenate([w_in[:, :4 * gw], w_in[:, 4 * gw + 2 * h:]], axis=1)
    pad = jnp.zeros((w_in.shape[0], HEAD_DIM - h), w_in.dtype)
    ab = jnp.concatenate([w_in[:, 4 * gw:4 * gw + h], pad, w_in[:, 4 * gw + h:4 * gw + 2 * h], pad], axis=1)
    return main, ab


def _merge_w_in(main, ab, h):
    gw = h * HEAD_DIM
    return jnp.concatenate([main[:, :4 * gw], ab[:, :h], ab[:, HEAD_DIM:HEAD_DIM + h], main[:, 4 * gw:]], axis=1)


def _local_step(x, target, w_main, w_ab, conv_w, a_log, dt_bias, gdn_norm_w, lb_logits, hgrn_norm_w,
                w_out, norm_mix_w, norm_ffn_w, w_ff1, w_ff2, norm_final_w, reducer=None, n1=None):
    t, d = x.shape
    h = d // (2 * HEAD_DIM)
    gw = h * HEAD_DIM
    k_blk, v_blk, z_blk, qb_blk, fb_blk, ib_blk, gb_blk = (i * h for i in range(1, 8))
    del k_blk, v_blk

    if n1 is None:
        n1 = _rms_fwd(x, norm_mix_w, "rms_mix")
    stacked = w_main.ndim == 3
    proj = _mm(n1, w_main, "nn", (F32,), "in_proj", b_stacked=stacked)
    ab = _mm(n1, w_ab, "nn", (F32,), "in_proj_ab")

    q, k, v = _gdn_prep_fwd(proj, conv_w, h)
    a_log_row, dt_row = _lane_row(a_log), _lane_row(dt_bias)
    g_tm, beta_tm = _gdn_gates_fwd(ab, a_log_row, dt_row)
    to_heads = lambda a: jnp.broadcast_to(a[:, :h].T[:, :, None], (h, t, HEAD_DIM))
    g_bc, beta_bc = to_heads(g_tm), to_heads(beta_tm)
    o_a, st_a, inv_a = _gdn_fwd(q, k, v, beta_bc, g_bc)

    qh, kh, lf = _hgrn_prep_fwd(proj, lb_logits, h, qb_blk, fb_blk)
    o_b, st_b = _hgrn_fwd(qh, kh, proj, lf, v_blk=ib_blk)

    gate_w = jnp.stack([gdn_norm_w.reshape(1, HEAD_DIM), hgrn_norm_w.reshape(1, HEAD_DIM)])
    y = _gate_fwd(o_a, o_b, proj, gate_w, h, z_blk, gb_blk)
    h1 = _mm(y, w_out, "nn", (F32,), "out_proj", epi=_add_epi, extras=(x,))
    n2 = _rms_fwd(h1, norm_ffn_w, "rms_ffn")
    act, r = _mm(n2, w_ff1, "nn", (F32, BF16), "ff1", b_stacked=True,
                 epi=lambda acc: (acc, jnp.square(jnp.maximum(acc, 0.0))))
    h2 = _mm(r, w_ff2, "nn", (F32,), "ff2", epi=_add_epi, extras=(h1,))
    loss, dh2, dh2_b, d_norm_final = _loss_head(h2, norm_final_w, target)

    da = _mm(dh2_b, w_ff2, "nt", (BF16,), "ff2_dx",
             epi=lambda acc, a: (acc * (2.0 * jnp.maximum(a, 0.0)),), extras=(act,))
    pending = []

    def step(anchor, name=None, full=None):
        if reducer is not None:
            pending.extend(reducer.step(name, full, anchor))

    def after_step(value):
        if not pending:
            return value
        value = lax.optimization_barrier((value, *pending))[0]
        pending.clear()
        return value

    d_ff2 = _mm(r, dh2_b, "tn", (F32,), "ff2_dw")
    step(None, "w_ff2", d_ff2)
    dn2 = _mm(da, w_ff1, "nt", (F32,), "ff1_dx", b_stacked=True)
    d_ff1 = _mm(n2, da, "tn", (F32,), "ff1_dw", out_stacked=True)
    step(d_ff1, "w_ff1", d_ff1)
    dh1, dh1_b, d_norm_ffn = _rms_bwd(after_step(dn2), h1, norm_ffn_w, dh2, "rms_ffn_bwd")
    dy = _mm(dh1_b, w_out, "nt", (F32,), "out_proj_dx")
    d_out = _mm(y, dh1_b, "tn", (F32,), "out_proj_dw")
    step(d_out, "w_out", d_out)

    do, dgate, d_gate_w = _gate_bwd(o_a, o_b, proj, gate_w, after_step(dy), h, z_blk, gb_blk)
    step(do)
    dq, dk, dv, dbeta_bc, dg_bc = _gdn_bwd(q, k, v, beta_bc, g_bc, st_a, inv_a, after_step(do), do_blk=0)
    step(dq)
    dxq, dxk, dxv, dcq, dck, dcv = _gdn_prep_bwd(proj, conv_w, after_step(dq), dk, dv, h)
    step(dxq)
    from_heads = lambda a: jnp.pad(a[:, :, 0, :].reshape(h, t).T, ((0, 0), (0, HEAD_DIM - h)))
    dab, d_a_log, d_dt_bias = _gdn_gates_bwd(ab, a_log_row, dt_row, from_heads(dg_bc), from_heads(dbeta_bc))
    dqh, dkh, dvh, dlf = _hgrn_bwd(after_step(qh), kh, proj, lf, st_b, do, v_blk=ib_blk, do_blk=h)
    step(dqh)
    dxqb, dxfb, d_lb = _hgrn_prep_bwd(proj, lb_logits, dqh, dkh, dlf, h, qb_blk, fb_blk)

    dproj = jnp.concatenate([after_step(dxq), dxk, dxv, dgate[:, :gw], dxqb, dxfb, dvh.astype(BF16), dgate[:, gw:]],
                            axis=1)
    d_main = _mm(n1, dproj, "tn", (F32,), "in_proj_dw", out_stacked=True)
    d_ab = _mm(n1, dab, "tn", (F32,), "in_proj_ab_dw")
    step(d_main, "w_in", d_main)
    dn1_ab = _mm(after_step(dab), w_ab, "nt", (F32,), "in_proj_ab_dx")
    step(dn1_ab)
    dn1 = _mm(after_step(dproj), w_main, "nt", (F32,), "in_proj_dx", epi=_add_epi, extras=(dn1_ab,),
              b_stacked=stacked)
    step(dn1)
    dx, _, d_norm_mix = _rms_bwd(after_step(dn1), x, norm_mix_w, dh1, "rms_mix_bwd")
    step(dx)

    grads = dict(
        w_main=d_main, w_ab=d_ab, conv_w=jnp.concatenate([dcq, dck, dcv], axis=1),
        gdn_a_log=d_a_log[:, :h], gdn_dt_bias=d_dt_bias[:, :h], gdn_norm_w=d_gate_w[0],
        hgrn_lb_logits=d_lb, hgrn_norm_w=d_gate_w[1], w_out=d_out, norm_mix_w=d_norm_mix,
        norm_ffn_w=d_norm_ffn, w_ff1=d_ff1, w_ff2=d_ff2, norm_final_w=d_norm_final)
    return loss, dx, grads


N_CHIPS = 4
ANY = pl.BlockSpec(memory_space=pl.ANY)


def _place():
    x, y, c = lax.axis_index("x"), lax.axis_index("y"), lax.axis_index("c")
    chips = [(1 - x, y), (x, 1 - y), (1 - x, 1 - y)]
    return x, y, c, chips


def _remote(src, dst, send_sems, recv_sems, k, to):
    return pltpu.make_async_remote_copy(src_ref=src, dst_ref=dst, send_sem=send_sems.at[k],
                                        recv_sem=recv_sems.at[k], device_id=to, device_id_type=MESH)


def _to_sibling(x, y, c, chips):
    return [(x, y, 1 - c)]


def _to_same_core_of_chips(x, y, c, chips):
    return [(*chip, c) for chip in chips]


def _to_all_gather_peers(x, y, c, chips):
    return _to_sibling(x, y, c, chips) + _to_same_core_of_chips(x, y, c, chips)


SIBLING_EXCHANGE = (1, _to_sibling)
CHIP_EXCHANGE = (2, _to_same_core_of_chips)
GATHER_EXCHANGE = (3, _to_all_gather_peers)


def _launch(body, name, out_shapes, arrays, sem_counts, sequencer=None, after=()):
    n, n_after = len(arrays), len(after)
    sems = [pltpu.SemaphoreType.DMA((k,)) for k in sem_counts]
    strip = lambda refs: refs[:n] + refs[n + n_after:]
    if sequencer is None:
        return pl.pallas_call(
            lambda *refs: body(*strip(refs)), name=name, out_shape=tuple(out_shapes),
            in_specs=[ANY] * (n + n_after), out_specs=tuple(ANY for _ in out_shapes), scratch_shapes=sems,
        )(*arrays, *after)
    collective_id, peers = sequencer

    def sequencer_body(*refs):
        x, y, c, chips = _place()
        barrier = pltpu.get_barrier_semaphore()
        targets = peers(x, y, c, chips)
        for target in targets:
            pl.semaphore_signal(barrier, inc=1, device_id=target, device_id_type=MESH)
        pl.semaphore_wait(barrier, len(targets))
        body(*strip(refs))

    return pl.kernel(
        sequencer_body, name=name, out_type=tuple(out_shapes),
        mesh=plsc.ScalarSubcoreMesh(axis_name="sequencer", num_cores=1), scratch_types=tuple(sems),
        compiler_params=pltpu.CompilerParams(collective_id=collective_id),
    )(*arrays, *after)


def _gather_weights(big, small, name, sequencer=None, after=()):
    nb, ns = len(big), len(small)
    n_sem = 6 * nb + 3 * ns

    def body(*refs):
        ins, outs = refs[:nb + ns], refs[nb + ns:2 * (nb + ns)]
        send_sems, recv_sems = refs[2 * (nb + ns):]
        x, y, c, chips = _place()
        me, sibling = 2 * x + y, (x, y, 1 - c)

        def half(a, chip, hc):
            rh = big[a].shape[0] // 2
            return outs[a].at[2 * chip[0] + chip[1], pl.ds(hc * rh, rh), :]

        first, passed = [], []
        for a in range(nb):
            rh = big[a].shape[0] // 2
            for j, chip in enumerate(chips):
                first.append(_remote(ins[a].at[pl.ds(c * rh, rh), :], half(a, (x, y), c),
                                     send_sems, recv_sems, 6 * a + j, (*chip, c)))
        for s in range(ns):
            for j, chip in enumerate(chips):
                first.append(_remote(ins[nb + s], outs[nb + s].at[me], send_sems, recv_sems,
                                     6 * nb + 3 * s + j, (*chip, c)))
        for cp in first:
            cp.start()
        for a in range(nb):
            for j, chip in enumerate(chips):
                _remote(half(a, chip, c), half(a, chip, c), send_sems, recv_sems, 6 * a + j, (*chip, c)).wait_recv()
                fwd = _remote(half(a, chip, c), half(a, chip, c), send_sems, recv_sems, 6 * a + 3 + j, sibling)
                fwd.start()
                passed.append(fwd)
        for s in range(ns):
            for j, chip in enumerate(chips):
                dst = outs[nb + s].at[2 * chip[0] + chip[1]]
                _remote(dst, dst, send_sems, recv_sems, 6 * nb + 3 * s + j, (*chip, c)).wait_recv()
        for a in range(nb):
            for j, chip in enumerate(chips):
                _remote(half(a, chip, 1 - c), half(a, chip, 1 - c), send_sems, recv_sems,
                        6 * a + 3 + j, sibling).wait_recv()
        for cp in first + passed:
            cp.wait_send()

    arrays = list(big) + list(small)
    out_shapes = [jax.ShapeDtypeStruct((N_CHIPS,) + a.shape, a.dtype) for a in arrays]
    return _launch(body, name, out_shapes, arrays, (n_sem, n_sem), sequencer, after)


def _swap_halves(parts, name, sequencer=None):
    n = len(parts)

    def body(*refs):
        ins, outs = refs[:n], refs[n:2 * n]
        send_sems, recv_sems = refs[2 * n:]
        x, y, c, _ = _place()
        copies = [_remote(ins[a].at[s, 1 - c], outs[a].at[s], send_sems, recv_sems, N_CHIPS * a + s, (x, y, 1 - c))
                  for a in range(n) for s in range(N_CHIPS)]
        for cp in copies:
            cp.start()
        for cp in copies:
            cp.wait()

    out_shapes = [jax.ShapeDtypeStruct((N_CHIPS,) + p.shape[2:], p.dtype) for p in parts]
    return _launch(body, name, out_shapes, parts, (N_CHIPS * n, N_CHIPS * n), sequencer)


def _scatter_to_owners(parts, name, sequencer=None):
    n = len(parts)

    def body(*refs):
        ins, outs = refs[:n], refs[n:2 * n]
        send_sems, recv_sems = refs[2 * n:]
        x, y, c, chips = _place()
        copies = [_remote(ins[a].at[2 * chip[0] + chip[1]], outs[a].at[j], send_sems, recv_sems,
                          3 * a + j, (*chip, c))
                  for a in range(n) for j, chip in enumerate(chips)]
        for cp in copies:
            cp.start()
        for cp in copies:
            cp.wait()

    out_shapes = [jax.ShapeDtypeStruct((3,) + p.shape[1:], p.dtype) for p in parts]
    return _launch(body, name, out_shapes, parts, (3 * n, 3 * n), sequencer)


def _send_to_sibling(halves, name, sequencer=None):
    n = len(halves)

    def body(*refs):
        ins, outs = refs[:n], refs[n:2 * n]
        send_sems, recv_sems = refs[2 * n:]
        x, y, c, _ = _place()
        copies = [_remote(ins[a], outs[a], send_sems, recv_sems, a, (x, y, 1 - c)) for a in range(n)]
        for cp in copies:
            cp.start()
        for cp in copies:
            cp.wait()

    out_shapes = [jax.ShapeDtypeStruct(p.shape, p.dtype) for p in halves]
    return _launch(body, name, out_shapes, halves, (n, n), sequencer)


N_DEV = 8


def _all_reduce_small(vec):
    def body(v_ref, gathered, total, send_sems, recv_sems):
        x, y, c, _ = _place()
        me = 4 * x + 2 * y + c
        gathered[me] = v_ref[...]
        copies = []
        for k in range(1, N_DEV):
            px = 1 - x if k & 4 else x
            py = 1 - y if k & 2 else y
            pc = 1 - c if k & 1 else c
            copies.append(_remote(v_ref, gathered.at[me], send_sems, recv_sems, k - 1, (px, py, pc)))
        for cp in copies:
            cp.start()
        for k, cp in enumerate(copies):
            cp.wait_send()
        for k in range(1, N_DEV):
            px = 1 - x if k & 4 else x
            py = 1 - y if k & 2 else y
            pc = 1 - c if k & 1 else c
            src = gathered.at[4 * px + 2 * py + pc]
            _remote(src, src, send_sems, recv_sems, k - 1, (px, py, pc)).wait_recv()
        acc = gathered[0]
        for dev in range(1, N_DEV):
            acc = acc + gathered[dev]
        total[...] = acc

    vm = pl.BlockSpec(memory_space=pltpu.VMEM)
    return pl.pallas_call(
        body, name="all_reduce_small",
        out_shape=(jax.ShapeDtypeStruct((N_DEV,) + vec.shape, F32), jax.ShapeDtypeStruct(vec.shape, F32)),
        in_specs=[vm], out_specs=(vm, vm),
        scratch_shapes=[pltpu.SemaphoreType.DMA((N_DEV - 1,)), pltpu.SemaphoreType.DMA((N_DEV - 1,))],
    )(vec)[1]


def _chip_sum(part, recv, c, chip):
    _, _, rh, cols = part.shape
    tr = _tile(rh, 256)

    def body(c_ref, chip_ref, p_ref, r_ref, own_ref, sb_ref):
        s = p_ref[...] + r_ref[...]
        sb_ref[...] = s.astype(BF16)

        @pl.when(pl.program_id(1) == chip_ref[0])
        def _():
            own_ref[...] = s

    blk = pl.BlockSpec((None, tr, cols), lambda i, s, c_ref, chip_ref: (s, i, 0))
    return pl.pallas_call(
        body, name="grad_chip_sum",
        out_shape=(jax.ShapeDtypeStruct(recv.shape[1:], F32), jax.ShapeDtypeStruct(recv.shape, BF16)),
        grid_spec=pltpu.PrefetchScalarGridSpec(
            num_scalar_prefetch=2, grid=(rh // tr, N_CHIPS),
            in_specs=[pl.BlockSpec((None, None, tr, cols), lambda i, s, c_ref, chip_ref: (s, c_ref[0], i, 0)), blk],
            out_specs=(pl.BlockSpec((tr, cols), lambda i, s, c_ref, chip_ref: (i, 0)), blk)),
        compiler_params=_params(("parallel", "arbitrary")),
    )(c, chip, part, recv)


def _owner_sum(own, recv):
    rh, cols = own.shape
    tr = _tile(rh, 256)

    def body(o_ref, r0, r1, r2, g_ref):
        g_ref[...] = ((o_ref[...] + r0[...].astype(F32)) + r1[...].astype(F32)) + r2[...].astype(F32)

    slot = lambda j: pl.BlockSpec((None, tr, cols), lambda i: (j, i, 0))
    row = pl.BlockSpec((tr, cols), lambda i: (i, 0))
    return pl.pallas_call(
        body, name="grad_owner_sum", out_shape=jax.ShapeDtypeStruct((rh, cols), F32), grid=(rh // tr,),
        in_specs=[row, slot(0), slot(1), slot(2)], out_specs=row,
        compiler_params=_params(("parallel",)),
    )(own, recv, recv, recv)


def _adamw_math(w, g, m, v):
    c1 = 1.0 / (1.0 - ADAM_B1 ** ADAM_STEP)
    c2 = 1.0 / (1.0 - ADAM_B2 ** ADAM_STEP)
    nm = ADAM_B1 * m + (1.0 - ADAM_B1) * g
    nv = ADAM_B2 * v + (1.0 - ADAM_B2) * (g * g)
    return -ADAM_LR * ((nm * c1) / (jnp.sqrt(nv * c2) + ADAM_EPS) + ADAM_WD * w), nm, nv


def _adamw_unit_rows(w, g, m, v, name):
    rows, _, cols = w.shape
    tr = max(d for d in range(1, 33) if rows % d == 0)

    def body(w_ref, g_ref, m_ref, v_ref, d_ref, nm_ref, nv_ref):
        d_ref[...], nm_ref[...], nv_ref[...] = _adamw_math(w_ref[...], g_ref[...], m_ref[...], v_ref[...])

    blk = pl.BlockSpec((tr, 1, cols), lambda i: (i, 0, 0))
    shape = jax.ShapeDtypeStruct(w.shape, F32)
    return pl.pallas_call(
        body, name=name, out_shape=(shape, shape, shape), grid=(rows // tr,),
        in_specs=[blk, blk, blk, blk], out_specs=(blk, blk, blk),
        compiler_params=_params(("parallel",)),
    )(w, g, m, v)


def _divisor_tile(n, want):
    return max(d for d in range(ROW_TILE, want + 1, ROW_TILE) if n % d == 0)


def _adamw(w, g, m, v, name):
    if w.ndim == 3:
        return _adamw_unit_rows(w, g, m, v, name)
    rows, cols = w.shape
    tr = _divisor_tile(rows, 2048) if rows % 8 == 0 else rows
    c1 = 1.0 / (1.0 - ADAM_B1 ** ADAM_STEP)
    c2 = 1.0 / (1.0 - ADAM_B2 ** ADAM_STEP)

    def body(w_ref, g_ref, m_ref, v_ref, d_ref, nm_ref, nv_ref):
        gv = g_ref[...]
        nm = ADAM_B1 * m_ref[...] + (1.0 - ADAM_B1) * gv
        nv = ADAM_B2 * v_ref[...] + (1.0 - ADAM_B2) * (gv * gv)
        d_ref[...] = -ADAM_LR * ((nm * c1) / (jnp.sqrt(nv * c2) + ADAM_EPS) + ADAM_WD * w_ref[...])
        nm_ref[...] = nm
        nv_ref[...] = nv

    blk = pl.BlockSpec((tr, cols), lambda i: (i, 0))
    shape = jax.ShapeDtypeStruct((rows, cols), F32)
    return pl.pallas_call(
        body, name=name, out_shape=(shape, shape, shape), grid=(rows // tr,),
        in_specs=[blk, blk, blk, blk], out_specs=(blk, blk, blk),
        compiler_params=_params(("parallel",)),
    )(w, g, m, v)


def _adamw_halves(w, g_own, g_sib, m, v, c, name):
    _, rows, cols = w.shape
    rh = rows // 2
    tr = _tile(rh, 256)
    per = rh // tr
    c1 = 1.0 / (1.0 - ADAM_B1 ** ADAM_STEP)
    c2 = 1.0 / (1.0 - ADAM_B2 ** ADAM_STEP)

    def body(c_ref, w_ref, go_ref, gs_ref, m_ref, v_ref, g_ref, d_ref, nm_ref, nv_ref):
        own = pl.program_id(0) // per == c_ref[0]
        gv = jnp.where(own, go_ref[...], gs_ref[...])
        nm = ADAM_B1 * m_ref[...] + (1.0 - ADAM_B1) * gv
        nv = ADAM_B2 * v_ref[...] + (1.0 - ADAM_B2) * (gv * gv)
        g_ref[...] = gv
        d_ref[...] = -ADAM_LR * ((nm * c1) / (jnp.sqrt(nv * c2) + ADAM_EPS) + ADAM_WD * w_ref[...])
        nm_ref[...] = nm
        nv_ref[...] = nv

    blk = pl.BlockSpec((None, tr, cols), lambda i, c_ref: (0, i, 0))
    half = pl.BlockSpec((tr, cols), lambda i, c_ref: (i % per, 0))
    shape = jax.ShapeDtypeStruct((1, rows, cols), F32)
    return pl.pallas_call(
        body, name=name, out_shape=(shape, shape, shape, shape),
        grid_spec=pltpu.PrefetchScalarGridSpec(
            num_scalar_prefetch=1, grid=(rows // tr,),
            in_specs=[blk, half, half, blk, blk], out_specs=(blk, blk, blk, blk)),
        compiler_params=_params(("parallel",)),
    )(c, w, g_own, g_sib, m, v)


def _by_shard(name, full):
    if name in ("w_in", "w_ff1"):
        st = full
    else:
        st = full.reshape(N_CHIPS, -1, full.shape[1])
    return st.reshape(N_CHIPS, 2, st.shape[1] // 2, st.shape[2])


class _GradReducer:
    def __init__(self, w, m, v, my_c, my_chip):
        self.w, self.m, self.v, self.my_c, self.my_chip = w, m, v, my_c, my_chip
        self.in_flight = []
        self.computed = []
        self.anchor = None
        self.done = {}

    def step(self, name=None, full=None, anchor=None):
        stages, self.in_flight, self.computed, self.anchor = self.in_flight, [], [], anchor
        for stage in [s for s in stages if not getattr(s, "long", False)]:
            self._advance(stage)
        if name is not None:
            self.in_flight.append(self._swap(name, _by_shard(name, full)))
        for stage in [s for s in stages if getattr(s, "long", False)]:
            self._advance(stage)
        return self.computed

    def _held(self, value):
        if self.anchor is None:
            return value
        return lax.optimization_barrier((value, self.anchor))[0]

    def _advance(self, stage):
        nxt = stage()
        if nxt is not None:
            self.in_flight.append(nxt)

    def finish(self):
        while self.in_flight:
            self.step()
        return self.done

    def _swap(self, name, part):
        got, = _swap_halves([part], "grad_swap_" + name, SIBLING_EXCHANGE)

        def scatter():
            total, total_bf16 = _chip_sum(part, self._held(got), self.my_c, self.my_chip.reshape(1))
            self.computed.append(total_bf16)
            recv, = _scatter_to_owners([total_bf16], "grad_scatter_" + name, CHIP_EXCHANGE)

            def send():
                half = _owner_sum(total, self._held(recv))
                self.computed.append(half)
                sib, = _send_to_sibling([half], "grad_send_" + name, SIBLING_EXCHANGE)

                def update():
                    self.done[name] = _adamw_halves(self.w[name], half, self._held(sib), self.m[name], self.v[name],
                                                    self.my_c, "adamw_" + name)
                    self.computed.append(self.done[name][0])

                if name == "w_in":
                    self.in_proj_halves = (half, sib)
                    return None
                return update
            return lambda: send
        scatter.long = True
        return scatter


def _adamw_minor_rows(w, g, m, v, name):
    _, rows, cols = w.shape
    turned = lambda a: jnp.transpose(a, (2, 0, 1))
    back = lambda a: jnp.transpose(a, (1, 2, 0))
    g = g.T.reshape(cols, 1, rows)
    delta, new_m, new_v = _adamw(turned(w), g, turned(m), turned(v), name)
    return back(g), back(delta), back(new_m), back(new_v)


def _in_proj_quarter_part(shard, chip, h):
    e = h // 2
    qw = shard.shape[1] - e
    zeros = jnp.zeros((shard.shape[0], h), shard.dtype)
    padded = jnp.concatenate([zeros, shard, zeros], axis=1)
    at_quarter = jnp.where(chip == 0, h, jnp.where(chip == 1, h - e, jnp.where(chip == 2, 2 * h, h + e)))
    at_edge = jnp.where(chip == 0, h + qw, jnp.where(chip == 1, h + qw - e, jnp.where(chip == 2, h, h - e)))
    return (lax.dynamic_slice_in_dim(padded, at_quarter, qw, axis=1).astype(BF16),
            lax.dynamic_slice_in_dim(padded, at_edge, h, axis=1))


def _in_proj_quarters(parts, edges, h):
    e = h // 2
    x1, a, b, x2 = edges[0][:, :e], edges[1], edges[2], edges[3][:, e:]
    parts = parts.at[1, :, :e].set(x1.astype(BF16))
    parts = parts.at[2, :, parts.shape[2] - e:].set(x2.astype(BF16))
    pad = jnp.zeros((a.shape[0], HEAD_DIM - h), a.dtype)
    return parts, jnp.concatenate([a, pad, b, pad], axis=1).astype(BF16)


def _in_proj_edge_columns(d_main, d_ab, h):
    e = h // 2
    return jnp.concatenate([d_main[1][:, :e], d_main[2][:, -e:], d_ab[:, :h], d_ab[:, HEAD_DIM:HEAD_DIM + h]], axis=1)


def _in_proj_shard_grad(q_own, q_sib, edges, c, chip, h):
    e = h // 2
    lower, upper = jnp.where(c[0] == 0, q_own, q_sib), jnp.where(c[0] == 0, q_sib, q_own)
    quarter = jnp.concatenate([lower, upper], axis=0)
    x1, x2, a, b = edges[:, :e], edges[:, e:2 * e], edges[:, 2 * e:2 * e + h], edges[:, 2 * e + h:]
    zeros = jnp.zeros_like(x1)
    left = jnp.where(chip == 2, b, jnp.concatenate([zeros, x2], axis=1))
    right = jnp.where(chip == 1, a, jnp.concatenate([x1, zeros], axis=1))
    start = jnp.where(chip == 0, h, jnp.where(chip == 1, h + e, jnp.where(chip == 2, 0, h - e)))
    padded = jnp.concatenate([left, quarter, right], axis=1)
    return lax.dynamic_slice_in_dim(padded, start, quarter.shape[1] + e, axis=1)


SMALL = ("gdn_a_log", "gdn_dt_bias", "gdn_norm_w", "hgrn_lb_logits", "hgrn_norm_w",
         "norm_mix_w", "norm_ffn_w", "norm_final_w")
BIG = ("w_in", "w_out", "w_ff1", "w_ff2")
ORDER = ("w_in", "conv_w", "gdn_a_log", "gdn_dt_bias", "gdn_norm_w", "hgrn_lb_logits", "hgrn_norm_w",
         "w_out", "norm_mix_w", "norm_ffn_w", "w_ff1", "w_ff2", "norm_final_w")


def _pack(pieces):
    flat = jnp.concatenate([p.reshape(-1).astype(F32) for p in pieces])
    rows = -(-flat.shape[0] // (8 * HEAD_DIM)) * 8
    return jnp.pad(flat, (0, rows * HEAD_DIM - flat.shape[0])).reshape(rows, HEAD_DIM)


def _unpack(packed, shapes):
    flat, out, at = packed.reshape(-1), [], 0
    for s in shapes:
        n = 1
        for dim in s:
            n *= dim
        out.append(flat[at:at + n].reshape(s))
        at += n
    return out


def kernel(x, w_in, conv_w, gdn_a_log, gdn_dt_bias, gdn_norm_w, hgrn_lb_logits, hgrn_norm_w, w_out, norm_mix_w, norm_ffn_w, w_ff1, w_ff2, norm_final_w, loss_target, m_w_in, m_conv_w, m_gdn_a_log, m_gdn_dt_bias, m_gdn_norm_w, m_hgrn_lb_logits, m_hgrn_norm_w, m_w_out, m_norm_mix_w, m_norm_ffn_w, m_w_ff1, m_w_ff2, m_norm_final_w, v_w_in, v_conv_w, v_gdn_a_log, v_gdn_dt_bias, v_gdn_norm_w, v_hgrn_lb_logits, v_hgrn_norm_w, v_w_out, v_norm_mix_w, v_norm_ffn_w, v_w_ff1, v_w_ff2, v_norm_final_w):
    w = dict(w_in=w_in, conv_w=conv_w, gdn_a_log=gdn_a_log, gdn_dt_bias=gdn_dt_bias, gdn_norm_w=gdn_norm_w,
             hgrn_lb_logits=hgrn_lb_logits, hgrn_norm_w=hgrn_norm_w, w_out=w_out, norm_mix_w=norm_mix_w,
             norm_ffn_w=norm_ffn_w, w_ff1=w_ff1, w_ff2=w_ff2, norm_final_w=norm_final_w)
    m = dict(w_in=m_w_in, conv_w=m_conv_w, gdn_a_log=m_gdn_a_log, gdn_dt_bias=m_gdn_dt_bias,
             gdn_norm_w=m_gdn_norm_w, hgrn_lb_logits=m_hgrn_lb_logits, hgrn_norm_w=m_hgrn_norm_w,
             w_out=m_w_out, norm_mix_w=m_norm_mix_w, norm_ffn_w=m_norm_ffn_w, w_ff1=m_w_ff1, w_ff2=m_w_ff2,
             norm_final_w=m_norm_final_w)
    v = dict(w_in=v_w_in, conv_w=v_conv_w, gdn_a_log=v_gdn_a_log, gdn_dt_bias=v_gdn_dt_bias,
             gdn_norm_w=v_gdn_norm_w, hgrn_lb_logits=v_hgrn_lb_logits, hgrn_norm_w=v_hgrn_norm_w,
             w_out=v_w_out, norm_mix_w=v_norm_mix_w, norm_ffn_w=v_norm_ffn_w, w_ff1=v_w_ff1, w_ff2=v_w_ff2,
             norm_final_w=v_norm_final_w)
    d = x.shape[-1]
    h = d // (2 * HEAD_DIM)
    my_c = lax.axis_index("c").astype(jnp.int32).reshape(1)
    my_chip = (2 * lax.axis_index("x") + lax.axis_index("y")).astype(jnp.int32)

    shards = [w[n][0].astype(BF16) for n in BIG]
    conv_shard = jnp.pad(conv_w[0], ((0, 8 - CONV_W), (0, 0)))
    quarter_part, edge_part = _in_proj_quarter_part(w_in[0], my_chip, h)
    first = _gather_weights([quarter_part], [conv_shard, edge_part], "gather_in_proj", GATHER_EXCHANGE)
    n1 = _rms_fwd(x[0], norm_mix_w[0], "rms_mix")
    gathered_in, n1, *shards[1:] = lax.optimization_barrier((first[0], n1, *shards[1:]))
    own_slot = lambda st, own: lax.dynamic_update_index_in_dim(st, own, my_chip, 0)
    f_conv, f_edges = own_slot(first[1], conv_shard), own_slot(first[2], edge_part)
    w_main, w_ab = _in_proj_quarters(own_slot(gathered_in, quarter_part), f_edges, h)
    cols = lambda st: st.transpose(1, 0, 2).reshape(st.shape[1], -1)
    conv_full = cols(f_conv[:, :CONV_W])
    rest = _gather_weights(shards[1:], [], "gather_rest", GATHER_EXCHANGE, after=[w_ab])
    f_out, f_ff1, f_ff2 = (own_slot(st, own) for st, own in zip(rest, shards[1:]))

    reducer = _GradReducer(w, m, v, my_c, my_chip)
    loss, dx, g = _local_step(
        x[0], loss_target[0], w_main, w_ab, conv_full, gdn_a_log[0], gdn_dt_bias[0], gdn_norm_w[0],
        hgrn_lb_logits, hgrn_norm_w[0], f_out.reshape(-1, d), norm_mix_w[0], norm_ffn_w[0],
        f_ff1, f_ff2.reshape(-1, d), norm_final_w, reducer, n1)

    grads, delta, new_m, new_v = {}, {}, {}, {}
    for n, out in reducer.finish().items():
        grads[n], delta[n], new_m[n], new_v[n] = out

    edges = _in_proj_edge_columns(g["w_main"], g["w_ab"], h)
    small_shapes = [w[n].shape for n in SMALL] + [conv_full.shape, (1,), edges.shape]
    total = _all_reduce_small(_pack([g[n] for n in SMALL] + [g["conv_w"], loss[0, :1], edges]))
    *small_grads, conv_grad, loss_sum, edges = _unpack(total, small_shapes)
    g_in = _in_proj_shard_grad(*reducer.in_proj_halves, edges, my_c, my_chip, h)
    grads["w_in"], delta["w_in"], new_m["w_in"], new_v["w_in"] = _adamw_minor_rows(
        w["w_in"], g_in, m["w_in"], v["w_in"], "adamw_w_in")
    for n, sg in zip(SMALL, small_grads):
        grads[n] = sg
    shard_cols = conv_w.shape[-1]
    grads["conv_w"] = lax.dynamic_slice_in_dim(conv_grad, my_chip * shard_cols, shard_cols, axis=1)[None]

    packed_names = SMALL + ("conv_w",)
    packed = [_pack([t[n] for n in packed_names]) for t in (w, grads, m, v)]
    outs = _adamw(*packed, "adamw_small")
    shapes = [w[n].shape for n in packed_names]
    for res, o in zip((delta, new_m, new_v), outs):
        for n, a in zip(packed_names, _unpack(o, shapes)):
            res[n] = a

    return (loss_sum.reshape(()), dx[None], *[grads[n] for n in ORDER], *[delta[n] for n in ORDER],
            *[new_m[n] for n in ORDER], *[new_v[n] for n in ORDER])
```

```python
import functools

import jax
import jax.numpy as jnp
from jax import lax
from jax.experimental import pallas as pl
from jax.experimental.pallas import tpu as pltpu
from jax.experimental.pallas import tpu_sc as plsc

F32 = jnp.float32
BF16 = jnp.bfloat16

HEAD_DIM = 128
CHUNK = 128
SUB = 16
EXP_CAP = 80.0
NORM_EPS = 1e-6
L2_EPS = 1e-6
CONV_W = 4
VMEM_LIMIT = 56 * 1024 * 1024

ADAM_LR, ADAM_B1, ADAM_B2, ADAM_EPS, ADAM_WD, ADAM_STEP = 1e-3, 0.9, 0.999, 1e-8, 0.01, 10

NN = ((1,), (0,))
NT = ((1,), (1,))
TN = ((0,), (0,))
MESH = pl.DeviceIdType.MESH


def _dot(a, b, dims):
    return lax.dot_general(a.astype(BF16), b.astype(BF16), (dims, ((), ())),
                           preferred_element_type=F32)


def _split(a):
    hi = a.astype(BF16)
    return hi, (a - hi.astype(F32)).astype(BF16)


def _dot3(a, b, dims):
    ah, al = _split(a)
    bh, bl = _split(b)
    d = lambda x, y: lax.dot_general(x, y, (dims, ((), ())), preferred_element_type=F32)
    return d(ah, bh) + (d(ah, bl) + d(al, bh))


def _sigmoid(x):
    return 1.0 / (1.0 + jnp.exp(-x))


def _silu(x):
    return x * _sigmoid(x)


def _dsilu(x):
    s = _sigmoid(x)
    return s * (1.0 + x * (1.0 - s))


def _softplus(x):
    e = jnp.exp(-jnp.abs(x))
    u = 1.0 + e
    log1p = jnp.where(u == 1.0, e, jnp.log(u) * (e / jnp.where(u == 1.0, 1.0, u - 1.0)))
    return jnp.maximum(x, 0.0) + log1p


def _iota(shape, axis):
    return lax.broadcasted_iota(jnp.int32, shape, axis)


def _cumsum_rows(x):
    n = x.shape[0]
    row = _iota(x.shape, 0)
    s = 1
    while s < n:
        x = x + jnp.where(row >= s, pltpu.roll(x, s, 0), 0.0)
        s *= 2
    return x


def _rev_cumsum_rows(x):
    return jnp.sum(x, axis=0, keepdims=True) - _cumsum_rows(x) + x


def _params(sem):
    return pltpu.CompilerParams(dimension_semantics=sem, vmem_limit_bytes=VMEM_LIMIT)


ROW_TILE = 8
HEADS_PER_STEP = 8


def _hps(h):
    return min(HEADS_PER_STEP, h)


def _head_view(ref, hb):
    if len(ref.shape) == 2:
        return ref.at[:, pl.ds(hb * HEAD_DIM, HEAD_DIM)]
    return ref.at[hb]


class _Staged:
    def __init__(self, ref, load):
        self.ref = ref
        self.loaded = ref[...] if load else None
        self.written = None

    def __getitem__(self, idx):
        return self.loaded

    def __setitem__(self, idx, value):
        self.written = value


def _each_head(one_head, n_in):
    def body(*refs):
        @pl.when(pl.program_id(1) == 0)
        def _():
            refs[-1][...] = jnp.zeros_like(refs[-1])

        last = len(refs) - 1
        staged = [[_Staged(_head_view(r, hb), i < n_in or i == last) for i, r in enumerate(refs)]
                  for hb in range(refs[-1].shape[0])]
        running = [one_head(*per_head) for per_head in staged]
        while running:
            for gen in list(running):
                try:
                    next(gen)
                except StopIteration:
                    running.remove(gen)
        for per_head in staged:
            for s in per_head:
                if s.written is not None:
                    s.ref[...] = s.written
    return body


def _tile(n, want):
    t = min(n, want)
    while n % t:
        t //= 2
    return t


def _mm(a, b, mode, out_dtypes, name, epi=None, extras=(), tm=1024, tn=1024, tk=2048,
        b_stacked=False, out_stacked=False):
    if mode == "tn":
        kdim, m = a.shape
    else:
        m, kdim = a.shape
    if b_stacked:
        n = N_CHIPS * b.shape[2] if mode == "nn" else b.shape[1]
        kdim_b = b.shape[1] if mode == "nn" else N_CHIPS * b.shape[2]
        assert kdim_b == kdim
    else:
        n = b.shape[0] if mode == "nt" else b.shape[1]
    per_shard = (n if (mode == "nn" or out_stacked) else kdim) // N_CHIPS
    tm, tn, tk = _tile(m, tm), _tile(n, tn), _tile(kdim, tk)
    if (b_stacked and mode == "nn") or out_stacked:
        tn = _tile(per_shard, tn)
    if b_stacked and mode == "nt":
        tk = _tile(per_shard, tk)
    nk = kdim // tk
    dims = {"nn": NN, "nt": NT, "tn": TN}[mode]
    a_spec = (pl.BlockSpec((tk, tm), lambda i, j, k: (k, i)) if mode == "tn"
              else pl.BlockSpec((tm, tk), lambda i, j, k: (i, k)))
    if b_stacked and mode == "nn":
        per = per_shard // tn
        b_spec = pl.BlockSpec((None, tk, tn), lambda i, j, k: (j // per, k, j % per))
    elif b_stacked:
        per = per_shard // tk
        b_spec = pl.BlockSpec((None, tn, tk), lambda i, j, k: (k // per, j, k % per))
    else:
        b_spec = (pl.BlockSpec((tn, tk), lambda i, j, k: (j, k)) if mode == "nt"
                  else pl.BlockSpec((tk, tn), lambda i, j, k: (k, j)))
    mn_spec = pl.BlockSpec((tm, tn), lambda i, j, k: (i, j))
    if out_stacked:
        per_o = per_shard // tn
        out_spec = pl.BlockSpec((None, tm, tn), lambda i, j, k: (j // per_o, i, j % per_o))
        out_shape = (N_CHIPS, m, per_shard)
    else:
        out_spec, out_shape = mn_spec, (m, n)
    ne, no = len(extras), len(out_dtypes)
    if epi is None:
        epi = lambda acc: (acc,)

    def body(a_ref, b_ref, *rest):
        extra_refs, out_refs = rest[:ne], rest[ne:ne + no]
        part = _dot(a_ref[...], b_ref[...], dims)

        def finish(total):
            outs = epi(total, *[r[...] for r in extra_refs])
            for o_ref, o in zip(out_refs, outs):
                o_ref[...] = o.astype(o_ref.dtype)

        if nk == 1:
            finish(part)
            return
        acc = rest[-1]
        k = pl.program_id(2)

        @pl.when(k == 0)
        def _():
            acc[...] = part

        @pl.when(jnp.logical_and(k > 0, k < nk - 1))
        def _():
            acc[...] += part

        @pl.when(k == nk - 1)
        def _():
            finish(acc[...] + part)

    outs = pl.pallas_call(
        body, name=name,
        out_shape=tuple(jax.ShapeDtypeStruct(out_shape, d) for d in out_dtypes),
        grid=(m // tm, n // tn, nk),
        in_specs=[a_spec, b_spec] + [mn_spec] * ne,
        out_specs=tuple(out_spec for _ in out_dtypes),
        scratch_shapes=[pltpu.VMEM((tm, tn), F32)] if nk > 1 else [],
        compiler_params=_params(("parallel", "parallel", "arbitrary")),
    )(a, b, *extras)
    return outs if no > 1 else outs[0]


def _mm_quarters(a, stack, quarters, name, into=None, tm=1024, tn=1024):
    m, kdim = a.shape
    qw = stack.shape[2]
    tm, tn = _tile(m, tm), _tile(qw, tn)
    per = qw // tn

    def body(q_ref, a_ref, b_ref, *rest):
        rest[-1][...] = _dot(a_ref[...], b_ref[...], NN)

    in_specs = [pl.BlockSpec((tm, kdim), lambda i, j, q: (i, 0)),
                pl.BlockSpec((None, kdim, tn), lambda i, j, q: (q[j // per], 0, j % per))]
    args = (quarters, a, stack)
    if into is not None:
        in_specs.append(ANY)
        args += (into,)
    return pl.pallas_call(
        body, name=name, out_shape=jax.ShapeDtypeStruct((m, N_CHIPS * qw), F32),
        grid_spec=pltpu.PrefetchScalarGridSpec(
            num_scalar_prefetch=1, grid=(m // tm, quarters.shape[0] * per), in_specs=in_specs,
            out_specs=pl.BlockSpec((tm, tn), lambda i, j, q: (i, q[j // per] * per + j % per))),
        input_output_aliases={3: 0} if into is not None else {},
        compiler_params=_params(("parallel", "parallel")),
    )(*args)


ROWS = 256


def _rms_fwd(x, w, name):
    t, d = x.shape
    tr = _tile(t, ROWS)

    def body(x_ref, w_ref, n_ref):
        xv = x_ref[...]
        r = lax.rsqrt(jnp.mean(xv * xv, axis=-1, keepdims=True) + NORM_EPS)
        n_ref[...] = (xv * r * w_ref[...]).astype(n_ref.dtype)

    return pl.pallas_call(
        body, name=name, out_shape=jax.ShapeDtypeStruct((t, d), BF16), grid=(t // tr,),
        in_specs=[pl.BlockSpec((tr, d), lambda i: (i, 0)), pl.BlockSpec((1, d), lambda i: (0, 0))],
        out_specs=pl.BlockSpec((tr, d), lambda i: (i, 0)),
        compiler_params=_params(("parallel",)),
    )(x, w.reshape(1, d))


def _rms_bwd(dn, x, w, dres, name):
    t, d = x.shape
    tr = _tile(t, ROWS)

    def body(dn_ref, x_ref, w_ref, dres_ref, dx_ref, dxb_ref, dw_ref):
        i = pl.program_id(0)
        xv, dnv = x_ref[...], dn_ref[...]
        r = lax.rsqrt(jnp.mean(xv * xv, axis=-1, keepdims=True) + NORM_EPS)
        xh = xv * r
        dxh = dnv * w_ref[...]
        dx = dres_ref[...] + r * (dxh - xh * jnp.mean(dxh * xh, axis=-1, keepdims=True))
        dx_ref[...] = dx
        dxb_ref[...] = dx.astype(BF16)

        @pl.when(i == 0)
        def _():
            dw_ref[...] = jnp.zeros_like(dw_ref)

        dw_ref[...] += jnp.sum(dnv * xh, axis=0, keepdims=True)

    row = pl.BlockSpec((tr, d), lambda i: (i, 0))
    vec = pl.BlockSpec((1, d), lambda i: (0, 0))
    return pl.pallas_call(
        body, name=name,
        out_shape=(jax.ShapeDtypeStruct((t, d), F32), jax.ShapeDtypeStruct((t, d), BF16),
                   jax.ShapeDtypeStruct((1, d), F32)),
        grid=(t // tr,), in_specs=[row, row, vec, row], out_specs=(row, row, vec),
        compiler_params=_params(("arbitrary",)),
    )(dn, x, w.reshape(1, d), dres)


def _loss_head(h, w, target):
    t, d = h.shape
    tr = _tile(t, ROWS)

    def body(h_ref, w_ref, t_ref, loss_ref, dh_ref, dhb_ref, dw_ref):
        i = pl.program_id(0)
        hv, wv = h_ref[...], w_ref[...]
        r = lax.rsqrt(jnp.mean(hv * hv, axis=-1, keepdims=True) + NORM_EPS)
        hh = hv * r
        err = hh * wv - t_ref[...]
        dout = err * (1.0 / d)
        dhh = dout * wv
        dh = r * (dhh - hh * jnp.mean(dhh * hh, axis=-1, keepdims=True))
        dh_ref[...] = dh
        dhb_ref[...] = dh.astype(BF16)

        @pl.when(i == 0)
        def _():
            dw_ref[...] = jnp.zeros_like(dw_ref)
            loss_ref[...] = jnp.zeros_like(loss_ref)

        dw_ref[...] += jnp.sum(dout * hh, axis=0, keepdims=True)
        loss_ref[...] += jnp.full((1, 128), 0.5 / d, F32) * jnp.sum(err * err)

    row = pl.BlockSpec((tr, d), lambda i: (i, 0))
    vec = pl.BlockSpec((1, d), lambda i: (0, 0))
    lspec = pl.BlockSpec((1, 128), lambda i: (0, 0))
    return pl.pallas_call(
        body, name="loss_head",
        out_shape=(jax.ShapeDtypeStruct((1, 128), F32), jax.ShapeDtypeStruct((t, d), F32),
                   jax.ShapeDtypeStruct((t, d), BF16), jax.ShapeDtypeStruct((1, d), F32)),
        grid=(t // tr,), in_specs=[row, vec, row], out_specs=(lspec, row, row, vec),
        compiler_params=_params(("arbitrary",)),
    )(h, w.reshape(1, d), target)


def _inv_unit_lower(a):
    c = a.shape[0]
    eye = (_iota((c, c), 0) == _iota((c, c), 1)).astype(F32)
    x = eye - a
    p = _dot3(a, a, NN)
    yield
    n = 2
    while n < c:
        x = x + _dot3(x, p, NN)
        n *= 2
        if n < c:
            p = _dot3(p, p, NN)
        yield
    return x


def _gdn_chunk(q, k, v, beta, g):
    c = q.shape[0]
    row, col = _iota((c, c), 0), _iota((c, c), 1)
    gc = _cumsum_rows(g)
    diff = gc - gc.T
    dec = jnp.where(row >= col, jnp.exp(jnp.minimum(diff, 0.0)), 0.0)
    dec_s = jnp.where(row > col, dec, 0.0)
    gam = jnp.exp(gc)
    g_last = jnp.sum(g, axis=0, keepdims=True)
    kk = _dot(k, k, NT)
    a = beta * kk * dec_s
    p = _dot(q, k, NT) * dec
    e_end = jnp.exp(g_last - gc)
    return dict(dec=dec, dec_s=dec_s, gam=gam, gam_last=jnp.exp(g_last), e_end=e_end,
                k_end=k * e_end, kk=kk, a=a, p=p)


def _gdn_fwd(q, k, v, beta_bc, g_bc):
    t = q.shape[0]
    h = q.shape[1] // HEAD_DIM
    nc = t // CHUNK

    def body(q_ref, k_ref, v_ref, b_ref, g_ref, o_ref, s_ref, t_ref, state):
        qv, kv, vv, beta = q_ref[...], k_ref[...], v_ref[...], b_ref[...]
        ch = _gdn_chunk(qv, kv, vv, beta, g_ref[...])
        yield
        tm = yield from _inv_unit_lower(ch["a"])
        sol = _dot(tm, jnp.concatenate([beta * vv, beta * ch["gam"] * kv], axis=1), NN)
        yield
        u_v, w = sol[:, :HEAD_DIM], sol[:, HEAD_DIM:]
        s0 = state[...]
        u = u_v - _dot(w, s0, NN)
        yield
        o_ref[...] = _dot(qv * ch["gam"], s0, NN) + _dot(ch["p"], u, NN)
        s_ref[...] = s0
        t_ref[...] = tm
        state[...] = ch["gam_last"] * s0 + _dot(ch["k_end"], u, TN)

    tok = pl.BlockSpec((CHUNK, _hps(h) * HEAD_DIM), lambda hh, c: (c, hh))
    bc = pl.BlockSpec((_hps(h), CHUNK, HEAD_DIM), lambda hh, c: (hh, c, 0))
    mat = pl.BlockSpec((_hps(h), None, HEAD_DIM, HEAD_DIM), lambda hh, c: (hh, c, 0, 0))
    return pl.pallas_call(
        _each_head(body, 5), name="gdn_fwd",
        out_shape=(jax.ShapeDtypeStruct(q.shape, F32),
                   jax.ShapeDtypeStruct((h, nc, HEAD_DIM, HEAD_DIM), F32),
                   jax.ShapeDtypeStruct((h, nc, CHUNK, CHUNK), F32)),
        grid=(h // _hps(h), nc), in_specs=[tok, tok, tok, bc, bc], out_specs=(tok, mat, mat),
        scratch_shapes=[pltpu.VMEM((_hps(h), HEAD_DIM, HEAD_DIM), F32)],
        compiler_params=_params(("parallel", "arbitrary")),
    )(q, k, v, beta_bc, g_bc)


def _gdn_bwd(q, k, v, beta_bc, g_bc, states, invs, do, do_blk=0):
    t = q.shape[0]
    h = q.shape[1] // HEAD_DIM
    nc = t // CHUNK

    def body(q_ref, k_ref, v_ref, b_ref, g_ref, s_ref, t_ref, do_ref,
             dq_ref, dk_ref, dv_ref, db_ref, dg_ref, dstate):
        qv, kv, vv, beta = q_ref[...], k_ref[...], v_ref[...], b_ref[...]
        dov, s0, tm, ds1 = do_ref[...], s_ref[...], t_ref[...], dstate[...]
        ch = _gdn_chunk(qv, kv, vv, beta, g_ref[...])
        yield
        gam, dec, dec_s, kk = ch["gam"], ch["dec"], ch["dec_s"], ch["kk"]
        r_v, r_w = beta * vv, beta * gam * kv
        sol = _dot(tm, jnp.concatenate([r_v, r_w], axis=1), NN)
        yield
        u_v, w = sol[:, :HEAD_DIM], sol[:, HEAD_DIM:]
        u = u_v - _dot(w, s0, NN)
        qg = qv * gam
        yield

        du = _dot(ch["p"], dov, TN) + _dot(ch["k_end"], ds1, NN)
        dp = _dot(dov, u, NT)
        dpd = dp * dec
        dqg = _dot(dov, s0, NT)
        dk_end = _dot(u, ds1, NT)
        yield
        dq = dqg * gam + _dot(dpd, kv, NN)
        dk = _dot(dpd, qv, TN) + dk_end * ch["e_end"]
        dstate[...] = _dot(qg, dov, TN) + ch["gam_last"] * ds1 - _dot(w, du, TN)
        dw = -_dot(du, s0, NT)
        yield
        dr = _dot(tm, jnp.concatenate([du, dw], axis=1), TN)
        yield
        dr_v, dr_w = dr[:, :HEAD_DIM], dr[:, HEAD_DIM:]
        da = -_dot(dr, sol, NT)
        yield
        dkk = da * beta * dec_s
        dk = dk + _dot(dkk, kv, NN) + _dot(dkk, kv, TN) + beta * gam * dr_w
        dbeta = (jnp.sum(da * kk * dec_s, axis=1, keepdims=True)
                 + jnp.sum(dr_v * vv + dr_w * gam * kv, axis=1, keepdims=True))

        pair = dp * ch["p"] + da * ch["a"]
        end = jnp.sum(dk_end * ch["k_end"], axis=1, keepdims=True)
        dgc = (jnp.sum(pair - pair.T, axis=1, keepdims=True)
               + jnp.sum(dqg * qg + dr_w * r_w, axis=1, keepdims=True) - end)
        at_end = jnp.sum(end) + ch["gam_last"] * jnp.sum(s0 * ds1)
        dgc = jnp.broadcast_to(dgc, (CHUNK, HEAD_DIM))
        dgc = dgc + jnp.where(_iota((CHUNK, HEAD_DIM), 0) == CHUNK - 1, at_end, 0.0)
        dq_ref[...] = dq
        dk_ref[...] = dk
        dv_ref[...] = beta * dr_v
        db_ref[...] = jnp.broadcast_to(dbeta, (CHUNK, HEAD_DIM)).T[:ROW_TILE]
        dg_ref[...] = _rev_cumsum_rows(dgc).T[:ROW_TILE]

    rev = lambda c: nc - 1 - c
    tok = pl.BlockSpec((CHUNK, _hps(h) * HEAD_DIM), lambda hh, c: (rev(c), hh))
    bc = pl.BlockSpec((_hps(h), CHUNK, HEAD_DIM), lambda hh, c: (hh, rev(c), 0))
    mat = pl.BlockSpec((_hps(h), None, HEAD_DIM, HEAD_DIM), lambda hh, c: (hh, rev(c), 0, 0))
    tok_shape = jax.ShapeDtypeStruct(q.shape, F32)
    row_shape = jax.ShapeDtypeStruct((h, nc, ROW_TILE, CHUNK), F32)
    rows = pl.BlockSpec((_hps(h), None, ROW_TILE, CHUNK), lambda hh, c: (hh, rev(c), 0, 0))
    return pl.pallas_call(
        _each_head(body, 8), name="gdn_bwd",
        out_shape=(tok_shape, tok_shape, tok_shape, row_shape, row_shape),
        grid=(h // _hps(h), nc),
        in_specs=[tok, tok, tok, bc, bc, mat, mat,
                  pl.BlockSpec((CHUNK, _hps(h) * HEAD_DIM), lambda hh, c: (rev(c), do_blk // _hps(h) + hh))],
        out_specs=(tok, tok, tok, rows, rows),
        scratch_shapes=[pltpu.VMEM((_hps(h), HEAD_DIM, HEAD_DIM), F32)],
        compiler_params=_params(("parallel", "arbitrary")),
    )(q, k, v, beta_bc, g_bc, states, invs, do)


def _hgrn_chunk(q, k, lf):
    c = q.shape[0]
    row = _iota((c, HEAD_DIM), 0)
    b = _cumsum_rows(lf)
    q_subs, k_facs, a_rows = [], [], []
    for x in range(c // SUB):
        b_start = jnp.sum(jnp.where(row < x * SUB, lf, 0.0), axis=0, keepdims=True)
        q_x = (q * jnp.exp(jnp.minimum(b - b_start, 0.0)))[x * SUB:(x + 1) * SUB]
        k_fac = jnp.where(row < (x + 1) * SUB, jnp.exp(jnp.minimum(b_start - b, EXP_CAP)), 0.0)
        q_subs.append(q_x)
        k_facs.append(k_fac)
        a_rows.append(_dot(q_x, k * k_fac, NT))
    a = jnp.concatenate(a_rows, axis=0)
    a = jnp.where(_iota((c, c), 0) >= _iota((c, c), 1), a, 0.0)
    b_last = jnp.sum(lf, axis=0, keepdims=True)
    return dict(b=b, a=a, q_subs=q_subs, k_facs=k_facs, e_b=jnp.exp(b),
                e_end=jnp.exp(b_last - b), e_last=jnp.exp(b_last))


def _hgrn_fwd(q, k, v, lf, v_blk=0):
    t = q.shape[0]
    h = q.shape[1] // HEAD_DIM
    nc = t // CHUNK

    def body(q_ref, k_ref, v_ref, lf_ref, o_ref, s_ref, state):
        qv, kv, vv = q_ref[...], k_ref[...], v_ref[...]
        ch = _hgrn_chunk(qv, kv, lf_ref[...])
        yield
        s0 = state[...]
        o_ref[...] = _dot(qv * ch["e_b"], s0, NT) + _dot(ch["a"], vv, NN)
        s_ref[...] = s0
        state[...] = s0 * ch["e_last"] + _dot(vv, kv * ch["e_end"], TN)

    tok = pl.BlockSpec((CHUNK, _hps(h) * HEAD_DIM), lambda hh, c: (c, hh))
    mat = pl.BlockSpec((_hps(h), None, HEAD_DIM, HEAD_DIM), lambda hh, c: (hh, c, 0, 0))
    return pl.pallas_call(
        _each_head(body, 4), name="hgrn_fwd",
        out_shape=(jax.ShapeDtypeStruct(q.shape, F32),
                   jax.ShapeDtypeStruct((h, nc, HEAD_DIM, HEAD_DIM), F32)),
        grid=(h // _hps(h), nc),
        in_specs=[tok, tok, pl.BlockSpec((CHUNK, _hps(h) * HEAD_DIM), lambda hh, c: (c, v_blk // _hps(h) + hh)), tok],
        out_specs=(tok, mat),
        scratch_shapes=[pltpu.VMEM((_hps(h), HEAD_DIM, HEAD_DIM), F32)],
        compiler_params=_params(("parallel", "arbitrary")),
    )(q, k, v, lf)


def _hgrn_bwd(q, k, v, lf, states, do, v_blk=0, do_blk=0):
    t = q.shape[0]
    nc = t // CHUNK
    h = q.shape[1] // HEAD_DIM

    def body(q_ref, k_ref, v_ref, lf_ref, s_ref, do_ref, dq_ref, dk_ref, dv_ref, dlf_ref, dstate):
        qv, kv, vv, dov, s0 = q_ref[...], k_ref[...], v_ref[...], do_ref[...], s_ref[...]
        ds1 = dstate[...]
        ch = _hgrn_chunk(qv, kv, lf_ref[...])
        yield
        c = CHUNK
        row = _iota((c, HEAD_DIM), 0)
        qh = qv * ch["e_b"]
        k_end = kv * ch["e_end"]
        da = jnp.where(_iota((c, c), 0) >= _iota((c, c), 1), _dot(dov, vv, NT), 0.0)
        dqh = _dot(dov, s0, NN)
        dk_end = _dot(vv, ds1, NN)
        yield
        end = dk_end * k_end
        dk = dk_end * ch["e_end"]
        db = dqh * qh - end + jnp.where(
            row == c - 1, jnp.sum(end + s0 * ch["e_last"] * ds1, axis=0, keepdims=True), 0.0)
        dq_rows, qdq_rows = [], []
        for x in range(c // SUB):
            da_x = da[x * SUB:(x + 1) * SUB]
            k_x = kv * ch["k_facs"][x]
            dq_x = _dot(da_x, k_x, NN)
            dk_x = _dot(da_x, ch["q_subs"][x], TN)
            dq_rows.append(dq_x)
            qdq_rows.append(dq_x * ch["q_subs"][x])
            dk = dk + dk_x * ch["k_facs"][x]
            kdk = dk_x * k_x
            db = db - kdk
            if x > 0:
                at_start = jnp.sum(kdk, axis=0, keepdims=True) - jnp.sum(qdq_rows[x], axis=0, keepdims=True)
                db = db + jnp.where(row == x * SUB - 1, at_start, 0.0)
        yield
        b_start = jnp.zeros((c, HEAD_DIM), F32)
        for x in range(1, c // SUB):
            b_x = jnp.sum(jnp.where(row < x * SUB, lf_ref[...], 0.0), axis=0, keepdims=True)
            b_start = jnp.where(row >= x * SUB, b_x, b_start)
        dq = dqh * ch["e_b"] + jnp.concatenate(dq_rows, axis=0) * jnp.exp(jnp.minimum(ch["b"] - b_start, 0.0))
        db = db + jnp.concatenate(qdq_rows, axis=0)
        dstate[...] = _dot(dov, qh, TN) + ds1 * ch["e_last"]
        dq_ref[...] = dq
        dk_ref[...] = dk
        dv_ref[...] = _dot(ch["a"], dov, TN) + _dot(k_end, ds1, NT)
        dlf_ref[...] = _rev_cumsum_rows(db)

    rev = lambda c: nc - 1 - c
    tok = pl.BlockSpec((CHUNK, _hps(h) * HEAD_DIM), lambda hh, c: (rev(c), hh))
    mat = pl.BlockSpec((_hps(h), None, HEAD_DIM, HEAD_DIM), lambda hh, c: (hh, rev(c), 0, 0))
    tok_shape = jax.ShapeDtypeStruct(q.shape, F32)
    return pl.pallas_call(
        _each_head(body, 6), name="hgrn_bwd",
        out_shape=(tok_shape, tok_shape, tok_shape, tok_shape),
        grid=(h // _hps(h), nc),
        in_specs=[tok, tok, pl.BlockSpec((CHUNK, _hps(h) * HEAD_DIM), lambda hh, c: (rev(c), v_blk // _hps(h) + hh)), tok, mat,
                  pl.BlockSpec((CHUNK, _hps(h) * HEAD_DIM), lambda hh, c: (rev(c), do_blk // _hps(h) + hh))],
        out_specs=(tok, tok, tok, tok),
        scratch_shapes=[pltpu.VMEM((_hps(h), HEAD_DIM, HEAD_DIM), F32)],
        compiler_params=_params(("parallel", "arbitrary")),
    )(q, k, v, lf, states, do)


CONV_ROWS = 256
HALO = 8


def _shift_down(cur, prev, s):
    rt = cur.shape[0]
    head = jnp.concatenate([pltpu.roll(prev, s, 0), jnp.zeros((rt - HALO, cur.shape[1]), F32)], axis=0)
    return jnp.where(_iota(cur.shape, 0) < s, head, pltpu.roll(cur, s, 0))


def _shift_up(cur, nxt, s):
    rt = cur.shape[0]
    tail = jnp.concatenate([jnp.zeros((rt - HALO, cur.shape[1]), F32), pltpu.roll(nxt, HALO - s, 0)], axis=0)
    return jnp.where(_iota(cur.shape, 0) >= rt - s, tail, pltpu.roll(cur, rt - s, 0))


def _tile_with_prev(ref, i, rt):
    r0 = pl.multiple_of(i * rt, rt)
    cur = ref[pl.ds(r0, rt), :]
    prev = ref[pl.ds(pl.multiple_of(jnp.maximum(r0 - HALO, 0), HALO), HALO), :]
    return cur, jnp.where(i > 0, prev, 0.0)


def _tile_with_next(ref, i, rt, n_tiles):
    r0 = pl.multiple_of(i * rt, rt)
    cur = ref[pl.ds(r0, rt), :]
    nxt = ref[pl.ds(pl.multiple_of(jnp.minimum(r0 + rt, (n_tiles - 1) * rt), HALO), HALO), :]
    return cur, jnp.where(i < n_tiles - 1, nxt, 0.0)


def _conv_tile(x_ref, w_ref, i, rt):
    cur, prev = _tile_with_prev(x_ref, i, rt)
    shifted = [_shift_down(cur, prev, CONV_W - 1 - j) for j in range(CONV_W - 1)] + [cur]
    c = shifted[0] * w_ref[pl.ds(0, 1), :]
    for j in range(1, CONV_W):
        c = c + shifted[j] * w_ref[pl.ds(j, 1), :]
    return c, shifted


def _l2n(s):
    return s * lax.rsqrt(jnp.sum(s * s, axis=-1, keepdims=True) + L2_EPS)


def _gdn_prep_fwd(proj, conv_w, h):
    t = proj.shape[0]
    rt = _tile(t, CONV_ROWS)
    nt = t // rt
    scale = HEAD_DIM ** -0.5

    def body(xq, xk, xv, wq, wk, wv, q_ref, k_ref, v_ref):
        def tile(i, carry):
            rows = pl.ds(pl.multiple_of(i * rt, rt), rt)
            q_ref[rows, :] = _l2n(_silu(_conv_tile(xq, wq, i, rt)[0])) * scale
            k_ref[rows, :] = _l2n(_silu(_conv_tile(xk, wk, i, rt)[0]))
            v_ref[rows, :] = _silu(_conv_tile(xv, wv, i, rt)[0])
            return carry

        lax.fori_loop(0, nt, tile, 0)

    col = lambda p: pl.BlockSpec((t, HEAD_DIM), lambda hh: (0, p * h + hh))
    wcol = lambda p: pl.BlockSpec((CONV_W, HEAD_DIM), lambda hh: (0, p * h + hh))
    out = pl.BlockSpec((t, HEAD_DIM), lambda hh: (0, hh))
    shape = jax.ShapeDtypeStruct((t, h * HEAD_DIM), F32)
    return pl.pallas_call(
        body, name="gdn_prep_fwd", out_shape=(shape, shape, shape), grid=(h,),
        in_specs=[col(0), col(1), col(2), wcol(0), wcol(1), wcol(2)], out_specs=(out, out, out),
        compiler_params=_params(("parallel",)),
    )(proj, proj, proj, conv_w, conv_w, conv_w)


def _gdn_prep_bwd(proj, conv_w, dq, dk, dv, h):
    t = proj.shape[0]
    rt = _tile(t, CONV_ROWS)
    nt = t // rt
    scale = HEAD_DIM ** -0.5

    def part(x_ref, w_ref, dy_ref, dx_ref, dw_ref, dc_ref, norm_scale):
        def first(i, dws):
            rows = pl.ds(pl.multiple_of(i * rt, rt), rt)
            c, shifted = _conv_tile(x_ref, w_ref, i, rt)
            ds = dy_ref[rows, :]
            s, ds_dc = _silu_and_grad(c)
            if norm_scale is not None:
                r = lax.rsqrt(jnp.sum(s * s, axis=-1, keepdims=True) + L2_EPS)
                y = s * r
                dyn = ds * norm_scale
                ds = r * (dyn - y * jnp.sum(dyn * y, axis=-1, keepdims=True))
            dc = ds * ds_dc
            dc_ref[rows, :] = dc
            return tuple(dws[j] + jnp.sum(dc * shifted[j], axis=0, keepdims=True) for j in range(CONV_W))

        dws = lax.fori_loop(0, nt, first, tuple(jnp.zeros((1, HEAD_DIM), F32) for _ in range(CONV_W)))
        for j in range(CONV_W):
            dw_ref[pl.ds(j, 1), :] = dws[j]

        def second(i, carry):
            rows = pl.ds(pl.multiple_of(i * rt, rt), rt)
            cur, nxt = _tile_with_next(dc_ref, i, rt, nt)
            dx = cur * w_ref[pl.ds(CONV_W - 1, 1), :]
            for j in range(CONV_W - 1):
                dx = dx + _shift_up(cur, nxt, CONV_W - 1 - j) * w_ref[pl.ds(j, 1), :]
            dx_ref[rows, :] = dx.astype(dx_ref.dtype)
            return carry

        lax.fori_loop(0, nt, second, 0)

    def body(xq, xk, xv, wq, wk, wv, dq_ref, dk_ref, dv_ref, dxq, dxk, dxv, dwq, dwk, dwv, dc_ref):
        part(xq, wq, dq_ref, dxq, dwq, dc_ref, scale)
        part(xk, wk, dk_ref, dxk, dwk, dc_ref, 1.0)
        part(xv, wv, dv_ref, dxv, dwv, dc_ref, None)

    col = lambda p: pl.BlockSpec((t, HEAD_DIM), lambda hh: (0, p * h + hh))
    wcol = lambda p: pl.BlockSpec((CONV_W, HEAD_DIM), lambda hh: (0, p * h + hh))
    own = pl.BlockSpec((t, HEAD_DIM), lambda hh: (0, hh))
    wown = pl.BlockSpec((CONV_W, HEAD_DIM), lambda hh: (0, hh))
    dx_shape = jax.ShapeDtypeStruct((t, h * HEAD_DIM), BF16)
    dw_shape = jax.ShapeDtypeStruct((CONV_W, h * HEAD_DIM), F32)
    return pl.pallas_call(
        body, name="gdn_prep_bwd",
        out_shape=(dx_shape, dx_shape, dx_shape, dw_shape, dw_shape, dw_shape), grid=(h,),
        in_specs=[col(0), col(1), col(2), wcol(0), wcol(1), wcol(2), own, own, own],
        out_specs=(own, own, own, wown, wown, wown),
        scratch_shapes=[pltpu.VMEM((t, HEAD_DIM), F32)],
        compiler_params=_params(("parallel",)),
    )(proj, proj, proj, conv_w, conv_w, conv_w, dq, dk, dv)


def _gdn_gates_fwd(ab, a_log_row, dt_bias_row):
    t = ab.shape[0]
    tr = _tile(t, 512)

    def body(ab_ref, al_ref, dt_ref, g_ref, b_ref):
        g_ref[...] = -jnp.exp(al_ref[...]) * _softplus(ab_ref[:, :HEAD_DIM] + dt_ref[...])
        b_ref[...] = _sigmoid(ab_ref[:, HEAD_DIM:])

    row = pl.BlockSpec((tr, HEAD_DIM), lambda i: (i, 0))
    vec = pl.BlockSpec((1, HEAD_DIM), lambda i: (0, 0))
    shape = jax.ShapeDtypeStruct((t, HEAD_DIM), F32)
    return pl.pallas_call(
        body, name="gdn_gates_fwd", out_shape=(shape, shape), grid=(t // tr,),
        in_specs=[pl.BlockSpec((tr, 2 * HEAD_DIM), lambda i: (i, 0)), vec, vec], out_specs=(row, row),
        compiler_params=_params(("parallel",)),
    )(ab, a_log_row, dt_bias_row)


def _gdn_gates_bwd(ab, a_log_row, dt_bias_row, dg, dbeta):
    t = ab.shape[0]
    tr = _tile(t, 512)

    def body(ab_ref, al_ref, dt_ref, dg_ref, db_ref, dab_ref, dal_ref, ddt_ref):
        @pl.when(pl.program_id(0) == 0)
        def _():
            dal_ref[...] = jnp.zeros_like(dal_ref)
            ddt_ref[...] = jnp.zeros_like(ddt_ref)

        xa = ab_ref[:, :HEAD_DIM] + dt_ref[...]
        neg_a = -jnp.exp(al_ref[...])
        dgv = dg_ref[...]
        da = dgv * neg_a * _sigmoid(xa)
        beta = _sigmoid(ab_ref[:, HEAD_DIM:])
        dab_ref[:, :HEAD_DIM] = da.astype(BF16)
        dab_ref[:, HEAD_DIM:] = (db_ref[...] * beta * (1.0 - beta)).astype(BF16)
        dal_ref[...] += jnp.sum(dgv * neg_a * _softplus(xa), axis=0, keepdims=True)
        ddt_ref[...] += jnp.sum(da, axis=0, keepdims=True)

    row = pl.BlockSpec((tr, HEAD_DIM), lambda i: (i, 0))
    row2 = pl.BlockSpec((tr, 2 * HEAD_DIM), lambda i: (i, 0))
    vec = pl.BlockSpec((1, HEAD_DIM), lambda i: (0, 0))
    vshape = jax.ShapeDtypeStruct((1, HEAD_DIM), F32)
    return pl.pallas_call(
        body, name="gdn_gates_bwd",
        out_shape=(jax.ShapeDtypeStruct((t, 2 * HEAD_DIM), BF16), vshape, vshape), grid=(t // tr,),
        in_specs=[row2, vec, vec, row, row], out_specs=(row2, vec, vec),
        compiler_params=_params(("arbitrary",)),
    )(ab, a_log_row, dt_bias_row, dg, dbeta)


def _lower_bound(lb_ref):
    return _sigmoid(lb_ref[pl.ds(0, 1), :] - lb_ref[pl.ds(1, 1), :])


def _hgrn_prep_fwd(proj, lb_logits, h, q_blk, f_blk):
    t = proj.shape[0]
    tr = _tile(t, 512)

    def body(xq, xf, lb_ref, q_ref, k_ref, lf_ref):
        lb = _lower_bound(lb_ref)
        s = _sigmoid(xf[...])
        q_ref[...] = _silu(xq[...])
        k_ref[...] = (1.0 - lb) * (1.0 - s)
        lf_ref[...] = jnp.log(lb + (1.0 - lb) * s)

    width = h * HEAD_DIM
    col = lambda b0: pl.BlockSpec((tr, width), lambda i: (i, b0 // h))
    own = pl.BlockSpec((tr, width), lambda i: (i, 0))
    shape = jax.ShapeDtypeStruct((t, width), F32)
    return pl.pallas_call(
        body, name="hgrn_prep_fwd", out_shape=(shape, shape, shape), grid=(t // tr,),
        in_specs=[col(q_blk), col(f_blk), pl.BlockSpec((2, width), lambda i: (0, 0))],
        out_specs=(own, own, own), compiler_params=_params(("parallel",)),
    )(proj, proj, lb_logits)


def _hgrn_prep_bwd(proj, lb_logits, dq, dk, dlf, h, q_blk, f_blk):
    t = proj.shape[0]
    tr = _tile(t, 512)

    def body(xq, xf, lb_ref, dq_ref, dk_ref, dlf_ref, dxq, dxf, dlb_ref):
        @pl.when(pl.program_id(0) == 0)
        def _():
            dlb_ref[...] = jnp.zeros_like(dlb_ref)

        lb = _lower_bound(lb_ref)
        s = _sigmoid(xf[...])
        e = dlf_ref[...] / (lb + (1.0 - lb) * s) - dk_ref[...]
        dxq[...] = (dq_ref[...] * _dsilu(xq[...])).astype(BF16)
        dxf[...] = (s * (1.0 - s) * (1.0 - lb) * e).astype(BF16)
        d0 = jnp.sum((1.0 - s) * e, axis=0, keepdims=True) * (lb * (1.0 - lb))
        dlb_ref[pl.ds(0, 1), :] += d0
        dlb_ref[pl.ds(1, 1), :] += -d0

    width = h * HEAD_DIM
    col = lambda b0: pl.BlockSpec((tr, width), lambda i: (i, b0 // h))
    own = pl.BlockSpec((tr, width), lambda i: (i, 0))
    lbs = pl.BlockSpec((2, width), lambda i: (0, 0))
    shape = jax.ShapeDtypeStruct((t, width), BF16)
    return pl.pallas_call(
        body, name="hgrn_prep_bwd",
        out_shape=(shape, shape, jax.ShapeDtypeStruct((2, width), F32)), grid=(t // tr,),
        in_specs=[col(q_blk), col(f_blk), lbs, own, own, own], out_specs=(own, own, lbs),
        compiler_params=_params(("arbitrary",)),
    )(proj, proj, lb_logits, dq, dk, dlf)


GATE_HEADS = 4


def _gate_specs(h, z_blk, g_blk, tr):
    g = min(GATE_HEADS, h)
    n = h // g
    width = g * HEAD_DIM
    o_a = pl.BlockSpec((tr, width), lambda gg, i: (i, jnp.minimum(gg, n - 1)))
    o_b = pl.BlockSpec((tr, width), lambda gg, i: (i, jnp.maximum(gg - n, 0)))
    gate = pl.BlockSpec((tr, width), lambda gg, i: (i, jnp.where(gg < n, z_blk // g + gg, g_blk // g + gg - n)))
    w = pl.BlockSpec((None, 1, HEAD_DIM), lambda gg, i: (gg // n, 0, 0))
    cat = pl.BlockSpec((tr, width), lambda gg, i: (i, gg))
    return (o_a, o_b, gate, w, cat), g, n


def _silu_and_grad(x):
    s = _sigmoid(x)
    return x * s, s * (1.0 + x * (1.0 - s))


def _gate_fwd(o_a, o_b, proj, norm_w, h, z_blk, g_blk):
    t = o_a.shape[0]
    tr = _tile(t, 512)

    (sa, sb, sg, sw, cat), g, n = _gate_specs(h, z_blk, g_blk, tr)

    def body(oa_ref, ob_ref, z_ref, w_ref, y_ref):
        for k in range(g):
            lanes = pl.ds(k * HEAD_DIM, HEAD_DIM)
            o = jnp.where(pl.program_id(0) < n, oa_ref[:, lanes], ob_ref[:, lanes])
            r = lax.rsqrt(jnp.mean(o * o, axis=-1, keepdims=True) + NORM_EPS)
            y_ref[:, lanes] = (o * r * w_ref[...] * _silu(z_ref[:, lanes])).astype(y_ref.dtype)

    return pl.pallas_call(
        body, name="gate_fwd", out_shape=jax.ShapeDtypeStruct((t, 2 * h * HEAD_DIM), BF16),
        grid=(2 * n, t // tr), in_specs=[sa, sb, sg, sw], out_specs=cat,
        compiler_params=_params(("parallel", "parallel")),
    )(o_a, o_b, proj, norm_w)


def _gate_bwd(o_a, o_b, proj, norm_w, dy, h, z_blk, g_blk):
    t = o_a.shape[0]
    tr = _tile(t, 512)

    (sa, sb, sg, sw, cat), g, n = _gate_specs(h, z_blk, g_blk, tr)

    def body(oa_ref, ob_ref, z_ref, w_ref, dy_ref, do_ref, dz_ref, dw_ref):
        gg = pl.program_id(0)

        @pl.when(jnp.logical_and(gg % n == 0, pl.program_id(1) == 0))
        def _():
            dw_ref[...] = jnp.zeros_like(dw_ref)

        w = w_ref[...]
        dw = jnp.zeros_like(w)
        for k in range(g):
            lanes = pl.ds(k * HEAD_DIM, HEAD_DIM)
            o = jnp.where(gg < n, oa_ref[:, lanes], ob_ref[:, lanes])
            dyv = dy_ref[:, lanes]
            r = lax.rsqrt(jnp.mean(o * o, axis=-1, keepdims=True) + NORM_EPS)
            oh = o * r
            act, dact = _silu_and_grad(z_ref[:, lanes])
            dz_ref[:, lanes] = (dyv * oh * w * dact).astype(dz_ref.dtype)
            dn = dyv * act
            doh = dn * w
            do_ref[:, lanes] = r * (doh - oh * jnp.mean(doh * oh, axis=-1, keepdims=True))
            dw = dw + jnp.sum(dn * oh, axis=0, keepdims=True)
        dw_ref[...] += dw

    width = 2 * h * HEAD_DIM
    return pl.pallas_call(
        body, name="gate_bwd",
        out_shape=(jax.ShapeDtypeStruct((t, width), F32), jax.ShapeDtypeStruct((t, width), BF16),
                   jax.ShapeDtypeStruct((2, 1, HEAD_DIM), F32)),
        grid=(2 * n, t // tr), in_specs=[sa, sb, sg, sw, cat], out_specs=(cat, cat, sw),
        compiler_params=_params(("arbitrary", "arbitrary")),
    )(o_a, o_b, proj, norm_w, dy)


def _lane_row(vec):
    return jnp.pad(vec.reshape(1, -1), ((0, 0), (0, HEAD_DIM - vec.shape[-1])))


def _add_epi(acc, res):
    return (acc + res,)


def _split_w_in(w_in, h):
    gw = h * HEAD_DIM
    main = jnp.concatenate([w_in[:, :4 * gw], w_in[:, 4 * gw + 2 * h:]], axis=1)
    pad = jnp.zeros((w_in.shape[0], HEAD_DIM - h), w_in.dtype)
    ab = jnp.concatenate([w_in[:, 4 * gw:4 * gw + h], pad, w_in[:, 4 * gw + h:4 * gw + 2 * h], pad], axis=1)
    return main, ab


def _merge_w_in(main, ab, h):
    gw = h * HEAD_DIM
    return jnp.concatenate([main[:, :4 * gw], ab[:, :h], ab[:, HEAD_DIM:HEAD_DIM + h], main[:, 4 * gw:]], axis=1)


def _local_step(x, target, w_main, w_ab, conv_w, a_log, dt_bias, gdn_norm_w, lb_logits, hgrn_norm_w,
                w_out, norm_mix_w, norm_ffn_w, w_ff1, w_ff2, norm_final_w, reducer=None, n1=None):
    t, d = x.shape
    h = d // (2 * HEAD_DIM)
    gw = h * HEAD_DIM
    k_blk, v_blk, z_blk, qb_blk, fb_blk, ib_blk, gb_blk = (i * h for i in range(1, 8))
    del k_blk, v_blk

    if n1 is None:
        n1 = _rms_fwd(x, norm_mix_w, "rms_mix")
    stacked = not hasattr(w_main, "ndim") or w_main.ndim == 3
    if stacked and not hasattr(w_main, "ndim"):
        early, late_parts, edges, chip = w_main
        proj = _mm_quarters(n1, early, jnp.stack([chip, chip ^ 2, chip ^ 1]), "in_proj_early")
        late, _ = _in_proj_quarters(lax.optimization_barrier((late_parts, proj))[0], edges, h)
        proj = _mm_quarters(n1, late, jnp.stack([chip ^ 3]), "in_proj_late", into=proj)
        w_main = lax.dynamic_update_index_in_dim(
            early, lax.dynamic_index_in_dim(late, chip ^ 3, 0, keepdims=False), chip ^ 3, 0)
    else:
        proj = _mm(n1, w_main, "nn", (F32,), "in_proj", b_stacked=stacked)
    ab = _mm(n1, w_ab, "nn", (F32,), "in_proj_ab")

    q, k, v = _gdn_prep_fwd(proj, conv_w, h)
    a_log_row, dt_row = _lane_row(a_log), _lane_row(dt_bias)
    g_tm, beta_tm = _gdn_gates_fwd(ab, a_log_row, dt_row)
    to_heads = lambda a: jnp.broadcast_to(a[:, :h].T[:, :, None], (h, t, HEAD_DIM))
    g_bc, beta_bc = to_heads(g_tm), to_heads(beta_tm)
    o_a, st_a, inv_a = _gdn_fwd(q, k, v, beta_bc, g_bc)

    qh, kh, lf = _hgrn_prep_fwd(proj, lb_logits, h, qb_blk, fb_blk)
    o_b, st_b = _hgrn_fwd(qh, kh, proj, lf, v_blk=ib_blk)

    gate_w = jnp.stack([gdn_norm_w.reshape(1, HEAD_DIM), hgrn_norm_w.reshape(1, HEAD_DIM)])
    y = _gate_fwd(o_a, o_b, proj, gate_w, h, z_blk, gb_blk)
    if callable(w_out):
        w_out, w_ff1, w_ff2 = w_out(y)
    h1 = _mm(y, w_out, "nn", (F32,), "out_proj", epi=_add_epi, extras=(x,))
    n2 = _rms_fwd(h1, norm_ffn_w, "rms_ffn")
    act, r = _mm(n2, w_ff1, "nn", (F32, BF16), "ff1", b_stacked=True,
                 epi=lambda acc: (acc, jnp.square(jnp.maximum(acc, 0.0))))
    h2 = _mm(r, w_ff2, "nn", (F32,), "ff2", epi=_add_epi, extras=(h1,))
    loss, dh2, dh2_b, d_norm_final = _loss_head(h2, norm_final_w, target)

    da = _mm(dh2_b, w_ff2, "nt", (BF16,), "ff2_dx",
             epi=lambda acc, a: (acc * (2.0 * jnp.maximum(a, 0.0)),), extras=(act,))
    pending = []

    def step(anchor, name=None, full=None):
        if reducer is not None:
            pending.extend(reducer.step(name, full, anchor))

    def after_step(value):
        if not pending:
            return value
        value = lax.optimization_barrier((value, *pending))[0]
        pending.clear()
        return value

    d_ff2 = _mm(r, dh2_b, "tn", (F32,), "ff2_dw")
    step(None, "w_ff2", d_ff2)
    dn2 = _mm(da, w_ff1, "nt", (F32,), "ff1_dx", b_stacked=True)
    d_ff1 = _mm(n2, da, "tn", (F32,), "ff1_dw", out_stacked=True)
    step(d_ff1, "w_ff1", d_ff1)
    dh1, dh1_b, d_norm_ffn = _rms_bwd(after_step(dn2), h1, norm_ffn_w, dh2, "rms_ffn_bwd")
    dy = _mm(dh1_b, w_out, "nt", (F32,), "out_proj_dx")
    d_out = _mm(y, dh1_b, "tn", (F32,), "out_proj_dw")
    step(d_out, "w_out", d_out)

    do, dgate, d_gate_w = _gate_bwd(o_a, o_b, proj, gate_w, after_step(dy), h, z_blk, gb_blk)
    step(do)
    dq, dk, dv, dbeta_bc, dg_bc = _gdn_bwd(q, k, v, beta_bc, g_bc, st_a, inv_a, after_step(do), do_blk=0)
    step(dq)
    dxq, dxk, dxv, dcq, dck, dcv = _gdn_prep_bwd(proj, conv_w, after_step(dq), dk, dv, h)
    step(dxq)
    from_heads = lambda a: jnp.pad(a[:, :, 0, :].reshape(h, t).T, ((0, 0), (0, HEAD_DIM - h)))
    dab, d_a_log, d_dt_bias = _gdn_gates_bwd(ab, a_log_row, dt_row, from_heads(dg_bc), from_heads(dbeta_bc))
    dqh, dkh, dvh, dlf = _hgrn_bwd(after_step(qh), kh, proj, lf, st_b, do, v_blk=ib_blk, do_blk=h)
    step(dqh)
    dxqb, dxfb, d_lb = _hgrn_prep_bwd(proj, lb_logits, dqh, dkh, dlf, h, qb_blk, fb_blk)

    dproj = jnp.concatenate([after_step(dxq), dxk, dxv, dgate[:, :gw], dxqb, dxfb, dvh.astype(BF16), dgate[:, gw:]],
                            axis=1)
    d_main = _mm(n1, dproj, "tn", (F32,), "in_proj_dw", out_stacked=True)
    d_ab = _mm(n1, dab, "tn", (F32,), "in_proj_ab_dw")
    step(d_main, "w_in", d_main)
    dn1_ab = _mm(after_step(dab), w_ab, "nt", (F32,), "in_proj_ab_dx")
    step(dn1_ab)
    dn1 = _mm(after_step(dproj), w_main, "nt", (F32,), "in_proj_dx", epi=_add_epi, extras=(dn1_ab,),
              b_stacked=stacked)
    step(dn1)
    dx, _, d_norm_mix = _rms_bwd(after_step(dn1), x, norm_mix_w, dh1, "rms_mix_bwd")
    step(dx)

    grads = dict(
        w_main=d_main, w_ab=d_ab, conv_w=jnp.concatenate([dcq, dck, dcv], axis=1),
        gdn_a_log=d_a_log[:, :h], gdn_dt_bias=d_dt_bias[:, :h], gdn_norm_w=d_gate_w[0],
        hgrn_lb_logits=d_lb, hgrn_norm_w=d_gate_w[1], w_out=d_out, norm_mix_w=d_norm_mix,
        norm_ffn_w=d_norm_ffn, w_ff1=d_ff1, w_ff2=d_ff2, norm_final_w=d_norm_final)
    return loss, dx, grads


N_CHIPS = 4
ANY = pl.BlockSpec(memory_space=pl.ANY)


def _place():
    x, y, c = lax.axis_index("x"), lax.axis_index("y"), lax.axis_index("c")
    chips = [(1 - x, y), (x, 1 - y), (1 - x, 1 - y)]
    return x, y, c, chips


def _remote(src, dst, send_sems, recv_sems, k, to):
    return pltpu.make_async_remote_copy(src_ref=src, dst_ref=dst, send_sem=send_sems.at[k],
                                        recv_sem=recv_sems.at[k], device_id=to, device_id_type=MESH)


def _to_sibling(x, y, c, chips):
    return [(x, y, 1 - c)]


def _to_same_core_of_chips(x, y, c, chips):
    return [(*chip, c) for chip in chips]


def _to_all_gather_peers(x, y, c, chips):
    return _to_sibling(x, y, c, chips) + _to_same_core_of_chips(x, y, c, chips)


SIBLING_EXCHANGE = (1, _to_sibling)
CHIP_EXCHANGE = (2, _to_same_core_of_chips)
GATHER_EXCHANGE = (3, _to_all_gather_peers)
DIAGONAL_EXCHANGE = (4, lambda x, y, c, chips: [(x, y, 1 - c), (*chips[2], c)])


def _launch(body, name, out_shapes, arrays, sem_counts, sequencer=None, after=()):
    n, n_after = len(arrays), len(after)
    sems = [pltpu.SemaphoreType.DMA((k,)) for k in sem_counts]
    strip = lambda refs: refs[:n] + refs[n + n_after:]
    if sequencer is None:
        return pl.pallas_call(
            lambda *refs: body(*strip(refs)), name=name, out_shape=tuple(out_shapes),
            in_specs=[ANY] * (n + n_after), out_specs=tuple(ANY for _ in out_shapes), scratch_shapes=sems,
        )(*arrays, *after)
    collective_id, peers = sequencer

    def sequencer_body(*refs):
        x, y, c, chips = _place()
        barrier = pltpu.get_barrier_semaphore()
        targets = peers(x, y, c, chips)
        for target in targets:
            pl.semaphore_signal(barrier, inc=1, device_id=target, device_id_type=MESH)
        pl.semaphore_wait(barrier, len(targets))
        body(*strip(refs))

    return pl.kernel(
        sequencer_body, name=name, out_type=tuple(out_shapes),
        mesh=plsc.ScalarSubcoreMesh(axis_name="sequencer", num_cores=1), scratch_types=tuple(sems),
        compiler_params=pltpu.CompilerParams(collective_id=collective_id),
    )(*arrays, *after)


def _gather_weights(big, small, name, sequencer=None, after=(), relations=(0, 1, 2)):
    nb, ns = len(big), len(small)
    n_sem = 6 * nb + 3 * ns

    def body(*refs):
        ins, outs = refs[:nb + ns], refs[nb + ns:2 * (nb + ns)]
        send_sems, recv_sems = refs[2 * (nb + ns):]
        x, y, c, chips = _place()
        me, sibling = 2 * x + y, (x, y, 1 - c)

        def half(a, chip, hc):
            rh = big[a].shape[0] // 2
            return outs[a].at[2 * chip[0] + chip[1], pl.ds(hc * rh, rh), :]

        first, passed = [], []
        for a in range(nb):
            rh = big[a].shape[0] // 2
            for j, chip in [(j, chips[j]) for j in relations]:
                first.append(_remote(ins[a].at[pl.ds(c * rh, rh), :], half(a, (x, y), c),
                                     send_sems, recv_sems, 6 * a + j, (*chip, c)))
        for s in range(ns):
            for j, chip in enumerate(chips):
                first.append(_remote(ins[nb + s], outs[nb + s].at[me], send_sems, recv_sems,
                                     6 * nb + 3 * s + j, (*chip, c)))
        for cp in first:
            cp.start()
        for a in range(nb):
            for j, chip in [(j, chips[j]) for j in relations]:
                _remote(half(a, chip, c), half(a, chip, c), send_sems, recv_sems, 6 * a + j, (*chip, c)).wait_recv()
                fwd = _remote(half(a, chip, c), half(a, chip, c), send_sems, recv_sems, 6 * a + 3 + j, sibling)
                fwd.start()
                passed.append(fwd)
        for s in range(ns):
            for j, chip in enumerate(chips):
                dst = outs[nb + s].at[2 * chip[0] + chip[1]]
                _remote(dst, dst, send_sems, recv_sems, 6 * nb + 3 * s + j, (*chip, c)).wait_recv()
        for a in range(nb):
            for j, chip in [(j, chips[j]) for j in relations]:
                _remote(half(a, chip, 1 - c), half(a, chip, 1 - c), send_sems, recv_sems,
                        6 * a + 3 + j, sibling).wait_recv()
        for cp in first + passed:
            cp.wait_send()

    arrays = list(big) + list(small)
    out_shapes = [jax.ShapeDtypeStruct((N_CHIPS,) + a.shape, a.dtype) for a in arrays]
    return _launch(body, name, out_shapes, arrays, (n_sem, n_sem), sequencer, after)


def _swap_halves(parts, name, sequencer=None):
    n = len(parts)

    def body(*refs):
        ins, outs = refs[:n], refs[n:2 * n]
        send_sems, recv_sems = refs[2 * n:]
        x, y, c, _ = _place()
        copies = [_remote(ins[a].at[s, 1 - c], outs[a].at[s], send_sems, recv_sems, N_CHIPS * a + s, (x, y, 1 - c))
                  for a in range(n) for s in range(N_CHIPS)]
        for cp in copies:
            cp.start()
        for cp in copies:
            cp.wait()

    out_shapes = [jax.ShapeDtypeStruct((N_CHIPS,) + p.shape[2:], p.dtype) for p in parts]
    return _launch(body, name, out_shapes, parts, (N_CHIPS * n, N_CHIPS * n), sequencer)


def _scatter_to_owners(parts, name, sequencer=None):
    n = len(parts)

    def body(*refs):
        ins, outs = refs[:n], refs[n:2 * n]
        send_sems, recv_sems = refs[2 * n:]
        x, y, c, chips = _place()
        copies = [_remote(ins[a].at[2 * chip[0] + chip[1]], outs[a].at[j], send_sems, recv_sems,
                          3 * a + j, (*chip, c))
                  for a in range(n) for j, chip in enumerate(chips)]
        for cp in copies:
            cp.start()
        for cp in copies:
            cp.wait()

    out_shapes = [jax.ShapeDtypeStruct((3,) + p.shape[1:], p.dtype) for p in parts]
    return _launch(body, name, out_shapes, parts, (3 * n, 3 * n), sequencer)


def _send_to_sibling(halves, name, sequencer=None):
    n = len(halves)

    def body(*refs):
        ins, outs = refs[:n], refs[n:2 * n]
        send_sems, recv_sems = refs[2 * n:]
        x, y, c, _ = _place()
        copies = [_remote(ins[a], outs[a], send_sems, recv_sems, a, (x, y, 1 - c)) for a in range(n)]
        for cp in copies:
            cp.start()
        for cp in copies:
            cp.wait()

    out_shapes = [jax.ShapeDtypeStruct(p.shape, p.dtype) for p in halves]
    return _launch(body, name, out_shapes, halves, (n, n), sequencer)


N_DEV = 8


def _all_reduce_small(vec):
    def body(v_ref, gathered, total, send_sems, recv_sems):
        x, y, c, _ = _place()
        me = 4 * x + 2 * y + c
        gathered[me] = v_ref[...]
        copies = []
        for k in range(1, N_DEV):
            px = 1 - x if k & 4 else x
            py = 1 - y if k & 2 else y
            pc = 1 - c if k & 1 else c
            copies.append(_remote(v_ref, gathered.at[me], send_sems, recv_sems, k - 1, (px, py, pc)))
        for cp in copies:
            cp.start()
        for k, cp in enumerate(copies):
            cp.wait_send()
        for k in range(1, N_DEV):
            px = 1 - x if k & 4 else x
            py = 1 - y if k & 2 else y
            pc = 1 - c if k & 1 else c
            src = gathered.at[4 * px + 2 * py + pc]
            _remote(src, src, send_sems, recv_sems, k - 1, (px, py, pc)).wait_recv()
        acc = gathered[0]
        for dev in range(1, N_DEV):
            acc = acc + gathered[dev]
        total[...] = acc

    vm = pl.BlockSpec(memory_space=pltpu.VMEM)
    return pl.pallas_call(
        body, name="all_reduce_small",
        out_shape=(jax.ShapeDtypeStruct((N_DEV,) + vec.shape, F32), jax.ShapeDtypeStruct(vec.shape, F32)),
        in_specs=[vm], out_specs=(vm, vm),
        scratch_shapes=[pltpu.SemaphoreType.DMA((N_DEV - 1,)), pltpu.SemaphoreType.DMA((N_DEV - 1,))],
    )(vec)[1]


def _chip_sum(part, recv, c, chip):
    _, _, rh, cols = part.shape
    tr = _tile(rh, 256)

    def body(c_ref, chip_ref, p_ref, r_ref, own_ref, sb_ref):
        s = p_ref[...] + r_ref[...]
        sb_ref[...] = s.astype(BF16)

        @pl.when(pl.program_id(1) == chip_ref[0])
        def _():
            own_ref[...] = s

    blk = pl.BlockSpec((None, tr, cols), lambda i, s, c_ref, chip_ref: (s, i, 0))
    return pl.pallas_call(
        body, name="grad_chip_sum",
        out_shape=(jax.ShapeDtypeStruct(recv.shape[1:], F32), jax.ShapeDtypeStruct(recv.shape, BF16)),
        grid_spec=pltpu.PrefetchScalarGridSpec(
            num_scalar_prefetch=2, grid=(rh // tr, N_CHIPS),
            in_specs=[pl.BlockSpec((None, None, tr, cols), lambda i, s, c_ref, chip_ref: (s, c_ref[0], i, 0)), blk],
            out_specs=(pl.BlockSpec((tr, cols), lambda i, s, c_ref, chip_ref: (i, 0)), blk)),
        compiler_params=_params(("parallel", "arbitrary")),
    )(c, chip, part, recv)


def _owner_sum(own, recv):
    rh, cols = own.shape
    tr = _tile(rh, 256)

    def body(o_ref, r0, r1, r2, g_ref):
        g_ref[...] = ((o_ref[...] + r0[...].astype(F32)) + r1[...].astype(F32)) + r2[...].astype(F32)

    slot = lambda j: pl.BlockSpec((None, tr, cols), lambda i: (j, i, 0))
    row = pl.BlockSpec((tr, cols), lambda i: (i, 0))
    return pl.pallas_call(
        body, name="grad_owner_sum", out_shape=jax.ShapeDtypeStruct((rh, cols), F32), grid=(rh // tr,),
        in_specs=[row, slot(0), slot(1), slot(2)], out_specs=row,
        compiler_params=_params(("parallel",)),
    )(own, recv, recv, recv)


def _adamw_math(w, g, m, v):
    c1 = 1.0 / (1.0 - ADAM_B1 ** ADAM_STEP)
    c2 = 1.0 / (1.0 - ADAM_B2 ** ADAM_STEP)
    nm = ADAM_B1 * m + (1.0 - ADAM_B1) * g
    nv = ADAM_B2 * v + (1.0 - ADAM_B2) * (g * g)
    return -ADAM_LR * ((nm * c1) / (jnp.sqrt(nv * c2) + ADAM_EPS) + ADAM_WD * w), nm, nv


def _adamw_unit_rows(w, g, m, v, name):
    rows, _, cols = w.shape
    tr = max(d for d in range(1, 33) if rows % d == 0)

    def body(w_ref, g_ref, m_ref, v_ref, d_ref, nm_ref, nv_ref):
        d_ref[...], nm_ref[...], nv_ref[...] = _adamw_math(w_ref[...], g_ref[...], m_ref[...], v_ref[...])

    blk = pl.BlockSpec((tr, 1, cols), lambda i: (i, 0, 0))
    shape = jax.ShapeDtypeStruct(w.shape, F32)
    return pl.pallas_call(
        body, name=name, out_shape=(shape, shape, shape), grid=(rows // tr,),
        in_specs=[blk, blk, blk, blk], out_specs=(blk, blk, blk),
        compiler_params=_params(("parallel",)),
    )(w, g, m, v)


def _divisor_tile(n, want):
    return max(d for d in range(ROW_TILE, want + 1, ROW_TILE) if n % d == 0)


def _adamw(w, g, m, v, name):
    if w.ndim == 3:
        return _adamw_unit_rows(w, g, m, v, name)
    rows, cols = w.shape
    tr = _divisor_tile(rows, 2048) if rows % 8 == 0 else rows
    c1 = 1.0 / (1.0 - ADAM_B1 ** ADAM_STEP)
    c2 = 1.0 / (1.0 - ADAM_B2 ** ADAM_STEP)

    def body(w_ref, g_ref, m_ref, v_ref, d_ref, nm_ref, nv_ref):
        gv = g_ref[...]
        nm = ADAM_B1 * m_ref[...] + (1.0 - ADAM_B1) * gv
        nv = ADAM_B2 * v_ref[...] + (1.0 - ADAM_B2) * (gv * gv)
        d_ref[...] = -ADAM_LR * ((nm * c1) / (jnp.sqrt(nv * c2) + ADAM_EPS) + ADAM_WD * w_ref[...])
        nm_ref[...] = nm
        nv_ref[...] = nv

    blk = pl.BlockSpec((tr, cols), lambda i: (i, 0))
    shape = jax.ShapeDtypeStruct((rows, cols), F32)
    return pl.pallas_call(
        body, name=name, out_shape=(shape, shape, shape), grid=(rows // tr,),
        in_specs=[blk, blk, blk, blk], out_specs=(blk, blk, blk),
        compiler_params=_params(("parallel",)),
    )(w, g, m, v)


def _adamw_halves(w, g_own, g_sib, m, v, c, name):
    _, rows, cols = w.shape
    rh = rows // 2
    tr = _tile(rh, 256)
    per = rh // tr
    c1 = 1.0 / (1.0 - ADAM_B1 ** ADAM_STEP)
    c2 = 1.0 / (1.0 - ADAM_B2 ** ADAM_STEP)

    def body(c_ref, w_ref, go_ref, gs_ref, m_ref, v_ref, g_ref, d_ref, nm_ref, nv_ref):
        own = pl.program_id(0) // per == c_ref[0]
        gv = jnp.where(own, go_ref[...], gs_ref[...])
        nm = ADAM_B1 * m_ref[...] + (1.0 - ADAM_B1) * gv
        nv = ADAM_B2 * v_ref[...] + (1.0 - ADAM_B2) * (gv * gv)
        g_ref[...] = gv
        d_ref[...] = -ADAM_LR * ((nm * c1) / (jnp.sqrt(nv * c2) + ADAM_EPS) + ADAM_WD * w_ref[...])
        nm_ref[...] = nm
        nv_ref[...] = nv

    blk = pl.BlockSpec((None, tr, cols), lambda i, c_ref: (0, i, 0))
    half = pl.BlockSpec((tr, cols), lambda i, c_ref: (i % per, 0))
    shape = jax.ShapeDtypeStruct((1, rows, cols), F32)
    return pl.pallas_call(
        body, name=name, out_shape=(shape, shape, shape, shape),
        grid_spec=pltpu.PrefetchScalarGridSpec(
            num_scalar_prefetch=1, grid=(rows // tr,),
            in_specs=[blk, half, half, blk, blk], out_specs=(blk, blk, blk, blk)),
        compiler_params=_params(("parallel",)),
    )(c, w, g_own, g_sib, m, v)


def _by_shard(name, full):
    if name in ("w_in", "w_ff1"):
        st = full
    else:
        st = full.reshape(N_CHIPS, -1, full.shape[1])
    return st.reshape(N_CHIPS, 2, st.shape[1] // 2, st.shape[2])


class _GradReducer:
    def __init__(self, w, m, v, my_c, my_chip):
        self.w, self.m, self.v, self.my_c, self.my_chip = w, m, v, my_c, my_chip
        self.in_flight = []
        self.computed = []
        self.anchor = None
        self.done = {}

    def step(self, name=None, full=None, anchor=None):
        stages, self.in_flight, self.computed, self.anchor = self.in_flight, [], [], anchor
        for stage in [s for s in stages if not getattr(s, "long", False)]:
            self._advance(stage)
        if name is not None:
            self.in_flight.append(self._swap(name, _by_shard(name, full)))
        for stage in [s for s in stages if getattr(s, "long", False)]:
            self._advance(stage)
        return self.computed

    def _held(self, value):
        if self.anchor is None:
            return value
        return lax.optimization_barrier((value, self.anchor))[0]

    def _advance(self, stage):
        nxt = stage()
        if nxt is not None:
            self.in_flight.append(nxt)

    def finish(self):
        while self.in_flight:
            self.step()
        return self.done

    def _swap(self, name, part):
        got, = _swap_halves([part], "grad_swap_" + name, SIBLING_EXCHANGE)

        def scatter():
            total, total_bf16 = _chip_sum(part, self._held(got), self.my_c, self.my_chip.reshape(1))
            self.computed.append(total_bf16)
            recv, = _scatter_to_owners([total_bf16], "grad_scatter_" + name, CHIP_EXCHANGE)

            def send():
                half = _owner_sum(total, self._held(recv))
                self.computed.append(half)
                sib, = _send_to_sibling([half], "grad_send_" + name, SIBLING_EXCHANGE)

                def update():
                    self.done[name] = _adamw_halves(self.w[name], half, self._held(sib), self.m[name], self.v[name],
                                                    self.my_c, "adamw_" + name)
                    self.computed.append(self.done[name][0])

                if name == "w_in":
                    self.in_proj_halves = (half, sib)
                    return None
                return update
            return lambda: send
        scatter.long = True
        return scatter


def _adamw_minor_rows(w, g, m, v, name):
    _, rows, cols = w.shape
    turned = lambda a: jnp.transpose(a, (2, 0, 1))
    back = lambda a: jnp.transpose(a, (1, 2, 0))
    g = g.T.reshape(cols, 1, rows)
    delta, new_m, new_v = _adamw(turned(w), g, turned(m), turned(v), name)
    return back(g), back(delta), back(new_m), back(new_v)


def _in_proj_quarter_part(shard, chip, h):
    e = h // 2
    qw = shard.shape[1] - e
    zeros = jnp.zeros((shard.shape[0], h), shard.dtype)
    padded = jnp.concatenate([zeros, shard, zeros], axis=1)
    at_quarter = jnp.where(chip == 0, h, jnp.where(chip == 1, h - e, jnp.where(chip == 2, 2 * h, h + e)))
    at_edge = jnp.where(chip == 0, h + qw, jnp.where(chip == 1, h + qw - e, jnp.where(chip == 2, h, h - e)))
    return (lax.dynamic_slice_in_dim(padded, at_quarter, qw, axis=1).astype(BF16),
            lax.dynamic_slice_in_dim(padded, at_edge, h, axis=1))


def _in_proj_quarters(parts, edges, h):
    e = h // 2
    x1, a, b, x2 = edges[0][:, :e], edges[1], edges[2], edges[3][:, e:]
    parts = parts.at[1, :, :e].set(x1.astype(BF16))
    parts = parts.at[2, :, parts.shape[2] - e:].set(x2.astype(BF16))
    pad = jnp.zeros((a.shape[0], HEAD_DIM - h), a.dtype)
    return parts, jnp.concatenate([a, pad, b, pad], axis=1).astype(BF16)


def _in_proj_edge_columns(d_main, d_ab, h):
    e = h // 2
    return jnp.concatenate([d_main[1][:, :e], d_main[2][:, -e:], d_ab[:, :h], d_ab[:, HEAD_DIM:HEAD_DIM + h]], axis=1)


def _in_proj_shard_grad(q_own, q_sib, edges, c, chip, h):
    e = h // 2
    lower, upper = jnp.where(c[0] == 0, q_own, q_sib), jnp.where(c[0] == 0, q_sib, q_own)
    quarter = jnp.concatenate([lower, upper], axis=0)
    x1, x2, a, b = edges[:, :e], edges[:, e:2 * e], edges[:, 2 * e:2 * e + h], edges[:, 2 * e + h:]
    zeros = jnp.zeros_like(x1)
    left = jnp.where(chip == 2, b, jnp.concatenate([zeros, x2], axis=1))
    right = jnp.where(chip == 1, a, jnp.concatenate([x1, zeros], axis=1))
    start = jnp.where(chip == 0, h, jnp.where(chip == 1, h + e, jnp.where(chip == 2, 0, h - e)))
    padded = jnp.concatenate([left, quarter, right], axis=1)
    return lax.dynamic_slice_in_dim(padded, start, quarter.shape[1] + e, axis=1)


SMALL = ("gdn_a_log", "gdn_dt_bias", "gdn_norm_w", "hgrn_lb_logits", "hgrn_norm_w",
         "norm_mix_w", "norm_ffn_w", "norm_final_w")
BIG = ("w_in", "w_out", "w_ff1", "w_ff2")
ORDER = ("w_in", "conv_w", "gdn_a_log", "gdn_dt_bias", "gdn_norm_w", "hgrn_lb_logits", "hgrn_norm_w",
         "w_out", "norm_mix_w", "norm_ffn_w", "w_ff1", "w_ff2", "norm_final_w")


def _pack(pieces):
    flat = jnp.concatenate([p.reshape(-1).astype(F32) for p in pieces])
    rows = -(-flat.shape[0] // (8 * HEAD_DIM)) * 8
    return jnp.pad(flat, (0, rows * HEAD_DIM - flat.shape[0])).reshape(rows, HEAD_DIM)


def _unpack(packed, shapes):
    flat, out, at = packed.reshape(-1), [], 0
    for s in shapes:
        n = 1
        for dim in s:
            n *= dim
        out.append(flat[at:at + n].reshape(s))
        at += n
    return out


def kernel(x, w_in, conv_w, gdn_a_log, gdn_dt_bias, gdn_norm_w, hgrn_lb_logits, hgrn_norm_w, w_out, norm_mix_w, norm_ffn_w, w_ff1, w_ff2, norm_final_w, loss_target, m_w_in, m_conv_w, m_gdn_a_log, m_gdn_dt_bias, m_gdn_norm_w, m_hgrn_lb_logits, m_hgrn_norm_w, m_w_out, m_norm_mix_w, m_norm_ffn_w, m_w_ff1, m_w_ff2, m_norm_final_w, v_w_in, v_conv_w, v_gdn_a_log, v_gdn_dt_bias, v_gdn_norm_w, v_hgrn_lb_logits, v_hgrn_norm_w, v_w_out, v_norm_mix_w, v_norm_ffn_w, v_w_ff1, v_w_ff2, v_norm_final_w):
    w = dict(w_in=w_in, conv_w=conv_w, gdn_a_log=gdn_a_log, gdn_dt_bias=gdn_dt_bias, gdn_norm_w=gdn_norm_w,
             hgrn_lb_logits=hgrn_lb_logits, hgrn_norm_w=hgrn_norm_w, w_out=w_out, norm_mix_w=norm_mix_w,
             norm_ffn_w=norm_ffn_w, w_ff1=w_ff1, w_ff2=w_ff2, norm_final_w=norm_final_w)
    m = dict(w_in=m_w_in, conv_w=m_conv_w, gdn_a_log=m_gdn_a_log, gdn_dt_bias=m_gdn_dt_bias,
             gdn_norm_w=m_gdn_norm_w, hgrn_lb_logits=m_hgrn_lb_logits, hgrn_norm_w=m_hgrn_norm_w,
             w_out=m_w_out, norm_mix_w=m_norm_mix_w, norm_ffn_w=m_norm_ffn_w, w_ff1=m_w_ff1, w_ff2=m_w_ff2,
             norm_final_w=m_norm_final_w)
    v = dict(w_in=v_w_in, conv_w=v_conv_w, gdn_a_log=v_gdn_a_log, gdn_dt_bias=v_gdn_dt_bias,
             gdn_norm_w=v_gdn_norm_w, hgrn_lb_logits=v_hgrn_lb_logits, hgrn_norm_w=v_hgrn_norm_w,
             w_out=v_w_out, norm_mix_w=v_norm_mix_w, norm_ffn_w=v_norm_ffn_w, w_ff1=v_w_ff1, w_ff2=v_w_ff2,
             norm_final_w=v_norm_final_w)
    d = x.shape[-1]
    h = d // (2 * HEAD_DIM)
    my_c = lax.axis_index("c").astype(jnp.int32).reshape(1)
    my_chip = (2 * lax.axis_index("x") + lax.axis_index("y")).astype(jnp.int32)

    shards = [w[n][0].astype(BF16) for n in BIG]
    conv_shard = jnp.pad(conv_w[0], ((0, 8 - CONV_W), (0, 0)))
    quarter_part, edge_part = _in_proj_quarter_part(w_in[0], my_chip, h)
    first = _gather_weights([quarter_part], [conv_shard, edge_part], "gather_in_proj", GATHER_EXCHANGE,
                            relations=(0, 1))
    diagonal, = _gather_weights([quarter_part], [], "gather_in_proj_diagonal", DIAGONAL_EXCHANGE, relations=(2,))
    n1 = _rms_fwd(x[0], norm_mix_w[0], "rms_mix")
    gathered_in, n1, *shards[1:] = lax.optimization_barrier((first[0], n1, *shards[1:]))
    own_slot = lambda st, own: lax.dynamic_update_index_in_dim(st, own, my_chip, 0)
    f_conv, f_edges = own_slot(first[1], conv_shard), own_slot(first[2], edge_part)
    early, w_ab = _in_proj_quarters(own_slot(gathered_in, quarter_part), f_edges, h)
    w_main = (early, diagonal, f_edges, my_chip)
    cols = lambda st: st.transpose(1, 0, 2).reshape(st.shape[1], -1)
    conv_full = cols(f_conv[:, :CONV_W])
    rest = _gather_weights(shards[1:], [], "gather_rest", GATHER_EXCHANGE, after=[w_ab])

    def late_weights(anchor):
        held = lax.optimization_barrier((*rest, anchor))[:len(rest)]
        f_out, f_ff1, f_ff2 = (own_slot(st, own) for st, own in zip(held, shards[1:]))
        return f_out.reshape(-1, d), f_ff1, f_ff2.reshape(-1, d)

    reducer = _GradReducer(w, m, v, my_c, my_chip)
    loss, dx, g = _local_step(
        x[0], loss_target[0], w_main, w_ab, conv_full, gdn_a_log[0], gdn_dt_bias[0], gdn_norm_w[0],
        hgrn_lb_logits, hgrn_norm_w[0], late_weights, norm_mix_w[0], norm_ffn_w[0],
        None, None, norm_final_w, reducer, n1)

    grads, delta, new_m, new_v = {}, {}, {}, {}
    for n, out in reducer.finish().items():
        grads[n], delta[n], new_m[n], new_v[n] = out

    edges = _in_proj_edge_columns(g["w_main"], g["w_ab"], h)
    small_shapes = [w[n].shape for n in SMALL] + [conv_full.shape, (1,), edges.shape]
    total = _all_reduce_small(_pack([g[n] for n in SMALL] + [g["conv_w"], loss[0, :1], edges]))
    *small_grads, conv_grad, loss_sum, edges = _unpack(total, small_shapes)
    g_in = _in_proj_shard_grad(*reducer.in_proj_halves, edges, my_c, my_chip, h)
    grads["w_in"], delta["w_in"], new_m["w_in"], new_v["w_in"] = _adamw_minor_rows(
        w["w_in"], g_in, m["w_in"], v["w_in"], "adamw_w_in")
    for n, sg in zip(SMALL, small_grads):
        grads[n] = sg
    shard_cols = conv_w.shape[-1]
    grads["conv_w"] = lax.dynamic_slice_in_dim(conv_grad, my_chip * shard_cols, shard_cols, axis=1)[None]

    packed_names = SMALL + ("conv_w",)
    packed = [_pack([t[n] for n in packed_names]) for t in (w, grads, m, v)]
    outs = _adamw(*packed, "adamw_small")
    shapes = [w[n].shape for n in packed_names]
    for res, o in zip((delta, new_m, new_v), outs):
        for n, a in zip(packed_names, _unpack(o, shapes)):
            res[n] = a

    return (loss_sum.reshape(()), dx[None], *[grads[n] for n in ORDER], *[delta[n] for n in ORDER],
            *[new_m[n] for n in ORDER], *[new_v[n] for n in ORDER])
```

```python
import functools

import jax
import jax.numpy as jnp
from jax import lax
from jax.experimental import pallas as pl
from jax.experimental.pallas import tpu as pltpu
from jax.experimental.pallas import tpu_sc as plsc

F32 = jnp.float32
BF16 = jnp.bfloat16

HEAD_DIM = 128
CHUNK = 128
SUB = 16
EXP_CAP = 80.0
NORM_EPS = 1e-6
L2_EPS = 1e-6
CONV_W = 4
VMEM_LIMIT = 56 * 1024 * 1024

ADAM_LR, ADAM_B1, ADAM_B2, ADAM_EPS, ADAM_WD, ADAM_STEP = 1e-3, 0.9, 0.999, 1e-8, 0.01, 10

NN = ((1,), (0,))
NT = ((1,), (1,))
TN = ((0,), (0,))
MESH = pl.DeviceIdType.MESH


def _dot(a, b, dims):
    return lax.dot_general(a.astype(BF16), b.astype(BF16), (dims, ((), ())),
                           preferred_element_type=F32)


def _split(a):
    hi = a.astype(BF16)
    return hi, (a - hi.astype(F32)).astype(BF16)


def _dot3(a, b, dims):
    ah, al = _split(a)
    bh, bl = _split(b)
    d = lambda x, y: lax.dot_general(x, y, (dims, ((), ())), preferred_element_type=F32)
    return d(ah, bh) + (d(ah, bl) + d(al, bh))


def _sigmoid(x):
    return 1.0 / (1.0 + jnp.exp(-x))


def _silu(x):
    return x * _sigmoid(x)


def _dsilu(x):
    s = _sigmoid(x)
    return s * (1.0 + x * (1.0 - s))


def _softplus(x):
    e = jnp.exp(-jnp.abs(x))
    u = 1.0 + e
    log1p = jnp.where(u == 1.0, e, jnp.log(u) * (e / jnp.where(u == 1.0, 1.0, u - 1.0)))
    return jnp.maximum(x, 0.0) + log1p


def _iota(shape, axis):
    return lax.broadcasted_iota(jnp.int32, shape, axis)


def _cumsum_rows(x):
    n = x.shape[0]
    row = _iota(x.shape, 0)
    s = 1
    while s < n:
        x = x + jnp.where(row >= s, pltpu.roll(x, s, 0), 0.0)
        s *= 2
    return x


def _rev_cumsum_rows(x):
    return jnp.sum(x, axis=0, keepdims=True) - _cumsum_rows(x) + x


def _params(sem):
    return pltpu.CompilerParams(dimension_semantics=sem, vmem_limit_bytes=VMEM_LIMIT)


ROW_TILE = 8
HEADS_PER_STEP = 8


def _hps(h):
    return min(HEADS_PER_STEP, h)


def _head_view(ref, hb):
    if len(ref.shape) == 2:
        return ref.at[:, pl.ds(hb * HEAD_DIM, HEAD_DIM)]
    return ref.at[hb]


class _Staged:
    def __init__(self, ref, load):
        self.ref = ref
        self.loaded = ref[...] if load else None
        self.written = None

    def __getitem__(self, idx):
        return self.loaded

    def __setitem__(self, idx, value):
        self.written = value


def _each_head(one_head, n_in):
    def body(*refs):
        @pl.when(pl.program_id(1) == 0)
        def _():
            refs[-1][...] = jnp.zeros_like(refs[-1])

        last = len(refs) - 1
        staged = [[_Staged(_head_view(r, hb), i < n_in or i == last) for i, r in enumerate(refs)]
                  for hb in range(refs[-1].shape[0])]
        running = [one_head(*per_head) for per_head in staged]
        while running:
            for gen in list(running):
                try:
                    next(gen)
                except StopIteration:
                    running.remove(gen)
        for per_head in staged:
            for s in per_head:
                if s.written is not None:
                    s.ref[...] = s.written
    return body


def _tile(n, want):
    t = min(n, want)
    while n % t:
        t //= 2
    return t


def _mm(a, b, mode, out_dtypes, name, epi=None, extras=(), tm=1024, tn=1024, tk=2048,
        b_stacked=False, out_stacked=False):
    if mode == "tn":
        kdim, m = a.shape
    else:
        m, kdim = a.shape
    if b_stacked:
        n = N_CHIPS * b.shape[2] if mode == "nn" else b.shape[1]
        kdim_b = b.shape[1] if mode == "nn" else N_CHIPS * b.shape[2]
        assert kdim_b == kdim
    else:
        n = b.shape[0] if mode == "nt" else b.shape[1]
    per_shard = (n if (mode == "nn" or out_stacked) else kdim) // N_CHIPS
    tm, tn, tk = _tile(m, tm), _tile(n, tn), _tile(kdim, tk)
    if (b_stacked and mode == "nn") or out_stacked:
        tn = _tile(per_shard, tn)
    if b_stacked and mode == "nt":
        tk = _tile(per_shard, tk)
    nk = kdim // tk
    dims = {"nn": NN, "nt": NT, "tn": TN}[mode]
    a_spec = (pl.BlockSpec((tk, tm), lambda i, j, k: (k, i)) if mode == "tn"
              else pl.BlockSpec((tm, tk), lambda i, j, k: (i, k)))
    if b_stacked and mode == "nn":
        per = per_shard // tn
        b_spec = pl.BlockSpec((None, tk, tn), lambda i, j, k: (j // per, k, j % per))
    elif b_stacked:
        per = per_shard // tk
        b_spec = pl.BlockSpec((None, tn, tk), lambda i, j, k: (k // per, j, k % per))
    else:
        b_spec = (pl.BlockSpec((tn, tk), lambda i, j, k: (j, k)) if mode == "nt"
                  else pl.BlockSpec((tk, tn), lambda i, j, k: (k, j)))
    mn_spec = pl.BlockSpec((tm, tn), lambda i, j, k: (i, j))
    if out_stacked:
        per_o = per_shard // tn
        out_spec = pl.BlockSpec((None, tm, tn), lambda i, j, k: (j // per_o, i, j % per_o))
        out_shape = (N_CHIPS, m, per_shard)
    else:
        out_spec, out_shape = mn_spec, (m, n)
    ne, no = len(extras), len(out_dtypes)
    if epi is None:
        epi = lambda acc: (acc,)

    def body(a_ref, b_ref, *rest):
        extra_refs, out_refs = rest[:ne], rest[ne:ne + no]
        part = _dot(a_ref[...], b_ref[...], dims)

        def finish(total):
            outs = epi(total, *[r[...] for r in extra_refs])
            for o_ref, o in zip(out_refs, outs):
                o_ref[...] = o.astype(o_ref.dtype)

        if nk == 1:
            finish(part)
            return
        acc = rest[-1]
        k = pl.program_id(2)

        @pl.when(k == 0)
        def _():
            acc[...] = part

        @pl.when(jnp.logical_and(k > 0, k < nk - 1))
        def _():
            acc[...] += part

        @pl.when(k == nk - 1)
        def _():
            finish(acc[...] + part)

    outs = pl.pallas_call(
        body, name=name,
        out_shape=tuple(jax.ShapeDtypeStruct(out_shape, d) for d in out_dtypes),
        grid=(m // tm, n // tn, nk),
        in_specs=[a_spec, b_spec] + [mn_spec] * ne,
        out_specs=tuple(out_spec for _ in out_dtypes),
        scratch_shapes=[pltpu.VMEM((tm, tn), F32)] if nk > 1 else [],
        compiler_params=_params(("parallel", "parallel", "arbitrary")),
    )(a, b, *extras)
    return outs if no > 1 else outs[0]


def _mm_quarters(a, stack, quarters, name, into=None, tm=1024, tn=1024):
    m, kdim = a.shape
    qw = stack.shape[2]
    tm, tn = _tile(m, tm), _tile(qw, tn)
    per = qw // tn

    def body(q_ref, a_ref, b_ref, *rest):
        rest[-1][...] = _dot(a_ref[...], b_ref[...], NN)

    in_specs = [pl.BlockSpec((tm, kdim), lambda i, j, q: (i, 0)),
                pl.BlockSpec((None, kdim, tn), lambda i, j, q: (q[j // per], 0, j % per))]
    args = (quarters, a, stack)
    if into is not None:
        in_specs.append(ANY)
        args += (into,)
    return pl.pallas_call(
        body, name=name, out_shape=jax.ShapeDtypeStruct((m, N_CHIPS * qw), F32),
        grid_spec=pltpu.PrefetchScalarGridSpec(
            num_scalar_prefetch=1, grid=(m // tm, quarters.shape[0] * per), in_specs=in_specs,
            out_specs=pl.BlockSpec((tm, tn), lambda i, j, q: (i, q[j // per] * per + j % per))),
        input_output_aliases={3: 0} if into is not None else {},
        compiler_params=_params(("parallel", "parallel")),
    )(*args)


ROWS = 256


def _rms_fwd(x, w, name):
    t, d = x.shape
    tr = _tile(t, ROWS)

    def body(x_ref, w_ref, n_ref):
        xv = x_ref[...]
        r = lax.rsqrt(jnp.mean(xv * xv, axis=-1, keepdims=True) + NORM_EPS)
        n_ref[...] = (xv * r * w_ref[...]).astype(n_ref.dtype)

    return pl.pallas_call(
        body, name=name, out_shape=jax.ShapeDtypeStruct((t, d), BF16), grid=(t // tr,),
        in_specs=[pl.BlockSpec((tr, d), lambda i: (i, 0)), pl.BlockSpec((1, d), lambda i: (0, 0))],
        out_specs=pl.BlockSpec((tr, d), lambda i: (i, 0)),
        compiler_params=_params(("parallel",)),
    )(x, w.reshape(1, d))


def _rms_bwd(dn, x, w, dres, name):
    t, d = x.shape
    tr = _tile(t, ROWS)

    def body(dn_ref, x_ref, w_ref, dres_ref, dx_ref, dxb_ref, dw_ref):
        i = pl.program_id(0)
        xv, dnv = x_ref[...], dn_ref[...]
        r = lax.rsqrt(jnp.mean(xv * xv, axis=-1, keepdims=True) + NORM_EPS)
        xh = xv * r
        dxh = dnv * w_ref[...]
        dx = dres_ref[...] + r * (dxh - xh * jnp.mean(dxh * xh, axis=-1, keepdims=True))
        dx_ref[...] = dx
        dxb_ref[...] = dx.astype(BF16)

        @pl.when(i == 0)
        def _():
            dw_ref[...] = jnp.zeros_like(dw_ref)

        dw_ref[...] += jnp.sum(dnv * xh, axis=0, keepdims=True)

    row = pl.BlockSpec((tr, d), lambda i: (i, 0))
    vec = pl.BlockSpec((1, d), lambda i: (0, 0))
    return pl.pallas_call(
        body, name=name,
        out_shape=(jax.ShapeDtypeStruct((t, d), F32), jax.ShapeDtypeStruct((t, d), BF16),
                   jax.ShapeDtypeStruct((1, d), F32)),
        grid=(t // tr,), in_specs=[row, row, vec, row], out_specs=(row, row, vec),
        compiler_params=_params(("arbitrary",)),
    )(dn, x, w.reshape(1, d), dres)


def _loss_head(h, w, target):
    t, d = h.shape
    tr = _tile(t, ROWS)

    def body(h_ref, w_ref, t_ref, loss_ref, dh_ref, dhb_ref, dw_ref):
        i = pl.program_id(0)
        hv, wv = h_ref[...], w_ref[...]
        r = lax.rsqrt(jnp.mean(hv * hv, axis=-1, keepdims=True) + NORM_EPS)
        hh = hv * r
        err = hh * wv - t_ref[...]
        dout = err * (1.0 / d)
        dhh = dout * wv
        dh = r * (dhh - hh * jnp.mean(dhh * hh, axis=-1, keepdims=True))
        dh_ref[...] = dh
        dhb_ref[...] = dh.astype(BF16)

        @pl.when(i == 0)
        def _():
            dw_ref[...] = jnp.zeros_like(dw_ref)
            loss_ref[...] = jnp.zeros_like(loss_ref)

        dw_ref[...] += jnp.sum(dout * hh, axis=0, keepdims=True)
        loss_ref[...] += jnp.full((1, 128), 0.5 / d, F32) * jnp.sum(err * err)

    row = pl.BlockSpec((tr, d), lambda i: (i, 0))
    vec = pl.BlockSpec((1, d), lambda i: (0, 0))
    lspec = pl.BlockSpec((1, 128), lambda i: (0, 0))
    return pl.pallas_call(
        body, name="loss_head",
        out_shape=(jax.ShapeDtypeStruct((1, 128), F32), jax.ShapeDtypeStruct((t, d), F32),
                   jax.ShapeDtypeStruct((t, d), BF16), jax.ShapeDtypeStruct((1, d), F32)),
        grid=(t // tr,), in_specs=[row, vec, row], out_specs=(lspec, row, row, vec),
        compiler_params=_params(("arbitrary",)),
    )(h, w.reshape(1, d), target)


def _inv_unit_lower(a):
    c = a.shape[0]
    eye = (_iota((c, c), 0) == _iota((c, c), 1)).astype(F32)
    x = eye - a
    p = _dot3(a, a, NN)
    yield
    n = 2
    while n < c:
        x = x + _dot3(x, p, NN)
        n *= 2
        if n < c:
            p = _dot3(p, p, NN)
        yield
    return x


def _gdn_chunk(q, k, v, beta, g):
    c = q.shape[0]
    row, col = _iota((c, c), 0), _iota((c, c), 1)
    gc = _cumsum_rows(g)
    diff = gc - gc.T
    dec = jnp.where(row >= col, jnp.exp(jnp.minimum(diff, 0.0)), 0.0)
    dec_s = jnp.where(row > col, dec, 0.0)
    gam = jnp.exp(gc)
    g_last = jnp.sum(g, axis=0, keepdims=True)
    kk = _dot(k, k, NT)
    a = beta * kk * dec_s
    p = _dot(q, k, NT) * dec
    e_end = jnp.exp(g_last - gc)
    return dict(dec=dec, dec_s=dec_s, gam=gam, gam_last=jnp.exp(g_last), e_end=e_end,
                k_end=k * e_end, kk=kk, a=a, p=p)


def _gdn_fwd(q, k, v, beta_bc, g_bc):
    t = q.shape[0]
    h = q.shape[1] // HEAD_DIM
    nc = t // CHUNK

    def body(q_ref, k_ref, v_ref, b_ref, g_ref, o_ref, s_ref, t_ref, state):
        qv, kv, vv, beta = q_ref[...], k_ref[...], v_ref[...], b_ref[...]
        ch = _gdn_chunk(qv, kv, vv, beta, g_ref[...])
        yield
        tm = yield from _inv_unit_lower(ch["a"])
        sol = _dot(tm, jnp.concatenate([beta * vv, beta * ch["gam"] * kv], axis=1), NN)
        yield
        u_v, w = sol[:, :HEAD_DIM], sol[:, HEAD_DIM:]
        s0 = state[...]
        u = u_v - _dot(w, s0, NN)
        yield
        o_ref[...] = _dot(qv * ch["gam"], s0, NN) + _dot(ch["p"], u, NN)
        s_ref[...] = s0
        t_ref[...] = tm
        state[...] = ch["gam_last"] * s0 + _dot(ch["k_end"], u, TN)

    tok = pl.BlockSpec((CHUNK, _hps(h) * HEAD_DIM), lambda hh, c: (c, hh))
    bc = pl.BlockSpec((_hps(h), CHUNK, HEAD_DIM), lambda hh, c: (hh, c, 0))
    mat = pl.BlockSpec((_hps(h), None, HEAD_DIM, HEAD_DIM), lambda hh, c: (hh, c, 0, 0))
    return pl.pallas_call(
        _each_head(body, 5), name="gdn_fwd",
        out_shape=(jax.ShapeDtypeStruct(q.shape, F32),
                   jax.ShapeDtypeStruct((h, nc, HEAD_DIM, HEAD_DIM), F32),
                   jax.ShapeDtypeStruct((h, nc, CHUNK, CHUNK), F32)),
        grid=(h // _hps(h), nc), in_specs=[tok, tok, tok, bc, bc], out_specs=(tok, mat, mat),
        scratch_shapes=[pltpu.VMEM((_hps(h), HEAD_DIM, HEAD_DIM), F32)],
        compiler_params=_params(("parallel", "arbitrary")),
    )(q, k, v, beta_bc, g_bc)


def _gdn_bwd(q, k, v, beta_bc, g_bc, states, invs, do, do_blk=0):
    t = q.shape[0]
    h = q.shape[1] // HEAD_DIM
    nc = t // CHUNK

    def body(q_ref, k_ref, v_ref, b_ref, g_ref, s_ref, t_ref, do_ref,
             dq_ref, dk_ref, dv_ref, db_ref, dg_ref, dstate):
        qv, kv, vv, beta = q_ref[...], k_ref[...], v_ref[...], b_ref[...]
        dov, s0, tm, ds1 = do_ref[...], s_ref[...], t_ref[...], dstate[...]
        ch = _gdn_chunk(qv, kv, vv, beta, g_ref[...])
        yield
        gam, dec, dec_s, kk = ch["gam"], ch["dec"], ch["dec_s"], ch["kk"]
        r_v, r_w = beta * vv, beta * gam * kv
        sol = _dot(tm, jnp.concatenate([r_v, r_w], axis=1), NN)
        yield
        u_v, w = sol[:, :HEAD_DIM], sol[:, HEAD_DIM:]
        u = u_v - _dot(w, s0, NN)
        qg = qv * gam
        yield

        du = _dot(ch["p"], dov, TN) + _dot(ch["k_end"], ds1, NN)
        dp = _dot(dov, u, NT)
        dpd = dp * dec
        dqg = _dot(dov, s0, NT)
        dk_end = _dot(u, ds1, NT)
        yield
        dq = dqg * gam + _dot(dpd, kv, NN)
        dk = _dot(dpd, qv, TN) + dk_end * ch["e_end"]
        dstate[...] = _dot(qg, dov, TN) + ch["gam_last"] * ds1 - _dot(w, du, TN)
        dw = -_dot(du, s0, NT)
        yield
        dr = _dot(tm, jnp.concatenate([du, dw], axis=1), TN)
        yield
        dr_v, dr_w = dr[:, :HEAD_DIM], dr[:, HEAD_DIM:]
        da = -_dot(dr, sol, NT)
        yield
        dkk = da * beta * dec_s
        dk = dk + _dot(dkk, kv, NN) + _dot(dkk, kv, TN) + beta * gam * dr_w
        dbeta = (jnp.sum(da * kk * dec_s, axis=1, keepdims=True)
                 + jnp.sum(dr_v * vv + dr_w * gam * kv, axis=1, keepdims=True))

        pair = dp * ch["p"] + da * ch["a"]
        end = jnp.sum(dk_end * ch["k_end"], axis=1, keepdims=True)
        dgc = (jnp.sum(pair - pair.T, axis=1, keepdims=True)
               + jnp.sum(dqg * qg + dr_w * r_w, axis=1, keepdims=True) - end)
        at_end = jnp.sum(end) + ch["gam_last"] * jnp.sum(s0 * ds1)
        dgc = jnp.broadcast_to(dgc, (CHUNK, HEAD_DIM))
        dgc = dgc + jnp.where(_iota((CHUNK, HEAD_DIM), 0) == CHUNK - 1, at_end, 0.0)
        dq_ref[...] = dq
        dk_ref[...] = dk
        dv_ref[...] = beta * dr_v
        db_ref[...] = jnp.broadcast_to(dbeta, (CHUNK, HEAD_DIM)).T[:ROW_TILE]
        dg_ref[...] = _rev_cumsum_rows(dgc).T[:ROW_TILE]

    rev = lambda c: nc - 1 - c
    tok = pl.BlockSpec((CHUNK, _hps(h) * HEAD_DIM), lambda hh, c: (rev(c), hh))
    bc = pl.BlockSpec((_hps(h), CHUNK, HEAD_DIM), lambda hh, c: (hh, rev(c), 0))
    mat = pl.BlockSpec((_hps(h), None, HEAD_DIM, HEAD_DIM), lambda hh, c: (hh, rev(c), 0, 0))
    tok_shape = jax.ShapeDtypeStruct(q.shape, F32)
    row_shape = jax.ShapeDtypeStruct((h, nc, ROW_TILE, CHUNK), F32)
    rows = pl.BlockSpec((_hps(h), None, ROW_TILE, CHUNK), lambda hh, c: (hh, rev(c), 0, 0))
    return pl.pallas_call(
        _each_head(body, 8), name="gdn_bwd",
        out_shape=(tok_shape, tok_shape, tok_shape, row_shape, row_shape),
        grid=(h // _hps(h), nc),
        in_specs=[tok, tok, tok, bc, bc, mat, mat,
                  pl.BlockSpec((CHUNK, _hps(h) * HEAD_DIM), lambda hh, c: (rev(c), do_blk // _hps(h) + hh))],
        out_specs=(tok, tok, tok, rows, rows),
        scratch_shapes=[pltpu.VMEM((_hps(h), HEAD_DIM, HEAD_DIM), F32)],
        compiler_params=_params(("parallel", "arbitrary")),
    )(q, k, v, beta_bc, g_bc, states, invs, do)


def _hgrn_chunk(q, k, lf):
    c = q.shape[0]
    row = _iota((c, HEAD_DIM), 0)
    b = _cumsum_rows(lf)
    q_subs, k_facs, a_rows = [], [], []
    for x in range(c // SUB):
        b_start = jnp.sum(jnp.where(row < x * SUB, lf, 0.0), axis=0, keepdims=True)
        q_x = (q * jnp.exp(jnp.minimum(b - b_start, 0.0)))[x * SUB:(x + 1) * SUB]
        k_fac = jnp.where(row < (x + 1) * SUB, jnp.exp(jnp.minimum(b_start - b, EXP_CAP)), 0.0)
        q_subs.append(q_x)
        k_facs.append(k_fac)
        a_rows.append(_dot(q_x, k * k_fac, NT))
    a = jnp.concatenate(a_rows, axis=0)
    a = jnp.where(_iota((c, c), 0) >= _iota((c, c), 1), a, 0.0)
    b_last = jnp.sum(lf, axis=0, keepdims=True)
    return dict(b=b, a=a, q_subs=q_subs, k_facs=k_facs, e_b=jnp.exp(b),
                e_end=jnp.exp(b_last - b), e_last=jnp.exp(b_last))


def _hgrn_fwd(q, k, v, lf, v_blk=0):
    t = q.shape[0]
    h = q.shape[1] // HEAD_DIM
    nc = t // CHUNK

    def body(q_ref, k_ref, v_ref, lf_ref, o_ref, s_ref, state):
        qv, kv, vv = q_ref[...], k_ref[...], v_ref[...]
        ch = _hgrn_chunk(qv, kv, lf_ref[...])
        yield
        s0 = state[...]
        o_ref[...] = _dot(qv * ch["e_b"], s0, NT) + _dot(ch["a"], vv, NN)
        s_ref[...] = s0
        state[...] = s0 * ch["e_last"] + _dot(vv, kv * ch["e_end"], TN)

    tok = pl.BlockSpec((CHUNK, _hps(h) * HEAD_DIM), lambda hh, c: (c, hh))
    mat = pl.BlockSpec((_hps(h), None, HEAD_DIM, HEAD_DIM), lambda hh, c: (hh, c, 0, 0))
    return pl.pallas_call(
        _each_head(body, 4), name="hgrn_fwd",
        out_shape=(jax.ShapeDtypeStruct(q.shape, F32),
                   jax.ShapeDtypeStruct((h, nc, HEAD_DIM, HEAD_DIM), F32)),
        grid=(h // _hps(h), nc),
        in_specs=[tok, tok, pl.BlockSpec((CHUNK, _hps(h) * HEAD_DIM), lambda hh, c: (c, v_blk // _hps(h) + hh)), tok],
        out_specs=(tok, mat),
        scratch_shapes=[pltpu.VMEM((_hps(h), HEAD_DIM, HEAD_DIM), F32)],
        compiler_params=_params(("parallel", "arbitrary")),
    )(q, k, v, lf)


def _hgrn_bwd(q, k, v, lf, states, do, v_blk=0, do_blk=0):
    t = q.shape[0]
    nc = t // CHUNK
    h = q.shape[1] // HEAD_DIM

    def body(q_ref, k_ref, v_ref, lf_ref, s_ref, do_ref, dq_ref, dk_ref, dv_ref, dlf_ref, dstate):
        qv, kv, vv, dov, s0 = q_ref[...], k_ref[...], v_ref[...], do_ref[...], s_ref[...]
        ds1 = dstate[...]
        ch = _hgrn_chunk(qv, kv, lf_ref[...])
        yield
        c = CHUNK
        row = _iota((c, HEAD_DIM), 0)
        qh = qv * ch["e_b"]
        k_end = kv * ch["e_end"]
        da = jnp.where(_iota((c, c), 0) >= _iota((c, c), 1), _dot(dov, vv, NT), 0.0)
        dqh = _dot(dov, s0, NN)
        dk_end = _dot(vv, ds1, NN)
        yield
        end = dk_end * k_end
        dk = dk_end * ch["e_end"]
        db = dqh * qh - end + jnp.where(
            row == c - 1, jnp.sum(end + s0 * ch["e_last"] * ds1, axis=0, keepdims=True), 0.0)
        dq_rows, qdq_rows = [], []
        for x in range(c // SUB):
            da_x = da[x * SUB:(x + 1) * SUB]
            k_x = kv * ch["k_facs"][x]
            dq_x = _dot(da_x, k_x, NN)
            dk_x = _dot(da_x, ch["q_subs"][x], TN)
            dq_rows.append(dq_x)
            qdq_rows.append(dq_x * ch["q_subs"][x])
            dk = dk + dk_x * ch["k_facs"][x]
            kdk = dk_x * k_x
            db = db - kdk
            if x > 0:
                at_start = jnp.sum(kdk, axis=0, keepdims=True) - jnp.sum(qdq_rows[x], axis=0, keepdims=True)
                db = db + jnp.where(row == x * SUB - 1, at_start, 0.0)
        yield
        b_start = jnp.zeros((c, HEAD_DIM), F32)
        for x in range(1, c // SUB):
            b_x = jnp.sum(jnp.where(row < x * SUB, lf_ref[...], 0.0), axis=0, keepdims=True)
            b_start = jnp.where(row >= x * SUB, b_x, b_start)
        dq = dqh * ch["e_b"] + jnp.concatenate(dq_rows, axis=0) * jnp.exp(jnp.minimum(ch["b"] - b_start, 0.0))
        db = db + jnp.concatenate(qdq_rows, axis=0)
        dstate[...] = _dot(dov, qh, TN) + ds1 * ch["e_last"]
        dq_ref[...] = dq
        dk_ref[...] = dk
        dv_ref[...] = _dot(ch["a"], dov, TN) + _dot(k_end, ds1, NT)
        dlf_ref[...] = _rev_cumsum_rows(db)

    rev = lambda c: nc - 1 - c
    tok = pl.BlockSpec((CHUNK, _hps(h) * HEAD_DIM), lambda hh, c: (rev(c), hh))
    mat = pl.BlockSpec((_hps(h), None, HEAD_DIM, HEAD_DIM), lambda hh, c: (hh, rev(c), 0, 0))
    tok_shape = jax.ShapeDtypeStruct(q.shape, F32)
    return pl.pallas_call(
        _each_head(body, 6), name="hgrn_bwd",
        out_shape=(tok_shape, tok_shape, tok_shape, tok_shape),
        grid=(h // _hps(h), nc),
        in_specs=[tok, tok, pl.BlockSpec((CHUNK, _hps(h) * HEAD_DIM), lambda hh, c: (rev(c), v_blk // _hps(h) + hh)), tok, mat,
                  pl.BlockSpec((CHUNK, _hps(h) * HEAD_DIM), lambda hh, c: (rev(c), do_blk // _hps(h) + hh))],
        out_specs=(tok, tok, tok, tok),
        scratch_shapes=[pltpu.VMEM((_hps(h), HEAD_DIM, HEAD_DIM), F32)],
        compiler_params=_params(("parallel", "arbitrary")),
    )(q, k, v, lf, states, do)


CONV_ROWS = 256
HALO = 8


def _shift_down(cur, prev, s):
    rt = cur.shape[0]
    head = jnp.concatenate([pltpu.roll(prev, s, 0), jnp.zeros((rt - HALO, cur.shape[1]), F32)], axis=0)
    return jnp.where(_iota(cur.shape, 0) < s, head, pltpu.roll(cur, s, 0))


def _shift_up(cur, nxt, s):
    rt = cur.shape[0]
    tail = jnp.concatenate([jnp.zeros((rt - HALO, cur.shape[1]), F32), pltpu.roll(nxt, HALO - s, 0)], axis=0)
    return jnp.where(_iota(cur.shape, 0) >= rt - s, tail, pltpu.roll(cur, rt - s, 0))


def _tile_with_prev(ref, i, rt):
    r0 = pl.multiple_of(i * rt, rt)
    cur = ref[pl.ds(r0, rt), :]
    prev = ref[pl.ds(pl.multiple_of(jnp.maximum(r0 - HALO, 0), HALO), HALO), :]
    return cur, jnp.where(i > 0, prev, 0.0)


def _tile_with_next(ref, i, rt, n_tiles):
    r0 = pl.multiple_of(i * rt, rt)
    cur = ref[pl.ds(r0, rt), :]
    nxt = ref[pl.ds(pl.multiple_of(jnp.minimum(r0 + rt, (n_tiles - 1) * rt), HALO), HALO), :]
    return cur, jnp.where(i < n_tiles - 1, nxt, 0.0)


def _conv_tile(x_ref, w_ref, i, rt):
    cur, prev = _tile_with_prev(x_ref, i, rt)
    shifted = [_shift_down(cur, prev, CONV_W - 1 - j) for j in range(CONV_W - 1)] + [cur]
    c = shifted[0] * w_ref[pl.ds(0, 1), :]
    for j in range(1, CONV_W):
        c = c + shifted[j] * w_ref[pl.ds(j, 1), :]
    return c, shifted


def _l2n(s):
    return s * lax.rsqrt(jnp.sum(s * s, axis=-1, keepdims=True) + L2_EPS)


def _gdn_prep_fwd(proj, conv_w, h):
    t = proj.shape[0]
    rt = _tile(t, CONV_ROWS)
    nt = t // rt
    scale = HEAD_DIM ** -0.5

    def body(xq, xk, xv, wq, wk, wv, q_ref, k_ref, v_ref):
        def tile(i, carry):
            rows = pl.ds(pl.multiple_of(i * rt, rt), rt)
            q_ref[rows, :] = _l2n(_silu(_conv_tile(xq, wq, i, rt)[0])) * scale
            k_ref[rows, :] = _l2n(_silu(_conv_tile(xk, wk, i, rt)[0]))
            v_ref[rows, :] = _silu(_conv_tile(xv, wv, i, rt)[0])
            return carry

        lax.fori_loop(0, nt, tile, 0)

    col = lambda p: pl.BlockSpec((t, HEAD_DIM), lambda hh: (0, p * h + hh))
    wcol = lambda p: pl.BlockSpec((CONV_W, HEAD_DIM), lambda hh: (0, p * h + hh))
    out = pl.BlockSpec((t, HEAD_DIM), lambda hh: (0, hh))
    shape = jax.ShapeDtypeStruct((t, h * HEAD_DIM), F32)
    return pl.pallas_call(
        body, name="gdn_prep_fwd", out_shape=(shape, shape, shape), grid=(h,),
        in_specs=[col(0), col(1), col(2), wcol(0), wcol(1), wcol(2)], out_specs=(out, out, out),
        compiler_params=_params(("parallel",)),
    )(proj, proj, proj, conv_w, conv_w, conv_w)


def _gdn_prep_bwd(proj, conv_w, dq, dk, dv, h):
    t = proj.shape[0]
    rt = _tile(t, CONV_ROWS)
    nt = t // rt
    scale = HEAD_DIM ** -0.5

    def part(x_ref, w_ref, dy_ref, dx_ref, dw_ref, dc_ref, norm_scale):
        def first(i, dws):
            rows = pl.ds(pl.multiple_of(i * rt, rt), rt)
            c, shifted = _conv_tile(x_ref, w_ref, i, rt)
            ds = dy_ref[rows, :]
            s, ds_dc = _silu_and_grad(c)
            if norm_scale is not None:
                r = lax.rsqrt(jnp.sum(s * s, axis=-1, keepdims=True) + L2_EPS)
                y = s * r
                dyn = ds * norm_scale
                ds = r * (dyn - y * jnp.sum(dyn * y, axis=-1, keepdims=True))
            dc = ds * ds_dc
            dc_ref[rows, :] = dc
            return tuple(dws[j] + jnp.sum(dc * shifted[j], axis=0, keepdims=True) for j in range(CONV_W))

        dws = lax.fori_loop(0, nt, first, tuple(jnp.zeros((1, HEAD_DIM), F32) for _ in range(CONV_W)))
        for j in range(CONV_W):
            dw_ref[pl.ds(j, 1), :] = dws[j]

        def second(i, carry):
            rows = pl.ds(pl.multiple_of(i * rt, rt), rt)
            cur, nxt = _tile_with_next(dc_ref, i, rt, nt)
            dx = cur * w_ref[pl.ds(CONV_W - 1, 1), :]
            for j in range(CONV_W - 1):
                dx = dx + _shift_up(cur, nxt, CONV_W - 1 - j) * w_ref[pl.ds(j, 1), :]
            dx_ref[rows, :] = dx.astype(dx_ref.dtype)
            return carry

        lax.fori_loop(0, nt, second, 0)

    def body(xq, xk, xv, wq, wk, wv, dq_ref, dk_ref, dv_ref, dxq, dxk, dxv, dwq, dwk, dwv, dc_ref):
        part(xq, wq, dq_ref, dxq, dwq, dc_ref, scale)
        part(xk, wk, dk_ref, dxk, dwk, dc_ref, 1.0)
        part(xv, wv, dv_ref, dxv, dwv, dc_ref, None)

    col = lambda p: pl.BlockSpec((t, HEAD_DIM), lambda hh: (0, p * h + hh))
    wcol = lambda p: pl.BlockSpec((CONV_W, HEAD_DIM), lambda hh: (0, p * h + hh))
    own = pl.BlockSpec((t, HEAD_DIM), lambda hh: (0, hh))
    wown = pl.BlockSpec((CONV_W, HEAD_DIM), lambda hh: (0, hh))
    dx_shape = jax.ShapeDtypeStruct((t, h * HEAD_DIM), BF16)
    dw_shape = jax.ShapeDtypeStruct((CONV_W, h * HEAD_DIM), F32)
    return pl.pallas_call(
        body, name="gdn_prep_bwd",
        out_shape=(dx_shape, dx_shape, dx_shape, dw_shape, dw_shape, dw_shape), grid=(h,),
        in_specs=[col(0), col(1), col(2), wcol(0), wcol(1), wcol(2), own, own, own],
        out_specs=(own, own, own, wown, wown, wown),
        scratch_shapes=[pltpu.VMEM((t, HEAD_DIM), F32)],
        compiler_params=_params(("parallel",)),
    )(proj, proj, proj, conv_w, conv_w, conv_w, dq, dk, dv)


def _gdn_gates_fwd(ab, a_log_row, dt_bias_row):
    t = ab.shape[0]
    tr = _tile(t, 512)

    def body(ab_ref, al_ref, dt_ref, g_ref, b_ref):
        g_ref[...] = -jnp.exp(al_ref[...]) * _softplus(ab_ref[:, :HEAD_DIM] + dt_ref[...])
        b_ref[...] = _sigmoid(ab_ref[:, HEAD_DIM:])

    row = pl.BlockSpec((tr, HEAD_DIM), lambda i: (i, 0))
    vec = pl.BlockSpec((1, HEAD_DIM), lambda i: (0, 0))
    shape = jax.ShapeDtypeStruct((t, HEAD_DIM), F32)
    return pl.pallas_call(
        body, name="gdn_gates_fwd", out_shape=(shape, shape), grid=(t // tr,),
        in_specs=[pl.BlockSpec((tr, 2 * HEAD_DIM), lambda i: (i, 0)), vec, vec], out_specs=(row, row),
        compiler_params=_params(("parallel",)),
    )(ab, a_log_row, dt_bias_row)


def _gdn_gates_bwd(ab, a_log_row, dt_bias_row, dg, dbeta):
    t = ab.shape[0]
    tr = _tile(t, 512)

    def body(ab_ref, al_ref, dt_ref, dg_ref, db_ref, dab_ref, dal_ref, ddt_ref):
        @pl.when(pl.program_id(0) == 0)
        def _():
            dal_ref[...] = jnp.zeros_like(dal_ref)
            ddt_ref[...] = jnp.zeros_like(ddt_ref)

        xa = ab_ref[:, :HEAD_DIM] + dt_ref[...]
        neg_a = -jnp.exp(al_ref[...])
        dgv = dg_ref[...]
        da = dgv * neg_a * _sigmoid(xa)
        beta = _sigmoid(ab_ref[:, HEAD_DIM:])
        dab_ref[:, :HEAD_DIM] = da.astype(BF16)
        dab_ref[:, HEAD_DIM:] = (db_ref[...] * beta * (1.0 - beta)).astype(BF16)
        dal_ref[...] += jnp.sum(dgv * neg_a * _softplus(xa), axis=0, keepdims=True)
        ddt_ref[...] += jnp.sum(da, axis=0, keepdims=True)

    row = pl.BlockSpec((tr, HEAD_DIM), lambda i: (i, 0))
    row2 = pl.BlockSpec((tr, 2 * HEAD_DIM), lambda i: (i, 0))
    vec = pl.BlockSpec((1, HEAD_DIM), lambda i: (0, 0))
    vshape = jax.ShapeDtypeStruct((1, HEAD_DIM), F32)
    return pl.pallas_call(
        body, name="gdn_gates_bwd",
        out_shape=(jax.ShapeDtypeStruct((t, 2 * HEAD_DIM), BF16), vshape, vshape), grid=(t // tr,),
        in_specs=[row2, vec, vec, row, row], out_specs=(row2, vec, vec),
        compiler_params=_params(("arbitrary",)),
    )(ab, a_log_row, dt_bias_row, dg, dbeta)


def _lower_bound(lb_ref):
    return _sigmoid(lb_ref[pl.ds(0, 1), :] - lb_ref[pl.ds(1, 1), :])


def _hgrn_prep_fwd(proj, lb_logits, h, q_blk, f_blk):
    t = proj.shape[0]
    tr = _tile(t, 512)

    def body(xq, xf, lb_ref, q_ref, k_ref, lf_ref):
        lb = _lower_bound(lb_ref)
        s = _sigmoid(xf[...])
        q_ref[...] = _silu(xq[...])
        k_ref[...] = (1.0 - lb) * (1.0 - s)
        lf_ref[...] = jnp.log(lb + (1.0 - lb) * s)

    width = h * HEAD_DIM
    col = lambda b0: pl.BlockSpec((tr, width), lambda i: (i, b0 // h))
    own = pl.BlockSpec((tr, width), lambda i: (i, 0))
    shape = jax.ShapeDtypeStruct((t, width), F32)
    return pl.pallas_call(
        body, name="hgrn_prep_fwd", out_shape=(shape, shape, shape), grid=(t // tr,),
        in_specs=[col(q_blk), col(f_blk), pl.BlockSpec((2, width), lambda i: (0, 0))],
        out_specs=(own, own, own), compiler_params=_params(("parallel",)),
    )(proj, proj, lb_logits)


def _hgrn_prep_bwd(proj, lb_logits, dq, dk, dlf, h, q_blk, f_blk):
    t = proj.shape[0]
    tr = _tile(t, 512)

    def body(xq, xf, lb_ref, dq_ref, dk_ref, dlf_ref, dxq, dxf, dlb_ref):
        @pl.when(pl.program_id(0) == 0)
        def _():
            dlb_ref[...] = jnp.zeros_like(dlb_ref)

        lb = _lower_bound(lb_ref)
        s = _sigmoid(xf[...])
        e = dlf_ref[...] / (lb + (1.0 - lb) * s) - dk_ref[...]
        dxq[...] = (dq_ref[...] * _dsilu(xq[...])).astype(BF16)
        dxf[...] = (s * (1.0 - s) * (1.0 - lb) * e).astype(BF16)
        d0 = jnp.sum((1.0 - s) * e, axis=0, keepdims=True) * (lb * (1.0 - lb))
        dlb_ref[pl.ds(0, 1), :] += d0
        dlb_ref[pl.ds(1, 1), :] += -d0

    width = h * HEAD_DIM
    col = lambda b0: pl.BlockSpec((tr, width), lambda i: (i, b0 // h))
    own = pl.BlockSpec((tr, width), lambda i: (i, 0))
    lbs = pl.BlockSpec((2, width), lambda i: (0, 0))
    shape = jax.ShapeDtypeStruct((t, width), BF16)
    return pl.pallas_call(
        body, name="hgrn_prep_bwd",
        out_shape=(shape, shape, jax.ShapeDtypeStruct((2, width), F32)), grid=(t // tr,),
        in_specs=[col(q_blk), col(f_blk), lbs, own, own, own], out_specs=(own, own, lbs),
        compiler_params=_params(("arbitrary",)),
    )(proj, proj, lb_logits, dq, dk, dlf)


GATE_HEADS = 8


def _gate_specs(h, z_blk, g_blk, tr):
    g = min(GATE_HEADS, h)
    n = h // g
    width = g * HEAD_DIM
    o_a = pl.BlockSpec((tr, width), lambda gg, i: (i, jnp.minimum(gg, n - 1)))
    o_b = pl.BlockSpec((tr, width), lambda gg, i: (i, jnp.maximum(gg - n, 0)))
    gate = pl.BlockSpec((tr, width), lambda gg, i: (i, jnp.where(gg < n, z_blk // g + gg, g_blk // g + gg - n)))
    w = pl.BlockSpec((None, 1, HEAD_DIM), lambda gg, i: (gg // n, 0, 0))
    cat = pl.BlockSpec((tr, width), lambda gg, i: (i, gg))
    return (o_a, o_b, gate, w, cat), g, n


def _silu_and_grad(x):
    s = _sigmoid(x)
    return x * s, s * (1.0 + x * (1.0 - s))


def _gate_fwd(o_a, o_b, proj, norm_w, h, z_blk, g_blk):
    t = o_a.shape[0]
    tr = _tile(t, 512)

    (sa, sb, sg, sw, cat), g, n = _gate_specs(h, z_blk, g_blk, tr)

    def body(oa_ref, ob_ref, z_ref, w_ref, y_ref):
        for k in range(g):
            lanes = pl.ds(k * HEAD_DIM, HEAD_DIM)
            o = jnp.where(pl.program_id(0) < n, oa_ref[:, lanes], ob_ref[:, lanes])
            r = lax.rsqrt(jnp.mean(o * o, axis=-1, keepdims=True) + NORM_EPS)
            y_ref[:, lanes] = (o * r * w_ref[...] * _silu(z_ref[:, lanes])).astype(y_ref.dtype)

    return pl.pallas_call(
        body, name="gate_fwd", out_shape=jax.ShapeDtypeStruct((t, 2 * h * HEAD_DIM), BF16),
        grid=(2 * n, t // tr), in_specs=[sa, sb, sg, sw], out_specs=cat,
        compiler_params=_params(("parallel", "parallel")),
    )(o_a, o_b, proj, norm_w)


def _gate_bwd(o_a, o_b, proj, norm_w, dy, h, z_blk, g_blk):
    t = o_a.shape[0]
    tr = _tile(t, 512)

    (sa, sb, sg, sw, cat), g, n = _gate_specs(h, z_blk, g_blk, tr)

    def body(oa_ref, ob_ref, z_ref, w_ref, dy_ref, do_ref, dz_ref, dw_ref):
        gg = pl.program_id(0)

        @pl.when(jnp.logical_and(gg % n == 0, pl.program_id(1) == 0))
        def _():
            dw_ref[...] = jnp.zeros_like(dw_ref)

        w = w_ref[...]
        dw = jnp.zeros_like(w)
        for k in range(g):
            lanes = pl.ds(k * HEAD_DIM, HEAD_DIM)
            o = jnp.where(gg < n, oa_ref[:, lanes], ob_ref[:, lanes])
            dyv = dy_ref[:, lanes]
            r = lax.rsqrt(jnp.mean(o * o, axis=-1, keepdims=True) + NORM_EPS)
            oh = o * r
            act, dact = _silu_and_grad(z_ref[:, lanes])
            dz_ref[:, lanes] = (dyv * oh * w * dact).astype(dz_ref.dtype)
            dn = dyv * act
            doh = dn * w
            do_ref[:, lanes] = r * (doh - oh * jnp.mean(doh * oh, axis=-1, keepdims=True))
            dw = dw + jnp.sum(dn * oh, axis=0, keepdims=True)
        dw_ref[...] += dw

    width = 2 * h * HEAD_DIM
    return pl.pallas_call(
        body, name="gate_bwd",
        out_shape=(jax.ShapeDtypeStruct((t, width), F32), jax.ShapeDtypeStruct((t, width), BF16),
                   jax.ShapeDtypeStruct((2, 1, HEAD_DIM), F32)),
        grid=(2 * n, t // tr), in_specs=[sa, sb, sg, sw, cat], out_specs=(cat, cat, sw),
        compiler_params=_params(("arbitrary", "arbitrary")),
    )(o_a, o_b, proj, norm_w, dy)


def _lane_row(vec):
    return jnp.pad(vec.reshape(1, -1), ((0, 0), (0, HEAD_DIM - vec.shape[-1])))


def _add_epi(acc, res):
    return (acc + res,)


def _split_w_in(w_in, h):
    gw = h * HEAD_DIM
    main = jnp.concatenate([w_in[:, :4 * gw], w_in[:, 4 * gw + 2 * h:]], axis=1)
    pad = jnp.zeros((w_in.shape[0], HEAD_DIM - h), w_in.dtype)
    ab = jnp.concatenate([w_in[:, 4 * gw:4 * gw + h], pad, w_in[:, 4 * gw + h:4 * gw + 2 * h], pad], axis=1)
    return main, ab


def _merge_w_in(main, ab, h):
    gw = h * HEAD_DIM
    return jnp.concatenate([main[:, :4 * gw], ab[:, :h], ab[:, HEAD_DIM:HEAD_DIM + h], main[:, 4 * gw:]], axis=1)


def _local_step(x, target, w_main, w_ab, conv_w, a_log, dt_bias, gdn_norm_w, lb_logits, hgrn_norm_w,
                w_out, norm_mix_w, norm_ffn_w, w_ff1, w_ff2, norm_final_w, reducer=None, n1=None):
    t, d = x.shape
    h = d // (2 * HEAD_DIM)
    gw = h * HEAD_DIM
    k_blk, v_blk, z_blk, qb_blk, fb_blk, ib_blk, gb_blk = (i * h for i in range(1, 8))
    del k_blk, v_blk

    if n1 is None:
        n1 = _rms_fwd(x, norm_mix_w, "rms_mix")
    stacked = not hasattr(w_main, "ndim") or w_main.ndim == 3
    if stacked and not hasattr(w_main, "ndim"):
        early, late_parts, edges, chip = w_main
        proj = _mm_quarters(n1, early, jnp.stack([chip, chip ^ 2, chip ^ 1]), "in_proj_early")
        late, _ = _in_proj_quarters(lax.optimization_barrier((late_parts, proj))[0], edges, h)
        proj = _mm_quarters(n1, late, jnp.stack([chip ^ 3]), "in_proj_late", into=proj)
        w_main = lax.dynamic_update_index_in_dim(
            early, lax.dynamic_index_in_dim(late, chip ^ 3, 0, keepdims=False), chip ^ 3, 0)
    else:
        proj = _mm(n1, w_main, "nn", (F32,), "in_proj", b_stacked=stacked)
    ab = _mm(n1, w_ab, "nn", (F32,), "in_proj_ab")

    q, k, v = _gdn_prep_fwd(proj, conv_w, h)
    a_log_row, dt_row = _lane_row(a_log), _lane_row(dt_bias)
    g_tm, beta_tm = _gdn_gates_fwd(ab, a_log_row, dt_row)
    to_heads = lambda a: jnp.broadcast_to(a[:, :h].T[:, :, None], (h, t, HEAD_DIM))
    g_bc, beta_bc = to_heads(g_tm), to_heads(beta_tm)
    o_a, st_a, inv_a = _gdn_fwd(q, k, v, beta_bc, g_bc)

    qh, kh, lf = _hgrn_prep_fwd(proj, lb_logits, h, qb_blk, fb_blk)
    o_b, st_b = _hgrn_fwd(qh, kh, proj, lf, v_blk=ib_blk)

    gate_w = jnp.stack([gdn_norm_w.reshape(1, HEAD_DIM), hgrn_norm_w.reshape(1, HEAD_DIM)])
    y = _gate_fwd(o_a, o_b, proj, gate_w, h, z_blk, gb_blk)
    if callable(w_out):
        w_out, w_ff1, w_ff2 = w_out(y)
    h1 = _mm(y, w_out, "nn", (F32,), "out_proj", epi=_add_epi, extras=(x,))
    n2 = _rms_fwd(h1, norm_ffn_w, "rms_ffn")
    act, r = _mm(n2, w_ff1, "nn", (F32, BF16), "ff1", b_stacked=True,
                 epi=lambda acc: (acc, jnp.square(jnp.maximum(acc, 0.0))))
    h2 = _mm(r, w_ff2, "nn", (F32,), "ff2", epi=_add_epi, extras=(h1,))
    loss, dh2, dh2_b, d_norm_final = _loss_head(h2, norm_final_w, target)

    da = _mm(dh2_b, w_ff2, "nt", (BF16,), "ff2_dx",
             epi=lambda acc, a: (acc * (2.0 * jnp.maximum(a, 0.0)),), extras=(act,))
    pending = []

    def step(anchor, name=None, full=None):
        if reducer is not None:
            pending.extend(reducer.step(name, full, anchor))

    def after_step(value):
        if not pending:
            return value
        value = lax.optimization_barrier((value, *pending))[0]
        pending.clear()
        return value

    d_ff2 = _mm(r, dh2_b, "tn", (F32,), "ff2_dw")
    step(None, "w_ff2", d_ff2)
    dn2 = _mm(da, w_ff1, "nt", (F32,), "ff1_dx", b_stacked=True)
    d_ff1 = _mm(n2, da, "tn", (F32,), "ff1_dw", out_stacked=True)
    step(d_ff1, "w_ff1", d_ff1)
    dh1, dh1_b, d_norm_ffn = _rms_bwd(after_step(dn2), h1, norm_ffn_w, dh2, "rms_ffn_bwd")
    dy = _mm(dh1_b, w_out, "nt", (F32,), "out_proj_dx")
    d_out = _mm(y, dh1_b, "tn", (F32,), "out_proj_dw")
    step(d_out, "w_out", d_out)

    do, dgate, d_gate_w = _gate_bwd(o_a, o_b, proj, gate_w, after_step(dy), h, z_blk, gb_blk)
    step(do)
    dq, dk, dv, dbeta_bc, dg_bc = _gdn_bwd(q, k, v, beta_bc, g_bc, st_a, inv_a, after_step(do), do_blk=0)
    step(dq)
    dxq, dxk, dxv, dcq, dck, dcv = _gdn_prep_bwd(proj, conv_w, after_step(dq), dk, dv, h)
    step(dxq)
    from_heads = lambda a: jnp.pad(a[:, :, 0, :].reshape(h, t).T, ((0, 0), (0, HEAD_DIM - h)))
    dab, d_a_log, d_dt_bias = _gdn_gates_bwd(ab, a_log_row, dt_row, from_heads(dg_bc), from_heads(dbeta_bc))
    dqh, dkh, dvh, dlf = _hgrn_bwd(after_step(qh), kh, proj, lf, st_b, do, v_blk=ib_blk, do_blk=h)
    step(dqh)
    dxqb, dxfb, d_lb = _hgrn_prep_bwd(proj, lb_logits, dqh, dkh, dlf, h, qb_blk, fb_blk)

    dproj = jnp.concatenate([after_step(dxq), dxk, dxv, dgate[:, :gw], dxqb, dxfb, dvh.astype(BF16), dgate[:, gw:]],
                            axis=1)
    d_main = _mm(n1, dproj, "tn", (F32,), "in_proj_dw", out_stacked=True)
    d_ab = _mm(n1, dab, "tn", (F32,), "in_proj_ab_dw")
    step(d_main, "w_in", d_main)
    dn1_ab = _mm(after_step(dab), w_ab, "nt", (F32,), "in_proj_ab_dx")
    step(dn1_ab)
    dn1 = _mm(after_step(dproj), w_main, "nt", (F32,), "in_proj_dx", epi=_add_epi, extras=(dn1_ab,),
              b_stacked=stacked)
    step(dn1)
    dx, _, d_norm_mix = _rms_bwd(after_step(dn1), x, norm_mix_w, dh1, "rms_mix_bwd")
    step(dx)

    grads = dict(
        w_main=d_main, w_ab=d_ab, conv_w=jnp.concatenate([dcq, dck, dcv], axis=1),
        gdn_a_log=d_a_log[:, :h], gdn_dt_bias=d_dt_bias[:, :h], gdn_norm_w=d_gate_w[0],
        hgrn_lb_logits=d_lb, hgrn_norm_w=d_gate_w[1], w_out=d_out, norm_mix_w=d_norm_mix,
        norm_ffn_w=d_norm_ffn, w_ff1=d_ff1, w_ff2=d_ff2, norm_final_w=d_norm_final)
    return loss, dx, grads


N_CHIPS = 4
ANY = pl.BlockSpec(memory_space=pl.ANY)


def _place():
    x, y, c = lax.axis_index("x"), lax.axis_index("y"), lax.axis_index("c")
    chips = [(1 - x, y), (x, 1 - y), (1 - x, 1 - y)]
    return x, y, c, chips


def _remote(src, dst, send_sems, recv_sems, k, to):
    return pltpu.make_async_remote_copy(src_ref=src, dst_ref=dst, send_sem=send_sems.at[k],
                                        recv_sem=recv_sems.at[k], device_id=to, device_id_type=MESH)


def _to_sibling(x, y, c, chips):
    return [(x, y, 1 - c)]


def _to_same_core_of_chips(x, y, c, chips):
    return [(*chip, c) for chip in chips]


def _to_all_gather_peers(x, y, c, chips):
    return _to_sibling(x, y, c, chips) + _to_same_core_of_chips(x, y, c, chips)


SIBLING_EXCHANGE = (1, _to_sibling)
CHIP_EXCHANGE = (2, _to_same_core_of_chips)
GATHER_EXCHANGE = (3, _to_all_gather_peers)
DIAGONAL_EXCHANGE = (4, lambda x, y, c, chips: [(x, y, 1 - c), (*chips[2], c)])


def _launch(body, name, out_shapes, arrays, sem_counts, sequencer=None, after=()):
    n, n_after = len(arrays), len(after)
    sems = [pltpu.SemaphoreType.DMA((k,)) for k in sem_counts]
    strip = lambda refs: refs[:n] + refs[n + n_after:]
    if sequencer is None:
        return pl.pallas_call(
            lambda *refs: body(*strip(refs)), name=name, out_shape=tuple(out_shapes),
            in_specs=[ANY] * (n + n_after), out_specs=tuple(ANY for _ in out_shapes), scratch_shapes=sems,
        )(*arrays, *after)
    collective_id, peers = sequencer

    def sequencer_body(*refs):
        x, y, c, chips = _place()
        barrier = pltpu.get_barrier_semaphore()
        targets = peers(x, y, c, chips)
        for target in targets:
            pl.semaphore_signal(barrier, inc=1, device_id=target, device_id_type=MESH)
        pl.semaphore_wait(barrier, len(targets))
        body(*strip(refs))

    return pl.kernel(
        sequencer_body, name=name, out_type=tuple(out_shapes),
        mesh=plsc.ScalarSubcoreMesh(axis_name="sequencer", num_cores=1), scratch_types=tuple(sems),
        compiler_params=pltpu.CompilerParams(collective_id=collective_id),
    )(*arrays, *after)


def _gather_weights(big, small, name, sequencer=None, after=(), relations=(0, 1, 2)):
    nb, ns = len(big), len(small)
    n_sem = 6 * nb + 3 * ns

    def body(*refs):
        ins, outs = refs[:nb + ns], refs[nb + ns:2 * (nb + ns)]
        send_sems, recv_sems = refs[2 * (nb + ns):]
        x, y, c, chips = _place()
        me, sibling = 2 * x + y, (x, y, 1 - c)

        def half(a, chip, hc):
            rh = big[a].shape[0] // 2
            return outs[a].at[2 * chip[0] + chip[1], pl.ds(hc * rh, rh), :]

        first, passed = [], []
        for a in range(nb):
            rh = big[a].shape[0] // 2
            for j, chip in [(j, chips[j]) for j in relations]:
                first.append(_remote(ins[a].at[pl.ds(c * rh, rh), :], half(a, (x, y), c),
                                     send_sems, recv_sems, 6 * a + j, (*chip, c)))
        for s in range(ns):
            for j, chip in enumerate(chips):
                first.append(_remote(ins[nb + s], outs[nb + s].at[me], send_sems, recv_sems,
                                     6 * nb + 3 * s + j, (*chip, c)))
        for cp in first:
            cp.start()
        for a in range(nb):
            for j, chip in [(j, chips[j]) for j in relations]:
                _remote(half(a, chip, c), half(a, chip, c), send_sems, recv_sems, 6 * a + j, (*chip, c)).wait_recv()
                fwd = _remote(half(a, chip, c), half(a, chip, c), send_sems, recv_sems, 6 * a + 3 + j, sibling)
                fwd.start()
                passed.append(fwd)
        for s in range(ns):
            for j, chip in enumerate(chips):
                dst = outs[nb + s].at[2 * chip[0] + chip[1]]
                _remote(dst, dst, send_sems, recv_sems, 6 * nb + 3 * s + j, (*chip, c)).wait_recv()
        for a in range(nb):
            for j, chip in [(j, chips[j]) for j in relations]:
                _remote(half(a, chip, 1 - c), half(a, chip, 1 - c), send_sems, recv_sems,
                        6 * a + 3 + j, sibling).wait_recv()
        for cp in first + passed:
            cp.wait_send()

    arrays = list(big) + list(small)
    out_shapes = [jax.ShapeDtypeStruct((N_CHIPS,) + a.shape, a.dtype) for a in arrays]
    return _launch(body, name, out_shapes, arrays, (n_sem, n_sem), sequencer, after)


def _swap_halves(parts, name, sequencer=None):
    n = len(parts)

    def body(*refs):
        ins, outs = refs[:n], refs[n:2 * n]
        send_sems, recv_sems = refs[2 * n:]
        x, y, c, _ = _place()
        copies = [_remote(ins[a].at[s, 1 - c], outs[a].at[s], send_sems, recv_sems, N_CHIPS * a + s, (x, y, 1 - c))
                  for a in range(n) for s in range(N_CHIPS)]
        for cp in copies:
            cp.start()
        for cp in copies:
            cp.wait()

    out_shapes = [jax.ShapeDtypeStruct((N_CHIPS,) + p.shape[2:], p.dtype) for p in parts]
    return _launch(body, name, out_shapes, parts, (N_CHIPS * n, N_CHIPS * n), sequencer)


def _scatter_to_owners(parts, name, sequencer=None):
    n = len(parts)

    def body(*refs):
        ins, outs = refs[:n], refs[n:2 * n]
        send_sems, recv_sems = refs[2 * n:]
        x, y, c, chips = _place()
        copies = [_remote(ins[a].at[2 * chip[0] + chip[1]], outs[a].at[j], send_sems, recv_sems,
                          3 * a + j, (*chip, c))
                  for a in range(n) for j, chip in enumerate(chips)]
        for cp in copies:
            cp.start()
        for cp in copies:
            cp.wait()

    out_shapes = [jax.ShapeDtypeStruct((3,) + p.shape[1:], p.dtype) for p in parts]
    return _launch(body, name, out_shapes, parts, (3 * n, 3 * n), sequencer)


def _send_to_sibling(halves, name, sequencer=None):
    n = len(halves)

    def body(*refs):
        ins, outs = refs[:n], refs[n:2 * n]
        send_sems, recv_sems = refs[2 * n:]
        x, y, c, _ = _place()
        copies = [_remote(ins[a], outs[a], send_sems, recv_sems, a, (x, y, 1 - c)) for a in range(n)]
        for cp in copies:
            cp.start()
        for cp in copies:
            cp.wait()

    out_shapes = [jax.ShapeDtypeStruct(p.shape, p.dtype) for p in halves]
    return _launch(body, name, out_shapes, halves, (n, n), sequencer)


N_DEV = 8


def _all_reduce_small(vec):
    def body(v_ref, gathered, total, send_sems, recv_sems):
        x, y, c, _ = _place()
        me = 4 * x + 2 * y + c
        gathered[me] = v_ref[...]
        copies = []
        for k in range(1, N_DEV):
            px = 1 - x if k & 4 else x
            py = 1 - y if k & 2 else y
            pc = 1 - c if k & 1 else c
            copies.append(_remote(v_ref, gathered.at[me], send_sems, recv_sems, k - 1, (px, py, pc)))
        for cp in copies:
            cp.start()
        for k, cp in enumerate(copies):
            cp.wait_send()
        for k in range(1, N_DEV):
            px = 1 - x if k & 4 else x
            py = 1 - y if k & 2 else y
            pc = 1 - c if k & 1 else c
            src = gathered.at[4 * px + 2 * py + pc]
            _remote(src, src, send_sems, recv_sems, k - 1, (px, py, pc)).wait_recv()
        acc = gathered[0]
        for dev in range(1, N_DEV):
            acc = acc + gathered[dev]
        total[...] = acc

    vm = pl.BlockSpec(memory_space=pltpu.VMEM)
    return pl.pallas_call(
        body, name="all_reduce_small",
        out_shape=(jax.ShapeDtypeStruct((N_DEV,) + vec.shape, F32), jax.ShapeDtypeStruct(vec.shape, F32)),
        in_specs=[vm], out_specs=(vm, vm),
        scratch_shapes=[pltpu.SemaphoreType.DMA((N_DEV - 1,)), pltpu.SemaphoreType.DMA((N_DEV - 1,))],
    )(vec)[1]


def _chip_sum(part, recv, c, chip):
    _, _, rh, cols = part.shape
    tr = _tile(rh, 256)

    def body(c_ref, chip_ref, p_ref, r_ref, own_ref, sb_ref):
        s = p_ref[...] + r_ref[...]
        sb_ref[...] = s.astype(BF16)

        @pl.when(pl.program_id(1) == chip_ref[0])
        def _():
            own_ref[...] = s

    blk = pl.BlockSpec((None, tr, cols), lambda i, s, c_ref, chip_ref: (s, i, 0))
    return pl.pallas_call(
        body, name="grad_chip_sum",
        out_shape=(jax.ShapeDtypeStruct(recv.shape[1:], F32), jax.ShapeDtypeStruct(recv.shape, BF16)),
        grid_spec=pltpu.PrefetchScalarGridSpec(
            num_scalar_prefetch=2, grid=(rh // tr, N_CHIPS),
            in_specs=[pl.BlockSpec((None, None, tr, cols), lambda i, s, c_ref, chip_ref: (s, c_ref[0], i, 0)), blk],
            out_specs=(pl.BlockSpec((tr, cols), lambda i, s, c_ref, chip_ref: (i, 0)), blk)),
        compiler_params=_params(("parallel", "arbitrary")),
    )(c, chip, part, recv)


def _owner_sum(own, recv):
    rh, cols = own.shape
    tr = _tile(rh, 256)

    def body(o_ref, r0, r1, r2, g_ref):
        g_ref[...] = ((o_ref[...] + r0[...].astype(F32)) + r1[...].astype(F32)) + r2[...].astype(F32)

    slot = lambda j: pl.BlockSpec((None, tr, cols), lambda i: (j, i, 0))
    row = pl.BlockSpec((tr, cols), lambda i: (i, 0))
    return pl.pallas_call(
        body, name="grad_owner_sum", out_shape=jax.ShapeDtypeStruct((rh, cols), F32), grid=(rh // tr,),
        in_specs=[row, slot(0), slot(1), slot(2)], out_specs=row,
        compiler_params=_params(("parallel",)),
    )(own, recv, recv, recv)


def _adamw_math(w, g, m, v):
    c1 = 1.0 / (1.0 - ADAM_B1 ** ADAM_STEP)
    c2 = 1.0 / (1.0 - ADAM_B2 ** ADAM_STEP)
    nm = ADAM_B1 * m + (1.0 - ADAM_B1) * g
    nv = ADAM_B2 * v + (1.0 - ADAM_B2) * (g * g)
    return -ADAM_LR * ((nm * c1) / (jnp.sqrt(nv * c2) + ADAM_EPS) + ADAM_WD * w), nm, nv


def _adamw_unit_rows(w, g, m, v, name):
    rows, _, cols = w.shape
    tr = max(d for d in range(1, 33) if rows % d == 0)

    def body(w_ref, g_ref, m_ref, v_ref, d_ref, nm_ref, nv_ref):
        d_ref[...], nm_ref[...], nv_ref[...] = _adamw_math(w_ref[...], g_ref[...], m_ref[...], v_ref[...])

    blk = pl.BlockSpec((tr, 1, cols), lambda i: (i, 0, 0))
    shape = jax.ShapeDtypeStruct(w.shape, F32)
    return pl.pallas_call(
        body, name=name, out_shape=(shape, shape, shape), grid=(rows // tr,),
        in_specs=[blk, blk, blk, blk], out_specs=(blk, blk, blk),
        compiler_params=_params(("parallel",)),
    )(w, g, m, v)


def _divisor_tile(n, want):
    return max(d for d in range(ROW_TILE, want + 1, ROW_TILE) if n % d == 0)


def _adamw(w, g, m, v, name):
    if w.ndim == 3:
        return _adamw_unit_rows(w, g, m, v, name)
    rows, cols = w.shape
    tr = _divisor_tile(rows, 2048) if rows % 8 == 0 else rows
    c1 = 1.0 / (1.0 - ADAM_B1 ** ADAM_STEP)
    c2 = 1.0 / (1.0 - ADAM_B2 ** ADAM_STEP)

    def body(w_ref, g_ref, m_ref, v_ref, d_ref, nm_ref, nv_ref):
        gv = g_ref[...]
        nm = ADAM_B1 * m_ref[...] + (1.0 - ADAM_B1) * gv
        nv = ADAM_B2 * v_ref[...] + (1.0 - ADAM_B2) * (gv * gv)
        d_ref[...] = -ADAM_LR * ((nm * c1) / (jnp.sqrt(nv * c2) + ADAM_EPS) + ADAM_WD * w_ref[...])
        nm_ref[...] = nm
        nv_ref[...] = nv

    blk = pl.BlockSpec((tr, cols), lambda i: (i, 0))
    shape = jax.ShapeDtypeStruct((rows, cols), F32)
    return pl.pallas_call(
        body, name=name, out_shape=(shape, shape, shape), grid=(rows // tr,),
        in_specs=[blk, blk, blk, blk], out_specs=(blk, blk, blk),
        compiler_params=_params(("parallel",)),
    )(w, g, m, v)


def _adamw_halves(w, g_own, g_sib, m, v, c, name):
    _, rows, cols = w.shape
    rh = rows // 2
    tr = _tile(rh, 256)
    per = rh // tr
    c1 = 1.0 / (1.0 - ADAM_B1 ** ADAM_STEP)
    c2 = 1.0 / (1.0 - ADAM_B2 ** ADAM_STEP)

    def body(c_ref, w_ref, go_ref, gs_ref, m_ref, v_ref, g_ref, d_ref, nm_ref, nv_ref):
        own = pl.program_id(0) // per == c_ref[0]
        gv = jnp.where(own, go_ref[...], gs_ref[...])
        nm = ADAM_B1 * m_ref[...] + (1.0 - ADAM_B1) * gv
        nv = ADAM_B2 * v_ref[...] + (1.0 - ADAM_B2) * (gv * gv)
        g_ref[...] = gv
        d_ref[...] = -ADAM_LR * ((nm * c1) / (jnp.sqrt(nv * c2) + ADAM_EPS) + ADAM_WD * w_ref[...])
        nm_ref[...] = nm
        nv_ref[...] = nv

    blk = pl.BlockSpec((None, tr, cols), lambda i, c_ref: (0, i, 0))
    half = pl.BlockSpec((tr, cols), lambda i, c_ref: (i % per, 0))
    shape = jax.ShapeDtypeStruct((1, rows, cols), F32)
    return pl.pallas_call(
        body, name=name, out_shape=(shape, shape, shape, shape),
        grid_spec=pltpu.PrefetchScalarGridSpec(
            num_scalar_prefetch=1, grid=(rows // tr,),
            in_specs=[blk, half, half, blk, blk], out_specs=(blk, blk, blk, blk)),
        compiler_params=_params(("parallel",)),
    )(c, w, g_own, g_sib, m, v)


def _by_shard(name, full):
    if name in ("w_in", "w_ff1"):
        st = full
    else:
        st = full.reshape(N_CHIPS, -1, full.shape[1])
    return st.reshape(N_CHIPS, 2, st.shape[1] // 2, st.shape[2])


class _GradReducer:
    def __init__(self, w, m, v, my_c, my_chip):
        self.w, self.m, self.v, self.my_c, self.my_chip = w, m, v, my_c, my_chip
        self.in_flight = []
        self.computed = []
        self.anchor = None
        self.done = {}

    def step(self, name=None, full=None, anchor=None):
        stages, self.in_flight, self.computed, self.anchor = self.in_flight, [], [], anchor
        for stage in [s for s in stages if not getattr(s, "long", False)]:
            self._advance(stage)
        if name is not None:
            self.in_flight.append(self._swap(name, _by_shard(name, full)))
        for stage in [s for s in stages if getattr(s, "long", False)]:
            self._advance(stage)
        return self.computed

    def _held(self, value):
        if self.anchor is None:
            return value
        return lax.optimization_barrier((value, self.anchor))[0]

    def _advance(self, stage):
        nxt = stage()
        if nxt is not None:
            self.in_flight.append(nxt)

    def finish(self):
        while self.in_flight:
            self.step()
        return self.done

    def _swap(self, name, part):
        got, = _swap_halves([part], "grad_swap_" + name, SIBLING_EXCHANGE)

        def scatter():
            total, total_bf16 = _chip_sum(part, self._held(got), self.my_c, self.my_chip.reshape(1))
            self.computed.append(total_bf16)
            recv, = _scatter_to_owners([total_bf16], "grad_scatter_" + name, CHIP_EXCHANGE)

            def send():
                half = _owner_sum(total, self._held(recv))
                self.computed.append(half)
                sib, = _send_to_sibling([half], "grad_send_" + name, SIBLING_EXCHANGE)

                def update():
                    self.done[name] = _adamw_halves(self.w[name], half, self._held(sib), self.m[name], self.v[name],
                                                    self.my_c, "adamw_" + name)
                    self.computed.append(self.done[name][0])

                if name == "w_in":
                    self.in_proj_halves = (half, sib)
                    return None
                return update
            return lambda: send
        scatter.long = True
        return scatter


def _adamw_minor_rows(w, g, m, v, name):
    _, rows, cols = w.shape
    turned = lambda a: jnp.transpose(a, (2, 0, 1))
    back = lambda a: jnp.transpose(a, (1, 2, 0))
    g = g.T.reshape(cols, 1, rows)
    delta, new_m, new_v = _adamw(turned(w), g, turned(m), turned(v), name)
    return back(g), back(delta), back(new_m), back(new_v)


def _in_proj_quarter_part(shard, chip, h):
    e = h // 2
    qw = shard.shape[1] - e
    zeros = jnp.zeros((shard.shape[0], h), shard.dtype)
    padded = jnp.concatenate([zeros, shard, zeros], axis=1)
    at_quarter = jnp.where(chip == 0, h, jnp.where(chip == 1, h - e, jnp.where(chip == 2, 2 * h, h + e)))
    at_edge = jnp.where(chip == 0, h + qw, jnp.where(chip == 1, h + qw - e, jnp.where(chip == 2, h, h - e)))
    return (lax.dynamic_slice_in_dim(padded, at_quarter, qw, axis=1).astype(BF16),
            lax.dynamic_slice_in_dim(padded, at_edge, h, axis=1))


def _in_proj_quarters(parts, edges, h):
    e = h // 2
    x1, a, b, x2 = edges[0][:, :e], edges[1], edges[2], edges[3][:, e:]
    parts = parts.at[1, :, :e].set(x1.astype(BF16))
    parts = parts.at[2, :, parts.shape[2] - e:].set(x2.astype(BF16))
    pad = jnp.zeros((a.shape[0], HEAD_DIM - h), a.dtype)
    return parts, jnp.concatenate([a, pad, b, pad], axis=1).astype(BF16)


def _in_proj_edge_columns(d_main, d_ab, h):
    e = h // 2
    return jnp.concatenate([d_main[1][:, :e], d_main[2][:, -e:], d_ab[:, :h], d_ab[:, HEAD_DIM:HEAD_DIM + h]], axis=1)


def _in_proj_shard_grad(q_own, q_sib, edges, c, chip, h):
    e = h // 2
    lower, upper = jnp.where(c[0] == 0, q_own, q_sib), jnp.where(c[0] == 0, q_sib, q_own)
    quarter = jnp.concatenate([lower, upper], axis=0)
    x1, x2, a, b = edges[:, :e], edges[:, e:2 * e], edges[:, 2 * e:2 * e + h], edges[:, 2 * e + h:]
    zeros = jnp.zeros_like(x1)
    left = jnp.where(chip == 2, b, jnp.concatenate([zeros, x2], axis=1))
    right = jnp.where(chip == 1, a, jnp.concatenate([x1, zeros], axis=1))
    start = jnp.where(chip == 0, h, jnp.where(chip == 1, h + e, jnp.where(chip == 2, 0, h - e)))
    padded = jnp.concatenate([left, quarter, right], axis=1)
    return lax.dynamic_slice_in_dim(padded, start, quarter.shape[1] + e, axis=1)


SMALL = ("gdn_a_log", "gdn_dt_bias", "gdn_norm_w", "hgrn_lb_logits", "hgrn_norm_w",
         "norm_mix_w", "norm_ffn_w", "norm_final_w")
BIG = ("w_in", "w_out", "w_ff1", "w_ff2")
ORDER = ("w_in", "conv_w", "gdn_a_log", "gdn_dt_bias", "gdn_norm_w", "hgrn_lb_logits", "hgrn_norm_w",
         "w_out", "norm_mix_w", "norm_ffn_w", "w_ff1", "w_ff2", "norm_final_w")


def _pack(pieces):
    flat = jnp.concatenate([p.reshape(-1).astype(F32) for p in pieces])
    rows = -(-flat.shape[0] // (8 * HEAD_DIM)) * 8
    return jnp.pad(flat, (0, rows * HEAD_DIM - flat.shape[0])).reshape(rows, HEAD_DIM)


def _unpack(packed, shapes):
    flat, out, at = packed.reshape(-1), [], 0
    for s in shapes:
        n = 1
        for dim in s:
            n *= dim
        out.append(flat[at:at + n].reshape(s))
        at += n
    return out


def kernel(x, w_in, conv_w, gdn_a_log, gdn_dt_bias, gdn_norm_w, hgrn_lb_logits, hgrn_norm_w, w_out, norm_mix_w, norm_ffn_w, w_ff1, w_ff2, norm_final_w, loss_target, m_w_in, m_conv_w, m_gdn_a_log, m_gdn_dt_bias, m_gdn_norm_w, m_hgrn_lb_logits, m_hgrn_norm_w, m_w_out, m_norm_mix_w, m_norm_ffn_w, m_w_ff1, m_w_ff2, m_norm_final_w, v_w_in, v_conv_w, v_gdn_a_log, v_gdn_dt_bias, v_gdn_norm_w, v_hgrn_lb_logits, v_hgrn_norm_w, v_w_out, v_norm_mix_w, v_norm_ffn_w, v_w_ff1, v_w_ff2, v_norm_final_w):
    w = dict(w_in=w_in, conv_w=conv_w, gdn_a_log=gdn_a_log, gdn_dt_bias=gdn_dt_bias, gdn_norm_w=gdn_norm_w,
             hgrn_lb_logits=hgrn_lb_logits, hgrn_norm_w=hgrn_norm_w, w_out=w_out, norm_mix_w=norm_mix_w,
             norm_ffn_w=norm_ffn_w, w_ff1=w_ff1, w_ff2=w_ff2, norm_final_w=norm_final_w)
    m = dict(w_in=m_w_in, conv_w=m_conv_w, gdn_a_log=m_gdn_a_log, gdn_dt_bias=m_gdn_dt_bias,
             gdn_norm_w=m_gdn_norm_w, hgrn_lb_logits=m_hgrn_lb_logits, hgrn_norm_w=m_hgrn_norm_w,
             w_out=m_w_out, norm_mix_w=m_norm_mix_w, norm_ffn_w=m_norm_ffn_w, w_ff1=m_w_ff1, w_ff2=m_w_ff2,
             norm_final_w=m_norm_final_w)
    v = dict(w_in=v_w_in, conv_w=v_conv_w, gdn_a_log=v_gdn_a_log, gdn_dt_bias=v_gdn_dt_bias,
             gdn_norm_w=v_gdn_norm_w, hgrn_lb_logits=v_hgrn_lb_logits, hgrn_norm_w=v_hgrn_norm_w,
             w_out=v_w_out, norm_mix_w=v_norm_mix_w, norm_ffn_w=v_norm_ffn_w, w_ff1=v_w_ff1, w_ff2=v_w_ff2,
             norm_final_w=v_norm_final_w)
    d = x.shape[-1]
    h = d // (2 * HEAD_DIM)
    my_c = lax.axis_index("c").astype(jnp.int32).reshape(1)
    my_chip = (2 * lax.axis_index("x") + lax.axis_index("y")).astype(jnp.int32)

    shards = [w[n][0].astype(BF16) for n in BIG]
    conv_shard = jnp.pad(conv_w[0], ((0, 8 - CONV_W), (0, 0)))
    quarter_part, edge_part = _in_proj_quarter_part(w_in[0], my_chip, h)
    first = _gather_weights([quarter_part], [conv_shard, edge_part], "gather_in_proj", GATHER_EXCHANGE,
                            relations=(0, 1))
    diagonal, = _gather_weights([quarter_part], [], "gather_in_proj_diagonal", DIAGONAL_EXCHANGE, relations=(2,))
    n1 = _rms_fwd(x[0], norm_mix_w[0], "rms_mix")
    gathered_in, n1, *shards[1:] = lax.optimization_barrier((first[0], n1, *shards[1:]))
    own_slot = lambda st, own: lax.dynamic_update_index_in_dim(st, own, my_chip, 0)
    f_conv, f_edges = own_slot(first[1], conv_shard), own_slot(first[2], edge_part)
    early, w_ab = _in_proj_quarters(own_slot(gathered_in, quarter_part), f_edges, h)
    w_main = (early, diagonal, f_edges, my_chip)
    cols = lambda st: st.transpose(1, 0, 2).reshape(st.shape[1], -1)
    conv_full = cols(f_conv[:, :CONV_W])
    rest = _gather_weights(shards[1:], [], "gather_rest", GATHER_EXCHANGE, after=[w_ab])

    def late_weights(anchor):
        held = lax.optimization_barrier((*rest, anchor))[:len(rest)]
        f_out, f_ff1, f_ff2 = (own_slot(st, own) for st, own in zip(held, shards[1:]))
        return f_out.reshape(-1, d), f_ff1, f_ff2.reshape(-1, d)

    reducer = _GradReducer(w, m, v, my_c, my_chip)
    loss, dx, g = _local_step(
        x[0], loss_target[0], w_main, w_ab, conv_full, gdn_a_log[0], gdn_dt_bias[0], gdn_norm_w[0],
        hgrn_lb_logits, hgrn_norm_w[0], late_weights, norm_mix_w[0], norm_ffn_w[0],
        None, None, norm_final_w, reducer, n1)

    grads, delta, new_m, new_v = {}, {}, {}, {}
    for n, out in reducer.finish().items():
        grads[n], delta[n], new_m[n], new_v[n] = out

    edges = _in_proj_edge_columns(g["w_main"], g["w_ab"], h)
    small_shapes = [w[n].shape for n in SMALL] + [conv_full.shape, (1,), edges.shape]
    total = _all_reduce_small(_pack([g[n] for n in SMALL] + [g["conv_w"], loss[0, :1], edges]))
    *small_grads, conv_grad, loss_sum, edges = _unpack(total, small_shapes)
    g_in = _in_proj_shard_grad(*reducer.in_proj_halves, edges, my_c, my_chip, h)
    grads["w_in"], delta["w_in"], new_m["w_in"], new_v["w_in"] = _adamw_minor_rows(
        w["w_in"], g_in, m["w_in"], v["w_in"], "adamw_w_in")
    for n, sg in zip(SMALL, small_grads):
        grads[n] = sg
    shard_cols = conv_w.shape[-1]
    grads["conv_w"] = lax.dynamic_slice_in_dim(conv_grad, my_chip * shard_cols, shard_cols, axis=1)[None]

    packed_names = SMALL + ("conv_w",)
    packed = [_pack([t[n] for n in packed_names]) for t in (w, grads, m, v)]
    outs = _adamw(*packed, "adamw_small")
    shapes = [w[n].shape for n in packed_names]
    for res, o in zip((delta, new_m, new_v), outs):
        for n, a in zip(packed_names, _unpack(o, shapes)):
            res[n] = a

    return (loss_sum.reshape(()), dx[None], *[grads[n] for n in ORDER], *[delta[n] for n in ORDER],
            *[new_m[n] for n in ORDER], *[new_v[n] for n in ORDER])
```

```python
import functools

import jax
import jax.numpy as jnp
from jax import lax
from jax.experimental import pallas as pl
from jax.experimental.pallas import tpu as pltpu
from jax.experimental.pallas import tpu_sc as plsc

F32 = jnp.float32
BF16 = jnp.bfloat16

HEAD_DIM = 128
CHUNK = 128
SUB = 16
EXP_CAP = 80.0
NORM_EPS = 1e-6
L2_EPS = 1e-6
CONV_W = 4
VMEM_LIMIT = 56 * 1024 * 1024

ADAM_LR, ADAM_B1, ADAM_B2, ADAM_EPS, ADAM_WD, ADAM_STEP = 1e-3, 0.9, 0.999, 1e-8, 0.01, 10

NN = ((1,), (0,))
NT = ((1,), (1,))
TN = ((0,), (0,))
MESH = pl.DeviceIdType.MESH


def _dot(a, b, dims):
    return lax.dot_general(a.astype(BF16), b.astype(BF16), (dims, ((), ())),
                           preferred_element_type=F32)


def _split(a):
    hi = a.astype(BF16)
    return hi, (a - hi.astype(F32)).astype(BF16)


def _dot3(a, b, dims):
    ah, al = _split(a)
    bh, bl = _split(b)
    d = lambda x, y: lax.dot_general(x, y, (dims, ((), ())), preferred_element_type=F32)
    return d(ah, bh) + (d(ah, bl) + d(al, bh))


def _sigmoid(x):
    return 1.0 / (1.0 + jnp.exp(-x))


def _silu(x):
    return x * _sigmoid(x)


def _dsilu(x):
    s = _sigmoid(x)
    return s * (1.0 + x * (1.0 - s))


def _softplus(x):
    e = jnp.exp(-jnp.abs(x))
    u = 1.0 + e
    log1p = jnp.where(u == 1.0, e, jnp.log(u) * (e / jnp.where(u == 1.0, 1.0, u - 1.0)))
    return jnp.maximum(x, 0.0) + log1p


def _iota(shape, axis):
    return lax.broadcasted_iota(jnp.int32, shape, axis)


def _cumsum_rows(x):
    n = x.shape[0]
    row = _iota(x.shape, 0)
    s = 1
    while s < n:
        x = x + jnp.where(row >= s, pltpu.roll(x, s, 0), 0.0)
        s *= 2
    return x


def _rev_cumsum_rows(x):
    return jnp.sum(x, axis=0, keepdims=True) - _cumsum_rows(x) + x


def _params(sem):
    return pltpu.CompilerParams(dimension_semantics=sem, vmem_limit_bytes=VMEM_LIMIT)


ROW_TILE = 8
HEADS_PER_STEP = 8


def _hps(h):
    return min(HEADS_PER_STEP, h)


def _head_view(ref, hb):
    if len(ref.shape) == 2:
        return ref.at[:, pl.ds(hb * HEAD_DIM, HEAD_DIM)]
    return ref.at[hb]


class _Staged:
    def __init__(self, ref, load):
        self.ref = ref
        self.loaded = ref[...] if load else None
        self.written = None

    def __getitem__(self, idx):
        return self.loaded

    def __setitem__(self, idx, value):
        self.written = value


def _each_head(one_head, n_in):
    def body(*refs):
        @pl.when(pl.program_id(1) == 0)
        def _():
            refs[-1][...] = jnp.zeros_like(refs[-1])

        last = len(refs) - 1
        staged = [[_Staged(_head_view(r, hb), i < n_in or i == last) for i, r in enumerate(refs)]
                  for hb in range(refs[-1].shape[0])]
        running = [one_head(*per_head) for per_head in staged]
        while running:
            for gen in list(running):
                try:
                    next(gen)
                except StopIteration:
                    running.remove(gen)
        for per_head in staged:
            for s in per_head:
                if s.written is not None:
                    s.ref[...] = s.written
    return body


def _tile(n, want):
    t = min(n, want)
    while n % t:
        t //= 2
    return t


def _mm(a, b, mode, out_dtypes, name, epi=None, extras=(), tm=1024, tn=1024, tk=2048,
        b_stacked=False, out_stacked=False):
    if mode == "tn":
        kdim, m = a.shape
    else:
        m, kdim = a.shape
    if b_stacked:
        n = N_CHIPS * b.shape[2] if mode == "nn" else b.shape[1]
        kdim_b = b.shape[1] if mode == "nn" else N_CHIPS * b.shape[2]
        assert kdim_b == kdim
    else:
        n = b.shape[0] if mode == "nt" else b.shape[1]
    per_shard = (n if (mode == "nn" or out_stacked) else kdim) // N_CHIPS
    tm, tn, tk = _tile(m, tm), _tile(n, tn), _tile(kdim, tk)
    if (b_stacked and mode == "nn") or out_stacked:
        tn = _tile(per_shard, tn)
    if b_stacked and mode == "nt":
        tk = _tile(per_shard, tk)
    nk = kdim // tk
    dims = {"nn": NN, "nt": NT, "tn": TN}[mode]
    a_spec = (pl.BlockSpec((tk, tm), lambda i, j, k: (k, i)) if mode == "tn"
              else pl.BlockSpec((tm, tk), lambda i, j, k: (i, k)))
    if b_stacked and mode == "nn":
        per = per_shard // tn
        b_spec = pl.BlockSpec((None, tk, tn), lambda i, j, k: (j // per, k, j % per))
    elif b_stacked:
        per = per_shard // tk
        b_spec = pl.BlockSpec((None, tn, tk), lambda i, j, k: (k // per, j, k % per))
    else:
        b_spec = (pl.BlockSpec((tn, tk), lambda i, j, k: (j, k)) if mode == "nt"
                  else pl.BlockSpec((tk, tn), lambda i, j, k: (k, j)))
    mn_spec = pl.BlockSpec((tm, tn), lambda i, j, k: (i, j))
    if out_stacked:
        per_o = per_shard // tn
        out_spec = pl.BlockSpec((None, tm, tn), lambda i, j, k: (j // per_o, i, j % per_o))
        out_shape = (N_CHIPS, m, per_shard)
    else:
        out_spec, out_shape = mn_spec, (m, n)
    ne, no = len(extras), len(out_dtypes)
    if epi is None:
        epi = lambda acc: (acc,)

    def body(a_ref, b_ref, *rest):
        extra_refs, out_refs = rest[:ne], rest[ne:ne + no]
        part = _dot(a_ref[...], b_ref[...], dims)

        def finish(total):
            outs = epi(total, *[r[...] for r in extra_refs])
            for o_ref, o in zip(out_refs, outs):
                o_ref[...] = o.astype(o_ref.dtype)

        if nk == 1:
            finish(part)
            return
        acc = rest[-1]
        k = pl.program_id(2)

        @pl.when(k == 0)
        def _():
            acc[...] = part

        @pl.when(jnp.logical_and(k > 0, k < nk - 1))
        def _():
            acc[...] += part

        @pl.when(k == nk - 1)
        def _():
            finish(acc[...] + part)

    outs = pl.pallas_call(
        body, name=name,
        out_shape=tuple(jax.ShapeDtypeStruct(out_shape, d) for d in out_dtypes),
        grid=(m // tm, n // tn, nk),
        in_specs=[a_spec, b_spec] + [mn_spec] * ne,
        out_specs=tuple(out_spec for _ in out_dtypes),
        scratch_shapes=[pltpu.VMEM((tm, tn), F32)] if nk > 1 else [],
        compiler_params=_params(("parallel", "parallel", "arbitrary")),
    )(a, b, *extras)
    return outs if no > 1 else outs[0]


def _mm_quarters(a, stack, quarters, name, into=None, tm=1024, tn=1024):
    m, kdim = a.shape
    qw = stack.shape[2]
    tm, tn = _tile(m, tm), _tile(qw, tn)
    per = qw // tn

    def body(q_ref, a_ref, b_ref, *rest):
        rest[-1][...] = _dot(a_ref[...], b_ref[...], NN)

    in_specs = [pl.BlockSpec((tm, kdim), lambda i, j, q: (i, 0)),
                pl.BlockSpec((None, kdim, tn), lambda i, j, q: (q[j // per], 0, j % per))]
    args = (quarters, a, stack)
    if into is not None:
        in_specs.append(ANY)
        args += (into,)
    return pl.pallas_call(
        body, name=name, out_shape=jax.ShapeDtypeStruct((m, N_CHIPS * qw), F32),
        grid_spec=pltpu.PrefetchScalarGridSpec(
            num_scalar_prefetch=1, grid=(m // tm, quarters.shape[0] * per), in_specs=in_specs,
            out_specs=pl.BlockSpec((tm, tn), lambda i, j, q: (i, q[j // per] * per + j % per))),
        input_output_aliases={3: 0} if into is not None else {},
        compiler_params=_params(("parallel", "parallel")),
    )(*args)


ROWS = 256


def _rms_fwd(x, w, name):
    t, d = x.shape
    tr = _tile(t, ROWS)

    def body(x_ref, w_ref, n_ref):
        xv = x_ref[...]
        r = lax.rsqrt(jnp.mean(xv * xv, axis=-1, keepdims=True) + NORM_EPS)
        n_ref[...] = (xv * r * w_ref[...]).astype(n_ref.dtype)

    return pl.pallas_call(
        body, name=name, out_shape=jax.ShapeDtypeStruct((t, d), BF16), grid=(t // tr,),
        in_specs=[pl.BlockSpec((tr, d), lambda i: (i, 0)), pl.BlockSpec((1, d), lambda i: (0, 0))],
        out_specs=pl.BlockSpec((tr, d), lambda i: (i, 0)),
        compiler_params=_params(("parallel",)),
    )(x, w.reshape(1, d))


def _rms_bwd(dn, x, w, dres, name):
    t, d = x.shape
    tr = _tile(t, ROWS)

    def body(dn_ref, x_ref, w_ref, dres_ref, dx_ref, dxb_ref, dw_ref):
        i = pl.program_id(0)
        xv, dnv = x_ref[...], dn_ref[...]
        r = lax.rsqrt(jnp.mean(xv * xv, axis=-1, keepdims=True) + NORM_EPS)
        xh = xv * r
        dxh = dnv * w_ref[...]
        dx = dres_ref[...] + r * (dxh - xh * jnp.mean(dxh * xh, axis=-1, keepdims=True))
        dx_ref[...] = dx
        dxb_ref[...] = dx.astype(BF16)

        @pl.when(i == 0)
        def _():
            dw_ref[...] = jnp.zeros_like(dw_ref)

        dw_ref[...] += jnp.sum(dnv * xh, axis=0, keepdims=True)

    row = pl.BlockSpec((tr, d), lambda i: (i, 0))
    vec = pl.BlockSpec((1, d), lambda i: (0, 0))
    return pl.pallas_call(
        body, name=name,
        out_shape=(jax.ShapeDtypeStruct((t, d), F32), jax.ShapeDtypeStruct((t, d), BF16),
                   jax.ShapeDtypeStruct((1, d), F32)),
        grid=(t // tr,), in_specs=[row, row, vec, row], out_specs=(row, row, vec),
        compiler_params=_params(("arbitrary",)),
    )(dn, x, w.reshape(1, d), dres)


def _loss_head(h, w, target):
    t, d = h.shape
    tr = _tile(t, ROWS)

    def body(h_ref, w_ref, t_ref, loss_ref, dh_ref, dhb_ref, dw_ref):
        i = pl.program_id(0)
        hv, wv = h_ref[...], w_ref[...]
        r = lax.rsqrt(jnp.mean(hv * hv, axis=-1, keepdims=True) + NORM_EPS)
        hh = hv * r
        err = hh * wv - t_ref[...]
        dout = err * (1.0 / d)
        dhh = dout * wv
        dh = r * (dhh - hh * jnp.mean(dhh * hh, axis=-1, keepdims=True))
        dh_ref[...] = dh
        dhb_ref[...] = dh.astype(BF16)

        @pl.when(i == 0)
        def _():
            dw_ref[...] = jnp.zeros_like(dw_ref)
            loss_ref[...] = jnp.zeros_like(loss_ref)

        dw_ref[...] += jnp.sum(dout * hh, axis=0, keepdims=True)
        loss_ref[...] += jnp.full((1, 128), 0.5 / d, F32) * jnp.sum(err * err)

    row = pl.BlockSpec((tr, d), lambda i: (i, 0))
    vec = pl.BlockSpec((1, d), lambda i: (0, 0))
    lspec = pl.BlockSpec((1, 128), lambda i: (0, 0))
    return pl.pallas_call(
        body, name="loss_head",
        out_shape=(jax.ShapeDtypeStruct((1, 128), F32), jax.ShapeDtypeStruct((t, d), F32),
                   jax.ShapeDtypeStruct((t, d), BF16), jax.ShapeDtypeStruct((1, d), F32)),
        grid=(t // tr,), in_specs=[row, vec, row], out_specs=(lspec, row, row, vec),
        compiler_params=_params(("arbitrary",)),
    )(h, w.reshape(1, d), target)


def _inv_unit_lower(a):
    c = a.shape[0]
    eye = (_iota((c, c), 0) == _iota((c, c), 1)).astype(F32)
    x = eye - a
    p = _dot3(a, a, NN)
    yield
    n = 2
    while n < c:
        x = x + _dot3(x, p, NN)
        n *= 2
        if n < c:
            p = _dot3(p, p, NN)
        yield
    return x


def _gdn_chunk(q, k, v, beta, g):
    c = q.shape[0]
    row, col = _iota((c, c), 0), _iota((c, c), 1)
    gc = _cumsum_rows(g)
    diff = gc - gc.T
    dec = jnp.where(row >= col, jnp.exp(jnp.minimum(diff, 0.0)), 0.0)
    dec_s = jnp.where(row > col, dec, 0.0)
    gam = jnp.exp(gc)
    g_last = jnp.sum(g, axis=0, keepdims=True)
    kk = _dot(k, k, NT)
    a = beta * kk * dec_s
    p = _dot(q, k, NT) * dec
    e_end = jnp.exp(g_last - gc)
    return dict(dec=dec, dec_s=dec_s, gam=gam, gam_last=jnp.exp(g_last), e_end=e_end,
                k_end=k * e_end, kk=kk, a=a, p=p)


def _gdn_fwd(q, k, v, beta_bc, g_bc):
    t = q.shape[0]
    h = q.shape[1] // HEAD_DIM
    nc = t // CHUNK

    def body(q_ref, k_ref, v_ref, b_ref, g_ref, o_ref, s_ref, t_ref, state):
        qv, kv, vv, beta = q_ref[...], k_ref[...], v_ref[...], b_ref[...]
        ch = _gdn_chunk(qv, kv, vv, beta, g_ref[...])
        yield
        tm = yield from _inv_unit_lower(ch["a"])
        sol = _dot(tm, jnp.concatenate([beta * vv, beta * ch["gam"] * kv], axis=1), NN)
        yield
        u_v, w = sol[:, :HEAD_DIM], sol[:, HEAD_DIM:]
        s0 = state[...]
        u = u_v - _dot(w, s0, NN)
        yield
        o_ref[...] = _dot(qv * ch["gam"], s0, NN) + _dot(ch["p"], u, NN)
        s_ref[...] = s0
        t_ref[...] = tm
        state[...] = ch["gam_last"] * s0 + _dot(ch["k_end"], u, TN)

    tok = pl.BlockSpec((CHUNK, _hps(h) * HEAD_DIM), lambda hh, c: (c, hh))
    bc = pl.BlockSpec((_hps(h), CHUNK, HEAD_DIM), lambda hh, c: (hh, c, 0))
    mat = pl.BlockSpec((_hps(h), None, HEAD_DIM, HEAD_DIM), lambda hh, c: (hh, c, 0, 0))
    return pl.pallas_call(
        _each_head(body, 5), name="gdn_fwd",
        out_shape=(jax.ShapeDtypeStruct(q.shape, F32),
                   jax.ShapeDtypeStruct((h, nc, HEAD_DIM, HEAD_DIM), F32),
                   jax.ShapeDtypeStruct((h, nc, CHUNK, CHUNK), F32)),
        grid=(h // _hps(h), nc), in_specs=[tok, tok, tok, bc, bc], out_specs=(tok, mat, mat),
        scratch_shapes=[pltpu.VMEM((_hps(h), HEAD_DIM, HEAD_DIM), F32)],
        compiler_params=_params(("parallel", "arbitrary")),
    )(q, k, v, beta_bc, g_bc)


def _gdn_bwd(q, k, v, beta_bc, g_bc, states, invs, do, do_blk=0):
    t = q.shape[0]
    h = q.shape[1] // HEAD_DIM
    nc = t // CHUNK

    def body(q_ref, k_ref, v_ref, b_ref, g_ref, s_ref, t_ref, do_ref,
             dq_ref, dk_ref, dv_ref, db_ref, dg_ref, dstate):
        qv, kv, vv, beta = q_ref[...], k_ref[...], v_ref[...], b_ref[...]
        dov, s0, tm, ds1 = do_ref[...], s_ref[...], t_ref[...], dstate[...]
        ch = _gdn_chunk(qv, kv, vv, beta, g_ref[...])
        yield
        gam, dec, dec_s, kk = ch["gam"], ch["dec"], ch["dec_s"], ch["kk"]
        r_v, r_w = beta * vv, beta * gam * kv
        sol = _dot(tm, jnp.concatenate([r_v, r_w], axis=1), NN)
        yield
        u_v, w = sol[:, :HEAD_DIM], sol[:, HEAD_DIM:]
        u = u_v - _dot(w, s0, NN)
        qg = qv * gam
        yield

        du = _dot(ch["p"], dov, TN) + _dot(ch["k_end"], ds1, NN)
        dp = _dot(dov, u, NT)
        dpd = dp * dec
        dqg = _dot(dov, s0, NT)
        dk_end = _dot(u, ds1, NT)
        yield
        dq = dqg * gam + _dot(dpd, kv, NN)
        dk = _dot(dpd, qv, TN) + dk_end * ch["e_end"]
        dstate[...] = _dot(qg, dov, TN) + ch["gam_last"] * ds1 - _dot(w, du, TN)
        dw = -_dot(du, s0, NT)
        yield
        dr = _dot(tm, jnp.concatenate([du, dw], axis=1), TN)
        yield
        dr_v, dr_w = dr[:, :HEAD_DIM], dr[:, HEAD_DIM:]
        da = -_dot(dr, sol, NT)
        yield
        dkk = da * beta * dec_s
        dk = dk + _dot(dkk, kv, NN) + _dot(dkk, kv, TN) + beta * gam * dr_w
        dbeta = (jnp.sum(da * kk * dec_s, axis=1, keepdims=True)
                 + jnp.sum(dr_v * vv + dr_w * gam * kv, axis=1, keepdims=True))

        pair = dp * ch["p"] + da * ch["a"]
        end = jnp.sum(dk_end * ch["k_end"], axis=1, keepdims=True)
        dgc = (jnp.sum(pair - pair.T, axis=1, keepdims=True)
               + jnp.sum(dqg * qg + dr_w * r_w, axis=1, keepdims=True) - end)
        at_end = jnp.sum(end) + ch["gam_last"] * jnp.sum(s0 * ds1)
        dgc = jnp.broadcast_to(dgc, (CHUNK, HEAD_DIM))
        dgc = dgc + jnp.where(_iota((CHUNK, HEAD_DIM), 0) == CHUNK - 1, at_end, 0.0)
        dq_ref[...] = dq
        dk_ref[...] = dk
        dv_ref[...] = beta * dr_v
        db_ref[...] = jnp.broadcast_to(dbeta, (CHUNK, HEAD_DIM)).T[:ROW_TILE]
        dg_ref[...] = _rev_cumsum_rows(dgc).T[:ROW_TILE]

    rev = lambda c: nc - 1 - c
    tok = pl.BlockSpec((CHUNK, _hps(h) * HEAD_DIM), lambda hh, c: (rev(c), hh))
    bc = pl.BlockSpec((_hps(h), CHUNK, HEAD_DIM), lambda hh, c: (hh, rev(c), 0))
    mat = pl.BlockSpec((_hps(h), None, HEAD_DIM, HEAD_DIM), lambda hh, c: (hh, rev(c), 0, 0))
    tok_shape = jax.ShapeDtypeStruct(q.shape, F32)
    row_shape = jax.ShapeDtypeStruct((h, nc, ROW_TILE, CHUNK), F32)
    rows = pl.BlockSpec((_hps(h), None, ROW_TILE, CHUNK), lambda hh, c: (hh, rev(c), 0, 0))
    return pl.pallas_call(
        _each_head(body, 8), name="gdn_bwd",
        out_shape=(tok_shape, tok_shape, tok_shape, row_shape, row_shape),
        grid=(h // _hps(h), nc),
        in_specs=[tok, tok, tok, bc, bc, mat, mat,
                  pl.BlockSpec((CHUNK, _hps(h) * HEAD_DIM), lambda hh, c: (rev(c), do_blk // _hps(h) + hh))],
        out_specs=(tok, tok, tok, rows, rows),
        scratch_shapes=[pltpu.VMEM((_hps(h), HEAD_DIM, HEAD_DIM), F32)],
        compiler_params=_params(("parallel", "arbitrary")),
    )(q, k, v, beta_bc, g_bc, states, invs, do)


def _hgrn_chunk(q, k, lf):
    c = q.shape[0]
    row = _iota((c, HEAD_DIM), 0)
    b = _cumsum_rows(lf)
    q_subs, k_facs, a_rows = [], [], []
    for x in range(c // SUB):
        b_start = jnp.sum(jnp.where(row < x * SUB, lf, 0.0), axis=0, keepdims=True)
        q_x = (q * jnp.exp(jnp.minimum(b - b_start, 0.0)))[x * SUB:(x + 1) * SUB]
        k_fac = jnp.where(row < (x + 1) * SUB, jnp.exp(jnp.minimum(b_start - b, EXP_CAP)), 0.0)
        q_subs.append(q_x)
        k_facs.append(k_fac)
        a_rows.append(_dot(q_x, k * k_fac, NT))
    a = jnp.concatenate(a_rows, axis=0)
    a = jnp.where(_iota((c, c), 0) >= _iota((c, c), 1), a, 0.0)
    b_last = jnp.sum(lf, axis=0, keepdims=True)
    return dict(b=b, a=a, q_subs=q_subs, k_facs=k_facs, e_b=jnp.exp(b),
                e_end=jnp.exp(b_last - b), e_last=jnp.exp(b_last))


def _hgrn_fwd(q, k, v, lf, v_blk=0):
    t = q.shape[0]
    h = q.shape[1] // HEAD_DIM
    nc = t // CHUNK

    def body(q_ref, k_ref, v_ref, lf_ref, o_ref, s_ref, state):
        qv, kv, vv = q_ref[...], k_ref[...], v_ref[...]
        ch = _hgrn_chunk(qv, kv, lf_ref[...])
        yield
        s0 = state[...]
        o_ref[...] = _dot(qv * ch["e_b"], s0, NT) + _dot(ch["a"], vv, NN)
        s_ref[...] = s0
        state[...] = s0 * ch["e_last"] + _dot(vv, kv * ch["e_end"], TN)

    tok = pl.BlockSpec((CHUNK, _hps(h) * HEAD_DIM), lambda hh, c: (c, hh))
    mat = pl.BlockSpec((_hps(h), None, HEAD_DIM, HEAD_DIM), lambda hh, c: (hh, c, 0, 0))
    return pl.pallas_call(
        _each_head(body, 4), name="hgrn_fwd",
        out_shape=(jax.ShapeDtypeStruct(q.shape, F32),
                   jax.ShapeDtypeStruct((h, nc, HEAD_DIM, HEAD_DIM), F32)),
        grid=(h // _hps(h), nc),
        in_specs=[tok, tok, pl.BlockSpec((CHUNK, _hps(h) * HEAD_DIM), lambda hh, c: (c, v_blk // _hps(h) + hh)), tok],
        out_specs=(tok, mat),
        scratch_shapes=[pltpu.VMEM((_hps(h), HEAD_DIM, HEAD_DIM), F32)],
        compiler_params=_params(("parallel", "arbitrary")),
    )(q, k, v, lf)


def _hgrn_bwd(q, k, v, lf, states, do, v_blk=0, do_blk=0):
    t = q.shape[0]
    nc = t // CHUNK
    h = q.shape[1] // HEAD_DIM

    def body(q_ref, k_ref, v_ref, lf_ref, s_ref, do_ref, dq_ref, dk_ref, dv_ref, dlf_ref, dstate):
        qv, kv, vv, dov, s0 = q_ref[...], k_ref[...], v_ref[...], do_ref[...], s_ref[...]
        ds1 = dstate[...]
        ch = _hgrn_chunk(qv, kv, lf_ref[...])
        yield
        c = CHUNK
        row = _iota((c, HEAD_DIM), 0)
        qh = qv * ch["e_b"]
        k_end = kv * ch["e_end"]
        da = jnp.where(_iota((c, c), 0) >= _iota((c, c), 1), _dot(dov, vv, NT), 0.0)
        dqh = _dot(dov, s0, NN)
        dk_end = _dot(vv, ds1, NN)
        yield
        end = dk_end * k_end
        dk = dk_end * ch["e_end"]
        db = dqh * qh - end + jnp.where(
            row == c - 1, jnp.sum(end + s0 * ch["e_last"] * ds1, axis=0, keepdims=True), 0.0)
        dq_rows, qdq_rows = [], []
        for x in range(c // SUB):
            da_x = da[x * SUB:(x + 1) * SUB]
            k_x = kv * ch["k_facs"][x]
            dq_x = _dot(da_x, k_x, NN)
            dk_x = _dot(da_x, ch["q_subs"][x], TN)
            dq_rows.append(dq_x)
            qdq_rows.append(dq_x * ch["q_subs"][x])
            dk = dk + dk_x * ch["k_facs"][x]
            kdk = dk_x * k_x
            db = db - kdk
            if x > 0:
                at_start = jnp.sum(kdk, axis=0, keepdims=True) - jnp.sum(qdq_rows[x], axis=0, keepdims=True)
                db = db + jnp.where(row == x * SUB - 1, at_start, 0.0)
        yield
        b_start = jnp.zeros((c, HEAD_DIM), F32)
        for x in range(1, c // SUB):
            b_x = jnp.sum(jnp.where(row < x * SUB, lf_ref[...], 0.0), axis=0, keepdims=True)
            b_start = jnp.where(row >= x * SUB, b_x, b_start)
        dq = dqh * ch["e_b"] + jnp.concatenate(dq_rows, axis=0) * jnp.exp(jnp.minimum(ch["b"] - b_start, 0.0))
        db = db + jnp.concatenate(qdq_rows, axis=0)
        dstate[...] = _dot(dov, qh, TN) + ds1 * ch["e_last"]
        dq_ref[...] = dq
        dk_ref[...] = dk
        dv_ref[...] = _dot(ch["a"], dov, TN) + _dot(k_end, ds1, NT)
        dlf_ref[...] = _rev_cumsum_rows(db)

    rev = lambda c: nc - 1 - c
    tok = pl.BlockSpec((CHUNK, _hps(h) * HEAD_DIM), lambda hh, c: (rev(c), hh))
    mat = pl.BlockSpec((_hps(h), None, HEAD_DIM, HEAD_DIM), lambda hh, c: (hh, rev(c), 0, 0))
    tok_shape = jax.ShapeDtypeStruct(q.shape, F32)
    return pl.pallas_call(
        _each_head(body, 6), name="hgrn_bwd",
        out_shape=(tok_shape, tok_shape, tok_shape, tok_shape),
        grid=(h // _hps(h), nc),
        in_specs=[tok, tok, pl.BlockSpec((CHUNK, _hps(h) * HEAD_DIM), lambda hh, c: (rev(c), v_blk // _hps(h) + hh)), tok, mat,
                  pl.BlockSpec((CHUNK, _hps(h) * HEAD_DIM), lambda hh, c: (rev(c), do_blk // _hps(h) + hh))],
        out_specs=(tok, tok, tok, tok),
        scratch_shapes=[pltpu.VMEM((_hps(h), HEAD_DIM, HEAD_DIM), F32)],
        compiler_params=_params(("parallel", "arbitrary")),
    )(q, k, v, lf, states, do)


CONV_ROWS = 256
HALO = 8


def _shift_down(cur, prev, s):
    rt = cur.shape[0]
    head = jnp.concatenate([pltpu.roll(prev, s, 0), jnp.zeros((rt - HALO, cur.shape[1]), F32)], axis=0)
    return jnp.where(_iota(cur.shape, 0) < s, head, pltpu.roll(cur, s, 0))


def _shift_up(cur, nxt, s):
    rt = cur.shape[0]
    tail = jnp.concatenate([jnp.zeros((rt - HALO, cur.shape[1]), F32), pltpu.roll(nxt, HALO - s, 0)], axis=0)
    return jnp.where(_iota(cur.shape, 0) >= rt - s, tail, pltpu.roll(cur, rt - s, 0))


def _tile_with_prev(ref, i, rt):
    r0 = pl.multiple_of(i * rt, rt)
    cur = ref[pl.ds(r0, rt), :]
    prev = ref[pl.ds(pl.multiple_of(jnp.maximum(r0 - HALO, 0), HALO), HALO), :]
    return cur, jnp.where(i > 0, prev, 0.0)


def _tile_with_next(ref, i, rt, n_tiles):
    r0 = pl.multiple_of(i * rt, rt)
    cur = ref[pl.ds(r0, rt), :]
    nxt = ref[pl.ds(pl.multiple_of(jnp.minimum(r0 + rt, (n_tiles - 1) * rt), HALO), HALO), :]
    return cur, jnp.where(i < n_tiles - 1, nxt, 0.0)


def _conv_tile(x_ref, w_ref, i, rt):
    cur, prev = _tile_with_prev(x_ref, i, rt)
    shifted = [_shift_down(cur, prev, CONV_W - 1 - j) for j in range(CONV_W - 1)] + [cur]
    c = shifted[0] * w_ref[pl.ds(0, 1), :]
    for j in range(1, CONV_W):
        c = c + shifted[j] * w_ref[pl.ds(j, 1), :]
    return c, shifted


def _l2n(s):
    return s * lax.rsqrt(jnp.sum(s * s, axis=-1, keepdims=True) + L2_EPS)


def _gdn_prep_fwd(proj, conv_w, h):
    t = proj.shape[0]
    rt = _tile(t, CONV_ROWS)
    nt = t // rt
    scale = HEAD_DIM ** -0.5

    def body(xq, xk, xv, wq, wk, wv, q_ref, k_ref, v_ref):
        def tile(i, carry):
            rows = pl.ds(pl.multiple_of(i * rt, rt), rt)
            q_ref[rows, :] = _l2n(_silu(_conv_tile(xq, wq, i, rt)[0])) * scale
            k_ref[rows, :] = _l2n(_silu(_conv_tile(xk, wk, i, rt)[0]))
            v_ref[rows, :] = _silu(_conv_tile(xv, wv, i, rt)[0])
            return carry

        lax.fori_loop(0, nt, tile, 0)

    col = lambda p: pl.BlockSpec((t, HEAD_DIM), lambda hh: (0, p * h + hh))
    wcol = lambda p: pl.BlockSpec((CONV_W, HEAD_DIM), lambda hh: (0, p * h + hh))
    out = pl.BlockSpec((t, HEAD_DIM), lambda hh: (0, hh))
    shape = jax.ShapeDtypeStruct((t, h * HEAD_DIM), F32)
    return pl.pallas_call(
        body, name="gdn_prep_fwd", out_shape=(shape, shape, shape), grid=(h,),
        in_specs=[col(0), col(1), col(2), wcol(0), wcol(1), wcol(2)], out_specs=(out, out, out),
        compiler_params=_params(("parallel",)),
    )(proj, proj, proj, conv_w, conv_w, conv_w)


def _gdn_prep_bwd(proj, conv_w, dq, dk, dv, h):
    t = proj.shape[0]
    rt = _tile(t, CONV_ROWS)
    nt = t // rt
    scale = HEAD_DIM ** -0.5

    def part(x_ref, w_ref, dy_ref, dx_ref, dw_ref, dc_ref, norm_scale):
        def first(i, dws):
            rows = pl.ds(pl.multiple_of(i * rt, rt), rt)
            c, shifted = _conv_tile(x_ref, w_ref, i, rt)
            ds = dy_ref[rows, :]
            s, ds_dc = _silu_and_grad(c)
            if norm_scale is not None:
                r = lax.rsqrt(jnp.sum(s * s, axis=-1, keepdims=True) + L2_EPS)
                y = s * r
                dyn = ds * norm_scale
                ds = r * (dyn - y * jnp.sum(dyn * y, axis=-1, keepdims=True))
            dc = ds * ds_dc
            dc_ref[rows, :] = dc
            return tuple(dws[j] + jnp.sum(dc * shifted[j], axis=0, keepdims=True) for j in range(CONV_W))

        dws = lax.fori_loop(0, nt, first, tuple(jnp.zeros((1, HEAD_DIM), F32) for _ in range(CONV_W)))
        for j in range(CONV_W):
            dw_ref[pl.ds(j, 1), :] = dws[j]

        def second(i, carry):
            rows = pl.ds(pl.multiple_of(i * rt, rt), rt)
            cur, nxt = _tile_with_next(dc_ref, i, rt, nt)
            dx = cur * w_ref[pl.ds(CONV_W - 1, 1), :]
            for j in range(CONV_W - 1):
                dx = dx + _shift_up(cur, nxt, CONV_W - 1 - j) * w_ref[pl.ds(j, 1), :]
            dx_ref[rows, :] = dx.astype(dx_ref.dtype)
            return carry

        lax.fori_loop(0, nt, second, 0)

    def body(xq, xk, xv, wq, wk, wv, dq_ref, dk_ref, dv_ref, dxq, dxk, dxv, dwq, dwk, dwv, dc_ref):
        part(xq, wq, dq_ref, dxq, dwq, dc_ref, scale)
        part(xk, wk, dk_ref, dxk, dwk, dc_ref, 1.0)
        part(xv, wv, dv_ref, dxv, dwv, dc_ref, None)

    col = lambda p: pl.BlockSpec((t, HEAD_DIM), lambda hh: (0, p * h + hh))
    wcol = lambda p: pl.BlockSpec((CONV_W, HEAD_DIM), lambda hh: (0, p * h + hh))
    own = pl.BlockSpec((t, HEAD_DIM), lambda hh: (0, hh))
    wown = pl.BlockSpec((CONV_W, HEAD_DIM), lambda hh: (0, hh))
    dx_shape = jax.ShapeDtypeStruct((t, h * HEAD_DIM), BF16)
    dw_shape = jax.ShapeDtypeStruct((CONV_W, h * HEAD_DIM), F32)
    return pl.pallas_call(
        body, name="gdn_prep_bwd",
        out_shape=(dx_shape, dx_shape, dx_shape, dw_shape, dw_shape, dw_shape), grid=(h,),
        in_specs=[col(0), col(1), col(2), wcol(0), wcol(1), wcol(2), own, own, own],
        out_specs=(own, own, own, wown, wown, wown),
        scratch_shapes=[pltpu.VMEM((t, HEAD_DIM), F32)],
        compiler_params=_params(("parallel",)),
    )(proj, proj, proj, conv_w, conv_w, conv_w, dq, dk, dv)


def _gdn_gates_fwd(ab, a_log_row, dt_bias_row):
    t = ab.shape[0]
    tr = _tile(t, 512)

    def body(ab_ref, al_ref, dt_ref, g_ref, b_ref):
        g_ref[...] = -jnp.exp(al_ref[...]) * _softplus(ab_ref[:, :HEAD_DIM] + dt_ref[...])
        b_ref[...] = _sigmoid(ab_ref[:, HEAD_DIM:])

    row = pl.BlockSpec((tr, HEAD_DIM), lambda i: (i, 0))
    vec = pl.BlockSpec((1, HEAD_DIM), lambda i: (0, 0))
    shape = jax.ShapeDtypeStruct((t, HEAD_DIM), F32)
    return pl.pallas_call(
        body, name="gdn_gates_fwd", out_shape=(shape, shape), grid=(t // tr,),
        in_specs=[pl.BlockSpec((tr, 2 * HEAD_DIM), lambda i: (i, 0)), vec, vec], out_specs=(row, row),
        compiler_params=_params(("parallel",)),
    )(ab, a_log_row, dt_bias_row)


def _gdn_gates_bwd(ab, a_log_row, dt_bias_row, dg, dbeta):
    t = ab.shape[0]
    tr = _tile(t, 512)

    def body(ab_ref, al_ref, dt_ref, dg_ref, db_ref, dab_ref, dal_ref, ddt_ref):
        @pl.when(pl.program_id(0) == 0)
        def _():
            dal_ref[...] = jnp.zeros_like(dal_ref)
            ddt_ref[...] = jnp.zeros_like(ddt_ref)

        xa = ab_ref[:, :HEAD_DIM] + dt_ref[...]
        neg_a = -jnp.exp(al_ref[...])
        dgv = dg_ref[...]
        da = dgv * neg_a * _sigmoid(xa)
        beta = _sigmoid(ab_ref[:, HEAD_DIM:])
        dab_ref[:, :HEAD_DIM] = da.astype(BF16)
        dab_ref[:, HEAD_DIM:] = (db_ref[...] * beta * (1.0 - beta)).astype(BF16)
        dal_ref[...] += jnp.sum(dgv * neg_a * _softplus(xa), axis=0, keepdims=True)
        ddt_ref[...] += jnp.sum(da, axis=0, keepdims=True)

    row = pl.BlockSpec((tr, HEAD_DIM), lambda i: (i, 0))
    row2 = pl.BlockSpec((tr, 2 * HEAD_DIM), lambda i: (i, 0))
    vec = pl.BlockSpec((1, HEAD_DIM), lambda i: (0, 0))
    vshape = jax.ShapeDtypeStruct((1, HEAD_DIM), F32)
    return pl.pallas_call(
        body, name="gdn_gates_bwd",
        out_shape=(jax.ShapeDtypeStruct((t, 2 * HEAD_DIM), BF16), vshape, vshape), grid=(t // tr,),
        in_specs=[row2, vec, vec, row, row], out_specs=(row2, vec, vec),
        compiler_params=_params(("arbitrary",)),
    )(ab, a_log_row, dt_bias_row, dg, dbeta)


def _lower_bound(lb_ref):
    return _sigmoid(lb_ref[pl.ds(0, 1), :] - lb_ref[pl.ds(1, 1), :])


def _hgrn_prep_fwd(proj, lb_logits, h, q_blk, f_blk):
    t = proj.shape[0]
    tr = _tile(t, 512)

    def body(xq, xf, lb_ref, q_ref, k_ref, lf_ref):
        lb = _lower_bound(lb_ref)
        s = _sigmoid(xf[...])
        q_ref[...] = _silu(xq[...])
        k_ref[...] = (1.0 - lb) * (1.0 - s)
        lf_ref[...] = jnp.log(lb + (1.0 - lb) * s)

    width = h * HEAD_DIM
    col = lambda b0: pl.BlockSpec((tr, width), lambda i: (i, b0 // h))
    own = pl.BlockSpec((tr, width), lambda i: (i, 0))
    shape = jax.ShapeDtypeStruct((t, width), F32)
    return pl.pallas_call(
        body, name="hgrn_prep_fwd", out_shape=(shape, shape, shape), grid=(t // tr,),
        in_specs=[col(q_blk), col(f_blk), pl.BlockSpec((2, width), lambda i: (0, 0))],
        out_specs=(own, own, own), compiler_params=_params(("parallel",)),
    )(proj, proj, lb_logits)


def _hgrn_prep_bwd(proj, lb_logits, dq, dk, dlf, h, q_blk, f_blk):
    t = proj.shape[0]
    tr = _tile(t, 512)

    def body(xq, xf, lb_ref, dq_ref, dk_ref, dlf_ref, dxq, dxf, dlb_ref):
        @pl.when(pl.program_id(0) == 0)
        def _():
            dlb_ref[...] = jnp.zeros_like(dlb_ref)

        lb = _lower_bound(lb_ref)
        s = _sigmoid(xf[...])
        e = dlf_ref[...] / (lb + (1.0 - lb) * s) - dk_ref[...]
        dxq[...] = (dq_ref[...] * _dsilu(xq[...])).astype(BF16)
        dxf[...] = (s * (1.0 - s) * (1.0 - lb) * e).astype(BF16)
        d0 = jnp.sum((1.0 - s) * e, axis=0, keepdims=True) * (lb * (1.0 - lb))
        dlb_ref[pl.ds(0, 1), :] += d0
        dlb_ref[pl.ds(1, 1), :] += -d0

    width = h * HEAD_DIM
    col = lambda b0: pl.BlockSpec((tr, width), lambda i: (i, b0 // h))
    own = pl.BlockSpec((tr, width), lambda i: (i, 0))
    lbs = pl.BlockSpec((2, width), lambda i: (0, 0))
    shape = jax.ShapeDtypeStruct((t, width), BF16)
    return pl.pallas_call(
        body, name="hgrn_prep_bwd",
        out_shape=(shape, shape, jax.ShapeDtypeStruct((2, width), F32)), grid=(t // tr,),
        in_specs=[col(q_blk), col(f_blk), lbs, own, own, own], out_specs=(own, own, lbs),
        compiler_params=_params(("arbitrary",)),
    )(proj, proj, lb_logits, dq, dk, dlf)


GATE_HEADS = 8


def _gate_specs(h, z_blk, g_blk, tr):
    g = min(GATE_HEADS, h)
    n = h // g
    width = g * HEAD_DIM
    o_a = pl.BlockSpec((tr, width), lambda gg, i: (i, jnp.minimum(gg, n - 1)))
    o_b = pl.BlockSpec((tr, width), lambda gg, i: (i, jnp.maximum(gg - n, 0)))
    gate = pl.BlockSpec((tr, width), lambda gg, i: (i, jnp.where(gg < n, z_blk // g + gg, g_blk // g + gg - n)))
    w = pl.BlockSpec((None, 1, HEAD_DIM), lambda gg, i: (gg // n, 0, 0))
    cat = pl.BlockSpec((tr, width), lambda gg, i: (i, gg))
    return (o_a, o_b, gate, w, cat), g, n


def _silu_and_grad(x):
    s = _sigmoid(x)
    return x * s, s * (1.0 + x * (1.0 - s))


def _gate_fwd(o_a, o_b, proj, norm_w, h, z_blk, g_blk):
    t = o_a.shape[0]
    tr = _tile(t, 512)

    (sa, sb, sg, sw, cat), g, n = _gate_specs(h, z_blk, g_blk, tr)

    def body(oa_ref, ob_ref, z_ref, w_ref, y_ref):
        for k in range(g):
            lanes = pl.ds(k * HEAD_DIM, HEAD_DIM)
            o = jnp.where(pl.program_id(0) < n, oa_ref[:, lanes], ob_ref[:, lanes])
            r = lax.rsqrt(jnp.mean(o * o, axis=-1, keepdims=True) + NORM_EPS)
            y_ref[:, lanes] = (o * r * w_ref[...] * _silu(z_ref[:, lanes])).astype(y_ref.dtype)

    return pl.pallas_call(
        body, name="gate_fwd", out_shape=jax.ShapeDtypeStruct((t, 2 * h * HEAD_DIM), BF16),
        grid=(2 * n, t // tr), in_specs=[sa, sb, sg, sw], out_specs=cat,
        compiler_params=_params(("parallel", "parallel")),
    )(o_a, o_b, proj, norm_w)


def _gate_bwd(o_a, o_b, proj, norm_w, dy, h, z_blk, g_blk):
    t = o_a.shape[0]
    tr = _tile(t, 512)

    (sa, sb, sg, sw, cat), g, n = _gate_specs(h, z_blk, g_blk, tr)

    def body(oa_ref, ob_ref, z_ref, w_ref, dy_ref, do_ref, dz_ref, dw_ref):
        gg = pl.program_id(0)

        @pl.when(jnp.logical_and(gg % n == 0, pl.program_id(1) == 0))
        def _():
            dw_ref[...] = jnp.zeros_like(dw_ref)

        w = w_ref[...]
        dw = jnp.zeros_like(w)
        for k in range(g):
            lanes = pl.ds(k * HEAD_DIM, HEAD_DIM)
            o = jnp.where(gg < n, oa_ref[:, lanes], ob_ref[:, lanes])
            dyv = dy_ref[:, lanes]
            r = lax.rsqrt(jnp.mean(o * o, axis=-1, keepdims=True) + NORM_EPS)
            oh = o * r
            act, dact = _silu_and_grad(z_ref[:, lanes])
            dz_ref[:, lanes] = (dyv * oh * w * dact).astype(dz_ref.dtype)
            dn = dyv * act
            doh = dn * w
            do_ref[:, lanes] = r * (doh - oh * jnp.mean(doh * oh, axis=-1, keepdims=True))
            dw = dw + jnp.sum(dn * oh, axis=0, keepdims=True)
        dw_ref[...] += dw

    width = 2 * h * HEAD_DIM
    return pl.pallas_call(
        body, name="gate_bwd",
        out_shape=(jax.ShapeDtypeStruct((t, width), F32), jax.ShapeDtypeStruct((t, width), BF16),
                   jax.ShapeDtypeStruct((2, 1, HEAD_DIM), F32)),
        grid=(2 * n, t // tr), in_specs=[sa, sb, sg, sw, cat], out_specs=(cat, cat, sw),
        compiler_params=_params(("arbitrary", "arbitrary")),
    )(o_a, o_b, proj, norm_w, dy)


def _lane_row(vec):
    return jnp.pad(vec.reshape(1, -1), ((0, 0), (0, HEAD_DIM - vec.shape[-1])))


def _add_epi(acc, res):
    return (acc + res,)


def _split_w_in(w_in, h):
    gw = h * HEAD_DIM
    main = jnp.concatenate([w_in[:, :4 * gw], w_in[:, 4 * gw + 2 * h:]], axis=1)
    pad = jnp.zeros((w_in.shape[0], HEAD_DIM - h), w_in.dtype)
    ab = jnp.concatenate([w_in[:, 4 * gw:4 * gw + h], pad, w_in[:, 4 * gw + h:4 * gw + 2 * h], pad], axis=1)
    return main, ab


def _merge_w_in(main, ab, h):
    gw = h * HEAD_DIM
    return jnp.concatenate([main[:, :4 * gw], ab[:, :h], ab[:, HEAD_DIM:HEAD_DIM + h], main[:, 4 * gw:]], axis=1)


def _local_step(x, target, w_main, w_ab, conv_w, a_log, dt_bias, gdn_norm_w, lb_logits, hgrn_norm_w,
                w_out, norm_mix_w, norm_ffn_w, w_ff1, w_ff2, norm_final_w, reducer=None, n1=None):
    t, d = x.shape
    h = d // (2 * HEAD_DIM)
    gw = h * HEAD_DIM
    k_blk, v_blk, z_blk, qb_blk, fb_blk, ib_blk, gb_blk = (i * h for i in range(1, 8))
    del k_blk, v_blk

    if n1 is None:
        n1 = _rms_fwd(x, norm_mix_w, "rms_mix")
    stacked = not hasattr(w_main, "ndim") or w_main.ndim == 3
    if stacked and not hasattr(w_main, "ndim"):
        early, late_parts, edges, chip = w_main
        proj = _mm_quarters(n1, early, jnp.stack([chip, chip ^ 2, chip ^ 1]), "in_proj_early")
        late, _ = _in_proj_quarters(lax.optimization_barrier((late_parts, proj))[0], edges, h)
        proj = _mm_quarters(n1, late, jnp.stack([chip ^ 3]), "in_proj_late", into=proj)
        w_main = lax.dynamic_update_index_in_dim(
            early, lax.dynamic_index_in_dim(late, chip ^ 3, 0, keepdims=False), chip ^ 3, 0)
    else:
        proj = _mm(n1, w_main, "nn", (F32,), "in_proj", b_stacked=stacked)
    ab = _mm(n1, w_ab, "nn", (F32,), "in_proj_ab")

    q, k, v = _gdn_prep_fwd(proj, conv_w, h)
    a_log_row, dt_row = _lane_row(a_log), _lane_row(dt_bias)
    g_tm, beta_tm = _gdn_gates_fwd(ab, a_log_row, dt_row)
    to_heads = lambda a: jnp.broadcast_to(a[:, :h].T[:, :, None], (h, t, HEAD_DIM))
    g_bc, beta_bc = to_heads(g_tm), to_heads(beta_tm)
    o_a, st_a, inv_a = _gdn_fwd(q, k, v, beta_bc, g_bc)

    qh, kh, lf = _hgrn_prep_fwd(proj, lb_logits, h, qb_blk, fb_blk)
    o_b, st_b = _hgrn_fwd(qh, kh, proj, lf, v_blk=ib_blk)

    gate_w = jnp.stack([gdn_norm_w.reshape(1, HEAD_DIM), hgrn_norm_w.reshape(1, HEAD_DIM)])
    y = _gate_fwd(o_a, o_b, proj, gate_w, h, z_blk, gb_blk)
    if callable(w_out):
        w_out, w_ff1, w_ff2 = w_out(y)
    h1 = _mm(y, w_out, "nn", (F32,), "out_proj", epi=_add_epi, extras=(x,))
    n2 = _rms_fwd(h1, norm_ffn_w, "rms_ffn")
    act, r = _mm(n2, w_ff1, "nn", (F32, BF16), "ff1", b_stacked=True,
                 epi=lambda acc: (acc, jnp.square(jnp.maximum(acc, 0.0))))
    h2 = _mm(r, w_ff2, "nn", (F32,), "ff2", epi=_add_epi, extras=(h1,))
    loss, dh2, dh2_b, d_norm_final = _loss_head(h2, norm_final_w, target)

    da = _mm(dh2_b, w_ff2, "nt", (BF16,), "ff2_dx",
             epi=lambda acc, a: (acc * (2.0 * jnp.maximum(a, 0.0)),), extras=(act,))
    pending = []

    def step(anchor, name=None, full=None):
        if reducer is not None:
            pending.extend(reducer.step(name, full, anchor))

    def after_step(value):
        if not pending:
            return value
        value = lax.optimization_barrier((value, *pending))[0]
        pending.clear()
        return value

    d_ff2 = _mm(r, dh2_b, "tn", (F32,), "ff2_dw")
    step(None, "w_ff2", d_ff2)
    dn2 = _mm(da, w_ff1, "nt", (F32,), "ff1_dx", b_stacked=True)
    d_ff1 = _mm(n2, da, "tn", (F32,), "ff1_dw", out_stacked=True)
    step(d_ff1, "w_ff1", d_ff1)
    dh1, dh1_b, d_norm_ffn = _rms_bwd(after_step(dn2), h1, norm_ffn_w, dh2, "rms_ffn_bwd")
    dy = _mm(dh1_b, w_out, "nt", (F32,), "out_proj_dx")
    d_out = _mm(y, dh1_b, "tn", (F32,), "out_proj_dw")
    step(d_out, "w_out", d_out)

    do, dgate, d_gate_w = _gate_bwd(o_a, o_b, proj, gate_w, after_step(dy), h, z_blk, gb_blk)
    step(do)
    dq, dk, dv, dbeta_bc, dg_bc = _gdn_bwd(q, k, v, beta_bc, g_bc, st_a, inv_a, after_step(do), do_blk=0)
    step(dq)
    dxq, dxk, dxv, dcq, dck, dcv = _gdn_prep_bwd(proj, conv_w, after_step(dq), dk, dv, h)
    step(dxq)
    from_heads = lambda a: jnp.pad(a[:, :, 0, :].reshape(h, t).T, ((0, 0), (0, HEAD_DIM - h)))
    dab, d_a_log, d_dt_bias = _gdn_gates_bwd(ab, a_log_row, dt_row, from_heads(dg_bc), from_heads(dbeta_bc))
    dqh, dkh, dvh, dlf = _hgrn_bwd(after_step(qh), kh, proj, lf, st_b, do, v_blk=ib_blk, do_blk=h)
    step(dqh)
    dxqb, dxfb, d_lb = _hgrn_prep_bwd(proj, lb_logits, dqh, dkh, dlf, h, qb_blk, fb_blk)

    dproj = jnp.concatenate([after_step(dxq), dxk, dxv, dgate[:, :gw], dxqb, dxfb, dvh.astype(BF16), dgate[:, gw:]],
                            axis=1)
    d_main = _mm(n1, dproj, "tn", (F32,), "in_proj_dw", out_stacked=True)
    d_ab = _mm(n1, dab, "tn", (F32,), "in_proj_ab_dw")
    step(d_main, "w_in", d_main)
    dn1_ab = _mm(after_step(dab), w_ab, "nt", (F32,), "in_proj_ab_dx")
    step(dn1_ab)
    dn1 = _mm(after_step(dproj), w_main, "nt", (F32,), "in_proj_dx", epi=_add_epi, extras=(dn1_ab,),
              b_stacked=stacked)
    step(dn1)
    dx, _, d_norm_mix = _rms_bwd(after_step(dn1), x, norm_mix_w, dh1, "rms_mix_bwd")
    step(dx)

    grads = dict(
        w_main=d_main, w_ab=d_ab, conv_w=jnp.concatenate([dcq, dck, dcv], axis=1),
        gdn_a_log=d_a_log[:, :h], gdn_dt_bias=d_dt_bias[:, :h], gdn_norm_w=d_gate_w[0],
        hgrn_lb_logits=d_lb, hgrn_norm_w=d_gate_w[1], w_out=d_out, norm_mix_w=d_norm_mix,
        norm_ffn_w=d_norm_ffn, w_ff1=d_ff1, w_ff2=d_ff2, norm_final_w=d_norm_final)
    return loss, dx, grads


N_CHIPS = 4
ANY = pl.BlockSpec(memory_space=pl.ANY)


def _place():
    x, y, c = lax.axis_index("x"), lax.axis_index("y"), lax.axis_index("c")
    chips = [(1 - x, y), (x, 1 - y), (1 - x, 1 - y)]
    return x, y, c, chips


def _remote(src, dst, send_sems, recv_sems, k, to):
    return pltpu.make_async_remote_copy(src_ref=src, dst_ref=dst, send_sem=send_sems.at[k],
                                        recv_sem=recv_sems.at[k], device_id=to, device_id_type=MESH)


def _to_sibling(x, y, c, chips):
    return [(x, y, 1 - c)]


def _to_same_core_of_chips(x, y, c, chips):
    return [(*chip, c) for chip in chips]


def _to_all_gather_peers(x, y, c, chips):
    return _to_sibling(x, y, c, chips) + _to_same_core_of_chips(x, y, c, chips)


SIBLING_EXCHANGE = (1, _to_sibling)
CHIP_EXCHANGE = (2, _to_same_core_of_chips)
GATHER_EXCHANGE = (3, _to_all_gather_peers)
DIAGONAL_EXCHANGE = (4, lambda x, y, c, chips: [(x, y, 1 - c), (*chips[2], c)])


def _launch(body, name, out_shapes, arrays, sem_counts, sequencer=None, after=()):
    n, n_after = len(arrays), len(after)
    sems = [pltpu.SemaphoreType.DMA((k,)) for k in sem_counts]
    strip = lambda refs: refs[:n] + refs[n + n_after:]
    if sequencer is None:
        return pl.pallas_call(
            lambda *refs: body(*strip(refs)), name=name, out_shape=tuple(out_shapes),
            in_specs=[ANY] * (n + n_after), out_specs=tuple(ANY for _ in out_shapes), scratch_shapes=sems,
        )(*arrays, *after)
    collective_id, peers = sequencer

    def sequencer_body(*refs):
        x, y, c, chips = _place()
        barrier = pltpu.get_barrier_semaphore()
        targets = peers(x, y, c, chips)
        for target in targets:
            pl.semaphore_signal(barrier, inc=1, device_id=target, device_id_type=MESH)
        pl.semaphore_wait(barrier, len(targets))
        body(*strip(refs))

    return pl.kernel(
        sequencer_body, name=name, out_type=tuple(out_shapes),
        mesh=plsc.ScalarSubcoreMesh(axis_name="sequencer", num_cores=1), scratch_types=tuple(sems),
        compiler_params=pltpu.CompilerParams(collective_id=collective_id),
    )(*arrays, *after)


def _gather_weights(big, small, name, sequencer=None, after=(), relations=(0, 1, 2)):
    nb, ns = len(big), len(small)
    n_sem = 6 * nb + 3 * ns

    def body(*refs):
        ins, outs = refs[:nb + ns], refs[nb + ns:2 * (nb + ns)]
        send_sems, recv_sems = refs[2 * (nb + ns):]
        x, y, c, chips = _place()
        me, sibling = 2 * x + y, (x, y, 1 - c)

        def half(a, chip, hc):
            rh = big[a].shape[0] // 2
            return outs[a].at[2 * chip[0] + chip[1], pl.ds(hc * rh, rh), :]

        first, passed = [], []
        for a in range(nb):
            rh = big[a].shape[0] // 2
            for j, chip in [(j, chips[j]) for j in relations]:
                first.append(_remote(ins[a].at[pl.ds(c * rh, rh), :], half(a, (x, y), c),
                                     send_sems, recv_sems, 6 * a + j, (*chip, c)))
        for s in range(ns):
            for j, chip in enumerate(chips):
                first.append(_remote(ins[nb + s], outs[nb + s].at[me], send_sems, recv_sems,
                                     6 * nb + 3 * s + j, (*chip, c)))
        for cp in first:
            cp.start()
        for a in range(nb):
            for j, chip in [(j, chips[j]) for j in relations]:
                _remote(half(a, chip, c), half(a, chip, c), send_sems, recv_sems, 6 * a + j, (*chip, c)).wait_recv()
                fwd = _remote(half(a, chip, c), half(a, chip, c), send_sems, recv_sems, 6 * a + 3 + j, sibling)
                fwd.start()
                passed.append(fwd)
        for s in range(ns):
            for j, chip in enumerate(chips):
                dst = outs[nb + s].at[2 * chip[0] + chip[1]]
                _remote(dst, dst, send_sems, recv_sems, 6 * nb + 3 * s + j, (*chip, c)).wait_recv()
        for a in range(nb):
            for j, chip in [(j, chips[j]) for j in relations]:
                _remote(half(a, chip, 1 - c), half(a, chip, 1 - c), send_sems, recv_sems,
                        6 * a + 3 + j, sibling).wait_recv()
        for cp in first + passed:
            cp.wait_send()

    arrays = list(big) + list(small)
    out_shapes = [jax.ShapeDtypeStruct((N_CHIPS,) + a.shape, a.dtype) for a in arrays]
    return _launch(body, name, out_shapes, arrays, (n_sem, n_sem), sequencer, after)


def _swap_halves(parts, name, sequencer=None):
    n = len(parts)

    def body(*refs):
        ins, outs = refs[:n], refs[n:2 * n]
        send_sems, recv_sems = refs[2 * n:]
        x, y, c, _ = _place()
        copies = [_remote(ins[a].at[s, 1 - c], outs[a].at[s], send_sems, recv_sems, N_CHIPS * a + s, (x, y, 1 - c))
                  for a in range(n) for s in range(N_CHIPS)]
        for cp in copies:
            cp.start()
        for cp in copies:
            cp.wait()

    out_shapes = [jax.ShapeDtypeStruct((N_CHIPS,) + p.shape[2:], p.dtype) for p in parts]
    return _launch(body, name, out_shapes, parts, (N_CHIPS * n, N_CHIPS * n), sequencer)


def _scatter_to_owners(parts, name, sequencer=None):
    n = len(parts)

    def body(*refs):
        ins, outs = refs[:n], refs[n:2 * n]
        send_sems, recv_sems = refs[2 * n:]
        x, y, c, chips = _place()
        copies = [_remote(ins[a].at[2 * chip[0] + chip[1]], outs[a].at[j], send_sems, recv_sems,
                          3 * a + j, (*chip, c))
                  for a in range(n) for j, chip in enumerate(chips)]
        for cp in copies:
            cp.start()
        for cp in copies:
            cp.wait()

    out_shapes = [jax.ShapeDtypeStruct((3,) + p.shape[1:], p.dtype) for p in parts]
    return _launch(body, name, out_shapes, parts, (3 * n, 3 * n), sequencer)


def _send_to_sibling(halves, name, sequencer=None):
    n = len(halves)

    def body(*refs):
        ins, outs = refs[:n], refs[n:2 * n]
        send_sems, recv_sems = refs[2 * n:]
        x, y, c, _ = _place()
        copies = [_remote(ins[a], outs[a], send_sems, recv_sems, a, (x, y, 1 - c)) for a in range(n)]
        for cp in copies:
            cp.start()
        for cp in copies:
            cp.wait()

    out_shapes = [jax.ShapeDtypeStruct(p.shape, p.dtype) for p in halves]
    return _launch(body, name, out_shapes, halves, (n, n), sequencer)


N_DEV = 8


def _all_reduce_small(vec):
    def body(v_ref, gathered, total, send_sems, recv_sems):
        x, y, c, _ = _place()
        me = 4 * x + 2 * y + c
        gathered[me] = v_ref[...]
        copies = []
        for k in range(1, N_DEV):
            px = 1 - x if k & 4 else x
            py = 1 - y if k & 2 else y
            pc = 1 - c if k & 1 else c
            copies.append(_remote(v_ref, gathered.at[me], send_sems, recv_sems, k - 1, (px, py, pc)))
        for cp in copies:
            cp.start()
        for k, cp in enumerate(copies):
            cp.wait_send()
        for k in range(1, N_DEV):
            px = 1 - x if k & 4 else x
            py = 1 - y if k & 2 else y
            pc = 1 - c if k & 1 else c
            src = gathered.at[4 * px + 2 * py + pc]
            _remote(src, src, send_sems, recv_sems, k - 1, (px, py, pc)).wait_recv()
        acc = gathered[0]
        for dev in range(1, N_DEV):
            acc = acc + gathered[dev]
        total[...] = acc

    vm = pl.BlockSpec(memory_space=pltpu.VMEM)
    return pl.pallas_call(
        body, name="all_reduce_small",
        out_shape=(jax.ShapeDtypeStruct((N_DEV,) + vec.shape, F32), jax.ShapeDtypeStruct(vec.shape, F32)),
        in_specs=[vm], out_specs=(vm, vm),
        scratch_shapes=[pltpu.SemaphoreType.DMA((N_DEV - 1,)), pltpu.SemaphoreType.DMA((N_DEV - 1,))],
    )(vec)[1]


def _chip_sum(part, recv, c, chip):
    _, _, rh, cols = part.shape
    tr = _tile(rh, 2 * ROWS)

    def body(c_ref, chip_ref, p_ref, r_ref, own_ref, sb_ref):
        s = p_ref[...] + r_ref[...]
        sb_ref[...] = s.astype(BF16)

        @pl.when(pl.program_id(1) == chip_ref[0])
        def _():
            own_ref[...] = s

    blk = pl.BlockSpec((None, tr, cols), lambda i, s, c_ref, chip_ref: (s, i, 0))
    return pl.pallas_call(
        body, name="grad_chip_sum",
        out_shape=(jax.ShapeDtypeStruct(recv.shape[1:], F32), jax.ShapeDtypeStruct(recv.shape, BF16)),
        grid_spec=pltpu.PrefetchScalarGridSpec(
            num_scalar_prefetch=2, grid=(rh // tr, N_CHIPS),
            in_specs=[pl.BlockSpec((None, None, tr, cols), lambda i, s, c_ref, chip_ref: (s, c_ref[0], i, 0)), blk],
            out_specs=(pl.BlockSpec((tr, cols), lambda i, s, c_ref, chip_ref: (i, 0)), blk)),
        compiler_params=_params(("parallel", "arbitrary")),
    )(c, chip, part, recv)


def _owner_sum(own, recv):
    rh, cols = own.shape
    tr = _tile(rh, 2 * ROWS)

    def body(o_ref, r0, r1, r2, g_ref):
        g_ref[...] = ((o_ref[...] + r0[...].astype(F32)) + r1[...].astype(F32)) + r2[...].astype(F32)

    slot = lambda j: pl.BlockSpec((None, tr, cols), lambda i: (j, i, 0))
    row = pl.BlockSpec((tr, cols), lambda i: (i, 0))
    return pl.pallas_call(
        body, name="grad_owner_sum", out_shape=jax.ShapeDtypeStruct((rh, cols), F32), grid=(rh // tr,),
        in_specs=[row, slot(0), slot(1), slot(2)], out_specs=row,
        compiler_params=_params(("parallel",)),
    )(own, recv, recv, recv)


def _adamw_math(w, g, m, v):
    c1 = 1.0 / (1.0 - ADAM_B1 ** ADAM_STEP)
    c2 = 1.0 / (1.0 - ADAM_B2 ** ADAM_STEP)
    nm = ADAM_B1 * m + (1.0 - ADAM_B1) * g
    nv = ADAM_B2 * v + (1.0 - ADAM_B2) * (g * g)
    return -ADAM_LR * ((nm * c1) / (jnp.sqrt(nv * c2) + ADAM_EPS) + ADAM_WD * w), nm, nv


def _adamw_unit_rows(w, g, m, v, name):
    rows, _, cols = w.shape
    tr = max(d for d in range(1, 33) if rows % d == 0)

    def body(w_ref, g_ref, m_ref, v_ref, d_ref, nm_ref, nv_ref):
        d_ref[...], nm_ref[...], nv_ref[...] = _adamw_math(w_ref[...], g_ref[...], m_ref[...], v_ref[...])

    blk = pl.BlockSpec((tr, 1, cols), lambda i: (i, 0, 0))
    shape = jax.ShapeDtypeStruct(w.shape, F32)
    return pl.pallas_call(
        body, name=name, out_shape=(shape, shape, shape), grid=(rows // tr,),
        in_specs=[blk, blk, blk, blk], out_specs=(blk, blk, blk),
        compiler_params=_params(("parallel",)),
    )(w, g, m, v)


def _divisor_tile(n, want):
    return max(d for d in range(ROW_TILE, want + 1, ROW_TILE) if n % d == 0)


def _adamw(w, g, m, v, name):
    if w.ndim == 3:
        return _adamw_unit_rows(w, g, m, v, name)
    rows, cols = w.shape
    tr = _divisor_tile(rows, 2048) if rows % 8 == 0 else rows
    c1 = 1.0 / (1.0 - ADAM_B1 ** ADAM_STEP)
    c2 = 1.0 / (1.0 - ADAM_B2 ** ADAM_STEP)

    def body(w_ref, g_ref, m_ref, v_ref, d_ref, nm_ref, nv_ref):
        gv = g_ref[...]
        nm = ADAM_B1 * m_ref[...] + (1.0 - ADAM_B1) * gv
        nv = ADAM_B2 * v_ref[...] + (1.0 - ADAM_B2) * (gv * gv)
        d_ref[...] = -ADAM_LR * ((nm * c1) / (jnp.sqrt(nv * c2) + ADAM_EPS) + ADAM_WD * w_ref[...])
        nm_ref[...] = nm
        nv_ref[...] = nv

    blk = pl.BlockSpec((tr, cols), lambda i: (i, 0))
    shape = jax.ShapeDtypeStruct((rows, cols), F32)
    return pl.pallas_call(
        body, name=name, out_shape=(shape, shape, shape), grid=(rows // tr,),
        in_specs=[blk, blk, blk, blk], out_specs=(blk, blk, blk),
        compiler_params=_params(("parallel",)),
    )(w, g, m, v)


def _adamw_halves(w, g_own, g_sib, m, v, c, name):
    _, rows, cols = w.shape
    rh = rows // 2
    tr = _tile(rh, 256)
    per = rh // tr
    c1 = 1.0 / (1.0 - ADAM_B1 ** ADAM_STEP)
    c2 = 1.0 / (1.0 - ADAM_B2 ** ADAM_STEP)

    def body(c_ref, w_ref, go_ref, gs_ref, m_ref, v_ref, g_ref, d_ref, nm_ref, nv_ref):
        own = pl.program_id(0) // per == c_ref[0]
        gv = jnp.where(own, go_ref[...], gs_ref[...])
        nm = ADAM_B1 * m_ref[...] + (1.0 - ADAM_B1) * gv
        nv = ADAM_B2 * v_ref[...] + (1.0 - ADAM_B2) * (gv * gv)
        g_ref[...] = gv
        d_ref[...] = -ADAM_LR * ((nm * c1) / (jnp.sqrt(nv * c2) + ADAM_EPS) + ADAM_WD * w_ref[...])
        nm_ref[...] = nm
        nv_ref[...] = nv

    blk = pl.BlockSpec((None, tr, cols), lambda i, c_ref: (0, i, 0))
    half = pl.BlockSpec((tr, cols), lambda i, c_ref: (i % per, 0))
    shape = jax.ShapeDtypeStruct((1, rows, cols), F32)
    return pl.pallas_call(
        body, name=name, out_shape=(shape, shape, shape, shape),
        grid_spec=pltpu.PrefetchScalarGridSpec(
            num_scalar_prefetch=1, grid=(rows // tr,),
            in_specs=[blk, half, half, blk, blk], out_specs=(blk, blk, blk, blk)),
        compiler_params=_params(("parallel",)),
    )(c, w, g_own, g_sib, m, v)


def _by_shard(name, full):
    if name in ("w_in", "w_ff1"):
        st = full
    else:
        st = full.reshape(N_CHIPS, -1, full.shape[1])
    return st.reshape(N_CHIPS, 2, st.shape[1] // 2, st.shape[2])


class _GradReducer:
    def __init__(self, w, m, v, my_c, my_chip):
        self.w, self.m, self.v, self.my_c, self.my_chip = w, m, v, my_c, my_chip
        self.in_flight = []
        self.computed = []
        self.anchor = None
        self.done = {}

    def step(self, name=None, full=None, anchor=None):
        stages, self.in_flight, self.computed, self.anchor = self.in_flight, [], [], anchor
        for stage in [s for s in stages if not getattr(s, "long", False)]:
            self._advance(stage)
        if name is not None:
            self.in_flight.append(self._swap(name, _by_shard(name, full)))
        for stage in [s for s in stages if getattr(s, "long", False)]:
            self._advance(stage)
        return self.computed

    def _held(self, value):
        if self.anchor is None:
            return value
        return lax.optimization_barrier((value, self.anchor))[0]

    def _advance(self, stage):
        nxt = stage()
        if nxt is not None:
            self.in_flight.append(nxt)

    def finish(self):
        while self.in_flight:
            self.step()
        return self.done

    def _swap(self, name, part):
        got, = _swap_halves([part], "grad_swap_" + name, SIBLING_EXCHANGE)

        def scatter():
            total, total_bf16 = _chip_sum(part, self._held(got), self.my_c, self.my_chip.reshape(1))
            self.computed.append(total_bf16)
            recv, = _scatter_to_owners([total_bf16], "grad_scatter_" + name, CHIP_EXCHANGE)

            def send():
                half = _owner_sum(total, self._held(recv))
                self.computed.append(half)
                sib, = _send_to_sibling([half], "grad_send_" + name, SIBLING_EXCHANGE)

                def update():
                    self.done[name] = _adamw_halves(self.w[name], half, self._held(sib), self.m[name], self.v[name],
                                                    self.my_c, "adamw_" + name)
                    self.computed.append(self.done[name][0])

                if name == "w_in":
                    self.in_proj_halves = (half, sib)
                    return None
                return update
            return lambda: send
        scatter.long = True
        return scatter


def _adamw_minor_rows(w, g, m, v, name):
    _, rows, cols = w.shape
    turned = lambda a: jnp.transpose(a, (2, 0, 1))
    back = lambda a: jnp.transpose(a, (1, 2, 0))
    g = g.T.reshape(cols, 1, rows)
    delta, new_m, new_v = _adamw(turned(w), g, turned(m), turned(v), name)
    return back(g), back(delta), back(new_m), back(new_v)


def _in_proj_quarter_part(shard, chip, h):
    e = h // 2
    qw = shard.shape[1] - e
    zeros = jnp.zeros((shard.shape[0], h), shard.dtype)
    padded = jnp.concatenate([zeros, shard, zeros], axis=1)
    at_quarter = jnp.where(chip == 0, h, jnp.where(chip == 1, h - e, jnp.where(chip == 2, 2 * h, h + e)))
    at_edge = jnp.where(chip == 0, h + qw, jnp.where(chip == 1, h + qw - e, jnp.where(chip == 2, h, h - e)))
    return (lax.dynamic_slice_in_dim(padded, at_quarter, qw, axis=1).astype(BF16),
            lax.dynamic_slice_in_dim(padded, at_edge, h, axis=1))


def _in_proj_quarters(parts, edges, h):
    e = h // 2
    x1, a, b, x2 = edges[0][:, :e], edges[1], edges[2], edges[3][:, e:]
    parts = parts.at[1, :, :e].set(x1.astype(BF16))
    parts = parts.at[2, :, parts.shape[2] - e:].set(x2.astype(BF16))
    pad = jnp.zeros((a.shape[0], HEAD_DIM - h), a.dtype)
    return parts, jnp.concatenate([a, pad, b, pad], axis=1).astype(BF16)


def _in_proj_edge_columns(d_main, d_ab, h):
    e = h // 2
    return jnp.concatenate([d_main[1][:, :e], d_main[2][:, -e:], d_ab[:, :h], d_ab[:, HEAD_DIM:HEAD_DIM + h]], axis=1)


def _in_proj_shard_grad(q_own, q_sib, edges, c, chip, h):
    e = h // 2
    lower, upper = jnp.where(c[0] == 0, q_own, q_sib), jnp.where(c[0] == 0, q_sib, q_own)
    quarter = jnp.concatenate([lower, upper], axis=0)
    x1, x2, a, b = edges[:, :e], edges[:, e:2 * e], edges[:, 2 * e:2 * e + h], edges[:, 2 * e + h:]
    zeros = jnp.zeros_like(x1)
    left = jnp.where(chip == 2, b, jnp.concatenate([zeros, x2], axis=1))
    right = jnp.where(chip == 1, a, jnp.concatenate([x1, zeros], axis=1))
    start = jnp.where(chip == 0, h, jnp.where(chip == 1, h + e, jnp.where(chip == 2, 0, h - e)))
    padded = jnp.concatenate([left, quarter, right], axis=1)
    return lax.dynamic_slice_in_dim(padded, start, quarter.shape[1] + e, axis=1)


SMALL = ("gdn_a_log", "gdn_dt_bias", "gdn_norm_w", "hgrn_lb_logits", "hgrn_norm_w",
         "norm_mix_w", "norm_ffn_w", "norm_final_w")
BIG = ("w_in", "w_out", "w_ff1", "w_ff2")
ORDER = ("w_in", "conv_w", "gdn_a_log", "gdn_dt_bias", "gdn_norm_w", "hgrn_lb_logits", "hgrn_norm_w",
         "w_out", "norm_mix_w", "norm_ffn_w", "w_ff1", "w_ff2", "norm_final_w")


def _pack(pieces):
    flat = jnp.concatenate([p.reshape(-1).astype(F32) for p in pieces])
    rows = -(-flat.shape[0] // (8 * HEAD_DIM)) * 8
    return jnp.pad(flat, (0, rows * HEAD_DIM - flat.shape[0])).reshape(rows, HEAD_DIM)


def _unpack(packed, shapes):
    flat, out, at = packed.reshape(-1), [], 0
    for s in shapes:
        n = 1
        for dim in s:
            n *= dim
        out.append(flat[at:at + n].reshape(s))
        at += n
    return out


def kernel(x, w_in, conv_w, gdn_a_log, gdn_dt_bias, gdn_norm_w, hgrn_lb_logits, hgrn_norm_w, w_out, norm_mix_w, norm_ffn_w, w_ff1, w_ff2, norm_final_w, loss_target, m_w_in, m_conv_w, m_gdn_a_log, m_gdn_dt_bias, m_gdn_norm_w, m_hgrn_lb_logits, m_hgrn_norm_w, m_w_out, m_norm_mix_w, m_norm_ffn_w, m_w_ff1, m_w_ff2, m_norm_final_w, v_w_in, v_conv_w, v_gdn_a_log, v_gdn_dt_bias, v_gdn_norm_w, v_hgrn_lb_logits, v_hgrn_norm_w, v_w_out, v_norm_mix_w, v_norm_ffn_w, v_w_ff1, v_w_ff2, v_norm_final_w):
    w = dict(w_in=w_in, conv_w=conv_w, gdn_a_log=gdn_a_log, gdn_dt_bias=gdn_dt_bias, gdn_norm_w=gdn_norm_w,
             hgrn_lb_logits=hgrn_lb_logits, hgrn_norm_w=hgrn_norm_w, w_out=w_out, norm_mix_w=norm_mix_w,
             norm_ffn_w=norm_ffn_w, w_ff1=w_ff1, w_ff2=w_ff2, norm_final_w=norm_final_w)
    m = dict(w_in=m_w_in, conv_w=m_conv_w, gdn_a_log=m_gdn_a_log, gdn_dt_bias=m_gdn_dt_bias,
             gdn_norm_w=m_gdn_norm_w, hgrn_lb_logits=m_hgrn_lb_logits, hgrn_norm_w=m_hgrn_norm_w,
             w_out=m_w_out, norm_mix_w=m_norm_mix_w, norm_ffn_w=m_norm_ffn_w, w_ff1=m_w_ff1, w_ff2=m_w_ff2,
             norm_final_w=m_norm_final_w)
    v = dict(w_in=v_w_in, conv_w=v_conv_w, gdn_a_log=v_gdn_a_log, gdn_dt_bias=v_gdn_dt_bias,
             gdn_norm_w=v_gdn_norm_w, hgrn_lb_logits=v_hgrn_lb_logits, hgrn_norm_w=v_hgrn_norm_w,
             w_out=v_w_out, norm_mix_w=v_norm_mix_w, norm_ffn_w=v_norm_ffn_w, w_ff1=v_w_ff1, w_ff2=v_w_ff2,
             norm_final_w=v_norm_final_w)
    d = x.shape[-1]
    h = d // (2 * HEAD_DIM)
    my_c = lax.axis_index("c").astype(jnp.int32).reshape(1)
    my_chip = (2 * lax.axis_index("x") + lax.axis_index("y")).astype(jnp.int32)

    shards = [w[n][0].astype(BF16) for n in BIG]
    conv_shard = jnp.pad(conv_w[0], ((0, 8 - CONV_W), (0, 0)))
    quarter_part, edge_part = _in_proj_quarter_part(w_in[0], my_chip, h)
    first = _gather_weights([quarter_part], [conv_shard, edge_part], "gather_in_proj", GATHER_EXCHANGE,
                            relations=(0, 1))
    diagonal, = _gather_weights([quarter_part], [], "gather_in_proj_diagonal", DIAGONAL_EXCHANGE, relations=(2,))
    n1 = _rms_fwd(x[0], norm_mix_w[0], "rms_mix")
    gathered_in, n1, *shards[1:] = lax.optimization_barrier((first[0], n1, *shards[1:]))
    own_slot = lambda st, own: lax.dynamic_update_index_in_dim(st, own, my_chip, 0)
    f_conv, f_edges = own_slot(first[1], conv_shard), own_slot(first[2], edge_part)
    early, w_ab = _in_proj_quarters(own_slot(gathered_in, quarter_part), f_edges, h)
    w_main = (early, diagonal, f_edges, my_chip)
    cols = lambda st: st.transpose(1, 0, 2).reshape(st.shape[1], -1)
    conv_full = cols(f_conv[:, :CONV_W])
    rest = _gather_weights(shards[1:], [], "gather_rest", GATHER_EXCHANGE, after=[w_ab])

    def late_weights(anchor):
        held = lax.optimization_barrier((*rest, anchor))[:len(rest)]
        f_out, f_ff1, f_ff2 = (own_slot(st, own) for st, own in zip(held, shards[1:]))
        return f_out.reshape(-1, d), f_ff1, f_ff2.reshape(-1, d)

    reducer = _GradReducer(w, m, v, my_c, my_chip)
    loss, dx, g = _local_step(
        x[0], loss_target[0], w_main, w_ab, conv_full, gdn_a_log[0], gdn_dt_bias[0], gdn_norm_w[0],
        hgrn_lb_logits, hgrn_norm_w[0], late_weights, norm_mix_w[0], norm_ffn_w[0],
        None, None, norm_final_w, reducer, n1)

    grads, delta, new_m, new_v = {}, {}, {}, {}
    for n, out in reducer.finish().items():
        grads[n], delta[n], new_m[n], new_v[n] = out

    edges = _in_proj_edge_columns(g["w_main"], g["w_ab"], h)
    small_shapes = [w[n].shape for n in SMALL] + [conv_full.shape, (1,), edges.shape]
    total = _all_reduce_small(_pack([g[n] for n in SMALL] + [g["conv_w"], loss[0, :1], edges]))
    *small_grads, conv_grad, loss_sum, edges = _unpack(total, small_shapes)
    g_in = _in_proj_shard_grad(*reducer.in_proj_halves, edges, my_c, my_chip, h)
    grads["w_in"], delta["w_in"], new_m["w_in"], new_v["w_in"] = _adamw_minor_rows(
        w["w_in"], g_in, m["w_in"], v["w_in"], "adamw_w_in")
    for n, sg in zip(SMALL, small_grads):
        grads[n] = sg
    shard_cols = conv_w.shape[-1]
    grads["conv_w"] = lax.dynamic_slice_in_dim(conv_grad, my_chip * shard_cols, shard_cols, axis=1)[None]

    packed_names = SMALL + ("conv_w",)
    packed = [_pack([t[n] for n in packed_names]) for t in (w, grads, m, v)]
    outs = _adamw(*packed, "adamw_small")
    shapes = [w[n].shape for n in packed_names]
    for res, o in zip((delta, new_m, new_v), outs):
        for n, a in zip(packed_names, _unpack(o, shapes)):
            res[n] = a

    return (loss_sum.reshape(()), dx[None], *[grads[n] for n in ORDER], *[delta[n] for n in ORDER],
            *[new_m[n] for n in ORDER], *[new_v[n] for n in ORDER])
```

```python
import functools

import jax
import jax.numpy as jnp
from jax import lax
from jax.experimental import pallas as pl
from jax.experimental.pallas import tpu as pltpu
from jax.experimental.pallas import tpu_sc as plsc

F32 = jnp.float32
BF16 = jnp.bfloat16

HEAD_DIM = 128
CHUNK = 128
SUB = 16
EXP_CAP = 80.0
NORM_EPS = 1e-6
L2_EPS = 1e-6
CONV_W = 4
VMEM_LIMIT = 56 * 1024 * 1024

ADAM_LR, ADAM_B1, ADAM_B2, ADAM_EPS, ADAM_WD, ADAM_STEP = 1e-3, 0.9, 0.999, 1e-8, 0.01, 10

NN = ((1,), (0,))
NT = ((1,), (1,))
TN = ((0,), (0,))
MESH = pl.DeviceIdType.MESH


def _dot(a, b, dims):
    return lax.dot_general(a.astype(BF16), b.astype(BF16), (dims, ((), ())),
                           preferred_element_type=F32)


def _split(a):
    hi = a.astype(BF16)
    return hi, (a - hi.astype(F32)).astype(BF16)


def _dot3(a, b, dims):
    ah, al = _split(a)
    bh, bl = _split(b)
    d = lambda x, y: lax.dot_general(x, y, (dims, ((), ())), preferred_element_type=F32)
    return d(ah, bh) + (d(ah, bl) + d(al, bh))


def _sigmoid(x):
    return 1.0 / (1.0 + jnp.exp(-x))


def _silu(x):
    return x * _sigmoid(x)


def _dsilu(x):
    s = _sigmoid(x)
    return s * (1.0 + x * (1.0 - s))


def _softplus(x):
    e = jnp.exp(-jnp.abs(x))
    u = 1.0 + e
    log1p = jnp.where(u == 1.0, e, jnp.log(u) * (e / jnp.where(u == 1.0, 1.0, u - 1.0)))
    return jnp.maximum(x, 0.0) + log1p


def _iota(shape, axis):
    return lax.broadcasted_iota(jnp.int32, shape, axis)


def _cumsum_rows(x):
    n = x.shape[0]
    row = _iota(x.shape, 0)
    s = 1
    while s < n:
        x = x + jnp.where(row >= s, pltpu.roll(x, s, 0), 0.0)
        s *= 2
    return x


def _rev_cumsum_rows(x):
    return jnp.sum(x, axis=0, keepdims=True) - _cumsum_rows(x) + x


def _params(sem):
    return pltpu.CompilerParams(dimension_semantics=sem, vmem_limit_bytes=VMEM_LIMIT)


ROW_TILE = 8
HEADS_PER_STEP = 8


def _hps(h):
    return min(HEADS_PER_STEP, h)


def _head_view(ref, hb):
    if len(ref.shape) == 2:
        return ref.at[:, pl.ds(hb * HEAD_DIM, HEAD_DIM)]
    return ref.at[hb]


class _Staged:
    def __init__(self, ref, load):
        self.ref = ref
        self.loaded = ref[...] if load else None
        self.written = None

    def __getitem__(self, idx):
        return self.loaded

    def __setitem__(self, idx, value):
        self.written = value


def _each_head(one_head, n_in):
    def body(*refs):
        @pl.when(pl.program_id(1) == 0)
        def _():
            refs[-1][...] = jnp.zeros_like(refs[-1])

        last = len(refs) - 1
        staged = [[_Staged(_head_view(r, hb), i < n_in or i == last) for i, r in enumerate(refs)]
                  for hb in range(refs[-1].shape[0])]
        running = [one_head(*per_head) for per_head in staged]
        while running:
            for gen in list(running):
                try:
                    next(gen)
                except StopIteration:
                    running.remove(gen)
        for per_head in staged:
            for s in per_head:
                if s.written is not None:
                    s.ref[...] = s.written
    return body


def _tile(n, want):
    t = min(n, want)
    while n % t:
        t //= 2
    return t


def _mm(a, b, mode, out_dtypes, name, epi=None, extras=(), tm=1024, tn=1024, tk=2048,
        b_stacked=False, out_stacked=False):
    if mode == "tn":
        kdim, m = a.shape
    else:
        m, kdim = a.shape
    if b_stacked:
        n = N_CHIPS * b.shape[2] if mode == "nn" else b.shape[1]
        kdim_b = b.shape[1] if mode == "nn" else N_CHIPS * b.shape[2]
        assert kdim_b == kdim
    else:
        n = b.shape[0] if mode == "nt" else b.shape[1]
    per_shard = (n if (mode == "nn" or out_stacked) else kdim) // N_CHIPS
    tm, tn, tk = _tile(m, tm), _tile(n, tn), _tile(kdim, tk)
    if (b_stacked and mode == "nn") or out_stacked:
        tn = _tile(per_shard, tn)
    if b_stacked and mode == "nt":
        tk = _tile(per_shard, tk)
    nk = kdim // tk
    dims = {"nn": NN, "nt": NT, "tn": TN}[mode]
    a_spec = (pl.BlockSpec((tk, tm), lambda i, j, k: (k, i)) if mode == "tn"
              else pl.BlockSpec((tm, tk), lambda i, j, k: (i, k)))
    if b_stacked and mode == "nn":
        per = per_shard // tn
        b_spec = pl.BlockSpec((None, tk, tn), lambda i, j, k: (j // per, k, j % per))
    elif b_stacked:
        per = per_shard // tk
        b_spec = pl.BlockSpec((None, tn, tk), lambda i, j, k: (k // per, j, k % per))
    else:
        b_spec = (pl.BlockSpec((tn, tk), lambda i, j, k: (j, k)) if mode == "nt"
                  else pl.BlockSpec((tk, tn), lambda i, j, k: (k, j)))
    mn_spec = pl.BlockSpec((tm, tn), lambda i, j, k: (i, j))
    if out_stacked:
        per_o = per_shard // tn
        out_spec = pl.BlockSpec((None, tm, tn), lambda i, j, k: (j // per_o, i, j % per_o))
        out_shape = (N_CHIPS, m, per_shard)
    else:
        out_spec, out_shape = mn_spec, (m, n)
    ne, no = len(extras), len(out_dtypes)
    if epi is None:
        epi = lambda acc: (acc,)

    def body(a_ref, b_ref, *rest):
        extra_refs, out_refs = rest[:ne], rest[ne:ne + no]
        part = _dot(a_ref[...], b_ref[...], dims)

        def finish(total):
            outs = epi(total, *[r[...] for r in extra_refs])
            for o_ref, o in zip(out_refs, outs):
                o_ref[...] = o.astype(o_ref.dtype)

        if nk == 1:
            finish(part)
            return
        acc = rest[-1]
        k = pl.program_id(2)

        @pl.when(k == 0)
        def _():
            acc[...] = part

        @pl.when(jnp.logical_and(k > 0, k < nk - 1))
        def _():
            acc[...] += part

        @pl.when(k == nk - 1)
        def _():
            finish(acc[...] + part)

    outs = pl.pallas_call(
        body, name=name,
        out_shape=tuple(jax.ShapeDtypeStruct(out_shape, d) for d in out_dtypes),
        grid=(m // tm, n // tn, nk),
        in_specs=[a_spec, b_spec] + [mn_spec] * ne,
        out_specs=tuple(out_spec for _ in out_dtypes),
        scratch_shapes=[pltpu.VMEM((tm, tn), F32)] if nk > 1 else [],
        compiler_params=_params(("parallel", "parallel", "arbitrary")),
    )(a, b, *extras)
    return outs if no > 1 else outs[0]


def _mm_residual_norm(a, b, res, norm_w, name, tm=512):
    m, kdim = a.shape
    n = b.shape[1]
    tm = _tile(m, tm)

    def body(a_ref, b_ref, r_ref, w_ref, h_ref, n_ref):
        h = _dot(a_ref[...], b_ref[...], NN) + r_ref[...]
        h_ref[...] = h
        inv = lax.rsqrt(jnp.mean(h * h, axis=-1, keepdims=True) + NORM_EPS)
        n_ref[...] = (h * inv * w_ref[...]).astype(n_ref.dtype)

    row = lambda width: pl.BlockSpec((tm, width), lambda i: (i, 0))
    whole = lambda rows: pl.BlockSpec((rows, n), lambda i: (0, 0))
    return pl.pallas_call(
        body, name=name,
        out_shape=(jax.ShapeDtypeStruct((m, n), F32), jax.ShapeDtypeStruct((m, n), BF16)), grid=(m // tm,),
        in_specs=[row(kdim), whole(kdim), row(n), whole(1)], out_specs=(row(n), row(n)),
        compiler_params=_params(("parallel",)),
    )(a, b, res, norm_w.reshape(1, n))


def _mm_quarters(a, stack, quarters, name, into=None, tm=1024, tn=1024):
    m, kdim = a.shape
    qw = stack.shape[2]
    tm, tn = _tile(m, tm), _tile(qw, tn)
    per = qw // tn

    def body(q_ref, a_ref, b_ref, *rest):
        rest[-1][...] = _dot(a_ref[...], b_ref[...], NN)

    in_specs = [pl.BlockSpec((tm, kdim), lambda i, j, q: (i, 0)),
                pl.BlockSpec((None, kdim, tn), lambda i, j, q: (q[j // per], 0, j % per))]
    args = (quarters, a, stack)
    if into is not None:
        in_specs.append(ANY)
        args += (into,)
    return pl.pallas_call(
        body, name=name, out_shape=jax.ShapeDtypeStruct((m, N_CHIPS * qw), F32),
        grid_spec=pltpu.PrefetchScalarGridSpec(
            num_scalar_prefetch=1, grid=(m // tm, quarters.shape[0] * per), in_specs=in_specs,
            out_specs=pl.BlockSpec((tm, tn), lambda i, j, q: (i, q[j // per] * per + j % per))),
        input_output_aliases={3: 0} if into is not None else {},
        compiler_params=_params(("parallel", "parallel")),
    )(*args)


ROWS = 256


def _rms_fwd(x, w, name):
    t, d = x.shape
    tr = _tile(t, ROWS)

    def body(x_ref, w_ref, n_ref):
        xv = x_ref[...]
        r = lax.rsqrt(jnp.mean(xv * xv, axis=-1, keepdims=True) + NORM_EPS)
        n_ref[...] = (xv * r * w_ref[...]).astype(n_ref.dtype)

    return pl.pallas_call(
        body, name=name, out_shape=jax.ShapeDtypeStruct((t, d), BF16), grid=(t // tr,),
        in_specs=[pl.BlockSpec((tr, d), lambda i: (i, 0)), pl.BlockSpec((1, d), lambda i: (0, 0))],
        out_specs=pl.BlockSpec((tr, d), lambda i: (i, 0)),
        compiler_params=_params(("parallel",)),
    )(x, w.reshape(1, d))


def _rms_bwd(dn, x, w, dres, name):
    t, d = x.shape
    tr = _tile(t, ROWS)

    def body(dn_ref, x_ref, w_ref, dres_ref, dx_ref, dxb_ref, dw_ref):
        i = pl.program_id(0)
        xv, dnv = x_ref[...], dn_ref[...]
        r = lax.rsqrt(jnp.mean(xv * xv, axis=-1, keepdims=True) + NORM_EPS)
        xh = xv * r
        dxh = dnv * w_ref[...]
        dx = dres_ref[...] + r * (dxh - xh * jnp.mean(dxh * xh, axis=-1, keepdims=True))
        dx_ref[...] = dx
        dxb_ref[...] = dx.astype(BF16)

        @pl.when(i == 0)
        def _():
            dw_ref[...] = jnp.zeros_like(dw_ref)

        dw_ref[...] += jnp.sum(dnv * xh, axis=0, keepdims=True)

    row = pl.BlockSpec((tr, d), lambda i: (i, 0))
    vec = pl.BlockSpec((1, d), lambda i: (0, 0))
    return pl.pallas_call(
        body, name=name,
        out_shape=(jax.ShapeDtypeStruct((t, d), F32), jax.ShapeDtypeStruct((t, d), BF16),
                   jax.ShapeDtypeStruct((1, d), F32)),
        grid=(t // tr,), in_specs=[row, row, vec, row], out_specs=(row, row, vec),
        compiler_params=_params(("arbitrary",)),
    )(dn, x, w.reshape(1, d), dres)


def _loss_head(h, w, target):
    t, d = h.shape
    tr = _tile(t, ROWS)

    def body(h_ref, w_ref, t_ref, loss_ref, dh_ref, dhb_ref, dw_ref):
        i = pl.program_id(0)
        hv, wv = h_ref[...], w_ref[...]
        r = lax.rsqrt(jnp.mean(hv * hv, axis=-1, keepdims=True) + NORM_EPS)
        hh = hv * r
        err = hh * wv - t_ref[...]
        dout = err * (1.0 / d)
        dhh = dout * wv
        dh = r * (dhh - hh * jnp.mean(dhh * hh, axis=-1, keepdims=True))
        dh_ref[...] = dh
        dhb_ref[...] = dh.astype(BF16)

        @pl.when(i == 0)
        def _():
            dw_ref[...] = jnp.zeros_like(dw_ref)
            loss_ref[...] = jnp.zeros_like(loss_ref)

        dw_ref[...] += jnp.sum(dout * hh, axis=0, keepdims=True)
        loss_ref[...] += jnp.full((1, 128), 0.5 / d, F32) * jnp.sum(err * err)

    row = pl.BlockSpec((tr, d), lambda i: (i, 0))
    vec = pl.BlockSpec((1, d), lambda i: (0, 0))
    lspec = pl.BlockSpec((1, 128), lambda i: (0, 0))
    return pl.pallas_call(
        body, name="loss_head",
        out_shape=(jax.ShapeDtypeStruct((1, 128), F32), jax.ShapeDtypeStruct((t, d), F32),
                   jax.ShapeDtypeStruct((t, d), BF16), jax.ShapeDtypeStruct((1, d), F32)),
        grid=(t // tr,), in_specs=[row, vec, row], out_specs=(lspec, row, row, vec),
        compiler_params=_params(("arbitrary",)),
    )(h, w.reshape(1, d), target)


def _inv_unit_lower(a):
    c = a.shape[0]
    eye = (_iota((c, c), 0) == _iota((c, c), 1)).astype(F32)
    x = eye - a
    p = _dot3(a, a, NN)
    yield
    n = 2
    while n < c:
        x = x + _dot3(x, p, NN)
        n *= 2
        if n < c:
            p = _dot3(p, p, NN)
        yield
    return x


def _gdn_chunk(q, k, v, beta, g):
    c = q.shape[0]
    row, col = _iota((c, c), 0), _iota((c, c), 1)
    gc = _cumsum_rows(g)
    diff = gc - gc.T
    dec = jnp.where(row >= col, jnp.exp(jnp.minimum(diff, 0.0)), 0.0)
    dec_s = jnp.where(row > col, dec, 0.0)
    gam = jnp.exp(gc)
    g_last = jnp.sum(g, axis=0, keepdims=True)
    kk = _dot(k, k, NT)
    a = beta * kk * dec_s
    p = _dot(q, k, NT) * dec
    e_end = jnp.exp(g_last - gc)
    return dict(dec=dec, dec_s=dec_s, gam=gam, gam_last=jnp.exp(g_last), e_end=e_end,
                k_end=k * e_end, kk=kk, a=a, p=p)


def _gdn_fwd(q, k, v, beta_bc, g_bc):
    t = q.shape[0]
    h = q.shape[1] // HEAD_DIM
    nc = t // CHUNK

    def body(q_ref, k_ref, v_ref, b_ref, g_ref, o_ref, s_ref, t_ref, state):
        qv, kv, vv, beta = q_ref[...], k_ref[...], v_ref[...], b_ref[...]
        ch = _gdn_chunk(qv, kv, vv, beta, g_ref[...])
        yield
        tm = yield from _inv_unit_lower(ch["a"])
        sol = _dot(tm, jnp.concatenate([beta * vv, beta * ch["gam"] * kv], axis=1), NN)
        yield
        u_v, w = sol[:, :HEAD_DIM], sol[:, HEAD_DIM:]
        s0 = state[...]
        u = u_v - _dot(w, s0, NN)
        yield
        o_ref[...] = _dot(qv * ch["gam"], s0, NN) + _dot(ch["p"], u, NN)
        s_ref[...] = s0
        t_ref[...] = tm
        state[...] = ch["gam_last"] * s0 + _dot(ch["k_end"], u, TN)

    tok = pl.BlockSpec((CHUNK, _hps(h) * HEAD_DIM), lambda hh, c: (c, hh))
    bc = pl.BlockSpec((_hps(h), CHUNK, HEAD_DIM), lambda hh, c: (hh, c, 0))
    mat = pl.BlockSpec((_hps(h), None, HEAD_DIM, HEAD_DIM), lambda hh, c: (hh, c, 0, 0))
    return pl.pallas_call(
        _each_head(body, 5), name="gdn_fwd",
        out_shape=(jax.ShapeDtypeStruct(q.shape, F32),
                   jax.ShapeDtypeStruct((h, nc, HEAD_DIM, HEAD_DIM), F32),
                   jax.ShapeDtypeStruct((h, nc, CHUNK, CHUNK), F32)),
        grid=(h // _hps(h), nc), in_specs=[tok, tok, tok, bc, bc], out_specs=(tok, mat, mat),
        scratch_shapes=[pltpu.VMEM((_hps(h), HEAD_DIM, HEAD_DIM), F32)],
        compiler_params=_params(("parallel", "arbitrary")),
    )(q, k, v, beta_bc, g_bc)


def _gdn_bwd(q, k, v, beta_bc, g_bc, states, invs, do, do_blk=0):
    t = q.shape[0]
    h = q.shape[1] // HEAD_DIM
    nc = t // CHUNK

    def body(q_ref, k_ref, v_ref, b_ref, g_ref, s_ref, t_ref, do_ref,
             dq_ref, dk_ref, dv_ref, db_ref, dg_ref, dstate):
        qv, kv, vv, beta = q_ref[...], k_ref[...], v_ref[...], b_ref[...]
        dov, s0, tm, ds1 = do_ref[...], s_ref[...], t_ref[...], dstate[...]
        ch = _gdn_chunk(qv, kv, vv, beta, g_ref[...])
        yield
        gam, dec, dec_s, kk = ch["gam"], ch["dec"], ch["dec_s"], ch["kk"]
        r_v, r_w = beta * vv, beta * gam * kv
        sol = _dot(tm, jnp.concatenate([r_v, r_w], axis=1), NN)
        yield
        u_v, w = sol[:, :HEAD_DIM], sol[:, HEAD_DIM:]
        u = u_v - _dot(w, s0, NN)
        qg = qv * gam
        yield

        du = _dot(ch["p"], dov, TN) + _dot(ch["k_end"], ds1, NN)
        dp = _dot(dov, u, NT)
        dpd = dp * dec
        dqg = _dot(dov, s0, NT)
        dk_end = _dot(u, ds1, NT)
        yield
        dq = dqg * gam + _dot(dpd, kv, NN)
        dk = _dot(dpd, qv, TN) + dk_end * ch["e_end"]
        dstate[...] = _dot(qg, dov, TN) + ch["gam_last"] * ds1 - _dot(w, du, TN)
        dw = -_dot(du, s0, NT)
        yield
        dr = _dot(tm, jnp.concatenate([du, dw], axis=1), TN)
        yield
        dr_v, dr_w = dr[:, :HEAD_DIM], dr[:, HEAD_DIM:]
        da = -_dot(dr, sol, NT)
        yield
        dkk = da * beta * dec_s
        dk = dk + _dot(dkk, kv, NN) + _dot(dkk, kv, TN) + beta * gam * dr_w
        dbeta = (jnp.sum(da * kk * dec_s, axis=1, keepdims=True)
                 + jnp.sum(dr_v * vv + dr_w * gam * kv, axis=1, keepdims=True))

        pair = dp * ch["p"] + da * ch["a"]
        end = jnp.sum(dk_end * ch["k_end"], axis=1, keepdims=True)
        dgc = (jnp.sum(pair - pair.T, axis=1, keepdims=True)
               + jnp.sum(dqg * qg + dr_w * r_w, axis=1, keepdims=True) - end)
        at_end = jnp.sum(end) + ch["gam_last"] * jnp.sum(s0 * ds1)
        dgc = jnp.broadcast_to(dgc, (CHUNK, HEAD_DIM))
        dgc = dgc + jnp.where(_iota((CHUNK, HEAD_DIM), 0) == CHUNK - 1, at_end, 0.0)
        dq_ref[...] = dq
        dk_ref[...] = dk
        dv_ref[...] = beta * dr_v
        db_ref[...] = jnp.broadcast_to(dbeta, (CHUNK, HEAD_DIM)).T[:ROW_TILE]
        dg_ref[...] = _rev_cumsum_rows(dgc).T[:ROW_TILE]

    rev = lambda c: nc - 1 - c
    tok = pl.BlockSpec((CHUNK, _hps(h) * HEAD_DIM), lambda hh, c: (rev(c), hh))
    bc = pl.BlockSpec((_hps(h), CHUNK, HEAD_DIM), lambda hh, c: (hh, rev(c), 0))
    mat = pl.BlockSpec((_hps(h), None, HEAD_DIM, HEAD_DIM), lambda hh, c: (hh, rev(c), 0, 0))
    tok_shape = jax.ShapeDtypeStruct(q.shape, F32)
    row_shape = jax.ShapeDtypeStruct((h, nc, ROW_TILE, CHUNK), F32)
    rows = pl.BlockSpec((_hps(h), None, ROW_TILE, CHUNK), lambda hh, c: (hh, rev(c), 0, 0))
    return pl.pallas_call(
        _each_head(body, 8), name="gdn_bwd",
        out_shape=(tok_shape, tok_shape, tok_shape, row_shape, row_shape),
        grid=(h // _hps(h), nc),
        in_specs=[tok, tok, tok, bc, bc, mat, mat,
                  pl.BlockSpec((CHUNK, _hps(h) * HEAD_DIM), lambda hh, c: (rev(c), do_blk // _hps(h) + hh))],
        out_specs=(tok, tok, tok, rows, rows),
        scratch_shapes=[pltpu.VMEM((_hps(h), HEAD_DIM, HEAD_DIM), F32)],
        compiler_params=_params(("parallel", "arbitrary")),
    )(q, k, v, beta_bc, g_bc, states, invs, do)


def _hgrn_chunk(q, k, lf):
    c = q.shape[0]
    row = _iota((c, HEAD_DIM), 0)
    b = _cumsum_rows(lf)
    q_subs, k_facs, a_rows = [], [], []
    for x in range(c // SUB):
        b_start = jnp.sum(jnp.where(row < x * SUB, lf, 0.0), axis=0, keepdims=True)
        q_x = (q * jnp.exp(jnp.minimum(b - b_start, 0.0)))[x * SUB:(x + 1) * SUB]
        k_fac = jnp.where(row < (x + 1) * SUB, jnp.exp(jnp.minimum(b_start - b, EXP_CAP)), 0.0)
        q_subs.append(q_x)
        k_facs.append(k_fac)
        a_rows.append(_dot(q_x, k * k_fac, NT))
    a = jnp.concatenate(a_rows, axis=0)
    a = jnp.where(_iota((c, c), 0) >= _iota((c, c), 1), a, 0.0)
    b_last = jnp.sum(lf, axis=0, keepdims=True)
    return dict(b=b, a=a, q_subs=q_subs, k_facs=k_facs, e_b=jnp.exp(b),
                e_end=jnp.exp(b_last - b), e_last=jnp.exp(b_last))


def _hgrn_fwd(q, k, v, lf, v_blk=0):
    t = q.shape[0]
    h = q.shape[1] // HEAD_DIM
    nc = t // CHUNK

    def body(q_ref, k_ref, v_ref, lf_ref, o_ref, s_ref, state):
        qv, kv, vv = q_ref[...], k_ref[...], v_ref[...]
        ch = _hgrn_chunk(qv, kv, lf_ref[...])
        yield
        s0 = state[...]
        o_ref[...] = _dot(qv * ch["e_b"], s0, NT) + _dot(ch["a"], vv, NN)
        s_ref[...] = s0
        state[...] = s0 * ch["e_last"] + _dot(vv, kv * ch["e_end"], TN)

    tok = pl.BlockSpec((CHUNK, _hps(h) * HEAD_DIM), lambda hh, c: (c, hh))
    mat = pl.BlockSpec((_hps(h), None, HEAD_DIM, HEAD_DIM), lambda hh, c: (hh, c, 0, 0))
    return pl.pallas_call(
        _each_head(body, 4), name="hgrn_fwd",
        out_shape=(jax.ShapeDtypeStruct(q.shape, F32),
                   jax.ShapeDtypeStruct((h, nc, HEAD_DIM, HEAD_DIM), F32)),
        grid=(h // _hps(h), nc),
        in_specs=[tok, tok, pl.BlockSpec((CHUNK, _hps(h) * HEAD_DIM), lambda hh, c: (c, v_blk // _hps(h) + hh)), tok],
        out_specs=(tok, mat),
        scratch_shapes=[pltpu.VMEM((_hps(h), HEAD_DIM, HEAD_DIM), F32)],
        compiler_params=_params(("parallel", "arbitrary")),
    )(q, k, v, lf)


def _hgrn_bwd(q, k, v, lf, states, do, v_blk=0, do_blk=0):
    t = q.shape[0]
    nc = t // CHUNK
    h = q.shape[1] // HEAD_DIM

    def body(q_ref, k_ref, v_ref, lf_ref, s_ref, do_ref, dq_ref, dk_ref, dv_ref, dlf_ref, dstate):
        qv, kv, vv, dov, s0 = q_ref[...], k_ref[...], v_ref[...], do_ref[...], s_ref[...]
        ds1 = dstate[...]
        ch = _hgrn_chunk(qv, kv, lf_ref[...])
        yield
        c = CHUNK
        row = _iota((c, HEAD_DIM), 0)
        qh = qv * ch["e_b"]
        k_end = kv * ch["e_end"]
        da = jnp.where(_iota((c, c), 0) >= _iota((c, c), 1), _dot(dov, vv, NT), 0.0)
        dqh = _dot(dov, s0, NN)
        dk_end = _dot(vv, ds1, NN)
        yield
        end = dk_end * k_end
        dk = dk_end * ch["e_end"]
        db = dqh * qh - end + jnp.where(
            row == c - 1, jnp.sum(end + s0 * ch["e_last"] * ds1, axis=0, keepdims=True), 0.0)
        dq_rows, qdq_rows = [], []
        for x in range(c // SUB):
            da_x = da[x * SUB:(x + 1) * SUB]
            k_x = kv * ch["k_facs"][x]
            dq_x = _dot(da_x, k_x, NN)
            dk_x = _dot(da_x, ch["q_subs"][x], TN)
            dq_rows.append(dq_x)
            qdq_rows.append(dq_x * ch["q_subs"][x])
            dk = dk + dk_x * ch["k_facs"][x]
            kdk = dk_x * k_x
            db = db - kdk
            if x > 0:
                at_start = jnp.sum(kdk, axis=0, keepdims=True) - jnp.sum(qdq_rows[x], axis=0, keepdims=True)
                db = db + jnp.where(row == x * SUB - 1, at_start, 0.0)
        yield
        b_start = jnp.zeros((c, HEAD_DIM), F32)
        for x in range(1, c // SUB):
            b_x = jnp.sum(jnp.where(row < x * SUB, lf_ref[...], 0.0), axis=0, keepdims=True)
            b_start = jnp.where(row >= x * SUB, b_x, b_start)
        dq = dqh * ch["e_b"] + jnp.concatenate(dq_rows, axis=0) * jnp.exp(jnp.minimum(ch["b"] - b_start, 0.0))
        db = db + jnp.concatenate(qdq_rows, axis=0)
        dstate[...] = _dot(dov, qh, TN) + ds1 * ch["e_last"]
        dq_ref[...] = dq
        dk_ref[...] = dk
        dv_ref[...] = _dot(ch["a"], dov, TN) + _dot(k_end, ds1, NT)
        dlf_ref[...] = _rev_cumsum_rows(db)

    rev = lambda c: nc - 1 - c
    tok = pl.BlockSpec((CHUNK, _hps(h) * HEAD_DIM), lambda hh, c: (rev(c), hh))
    mat = pl.BlockSpec((_hps(h), None, HEAD_DIM, HEAD_DIM), lambda hh, c: (hh, rev(c), 0, 0))
    tok_shape = jax.ShapeDtypeStruct(q.shape, F32)
    return pl.pallas_call(
        _each_head(body, 6), name="hgrn_bwd",
        out_shape=(tok_shape, tok_shape, tok_shape, tok_shape),
        grid=(h // _hps(h), nc),
        in_specs=[tok, tok, pl.BlockSpec((CHUNK, _hps(h) * HEAD_DIM), lambda hh, c: (rev(c), v_blk // _hps(h) + hh)), tok, mat,
                  pl.BlockSpec((CHUNK, _hps(h) * HEAD_DIM), lambda hh, c: (rev(c), do_blk // _hps(h) + hh))],
        out_specs=(tok, tok, tok, tok),
        scratch_shapes=[pltpu.VMEM((_hps(h), HEAD_DIM, HEAD_DIM), F32)],
        compiler_params=_params(("parallel", "arbitrary")),
    )(q, k, v, lf, states, do)


CONV_ROWS = 256
HALO = 8


def _shift_down(cur, prev, s):
    rt = cur.shape[0]
    head = jnp.concatenate([pltpu.roll(prev, s, 0), jnp.zeros((rt - HALO, cur.shape[1]), F32)], axis=0)
    return jnp.where(_iota(cur.shape, 0) < s, head, pltpu.roll(cur, s, 0))


def _shift_up(cur, nxt, s):
    rt = cur.shape[0]
    tail = jnp.concatenate([jnp.zeros((rt - HALO, cur.shape[1]), F32), pltpu.roll(nxt, HALO - s, 0)], axis=0)
    return jnp.where(_iota(cur.shape, 0) >= rt - s, tail, pltpu.roll(cur, rt - s, 0))


def _tile_with_prev(ref, i, rt):
    r0 = pl.multiple_of(i * rt, rt)
    cur = ref[pl.ds(r0, rt), :]
    prev = ref[pl.ds(pl.multiple_of(jnp.maximum(r0 - HALO, 0), HALO), HALO), :]
    return cur, jnp.where(i > 0, prev, 0.0)


def _tile_with_next(ref, i, rt, n_tiles):
    r0 = pl.multiple_of(i * rt, rt)
    cur = ref[pl.ds(r0, rt), :]
    nxt = ref[pl.ds(pl.multiple_of(jnp.minimum(r0 + rt, (n_tiles - 1) * rt), HALO), HALO), :]
    return cur, jnp.where(i < n_tiles - 1, nxt, 0.0)


def _conv_tile(x_ref, w_ref, i, rt):
    cur, prev = _tile_with_prev(x_ref, i, rt)
    shifted = [_shift_down(cur, prev, CONV_W - 1 - j) for j in range(CONV_W - 1)] + [cur]
    c = shifted[0] * w_ref[pl.ds(0, 1), :]
    for j in range(1, CONV_W):
        c = c + shifted[j] * w_ref[pl.ds(j, 1), :]
    return c, shifted


def _l2n(s):
    return s * lax.rsqrt(jnp.sum(s * s, axis=-1, keepdims=True) + L2_EPS)


def _gdn_prep_fwd(proj, conv_w, h):
    t = proj.shape[0]
    rt = _tile(t, CONV_ROWS)
    nt = t // rt
    scale = HEAD_DIM ** -0.5

    def body(xq, xk, xv, wq, wk, wv, q_ref, k_ref, v_ref):
        def tile(i, carry):
            rows = pl.ds(pl.multiple_of(i * rt, rt), rt)
            q_ref[rows, :] = _l2n(_silu(_conv_tile(xq, wq, i, rt)[0])) * scale
            k_ref[rows, :] = _l2n(_silu(_conv_tile(xk, wk, i, rt)[0]))
            v_ref[rows, :] = _silu(_conv_tile(xv, wv, i, rt)[0])
            return carry

        lax.fori_loop(0, nt, tile, 0)

    col = lambda p: pl.BlockSpec((t, HEAD_DIM), lambda hh: (0, p * h + hh))
    wcol = lambda p: pl.BlockSpec((CONV_W, HEAD_DIM), lambda hh: (0, p * h + hh))
    out = pl.BlockSpec((t, HEAD_DIM), lambda hh: (0, hh))
    shape = jax.ShapeDtypeStruct((t, h * HEAD_DIM), F32)
    return pl.pallas_call(
        body, name="gdn_prep_fwd", out_shape=(shape, shape, shape), grid=(h,),
        in_specs=[col(0), col(1), col(2), wcol(0), wcol(1), wcol(2)], out_specs=(out, out, out),
        compiler_params=_params(("parallel",)),
    )(proj, proj, proj, conv_w, conv_w, conv_w)


def _gdn_prep_bwd(proj, conv_w, dq, dk, dv, h):
    t = proj.shape[0]
    rt = _tile(t, CONV_ROWS)
    nt = t // rt
    scale = HEAD_DIM ** -0.5

    def part(x_ref, w_ref, dy_ref, dx_ref, dw_ref, dc_ref, norm_scale):
        def first(i, dws):
            rows = pl.ds(pl.multiple_of(i * rt, rt), rt)
            c, shifted = _conv_tile(x_ref, w_ref, i, rt)
            ds = dy_ref[rows, :]
            s, ds_dc = _silu_and_grad(c)
            if norm_scale is not None:
                r = lax.rsqrt(jnp.sum(s * s, axis=-1, keepdims=True) + L2_EPS)
                y = s * r
                dyn = ds * norm_scale
                ds = r * (dyn - y * jnp.sum(dyn * y, axis=-1, keepdims=True))
            dc = ds * ds_dc
            dc_ref[rows, :] = dc
            return tuple(dws[j] + jnp.sum(dc * shifted[j], axis=0, keepdims=True) for j in range(CONV_W))

        dws = lax.fori_loop(0, nt, first, tuple(jnp.zeros((1, HEAD_DIM), F32) for _ in range(CONV_W)))
        for j in range(CONV_W):
            dw_ref[pl.ds(j, 1), :] = dws[j]

        def second(i, carry):
            rows = pl.ds(pl.multiple_of(i * rt, rt), rt)
            cur, nxt = _tile_with_next(dc_ref, i, rt, nt)
            dx = cur * w_ref[pl.ds(CONV_W - 1, 1), :]
            for j in range(CONV_W - 1):
                dx = dx + _shift_up(cur, nxt, CONV_W - 1 - j) * w_ref[pl.ds(j, 1), :]
            dx_ref[rows, :] = dx.astype(dx_ref.dtype)
            return carry

        lax.fori_loop(0, nt, second, 0)

    def body(xq, xk, xv, wq, wk, wv, dq_ref, dk_ref, dv_ref, dxq, dxk, dxv, dwq, dwk, dwv, dc_ref):
        part(xq, wq, dq_ref, dxq, dwq, dc_ref, scale)
        part(xk, wk, dk_ref, dxk, dwk, dc_ref, 1.0)
        part(xv, wv, dv_ref, dxv, dwv, dc_ref, None)

    col = lambda p: pl.BlockSpec((t, HEAD_DIM), lambda hh: (0, p * h + hh))
    wcol = lambda p: pl.BlockSpec((CONV_W, HEAD_DIM), lambda hh: (0, p * h + hh))
    own = pl.BlockSpec((t, HEAD_DIM), lambda hh: (0, hh))
    wown = pl.BlockSpec((CONV_W, HEAD_DIM), lambda hh: (0, hh))
    dx_shape = jax.ShapeDtypeStruct((t, h * HEAD_DIM), BF16)
    dw_shape = jax.ShapeDtypeStruct((CONV_W, h * HEAD_DIM), F32)
    return pl.pallas_call(
        body, name="gdn_prep_bwd",
        out_shape=(dx_shape, dx_shape, dx_shape, dw_shape, dw_shape, dw_shape), grid=(h,),
        in_specs=[col(0), col(1), col(2), wcol(0), wcol(1), wcol(2), own, own, own],
        out_specs=(own, own, own, wown, wown, wown),
        scratch_shapes=[pltpu.VMEM((t, HEAD_DIM), F32)],
        compiler_params=_params(("parallel",)),
    )(proj, proj, proj, conv_w, conv_w, conv_w, dq, dk, dv)


def _gdn_gates_fwd(ab, a_log_row, dt_bias_row):
    t = ab.shape[0]
    tr = _tile(t, 512)

    def body(ab_ref, al_ref, dt_ref, g_ref, b_ref):
        g_ref[...] = -jnp.exp(al_ref[...]) * _softplus(ab_ref[:, :HEAD_DIM] + dt_ref[...])
        b_ref[...] = _sigmoid(ab_ref[:, HEAD_DIM:])

    row = pl.BlockSpec((tr, HEAD_DIM), lambda i: (i, 0))
    vec = pl.BlockSpec((1, HEAD_DIM), lambda i: (0, 0))
    shape = jax.ShapeDtypeStruct((t, HEAD_DIM), F32)
    return pl.pallas_call(
        body, name="gdn_gates_fwd", out_shape=(shape, shape), grid=(t // tr,),
        in_specs=[pl.BlockSpec((tr, 2 * HEAD_DIM), lambda i: (i, 0)), vec, vec], out_specs=(row, row),
        compiler_params=_params(("parallel",)),
    )(ab, a_log_row, dt_bias_row)


def _gdn_gates_bwd(ab, a_log_row, dt_bias_row, dg, dbeta):
    t = ab.shape[0]
    tr = _tile(t, 512)

    def body(ab_ref, al_ref, dt_ref, dg_ref, db_ref, dab_ref, dal_ref, ddt_ref):
        @pl.when(pl.program_id(0) == 0)
        def _():
            dal_ref[...] = jnp.zeros_like(dal_ref)
            ddt_ref[...] = jnp.zeros_like(ddt_ref)

        xa = ab_ref[:, :HEAD_DIM] + dt_ref[...]
        neg_a = -jnp.exp(al_ref[...])
        dgv = dg_ref[...]
        da = dgv * neg_a * _sigmoid(xa)
        beta = _sigmoid(ab_ref[:, HEAD_DIM:])
        dab_ref[:, :HEAD_DIM] = da.astype(BF16)
        dab_ref[:, HEAD_DIM:] = (db_ref[...] * beta * (1.0 - beta)).astype(BF16)
        dal_ref[...] += jnp.sum(dgv * neg_a * _softplus(xa), axis=0, keepdims=True)
        ddt_ref[...] += jnp.sum(da, axis=0, keepdims=True)

    row = pl.BlockSpec((tr, HEAD_DIM), lambda i: (i, 0))
    row2 = pl.BlockSpec((tr, 2 * HEAD_DIM), lambda i: (i, 0))
    vec = pl.BlockSpec((1, HEAD_DIM), lambda i: (0, 0))
    vshape = jax.ShapeDtypeStruct((1, HEAD_DIM), F32)
    return pl.pallas_call(
        body, name="gdn_gates_bwd",
        out_shape=(jax.ShapeDtypeStruct((t, 2 * HEAD_DIM), BF16), vshape, vshape), grid=(t // tr,),
        in_specs=[row2, vec, vec, row, row], out_specs=(row2, vec, vec),
        compiler_params=_params(("arbitrary",)),
    )(ab, a_log_row, dt_bias_row, dg, dbeta)


def _lower_bound(lb_ref):
    return _sigmoid(lb_ref[pl.ds(0, 1), :] - lb_ref[pl.ds(1, 1), :])


def _hgrn_prep_fwd(proj, lb_logits, h, q_blk, f_blk):
    t = proj.shape[0]
    tr = _tile(t, 512)

    def body(xq, xf, lb_ref, q_ref, k_ref, lf_ref):
        lb = _lower_bound(lb_ref)
        s = _sigmoid(xf[...])
        q_ref[...] = _silu(xq[...])
        k_ref[...] = (1.0 - lb) * (1.0 - s)
        lf_ref[...] = jnp.log(lb + (1.0 - lb) * s)

    width = h * HEAD_DIM
    col = lambda b0: pl.BlockSpec((tr, width), lambda i: (i, b0 // h))
    own = pl.BlockSpec((tr, width), lambda i: (i, 0))
    shape = jax.ShapeDtypeStruct((t, width), F32)
    return pl.pallas_call(
        body, name="hgrn_prep_fwd", out_shape=(shape, shape, shape), grid=(t // tr,),
        in_specs=[col(q_blk), col(f_blk), pl.BlockSpec((2, width), lambda i: (0, 0))],
        out_specs=(own, own, own), compiler_params=_params(("parallel",)),
    )(proj, proj, lb_logits)


def _hgrn_prep_bwd(proj, lb_logits, dq, dk, dlf, h, q_blk, f_blk):
    t = proj.shape[0]
    tr = _tile(t, 512)

    def body(xq, xf, lb_ref, dq_ref, dk_ref, dlf_ref, dxq, dxf, dlb_ref):
        @pl.when(pl.program_id(0) == 0)
        def _():
            dlb_ref[...] = jnp.zeros_like(dlb_ref)

        lb = _lower_bound(lb_ref)
        s = _sigmoid(xf[...])
        e = dlf_ref[...] / (lb + (1.0 - lb) * s) - dk_ref[...]
        dxq[...] = (dq_ref[...] * _dsilu(xq[...])).astype(BF16)
        dxf[...] = (s * (1.0 - s) * (1.0 - lb) * e).astype(BF16)
        d0 = jnp.sum((1.0 - s) * e, axis=0, keepdims=True) * (lb * (1.0 - lb))
        dlb_ref[pl.ds(0, 1), :] += d0
        dlb_ref[pl.ds(1, 1), :] += -d0

    width = h * HEAD_DIM
    col = lambda b0: pl.BlockSpec((tr, width), lambda i: (i, b0 // h))
    own = pl.BlockSpec((tr, width), lambda i: (i, 0))
    lbs = pl.BlockSpec((2, width), lambda i: (0, 0))
    shape = jax.ShapeDtypeStruct((t, width), BF16)
    return pl.pallas_call(
        body, name="hgrn_prep_bwd",
        out_shape=(shape, shape, jax.ShapeDtypeStruct((2, width), F32)), grid=(t // tr,),
        in_specs=[col(q_blk), col(f_blk), lbs, own, own, own], out_specs=(own, own, lbs),
        compiler_params=_params(("arbitrary",)),
    )(proj, proj, lb_logits, dq, dk, dlf)


GATE_HEADS = 8


def _gate_specs(h, z_blk, g_blk, tr):
    g = min(GATE_HEADS, h)
    n = h // g
    width = g * HEAD_DIM
    o_a = pl.BlockSpec((tr, width), lambda gg, i: (i, jnp.minimum(gg, n - 1)))
    o_b = pl.BlockSpec((tr, width), lambda gg, i: (i, jnp.maximum(gg - n, 0)))
    gate = pl.BlockSpec((tr, width), lambda gg, i: (i, jnp.where(gg < n, z_blk // g + gg, g_blk // g + gg - n)))
    w = pl.BlockSpec((None, 1, HEAD_DIM), lambda gg, i: (gg // n, 0, 0))
    cat = pl.BlockSpec((tr, width), lambda gg, i: (i, gg))
    return (o_a, o_b, gate, w, cat), g, n


def _silu_and_grad(x):
    s = _sigmoid(x)
    return x * s, s * (1.0 + x * (1.0 - s))


def _gate_fwd(o_a, o_b, proj, norm_w, h, z_blk, g_blk):
    t = o_a.shape[0]
    tr = _tile(t, 512)

    (sa, sb, sg, sw, cat), g, n = _gate_specs(h, z_blk, g_blk, tr)

    def body(oa_ref, ob_ref, z_ref, w_ref, y_ref):
        for k in range(g):
            lanes = pl.ds(k * HEAD_DIM, HEAD_DIM)
            o = jnp.where(pl.program_id(0) < n, oa_ref[:, lanes], ob_ref[:, lanes])
            r = lax.rsqrt(jnp.mean(o * o, axis=-1, keepdims=True) + NORM_EPS)
            y_ref[:, lanes] = (o * r * w_ref[...] * _silu(z_ref[:, lanes])).astype(y_ref.dtype)

    return pl.pallas_call(
        body, name="gate_fwd", out_shape=jax.ShapeDtypeStruct((t, 2 * h * HEAD_DIM), BF16),
        grid=(2 * n, t // tr), in_specs=[sa, sb, sg, sw], out_specs=cat,
        compiler_params=_params(("parallel", "parallel")),
    )(o_a, o_b, proj, norm_w)


def _gate_bwd(o_a, o_b, proj, norm_w, dy, h, z_blk, g_blk):
    t = o_a.shape[0]
    tr = _tile(t, 512)

    (sa, sb, sg, sw, cat), g, n = _gate_specs(h, z_blk, g_blk, tr)

    def body(oa_ref, ob_ref, z_ref, w_ref, dy_ref, do_ref, dz_ref, dw_ref):
        gg = pl.program_id(0)

        @pl.when(jnp.logical_and(gg % n == 0, pl.program_id(1) == 0))
        def _():
            dw_ref[...] = jnp.zeros_like(dw_ref)

        w = w_ref[...]
        dw = jnp.zeros_like(w)
        for k in range(g):
            lanes = pl.ds(k * HEAD_DIM, HEAD_DIM)
            o = jnp.where(gg < n, oa_ref[:, lanes], ob_ref[:, lanes])
            dyv = dy_ref[:, lanes]
            r = lax.rsqrt(jnp.mean(o * o, axis=-1, keepdims=True) + NORM_EPS)
            oh = o * r
            act, dact = _silu_and_grad(z_ref[:, lanes])
            dz_ref[:, lanes] = (dyv * oh * w * dact).astype(dz_ref.dtype)
            dn = dyv * act
            doh = dn * w
            do_ref[:, lanes] = r * (doh - oh * jnp.mean(doh * oh, axis=-1, keepdims=True))
            dw = dw + jnp.sum(dn * oh, axis=0, keepdims=True)
        dw_ref[...] += dw

    width = 2 * h * HEAD_DIM
    return pl.pallas_call(
        body, name="gate_bwd",
        out_shape=(jax.ShapeDtypeStruct((t, width), F32), jax.ShapeDtypeStruct((t, width), BF16),
                   jax.ShapeDtypeStruct((2, 1, HEAD_DIM), F32)),
        grid=(2 * n, t // tr), in_specs=[sa, sb, sg, sw, cat], out_specs=(cat, cat, sw),
        compiler_params=_params(("arbitrary", "arbitrary")),
    )(o_a, o_b, proj, norm_w, dy)


def _lane_row(vec):
    return jnp.pad(vec.reshape(1, -1), ((0, 0), (0, HEAD_DIM - vec.shape[-1])))


def _add_epi(acc, res):
    return (acc + res,)


def _split_w_in(w_in, h):
    gw = h * HEAD_DIM
    main = jnp.concatenate([w_in[:, :4 * gw], w_in[:, 4 * gw + 2 * h:]], axis=1)
    pad = jnp.zeros((w_in.shape[0], HEAD_DIM - h), w_in.dtype)
    ab = jnp.concatenate([w_in[:, 4 * gw:4 * gw + h], pad, w_in[:, 4 * gw + h:4 * gw + 2 * h], pad], axis=1)
    return main, ab


def _merge_w_in(main, ab, h):
    gw = h * HEAD_DIM
    return jnp.concatenate([main[:, :4 * gw], ab[:, :h], ab[:, HEAD_DIM:HEAD_DIM + h], main[:, 4 * gw:]], axis=1)


def _local_step(x, target, w_main, w_ab, conv_w, a_log, dt_bias, gdn_norm_w, lb_logits, hgrn_norm_w,
                w_out, norm_mix_w, norm_ffn_w, w_ff1, w_ff2, norm_final_w, reducer=None, n1=None):
    t, d = x.shape
    h = d // (2 * HEAD_DIM)
    gw = h * HEAD_DIM
    k_blk, v_blk, z_blk, qb_blk, fb_blk, ib_blk, gb_blk = (i * h for i in range(1, 8))
    del k_blk, v_blk

    if n1 is None:
        n1 = _rms_fwd(x, norm_mix_w, "rms_mix")
    stacked = not hasattr(w_main, "ndim") or w_main.ndim == 3
    if stacked and not hasattr(w_main, "ndim"):
        early, late_parts, edges, chip = w_main
        proj = _mm_quarters(n1, early, jnp.stack([chip, chip ^ 2, chip ^ 1]), "in_proj_early")
        late, _ = _in_proj_quarters(lax.optimization_barrier((late_parts, proj))[0], edges, h)
        proj = _mm_quarters(n1, late, jnp.stack([chip ^ 3]), "in_proj_late", into=proj)
        w_main = lax.dynamic_update_index_in_dim(
            early, lax.dynamic_index_in_dim(late, chip ^ 3, 0, keepdims=False), chip ^ 3, 0)
    else:
        proj = _mm(n1, w_main, "nn", (F32,), "in_proj", b_stacked=stacked)
    ab = _mm(n1, w_ab, "nn", (F32,), "in_proj_ab")

    q, k, v = _gdn_prep_fwd(proj, conv_w, h)
    a_log_row, dt_row = _lane_row(a_log), _lane_row(dt_bias)
    g_tm, beta_tm = _gdn_gates_fwd(ab, a_log_row, dt_row)
    to_heads = lambda a: jnp.broadcast_to(a[:, :h].T[:, :, None], (h, t, HEAD_DIM))
    g_bc, beta_bc = to_heads(g_tm), to_heads(beta_tm)
    o_a, st_a, inv_a = _gdn_fwd(q, k, v, beta_bc, g_bc)

    qh, kh, lf = _hgrn_prep_fwd(proj, lb_logits, h, qb_blk, fb_blk)
    o_b, st_b = _hgrn_fwd(qh, kh, proj, lf, v_blk=ib_blk)

    gate_w = jnp.stack([gdn_norm_w.reshape(1, HEAD_DIM), hgrn_norm_w.reshape(1, HEAD_DIM)])
    y = _gate_fwd(o_a, o_b, proj, gate_w, h, z_blk, gb_blk)
    if callable(w_out):
        w_out, w_ff1, w_ff2 = w_out(y)
    h1, n2 = _mm_residual_norm(y, w_out, x, norm_ffn_w, "out_proj")
    act, r = _mm(n2, w_ff1, "nn", (F32, BF16), "ff1", b_stacked=True,
                 epi=lambda acc: (acc, jnp.square(jnp.maximum(acc, 0.0))))
    h2 = _mm(r, w_ff2, "nn", (F32,), "ff2", epi=_add_epi, extras=(h1,))
    loss, dh2, dh2_b, d_norm_final = _loss_head(h2, norm_final_w, target)

    da = _mm(dh2_b, w_ff2, "nt", (BF16,), "ff2_dx",
             epi=lambda acc, a: (acc * (2.0 * jnp.maximum(a, 0.0)),), extras=(act,))
    pending = []

    def step(anchor, name=None, full=None):
        if reducer is not None:
            pending.extend(reducer.step(name, full, anchor))

    def after_step(value):
        if not pending:
            return value
        value = lax.optimization_barrier((value, *pending))[0]
        pending.clear()
        return value

    d_ff2 = _mm(r, dh2_b, "tn", (F32,), "ff2_dw")
    step(None, "w_ff2", d_ff2)
    dn2 = _mm(da, w_ff1, "nt", (F32,), "ff1_dx", b_stacked=True)
    d_ff1 = _mm(n2, da, "tn", (F32,), "ff1_dw", out_stacked=True)
    step(d_ff1, "w_ff1", d_ff1)
    dh1, dh1_b, d_norm_ffn = _rms_bwd(after_step(dn2), h1, norm_ffn_w, dh2, "rms_ffn_bwd")
    dy = _mm(dh1_b, w_out, "nt", (F32,), "out_proj_dx")
    d_out = _mm(y, dh1_b, "tn", (F32,), "out_proj_dw")
    step(d_out, "w_out", d_out)

    do, dgate, d_gate_w = _gate_bwd(o_a, o_b, proj, gate_w, after_step(dy), h, z_blk, gb_blk)
    step(do)
    dq, dk, dv, dbeta_bc, dg_bc = _gdn_bwd(q, k, v, beta_bc, g_bc, st_a, inv_a, after_step(do), do_blk=0)
    step(dq)
    dxq, dxk, dxv, dcq, dck, dcv = _gdn_prep_bwd(proj, conv_w, after_step(dq), dk, dv, h)
    step(dxq)
    from_heads = lambda a: jnp.pad(a[:, :, 0, :].reshape(h, t).T, ((0, 0), (0, HEAD_DIM - h)))
    dab, d_a_log, d_dt_bias = _gdn_gates_bwd(ab, a_log_row, dt_row, from_heads(dg_bc), from_heads(dbeta_bc))
    dqh, dkh, dvh, dlf = _hgrn_bwd(after_step(qh), kh, proj, lf, st_b, do, v_blk=ib_blk, do_blk=h)
    step(dqh)
    dxqb, dxfb, d_lb = _hgrn_prep_bwd(proj, lb_logits, dqh, dkh, dlf, h, qb_blk, fb_blk)

    dproj = jnp.concatenate([after_step(dxq), dxk, dxv, dgate[:, :gw], dxqb, dxfb, dvh.astype(BF16), dgate[:, gw:]],
                            axis=1)
    d_main = _mm(n1, dproj, "tn", (F32,), "in_proj_dw", out_stacked=True)
    d_ab = _mm(n1, dab, "tn", (F32,), "in_proj_ab_dw")
    step(d_main, "w_in", d_main)
    dn1_ab = _mm(after_step(dab), w_ab, "nt", (F32,), "in_proj_ab_dx")
    step(dn1_ab)
    dn1 = _mm(after_step(dproj), w_main, "nt", (F32,), "in_proj_dx", epi=_add_epi, extras=(dn1_ab,),
              b_stacked=stacked)
    step(dn1)
    dx, _, d_norm_mix = _rms_bwd(after_step(dn1), x, norm_mix_w, dh1, "rms_mix_bwd")
    step(dx)

    grads = dict(
        w_main=d_main, w_ab=d_ab, conv_w=jnp.concatenate([dcq, dck, dcv], axis=1),
        gdn_a_log=d_a_log[:, :h], gdn_dt_bias=d_dt_bias[:, :h], gdn_norm_w=d_gate_w[0],
        hgrn_lb_logits=d_lb, hgrn_norm_w=d_gate_w[1], w_out=d_out, norm_mix_w=d_norm_mix,
        norm_ffn_w=d_norm_ffn, w_ff1=d_ff1, w_ff2=d_ff2, norm_final_w=d_norm_final)
    return loss, dx, grads


N_CHIPS = 4
ANY = pl.BlockSpec(memory_space=pl.ANY)


def _place():
    x, y, c = lax.axis_index("x"), lax.axis_index("y"), lax.axis_index("c")
    chips = [(1 - x, y), (x, 1 - y), (1 - x, 1 - y)]
    return x, y, c, chips


def _remote(src, dst, send_sems, recv_sems, k, to):
    return pltpu.make_async_remote_copy(src_ref=src, dst_ref=dst, send_sem=send_sems.at[k],
                                        recv_sem=recv_sems.at[k], device_id=to, device_id_type=MESH)


def _to_sibling(x, y, c, chips):
    return [(x, y, 1 - c)]


def _to_same_core_of_chips(x, y, c, chips):
    return [(*chip, c) for chip in chips]


def _to_all_gather_peers(x, y, c, chips):
    return _to_sibling(x, y, c, chips) + _to_same_core_of_chips(x, y, c, chips)


SIBLING_EXCHANGE = (1, _to_sibling)
CHIP_EXCHANGE = (2, _to_same_core_of_chips)
GATHER_EXCHANGE = (3, _to_all_gather_peers)
DIAGONAL_EXCHANGE = (4, lambda x, y, c, chips: [(x, y, 1 - c), (*chips[2], c)])


def _launch(body, name, out_shapes, arrays, sem_counts, sequencer=None, after=()):
    n, n_after = len(arrays), len(after)
    sems = [pltpu.SemaphoreType.DMA((k,)) for k in sem_counts]
    strip = lambda refs: refs[:n] + refs[n + n_after:]
    if sequencer is None:
        return pl.pallas_call(
            lambda *refs: body(*strip(refs)), name=name, out_shape=tuple(out_shapes),
            in_specs=[ANY] * (n + n_after), out_specs=tuple(ANY for _ in out_shapes), scratch_shapes=sems,
        )(*arrays, *after)
    collective_id, peers = sequencer

    def sequencer_body(*refs):
        x, y, c, chips = _place()
        barrier = pltpu.get_barrier_semaphore()
        targets = peers(x, y, c, chips)
        for target in targets:
            pl.semaphore_signal(barrier, inc=1, device_id=target, device_id_type=MESH)
        pl.semaphore_wait(barrier, len(targets))
        body(*strip(refs))

    return pl.kernel(
        sequencer_body, name=name, out_type=tuple(out_shapes),
        mesh=plsc.ScalarSubcoreMesh(axis_name="sequencer", num_cores=1), scratch_types=tuple(sems),
        compiler_params=pltpu.CompilerParams(collective_id=collective_id),
    )(*arrays, *after)


def _gather_weights(big, small, name, sequencer=None, after=(), relations=(0, 1, 2)):
    nb, ns = len(big), len(small)
    n_sem = 6 * nb + 3 * ns

    def body(*refs):
        ins, outs = refs[:nb + ns], refs[nb + ns:2 * (nb + ns)]
        send_sems, recv_sems = refs[2 * (nb + ns):]
        x, y, c, chips = _place()
        me, sibling = 2 * x + y, (x, y, 1 - c)

        def half(a, chip, hc):
            rh = big[a].shape[0] // 2
            return outs[a].at[2 * chip[0] + chip[1], pl.ds(hc * rh, rh), :]

        first, passed = [], []
        for a in range(nb):
            rh = big[a].shape[0] // 2
            for j, chip in [(j, chips[j]) for j in relations]:
                first.append(_remote(ins[a].at[pl.ds(c * rh, rh), :], half(a, (x, y), c),
                                     send_sems, recv_sems, 6 * a + j, (*chip, c)))
        for s in range(ns):
            for j, chip in enumerate(chips):
                first.append(_remote(ins[nb + s], outs[nb + s].at[me], send_sems, recv_sems,
                                     6 * nb + 3 * s + j, (*chip, c)))
        for cp in first:
            cp.start()
        for a in range(nb):
            for j, chip in [(j, chips[j]) for j in relations]:
                _remote(half(a, chip, c), half(a, chip, c), send_sems, recv_sems, 6 * a + j, (*chip, c)).wait_recv()
                fwd = _remote(half(a, chip, c), half(a, chip, c), send_sems, recv_sems, 6 * a + 3 + j, sibling)
                fwd.start()
                passed.append(fwd)
        for s in range(ns):
            for j, chip in enumerate(chips):
                dst = outs[nb + s].at[2 * chip[0] + chip[1]]
                _remote(dst, dst, send_sems, recv_sems, 6 * nb + 3 * s + j, (*chip, c)).wait_recv()
        for a in range(nb):
            for j, chip in [(j, chips[j]) for j in relations]:
                _remote(half(a, chip, 1 - c), half(a, chip, 1 - c), send_sems, recv_sems,
                        6 * a + 3 + j, sibling).wait_recv()
        for cp in first + passed:
            cp.wait_send()

    arrays = list(big) + list(small)
    out_shapes = [jax.ShapeDtypeStruct((N_CHIPS,) + a.shape, a.dtype) for a in arrays]
    return _launch(body, name, out_shapes, arrays, (n_sem, n_sem), sequencer, after)


def _swap_halves(parts, name, sequencer=None):
    n = len(parts)

    def body(*refs):
        ins, outs = refs[:n], refs[n:2 * n]
        send_sems, recv_sems = refs[2 * n:]
        x, y, c, _ = _place()
        copies = [_remote(ins[a].at[s, 1 - c], outs[a].at[s], send_sems, recv_sems, N_CHIPS * a + s, (x, y, 1 - c))
                  for a in range(n) for s in range(N_CHIPS)]
        for cp in copies:
            cp.start()
        for cp in copies:
            cp.wait()

    out_shapes = [jax.ShapeDtypeStruct((N_CHIPS,) + p.shape[2:], p.dtype) for p in parts]
    return _launch(body, name, out_shapes, parts, (N_CHIPS * n, N_CHIPS * n), sequencer)


def _scatter_to_owners(parts, name, sequencer=None):
    n = len(parts)

    def body(*refs):
        ins, outs = refs[:n], refs[n:2 * n]
        send_sems, recv_sems = refs[2 * n:]
        x, y, c, chips = _place()
        copies = [_remote(ins[a].at[2 * chip[0] + chip[1]], outs[a].at[j], send_sems, recv_sems,
                          3 * a + j, (*chip, c))
                  for a in range(n) for j, chip in enumerate(chips)]
        for cp in copies:
            cp.start()
        for cp in copies:
            cp.wait()

    out_shapes = [jax.ShapeDtypeStruct((3,) + p.shape[1:], p.dtype) for p in parts]
    return _launch(body, name, out_shapes, parts, (3 * n, 3 * n), sequencer)


def _send_to_sibling(halves, name, sequencer=None):
    n = len(halves)

    def body(*refs):
        ins, outs = refs[:n], refs[n:2 * n]
        send_sems, recv_sems = refs[2 * n:]
        x, y, c, _ = _place()
        copies = [_remote(ins[a], outs[a], send_sems, recv_sems, a, (x, y, 1 - c)) for a in range(n)]
        for cp in copies:
            cp.start()
        for cp in copies:
            cp.wait()

    out_shapes = [jax.ShapeDtypeStruct(p.shape, p.dtype) for p in halves]
    return _launch(body, name, out_shapes, halves, (n, n), sequencer)


N_DEV = 8


def _all_reduce_small(vec):
    def body(v_ref, gathered, total, send_sems, recv_sems):
        x, y, c, _ = _place()
        me = 4 * x + 2 * y + c
        gathered[me] = v_ref[...]
        copies = []
        for k in range(1, N_DEV):
            px = 1 - x if k & 4 else x
            py = 1 - y if k & 2 else y
            pc = 1 - c if k & 1 else c
            copies.append(_remote(v_ref, gathered.at[me], send_sems, recv_sems, k - 1, (px, py, pc)))
        for cp in copies:
            cp.start()
        for k, cp in enumerate(copies):
            cp.wait_send()
        for k in range(1, N_DEV):
            px = 1 - x if k & 4 else x
            py = 1 - y if k & 2 else y
            pc = 1 - c if k & 1 else c
            src = gathered.at[4 * px + 2 * py + pc]
            _remote(src, src, send_sems, recv_sems, k - 1, (px, py, pc)).wait_recv()
        acc = gathered[0]
        for dev in range(1, N_DEV):
            acc = acc + gathered[dev]
        total[...] = acc

    vm = pl.BlockSpec(memory_space=pltpu.VMEM)
    return pl.pallas_call(
        body, name="all_reduce_small",
        out_shape=(jax.ShapeDtypeStruct((N_DEV,) + vec.shape, F32), jax.ShapeDtypeStruct(vec.shape, F32)),
        in_specs=[vm], out_specs=(vm, vm),
        scratch_shapes=[pltpu.SemaphoreType.DMA((N_DEV - 1,)), pltpu.SemaphoreType.DMA((N_DEV - 1,))],
    )(vec)[1]


def _chip_sum(part, recv, c, chip):
    _, _, rh, cols = part.shape
    tr = _tile(rh, 256)

    def body(c_ref, chip_ref, p_ref, r_ref, own_ref, sb_ref):
        s = p_ref[...] + r_ref[...]
        sb_ref[...] = s.astype(BF16)

        @pl.when(pl.program_id(1) == chip_ref[0])
        def _():
            own_ref[...] = s

    blk = pl.BlockSpec((None, tr, cols), lambda i, s, c_ref, chip_ref: (s, i, 0))
    return pl.pallas_call(
        body, name="grad_chip_sum",
        out_shape=(jax.ShapeDtypeStruct(recv.shape[1:], F32), jax.ShapeDtypeStruct(recv.shape, BF16)),
        grid_spec=pltpu.PrefetchScalarGridSpec(
            num_scalar_prefetch=2, grid=(rh // tr, N_CHIPS),
            in_specs=[pl.BlockSpec((None, None, tr, cols), lambda i, s, c_ref, chip_ref: (s, c_ref[0], i, 0)), blk],
            out_specs=(pl.BlockSpec((tr, cols), lambda i, s, c_ref, chip_ref: (i, 0)), blk)),
        compiler_params=_params(("parallel", "arbitrary")),
    )(c, chip, part, recv)


def _owner_sum(own, recv):
    rh, cols = own.shape
    tr = _tile(rh, 256)

    def body(o_ref, r0, r1, r2, g_ref):
        g_ref[...] = ((o_ref[...] + r0[...].astype(F32)) + r1[...].astype(F32)) + r2[...].astype(F32)

    slot = lambda j: pl.BlockSpec((None, tr, cols), lambda i: (j, i, 0))
    row = pl.BlockSpec((tr, cols), lambda i: (i, 0))
    return pl.pallas_call(
        body, name="grad_owner_sum", out_shape=jax.ShapeDtypeStruct((rh, cols), F32), grid=(rh // tr,),
        in_specs=[row, slot(0), slot(1), slot(2)], out_specs=row,
        compiler_params=_params(("parallel",)),
    )(own, recv, recv, recv)


def _adamw_math(w, g, m, v):
    c1 = 1.0 / (1.0 - ADAM_B1 ** ADAM_STEP)
    c2 = 1.0 / (1.0 - ADAM_B2 ** ADAM_STEP)
    nm = ADAM_B1 * m + (1.0 - ADAM_B1) * g
    nv = ADAM_B2 * v + (1.0 - ADAM_B2) * (g * g)
    return -ADAM_LR * ((nm * c1) / (jnp.sqrt(nv * c2) + ADAM_EPS) + ADAM_WD * w), nm, nv


def _adamw_unit_rows(w, g, m, v, name):
    rows, _, cols = w.shape
    tr = max(d for d in range(1, 33) if rows % d == 0)

    def body(w_ref, g_ref, m_ref, v_ref, d_ref, nm_ref, nv_ref):
        d_ref[...], nm_ref[...], nv_ref[...] = _adamw_math(w_ref[...], g_ref[...], m_ref[...], v_ref[...])

    blk = pl.BlockSpec((tr, 1, cols), lambda i: (i, 0, 0))
    shape = jax.ShapeDtypeStruct(w.shape, F32)
    return pl.pallas_call(
        body, name=name, out_shape=(shape, shape, shape), grid=(rows // tr,),
        in_specs=[blk, blk, blk, blk], out_specs=(blk, blk, blk),
        compiler_params=_params(("parallel",)),
    )(w, g, m, v)


def _divisor_tile(n, want):
    return max(d for d in range(ROW_TILE, want + 1, ROW_TILE) if n % d == 0)


def _adamw(w, g, m, v, name):
    if w.ndim == 3:
        return _adamw_unit_rows(w, g, m, v, name)
    rows, cols = w.shape
    tr = _divisor_tile(rows, 2048) if rows % 8 == 0 else rows
    c1 = 1.0 / (1.0 - ADAM_B1 ** ADAM_STEP)
    c2 = 1.0 / (1.0 - ADAM_B2 ** ADAM_STEP)

    def body(w_ref, g_ref, m_ref, v_ref, d_ref, nm_ref, nv_ref):
        gv = g_ref[...]
        nm = ADAM_B1 * m_ref[...] + (1.0 - ADAM_B1) * gv
        nv = ADAM_B2 * v_ref[...] + (1.0 - ADAM_B2) * (gv * gv)
        d_ref[...] = -ADAM_LR * ((nm * c1) / (jnp.sqrt(nv * c2) + ADAM_EPS) + ADAM_WD * w_ref[...])
        nm_ref[...] = nm
        nv_ref[...] = nv

    blk = pl.BlockSpec((tr, cols), lambda i: (i, 0))
    shape = jax.ShapeDtypeStruct((rows, cols), F32)
    return pl.pallas_call(
        body, name=name, out_shape=(shape, shape, shape), grid=(rows // tr,),
        in_specs=[blk, blk, blk, blk], out_specs=(blk, blk, blk),
        compiler_params=_params(("parallel",)),
    )(w, g, m, v)


def _adamw_halves(w, g_own, g_sib, m, v, c, name):
    _, rows, cols = w.shape
    rh = rows // 2
    tr = _tile(rh, 256)
    per = rh // tr
    c1 = 1.0 / (1.0 - ADAM_B1 ** ADAM_STEP)
    c2 = 1.0 / (1.0 - ADAM_B2 ** ADAM_STEP)

    def body(c_ref, w_ref, go_ref, gs_ref, m_ref, v_ref, g_ref, d_ref, nm_ref, nv_ref):
        own = pl.program_id(0) // per == c_ref[0]
        gv = jnp.where(own, go_ref[...], gs_ref[...])
        nm = ADAM_B1 * m_ref[...] + (1.0 - ADAM_B1) * gv
        nv = ADAM_B2 * v_ref[...] + (1.0 - ADAM_B2) * (gv * gv)
        g_ref[...] = gv
        d_ref[...] = -ADAM_LR * ((nm * c1) / (jnp.sqrt(nv * c2) + ADAM_EPS) + ADAM_WD * w_ref[...])
        nm_ref[...] = nm
        nv_ref[...] = nv

    blk = pl.BlockSpec((None, tr, cols), lambda i, c_ref: (0, i, 0))
    half = pl.BlockSpec((tr, cols), lambda i, c_ref: (i % per, 0))
    shape = jax.ShapeDtypeStruct((1, rows, cols), F32)
    return pl.pallas_call(
        body, name=name, out_shape=(shape, shape, shape, shape),
        grid_spec=pltpu.PrefetchScalarGridSpec(
            num_scalar_prefetch=1, grid=(rows // tr,),
            in_specs=[blk, half, half, blk, blk], out_specs=(blk, blk, blk, blk)),
        compiler_params=_params(("parallel",)),
    )(c, w, g_own, g_sib, m, v)


def _by_shard(name, full):
    if name in ("w_in", "w_ff1"):
        st = full
    else:
        st = full.reshape(N_CHIPS, -1, full.shape[1])
    return st.reshape(N_CHIPS, 2, st.shape[1] // 2, st.shape[2])


class _GradReducer:
    def __init__(self, w, m, v, my_c, my_chip):
        self.w, self.m, self.v, self.my_c, self.my_chip = w, m, v, my_c, my_chip
        self.in_flight = []
        self.computed = []
        self.anchor = None
        self.done = {}

    def step(self, name=None, full=None, anchor=None):
        stages, self.in_flight, self.computed, self.anchor = self.in_flight, [], [], anchor
        for stage in [s for s in stages if not getattr(s, "long", False)]:
            self._advance(stage)
        if name is not None:
            self.in_flight.append(self._swap(name, _by_shard(name, full)))
        for stage in [s for s in stages if getattr(s, "long", False)]:
            self._advance(stage)
        return self.computed

    def _held(self, value):
        if self.anchor is None:
            return value
        return lax.optimization_barrier((value, self.anchor))[0]

    def _advance(self, stage):
        nxt = stage()
        if nxt is not None:
            self.in_flight.append(nxt)

    def finish(self):
        while self.in_flight:
            self.step()
        return self.done

    def _swap(self, name, part):
        got, = _swap_halves([part], "grad_swap_" + name, SIBLING_EXCHANGE)

        def scatter():
            total, total_bf16 = _chip_sum(part, self._held(got), self.my_c, self.my_chip.reshape(1))
            self.computed.append(total_bf16)
            recv, = _scatter_to_owners([total_bf16], "grad_scatter_" + name, CHIP_EXCHANGE)

            def send():
                half = _owner_sum(total, self._held(recv))
                self.computed.append(half)
                sib, = _send_to_sibling([half], "grad_send_" + name, SIBLING_EXCHANGE)

                def update():
                    self.done[name] = _adamw_halves(self.w[name], half, self._held(sib), self.m[name], self.v[name],
                                                    self.my_c, "adamw_" + name)
                    self.computed.append(self.done[name][0])

                if name == "w_in":
                    self.in_proj_halves = (half, sib)
                    return None
                return update
            return lambda: send
        scatter.long = True
        return scatter


def _adamw_minor_rows(w, g, m, v, name):
    _, rows, cols = w.shape
    turned = lambda a: jnp.transpose(a, (2, 0, 1))
    back = lambda a: jnp.transpose(a, (1, 2, 0))
    g = g.T.reshape(cols, 1, rows)
    delta, new_m, new_v = _adamw(turned(w), g, turned(m), turned(v), name)
    return back(g), back(delta), back(new_m), back(new_v)


def _in_proj_quarter_part(shard, chip, h):
    e = h // 2
    qw = shard.shape[1] - e
    zeros = jnp.zeros((shard.shape[0], h), shard.dtype)
    padded = jnp.concatenate([zeros, shard, zeros], axis=1)
    at_quarter = jnp.where(chip == 0, h, jnp.where(chip == 1, h - e, jnp.where(chip == 2, 2 * h, h + e)))
    at_edge = jnp.where(chip == 0, h + qw, jnp.where(chip == 1, h + qw - e, jnp.where(chip == 2, h, h - e)))
    return (lax.dynamic_slice_in_dim(padded, at_quarter, qw, axis=1).astype(BF16),
            lax.dynamic_slice_in_dim(padded, at_edge, h, axis=1))


def _in_proj_quarters(parts, edges, h):
    e = h // 2
    x1, a, b, x2 = edges[0][:, :e], edges[1], edges[2], edges[3][:, e:]
    parts = parts.at[1, :, :e].set(x1.astype(BF16))
    parts = parts.at[2, :, parts.shape[2] - e:].set(x2.astype(BF16))
    pad = jnp.zeros((a.shape[0], HEAD_DIM - h), a.dtype)
    return parts, jnp.concatenate([a, pad, b, pad], axis=1).astype(BF16)


def _in_proj_edge_columns(d_main, d_ab, h):
    e = h // 2
    return jnp.concatenate([d_main[1][:, :e], d_main[2][:, -e:], d_ab[:, :h], d_ab[:, HEAD_DIM:HEAD_DIM + h]], axis=1)


def _in_proj_shard_grad(q_own, q_sib, edges, c, chip, h):
    e = h // 2
    lower, upper = jnp.where(c[0] == 0, q_own, q_sib), jnp.where(c[0] == 0, q_sib, q_own)
    quarter = jnp.concatenate([lower, upper], axis=0)
    x1, x2, a, b = edges[:, :e], edges[:, e:2 * e], edges[:, 2 * e:2 * e + h], edges[:, 2 * e + h:]
    zeros = jnp.zeros_like(x1)
    left = jnp.where(chip == 2, b, jnp.concatenate([zeros, x2], axis=1))
    right = jnp.where(chip == 1, a, jnp.concatenate([x1, zeros], axis=1))
    start = jnp.where(chip == 0, h, jnp.where(chip == 1, h + e, jnp.where(chip == 2, 0, h - e)))
    padded = jnp.concatenate([left, quarter, right], axis=1)
    return lax.dynamic_slice_in_dim(padded, start, quarter.shape[1] + e, axis=1)


SMALL = ("gdn_a_log", "gdn_dt_bias", "gdn_norm_w", "hgrn_lb_logits", "hgrn_norm_w",
         "norm_mix_w", "norm_ffn_w", "norm_final_w")
BIG = ("w_in", "w_out", "w_ff1", "w_ff2")
ORDER = ("w_in", "conv_w", "gdn_a_log", "gdn_dt_bias", "gdn_norm_w", "hgrn_lb_logits", "hgrn_norm_w",
         "w_out", "norm_mix_w", "norm_ffn_w", "w_ff1", "w_ff2", "norm_final_w")


def _pack(pieces):
    flat = jnp.concatenate([p.reshape(-1).astype(F32) for p in pieces])
    rows = -(-flat.shape[0] // (8 * HEAD_DIM)) * 8
    return jnp.pad(flat, (0, rows * HEAD_DIM - flat.shape[0])).reshape(rows, HEAD_DIM)


def _unpack(packed, shapes):
    flat, out, at = packed.reshape(-1), [], 0
    for s in shapes:
        n = 1
        for dim in s:
            n *= dim
        out.append(flat[at:at + n].reshape(s))
        at += n
    return out


def kernel(x, w_in, conv_w, gdn_a_log, gdn_dt_bias, gdn_norm_w, hgrn_lb_logits, hgrn_norm_w, w_out, norm_mix_w, norm_ffn_w, w_ff1, w_ff2, norm_final_w, loss_target, m_w_in, m_conv_w, m_gdn_a_log, m_gdn_dt_bias, m_gdn_norm_w, m_hgrn_lb_logits, m_hgrn_norm_w, m_w_out, m_norm_mix_w, m_norm_ffn_w, m_w_ff1, m_w_ff2, m_norm_final_w, v_w_in, v_conv_w, v_gdn_a_log, v_gdn_dt_bias, v_gdn_norm_w, v_hgrn_lb_logits, v_hgrn_norm_w, v_w_out, v_norm_mix_w, v_norm_ffn_w, v_w_ff1, v_w_ff2, v_norm_final_w):
    w = dict(w_in=w_in, conv_w=conv_w, gdn_a_log=gdn_a_log, gdn_dt_bias=gdn_dt_bias, gdn_norm_w=gdn_norm_w,
             hgrn_lb_logits=hgrn_lb_logits, hgrn_norm_w=hgrn_norm_w, w_out=w_out, norm_mix_w=norm_mix_w,
             norm_ffn_w=norm_ffn_w, w_ff1=w_ff1, w_ff2=w_ff2, norm_final_w=norm_final_w)
    m = dict(w_in=m_w_in, conv_w=m_conv_w, gdn_a_log=m_gdn_a_log, gdn_dt_bias=m_gdn_dt_bias,
             gdn_norm_w=m_gdn_norm_w, hgrn_lb_logits=m_hgrn_lb_logits, hgrn_norm_w=m_hgrn_norm_w,
             w_out=m_w_out, norm_mix_w=m_norm_mix_w, norm_ffn_w=m_norm_ffn_w, w_ff1=m_w_ff1, w_ff2=m_w_ff2,
             norm_final_w=m_norm_final_w)
    v = dict(w_in=v_w_in, conv_w=v_conv_w, gdn_a_log=v_gdn_a_log, gdn_dt_bias=v_gdn_dt_bias,
             gdn_norm_w=v_gdn_norm_w, hgrn_lb_logits=v_hgrn_lb_logits, hgrn_norm_w=v_hgrn_norm_w,
             w_out=v_w_out, norm_mix_w=v_norm_mix_w, norm_ffn_w=v_norm_ffn_w, w_ff1=v_w_ff1, w_ff2=v_w_ff2,
             norm_final_w=v_norm_final_w)
    d = x.shape[-1]
    h = d // (2 * HEAD_DIM)
    my_c = lax.axis_index("c").astype(jnp.int32).reshape(1)
    my_chip = (2 * lax.axis_index("x") + lax.axis_index("y")).astype(jnp.int32)

    shards = [w[n][0].astype(BF16) for n in BIG]
    conv_shard = jnp.pad(conv_w[0], ((0, 8 - CONV_W), (0, 0)))
    quarter_part, edge_part = _in_proj_quarter_part(w_in[0], my_chip, h)
    first = _gather_weights([quarter_part], [conv_shard, edge_part], "gather_in_proj", GATHER_EXCHANGE,
                            relations=(0, 1))
    diagonal, = _gather_weights([quarter_part], [], "gather_in_proj_diagonal", DIAGONAL_EXCHANGE, relations=(2,))
    n1 = _rms_fwd(x[0], norm_mix_w[0], "rms_mix")
    gathered_in, n1, *shards[1:] = lax.optimization_barrier((first[0], n1, *shards[1:]))
    own_slot = lambda st, own: lax.dynamic_update_index_in_dim(st, own, my_chip, 0)
    f_conv, f_edges = own_slot(first[1], conv_shard), own_slot(first[2], edge_part)
    early, w_ab = _in_proj_quarters(own_slot(gathered_in, quarter_part), f_edges, h)
    w_main = (early, diagonal, f_edges, my_chip)
    cols = lambda st: st.transpose(1, 0, 2).reshape(st.shape[1], -1)
    conv_full = cols(f_conv[:, :CONV_W])
    rest = _gather_weights(shards[1:], [], "gather_rest", GATHER_EXCHANGE, after=[w_ab])

    def late_weights(anchor):
        held = lax.optimization_barrier((*rest, anchor))[:len(rest)]
        f_out, f_ff1, f_ff2 = (own_slot(st, own) for st, own in zip(held, shards[1:]))
        return f_out.reshape(-1, d), f_ff1, f_ff2.reshape(-1, d)

    reducer = _GradReducer(w, m, v, my_c, my_chip)
    loss, dx, g = _local_step(
        x[0], loss_target[0], w_main, w_ab, conv_full, gdn_a_log[0], gdn_dt_bias[0], gdn_norm_w[0],
        hgrn_lb_logits, hgrn_norm_w[0], late_weights, norm_mix_w[0], norm_ffn_w[0],
        None, None, norm_final_w, reducer, n1)

    grads, delta, new_m, new_v = {}, {}, {}, {}
    for n, out in reducer.finish().items():
        grads[n], delta[n], new_m[n], new_v[n] = out

    edges = _in_proj_edge_columns(g["w_main"], g["w_ab"], h)
    small_shapes = [w[n].shape for n in SMALL] + [conv_full.shape, (1,), edges.shape]
    total = _all_reduce_small(_pack([g[n] for n in SMALL] + [g["conv_w"], loss[0, :1], edges]))
    *small_grads, conv_grad, loss_sum, edges = _unpack(total, small_shapes)
    g_in = _in_proj_shard_grad(*reducer.in_proj_halves, edges, my_c, my_chip, h)
    grads["w_in"], delta["w_in"], new_m["w_in"], new_v["w_in"] = _adamw_minor_rows(
        w["w_in"], g_in, m["w_in"], v["w_in"], "adamw_w_in")
    for n, sg in zip(SMALL, small_grads):
        grads[n] = sg
    shard_cols = conv_w.shape[-1]
    grads["conv_w"] = lax.dynamic_slice_in_dim(conv_grad, my_chip * shard_cols, shard_cols, axis=1)[None]

    packed_names = SMALL + ("conv_w",)
    packed = [_pack([t[n] for n in packed_names]) for t in (w, grads, m, v)]
    outs = _adamw(*packed, "adamw_small")
    shapes = [w[n].shape for n in packed_names]
    for res, o in zip((delta, new_m, new_v), outs):
        for n, a in zip(packed_names, _unpack(o, shapes)):
            res[n] = a

    return (loss_sum.reshape(()), dx[None], *[grads[n] for n in ORDER], *[delta[n] for n in ORDER],
            *[new_m[n] for n in ORDER], *[new_v[n] for n in ORDER])
```
